```python
import math
import jax, jax.numpy as jnp
from jax import lax
import numpy as np

D_MODEL = 1024
BATCH = 8
SEQ = 2048
DEPTH = 1

MEM_LEN = 256
SSM_HEAD_DIM = 64
SSM_HEADS = D_MODEL // SSM_HEAD_DIM
SSM_D_INNER = SSM_HEADS * SSM_HEAD_DIM
SSM_GROUPS = 2
SSM_STATE = 128
CONV_WIDTH = 4
CHUNK = 128
CONV_DIM = SSM_D_INNER + 2 * SSM_GROUPS * SSM_STATE
ATTN_HEAD_DIM = 64
ATTN_HEADS = D_MODEL // ATTN_HEAD_DIM
ATTN_WIDTH = ATTN_HEADS * ATTN_HEAD_DIM
Q_BLOCK = 128
MIX_WIDTH = SSM_D_INNER + ATTN_WIDTH
IN_COLS = 2 * SSM_D_INNER + 2 * SSM_GROUPS * SSM_STATE + SSM_HEADS + 3 * ATTN_WIDTH + ATTN_HEADS
XATTN_HEADS = 4
XATTN_HEAD_DIM = D_MODEL // XATTN_HEADS
D_FF = 4 * D_MODEL
EPS = 1e-5

kernel_name = "hymba_ssd_fox_memxattn_layer"


def rms_norm(u, g):
    uf = u.astype(jnp.float32)
    y = uf * lax.rsqrt(jnp.mean(uf * uf, axis=-1, keepdims=True) + EPS)
    return (y * g.astype(jnp.float32)).astype(u.dtype)


def segsum(a):
    T = a.shape[-1]
    x = jnp.broadcast_to(a[..., :, None], a.shape + (T,))
    x = jnp.where(jnp.tril(jnp.ones((T, T), dtype=bool), -1), x, 0.0)
    x = jnp.cumsum(x, axis=-2)
    return jnp.where(jnp.tril(jnp.ones((T, T), dtype=bool)), x, -jnp.inf)


def causal_depthwise_conv(u, w, b):
    c = u.shape[-1]
    out = lax.conv_general_dilated(
        u, w[:, None, :].astype(u.dtype), window_strides=(1,),
        padding=[(CONV_WIDTH - 1, 0)], dimension_numbers=("NWC", "WIO", "NWC"),
        feature_group_count=c)
    return out + b.astype(u.dtype)


def ssd_chunked(xh, dt, A, Bm, Cm):
    b, S, g, r, p = xh.shape
    n = Bm.shape[-1]
    c = S // CHUNK
    X = (xh * dt[..., None]).reshape(b, c, CHUNK, g, r, p)
    dA = (dt * A).reshape(b, c, CHUNK, g, r).transpose(0, 3, 4, 1, 2)
    Bc = Bm.reshape(b, c, CHUNK, g, n)
    Cc = Cm.reshape(b, c, CHUNK, g, n)
    A_cs = jnp.cumsum(dA, axis=-1)
    Lmat = jnp.exp(segsum(dA))
    CB = jnp.einsum("bclgn,bcsgn->bcgls", Cc, Bc)
    y_diag = jnp.einsum("bcgls,bgrcls,bcsgrp->bclgrp", CB, Lmat, X)
    decay_states = jnp.exp(A_cs[..., -1:] - A_cs)
    states = jnp.einsum("bclgn,bgrcl,bclgrp->bcgrpn", Bc, decay_states, X)
    states = jnp.concatenate([jnp.zeros_like(states[:, :1]), states], axis=1)
    A_last = jnp.pad(A_cs[..., -1], ((0, 0), (0, 0), (0, 0), (1, 0)))
    chunk_decay = jnp.exp(segsum(A_last))
    new_states = jnp.einsum("bgrzc,bcgrpn->bzgrpn", chunk_decay, states)
    states_in = new_states[:, :-1]
    y_off = jnp.einsum("bclgn,bcgrpn,bgrcl->bclgrp", Cc, states_in, jnp.exp(A_cs))
    return (y_diag + y_off).reshape(b, S, g, r, p)


def forgetting_attention(q, k, v, log_f):
    b, S, h, d = q.shape
    cum = jnp.cumsum(log_f, axis=1).transpose(0, 2, 1)
    scale = d ** -0.5
    outs = []
    for i in range(S // Q_BLOCK):
        qs, qe = i * Q_BLOCK, (i + 1) * Q_BLOCK
        s = jnp.einsum("bqhd,bkhd->bhqk", q[:, qs:qe], k[:, :qe]) * scale
        s = s + cum[:, :, qs:qe, None] - cum[:, :, None, :qe]
        mask = jnp.arange(qs, qe)[:, None] >= jnp.arange(qe)[None, :]
        s = jnp.where(mask, s, -jnp.inf)
        pr = jax.nn.softmax(s, axis=-1)
        outs.append(jnp.einsum("bhqk,bkhd->bqhd", pr, v[:, :qe]))
    return jnp.concatenate(outs, axis=1)


def parallel_mixer(h, w_in, conv_w, conv_b, dt_bias, a_log, d_skip, ssm_norm_w,
                   g_q, g_k, f_bias, w_out):
    b, S, _ = h.shape
    proj = h @ w_in
    sizes = [SSM_D_INNER, CONV_DIM, SSM_HEADS, ATTN_WIDTH, ATTN_WIDTH, ATTN_WIDTH]
    idx = list(np.cumsum(sizes))
    z, xbc, dt_raw, q, k, v, f_raw = jnp.split(proj, idx, axis=-1)
    xbc = jax.nn.silu(causal_depthwise_conv(xbc, conv_w, conv_b)).astype(jnp.float32)
    xs, Bm, Cm = jnp.split(xbc, [SSM_D_INNER, SSM_D_INNER + SSM_GROUPS * SSM_STATE], axis=-1)
    r = SSM_HEADS // SSM_GROUPS
    xs = xs.reshape(b, S, SSM_GROUPS, r, SSM_HEAD_DIM)
    Bm = Bm.reshape(b, S, SSM_GROUPS, SSM_STATE)
    Cm = Cm.reshape(b, S, SSM_GROUPS, SSM_STATE)
    dt = jax.nn.softplus(dt_raw.astype(jnp.float32) + dt_bias.astype(jnp.float32))
    dt = dt.reshape(b, S, SSM_GROUPS, r)
    A = -jnp.exp(a_log.astype(jnp.float32)).reshape(SSM_GROUPS, r)
    y = ssd_chunked(xs, dt, A, Bm, Cm)
    y = y + d_skip.astype(jnp.float32).reshape(SSM_GROUPS, r)[..., None] * xs
    y = y.reshape(b, S, SSM_D_INNER) * jax.nn.silu(z.astype(jnp.float32))
    y = y.reshape(b, S, SSM_GROUPS, SSM_D_INNER // SSM_GROUPS)
    y = y * lax.rsqrt(jnp.mean(y * y, axis=-1, keepdims=True) + EPS)
    y = y.reshape(b, S, SSM_D_INNER) * ssm_norm_w.astype(jnp.float32)
    q = rms_norm(q.astype(jnp.float32).reshape(b, S, ATTN_HEADS, ATTN_HEAD_DIM), g_q)
    k = rms_norm(k.astype(jnp.float32).reshape(b, S, ATTN_HEADS, ATTN_HEAD_DIM), g_k)
    v = v.astype(jnp.float32).reshape(b, S, ATTN_HEADS, ATTN_HEAD_DIM)
    log_f = jax.nn.log_sigmoid(f_raw.astype(jnp.float32) + f_bias.astype(jnp.float32))
    o = forgetting_attention(q, k, v, log_f).reshape(b, S, ATTN_WIDTH)
    mixed = jnp.concatenate([y, o], axis=-1).astype(h.dtype)
    return mixed @ w_out


def memory_cross_attention(h, mem_n, xq_w, xkv_w, xg_q, xg_k, xo_w):
    b, S, _ = h.shape
    q = (h @ xq_w).astype(jnp.float32).reshape(b, S, XATTN_HEADS, XATTN_HEAD_DIM)
    kv = (mem_n @ xkv_w).astype(jnp.float32)
    k, v = jnp.split(kv, 2, axis=-1)
    k = k.reshape(b, MEM_LEN, XATTN_HEADS, XATTN_HEAD_DIM)
    v = v.reshape(b, MEM_LEN, XATTN_HEADS, XATTN_HEAD_DIM)
    q = rms_norm(q, xg_q)
    k = rms_norm(k, xg_k)
    s = jnp.einsum("bqhd,bkhd->bhqk", q, k) * (XATTN_HEAD_DIM ** -0.5)
    pr = jax.nn.softmax(s, axis=-1)
    o = jnp.einsum("bhqk,bkhd->bqhd", pr, v).reshape(b, S, D_MODEL).astype(h.dtype)
    return o @ xo_w


def squared_relu_mlp(h, w_up, w_down):
    u = jax.nn.relu(h @ w_up)
    return (u * u) @ w_down


def _fwd_setup_inputs(seed: int = 0) -> dict:
    key = jax.random.key(seed)
    ks = jax.random.split(key, 24)
    f32 = jnp.float32

    def nrm(k, shape, fan_in):
        return jax.random.normal(k, shape, f32) * (fan_in ** -0.5)

    def gain(k, shape):
        return 1.0 + 0.02 * jax.random.normal(k, shape, f32)

    dt0 = jnp.exp(jax.random.uniform(ks[6], (DEPTH, SSM_HEADS), f32,
                                     math.log(1e-3), math.log(1e-1)))
    dt_bias = dt0 + jnp.log(-jnp.expm1(-dt0))
    return {
        "x": jax.random.normal(ks[0], (BATCH, SEQ, D_MODEL), f32),
        "mem": jax.random.normal(ks[1], (BATCH, MEM_LEN, D_MODEL), f32),
        "g_mix": gain(ks[2], (DEPTH, D_MODEL)),
        "w_in": nrm(ks[3], (DEPTH, D_MODEL, IN_COLS), D_MODEL),
        "conv_w": nrm(ks[4], (DEPTH, CONV_WIDTH, CONV_DIM), CONV_WIDTH),
        "conv_b": 0.02 * jax.random.normal(ks[5], (DEPTH, CONV_DIM), f32),
        "dt_bias": dt_bias,
        "a_log": jnp.log(jax.random.uniform(ks[7], (DEPTH, SSM_HEADS), f32, 1.0, 16.0)),
        "d_skip": gain(ks[8], (DEPTH, SSM_HEADS)),
        "ssm_norm_w": gain(ks[9], (DEPTH, SSM_D_INNER)),
        "g_q": gain(ks[10], (DEPTH, ATTN_HEAD_DIM)),
        "g_k": gain(ks[11], (DEPTH, ATTN_HEAD_DIM)),
        "f_bias": jax.random.uniform(ks[12], (DEPTH, ATTN_HEADS), f32, 2.0, 6.0),
        "w_out": nrm(ks[13], (DEPTH, MIX_WIDTH, D_MODEL), MIX_WIDTH),
        "g_xattn": gain(ks[14], (DEPTH, D_MODEL)),
        "g_mem": gain(ks[15], (DEPTH, D_MODEL)),
        "xq_w": nrm(ks[16], (DEPTH, D_MODEL, D_MODEL), D_MODEL),
        "xkv_w": nrm(ks[17], (DEPTH, D_MODEL, 2 * D_MODEL), D_MODEL),
        "xg_q": gain(ks[18], (DEPTH, XATTN_HEAD_DIM)),
        "xg_k": gain(ks[19], (DEPTH, XATTN_HEAD_DIM)),
        "xo_w": nrm(ks[20], (DEPTH, D_MODEL, D_MODEL), D_MODEL),
        "g_mlp": gain(ks[21], (DEPTH, D_MODEL)),
        "w_up": nrm(ks[22], (DEPTH, D_MODEL, D_FF), D_MODEL),
        "w_down": nrm(ks[23], (DEPTH, D_FF, D_MODEL), D_FF),
    }


def _fwd_reference(x, mem, g_mix, w_in, conv_w, conv_b, dt_bias, a_log, d_skip, ssm_norm_w,
              g_q, g_k, f_bias, w_out, g_xattn, g_mem, xq_w, xkv_w, xg_q, xg_k, xo_w,
              g_mlp, w_up, w_down):
    for l in range(DEPTH):
        h = rms_norm(x, g_mix[l])
        x = x + parallel_mixer(h, w_in[l], conv_w[l], conv_b[l], dt_bias[l], a_log[l],
                               d_skip[l], ssm_norm_w[l], g_q[l], g_k[l], f_bias[l], w_out[l])
        h = rms_norm(x, g_xattn[l])
        mem_n = rms_norm(mem, g_mem[l])
        x = x + memory_cross_attention(h, mem_n, xq_w[l], xkv_w[l], xg_q[l], xg_k[l], xo_w[l])
        h = rms_norm(x, g_mlp[l])
        x = x + squared_relu_mlp(h, w_up[l], w_down[l])
    return x


import jax as _jax
import jax.numpy as _jnp

TWIN_FORMAT = 'train_step'
FWD_PARAMS = ['x', 'mem', 'g_mix', 'w_in', 'conv_w', 'conv_b', 'dt_bias', 'a_log', 'd_skip', 'ssm_norm_w', 'g_q', 'g_k', 'f_bias', 'w_out', 'g_xattn', 'g_mem', 'xq_w', 'xkv_w', 'xg_q', 'xg_k', 'xo_w', 'g_mlp', 'w_up', 'w_down']
TWIN_WEIGHTS = ['g_mix', 'w_in', 'conv_w', 'conv_b', 'dt_bias', 'a_log', 'd_skip', 'ssm_norm_w', 'g_q', 'g_k', 'f_bias', 'w_out', 'g_xattn', 'g_mem', 'xq_w', 'xkv_w', 'xg_q', 'xg_k', 'xo_w', 'g_mlp', 'w_up', 'w_down']
TWIN_DIFF_INPUT = 'x'
TWIN_INPUTS = ['x', 'mem', 'g_mix', 'w_in', 'conv_w', 'conv_b', 'dt_bias', 'a_log', 'd_skip', 'ssm_norm_w', 'g_q', 'g_k', 'f_bias', 'w_out', 'g_xattn', 'g_mem', 'xq_w', 'xkv_w', 'xg_q', 'xg_k', 'xo_w', 'g_mlp', 'w_up', 'w_down', 'loss_target', 'm_g_mix', 'm_w_in', 'm_conv_w', 'm_conv_b', 'm_dt_bias', 'm_a_log', 'm_d_skip', 'm_ssm_norm_w', 'm_g_q', 'm_g_k', 'm_f_bias', 'm_w_out', 'm_g_xattn', 'm_g_mem', 'm_xq_w', 'm_xkv_w', 'm_xg_q', 'm_xg_k', 'm_xo_w', 'm_g_mlp', 'm_w_up', 'm_w_down', 'v_g_mix', 'v_w_in', 'v_conv_w', 'v_conv_b', 'v_dt_bias', 'v_a_log', 'v_d_skip', 'v_ssm_norm_w', 'v_g_q', 'v_g_k', 'v_f_bias', 'v_w_out', 'v_g_xattn', 'v_g_mem', 'v_xq_w', 'v_xkv_w', 'v_xg_q', 'v_xg_k', 'v_xo_w', 'v_g_mlp', 'v_w_up', 'v_w_down']
TWIN_OUTPUTS = ['loss', 'grad_x', 'grad_g_mix', 'grad_w_in', 'grad_conv_w', 'grad_conv_b', 'grad_dt_bias', 'grad_a_log', 'grad_d_skip', 'grad_ssm_norm_w', 'grad_g_q', 'grad_g_k', 'grad_f_bias', 'grad_w_out', 'grad_g_xattn', 'grad_g_mem', 'grad_xq_w', 'grad_xkv_w', 'grad_xg_q', 'grad_xg_k', 'grad_xo_w', 'grad_g_mlp', 'grad_w_up', 'grad_w_down', 'delta_g_mix', 'delta_w_in', 'delta_conv_w', 'delta_conv_b', 'delta_dt_bias', 'delta_a_log', 'delta_d_skip', 'delta_ssm_norm_w', 'delta_g_q', 'delta_g_k', 'delta_f_bias', 'delta_w_out', 'delta_g_xattn', 'delta_g_mem', 'delta_xq_w', 'delta_xkv_w', 'delta_xg_q', 'delta_xg_k', 'delta_xo_w', 'delta_g_mlp', 'delta_w_up', 'delta_w_down', 'new_m_g_mix', 'new_m_w_in', 'new_m_conv_w', 'new_m_conv_b', 'new_m_dt_bias', 'new_m_a_log', 'new_m_d_skip', 'new_m_ssm_norm_w', 'new_m_g_q', 'new_m_g_k', 'new_m_f_bias', 'new_m_w_out', 'new_m_g_xattn', 'new_m_g_mem', 'new_m_xq_w', 'new_m_xkv_w', 'new_m_xg_q', 'new_m_xg_k', 'new_m_xo_w', 'new_m_g_mlp', 'new_m_w_up', 'new_m_w_down', 'new_v_g_mix', 'new_v_w_in', 'new_v_conv_w', 'new_v_conv_b', 'new_v_dt_bias', 'new_v_a_log', 'new_v_d_skip', 'new_v_ssm_norm_w', 'new_v_g_q', 'new_v_g_k', 'new_v_f_bias', 'new_v_w_out', 'new_v_g_xattn', 'new_v_g_mem', 'new_v_xq_w', 'new_v_xkv_w', 'new_v_xg_q', 'new_v_xg_k', 'new_v_xo_w', 'new_v_g_mlp', 'new_v_w_up', 'new_v_w_down']
TWIN_LEAF_KINDS = {'loss': 'loss', 'grad_x': 'grad_x', 'grad_g_mix': 'grad_w', 'grad_w_in': 'grad_w', 'grad_conv_w': 'grad_w', 'grad_conv_b': 'grad_w', 'grad_dt_bias': 'grad_w', 'grad_a_log': 'grad_w', 'grad_d_skip': 'grad_w', 'grad_ssm_norm_w': 'grad_w', 'grad_g_q': 'grad_w', 'grad_g_k': 'grad_w', 'grad_f_bias': 'grad_w', 'grad_w_out': 'grad_w', 'grad_g_xattn': 'grad_w', 'grad_g_mem': 'grad_w', 'grad_xq_w': 'grad_w', 'grad_xkv_w': 'grad_w', 'grad_xg_q': 'grad_w', 'grad_xg_k': 'grad_w', 'grad_xo_w': 'grad_w', 'grad_g_mlp': 'grad_w', 'grad_w_up': 'grad_w', 'grad_w_down': 'grad_w', 'delta_g_mix': 'delta_w', 'delta_w_in': 'delta_w', 'delta_conv_w': 'delta_w', 'delta_conv_b': 'delta_w', 'delta_dt_bias': 'delta_w', 'delta_a_log': 'delta_w', 'delta_d_skip': 'delta_w', 'delta_ssm_norm_w': 'delta_w', 'delta_g_q': 'delta_w', 'delta_g_k': 'delta_w', 'delta_f_bias': 'delta_w', 'delta_w_out': 'delta_w', 'delta_g_xattn': 'delta_w', 'delta_g_mem': 'delta_w', 'delta_xq_w': 'delta_w', 'delta_xkv_w': 'delta_w', 'delta_xg_q': 'delta_w', 'delta_xg_k': 'delta_w', 'delta_xo_w': 'delta_w', 'delta_g_mlp': 'delta_w', 'delta_w_up': 'delta_w', 'delta_w_down': 'delta_w', 'new_m_g_mix': 'new_m', 'new_m_w_in': 'new_m', 'new_m_conv_w': 'new_m', 'new_m_conv_b': 'new_m', 'new_m_dt_bias': 'new_m', 'new_m_a_log': 'new_m', 'new_m_d_skip': 'new_m', 'new_m_ssm_norm_w': 'new_m', 'new_m_g_q': 'new_m', 'new_m_g_k': 'new_m', 'new_m_f_bias': 'new_m', 'new_m_w_out': 'new_m', 'new_m_g_xattn': 'new_m', 'new_m_g_mem': 'new_m', 'new_m_xq_w': 'new_m', 'new_m_xkv_w': 'new_m', 'new_m_xg_q': 'new_m', 'new_m_xg_k': 'new_m', 'new_m_xo_w': 'new_m', 'new_m_g_mlp': 'new_m', 'new_m_w_up': 'new_m', 'new_m_w_down': 'new_m', 'new_v_g_mix': 'new_v', 'new_v_w_in': 'new_v', 'new_v_conv_w': 'new_v', 'new_v_conv_b': 'new_v', 'new_v_dt_bias': 'new_v', 'new_v_a_log': 'new_v', 'new_v_d_skip': 'new_v', 'new_v_ssm_norm_w': 'new_v', 'new_v_g_q': 'new_v', 'new_v_g_k': 'new_v', 'new_v_f_bias': 'new_v', 'new_v_w_out': 'new_v', 'new_v_g_xattn': 'new_v', 'new_v_g_mem': 'new_v', 'new_v_xq_w': 'new_v', 'new_v_xkv_w': 'new_v', 'new_v_xg_q': 'new_v', 'new_v_xg_k': 'new_v', 'new_v_xo_w': 'new_v', 'new_v_g_mlp': 'new_v', 'new_v_w_up': 'new_v', 'new_v_w_down': 'new_v'}


def _forward(args):
    return _fwd_reference(*[args[k] for k in FWD_PARAMS])


def _output_shape():
    out = _jax.eval_shape(lambda: _forward(_fwd_setup_inputs(0)))
    return out.shape, out.dtype

N_MICROBATCH = 1
ADAM_LR = 0.001
ADAM_B1 = 0.9
ADAM_B2 = 0.999
ADAM_EPS = 1e-08
ADAM_WD = 0.01
ADAM_STEP = 10
PER_EXAMPLE_BATCH_AXIS = {'x': 0, 'mem': 0, 'loss_target': 0}
SHARED_INPUTS = []
_WEIGHT_DTYPES = {'g_mix': _jnp.float32, 'w_in': _jnp.float32, 'conv_w': _jnp.float32, 'conv_b': _jnp.float32, 'dt_bias': _jnp.float32, 'a_log': _jnp.float32, 'd_skip': _jnp.float32, 'ssm_norm_w': _jnp.float32, 'g_q': _jnp.float32, 'g_k': _jnp.float32, 'f_bias': _jnp.float32, 'w_out': _jnp.float32, 'g_xattn': _jnp.float32, 'g_mem': _jnp.float32, 'xq_w': _jnp.float32, 'xkv_w': _jnp.float32, 'xg_q': _jnp.float32, 'xg_k': _jnp.float32, 'xo_w': _jnp.float32, 'g_mlp': _jnp.float32, 'w_up': _jnp.float32, 'w_down': _jnp.float32}
MOMENT_SCALE = {'g_mix': 7.209094e-01, 'w_in': 2.532024e-01, 'conv_w': 1.105619e+00, 'conv_b': 3.799608e+00, 'dt_bias': 1.419540e+00, 'a_log': 6.839252e+00, 'd_skip': 4.108299e+00, 'ssm_norm_w': 1.218748e+01, 'g_q': 3.513404e+00, 'g_k': 3.527393e+00, 'f_bias': 1.084669e+01, 'w_out': 2.073698e+00, 'g_xattn': 8.318145e-02, 'g_mem': 4.798193e-01, 'xq_w': 8.391121e-02, 'xkv_w': 3.091495e-01, 'xg_q': 6.546346e-01, 'xg_k': 6.561084e-01, 'xo_w': 4.379692e-01, 'g_mlp': 4.844985e+01, 'w_up': 9.232363e-01, 'w_down': 4.364625e+00}


def _to_microbatches(a, axis):
    t = _jnp.moveaxis(a, axis, 0)
    t = t.reshape((N_MICROBATCH, t.shape[0] // N_MICROBATCH) + t.shape[1:])
    return _jnp.moveaxis(t, 1, axis + 1)


def setup_inputs(seed: int = 0) -> dict:
    inp = _fwd_setup_inputs(seed)
    key = _jax.random.fold_in(_jax.random.key(seed), 7919)
    shape, _ = _output_shape()
    out = dict(inp)
    out["loss_target"] = _jax.random.normal(_jax.random.fold_in(key, 0), shape, _jnp.float32)
    for i, name in enumerate(TWIN_WEIGHTS):
        w = inp[name].astype(_jnp.float32)
        if MOMENT_SCALE is None:
            s = _jnp.sqrt(_jnp.mean(_jnp.square(w)) + 1e-30)
        else:
            s = MOMENT_SCALE[name]
        km, kv = _jax.random.split(_jax.random.fold_in(key, i + 1))
        out[name] = w
        out["m_" + name] = s * _jax.random.normal(km, w.shape, _jnp.float32)
        out["v_" + name] = (s * s) * _jax.random.uniform(kv, w.shape, _jnp.float32, 0.5, 1.5)
    if N_MICROBATCH > 1:
        for name, axis in PER_EXAMPLE_BATCH_AXIS.items():
            out[name] = _to_microbatches(out[name], axis)
    return {'x': out['x'], 'mem': out['mem'], 'g_mix': out['g_mix'], 'w_in': out['w_in'], 'conv_w': out['conv_w'], 'conv_b': out['conv_b'], 'dt_bias': out['dt_bias'], 'a_log': out['a_log'], 'd_skip': out['d_skip'], 'ssm_norm_w': out['ssm_norm_w'], 'g_q': out['g_q'], 'g_k': out['g_k'], 'f_bias': out['f_bias'], 'w_out': out['w_out'], 'g_xattn': out['g_xattn'], 'g_mem': out['g_mem'], 'xq_w': out['xq_w'], 'xkv_w': out['xkv_w'], 'xg_q': out['xg_q'], 'xg_k': out['xg_k'], 'xo_w': out['xo_w'], 'g_mlp': out['g_mlp'], 'w_up': out['w_up'], 'w_down': out['w_down'], 'loss_target': out['loss_target'], 'm_g_mix': out['m_g_mix'], 'm_w_in': out['m_w_in'], 'm_conv_w': out['m_conv_w'], 'm_conv_b': out['m_conv_b'], 'm_dt_bias': out['m_dt_bias'], 'm_a_log': out['m_a_log'], 'm_d_skip': out['m_d_skip'], 'm_ssm_norm_w': out['m_ssm_norm_w'], 'm_g_q': out['m_g_q'], 'm_g_k': out['m_g_k'], 'm_f_bias': out['m_f_bias'], 'm_w_out': out['m_w_out'], 'm_g_xattn': out['m_g_xattn'], 'm_g_mem': out['m_g_mem'], 'm_xq_w': out['m_xq_w'], 'm_xkv_w': out['m_xkv_w'], 'm_xg_q': out['m_xg_q'], 'm_xg_k': out['m_xg_k'], 'm_xo_w': out['m_xo_w'], 'm_g_mlp': out['m_g_mlp'], 'm_w_up': out['m_w_up'], 'm_w_down': out['m_w_down'], 'v_g_mix': out['v_g_mix'], 'v_w_in': out['v_w_in'], 'v_conv_w': out['v_conv_w'], 'v_conv_b': out['v_conv_b'], 'v_dt_bias': out['v_dt_bias'], 'v_a_log': out['v_a_log'], 'v_d_skip': out['v_d_skip'], 'v_ssm_norm_w': out['v_ssm_norm_w'], 'v_g_q': out['v_g_q'], 'v_g_k': out['v_g_k'], 'v_f_bias': out['v_f_bias'], 'v_w_out': out['v_w_out'], 'v_g_xattn': out['v_g_xattn'], 'v_g_mem': out['v_g_mem'], 'v_xq_w': out['v_xq_w'], 'v_xkv_w': out['v_xkv_w'], 'v_xg_q': out['v_xg_q'], 'v_xg_k': out['v_xg_k'], 'v_xo_w': out['v_xo_w'], 'v_g_mlp': out['v_g_mlp'], 'v_w_up': out['v_w_up'], 'v_w_down': out['v_w_down']}


def _loss(weights, diff, rest, loss_target):
    with _jax.named_scope("forward"):
        args = {**rest, TWIN_DIFF_INPUT: diff, **{k: w.astype(_WEIGHT_DTYPES[k]) for k, w in weights.items()}}
        y = _forward(args)
    with _jax.named_scope("loss_head"):
        err = _jnp.square(y.astype(_jnp.float32) - loss_target)
        return 0.5 * _jnp.sum(_jnp.mean(err, axis=-1)) if err.ndim else 0.5 * err


def _adamw(w, g, m, v):
    m = ADAM_B1 * m + (1.0 - ADAM_B1) * g
    v = ADAM_B2 * v + (1.0 - ADAM_B2) * _jnp.square(g)
    m_hat = m / (1.0 - ADAM_B1 ** ADAM_STEP)
    v_hat = v / (1.0 - ADAM_B2 ** ADAM_STEP)
    delta = -ADAM_LR * (m_hat / (_jnp.sqrt(v_hat) + ADAM_EPS) + ADAM_WD * w)
    return delta, m, v


def reference(x, mem, g_mix, w_in, conv_w, conv_b, dt_bias, a_log, d_skip, ssm_norm_w, g_q, g_k, f_bias, w_out, g_xattn, g_mem, xq_w, xkv_w, xg_q, xg_k, xo_w, g_mlp, w_up, w_down, loss_target, m_g_mix, m_w_in, m_conv_w, m_conv_b, m_dt_bias, m_a_log, m_d_skip, m_ssm_norm_w, m_g_q, m_g_k, m_f_bias, m_w_out, m_g_xattn, m_g_mem, m_xq_w, m_xkv_w, m_xg_q, m_xg_k, m_xo_w, m_g_mlp, m_w_up, m_w_down, v_g_mix, v_w_in, v_conv_w, v_conv_b, v_dt_bias, v_a_log, v_d_skip, v_ssm_norm_w, v_g_q, v_g_k, v_f_bias, v_w_out, v_g_xattn, v_g_mem, v_xq_w, v_xkv_w, v_xg_q, v_xg_k, v_xo_w, v_g_mlp, v_w_up, v_w_down):
    given = dict(x=x, mem=mem, g_mix=g_mix, w_in=w_in, conv_w=conv_w, conv_b=conv_b, dt_bias=dt_bias, a_log=a_log, d_skip=d_skip, ssm_norm_w=ssm_norm_w, g_q=g_q, g_k=g_k, f_bias=f_bias, w_out=w_out, g_xattn=g_xattn, g_mem=g_mem, xq_w=xq_w, xkv_w=xkv_w, xg_q=xg_q, xg_k=xg_k, xo_w=xo_w, g_mlp=g_mlp, w_up=w_up, w_down=w_down, loss_target=loss_target, m_g_mix=m_g_mix, m_w_in=m_w_in, m_conv_w=m_conv_w, m_conv_b=m_conv_b, m_dt_bias=m_dt_bias, m_a_log=m_a_log, m_d_skip=m_d_skip, m_ssm_norm_w=m_ssm_norm_w, m_g_q=m_g_q, m_g_k=m_g_k, m_f_bias=m_f_bias, m_w_out=m_w_out, m_g_xattn=m_g_xattn, m_g_mem=m_g_mem, m_xq_w=m_xq_w, m_xkv_w=m_xkv_w, m_xg_q=m_xg_q, m_xg_k=m_xg_k, m_xo_w=m_xo_w, m_g_mlp=m_g_mlp, m_w_up=m_w_up, m_w_down=m_w_down, v_g_mix=v_g_mix, v_w_in=v_w_in, v_conv_w=v_conv_w, v_conv_b=v_conv_b, v_dt_bias=v_dt_bias, v_a_log=v_a_log, v_d_skip=v_d_skip, v_ssm_norm_w=v_ssm_norm_w, v_g_q=v_g_q, v_g_k=v_g_k, v_f_bias=v_f_bias, v_w_out=v_w_out, v_g_xattn=v_g_xattn, v_g_mem=v_g_mem, v_xq_w=v_xq_w, v_xkv_w=v_xkv_w, v_xg_q=v_xg_q, v_xg_k=v_xg_k, v_xo_w=v_xo_w, v_g_mlp=v_g_mlp, v_w_up=v_w_up, v_w_down=v_w_down)
    weights = {n: given[n] for n in TWIN_WEIGHTS}
    shared = {n: given[n] for n in SHARED_INPUTS}
    per_example = {n: given[n] for n in ['x', 'mem']}
    grad_fn = _jax.value_and_grad(_loss, argnums=(0, 1))

    def one_microbatch(ex, loss_target):
        ex = dict(ex)
        diff = ex.pop(TWIN_DIFF_INPUT)
        return grad_fn(weights, diff, {**shared, **ex}, loss_target)

    if N_MICROBATCH == 1:
        loss, (grad_w, grad_x) = one_microbatch(per_example, given["loss_target"])
    else:
        def body(carry, xs):
            loss_sum, grad_sum = carry
            l_k, (gw_k, gx_k) = one_microbatch(xs[0], xs[1])
            with _jax.named_scope("update"):
                return (loss_sum + l_k, _jax.tree.map(_jnp.add, grad_sum, gw_k)), gx_k

        init = (_jnp.zeros((), _jnp.float32), _jax.tree.map(_jnp.zeros_like, weights))
        (loss, grad_w), grad_x = _jax.lax.scan(body, init, (per_example, given["loss_target"]))
    with _jax.named_scope("update"):
        delta_w, new_m, new_v = {}, {}, {}
        for n in TWIN_WEIGHTS:
            delta_w[n], new_m[n], new_v[n] = _adamw(weights[n], grad_w[n], given["m_" + n], given["v_" + n])
    return (loss, grad_x, *[grad_w[n] for n in TWIN_WEIGHTS], *[delta_w[n] for n in TWIN_WEIGHTS],
            *[new_m[n] for n in TWIN_WEIGHTS], *[new_v[n] for n in TWIN_WEIGHTS])
```

```python
import functools
import math

import jax
import jax.numpy as jnp
from jax import lax
from jax.experimental import pallas as pl
from jax.experimental.pallas import tpu as pltpu

F32 = jnp.float32
BF16 = jnp.bfloat16
HI = lax.Precision.HIGHEST
MESH = pl.DeviceIdType.MESH

EPS = 1e-5
CHUNK = 128
SSM_HEADS = 16
SSM_GROUPS = 2
HEADS_PER_GROUP = SSM_HEADS // SSM_GROUPS
HEAD_DIM = 64
SSM_STATE = 128
ATTN_HEADS = 16
XATTN_HEADS = 4
XATTN_DIM = 256
CONV_WIDTH = 4
N_CHIPS = 4
N_DEV = 8
LANES = 128
VMEM_LIMIT = 56 * 1024 * 1024

ADAM_LR = 0.001
ADAM_B1 = 0.9
ADAM_B2 = 0.999
ADAM_EPS = 1e-08
ADAM_WD = 0.01
ADAM_STEP = 10


def _params(sem):
    return pltpu.CompilerParams(dimension_semantics=sem, vmem_limit_bytes=VMEM_LIMIT)


def _pick(n, cands):
    for c in cands:
        if n % c == 0:
            return c
    return n


def _mm(a, b, mode, name, out_dtypes=(F32,), epilogue=None, extras=(), b_chunks=1, out_chunks=1,
        tm=None, tn=None, tk=None):
    if mode == "nn":
        M, K = a.shape
        N = b.shape[-1] * b_chunks
    elif mode == "nt":
        M, K = a.shape
        N = b.shape[-2]
        assert b.shape[-1] * b_chunks == K
    else:
        K, M = a.shape
        N = b.shape[-1] * b_chunks
    tm = tm or _pick(M, (512, 256, 128))
    tn = tn or _pick(N // max(b_chunks if mode != "nt" else 1, out_chunks), (512, 1152, 640, 384, 256, 128))
    if tk is None:
        kmax = b.shape[-1] if mode == "nt" else K
        tk = kmax if kmax <= 2048 else _pick(kmax, (2048, 1152, 1024, 512))
    nk = K // tk
    assert M % tm == 0 and N % tn == 0 and K % tk == 0
    grid = (M // tm, N // tn, nk)

    if mode == "tn":
        a_spec = pl.BlockSpec((tk, tm), lambda i, j, k: (k, i))
    else:
        a_spec = pl.BlockSpec((tm, tk), lambda i, j, k: (i, k))

    def b_index(t_row, t_last, tile_last):
        if b_chunks == 1:
            return (t_row, t_last)
        q = (b.shape[-1]) // tile_last
        return (t_last // q, t_row, t_last % q)

    if mode == "nn" or mode == "tn":
        bshape = (tk, tn)
        bmap = lambda i, j, k: b_index(k, j, tn)
    else:
        bshape = (tn, tk)
        bmap = lambda i, j, k: b_index(j, k, tk)
    if b_chunks > 1:
        bshape = (None,) + bshape
    b_spec = pl.BlockSpec(bshape, bmap)

    if out_chunks == 1:
        o_spec = pl.BlockSpec((tm, tn), lambda i, j, k: (i, j))
        o_shape = (M, N)
    else:
        qo = (N // out_chunks) // tn
        o_spec = pl.BlockSpec((None, tm, tn), lambda i, j, k: (j // qo, i, j % qo))
        o_shape = (out_chunks, M, N // out_chunks)
    e_spec = pl.BlockSpec((tm, tn), lambda i, j, k: (i, j))

    dims = {"nn": (((1,), (0,)), ((), ())), "nt": (((1,), (1,)), ((), ())), "tn": (((0,), (0,)), ((), ()))}[mode]
    n_ex = len(extras)
    n_out = len(out_dtypes)

    def body(*refs):
        a_ref, b_ref = refs[0], refs[1]
        ex_refs = refs[2:2 + n_ex]
        o_refs = refs[2 + n_ex:2 + n_ex + n_out]

        def finish(acc):
            outs = epilogue(acc, *[r[...] for r in ex_refs]) if epilogue is not None else (acc,)
            for r, o in zip(o_refs, outs):
                r[...] = o.astype(r.dtype)

        part = lax.dot_general(a_ref[...].astype(BF16), b_ref[...].astype(BF16), dims,
                               preferred_element_type=F32)
        if nk == 1:
            finish(part)
        else:
            acc_ref = refs[-1]
            k = pl.program_id(2)

            @pl.when(k == 0)
            def _():
                acc_ref[...] = part

            @pl.when(k > 0)
            def _():
                acc_ref[...] += part

            @pl.when(k == nk - 1)
            def _():
                finish(acc_ref[...])

    outs = pl.pallas_call(
        body,
        grid=grid,
        in_specs=[a_spec, b_spec] + [e_spec] * n_ex,
        out_specs=[o_spec] * n_out,
        out_shape=[jax.ShapeDtypeStruct(o_shape, d) for d in out_dtypes],
        scratch_shapes=[pltpu.VMEM((tm, tn), F32)] if nk > 1 else [],
        compiler_params=_params(("parallel", "parallel", "arbitrary")),
        name=name,
    )(a, b, *extras)
    return outs[0] if n_out == 1 else outs


def _rms(x, g):
    r = lax.rsqrt(jnp.mean(x * x, axis=-1, keepdims=True) + EPS)
    return x * r * g


def _rmsnorm_fwd(x, g, name):
    R, D = x.shape
    tr = _pick(R, (512, 256))

    def body(x_ref, g_ref, o_ref):
        o_ref[...] = _rms(x_ref[...], g_ref[...]).astype(o_ref.dtype)

    return pl.pallas_call(
        body, grid=(R // tr,),
        in_specs=[pl.BlockSpec((tr, D), lambda i: (i, 0)), pl.BlockSpec((1, D), lambda i: (0, 0))],
        out_specs=pl.BlockSpec((tr, D), lambda i: (i, 0)),
        out_shape=jax.ShapeDtypeStruct((R, D), BF16),
        compiler_params=_params(("parallel",)), name=name)(x, g)


def _rmsnorm_bwd(x, g, dh, dres, name):
    R, D = x.shape
    tr = _pick(R, (256,))
    has_res = dres is not None

    def body(*refs):
        if has_res:
            x_ref, g_ref, dh_ref, dres_ref, dx_ref, dg_ref = refs
        else:
            x_ref, g_ref, dh_ref, dx_ref, dg_ref = refs
        _, vjp = jax.vjp(_rms, x_ref[...], g_ref[...])
        dx, dg = vjp(dh_ref[...])
        if has_res:
            dx = dx + dres_ref[...]
        dx_ref[...] = dx

        @pl.when(pl.program_id(0) == 0)
        def _():
            dg_ref[...] = jnp.zeros_like(dg_ref)

        dg_ref[...] += dg

    row = pl.BlockSpec((tr, D), lambda i: (i, 0))
    vec = pl.BlockSpec((1, D), lambda i: (0, 0))
    ins = [x, g, dh] + ([dres] if has_res else [])
    return pl.pallas_call(
        body, grid=(R // tr,),
        in_specs=[row, vec, row] + ([row] if has_res else []),
        out_specs=[row, vec],
        out_shape=[jax.ShapeDtypeStruct((R, D), F32), jax.ShapeDtypeStruct((1, D), F32)],
        compiler_params=_params(("arbitrary",)), name=name)(*ins)


def _shift_down(u, k):
    if k == 0:
        return u
    rows = lax.broadcasted_iota(jnp.int32, u.shape, 0)
    return jnp.where(rows >= k, pltpu.roll(u, k, axis=0), 0.0)


def _shift_up(u, k):
    if k == 0:
        return u
    n = u.shape[0]
    rows = lax.broadcasted_iota(jnp.int32, u.shape, 0)
    return jnp.where(rows < n - k, pltpu.roll(u, n - k, axis=0), 0.0)


def _conv_pre(u, w, b):
    pre = b
    for j in range(CONV_WIDTH):
        pre = pre + w[j:j + 1, :] * _shift_down(u, CONV_WIDTH - 1 - j)
    return pre


def _conv_fwd(proj, col0, ncols, conv_w, conv_b):
    S = proj.shape[0]
    cb0 = col0 // LANES

    def body(u_ref, w_ref, b_ref, o_ref):
        pre = _conv_pre(u_ref[...], w_ref[...], b_ref[...])
        o_ref[...] = pre * jax.nn.sigmoid(pre)

    return pl.pallas_call(
        body, grid=(ncols // LANES,),
        in_specs=[pl.BlockSpec((S, LANES), lambda j: (0, j + cb0)),
                  pl.BlockSpec((CONV_WIDTH, LANES), lambda j: (0, j)),
                  pl.BlockSpec((1, LANES), lambda j: (0, j))],
        out_specs=pl.BlockSpec((S, LANES), lambda j: (0, j)),
        out_shape=jax.ShapeDtypeStruct((S, ncols), F32),
        compiler_params=_params(("parallel",)), name="conv_fwd")(proj, conv_w, conv_b)


def _conv_bwd(proj, col0, ncols, conv_w, conv_b, dout):
    S = proj.shape[0]
    cb0 = col0 // LANES

    def body(u_ref, w_ref, b_ref, d_ref, du_ref, dw_ref, db_ref):
        u = u_ref[...]
        w = w_ref[...]
        pre = _conv_pre(u, w, b_ref[...])
        s = jax.nn.sigmoid(pre)
        dpre = d_ref[...] * (s * (1.0 + pre * (1.0 - s)))
        du = jnp.zeros_like(u)
        rows = []
        for j in range(CONV_WIDTH):
            k = CONV_WIDTH - 1 - j
            du = du + w[j:j + 1, :] * _shift_up(dpre, k)
            rows.append(jnp.sum(dpre * _shift_down(u, k), axis=0, keepdims=True))
        du_ref[...] = du.astype(du_ref.dtype)
        rows.append(jnp.zeros((8 - CONV_WIDTH, LANES), F32))
        dw_ref[...] = jnp.concatenate(rows, axis=0)
        db_ref[...] = jnp.sum(dpre, axis=0, keepdims=True)

    return pl.pallas_call(
        body, grid=(ncols // LANES,),
        in_specs=[pl.BlockSpec((S, LANES), lambda j: (0, j + cb0)),
                  pl.BlockSpec((CONV_WIDTH, LANES), lambda j: (0, j)),
                  pl.BlockSpec((1, LANES), lambda j: (0, j)),
                  pl.BlockSpec((S, LANES), lambda j: (0, j))],
        out_specs=[pl.BlockSpec((S, LANES), lambda j: (0, j)),
                   pl.BlockSpec((8, LANES), lambda j: (0, j)),
                   pl.BlockSpec((1, LANES), lambda j: (0, j))],
        out_shape=[jax.ShapeDtypeStruct((S, ncols), BF16),
                   jax.ShapeDtypeStruct((8, ncols), F32),
                   jax.ShapeDtypeStruct((1, ncols), F32)],
        compiler_params=_params(("parallel",)), name="conv_bwd")(proj, conv_w, conv_b, dout)


def _softplus(x):
    return jnp.maximum(x, 0.0) + jnp.log1p(jnp.exp(-jnp.abs(x)))


def _dot32(a, b, dims=(((1,), (0,)), ((), ()))):
    return lax.dot_general(a, b, dims, precision=HI, preferred_element_type=F32)


def _ssd_chunk(xs, Bm, Cm, z, dtr, dtb, alog, dsk, nw, h):
    L = Bm.shape[0]
    ri = lax.broadcasted_iota(jnp.int32, (L, L), 0)
    ci = lax.broadcasted_iota(jnp.int32, (L, L), 1)
    causal = ri >= ci
    tril = causal.astype(F32)
    CB = _dot32(Cm, Bm, (((1,), (1,)), ((), ())))
    gated, hnew = [], []
    ssq = jnp.zeros((L, 1), F32)
    for r in range(len(xs)):
        dt = _softplus(dtr[r] + dtb[r])
        dA = dt * (-jnp.exp(alog[r]))
        acs = _dot32(tril, dA)
        tot = jnp.sum(dA, axis=0, keepdims=True)
        cc = _dot32(tril, jnp.broadcast_to(dA, (L, L)))
        seg = jnp.where(causal, cc - cc.T, -1e30)
        Lmat = jnp.exp(seg)
        X = xs[r] * dt
        y = _dot32(CB * Lmat, X) + jnp.exp(acs) * _dot32(Cm, h[r]) + dsk[r] * xs[r]
        hnew.append(jnp.exp(tot) * h[r] + _dot32(Bm, X * jnp.exp(tot - acs), (((0,), (0,)), ((), ()))))
        g = y * (z[r] * jax.nn.sigmoid(z[r]))
        ssq = ssq + jnp.sum(g * g, axis=-1, keepdims=True)
        gated.append(g)
    rs = lax.rsqrt(ssq / (len(xs) * xs[0].shape[-1]) + EPS)
    return [g * rs * nw[r] for r, g in enumerate(gated)], hnew


def _ssd_specs(S):
    H, P, N, L = HEADS_PER_GROUP, HEAD_DIM, SSM_STATE, CHUNK
    return dict(
        head=lambda rev: pl.BlockSpec((H, L, P), (lambda g, c: (g, rev(c), 0))),
        bc=lambda rev, off: pl.BlockSpec((L, N), (lambda g, c: (rev(c), off + g))),
        dt=lambda rev: pl.BlockSpec((H, L, 1), (lambda g, c: (g, rev(c), 0))),
        scal=pl.BlockSpec((H, 1, 1), lambda g, c: (g, 0, 0)),
        nw=pl.BlockSpec((H, 1, P), lambda g, c: (g, 0, 0)),
        hs=lambda rev: pl.BlockSpec((None, H, N, P), (lambda g, c: (rev(c), g, 0, 0))),
    )


def _ssd_fwd(xs_hm, xbc, z_hm, dt_hm, dtb, alog, dsk, nw_hm):
    S = xbc.shape[0]
    H, P, N, L = HEADS_PER_GROUP, HEAD_DIM, SSM_STATE, CHUNK
    nc = S // L
    sp = _ssd_specs(S)
    ident = lambda c: c
    xoff = (SSM_HEADS * HEAD_DIM) // LANES

    def body(xs_ref, b_ref, c_ref, z_ref, dt_ref, dtb_ref, al_ref, dsk_ref, nw_ref, y_ref, hs_ref, h_ref):
        @pl.when(pl.program_id(1) == 0)
        def _():
            h_ref[...] = jnp.zeros_like(h_ref)

        hs_ref[...] = h_ref[...]
        hd = range(H)
        out, hnew = _ssd_chunk([xs_ref[r] for r in hd], b_ref[...], c_ref[...], [z_ref[r] for r in hd],
                               [dt_ref[r] for r in hd], [dtb_ref[r] for r in hd], [al_ref[r] for r in hd],
                               [dsk_ref[r] for r in hd], [nw_ref[r] for r in hd], [h_ref[r] for r in hd])
        for r in hd:
            y_ref[r] = out[r]
            h_ref[r] = hnew[r]

    return pl.pallas_call(
        body, grid=(SSM_GROUPS, nc),
        in_specs=[sp["head"](ident), sp["bc"](ident, xoff), sp["bc"](ident, xoff + SSM_GROUPS), sp["head"](ident),
                  sp["dt"](ident), sp["scal"], sp["scal"], sp["scal"], sp["nw"]],
        out_specs=[sp["head"](ident), sp["hs"](ident)],
        out_shape=[jax.ShapeDtypeStruct((SSM_HEADS, S, P), F32),
                   jax.ShapeDtypeStruct((nc, SSM_HEADS, N, P), F32)],
        scratch_shapes=[pltpu.VMEM((H, N, P), F32)],
        compiler_params=_params(("parallel", "arbitrary")), name="ssd_fwd",
    )(xs_hm, xbc, xbc, z_hm, dt_hm, dtb, alog, dsk, nw_hm)


def _ssd_bwd(xs_hm, xbc, z_hm, dt_hm, dtb, alog, dsk, nw_hm, hs, dy_hm):
    S = xbc.shape[0]
    H, P, N, L = HEADS_PER_GROUP, HEAD_DIM, SSM_STATE, CHUNK
    nc = S // L
    sp = _ssd_specs(S)
    rev = lambda c: nc - 1 - c
    xoff = (SSM_HEADS * HEAD_DIM) // LANES

    def body(xs_ref, b_ref, c_ref, z_ref, dt_ref, dtb_ref, al_ref, dsk_ref, nw_ref, hs_ref, dy_ref,
             dxs_ref, dz_ref, db_ref, dc_ref, ddt_ref, ddtb_ref, dal_ref, ddsk_ref, dnw_ref, dh_ref):
        first = pl.program_id(1) == 0

        @pl.when(first)
        def _():
            dh_ref[...] = jnp.zeros_like(dh_ref)
            ddtb_ref[...] = jnp.zeros_like(ddtb_ref)
            dal_ref[...] = jnp.zeros_like(dal_ref)
            ddsk_ref[...] = jnp.zeros_like(ddsk_ref)
            dnw_ref[...] = jnp.zeros_like(dnw_ref)

        hd = range(H)
        args = ([xs_ref[r] for r in hd], b_ref[...], c_ref[...], [z_ref[r] for r in hd],
                [dt_ref[r] for r in hd], [dtb_ref[r] for r in hd], [al_ref[r] for r in hd],
                [dsk_ref[r] for r in hd], [nw_ref[r] for r in hd], [hs_ref[r] for r in hd])
        _, vjp = jax.vjp(_ssd_chunk, *args)
        dxs, dB, dC, dz, ddt, ddtb, dal, ddsk, dnw, dh = vjp(([dy_ref[r] for r in hd], [dh_ref[r] for r in hd]))
        db_ref[...] = dB
        dc_ref[...] = dC
        for r in hd:
            dxs_ref[r] = dxs[r]
            dz_ref[r] = dz[r]
            ddt_ref[r] = ddt[r]
            dh_ref[r] = dh[r]
            ddtb_ref[r] += ddtb[r]
            dal_ref[r] += dal[r]
            ddsk_ref[r] += ddsk[r]
            dnw_ref[r] += dnw[r]

    bc_out = lambda: pl.BlockSpec((L, N), lambda g, c: (rev(c), g))
    return pl.pallas_call(
        body, grid=(SSM_GROUPS, nc),
        in_specs=[sp["head"](rev), sp["bc"](rev, xoff), sp["bc"](rev, xoff + SSM_GROUPS), sp["head"](rev),
                  sp["dt"](rev), sp["scal"], sp["scal"], sp["scal"], sp["nw"], sp["hs"](rev), sp["head"](rev)],
        out_specs=[sp["head"](rev), sp["head"](rev), bc_out(), bc_out(), sp["dt"](rev),
                   sp["scal"], sp["scal"], sp["scal"], sp["nw"]],
        out_shape=[jax.ShapeDtypeStruct((SSM_HEADS, S, P), F32), jax.ShapeDtypeStruct((SSM_HEADS, S, P), F32),
                   jax.ShapeDtypeStruct((S, SSM_GROUPS * N), F32), jax.ShapeDtypeStruct((S, SSM_GROUPS * N), F32),
                   jax.ShapeDtypeStruct((SSM_HEADS, S, 1), F32),
                   jax.ShapeDtypeStruct((SSM_HEADS, 1, 1), F32), jax.ShapeDtypeStruct((SSM_HEADS, 1, 1), F32),
                   jax.ShapeDtypeStruct((SSM_HEADS, 1, 1), F32), jax.ShapeDtypeStruct((SSM_HEADS, 1, P), F32)],
        scratch_shapes=[pltpu.VMEM((H, N, P), F32)],
        compiler_params=_params(("parallel", "arbitrary")), name="ssd_bwd",
    )(xs_hm, xbc, xbc, z_hm, dt_hm, dtb, alog, dsk, nw_hm, hs, dy_hm)


ATTN_SCALE = HEAD_DIM ** -0.5


def _headnorm_q(q, g):
    return _rms(q, g) * ATTN_SCALE


def _qk_prep_fwd(q_hm, k_hm, gq, gk):
    Hh, S, P = q_hm.shape
    tq = _pick(S, (512, 256))

    def body(q_ref, k_ref, gq_ref, gk_ref, qo_ref, ko_ref):
        qo_ref[...] = _headnorm_q(q_ref[...], gq_ref[...]).astype(BF16)
        ko_ref[...] = _rms(k_ref[...], gk_ref[...]).astype(BF16)

    blk = pl.BlockSpec((None, tq, P), lambda h, i: (h, i, 0))
    vec = pl.BlockSpec((1, P), lambda h, i: (0, 0))
    return pl.pallas_call(
        body, grid=(Hh, S // tq), in_specs=[blk, blk, vec, vec], out_specs=[blk, blk],
        out_shape=[jax.ShapeDtypeStruct((Hh, S, P), BF16)] * 2,
        compiler_params=_params(("parallel", "parallel")), name="qk_prep_fwd")(q_hm, k_hm, gq, gk)


def _qk_prep_bwd(q_hm, k_hm, gq, gk, dqs, dkn):
    Hh, S, P = q_hm.shape
    tq = _pick(S, (512, 256))

    def body(q_ref, k_ref, gq_ref, gk_ref, dqs_ref, dkn_ref, dq_ref, dk_ref, dgq_ref, dgk_ref):
        @pl.when((pl.program_id(0) == 0) & (pl.program_id(1) == 0))
        def _():
            dgq_ref[...] = jnp.zeros_like(dgq_ref)
            dgk_ref[...] = jnp.zeros_like(dgk_ref)

        _, vq = jax.vjp(_headnorm_q, q_ref[...], gq_ref[...])
        dq, dgq = vq(dqs_ref[...])
        _, vk = jax.vjp(_rms, k_ref[...], gk_ref[...])
        dk, dgk = vk(dkn_ref[...])
        dq_ref[...] = dq.astype(dq_ref.dtype)
        dk_ref[...] = dk.astype(dk_ref.dtype)
        dgq_ref[...] += dgq
        dgk_ref[...] += dgk

    blk = pl.BlockSpec((None, tq, P), lambda h, i: (h, i, 0))
    vec = pl.BlockSpec((1, P), lambda h, i: (0, 0))
    return pl.pallas_call(
        body, grid=(Hh, S // tq), in_specs=[blk, blk, vec, vec, blk, blk], out_specs=[blk, blk, vec, vec],
        out_shape=[jax.ShapeDtypeStruct((Hh, S, P), BF16)] * 2 + [jax.ShapeDtypeStruct((1, P), F32)] * 2,
        compiler_params=_params(("arbitrary", "arbitrary")), name="qk_prep_bwd")(q_hm, k_hm, gq, gk, dqs, dkn)


def _logf_cumsum_fwd(f_raw, f_bias):
    S, Hh = f_raw.shape
    L = CHUNK

    def body(f_ref, b_ref, o_ref):
        ri = lax.broadcasted_iota(jnp.int32, (L, L), 0)
        ci = lax.broadcasted_iota(jnp.int32, (L, L), 1)
        tril = (ri >= ci).astype(F32)
        carry = jnp.zeros((1, Hh), F32)
        for c in range(S // L):
            lf = -_softplus(-(f_ref[c * L:(c + 1) * L, :] + b_ref[...]))
            cum = _dot32(tril, lf) + carry
            o_ref[c * L:(c + 1) * L, :] = cum
            carry = cum[L - 1:L, :]

    return pl.pallas_call(body, out_shape=jax.ShapeDtypeStruct((S, Hh), F32), name="logf_cumsum_fwd")(f_raw, f_bias)


def _logf_cumsum_bwd(f_raw, f_bias, dcum):
    S, Hh = f_raw.shape
    L = CHUNK

    def body(f_ref, b_ref, d_ref, df_ref, db_ref):
        ri = lax.broadcasted_iota(jnp.int32, (L, L), 0)
        ci = lax.broadcasted_iota(jnp.int32, (L, L), 1)
        triu = (ri <= ci).astype(F32)
        carry = jnp.zeros((1, Hh), F32)
        db = jnp.zeros((1, Hh), F32)
        for c in reversed(range(S // L)):
            suf = _dot32(triu, d_ref[c * L:(c + 1) * L, :]) + carry
            df = suf * jax.nn.sigmoid(-(f_ref[c * L:(c + 1) * L, :] + b_ref[...]))
            df_ref[c * L:(c + 1) * L, :] = df
            db = db + jnp.sum(df, axis=0, keepdims=True)
            carry = suf[0:1, :]
        db_ref[...] = db

    return pl.pallas_call(
        body, out_shape=[jax.ShapeDtypeStruct((S, Hh), F32), jax.ShapeDtypeStruct((1, Hh), F32)],
        name="logf_cumsum_bwd")(f_raw, f_bias, dcum)


_NT = (((1,), (1,)), ((), ()))
_TN = (((0,), (0,)), ((), ()))


def _mxu(a, b, dims=(((1,), (0,)), ((), ()))):
    return lax.dot_general(a, b, dims, preferred_element_type=F32)


def _flash_fwd(qs, kn, v, cq, ck):
    Hh, S, P = qs.shape
    t = _pick(S, (256,))

    def body(q_ref, k_ref, v_ref, cq_ref, ck_ref, o_ref, of_ref, lse_ref):
        i = pl.program_id(1)
        q = q_ref[...]
        cq_t = cq_ref[...]

        def step(j, carry, masked):
            m, l, acc, rem = carry
            off = pl.multiple_of(j * t, t)
            k = k_ref[pl.ds(off, t), :]
            vv = v_ref[pl.ds(off, t), :]
            s = _mxu(q, k, _NT) + cq_t - ck_ref[:, pl.ds(off, t)]
            if masked:
                ri = lax.broadcasted_iota(jnp.int32, (t, t), 0)
                ci = lax.broadcasted_iota(jnp.int32, (t, t), 1)
                s = jnp.where(ri >= ci, s, -1e30)
            m_new = jnp.maximum(m, jnp.max(s, axis=-1, keepdims=True))
            alpha = jnp.exp(m - m_new)
            p = jnp.exp(s - m_new)
            l = alpha * l + jnp.sum(p, axis=-1, keepdims=True)
            p_hi = p.astype(BF16)
            acc = alpha * acc + _mxu(p_hi, vv)
            rem = alpha * rem + _mxu((p - p_hi.astype(F32)).astype(BF16), vv)
            return m_new, l, acc, rem

        init = (jnp.full((t, 1), -1e30, F32), jnp.zeros((t, 1), F32), jnp.zeros((t, P), F32), jnp.zeros((t, P), F32))
        carry = lax.fori_loop(0, i, lambda j, c: step(j, c, False), init)
        m, l, acc, rem = step(i, carry, True)
        o_ref[...] = acc / l
        of_ref[...] = (acc + rem) / l
        lse_ref[...] = m + jnp.log(l)

    qblk = pl.BlockSpec((None, t, P), lambda h, i: (h, i, 0))
    full = pl.BlockSpec((None, S, P), lambda h, i: (h, 0, 0))
    return pl.pallas_call(
        body, grid=(Hh, S // t),
        in_specs=[qblk, full, full, pl.BlockSpec((None, t, 1), lambda h, i: (h, i, 0)),
                  pl.BlockSpec((None, 1, S), lambda h, i: (h, 0, 0))],
        out_specs=[qblk, qblk, pl.BlockSpec((None, t, 1), lambda h, i: (h, i, 0))],
        out_shape=[jax.ShapeDtypeStruct((Hh, S, P), F32), jax.ShapeDtypeStruct((Hh, S, P), F32),
                   jax.ShapeDtypeStruct((Hh, S, 1), F32)],
        compiler_params=_params(("parallel", "parallel")), name="flash_fwd")(qs, kn, v, cq, ck)


def _flash_bwd(qs, kn, v, cq, ck, o_fine, do, lse):
    Hh, S, P = qs.shape
    t = _pick(S, (256,))
    nq = S // t

    def body(q_ref, k_ref, v_ref, cq_ref, ck_ref, o_ref, do_ref, lse_ref, dq_ref, dk_ref, dv_ref, dck_ref):
        j = pl.program_id(1)

        @pl.when(j == 0)
        def _():
            dq_ref[...] = jnp.zeros_like(dq_ref)

        k = k_ref[...]
        vv = v_ref[...]
        ck_t = ck_ref[...]

        def step(i, carry, masked):
            dk, dv, dck = carry
            off = pl.multiple_of(i * t, t)
            rows = pl.ds(off, t)
            q = q_ref[rows, :]
            do_t = do_ref[rows, :]
            s = _mxu(q, k, _NT) + cq_ref[rows, :] - ck_t
            if masked:
                ri = lax.broadcasted_iota(jnp.int32, (t, t), 0)
                ci = lax.broadcasted_iota(jnp.int32, (t, t), 1)
                s = jnp.where(ri >= ci, s, -1e30)
            p = jnp.exp(s - lse_ref[rows, :])
            dob = do_t.astype(BF16)
            dv = dv + _mxu(p.astype(BF16), dob, _TN)
            dp = _mxu(dob, vv, _NT)
            delta = jnp.sum(dob.astype(F32) * o_ref[rows, :], axis=-1, keepdims=True)
            ds = p * (dp - delta)
            dsb = ds.astype(BF16)
            dk = dk + _mxu(dsb, q, _TN)
            dq_ref[rows, :] += _mxu(dsb, k)
            dck = dck - jnp.sum(ds, axis=0, keepdims=True)
            return dk, dv, dck

        init = (jnp.zeros((t, P), F32), jnp.zeros((t, P), F32), jnp.zeros((1, t), F32))
        carry = step(j, init, True)
        dk, dv, dck = lax.fori_loop(j + 1, nq, lambda i, c: step(i, c, False), carry)
        dk_ref[...] = dk
        dv_ref[...] = dv
        dck_ref[...] = dck

    kblk = pl.BlockSpec((None, t, P), lambda h, j: (h, j, 0))
    full = pl.BlockSpec((None, S, P), lambda h, j: (h, 0, 0))
    col = pl.BlockSpec((None, S, 1), lambda h, j: (h, 0, 0))
    rowt = pl.BlockSpec((None, 1, t), lambda h, j: (h, 0, j))
    return pl.pallas_call(
        body, grid=(Hh, nq),
        in_specs=[full, kblk, kblk, col, rowt, full, full, col],
        out_specs=[full, kblk, kblk, rowt],
        out_shape=[jax.ShapeDtypeStruct((Hh, S, P), F32)] * 3 + [jax.ShapeDtypeStruct((Hh, 1, S), F32)],
        compiler_params=_params(("parallel", "arbitrary")), name="flash_bwd")(qs, kn, v, cq, ck, o_fine, do, lse)


XATTN_SCALE = XATTN_DIM ** -0.5


def _xq_norm(q, g):
    return _rms(q, g) * XATTN_SCALE


def _xattn_fwd(xq, kv, gq, gk):
    S = xq.shape[0]
    Mm = kv.shape[0]
    Dh = XATTN_DIM
    tq = _pick(S, (512, 256))

    def body(q_ref, k_ref, v_ref, gq_ref, gk_ref, o_ref):
        qn = _xq_norm(q_ref[...], gq_ref[...]).astype(BF16)
        kn = _rms(k_ref[...], gk_ref[...]).astype(BF16)
        s = _mxu(qn, kn, _NT)
        m = jnp.max(s, axis=-1, keepdims=True)
        p = jnp.exp(s - m)
        l = jnp.sum(p, axis=-1, keepdims=True)
        o_ref[...] = (_mxu(p.astype(BF16), v_ref[...].astype(BF16)) / l).astype(o_ref.dtype)

    vec = pl.BlockSpec((1, Dh), lambda h, i: (0, 0))
    return pl.pallas_call(
        body, grid=(XATTN_HEADS, S // tq),
        in_specs=[pl.BlockSpec((tq, Dh), lambda h, i: (i, h)), pl.BlockSpec((Mm, Dh), lambda h, i: (0, h)),
                  pl.BlockSpec((Mm, Dh), lambda h, i: (0, XATTN_HEADS + h)), vec, vec],
        out_specs=pl.BlockSpec((tq, Dh), lambda h, i: (i, h)),
        out_shape=jax.ShapeDtypeStruct((S, XATTN_HEADS * Dh), BF16),
        compiler_params=_params(("parallel", "parallel")), name="xattn_fwd")(xq, kv, kv, gq, gk)


def _xattn_bwd(xq, kv, gq, gk, do):
    S = xq.shape[0]
    Mm = kv.shape[0]
    Dh = XATTN_DIM
    tq = _pick(S, (512, 256))
    nq = S // tq

    def body(q_ref, k_ref, v_ref, gq_ref, gk_ref, do_ref, dq_ref, dk_ref, dv_ref, dgq_ref, dgk_ref, dkn_acc, dv_acc):
        h = pl.program_id(0)
        i = pl.program_id(1)

        @pl.when((h == 0) & (i == 0))
        def _():
            dgq_ref[...] = jnp.zeros_like(dgq_ref)
            dgk_ref[...] = jnp.zeros_like(dgk_ref)

        @pl.when(i == 0)
        def _():
            dkn_acc[...] = jnp.zeros_like(dkn_acc)
            dv_acc[...] = jnp.zeros_like(dv_acc)

        qn32, vq = jax.vjp(_xq_norm, q_ref[...], gq_ref[...])
        kn32, vk = jax.vjp(_rms, k_ref[...], gk_ref[...])
        qn = qn32.astype(BF16)
        kn = kn32.astype(BF16)
        vb = v_ref[...].astype(BF16)
        s = _mxu(qn, kn, _NT)
        m = jnp.max(s, axis=-1, keepdims=True)
        p = jnp.exp(s - m)
        p = p / jnp.sum(p, axis=-1, keepdims=True)
        dob = do_ref[...].astype(BF16)
        dp = _mxu(dob, vb, _NT)
        delta = jnp.sum(p * dp, axis=-1, keepdims=True)
        ds = (p * (dp - delta)).astype(BF16)
        dv_acc[...] += _mxu(p.astype(BF16), dob, _TN)
        dkn_acc[...] += _mxu(ds, qn, _TN)
        dq, dgq = vq(_mxu(ds, kn))
        dq_ref[...] = dq.astype(dq_ref.dtype)
        dgq_ref[...] += dgq

        @pl.when(i == nq - 1)
        def _():
            dk, dgk = vk(dkn_acc[...])
            dk_ref[...] = dk.astype(dk_ref.dtype)
            dv_ref[...] = dv_acc[...].astype(dv_ref.dtype)
            dgk_ref[...] += dgk

    vec = pl.BlockSpec((1, Dh), lambda h, i: (0, 0))
    qblk = pl.BlockSpec((tq, Dh), lambda h, i: (i, h))
    kblk = pl.BlockSpec((Mm, Dh), lambda h, i: (0, h))
    vblk = pl.BlockSpec((Mm, Dh), lambda h, i: (0, XATTN_HEADS + h))
    return pl.pallas_call(
        body, grid=(XATTN_HEADS, nq),
        in_specs=[qblk, kblk, vblk, vec, vec, qblk],
        out_specs=[qblk, kblk, kblk, vec, vec],
        out_shape=[jax.ShapeDtypeStruct((S, XATTN_HEADS * Dh), BF16),
                   jax.ShapeDtypeStruct((Mm, XATTN_HEADS * Dh), BF16),
                   jax.ShapeDtypeStruct((Mm, XATTN_HEADS * Dh), BF16),
                   jax.ShapeDtypeStruct((1, Dh), F32), jax.ShapeDtypeStruct((1, Dh), F32)],
        scratch_shapes=[pltpu.VMEM((Mm, Dh), F32), pltpu.VMEM((Mm, Dh), F32)],
        compiler_params=_params(("arbitrary", "arbitrary")), name="xattn_bwd")(xq, kv, kv, gq, gk, do)


def _loss_head(y, target):
    S, D = y.shape
    tr = _pick(S, (512, 256))

    def body(y_ref, t_ref, dy_ref, loss_ref):
        @pl.when(pl.program_id(0) == 0)
        def _():
            loss_ref[...] = jnp.zeros_like(loss_ref)

        err = y_ref[...] - t_ref[...]
        dy_ref[...] = err * (1.0 / D)
        loss_ref[...] += jnp.sum(err * err) * (0.5 / D)

    row = pl.BlockSpec((tr, D), lambda i: (i, 0))
    return pl.pallas_call(
        body, grid=(S // tr,), in_specs=[row, row],
        out_specs=[row, pl.BlockSpec((1, LANES), lambda i: (0, 0))],
        out_shape=[jax.ShapeDtypeStruct((S, D), F32), jax.ShapeDtypeStruct((1, LANES), F32)],
        compiler_params=_params(("arbitrary",)), name="loss_head")(y, target)


def _row_tile(R, C):
    for tr in (1024, 512, 256, 128, 64, 32, 16, 8):
        if R % tr == 0 and tr * C * 4 <= (1 << 20):
            return tr
    return R


def _nsum(arrs, out_dtypes, name):
    R, C = arrs[0].shape
    tr = _row_tile(R, C)
    n = len(arrs)

    def body(*refs):
        acc = refs[0][...].astype(F32)
        for r in refs[1:n]:
            acc = acc + r[...].astype(F32)
        for o in refs[n:]:
            o[...] = acc.astype(o.dtype)

    blk = pl.BlockSpec((tr, C), lambda i: (i, 0))
    outs = pl.pallas_call(
        body, grid=(R // tr,), in_specs=[blk] * n, out_specs=[blk] * len(out_dtypes),
        out_shape=[jax.ShapeDtypeStruct((R, C), d) for d in out_dtypes],
        compiler_params=_params(("parallel",)), name=name)(*arrs)
    return outs


def _adamw(w, g, m, v, name):
    R, C = w.shape
    tr = _row_tile(R, C)
    c1 = 1.0 - ADAM_B1 ** ADAM_STEP
    c2 = 1.0 - ADAM_B2 ** ADAM_STEP

    def body(w_ref, g_ref, m_ref, v_ref, d_ref, mo_ref, vo_ref):
        g_t = g_ref[...]
        m_new = ADAM_B1 * m_ref[...] + (1.0 - ADAM_B1) * g_t
        v_new = ADAM_B2 * v_ref[...] + (1.0 - ADAM_B2) * (g_t * g_t)
        d_ref[...] = -ADAM_LR * ((m_new / c1) / (jnp.sqrt(v_new / c2) + ADAM_EPS) + ADAM_WD * w_ref[...])
        mo_ref[...] = m_new
        vo_ref[...] = v_new

    blk = pl.BlockSpec((tr, C), lambda i: (i, 0))
    return pl.pallas_call(
        body, grid=(R // tr,), in_specs=[blk] * 4, out_specs=[blk] * 3,
        out_shape=[jax.ShapeDtypeStruct((R, C), F32)] * 3,
        compiler_params=_params(("parallel",)), name=name)(w, g, m, v)


D_MODEL = 1024
SSM_INNER = SSM_HEADS * HEAD_DIM
CONV_DIM = SSM_INNER + 2 * SSM_GROUPS * SSM_STATE
ATTN_WIDTH = ATTN_HEADS * HEAD_DIM
COL_Z = 0
COL_XBC = COL_Z + SSM_INNER
COL_DT = COL_XBC + CONV_DIM
COL_Q = COL_DT + SSM_HEADS
COL_K = COL_Q + ATTN_WIDTH
COL_V = COL_K + ATTN_WIDTH
COL_F = COL_V + ATTN_WIDTH
IN_COLS = COL_F + ATTN_HEADS
IN_COLS_PAD = -(-IN_COLS // LANES) * LANES


def _to_heads(a):
    S = a.shape[0]
    return a.reshape(S, -1, HEAD_DIM).transpose(1, 0, 2)


def _from_heads(a):
    return a.transpose(1, 0, 2).reshape(a.shape[1], -1)


def _add_residual(acc, res):
    return (res + acc,)


def _relu2(acc):
    r = jnp.maximum(acc, 0.0)
    return acc, r * r


def _relu2_bwd(acc, a):
    return (acc * (2.0 * jnp.maximum(a, 0.0)),)


def _layer_fwd_bwd(x, mem, target, W, p):
    S = x.shape[0]
    hd3 = lambda a: a.reshape(SSM_HEADS, 1, 1)

    h1 = _rmsnorm_fwd(x, p["g_mix"], "norm_mix")
    proj = _mm(h1, W["w_in"], "nn", "in_proj")
    xbc = _conv_fwd(proj, COL_XBC, CONV_DIM, p["conv_w"], p["conv_b"])
    xs_hm = _to_heads(xbc[:, :SSM_INNER])
    z_hm = _to_heads(proj[:, COL_Z:COL_Z + SSM_INNER])
    dt_hm = proj[:, COL_DT:COL_DT + SSM_HEADS].T[:, :, None]
    ssd_par = (hd3(p["dt_bias"]), hd3(p["a_log"]), hd3(p["d_skip"]), p["ssm_norm_w"].reshape(SSM_HEADS, 1, HEAD_DIM))
    y_hm, hs = _ssd_fwd(xs_hm, xbc, z_hm, dt_hm, *ssd_par)
    q_hm = _to_heads(proj[:, COL_Q:COL_Q + ATTN_WIDTH])
    k_hm = _to_heads(proj[:, COL_K:COL_K + ATTN_WIDTH])
    v_hm = _to_heads(proj[:, COL_V:COL_V + ATTN_WIDTH]).astype(BF16)
    f_raw = proj[:, COL_F:COL_F + ATTN_HEADS]
    qs, kn = _qk_prep_fwd(q_hm, k_hm, p["g_q"], p["g_k"])
    cum = _logf_cumsum_fwd(f_raw, p["f_bias"])
    cq = cum.T[:, :, None]
    ck = cum.T[:, None, :]
    o_hm, o_fine, lse = _flash_fwd(qs, kn, v_hm, cq, ck)
    mixed = jnp.concatenate([_from_heads(y_hm), _from_heads(o_hm)], axis=-1).astype(BF16)
    x1 = _mm(mixed, W["w_out"], "nn", "out_proj", epilogue=_add_residual, extras=(x,))
    h2 = _rmsnorm_fwd(x1, p["g_xattn"], "norm_xattn")
    mem_n = _rmsnorm_fwd(mem, p["g_mem"], "norm_mem")
    xq = _mm(h2, W["xq_w"], "nn", "xq_proj")
    kv = _mm(mem_n, W["xkv_w"], "nn", "xkv_proj", b_chunks=N_CHIPS)
    xo = _xattn_fwd(xq, kv, p["xg_q"], p["xg_k"])
    x2 = _mm(xo, W["xo_w"], "nn", "xo_proj", epilogue=_add_residual, extras=(x1,))
    h3 = _rmsnorm_fwd(x2, p["g_mlp"], "norm_mlp")
    a, act = _mm(h3, W["w_up"], "nn", "mlp_up", out_dtypes=(F32, BF16), epilogue=_relu2, b_chunks=N_CHIPS)
    x3 = _mm(act, W["w_down"], "nn", "mlp_down", epilogue=_add_residual, extras=(x2,))
    dy, loss_row = _loss_head(x3, target)

    gW, gp = {}, {}
    da = _mm(dy, W["w_down"], "nt", "d_act", out_dtypes=(BF16,), epilogue=_relu2_bwd, extras=(a,))
    gW["w_down"] = _mm(act, dy, "tn", "g_w_down", out_dtypes=(BF16,))
    gW["w_up"] = _mm(h3, da, "tn", "g_w_up", out_dtypes=(BF16,), out_chunks=N_CHIPS)
    dh3 = _mm(da, W["w_up"], "nt", "d_h3", b_chunks=N_CHIPS)
    dx2, gp["g_mlp"] = _rmsnorm_bwd(x2, p["g_mlp"], dh3, dy, "norm_mlp_bwd")
    dxo = _mm(dx2, W["xo_w"], "nt", "d_xo", out_dtypes=(BF16,))
    gW["xo_w"] = _mm(xo, dx2, "tn", "g_xo_w", out_dtypes=(BF16,))
    dxq, dk_x, dv_x, gp["xg_q"], gp["xg_k"] = _xattn_bwd(xq, kv, p["xg_q"], p["xg_k"], dxo)
    dkv = jnp.concatenate([dk_x, dv_x], axis=-1)
    gW["xq_w"] = _mm(h2, dxq, "tn", "g_xq_w", out_dtypes=(BF16,))
    dh2 = _mm(dxq, W["xq_w"], "nt", "d_h2")
    gW["xkv_w"] = _mm(mem_n, dkv, "tn", "g_xkv_w", out_dtypes=(BF16,), out_chunks=N_CHIPS)
    dmem_n = _mm(dkv, W["xkv_w"], "nt", "d_mem_n", b_chunks=N_CHIPS)
    _, gp["g_mem"] = _rmsnorm_bwd(mem, p["g_mem"], dmem_n, None, "norm_mem_bwd")
    dx1, gp["g_xattn"] = _rmsnorm_bwd(x1, p["g_xattn"], dh2, dx2, "norm_xattn_bwd")
    dmixed = _mm(dx1, W["w_out"], "nt", "d_mixed")
    gW["w_out"] = _mm(mixed, dx1, "tn", "g_w_out", out_dtypes=(BF16,))
    dy_hm = _to_heads(dmixed[:, :SSM_INNER])
    do_hm = _to_heads(dmixed[:, SSM_INNER:])
    dqs, dkn, dv_hm, dck = _flash_bwd(qs, kn, v_hm, cq, ck, o_fine, do_hm, lse)
    dq_raw, dk_raw, gp["g_q"], gp["g_k"] = _qk_prep_bwd(q_hm, k_hm, p["g_q"], p["g_k"], dqs, dkn)
    df, gp["f_bias"] = _logf_cumsum_bwd(f_raw, p["f_bias"], dck[:, 0, :].T)
    dxs_hm, dz_hm, dB, dC, ddt, ddtb, dalog, ddsk, dnw = _ssd_bwd(xs_hm, xbc, z_hm, dt_hm, *ssd_par, hs, dy_hm)
    gp["dt_bias"] = ddtb.reshape(1, SSM_HEADS)
    gp["a_log"] = dalog.reshape(1, SSM_HEADS)
    gp["d_skip"] = ddsk.reshape(1, SSM_HEADS)
    gp["ssm_norm_w"] = dnw.reshape(1, SSM_INNER)
    dxbc = jnp.concatenate([_from_heads(dxs_hm), dB, dC], axis=-1)
    dxbc_raw, dconv_w, gp["conv_b"] = _conv_bwd(proj, COL_XBC, CONV_DIM, p["conv_w"], p["conv_b"], dxbc)
    gp["conv_w"] = dconv_w[:CONV_WIDTH]
    dproj = jnp.concatenate(
        [_from_heads(dz_hm).astype(BF16), dxbc_raw, ddt[:, :, 0].T.astype(BF16), _from_heads(dq_raw),
         _from_heads(dk_raw), _from_heads(dv_hm).astype(BF16), df.astype(BF16),
         jnp.zeros((S, IN_COLS_PAD - IN_COLS), BF16)], axis=-1)
    gW["w_in"] = _mm(h1, dproj, "tn", "g_w_in", out_dtypes=(BF16,))
    dh1 = _mm(dproj, W["w_in"], "nt", "d_h1")
    dx, gp["g_mix"] = _rmsnorm_bwd(x, p["g_mix"], dh1, dx1, "norm_mix_bwd")
    return loss_row, dx, gW, gp


_ANY = pl.BlockSpec(memory_space=pl.ANY)


def _place():
    x, y, c = lax.axis_index("x"), lax.axis_index("y"), lax.axis_index("c")
    chips = [(1 - x, y), (x, 1 - y), (1 - x, 1 - y)]
    return x, y, c, chips


def _chip_index(px, py):
    return 2 * px + py


def _all_gather_chips(split, whole):
    ns, nw = len(split), len(whole)
    n = ns + nw

    def body(*refs):
        ins, outs = refs[:n], refs[n:2 * n]
        send_ici, recv_ici, send_d2d, recv_d2d, local_sem = refs[2 * n:]
        x, y, c, chips = _place()
        me = _chip_index(x, y)
        sib = (x, y, 1 - c)

        def ici(k, j, src, dst):
            return pltpu.make_async_remote_copy(src_ref=src, dst_ref=dst, send_sem=send_ici.at[3 * k + j],
                                                recv_sem=recv_ici.at[3 * k + j], device_id=(*chips[j], c),
                                                device_id_type=MESH)

        def d2d(k, j, piece):
            return pltpu.make_async_remote_copy(src_ref=piece, dst_ref=piece, send_sem=send_d2d.at[3 * k + j],
                                                recv_sem=recv_d2d.at[3 * k + j], device_id=sib, device_id_type=MESH)

        local = [pltpu.make_async_copy(ins[k], outs[k].at[me], local_sem.at[k]) for k in range(n)]
        for cp in local:
            cp.start()
        sends = []
        for k in range(n):
            for j in range(3):
                if k < ns:
                    sends.append(ici(k, j, ins[k].at[c], outs[k].at[me, c]))
                else:
                    sends.append(ici(k, j, ins[k], outs[k].at[me]))
                sends[-1].start()
        passed = []
        for k in range(n):
            for j in range(3):
                src_chip = _chip_index(*chips[j])
                if k < ns:
                    ici(k, j, ins[k].at[c], outs[k].at[src_chip, c]).wait_recv()
                    passed.append(d2d(k, j, outs[k].at[src_chip, c]))
                    passed[-1].start()
                else:
                    ici(k, j, ins[k], outs[k].at[src_chip]).wait_recv()
        for k in range(ns):
            for j in range(3):
                d2d(k, j, outs[k].at[_chip_index(*chips[j]), 1 - c]).wait_recv()
        for cp in sends + passed:
            cp.wait_send()
        for cp in local:
            cp.wait()

    arrs = list(split) + list(whole)
    return pl.pallas_call(
        body, in_specs=[_ANY] * n, out_specs=[_ANY] * n,
        out_shape=[jax.ShapeDtypeStruct((N_CHIPS,) + a.shape, a.dtype) for a in arrs],
        scratch_shapes=[pltpu.SemaphoreType.DMA((3 * n,)), pltpu.SemaphoreType.DMA((3 * n,)),
                        pltpu.SemaphoreType.DMA((3 * ns,)), pltpu.SemaphoreType.DMA((3 * ns,)),
                        pltpu.SemaphoreType.DMA((n,))],
        name="all_gather_chips")(*arrs)


def _sibling_send_halves(grads):
    n = len(grads)

    def body(*refs):
        ins, outs = refs[:n], refs[n:2 * n]
        send_sem, recv_sem = refs[2 * n:]
        x, y, c, _ = _place()

        def cp(k, j, half):
            return pltpu.make_async_remote_copy(src_ref=ins[k].at[j, half], dst_ref=outs[k].at[j],
                                                send_sem=send_sem.at[N_CHIPS * k + j],
                                                recv_sem=recv_sem.at[N_CHIPS * k + j],
                                                device_id=(x, y, 1 - c), device_id_type=MESH)

        copies = [cp(k, j, 1 - c) for k in range(n) for j in range(N_CHIPS)]
        for q in copies:
            q.start()
        for q in copies:
            q.wait()

    return pl.pallas_call(
        body, in_specs=[_ANY] * n, out_specs=[_ANY] * n,
        out_shape=[jax.ShapeDtypeStruct((N_CHIPS,) + g.shape[2:], g.dtype) for g in grads],
        scratch_shapes=[pltpu.SemaphoreType.DMA((N_CHIPS * n,)), pltpu.SemaphoreType.DMA((N_CHIPS * n,))],
        name="rs_sibling_halves")(*grads)


def _chips_send_shards(parts):
    n = len(parts)

    def body(*refs):
        ins, outs = refs[:n], refs[n:2 * n]
        send_sem, recv_sem = refs[2 * n:]
        x, y, c, chips = _place()

        def cp(k, j):
            return pltpu.make_async_remote_copy(src_ref=ins[k].at[_chip_index(*chips[j])], dst_ref=outs[k].at[j],
                                                send_sem=send_sem.at[3 * k + j], recv_sem=recv_sem.at[3 * k + j],
                                                device_id=(*chips[j], c), device_id_type=MESH)

        copies = [cp(k, j) for k in range(n) for j in range(3)]
        for q in copies:
            q.start()
        for q in copies:
            q.wait()

    return pl.pallas_call(
        body, in_specs=[_ANY] * n, out_specs=[_ANY] * n,
        out_shape=[jax.ShapeDtypeStruct((3,) + g.shape[1:], g.dtype) for g in parts],
        scratch_shapes=[pltpu.SemaphoreType.DMA((3 * n,)), pltpu.SemaphoreType.DMA((3 * n,))],
        name="rs_chip_shards")(*parts)


def _sibling_exchange(halves):
    n = len(halves)

    def body(*refs):
        ins, outs = refs[:n], refs[n:2 * n]
        send_sem, recv_sem, local_sem = refs[2 * n:]
        x, y, c, _ = _place()
        local = [pltpu.make_async_copy(ins[k], outs[k].at[c], local_sem.at[k]) for k in range(n)]
        for q in local:
            q.start()

        def cp(k, half):
            return pltpu.make_async_remote_copy(src_ref=ins[k], dst_ref=outs[k].at[half], send_sem=send_sem.at[k],
                                                recv_sem=recv_sem.at[k], device_id=(x, y, 1 - c), device_id_type=MESH)

        sends = [cp(k, c) for k in range(n)]
        for q in sends:
            q.start()
        for k in range(n):
            cp(k, 1 - c).wait_recv()
        for q in sends:
            q.wait_send()
        for q in local:
            q.wait()

    return pl.pallas_call(
        body, in_specs=[_ANY] * n, out_specs=[_ANY] * n,
        out_shape=[jax.ShapeDtypeStruct((2,) + h.shape, h.dtype) for h in halves],
        scratch_shapes=[pltpu.SemaphoreType.DMA((n,)), pltpu.SemaphoreType.DMA((n,)), pltpu.SemaphoreType.DMA((n,))],
        name="rs_sibling_exchange")(*halves)


def _all_reduce_small(vec):
    R = vec.shape[0]

    def body(v_ref, o_ref, buf, send_sem, recv_sem):
        x, y, c = lax.axis_index("x"), lax.axis_index("y"), lax.axis_index("c")
        me = 4 * x + 2 * y + c
        buf[me] = v_ref[...]
        copies = []
        for r in range(1, N_DEV):
            fx, fy, fc = (r >> 2) & 1, (r >> 1) & 1, r & 1
            peer = (x ^ fx, y ^ fy, c ^ fc)
            copies.append(pltpu.make_async_remote_copy(src_ref=v_ref, dst_ref=buf.at[me], send_sem=send_sem.at[r - 1],
                                                       recv_sem=recv_sem.at[r - 1], device_id=peer, device_id_type=MESH))
        for q in copies:
            q.start()
        for r in range(1, N_DEV):
            fx, fy, fc = (r >> 2) & 1, (r >> 1) & 1, r & 1
            src = 4 * (x ^ fx) + 2 * (y ^ fy) + (c ^ fc)
            pltpu.make_async_remote_copy(src_ref=v_ref, dst_ref=buf.at[src], send_sem=send_sem.at[r - 1],
                                         recv_sem=recv_sem.at[r - 1], device_id=(x, y, c), device_id_type=MESH).wait_recv()
        acc = buf[0]
        for d in range(1, N_DEV):
            acc = acc + buf[d]
        o_ref[...] = acc
        for q in copies:
            q.wait_send()

    vm = pl.BlockSpec(memory_space=pltpu.VMEM)
    return pl.pallas_call(
        body, in_specs=[vm], out_specs=vm, out_shape=jax.ShapeDtypeStruct((R, LANES), F32),
        scratch_shapes=[pltpu.VMEM((N_DEV, R, LANES), F32), pltpu.SemaphoreType.DMA((N_DEV - 1,)),
                        pltpu.SemaphoreType.DMA((N_DEV - 1,))],
        name="all_reduce_small")(vec)


_INPUTS = ["x", "mem", "g_mix", "w_in", "conv_w", "conv_b", "dt_bias", "a_log", "d_skip", "ssm_norm_w", "g_q", "g_k",
           "f_bias", "w_out", "g_xattn", "g_mem", "xq_w", "xkv_w", "xg_q", "xg_k", "xo_w", "g_mlp", "w_up", "w_down"]
_WEIGHTS = _INPUTS[2:]
_BIG = ["w_in", "w_out", "xq_w", "xkv_w", "xo_w", "w_up", "w_down"]
_COL_SHARDED = ["w_in", "xkv_w", "w_up"]
_SMALL = [n for n in _WEIGHTS if n not in _BIG]


def _pack_rows(arrs):
    rows = []
    for a in arrs:
        flat = a.reshape(-1)
        pad = -flat.shape[0] % LANES
        rows.append(jnp.pad(flat, (0, pad)).reshape(-1, LANES))
    out = jnp.concatenate(rows, axis=0)
    return jnp.pad(out, ((0, -out.shape[0] % 8), (0, 0)))


def _unpack_rows(packed, shapes):
    out, r = [], 0
    for s in shapes:
        n = math.prod(s)
        nr = -(-n // LANES)
        out.append(packed[r:r + nr].reshape(-1)[:n].reshape(s))
        r += nr
    return out


def kernel(x, mem, g_mix, w_in, conv_w, conv_b, dt_bias, a_log, d_skip, ssm_norm_w, g_q, g_k, f_bias, w_out, g_xattn, g_mem, xq_w, xkv_w, xg_q, xg_k, xo_w, g_mlp, w_up, w_down, loss_target, m_g_mix, m_w_in, m_conv_w, m_conv_b, m_dt_bias, m_a_log, m_d_skip, m_ssm_norm_w, m_g_q, m_g_k, m_f_bias, m_w_out, m_g_xattn, m_g_mem, m_xq_w, m_xkv_w, m_xg_q, m_xg_k, m_xo_w, m_g_mlp, m_w_up, m_w_down, v_g_mix, v_w_in, v_conv_w, v_conv_b, v_dt_bias, v_a_log, v_d_skip, v_ssm_norm_w, v_g_q, v_g_k, v_f_bias, v_w_out, v_g_xattn, v_g_mem, v_xq_w, v_xkv_w, v_xg_q, v_xg_k, v_xo_w, v_g_mlp, v_w_up, v_w_down):
    args = (x, mem, g_mix, w_in, conv_w, conv_b, dt_bias, a_log, d_skip, ssm_norm_w, g_q, g_k, f_bias, w_out, g_xattn,
            g_mem, xq_w, xkv_w, xg_q, xg_k, xo_w, g_mlp, w_up, w_down)
    w = dict(zip(_INPUTS, args))
    mom1 = dict(zip(_WEIGHTS, (m_g_mix, m_w_in, m_conv_w, m_conv_b, m_dt_bias, m_a_log, m_d_skip, m_ssm_norm_w, m_g_q,
                               m_g_k, m_f_bias, m_w_out, m_g_xattn, m_g_mem, m_xq_w, m_xkv_w, m_xg_q, m_xg_k, m_xo_w,
                               m_g_mlp, m_w_up, m_w_down)))
    mom2 = dict(zip(_WEIGHTS, (v_g_mix, v_w_in, v_conv_w, v_conv_b, v_dt_bias, v_a_log, v_d_skip, v_ssm_norm_w, v_g_q,
                               v_g_k, v_f_bias, v_w_out, v_g_xattn, v_g_mem, v_xq_w, v_xkv_w, v_xg_q, v_xg_k, v_xo_w,
                               v_g_mlp, v_w_up, v_w_down)))
    chip = _chip_index(lax.axis_index("x"), lax.axis_index("y"))
    core = lax.axis_index("c")

    shards = [w[n][0] for n in _BIG]
    halves = [s.astype(BF16).reshape(2, s.shape[0] // 2, s.shape[1]) for s in shards]
    gathered = _all_gather_chips(halves, [w["conv_w"][0]])
    full = dict(zip(_BIG, gathered[:len(_BIG)]))
    W = {}
    for n in _BIG:
        g = full[n]
        g = g.reshape(N_CHIPS, 2 * g.shape[2], g.shape[3])
        if n == "w_in":
            g = g.transpose(1, 0, 2).reshape(g.shape[1], IN_COLS)
            W[n] = jnp.pad(g, ((0, 0), (0, IN_COLS_PAD - IN_COLS)))
        elif n in _COL_SHARDED:
            W[n] = g
        else:
            W[n] = g.reshape(N_CHIPS * g.shape[1], g.shape[2])
    conv_w_full = gathered[-1].transpose(1, 0, 2).reshape(CONV_WIDTH, CONV_DIM)
    p = {n: w[n] for n in _SMALL}
    p["conv_w"] = conv_w_full

    loss_row, dx, gW, gp = _layer_fwd_bwd(x[0], mem[0], loss_target[0], W, p)

    grads4 = []
    for n, s in zip(_BIG, shards):
        g = gW[n]
        if n == "w_in":
            g = g[:, :IN_COLS].reshape(g.shape[0], N_CHIPS, IN_COLS // N_CHIPS).transpose(1, 0, 2)
        elif n not in _COL_SHARDED:
            g = g.reshape(N_CHIPS, g.shape[0] // N_CHIPS, g.shape[1])
        grads4.append(g.reshape(N_CHIPS, 2, g.shape[1] // 2, g.shape[2]))
    from_sibling = _sibling_send_halves(grads4)
    pair_sums = []
    for k, g in enumerate(grads4):
        mine = lax.dynamic_index_in_dim(g, core, axis=1, keepdims=False)
        flat = lambda a: a.reshape(-1, a.shape[-1])
        (s,) = _nsum([flat(mine), flat(from_sibling[k])], (BF16,), "rs_pair_sum_" + _BIG[k])
        pair_sums.append(s.reshape(mine.shape))
    from_chips = _chips_send_shards(pair_sums)
    reduced = []
    for k, ps in enumerate(pair_sums):
        own = lax.dynamic_index_in_dim(ps, chip, axis=0, keepdims=False)
        (r,) = _nsum([own, from_chips[k][0], from_chips[k][1], from_chips[k][2]], (F32,), "rs_chip_sum_" + _BIG[k])
        reduced.append(r)
    grad_shards = _sibling_exchange(reduced)

    small_shapes = [gp[n].shape for n in _SMALL] + [(1, LANES)]
    packed = _pack_rows([gp[n] for n in _SMALL] + [loss_row])
    summed = _unpack_rows(_all_reduce_small(packed), small_shapes)
    gsmall = dict(zip(_SMALL, summed[:-1]))
    loss = summed[-1][0, 0]
    shard_cols = CONV_DIM // N_CHIPS
    gsmall["conv_w"] = lax.dynamic_slice_in_dim(gsmall["conv_w"], chip * shard_cols, shard_cols, axis=1)

    grad, delta, new_m, new_v = {}, {}, {}, {}
    for k, n in enumerate(_BIG):
        shape = w[n].shape
        g2 = grad_shards[k].reshape(shape[1], shape[2])
        d, m1, v1 = _adamw(w[n][0], g2, mom1[n][0], mom2[n][0], "adamw_" + n)
        grad[n], delta[n], new_m[n], new_v[n] = (a.reshape(shape) for a in (g2, d, m1, v1))
    pk = lambda src: _pack_rows([src[n] for n in _SMALL])
    for n in _SMALL:
        gsmall[n] = gsmall[n].reshape(w[n].shape)
    d, m1, v1 = _adamw(pk(w), pk(gsmall), pk(mom1), pk(mom2), "adamw_small")
    shapes = [w[n].shape for n in _SMALL]
    for n, dn, mn, vn in zip(_SMALL, _unpack_rows(d, shapes), _unpack_rows(m1, shapes), _unpack_rows(v1, shapes)):
        grad[n], delta[n], new_m[n], new_v[n] = gsmall[n], dn, mn, vn

    return (loss, dx[None], *[grad[n] for n in _WEIGHTS], *[delta[n] for n in _WEIGHTS],
            *[new_m[n] for n in _WEIGHTS], *[new_v[n] for n in _WEIGHTS])
```

```python
import functools
import math

import jax
import jax.numpy as jnp
from jax import lax
from jax.experimental import pallas as pl
from jax.experimental.pallas import tpu as pltpu

F32 = jnp.float32
BF16 = jnp.bfloat16
HI = lax.Precision.HIGHEST
MESH = pl.DeviceIdType.MESH

EPS = 1e-5
CHUNK = 128
SSM_HEADS = 16
SSM_GROUPS = 2
HEADS_PER_GROUP = SSM_HEADS // SSM_GROUPS
HEAD_DIM = 64
SSM_STATE = 128
ATTN_HEADS = 16
XATTN_HEADS = 4
XATTN_DIM = 256
CONV_WIDTH = 4
N_CHIPS = 4
N_DEV = 8
LANES = 128
VMEM_LIMIT = 56 * 1024 * 1024

ADAM_LR = 0.001
ADAM_B1 = 0.9
ADAM_B2 = 0.999
ADAM_EPS = 1e-08
ADAM_WD = 0.01
ADAM_STEP = 10


def _params(sem):
    return pltpu.CompilerParams(dimension_semantics=sem, vmem_limit_bytes=VMEM_LIMIT)


def _pick(n, cands):
    for c in cands:
        if n % c == 0:
            return c
    return n


def _mm(a, b, mode, name, out_dtypes=(F32,), epilogue=None, extras=(), b_chunks=1, out_chunks=1,
        tm=None, tn=None, tk=None):
    if mode == "nn":
        M, K = a.shape
        N = b.shape[-1] * b_chunks
    elif mode == "nt":
        M, K = a.shape
        N = b.shape[-2]
        assert b.shape[-1] * b_chunks == K
    else:
        K, M = a.shape
        N = b.shape[-1] * b_chunks
    tm = tm or _pick(M, (512, 256, 128))
    tn = tn or _pick(N // max(b_chunks if mode != "nt" else 1, out_chunks), (512, 1152, 640, 384, 256, 128))
    if tk is None:
        kmax = b.shape[-1] if mode == "nt" else K
        tk = kmax if kmax <= 2048 else _pick(kmax, (2048, 1152, 1024, 512))
    nk = K // tk
    assert M % tm == 0 and N % tn == 0 and K % tk == 0
    grid = (M // tm, N // tn, nk)

    if mode == "tn":
        a_spec = pl.BlockSpec((tk, tm), lambda i, j, k: (k, i))
    else:
        a_spec = pl.BlockSpec((tm, tk), lambda i, j, k: (i, k))

    def b_index(t_row, t_last, tile_last):
        if b_chunks == 1:
            return (t_row, t_last)
        q = (b.shape[-1]) // tile_last
        return (t_last // q, t_row, t_last % q)

    if mode == "nn" or mode == "tn":
        bshape = (tk, tn)
        bmap = lambda i, j, k: b_index(k, j, tn)
    else:
        bshape = (tn, tk)
        bmap = lambda i, j, k: b_index(j, k, tk)
    if b_chunks > 1:
        bshape = (None,) + bshape
    b_spec = pl.BlockSpec(bshape, bmap)

    if out_chunks == 1:
        o_spec = pl.BlockSpec((tm, tn), lambda i, j, k: (i, j))
        o_shape = (M, N)
    else:
        qo = (N // out_chunks) // tn
        o_spec = pl.BlockSpec((None, tm, tn), lambda i, j, k: (j // qo, i, j % qo))
        o_shape = (out_chunks, M, N // out_chunks)
    e_spec = pl.BlockSpec((tm, tn), lambda i, j, k: (i, j))

    dims = {"nn": (((1,), (0,)), ((), ())), "nt": (((1,), (1,)), ((), ())), "tn": (((0,), (0,)), ((), ()))}[mode]
    n_ex = len(extras)
    n_out = len(out_dtypes)

    def body(*refs):
        a_ref, b_ref = refs[0], refs[1]
        ex_refs = refs[2:2 + n_ex]
        o_refs = refs[2 + n_ex:2 + n_ex + n_out]

        def finish(acc):
            outs = epilogue(acc, *[r[...] for r in ex_refs]) if epilogue is not None else (acc,)
            for r, o in zip(o_refs, outs):
                r[...] = o.astype(r.dtype)

        part = lax.dot_general(a_ref[...].astype(BF16), b_ref[...].astype(BF16), dims,
                               preferred_element_type=F32)
        if nk == 1:
            finish(part)
        else:
            acc_ref = refs[-1]
            k = pl.program_id(2)

            @pl.when(k == 0)
            def _():
                acc_ref[...] = part

            @pl.when(k > 0)
            def _():
                acc_ref[...] += part

            @pl.when(k == nk - 1)
            def _():
                finish(acc_ref[...])

    outs = pl.pallas_call(
        body,
        grid=grid,
        in_specs=[a_spec, b_spec] + [e_spec] * n_ex,
        out_specs=[o_spec] * n_out,
        out_shape=[jax.ShapeDtypeStruct(o_shape, d) for d in out_dtypes],
        scratch_shapes=[pltpu.VMEM((tm, tn), F32)] if nk > 1 else [],
        compiler_params=_params(("parallel", "parallel", "arbitrary")),
        name=name,
    )(a, b, *extras)
    return outs[0] if n_out == 1 else outs


def _rms(x, g):
    r = lax.rsqrt(jnp.mean(x * x, axis=-1, keepdims=True) + EPS)
    return x * r * g


def _rmsnorm_fwd(x, g, name):
    R, D = x.shape
    tr = _pick(R, (512, 256))

    def body(x_ref, g_ref, o_ref):
        o_ref[...] = _rms(x_ref[...], g_ref[...]).astype(o_ref.dtype)

    return pl.pallas_call(
        body, grid=(R // tr,),
        in_specs=[pl.BlockSpec((tr, D), lambda i: (i, 0)), pl.BlockSpec((1, D), lambda i: (0, 0))],
        out_specs=pl.BlockSpec((tr, D), lambda i: (i, 0)),
        out_shape=jax.ShapeDtypeStruct((R, D), BF16),
        compiler_params=_params(("parallel",)), name=name)(x, g)


def _rmsnorm_bwd(x, g, dh, dres, name):
    R, D = x.shape
    tr = _pick(R, (256,))
    has_res = dres is not None

    def body(*refs):
        if has_res:
            x_ref, g_ref, dh_ref, dres_ref, dx_ref, dg_ref = refs
        else:
            x_ref, g_ref, dh_ref, dx_ref, dg_ref = refs
        _, vjp = jax.vjp(_rms, x_ref[...], g_ref[...])
        dx, dg = vjp(dh_ref[...])
        if has_res:
            dx = dx + dres_ref[...]
        dx_ref[...] = dx

        @pl.when(pl.program_id(0) == 0)
        def _():
            dg_ref[...] = jnp.zeros_like(dg_ref)

        dg_ref[...] += dg

    row = pl.BlockSpec((tr, D), lambda i: (i, 0))
    vec = pl.BlockSpec((1, D), lambda i: (0, 0))
    ins = [x, g, dh] + ([dres] if has_res else [])
    return pl.pallas_call(
        body, grid=(R // tr,),
        in_specs=[row, vec, row] + ([row] if has_res else []),
        out_specs=[row, vec],
        out_shape=[jax.ShapeDtypeStruct((R, D), F32), jax.ShapeDtypeStruct((1, D), F32)],
        compiler_params=_params(("arbitrary",)), name=name)(*ins)


def _shift_down(u, k):
    if k == 0:
        return u
    rows = lax.broadcasted_iota(jnp.int32, u.shape, 0)
    return jnp.where(rows >= k, pltpu.roll(u, k, axis=0), 0.0)


def _shift_up(u, k):
    if k == 0:
        return u
    n = u.shape[0]
    rows = lax.broadcasted_iota(jnp.int32, u.shape, 0)
    return jnp.where(rows < n - k, pltpu.roll(u, n - k, axis=0), 0.0)


def _conv_pre(u, w, b):
    pre = b
    for j in range(CONV_WIDTH):
        pre = pre + w[j:j + 1, :] * _shift_down(u, CONV_WIDTH - 1 - j)
    return pre


def _conv_fwd(proj, col0, ncols, conv_w, conv_b):
    S = proj.shape[0]
    cb0 = col0 // LANES

    def body(u_ref, w_ref, b_ref, o_ref):
        pre = _conv_pre(u_ref[...], w_ref[...], b_ref[...])
        o_ref[...] = pre * jax.nn.sigmoid(pre)

    return pl.pallas_call(
        body, grid=(ncols // LANES,),
        in_specs=[pl.BlockSpec((S, LANES), lambda j: (0, j + cb0)),
                  pl.BlockSpec((CONV_WIDTH, LANES), lambda j: (0, j)),
                  pl.BlockSpec((1, LANES), lambda j: (0, j))],
        out_specs=pl.BlockSpec((S, LANES), lambda j: (0, j)),
        out_shape=jax.ShapeDtypeStruct((S, ncols), F32),
        compiler_params=_params(("parallel",)), name="conv_fwd")(proj, conv_w, conv_b)


def _conv_bwd(proj, col0, ncols, conv_w, conv_b, dout):
    S = proj.shape[0]
    cb0 = col0 // LANES

    def body(u_ref, w_ref, b_ref, d_ref, du_ref, dw_ref, db_ref):
        u = u_ref[...]
        w = w_ref[...]
        pre = _conv_pre(u, w, b_ref[...])
        s = jax.nn.sigmoid(pre)
        dpre = d_ref[...] * (s * (1.0 + pre * (1.0 - s)))
        du = jnp.zeros_like(u)
        rows = []
        for j in range(CONV_WIDTH):
            k = CONV_WIDTH - 1 - j
            du = du + w[j:j + 1, :] * _shift_up(dpre, k)
            rows.append(jnp.sum(dpre * _shift_down(u, k), axis=0, keepdims=True))
        du_ref[...] = du.astype(du_ref.dtype)
        rows.append(jnp.zeros((8 - CONV_WIDTH, LANES), F32))
        dw_ref[...] = jnp.concatenate(rows, axis=0)
        db_ref[...] = jnp.sum(dpre, axis=0, keepdims=True)

    return pl.pallas_call(
        body, grid=(ncols // LANES,),
        in_specs=[pl.BlockSpec((S, LANES), lambda j: (0, j + cb0)),
                  pl.BlockSpec((CONV_WIDTH, LANES), lambda j: (0, j)),
                  pl.BlockSpec((1, LANES), lambda j: (0, j)),
                  pl.BlockSpec((S, LANES), lambda j: (0, j))],
        out_specs=[pl.BlockSpec((S, LANES), lambda j: (0, j)),
                   pl.BlockSpec((8, LANES), lambda j: (0, j)),
                   pl.BlockSpec((1, LANES), lambda j: (0, j))],
        out_shape=[jax.ShapeDtypeStruct((S, ncols), BF16),
                   jax.ShapeDtypeStruct((8, ncols), F32),
                   jax.ShapeDtypeStruct((1, ncols), F32)],
        compiler_params=_params(("parallel",)), name="conv_bwd")(proj, conv_w, conv_b, dout)


def _softplus(x):
    return jnp.maximum(x, 0.0) + jnp.log1p(jnp.exp(-jnp.abs(x)))


def _dot32(a, b, dims=(((1,), (0,)), ((), ()))):
    return lax.dot_general(a, b, dims, precision=HI, preferred_element_type=F32)


def _dotd(a, b, dims=(((1,), (0,)), ((), ()))):
    return lax.dot_general(a, b, dims, preferred_element_type=F32)


def _ssd_chunk(xs, Bm, Cm, z, dtr, dtb, alog, dsk, nw, h):
    L = Bm.shape[0]
    ri = lax.broadcasted_iota(jnp.int32, (L, L), 0)
    ci = lax.broadcasted_iota(jnp.int32, (L, L), 1)
    causal = ri >= ci
    tril = causal.astype(F32)
    CB = _dotd(Cm, Bm, (((1,), (1,)), ((), ())))
    gated, hnew = [], []
    ssq = jnp.zeros((L, 1), F32)
    for r in range(len(xs)):
        dt = _softplus(dtr[r] + dtb[r])
        dA = dt * (-jnp.exp(alog[r]))
        acs = _dot32(tril, dA)
        tot = jnp.sum(dA, axis=0, keepdims=True)
        cc = jnp.broadcast_to(acs, (L, L))
        seg = jnp.where(causal, cc - cc.T, -1e30)
        Lmat = jnp.exp(seg)
        X = xs[r] * dt
        y = _dotd(CB * Lmat, X) + jnp.exp(acs) * _dotd(Cm, h[r]) + dsk[r] * xs[r]
        hnew.append(jnp.exp(tot) * h[r] + _dotd(Bm, X * jnp.exp(tot - acs), (((0,), (0,)), ((), ()))))
        g = y * (z[r] * jax.nn.sigmoid(z[r]))
        ssq = ssq + jnp.sum(g * g, axis=-1, keepdims=True)
        gated.append(g)
    rs = lax.rsqrt(ssq / (len(xs) * xs[0].shape[-1]) + EPS)
    return [g * rs * nw[r] for r, g in enumerate(gated)], hnew


def _ssd_specs(S):
    H, P, N, L = HEADS_PER_GROUP, HEAD_DIM, SSM_STATE, CHUNK
    return dict(
        head=lambda rev: pl.BlockSpec((H, L, P), (lambda g, c: (g, rev(c), 0))),
        bc=lambda rev, off: pl.BlockSpec((L, N), (lambda g, c: (rev(c), off + g))),
        dt=lambda rev: pl.BlockSpec((H, L, 1), (lambda g, c: (g, rev(c), 0))),
        scal=pl.BlockSpec((H, 1, 1), lambda g, c: (g, 0, 0)),
        nw=pl.BlockSpec((H, 1, P), lambda g, c: (g, 0, 0)),
        hs=lambda rev: pl.BlockSpec((None, H, N, P), (lambda g, c: (rev(c), g, 0, 0))),
    )


def _ssd_fwd(xs_hm, xbc, z_hm, dt_hm, dtb, alog, dsk, nw_hm):
    S = xbc.shape[0]
    H, P, N, L = HEADS_PER_GROUP, HEAD_DIM, SSM_STATE, CHUNK
    nc = S // L
    sp = _ssd_specs(S)
    ident = lambda c: c
    xoff = (SSM_HEADS * HEAD_DIM) // LANES

    def body(xs_ref, b_ref, c_ref, z_ref, dt_ref, dtb_ref, al_ref, dsk_ref, nw_ref, y_ref, hs_ref, h_ref):
        @pl.when(pl.program_id(1) == 0)
        def _():
            h_ref[...] = jnp.zeros_like(h_ref)

        hs_ref[...] = h_ref[...]
        hd = range(H)
        out, hnew = _ssd_chunk([xs_ref[r] for r in hd], b_ref[...], c_ref[...], [z_ref[r] for r in hd],
                               [dt_ref[r] for r in hd], [dtb_ref[r] for r in hd], [al_ref[r] for r in hd],
                               [dsk_ref[r] for r in hd], [nw_ref[r] for r in hd], [h_ref[r] for r in hd])
        for r in hd:
            y_ref[r] = out[r]
            h_ref[r] = hnew[r]

    return pl.pallas_call(
        body, grid=(SSM_GROUPS, nc),
        in_specs=[sp["head"](ident), sp["bc"](ident, xoff), sp["bc"](ident, xoff + SSM_GROUPS), sp["head"](ident),
                  sp["dt"](ident), sp["scal"], sp["scal"], sp["scal"], sp["nw"]],
        out_specs=[sp["head"](ident), sp["hs"](ident)],
        out_shape=[jax.ShapeDtypeStruct((SSM_HEADS, S, P), F32),
                   jax.ShapeDtypeStruct((nc, SSM_HEADS, N, P), F32)],
        scratch_shapes=[pltpu.VMEM((H, N, P), F32)],
        compiler_params=_params(("parallel", "arbitrary")), name="ssd_fwd",
    )(xs_hm, xbc, xbc, z_hm, dt_hm, dtb, alog, dsk, nw_hm)


def _ssd_bwd(xs_hm, xbc, z_hm, dt_hm, dtb, alog, dsk, nw_hm, hs, dy_hm):
    S = xbc.shape[0]
    H, P, N, L = HEADS_PER_GROUP, HEAD_DIM, SSM_STATE, CHUNK
    nc = S // L
    sp = _ssd_specs(S)
    rev = lambda c: nc - 1 - c
    xoff = (SSM_HEADS * HEAD_DIM) // LANES

    def body(xs_ref, b_ref, c_ref, z_ref, dt_ref, dtb_ref, al_ref, dsk_ref, nw_ref, hs_ref, dy_ref,
             dxs_ref, dz_ref, db_ref, dc_ref, ddt_ref, ddtb_ref, dal_ref, ddsk_ref, dnw_ref, dh_ref):
        first = pl.program_id(1) == 0

        @pl.when(first)
        def _():
            dh_ref[...] = jnp.zeros_like(dh_ref)
            ddtb_ref[...] = jnp.zeros_like(ddtb_ref)
            dal_ref[...] = jnp.zeros_like(dal_ref)
            ddsk_ref[...] = jnp.zeros_like(ddsk_ref)
            dnw_ref[...] = jnp.zeros_like(dnw_ref)

        hd = range(H)
        args = ([xs_ref[r] for r in hd], b_ref[...], c_ref[...], [z_ref[r] for r in hd],
                [dt_ref[r] for r in hd], [dtb_ref[r] for r in hd], [al_ref[r] for r in hd],
                [dsk_ref[r] for r in hd], [nw_ref[r] for r in hd], [hs_ref[r] for r in hd])
        _, vjp = jax.vjp(_ssd_chunk, *args)
        dxs, dB, dC, dz, ddt, ddtb, dal, ddsk, dnw, dh = vjp(([dy_ref[r] for r in hd], [dh_ref[r] for r in hd]))
        db_ref[...] = dB
        dc_ref[...] = dC
        for r in hd:
            dxs_ref[r] = dxs[r]
            dz_ref[r] = dz[r]
            ddt_ref[r] = ddt[r]
            dh_ref[r] = dh[r]
            ddtb_ref[r] += ddtb[r]
            dal_ref[r] += dal[r]
            ddsk_ref[r] += ddsk[r]
            dnw_ref[r] += dnw[r]

    bc_out = lambda: pl.BlockSpec((L, N), lambda g, c: (rev(c), g))
    return pl.pallas_call(
        body, grid=(SSM_GROUPS, nc),
        in_specs=[sp["head"](rev), sp["bc"](rev, xoff), sp["bc"](rev, xoff + SSM_GROUPS), sp["head"](rev),
                  sp["dt"](rev), sp["scal"], sp["scal"], sp["scal"], sp["nw"], sp["hs"](rev), sp["head"](rev)],
        out_specs=[sp["head"](rev), sp["head"](rev), bc_out(), bc_out(), sp["dt"](rev),
                   sp["scal"], sp["scal"], sp["scal"], sp["nw"]],
        out_shape=[jax.ShapeDtypeStruct((SSM_HEADS, S, P), F32), jax.ShapeDtypeStruct((SSM_HEADS, S, P), F32),
                   jax.ShapeDtypeStruct((S, SSM_GROUPS * N), F32), jax.ShapeDtypeStruct((S, SSM_GROUPS * N), F32),
                   jax.ShapeDtypeStruct((SSM_HEADS, S, 1), F32),
                   jax.ShapeDtypeStruct((SSM_HEADS, 1, 1), F32), jax.ShapeDtypeStruct((SSM_HEADS, 1, 1), F32),
                   jax.ShapeDtypeStruct((SSM_HEADS, 1, 1), F32), jax.ShapeDtypeStruct((SSM_HEADS, 1, P), F32)],
        scratch_shapes=[pltpu.VMEM((H, N, P), F32)],
        compiler_params=_params(("parallel", "arbitrary")), name="ssd_bwd",
    )(xs_hm, xbc, xbc, z_hm, dt_hm, dtb, alog, dsk, nw_hm, hs, dy_hm)


ATTN_SCALE = HEAD_DIM ** -0.5


def _headnorm_q(q, g):
    return _rms(q, g) * ATTN_SCALE


def _qk_prep_fwd(q_hm, k_hm, gq, gk):
    Hh, S, P = q_hm.shape
    tq = _pick(S, (512, 256))

    def body(q_ref, k_ref, gq_ref, gk_ref, qo_ref, ko_ref):
        qo_ref[...] = _headnorm_q(q_ref[...], gq_ref[...]).astype(BF16)
        ko_ref[...] = _rms(k_ref[...], gk_ref[...]).astype(BF16)

    blk = pl.BlockSpec((None, tq, P), lambda h, i: (h, i, 0))
    vec = pl.BlockSpec((1, P), lambda h, i: (0, 0))
    return pl.pallas_call(
        body, grid=(Hh, S // tq), in_specs=[blk, blk, vec, vec], out_specs=[blk, blk],
        out_shape=[jax.ShapeDtypeStruct((Hh, S, P), BF16)] * 2,
        compiler_params=_params(("parallel", "parallel")), name="qk_prep_fwd")(q_hm, k_hm, gq, gk)


def _qk_prep_bwd(q_hm, k_hm, gq, gk, dqs, dkn):
    Hh, S, P = q_hm.shape
    tq = _pick(S, (512, 256))

    def body(q_ref, k_ref, gq_ref, gk_ref, dqs_ref, dkn_ref, dq_ref, dk_ref, dgq_ref, dgk_ref):
        @pl.when((pl.program_id(0) == 0) & (pl.program_id(1) == 0))
        def _():
            dgq_ref[...] = jnp.zeros_like(dgq_ref)
            dgk_ref[...] = jnp.zeros_like(dgk_ref)

        _, vq = jax.vjp(_headnorm_q, q_ref[...], gq_ref[...])
        dq, dgq = vq(dqs_ref[...])
        _, vk = jax.vjp(_rms, k_ref[...], gk_ref[...])
        dk, dgk = vk(dkn_ref[...])
        dq_ref[...] = dq.astype(dq_ref.dtype)
        dk_ref[...] = dk.astype(dk_ref.dtype)
        dgq_ref[...] += dgq
        dgk_ref[...] += dgk

    blk = pl.BlockSpec((None, tq, P), lambda h, i: (h, i, 0))
    vec = pl.BlockSpec((1, P), lambda h, i: (0, 0))
    return pl.pallas_call(
        body, grid=(Hh, S // tq), in_specs=[blk, blk, vec, vec, blk, blk], out_specs=[blk, blk, vec, vec],
        out_shape=[jax.ShapeDtypeStruct((Hh, S, P), BF16)] * 2 + [jax.ShapeDtypeStruct((1, P), F32)] * 2,
        compiler_params=_params(("arbitrary", "arbitrary")), name="qk_prep_bwd")(q_hm, k_hm, gq, gk, dqs, dkn)


def _logf_cumsum_fwd(f_raw, f_bias):
    S, Hh = f_raw.shape
    L = CHUNK

    def body(f_ref, b_ref, o_ref):
        ri = lax.broadcasted_iota(jnp.int32, (L, L), 0)
        ci = lax.broadcasted_iota(jnp.int32, (L, L), 1)
        tril = (ri >= ci).astype(F32)
        carry = jnp.zeros((1, Hh), F32)
        for c in range(S // L):
            lf = -_softplus(-(f_ref[c * L:(c + 1) * L, :] + b_ref[...]))
            cum = _dot32(tril, lf) + carry
            o_ref[c * L:(c + 1) * L, :] = cum
            carry = cum[L - 1:L, :]

    return pl.pallas_call(body, out_shape=jax.ShapeDtypeStruct((S, Hh), F32), name="logf_cumsum_fwd")(f_raw, f_bias)


def _logf_cumsum_bwd(f_raw, f_bias, dcum):
    S, Hh = f_raw.shape
    L = CHUNK

    def body(f_ref, b_ref, d_ref, df_ref, db_ref):
        ri = lax.broadcasted_iota(jnp.int32, (L, L), 0)
        ci = lax.broadcasted_iota(jnp.int32, (L, L), 1)
        triu = (ri <= ci).astype(F32)
        carry = jnp.zeros((1, Hh), F32)
        db = jnp.zeros((1, Hh), F32)
        for c in reversed(range(S // L)):
            suf = _dot32(triu, d_ref[c * L:(c + 1) * L, :]) + carry
            df = suf * jax.nn.sigmoid(-(f_ref[c * L:(c + 1) * L, :] + b_ref[...]))
            df_ref[c * L:(c + 1) * L, :] = df
            db = db + jnp.sum(df, axis=0, keepdims=True)
            carry = suf[0:1, :]
        db_ref[...] = db

    return pl.pallas_call(
        body, out_shape=[jax.ShapeDtypeStruct((S, Hh), F32), jax.ShapeDtypeStruct((1, Hh), F32)],
        name="logf_cumsum_bwd")(f_raw, f_bias, dcum)


_NT = (((1,), (1,)), ((), ()))
_TN = (((0,), (0,)), ((), ()))


def _mxu(a, b, dims=(((1,), (0,)), ((), ()))):
    return lax.dot_general(a, b, dims, preferred_element_type=F32)


def _flash_fwd(qs, kn, v, cq, ck):
    Hh, S, P = qs.shape
    t = _pick(S, (256,))

    def body(q_ref, k_ref, v_ref, cq_ref, ck_ref, o_ref, of_ref, lse_ref):
        i = pl.program_id(1)
        q = q_ref[...]
        cq_t = cq_ref[...]

        def step(j, carry, masked):
            m, l, acc, rem = carry
            off = pl.multiple_of(j * t, t)
            k = k_ref[pl.ds(off, t), :]
            vv = v_ref[pl.ds(off, t), :]
            s = _mxu(q, k, _NT) + cq_t - ck_ref[:, pl.ds(off, t)]
            if masked:
                ri = lax.broadcasted_iota(jnp.int32, (t, t), 0)
                ci = lax.broadcasted_iota(jnp.int32, (t, t), 1)
                s = jnp.where(ri >= ci, s, -1e30)
            m_new = jnp.maximum(m, jnp.max(s, axis=-1, keepdims=True))
            alpha = jnp.exp(m - m_new)
            p = jnp.exp(s - m_new)
            l = alpha * l + jnp.sum(p, axis=-1, keepdims=True)
            p_hi = p.astype(BF16)
            acc = alpha * acc + _mxu(p_hi, vv)
            rem = alpha * rem + _mxu((p - p_hi.astype(F32)).astype(BF16), vv)
            return m_new, l, acc, rem

        init = (jnp.full((t, 1), -1e30, F32), jnp.zeros((t, 1), F32), jnp.zeros((t, P), F32), jnp.zeros((t, P), F32))
        carry = lax.fori_loop(0, i, lambda j, c: step(j, c, False), init)
        m, l, acc, rem = step(i, carry, True)
        o_ref[...] = acc / l
        of_ref[...] = (acc + rem) / l
        lse_ref[...] = m + jnp.log(l)

    qblk = pl.BlockSpec((None, t, P), lambda h, i: (h, i, 0))
    full = pl.BlockSpec((None, S, P), lambda h, i: (h, 0, 0))
    return pl.pallas_call(
        body, grid=(Hh, S // t),
        in_specs=[qblk, full, full, pl.BlockSpec((None, t, 1), lambda h, i: (h, i, 0)),
                  pl.BlockSpec((None, 1, S), lambda h, i: (h, 0, 0))],
        out_specs=[qblk, qblk, pl.BlockSpec((None, t, 1), lambda h, i: (h, i, 0))],
        out_shape=[jax.ShapeDtypeStruct((Hh, S, P), F32), jax.ShapeDtypeStruct((Hh, S, P), F32),
                   jax.ShapeDtypeStruct((Hh, S, 1), F32)],
        compiler_params=_params(("parallel", "parallel")), name="flash_fwd")(qs, kn, v, cq, ck)


def _flash_bwd(qs, kn, v, cq, ck, o_fine, do, lse):
    Hh, S, P = qs.shape
    t = _pick(S, (256,))
    nq = S // t

    def body(q_ref, k_ref, v_ref, cq_ref, ck_ref, o_ref, do_ref, lse_ref, dq_ref, dk_ref, dv_ref, dck_ref):
        j = pl.program_id(1)

        @pl.when(j == 0)
        def _():
            dq_ref[...] = jnp.zeros_like(dq_ref)

        k = k_ref[...]
        vv = v_ref[...]
        ck_t = ck_ref[...]

        def step(i, carry, masked):
            dk, dv, dck = carry
            off = pl.multiple_of(i * t, t)
            rows = pl.ds(off, t)
            q = q_ref[rows, :]
            do_t = do_ref[rows, :]
            s = _mxu(q, k, _NT) + cq_ref[rows, :] - ck_t
            if masked:
                ri = lax.broadcasted_iota(jnp.int32, (t, t), 0)
                ci = lax.broadcasted_iota(jnp.int32, (t, t), 1)
                s = jnp.where(ri >= ci, s, -1e30)
            p = jnp.exp(s - lse_ref[rows, :])
            dob = do_t.astype(BF16)
            dv = dv + _mxu(p.astype(BF16), dob, _TN)
            dp = _mxu(dob, vv, _NT)
            delta = jnp.sum(dob.astype(F32) * o_ref[rows, :], axis=-1, keepdims=True)
            ds = p * (dp - delta)
            dsb = ds.astype(BF16)
            dk = dk + _mxu(dsb, q, _TN)
            dq_ref[rows, :] += _mxu(dsb, k)
            dck = dck - jnp.sum(ds, axis=0, keepdims=True)
            return dk, dv, dck

        init = (jnp.zeros((t, P), F32), jnp.zeros((t, P), F32), jnp.zeros((1, t), F32))
        carry = step(j, init, True)
        dk, dv, dck = lax.fori_loop(j + 1, nq, lambda i, c: step(i, c, False), carry)
        dk_ref[...] = dk
        dv_ref[...] = dv
        dck_ref[...] = dck

    kblk = pl.BlockSpec((None, t, P), lambda h, j: (h, j, 0))
    full = pl.BlockSpec((None, S, P), lambda h, j: (h, 0, 0))
    col = pl.BlockSpec((None, S, 1), lambda h, j: (h, 0, 0))
    rowt = pl.BlockSpec((None, 1, t), lambda h, j: (h, 0, j))
    return pl.pallas_call(
        body, grid=(Hh, nq),
        in_specs=[full, kblk, kblk, col, rowt, full, full, col],
        out_specs=[full, kblk, kblk, rowt],
        out_shape=[jax.ShapeDtypeStruct((Hh, S, P), F32)] * 3 + [jax.ShapeDtypeStruct((Hh, 1, S), F32)],
        compiler_params=_params(("parallel", "arbitrary")), name="flash_bwd")(qs, kn, v, cq, ck, o_fine, do, lse)


XATTN_SCALE = XATTN_DIM ** -0.5


def _xq_norm(q, g):
    return _rms(q, g) * XATTN_SCALE


def _xattn_fwd(xq, kv, gq, gk):
    S = xq.shape[0]
    Mm = kv.shape[0]
    Dh = XATTN_DIM
    tq = _pick(S, (512, 256))

    def body(q_ref, k_ref, v_ref, gq_ref, gk_ref, o_ref):
        qn = _xq_norm(q_ref[...], gq_ref[...]).astype(BF16)
        kn = _rms(k_ref[...], gk_ref[...]).astype(BF16)
        s = _mxu(qn, kn, _NT)
        m = jnp.max(s, axis=-1, keepdims=True)
        p = jnp.exp(s - m)
        l = jnp.sum(p, axis=-1, keepdims=True)
        o_ref[...] = (_mxu(p.astype(BF16), v_ref[...].astype(BF16)) / l).astype(o_ref.dtype)

    vec = pl.BlockSpec((1, Dh), lambda h, i: (0, 0))
    return pl.pallas_call(
        body, grid=(XATTN_HEADS, S // tq),
        in_specs=[pl.BlockSpec((tq, Dh), lambda h, i: (i, h)), pl.BlockSpec((Mm, Dh), lambda h, i: (0, h)),
                  pl.BlockSpec((Mm, Dh), lambda h, i: (0, XATTN_HEADS + h)), vec, vec],
        out_specs=pl.BlockSpec((tq, Dh), lambda h, i: (i, h)),
        out_shape=jax.ShapeDtypeStruct((S, XATTN_HEADS * Dh), BF16),
        compiler_params=_params(("parallel", "parallel")), name="xattn_fwd")(xq, kv, kv, gq, gk)


def _xattn_bwd(xq, kv, gq, gk, do):
    S = xq.shape[0]
    Mm = kv.shape[0]
    Dh = XATTN_DIM
    tq = _pick(S, (512, 256))
    nq = S // tq

    def body(q_ref, k_ref, v_ref, gq_ref, gk_ref, do_ref, dq_ref, dk_ref, dv_ref, dgq_ref, dgk_ref, dkn_acc, dv_acc):
        h = pl.program_id(0)
        i = pl.program_id(1)

        @pl.when((h == 0) & (i == 0))
        def _():
            dgq_ref[...] = jnp.zeros_like(dgq_ref)
            dgk_ref[...] = jnp.zeros_like(dgk_ref)

        @pl.when(i == 0)
        def _():
            dkn_acc[...] = jnp.zeros_like(dkn_acc)
            dv_acc[...] = jnp.zeros_like(dv_acc)

        qn32, vq = jax.vjp(_xq_norm, q_ref[...], gq_ref[...])
        kn32, vk = jax.vjp(_rms, k_ref[...], gk_ref[...])
        qn = qn32.astype(BF16)
        kn = kn32.astype(BF16)
        vb = v_ref[...].astype(BF16)
        s = _mxu(qn, kn, _NT)
        m = jnp.max(s, axis=-1, keepdims=True)
        p = jnp.exp(s - m)
        p = p / jnp.sum(p, axis=-1, keepdims=True)
        dob = do_ref[...].astype(BF16)
        dp = _mxu(dob, vb, _NT)
        delta = jnp.sum(p * dp, axis=-1, keepdims=True)
        ds = (p * (dp - delta)).astype(BF16)
        dv_acc[...] += _mxu(p.astype(BF16), dob, _TN)
        dkn_acc[...] += _mxu(ds, qn, _TN)
        dq, dgq = vq(_mxu(ds, kn))
        dq_ref[...] = dq.astype(dq_ref.dtype)
        dgq_ref[...] += dgq

        @pl.when(i == nq - 1)
        def _():
            dk, dgk = vk(dkn_acc[...])
            dk_ref[...] = dk.astype(dk_ref.dtype)
            dv_ref[...] = dv_acc[...].astype(dv_ref.dtype)
            dgk_ref[...] += dgk

    vec = pl.BlockSpec((1, Dh), lambda h, i: (0, 0))
    qblk = pl.BlockSpec((tq, Dh), lambda h, i: (i, h))
    kblk = pl.BlockSpec((Mm, Dh), lambda h, i: (0, h))
    vblk = pl.BlockSpec((Mm, Dh), lambda h, i: (0, XATTN_HEADS + h))
    return pl.pallas_call(
        body, grid=(XATTN_HEADS, nq),
        in_specs=[qblk, kblk, vblk, vec, vec, qblk],
        out_specs=[qblk, kblk, kblk, vec, vec],
        out_shape=[jax.ShapeDtypeStruct((S, XATTN_HEADS * Dh), BF16),
                   jax.ShapeDtypeStruct((Mm, XATTN_HEADS * Dh), BF16),
                   jax.ShapeDtypeStruct((Mm, XATTN_HEADS * Dh), BF16),
                   jax.ShapeDtypeStruct((1, Dh), F32), jax.ShapeDtypeStruct((1, Dh), F32)],
        scratch_shapes=[pltpu.VMEM((Mm, Dh), F32), pltpu.VMEM((Mm, Dh), F32)],
        compiler_params=_params(("arbitrary", "arbitrary")), name="xattn_bwd")(xq, kv, kv, gq, gk, do)


def _loss_head(y, target):
    S, D = y.shape
    tr = _pick(S, (512, 256))

    def body(y_ref, t_ref, dy_ref, loss_ref):
        @pl.when(pl.program_id(0) == 0)
        def _():
            loss_ref[...] = jnp.zeros_like(loss_ref)

        err = y_ref[...] - t_ref[...]
        dy_ref[...] = err * (1.0 / D)
        loss_ref[...] += jnp.sum(err * err) * (0.5 / D)

    row = pl.BlockSpec((tr, D), lambda i: (i, 0))
    return pl.pallas_call(
        body, grid=(S // tr,), in_specs=[row, row],
        out_specs=[row, pl.BlockSpec((1, LANES), lambda i: (0, 0))],
        out_shape=[jax.ShapeDtypeStruct((S, D), F32), jax.ShapeDtypeStruct((1, LANES), F32)],
        compiler_params=_params(("arbitrary",)), name="loss_head")(y, target)


def _row_tile(R, C):
    for tr in (1024, 512, 256, 128, 64, 32, 16, 8):
        if R % tr == 0 and tr * C * 4 <= (1 << 20):
            return tr
    return R


def _nsum(arrs, out_dtypes, name):
    R, C = arrs[0].shape
    tr = _row_tile(R, C)
    n = len(arrs)

    def body(*refs):
        acc = refs[0][...].astype(F32)
        for r in refs[1:n]:
            acc = acc + r[...].astype(F32)
        for o in refs[n:]:
            o[...] = acc.astype(o.dtype)

    blk = pl.BlockSpec((tr, C), lambda i: (i, 0))
    outs = pl.pallas_call(
        body, grid=(R // tr,), in_specs=[blk] * n, out_specs=[blk] * len(out_dtypes),
        out_shape=[jax.ShapeDtypeStruct((R, C), d) for d in out_dtypes],
        compiler_params=_params(("parallel",)), name=name)(*arrs)
    return outs


def _adamw(w, g, m, v, name):
    R, C = w.shape
    tr = _row_tile(R, C)
    c1 = 1.0 - ADAM_B1 ** ADAM_STEP
    c2 = 1.0 - ADAM_B2 ** ADAM_STEP

    def body(w_ref, g_ref, m_ref, v_ref, d_ref, mo_ref, vo_ref):
        g_t = g_ref[...]
        m_new = ADAM_B1 * m_ref[...] + (1.0 - ADAM_B1) * g_t
        v_new = ADAM_B2 * v_ref[...] + (1.0 - ADAM_B2) * (g_t * g_t)
        d_ref[...] = -ADAM_LR * ((m_new / c1) / (jnp.sqrt(v_new / c2) + ADAM_EPS) + ADAM_WD * w_ref[...])
        mo_ref[...] = m_new
        vo_ref[...] = v_new

    blk = pl.BlockSpec((tr, C), lambda i: (i, 0))
    return pl.pallas_call(
        body, grid=(R // tr,), in_specs=[blk] * 4, out_specs=[blk] * 3,
        out_shape=[jax.ShapeDtypeStruct((R, C), F32)] * 3,
        compiler_params=_params(("parallel",)), name=name)(w, g, m, v)


D_MODEL = 1024
SSM_INNER = SSM_HEADS * HEAD_DIM
CONV_DIM = SSM_INNER + 2 * SSM_GROUPS * SSM_STATE
ATTN_WIDTH = ATTN_HEADS * HEAD_DIM
COL_Z = 0
COL_XBC = COL_Z + SSM_INNER
COL_DT = COL_XBC + CONV_DIM
COL_Q = COL_DT + SSM_HEADS
COL_K = COL_Q + ATTN_WIDTH
COL_V = COL_K + ATTN_WIDTH
COL_F = COL_V + ATTN_WIDTH
IN_COLS = COL_F + ATTN_HEADS
IN_COLS_PAD = -(-IN_COLS // LANES) * LANES


def _to_heads(a):
    S = a.shape[0]
    return a.reshape(S, -1, HEAD_DIM).transpose(1, 0, 2)


def _from_heads(a):
    return a.transpose(1, 0, 2).reshape(a.shape[1], -1)


def _add_residual(acc, res):
    return (res + acc,)


def _relu2(acc):
    r = jnp.maximum(acc, 0.0)
    return acc, r * r


def _relu2_bwd(acc, a):
    return (acc * (2.0 * jnp.maximum(a, 0.0)),)


def _layer_fwd_bwd(x, mem, target, W, p):
    S = x.shape[0]
    hd3 = lambda a: a.reshape(SSM_HEADS, 1, 1)

    h1 = _rmsnorm_fwd(x, p["g_mix"], "norm_mix")
    proj = _mm(h1, W["w_in"], "nn", "in_proj")
    xbc = _conv_fwd(proj, COL_XBC, CONV_DIM, p["conv_w"], p["conv_b"])
    xs_hm = _to_heads(xbc[:, :SSM_INNER])
    z_hm = _to_heads(proj[:, COL_Z:COL_Z + SSM_INNER])
    dt_hm = proj[:, COL_DT:COL_DT + SSM_HEADS].T[:, :, None]
    ssd_par = (hd3(p["dt_bias"]), hd3(p["a_log"]), hd3(p["d_skip"]), p["ssm_norm_w"].reshape(SSM_HEADS, 1, HEAD_DIM))
    y_hm, hs = _ssd_fwd(xs_hm, xbc, z_hm, dt_hm, *ssd_par)
    q_hm = _to_heads(proj[:, COL_Q:COL_Q + ATTN_WIDTH])
    k_hm = _to_heads(proj[:, COL_K:COL_K + ATTN_WIDTH])
    v_hm = _to_heads(proj[:, COL_V:COL_V + ATTN_WIDTH]).astype(BF16)
    f_raw = proj[:, COL_F:COL_F + ATTN_HEADS]
    qs, kn = _qk_prep_fwd(q_hm, k_hm, p["g_q"], p["g_k"])
    cum = _logf_cumsum_fwd(f_raw, p["f_bias"])
    cq = cum.T[:, :, None]
    ck = cum.T[:, None, :]
    o_hm, o_fine, lse = _flash_fwd(qs, kn, v_hm, cq, ck)
    mixed = jnp.concatenate([_from_heads(y_hm), _from_heads(o_hm)], axis=-1).astype(BF16)
    x1 = _mm(mixed, W["w_out"], "nn", "out_proj", epilogue=_add_residual, extras=(x,))
    h2 = _rmsnorm_fwd(x1, p["g_xattn"], "norm_xattn")
    mem_n = _rmsnorm_fwd(mem, p["g_mem"], "norm_mem")
    xq = _mm(h2, W["xq_w"], "nn", "xq_proj")
    kv = _mm(mem_n, W["xkv_w"], "nn", "xkv_proj", b_chunks=N_CHIPS)
    xo = _xattn_fwd(xq, kv, p["xg_q"], p["xg_k"])
    x2 = _mm(xo, W["xo_w"], "nn", "xo_proj", epilogue=_add_residual, extras=(x1,))
    h3 = _rmsnorm_fwd(x2, p["g_mlp"], "norm_mlp")
    a, act = _mm(h3, W["w_up"], "nn", "mlp_up", out_dtypes=(F32, BF16), epilogue=_relu2, b_chunks=N_CHIPS)
    x3 = _mm(act, W["w_down"], "nn", "mlp_down", epilogue=_add_residual, extras=(x2,))
    dy, loss_row = _loss_head(x3, target)

    gW, gp = {}, {}
    da = _mm(dy, W["w_down"], "nt", "d_act", out_dtypes=(BF16,), epilogue=_relu2_bwd, extras=(a,))
    gW["w_down"] = _mm(act, dy, "tn", "g_w_down", out_dtypes=(BF16,))
    gW["w_up"] = _mm(h3, da, "tn", "g_w_up", out_dtypes=(BF16,), out_chunks=N_CHIPS)
    dh3 = _mm(da, W["w_up"], "nt", "d_h3", b_chunks=N_CHIPS)
    dx2, gp["g_mlp"] = _rmsnorm_bwd(x2, p["g_mlp"], dh3, dy, "norm_mlp_bwd")
    dxo = _mm(dx2, W["xo_w"], "nt", "d_xo", out_dtypes=(BF16,))
    gW["xo_w"] = _mm(xo, dx2, "tn", "g_xo_w", out_dtypes=(BF16,))
    dxq, dk_x, dv_x, gp["xg_q"], gp["xg_k"] = _xattn_bwd(xq, kv, p["xg_q"], p["xg_k"], dxo)
    dkv = jnp.concatenate([dk_x, dv_x], axis=-1)
    gW["xq_w"] = _mm(h2, dxq, "tn", "g_xq_w", out_dtypes=(BF16,))
    dh2 = _mm(dxq, W["xq_w"], "nt", "d_h2")
    gW["xkv_w"] = _mm(mem_n, dkv, "tn", "g_xkv_w", out_dtypes=(BF16,), out_chunks=N_CHIPS)
    dmem_n = _mm(dkv, W["xkv_w"], "nt", "d_mem_n", b_chunks=N_CHIPS)
    _, gp["g_mem"] = _rmsnorm_bwd(mem, p["g_mem"], dmem_n, None, "norm_mem_bwd")
    dx1, gp["g_xattn"] = _rmsnorm_bwd(x1, p["g_xattn"], dh2, dx2, "norm_xattn_bwd")
    dmixed = _mm(dx1, W["w_out"], "nt", "d_mixed")
    gW["w_out"] = _mm(mixed, dx1, "tn", "g_w_out", out_dtypes=(BF16,))
    dy_hm = _to_heads(dmixed[:, :SSM_INNER])
    do_hm = _to_heads(dmixed[:, SSM_INNER:])
    dqs, dkn, dv_hm, dck = _flash_bwd(qs, kn, v_hm, cq, ck, o_fine, do_hm, lse)
    dq_raw, dk_raw, gp["g_q"], gp["g_k"] = _qk_prep_bwd(q_hm, k_hm, p["g_q"], p["g_k"], dqs, dkn)
    df, gp["f_bias"] = _logf_cumsum_bwd(f_raw, p["f_bias"], dck[:, 0, :].T)
    dxs_hm, dz_hm, dB, dC, ddt, ddtb, dalog, ddsk, dnw = _ssd_bwd(xs_hm, xbc, z_hm, dt_hm, *ssd_par, hs, dy_hm)
    gp["dt_bias"] = ddtb.reshape(1, SSM_HEADS)
    gp["a_log"] = dalog.reshape(1, SSM_HEADS)
    gp["d_skip"] = ddsk.reshape(1, SSM_HEADS)
    gp["ssm_norm_w"] = dnw.reshape(1, SSM_INNER)
    dxbc = jnp.concatenate([_from_heads(dxs_hm), dB, dC], axis=-1)
    dxbc_raw, dconv_w, gp["conv_b"] = _conv_bwd(proj, COL_XBC, CONV_DIM, p["conv_w"], p["conv_b"], dxbc)
    gp["conv_w"] = dconv_w[:CONV_WIDTH]
    dproj = jnp.concatenate(
        [_from_heads(dz_hm).astype(BF16), dxbc_raw, ddt[:, :, 0].T.astype(BF16), _from_heads(dq_raw),
         _from_heads(dk_raw), _from_heads(dv_hm).astype(BF16), df.astype(BF16),
         jnp.zeros((S, IN_COLS_PAD - IN_COLS), BF16)], axis=-1)
    gW["w_in"] = _mm(h1, dproj, "tn", "g_w_in", out_dtypes=(BF16,))
    dh1 = _mm(dproj, W["w_in"], "nt", "d_h1")
    dx, gp["g_mix"] = _rmsnorm_bwd(x, p["g_mix"], dh1, dx1, "norm_mix_bwd")
    return loss_row, dx, gW, gp


_ANY = pl.BlockSpec(memory_space=pl.ANY)


def _place():
    x, y, c = lax.axis_index("x"), lax.axis_index("y"), lax.axis_index("c")
    chips = [(1 - x, y), (x, 1 - y), (1 - x, 1 - y)]
    return x, y, c, chips


def _chip_index(px, py):
    return 2 * px + py


def _all_gather_chips(split, whole):
    ns, nw = len(split), len(whole)
    n = ns + nw

    def body(*refs):
        ins, outs = refs[:n], refs[n:2 * n]
        send_ici, recv_ici, send_d2d, recv_d2d = refs[2 * n:]
        x, y, c, chips = _place()
        me = _chip_index(x, y)
        sib = (x, y, 1 - c)

        def ici(k, j, src, dst):
            return pltpu.make_async_remote_copy(src_ref=src, dst_ref=dst, send_sem=send_ici.at[3 * k + j],
                                                recv_sem=recv_ici.at[3 * k + j], device_id=(*chips[j], c),
                                                device_id_type=MESH)

        def d2d(k, j, piece):
            return pltpu.make_async_remote_copy(src_ref=piece, dst_ref=piece, send_sem=send_d2d.at[3 * k + j],
                                                recv_sem=recv_d2d.at[3 * k + j], device_id=sib, device_id_type=MESH)

        sends = []
        for k in range(n):
            for j in range(3):
                if k < ns:
                    sends.append(ici(k, j, ins[k].at[c], outs[k].at[me, c]))
                else:
                    sends.append(ici(k, j, ins[k], outs[k].at[me]))
                sends[-1].start()
        passed = []
        for k in range(n):
            for j in range(3):
                src_chip = _chip_index(*chips[j])
                if k < ns:
                    ici(k, j, ins[k].at[c], outs[k].at[src_chip, c]).wait_recv()
                    passed.append(d2d(k, j, outs[k].at[src_chip, c]))
                    passed[-1].start()
                else:
                    ici(k, j, ins[k], outs[k].at[src_chip]).wait_recv()
        for k in range(ns):
            for j in range(3):
                d2d(k, j, outs[k].at[_chip_index(*chips[j]), 1 - c]).wait_recv()
        for cp in sends + passed:
            cp.wait_send()

    arrs = list(split) + list(whole)
    return pl.pallas_call(
        body, in_specs=[_ANY] * n, out_specs=[_ANY] * n,
        out_shape=[jax.ShapeDtypeStruct((N_CHIPS,) + a.shape, a.dtype) for a in arrs],
        scratch_shapes=[pltpu.SemaphoreType.DMA((3 * n,)), pltpu.SemaphoreType.DMA((3 * n,)),
                        pltpu.SemaphoreType.DMA((3 * ns,)), pltpu.SemaphoreType.DMA((3 * ns,))],
        name="all_gather_chips")(*arrs)


def _sibling_send_halves(grads):
    n = len(grads)

    def body(*refs):
        ins, outs = refs[:n], refs[n:2 * n]
        send_sem, recv_sem = refs[2 * n:]
        x, y, c, _ = _place()

        def cp(k, j, half):
            return pltpu.make_async_remote_copy(src_ref=ins[k].at[j, half], dst_ref=outs[k].at[j],
                                                send_sem=send_sem.at[N_CHIPS * k + j],
                                                recv_sem=recv_sem.at[N_CHIPS * k + j],
                                                device_id=(x, y, 1 - c), device_id_type=MESH)

        copies = [cp(k, j, 1 - c) for k in range(n) for j in range(N_CHIPS)]
        for q in copies:
            q.start()
        for q in copies:
            q.wait()

    return pl.pallas_call(
        body, in_specs=[_ANY] * n, out_specs=[_ANY] * n,
        out_shape=[jax.ShapeDtypeStruct((N_CHIPS,) + g.shape[2:], g.dtype) for g in grads],
        scratch_shapes=[pltpu.SemaphoreType.DMA((N_CHIPS * n,)), pltpu.SemaphoreType.DMA((N_CHIPS * n,))],
        name="rs_sibling_halves")(*grads)


def _chips_send_shards(parts):
    n = len(parts)

    def body(*refs):
        ins, outs = refs[:n], refs[n:2 * n]
        send_sem, recv_sem = refs[2 * n:]
        x, y, c, chips = _place()

        def cp(k, j):
            return pltpu.make_async_remote_copy(src_ref=ins[k].at[_chip_index(*chips[j])], dst_ref=outs[k].at[j],
                                                send_sem=send_sem.at[3 * k + j], recv_sem=recv_sem.at[3 * k + j],
                                                device_id=(*chips[j], c), device_id_type=MESH)

        copies = [cp(k, j) for k in range(n) for j in range(3)]
        for q in copies:
            q.start()
        for q in copies:
            q.wait()

    return pl.pallas_call(
        body, in_specs=[_ANY] * n, out_specs=[_ANY] * n,
        out_shape=[jax.ShapeDtypeStruct((3,) + g.shape[1:], g.dtype) for g in parts],
        scratch_shapes=[pltpu.SemaphoreType.DMA((3 * n,)), pltpu.SemaphoreType.DMA((3 * n,))],
        name="rs_chip_shards")(*parts)


def _sibling_exchange(halves):
    n = len(halves)

    def body(*refs):
        ins, outs = refs[:n], refs[n:2 * n]
        send_sem, recv_sem = refs[2 * n:]
        x, y, c, _ = _place()

        def cp(k, half):
            return pltpu.make_async_remote_copy(src_ref=ins[k], dst_ref=outs[k].at[half], send_sem=send_sem.at[k],
                                                recv_sem=recv_sem.at[k], device_id=(x, y, 1 - c), device_id_type=MESH)

        sends = [cp(k, c) for k in range(n)]
        for q in sends:
            q.start()
        for k in range(n):
            cp(k, 1 - c).wait_recv()
        for q in sends:
            q.wait_send()

    return pl.pallas_call(
        body, in_specs=[_ANY] * n, out_specs=[_ANY] * n,
        out_shape=[jax.ShapeDtypeStruct((2,) + h.shape, h.dtype) for h in halves],
        scratch_shapes=[pltpu.SemaphoreType.DMA((n,)), pltpu.SemaphoreType.DMA((n,))],
        name="rs_sibling_exchange")(*halves)


def _all_reduce_small(vec):
    R = vec.shape[0]

    def body(v_ref, o_ref, buf, send_sem, recv_sem):
        x, y, c = lax.axis_index("x"), lax.axis_index("y"), lax.axis_index("c")
        me = 4 * x + 2 * y + c
        buf[me] = v_ref[...]
        copies = []
        for r in range(1, N_DEV):
            fx, fy, fc = (r >> 2) & 1, (r >> 1) & 1, r & 1
            peer = (x ^ fx, y ^ fy, c ^ fc)
            copies.append(pltpu.make_async_remote_copy(src_ref=v_ref, dst_ref=buf.at[me], send_sem=send_sem.at[r - 1],
                                                       recv_sem=recv_sem.at[r - 1], device_id=peer, device_id_type=MESH))
        for q in copies:
            q.start()
        for r in range(1, N_DEV):
            fx, fy, fc = (r >> 2) & 1, (r >> 1) & 1, r & 1
            src = 4 * (x ^ fx) + 2 * (y ^ fy) + (c ^ fc)
            pltpu.make_async_remote_copy(src_ref=v_ref, dst_ref=buf.at[src], send_sem=send_sem.at[r - 1],
                                         recv_sem=recv_sem.at[r - 1], device_id=(x, y, c), device_id_type=MESH).wait_recv()
        acc = buf[0]
        for d in range(1, N_DEV):
            acc = acc + buf[d]
        o_ref[...] = acc
        for q in copies:
            q.wait_send()

    vm = pl.BlockSpec(memory_space=pltpu.VMEM)
    return pl.pallas_call(
        body, in_specs=[vm], out_specs=vm, out_shape=jax.ShapeDtypeStruct((R, LANES), F32),
        scratch_shapes=[pltpu.VMEM((N_DEV, R, LANES), F32), pltpu.SemaphoreType.DMA((N_DEV - 1,)),
                        pltpu.SemaphoreType.DMA((N_DEV - 1,))],
        name="all_reduce_small")(vec)


_INPUTS = ["x", "mem", "g_mix", "w_in", "conv_w", "conv_b", "dt_bias", "a_log", "d_skip", "ssm_norm_w", "g_q", "g_k",
           "f_bias", "w_out", "g_xattn", "g_mem", "xq_w", "xkv_w", "xg_q", "xg_k", "xo_w", "g_mlp", "w_up", "w_down"]
_WEIGHTS = _INPUTS[2:]
_BIG = ["w_in", "w_out", "xq_w", "xkv_w", "xo_w", "w_up", "w_down"]
_COL_SHARDED = ["w_in", "xkv_w", "w_up"]
_SMALL = [n for n in _WEIGHTS if n not in _BIG]


def _pack_rows(arrs):
    rows = []
    for a in arrs:
        flat = a.reshape(-1)
        pad = -flat.shape[0] % LANES
        rows.append(jnp.pad(flat, (0, pad)).reshape(-1, LANES))
    out = jnp.concatenate(rows, axis=0)
    return jnp.pad(out, ((0, -out.shape[0] % 8), (0, 0)))


def _unpack_rows(packed, shapes):
    out, r = [], 0
    for s in shapes:
        n = math.prod(s)
        nr = -(-n // LANES)
        out.append(packed[r:r + nr].reshape(-1)[:n].reshape(s))
        r += nr
    return out


def kernel(x, mem, g_mix, w_in, conv_w, conv_b, dt_bias, a_log, d_skip, ssm_norm_w, g_q, g_k, f_bias, w_out, g_xattn, g_mem, xq_w, xkv_w, xg_q, xg_k, xo_w, g_mlp, w_up, w_down, loss_target, m_g_mix, m_w_in, m_conv_w, m_conv_b, m_dt_bias, m_a_log, m_d_skip, m_ssm_norm_w, m_g_q, m_g_k, m_f_bias, m_w_out, m_g_xattn, m_g_mem, m_xq_w, m_xkv_w, m_xg_q, m_xg_k, m_xo_w, m_g_mlp, m_w_up, m_w_down, v_g_mix, v_w_in, v_conv_w, v_conv_b, v_dt_bias, v_a_log, v_d_skip, v_ssm_norm_w, v_g_q, v_g_k, v_f_bias, v_w_out, v_g_xattn, v_g_mem, v_xq_w, v_xkv_w, v_xg_q, v_xg_k, v_xo_w, v_g_mlp, v_w_up, v_w_down):
    args = (x, mem, g_mix, w_in, conv_w, conv_b, dt_bias, a_log, d_skip, ssm_norm_w, g_q, g_k, f_bias, w_out, g_xattn,
            g_mem, xq_w, xkv_w, xg_q, xg_k, xo_w, g_mlp, w_up, w_down)
    w = dict(zip(_INPUTS, args))
    mom1 = dict(zip(_WEIGHTS, (m_g_mix, m_w_in, m_conv_w, m_conv_b, m_dt_bias, m_a_log, m_d_skip, m_ssm_norm_w, m_g_q,
                               m_g_k, m_f_bias, m_w_out, m_g_xattn, m_g_mem, m_xq_w, m_xkv_w, m_xg_q, m_xg_k, m_xo_w,
                               m_g_mlp, m_w_up, m_w_down)))
    mom2 = dict(zip(_WEIGHTS, (v_g_mix, v_w_in, v_conv_w, v_conv_b, v_dt_bias, v_a_log, v_d_skip, v_ssm_norm_w, v_g_q,
                               v_g_k, v_f_bias, v_w_out, v_g_xattn, v_g_mem, v_xq_w, v_xkv_w, v_xg_q, v_xg_k, v_xo_w,
                               v_g_mlp, v_w_up, v_w_down)))
    chip = _chip_index(lax.axis_index("x"), lax.axis_index("y"))
    core = lax.axis_index("c")

    shards = [w[n][0] for n in _BIG]
    halves = [s.astype(BF16).reshape(2, s.shape[0] // 2, s.shape[1]) for s in shards]
    gathered = _all_gather_chips(halves, [w["conv_w"][0]])
    own = halves + [w["conv_w"][0]]
    gathered = [lax.dynamic_update_index_in_dim(g, o, chip, axis=0) for g, o in zip(gathered, own)]
    full = dict(zip(_BIG, gathered[:len(_BIG)]))
    W = {}
    for n in _BIG:
        g = full[n]
        g = g.reshape(N_CHIPS, 2 * g.shape[2], g.shape[3])
        if n == "w_in":
            g = g.transpose(1, 0, 2).reshape(g.shape[1], IN_COLS)
            W[n] = jnp.pad(g, ((0, 0), (0, IN_COLS_PAD - IN_COLS)))
        elif n in _COL_SHARDED:
            W[n] = g
        else:
            W[n] = g.reshape(N_CHIPS * g.shape[1], g.shape[2])
    conv_w_full = gathered[-1].transpose(1, 0, 2).reshape(CONV_WIDTH, CONV_DIM)
    p = {n: w[n] for n in _SMALL}
    p["conv_w"] = conv_w_full

    loss_row, dx, gW, gp = _layer_fwd_bwd(x[0], mem[0], loss_target[0], W, p)

    grads4 = []
    for n, s in zip(_BIG, shards):
        g = gW[n]
        if n == "w_in":
            g = g[:, :IN_COLS].reshape(g.shape[0], N_CHIPS, IN_COLS // N_CHIPS).transpose(1, 0, 2)
        elif n not in _COL_SHARDED:
            g = g.reshape(N_CHIPS, g.shape[0] // N_CHIPS, g.shape[1])
        grads4.append(g.reshape(N_CHIPS, 2, g.shape[1] // 2, g.shape[2]))
    from_sibling = _sibling_send_halves(grads4)
    pair_sums = []
    for k, g in enumerate(grads4):
        mine = lax.dynamic_index_in_dim(g, core, axis=1, keepdims=False)
        flat = lambda a: a.reshape(-1, a.shape[-1])
        (s,) = _nsum([flat(mine), flat(from_sibling[k])], (BF16,), "rs_pair_sum_" + _BIG[k])
        pair_sums.append(s.reshape(mine.shape))
    from_chips = _chips_send_shards(pair_sums)
    reduced = []
    for k, ps in enumerate(pair_sums):
        own = lax.dynamic_index_in_dim(ps, chip, axis=0, keepdims=False)
        (r,) = _nsum([own, from_chips[k][0], from_chips[k][1], from_chips[k][2]], (F32,), "rs_chip_sum_" + _BIG[k])
        reduced.append(r)
    grad_shards = [lax.dynamic_update_index_in_dim(g, r, core, axis=0)
                   for g, r in zip(_sibling_exchange(reduced), reduced)]

    small_shapes = [gp[n].shape for n in _SMALL] + [(1, LANES)]
    packed = _pack_rows([gp[n] for n in _SMALL] + [loss_row])
    summed = _unpack_rows(_all_reduce_small(packed), small_shapes)
    gsmall = dict(zip(_SMALL, summed[:-1]))
    loss = summed[-1][0, 0]
    shard_cols = CONV_DIM // N_CHIPS
    gsmall["conv_w"] = lax.dynamic_slice_in_dim(gsmall["conv_w"], chip * shard_cols, shard_cols, axis=1)

    grad, delta, new_m, new_v = {}, {}, {}, {}
    for k, n in enumerate(_BIG):
        shape = w[n].shape
        g2 = grad_shards[k].reshape(shape[1], shape[2])
        d, m1, v1 = _adamw(w[n][0], g2, mom1[n][0], mom2[n][0], "adamw_" + n)
        grad[n], delta[n], new_m[n], new_v[n] = (a.reshape(shape) for a in (g2, d, m1, v1))
    pk = lambda src: _pack_rows([src[n] for n in _SMALL])
    for n in _SMALL:
        gsmall[n] = gsmall[n].reshape(w[n].shape)
    d, m1, v1 = _adamw(pk(w), pk(gsmall), pk(mom1), pk(mom2), "adamw_small")
    shapes = [w[n].shape for n in _SMALL]
    for n, dn, mn, vn in zip(_SMALL, _unpack_rows(d, shapes), _unpack_rows(m1, shapes), _unpack_rows(v1, shapes)):
        grad[n], delta[n], new_m[n], new_v[n] = gsmall[n], dn, mn, vn

    return (loss, dx[None], *[grad[n] for n in _WEIGHTS], *[delta[n] for n in _WEIGHTS],
            *[new_m[n] for n in _WEIGHTS], *[new_v[n] for n in _WEIGHTS])
```

```python
import functools
import math

import jax
import jax.numpy as jnp
from jax import lax
from jax.experimental import pallas as pl
from jax.experimental.pallas import tpu as pltpu

F32 = jnp.float32
BF16 = jnp.bfloat16
HI = lax.Precision.HIGHEST
MESH = pl.DeviceIdType.MESH

EPS = 1e-5
CHUNK = 128
SSM_HEADS = 16
SSM_GROUPS = 2
HEADS_PER_GROUP = SSM_HEADS // SSM_GROUPS
HEAD_DIM = 64
SSM_STATE = 128
ATTN_HEADS = 16
XATTN_HEADS = 4
XATTN_DIM = 256
CONV_WIDTH = 4
N_CHIPS = 4
N_DEV = 8
LANES = 128
VMEM_LIMIT = 56 * 1024 * 1024

ADAM_LR = 0.001
ADAM_B1 = 0.9
ADAM_B2 = 0.999
ADAM_EPS = 1e-08
ADAM_WD = 0.01
ADAM_STEP = 10


def _params(sem):
    return pltpu.CompilerParams(dimension_semantics=sem, vmem_limit_bytes=VMEM_LIMIT)


def _pick(n, cands):
    for c in cands:
        if n % c == 0:
            return c
    return n


def _mm(a, b, mode, name, out_dtypes=(F32,), epilogue=None, extras=(), b_chunks=1, out_chunks=1,
        tm=None, tn=None, tk=None):
    if mode == "nn":
        M, K = a.shape
        N = b.shape[-1] * b_chunks
    elif mode == "nt":
        M, K = a.shape
        N = b.shape[-2]
        assert b.shape[-1] * b_chunks == K
    else:
        K, M = a.shape
        N = b.shape[-1] * b_chunks
    tm = tm or _pick(M, (2048, 1024, 512, 256, 128))
    tn = tn or _pick(N // max(b_chunks if mode != "nt" else 1, out_chunks), (512, 640, 384, 256, 128))
    if tk is None:
        kmax = b.shape[-1] if mode == "nt" else K
        tk = kmax if kmax <= 2048 else _pick(kmax, (2048, 1152, 1024, 512))
    nk = K // tk
    assert M % tm == 0 and N % tn == 0 and K % tk == 0
    grid = (M // tm, N // tn, nk)

    if mode == "tn":
        a_spec = pl.BlockSpec((tk, tm), lambda i, j, k: (k, i))
    else:
        a_spec = pl.BlockSpec((tm, tk), lambda i, j, k: (i, k))

    def b_index(t_row, t_last, tile_last):
        if b_chunks == 1:
            return (t_row, t_last)
        q = (b.shape[-1]) // tile_last
        return (t_last // q, t_row, t_last % q)

    if mode == "nn" or mode == "tn":
        bshape = (tk, tn)
        bmap = lambda i, j, k: b_index(k, j, tn)
    else:
        bshape = (tn, tk)
        bmap = lambda i, j, k: b_index(j, k, tk)
    if b_chunks > 1:
        bshape = (None,) + bshape
    b_spec = pl.BlockSpec(bshape, bmap)

    if out_chunks == 1:
        o_spec = pl.BlockSpec((tm, tn), lambda i, j, k: (i, j))
        o_shape = (M, N)
    else:
        qo = (N // out_chunks) // tn
        o_spec = pl.BlockSpec((None, tm, tn), lambda i, j, k: (j // qo, i, j % qo))
        o_shape = (out_chunks, M, N // out_chunks)
    e_spec = pl.BlockSpec((tm, tn), lambda i, j, k: (i, j))

    dims = {"nn": (((1,), (0,)), ((), ())), "nt": (((1,), (1,)), ((), ())), "tn": (((0,), (0,)), ((), ()))}[mode]
    n_ex = len(extras)
    n_out = len(out_dtypes)

    def body(*refs):
        a_ref, b_ref = refs[0], refs[1]
        ex_refs = refs[2:2 + n_ex]
        o_refs = refs[2 + n_ex:2 + n_ex + n_out]

        def finish(acc):
            outs = epilogue(acc, *[r[...] for r in ex_refs]) if epilogue is not None else (acc,)
            for r, o in zip(o_refs, outs):
                r[...] = o.astype(r.dtype)

        part = lax.dot_general(a_ref[...].astype(BF16), b_ref[...].astype(BF16), dims,
                               preferred_element_type=F32)
        if nk == 1:
            finish(part)
        else:
            acc_ref = refs[-1]
            k = pl.program_id(2)

            @pl.when(k == 0)
            def _():
                acc_ref[...] = part

            @pl.when(k > 0)
            def _():
                acc_ref[...] += part

            @pl.when(k == nk - 1)
            def _():
                finish(acc_ref[...])

    outs = pl.pallas_call(
        body,
        grid=grid,
        in_specs=[a_spec, b_spec] + [e_spec] * n_ex,
        out_specs=[o_spec] * n_out,
        out_shape=[jax.ShapeDtypeStruct(o_shape, d) for d in out_dtypes],
        scratch_shapes=[pltpu.VMEM((tm, tn), F32)] if nk > 1 else [],
        compiler_params=_params(("parallel", "parallel", "arbitrary")),
        name=name,
    )(a, b, *extras)
    return outs[0] if n_out == 1 else outs


def _rms(x, g):
    r = lax.rsqrt(jnp.mean(x * x, axis=-1, keepdims=True) + EPS)
    return x * r * g


def _rmsnorm_fwd(x, g, name):
    R, D = x.shape
    tr = _pick(R, (512, 256))

    def body(x_ref, g_ref, o_ref):
        o_ref[...] = _rms(x_ref[...], g_ref[...]).astype(o_ref.dtype)

    return pl.pallas_call(
        body, grid=(R // tr,),
        in_specs=[pl.BlockSpec((tr, D), lambda i: (i, 0)), pl.BlockSpec((1, D), lambda i: (0, 0))],
        out_specs=pl.BlockSpec((tr, D), lambda i: (i, 0)),
        out_shape=jax.ShapeDtypeStruct((R, D), BF16),
        compiler_params=_params(("parallel",)), name=name)(x, g)


def _rmsnorm_bwd(x, g, dh, dres, name):
    R, D = x.shape
    tr = _pick(R, (256,))
    has_res = dres is not None

    def body(*refs):
        if has_res:
            x_ref, g_ref, dh_ref, dres_ref, dx_ref, dg_ref = refs
        else:
            x_ref, g_ref, dh_ref, dx_ref, dg_ref = refs
        _, vjp = jax.vjp(_rms, x_ref[...], g_ref[...])
        dx, dg = vjp(dh_ref[...])
        if has_res:
            dx = dx + dres_ref[...]
        dx_ref[...] = dx

        @pl.when(pl.program_id(0) == 0)
        def _():
            dg_ref[...] = jnp.zeros_like(dg_ref)

        dg_ref[...] += dg

    row = pl.BlockSpec((tr, D), lambda i: (i, 0))
    vec = pl.BlockSpec((1, D), lambda i: (0, 0))
    ins = [x, g, dh] + ([dres] if has_res else [])
    return pl.pallas_call(
        body, grid=(R // tr,),
        in_specs=[row, vec, row] + ([row] if has_res else []),
        out_specs=[row, vec],
        out_shape=[jax.ShapeDtypeStruct((R, D), F32), jax.ShapeDtypeStruct((1, D), F32)],
        compiler_params=_params(("arbitrary",)), name=name)(*ins)


def _shift_down(u, k):
    if k == 0:
        return u
    rows = lax.broadcasted_iota(jnp.int32, u.shape, 0)
    return jnp.where(rows >= k, pltpu.roll(u, k, axis=0), 0.0)


def _shift_up(u, k):
    if k == 0:
        return u
    n = u.shape[0]
    rows = lax.broadcasted_iota(jnp.int32, u.shape, 0)
    return jnp.where(rows < n - k, pltpu.roll(u, n - k, axis=0), 0.0)


def _conv_pre(u, w, b):
    pre = b
    for j in range(CONV_WIDTH):
        pre = pre + w[j:j + 1, :] * _shift_down(u, CONV_WIDTH - 1 - j)
    return pre


def _conv_fwd(proj, col0, ncols, conv_w, conv_b):
    S = proj.shape[0]
    cb0 = col0 // LANES

    def body(u_ref, w_ref, b_ref, o_ref):
        pre = _conv_pre(u_ref[...], w_ref[...], b_ref[...])
        o_ref[...] = pre * jax.nn.sigmoid(pre)

    return pl.pallas_call(
        body, grid=(ncols // LANES,),
        in_specs=[pl.BlockSpec((S, LANES), lambda j: (0, j + cb0)),
                  pl.BlockSpec((CONV_WIDTH, LANES), lambda j: (0, j)),
                  pl.BlockSpec((1, LANES), lambda j: (0, j))],
        out_specs=pl.BlockSpec((S, LANES), lambda j: (0, j)),
        out_shape=jax.ShapeDtypeStruct((S, ncols), F32),
        compiler_params=_params(("parallel",)), name="conv_fwd")(proj, conv_w, conv_b)


def _conv_bwd(proj, col0, ncols, conv_w, conv_b, dout):
    S = proj.shape[0]
    cb0 = col0 // LANES

    def body(u_ref, w_ref, b_ref, d_ref, du_ref, dw_ref, db_ref):
        u = u_ref[...]
        w = w_ref[...]
        pre = _conv_pre(u, w, b_ref[...])
        s = jax.nn.sigmoid(pre)
        dpre = d_ref[...] * (s * (1.0 + pre * (1.0 - s)))
        du = jnp.zeros_like(u)
        rows = []
        for j in range(CONV_WIDTH):
            k = CONV_WIDTH - 1 - j
            du = du + w[j:j + 1, :] * _shift_up(dpre, k)
            rows.append(jnp.sum(dpre * _shift_down(u, k), axis=0, keepdims=True))
        du_ref[...] = du.astype(du_ref.dtype)
        rows.append(jnp.zeros((8 - CONV_WIDTH, LANES), F32))
        dw_ref[...] = jnp.concatenate(rows, axis=0)
        db_ref[...] = jnp.sum(dpre, axis=0, keepdims=True)

    return pl.pallas_call(
        body, grid=(ncols // LANES,),
        in_specs=[pl.BlockSpec((S, LANES), lambda j: (0, j + cb0)),
                  pl.BlockSpec((CONV_WIDTH, LANES), lambda j: (0, j)),
                  pl.BlockSpec((1, LANES), lambda j: (0, j)),
                  pl.BlockSpec((S, LANES), lambda j: (0, j))],
        out_specs=[pl.BlockSpec((S, LANES), lambda j: (0, j)),
                   pl.BlockSpec((8, LANES), lambda j: (0, j)),
                   pl.BlockSpec((1, LANES), lambda j: (0, j))],
        out_shape=[jax.ShapeDtypeStruct((S, ncols), BF16),
                   jax.ShapeDtypeStruct((8, ncols), F32),
                   jax.ShapeDtypeStruct((1, ncols), F32)],
        compiler_params=_params(("parallel",)), name="conv_bwd")(proj, conv_w, conv_b, dout)


def _softplus(x):
    return jnp.maximum(x, 0.0) + jnp.log1p(jnp.exp(-jnp.abs(x)))


def _dot32(a, b, dims=(((1,), (0,)), ((), ()))):
    return lax.dot_general(a, b, dims, precision=HI, preferred_element_type=F32)


def _dotd(a, b, dims=(((1,), (0,)), ((), ()))):
    return lax.dot_general(a, b, dims, preferred_element_type=F32)


def _ssd_chunk(xs, Bm, Cm, z, dtr, dtb, alog, dsk, nw, h):
    L = Bm.shape[0]
    ri = lax.broadcasted_iota(jnp.int32, (L, L), 0)
    ci = lax.broadcasted_iota(jnp.int32, (L, L), 1)
    causal = ri >= ci
    tril = causal.astype(F32)
    CB = _dotd(Cm, Bm, (((1,), (1,)), ((), ())))
    gated, hnew = [], []
    ssq = jnp.zeros((L, 1), F32)
    for r in range(len(xs)):
        dt = _softplus(dtr[r] + dtb[r])
        dA = dt * (-jnp.exp(alog[r]))
        acs = _dot32(tril, dA)
        tot = jnp.sum(dA, axis=0, keepdims=True)
        cc = jnp.broadcast_to(acs, (L, L))
        seg = jnp.where(causal, cc - cc.T, -1e30)
        Lmat = jnp.exp(seg)
        X = xs[r] * dt
        y = _dotd(CB * Lmat, X) + jnp.exp(acs) * _dotd(Cm, h[r]) + dsk[r] * xs[r]
        hnew.append(jnp.exp(tot) * h[r] + _dotd(Bm, X * jnp.exp(tot - acs), (((0,), (0,)), ((), ()))))
        g = y * (z[r] * jax.nn.sigmoid(z[r]))
        ssq = ssq + jnp.sum(g * g, axis=-1, keepdims=True)
        gated.append(g)
    rs = lax.rsqrt(ssq / (len(xs) * xs[0].shape[-1]) + EPS)
    return [g * rs * nw[r] for r, g in enumerate(gated)], hnew


def _ssd_specs(S):
    H, P, N, L = HEADS_PER_GROUP, HEAD_DIM, SSM_STATE, CHUNK
    return dict(
        head=lambda rev: pl.BlockSpec((H, L, P), (lambda g, c: (g, rev(c), 0))),
        bc=lambda rev, off: pl.BlockSpec((L, N), (lambda g, c: (rev(c), off + g))),
        dt=lambda rev: pl.BlockSpec((H, L, 1), (lambda g, c: (g, rev(c), 0))),
        scal=pl.BlockSpec((H, 1, 1), lambda g, c: (g, 0, 0)),
        nw=pl.BlockSpec((H, 1, P), lambda g, c: (g, 0, 0)),
        hs=lambda rev: pl.BlockSpec((None, H, N, P), (lambda g, c: (rev(c), g, 0, 0))),
    )


def _ssd_fwd(xs_hm, xbc, z_hm, dt_hm, dtb, alog, dsk, nw_hm):
    S = xbc.shape[0]
    H, P, N, L = HEADS_PER_GROUP, HEAD_DIM, SSM_STATE, CHUNK
    nc = S // L
    sp = _ssd_specs(S)
    ident = lambda c: c
    xoff = (SSM_HEADS * HEAD_DIM) // LANES

    def body(xs_ref, b_ref, c_ref, z_ref, dt_ref, dtb_ref, al_ref, dsk_ref, nw_ref, y_ref, hs_ref, h_ref):
        @pl.when(pl.program_id(1) == 0)
        def _():
            h_ref[...] = jnp.zeros_like(h_ref)

        hs_ref[...] = h_ref[...]
        hd = range(H)
        out, hnew = _ssd_chunk([xs_ref[r] for r in hd], b_ref[...], c_ref[...], [z_ref[r] for r in hd],
                               [dt_ref[r] for r in hd], [dtb_ref[r] for r in hd], [al_ref[r] for r in hd],
                               [dsk_ref[r] for r in hd], [nw_ref[r] for r in hd], [h_ref[r] for r in hd])
        for r in hd:
            y_ref[r] = out[r]
            h_ref[r] = hnew[r]

    return pl.pallas_call(
        body, grid=(SSM_GROUPS, nc),
        in_specs=[sp["head"](ident), sp["bc"](ident, xoff), sp["bc"](ident, xoff + SSM_GROUPS), sp["head"](ident),
                  sp["dt"](ident), sp["scal"], sp["scal"], sp["scal"], sp["nw"]],
        out_specs=[sp["head"](ident), sp["hs"](ident)],
        out_shape=[jax.ShapeDtypeStruct((SSM_HEADS, S, P), F32),
                   jax.ShapeDtypeStruct((nc, SSM_HEADS, N, P), F32)],
        scratch_shapes=[pltpu.VMEM((H, N, P), F32)],
        compiler_params=_params(("parallel", "arbitrary")), name="ssd_fwd",
    )(xs_hm, xbc, xbc, z_hm, dt_hm, dtb, alog, dsk, nw_hm)


def _ssd_bwd(xs_hm, xbc, z_hm, dt_hm, dtb, alog, dsk, nw_hm, hs, dy_hm):
    S = xbc.shape[0]
    H, P, N, L = HEADS_PER_GROUP, HEAD_DIM, SSM_STATE, CHUNK
    nc = S // L
    sp = _ssd_specs(S)
    rev = lambda c: nc - 1 - c
    xoff = (SSM_HEADS * HEAD_DIM) // LANES

    def body(xs_ref, b_ref, c_ref, z_ref, dt_ref, dtb_ref, al_ref, dsk_ref, nw_ref, hs_ref, dy_ref,
             dxs_ref, dz_ref, db_ref, dc_ref, ddt_ref, ddtb_ref, dal_ref, ddsk_ref, dnw_ref, dh_ref):
        first = pl.program_id(1) == 0

        @pl.when(first)
        def _():
            dh_ref[...] = jnp.zeros_like(dh_ref)
            ddtb_ref[...] = jnp.zeros_like(ddtb_ref)
            dal_ref[...] = jnp.zeros_like(dal_ref)
            ddsk_ref[...] = jnp.zeros_like(ddsk_ref)
            dnw_ref[...] = jnp.zeros_like(dnw_ref)

        hd = range(H)
        args = ([xs_ref[r] for r in hd], b_ref[...], c_ref[...], [z_ref[r] for r in hd],
                [dt_ref[r] for r in hd], [dtb_ref[r] for r in hd], [al_ref[r] for r in hd],
                [dsk_ref[r] for r in hd], [nw_ref[r] for r in hd], [hs_ref[r] for r in hd])
        _, vjp = jax.vjp(_ssd_chunk, *args)
        dxs, dB, dC, dz, ddt, ddtb, dal, ddsk, dnw, dh = vjp(([dy_ref[r] for r in hd], [dh_ref[r] for r in hd]))
        db_ref[...] = dB
        dc_ref[...] = dC
        for r in hd:
            dxs_ref[r] = dxs[r]
            dz_ref[r] = dz[r]
            ddt_ref[r] = ddt[r]
            dh_ref[r] = dh[r]
            ddtb_ref[r] += ddtb[r]
            dal_ref[r] += dal[r]
            ddsk_ref[r] += ddsk[r]
            dnw_ref[r] += dnw[r]

    bc_out = lambda: pl.BlockSpec((L, N), lambda g, c: (rev(c), g))
    return pl.pallas_call(
        body, grid=(SSM_GROUPS, nc),
        in_specs=[sp["head"](rev), sp["bc"](rev, xoff), sp["bc"](rev, xoff + SSM_GROUPS), sp["head"](rev),
                  sp["dt"](rev), sp["scal"], sp["scal"], sp["scal"], sp["nw"], sp["hs"](rev), sp["head"](rev)],
        out_specs=[sp["head"](rev), sp["head"](rev), bc_out(), bc_out(), sp["dt"](rev),
                   sp["scal"], sp["scal"], sp["scal"], sp["nw"]],
        out_shape=[jax.ShapeDtypeStruct((SSM_HEADS, S, P), F32), jax.ShapeDtypeStruct((SSM_HEADS, S, P), F32),
                   jax.ShapeDtypeStruct((S, SSM_GROUPS * N), F32), jax.ShapeDtypeStruct((S, SSM_GROUPS * N), F32),
                   jax.ShapeDtypeStruct((SSM_HEADS, S, 1), F32),
                   jax.ShapeDtypeStruct((SSM_HEADS, 1, 1), F32), jax.ShapeDtypeStruct((SSM_HEADS, 1, 1), F32),
                   jax.ShapeDtypeStruct((SSM_HEADS, 1, 1), F32), jax.ShapeDtypeStruct((SSM_HEADS, 1, P), F32)],
        scratch_shapes=[pltpu.VMEM((H, N, P), F32)],
        compiler_params=_params(("parallel", "arbitrary")), name="ssd_bwd",
    )(xs_hm, xbc, xbc, z_hm, dt_hm, dtb, alog, dsk, nw_hm, hs, dy_hm)


ATTN_SCALE = HEAD_DIM ** -0.5
ATTN_PAIRS = ATTN_HEADS // 2


def _first_head(rows):
    return lax.broadcasted_iota(jnp.int32, (rows, LANES), 1) < HEAD_DIM


def _pair_norm(x, g2, scale):
    first = _first_head(x.shape[0])
    sq = x * x
    ms0 = jnp.sum(jnp.where(first, sq, 0.0), axis=-1, keepdims=True) * (1.0 / HEAD_DIM)
    ms1 = jnp.sum(jnp.where(first, 0.0, sq), axis=-1, keepdims=True) * (1.0 / HEAD_DIM)
    r = jnp.where(first, lax.rsqrt(ms0 + EPS), lax.rsqrt(ms1 + EPS))
    return x * r * g2 * scale


def _qk_prep_fwd(proj, gq2, gk2):
    S = proj.shape[0]
    tq = _pick(S, (512, 256))

    def body(q_ref, k_ref, v_ref, gq_ref, gk_ref, qo_ref, ko_ref, vo_ref):
        qo_ref[...] = _pair_norm(q_ref[...], gq_ref[...], ATTN_SCALE).astype(BF16)
        ko_ref[...] = _pair_norm(k_ref[...], gk_ref[...], 1.0).astype(BF16)
        vo_ref[...] = v_ref[...].astype(BF16)

    col = lambda c0: pl.BlockSpec((tq, LANES), lambda h, i: (i, c0 // LANES + h))
    blk = pl.BlockSpec((tq, LANES), lambda h, i: (i, h))
    vec = pl.BlockSpec((1, LANES), lambda h, i: (0, 0))
    return pl.pallas_call(
        body, grid=(ATTN_PAIRS, S // tq), in_specs=[col(COL_Q), col(COL_K), col(COL_V), vec, vec],
        out_specs=[blk, blk, blk], out_shape=[jax.ShapeDtypeStruct((S, ATTN_WIDTH), BF16)] * 3,
        compiler_params=_params(("parallel", "parallel")), name="qk_prep_fwd")(proj, proj, proj, gq2, gk2)


def _qk_prep_bwd(proj, gq2, gk2, dqs, dkn):
    S = proj.shape[0]
    tq = _pick(S, (512, 256))

    def body(q_ref, k_ref, gq_ref, gk_ref, dqs_ref, dkn_ref, dq_ref, dk_ref, dgq_ref, dgk_ref):
        @pl.when((pl.program_id(0) == 0) & (pl.program_id(1) == 0))
        def _():
            dgq_ref[...] = jnp.zeros_like(dgq_ref)
            dgk_ref[...] = jnp.zeros_like(dgk_ref)

        _, vq = jax.vjp(lambda q, g: _pair_norm(q, g, ATTN_SCALE), q_ref[...], gq_ref[...])
        dq, dgq = vq(dqs_ref[...])
        _, vk = jax.vjp(lambda k, g: _pair_norm(k, g, 1.0), k_ref[...], gk_ref[...])
        dk, dgk = vk(dkn_ref[...])
        dq_ref[...] = dq.astype(dq_ref.dtype)
        dk_ref[...] = dk.astype(dk_ref.dtype)
        dgq_ref[...] += dgq
        dgk_ref[...] += dgk

    col = lambda c0: pl.BlockSpec((tq, LANES), lambda h, i: (i, c0 // LANES + h))
    blk = pl.BlockSpec((tq, LANES), lambda h, i: (i, h))
    vec = pl.BlockSpec((1, LANES), lambda h, i: (0, 0))
    return pl.pallas_call(
        body, grid=(ATTN_PAIRS, S // tq), in_specs=[col(COL_Q), col(COL_K), vec, vec, blk, blk],
        out_specs=[blk, blk, vec, vec],
        out_shape=[jax.ShapeDtypeStruct((S, ATTN_WIDTH), BF16)] * 2 + [jax.ShapeDtypeStruct((1, LANES), F32)] * 2,
        compiler_params=_params(("arbitrary", "arbitrary")), name="qk_prep_bwd")(proj, proj, gq2, gk2, dqs, dkn)


def _logf_cumsum_fwd(f_raw, f_bias):
    S, Hh = f_raw.shape
    L = CHUNK

    def body(f_ref, b_ref, o_ref):
        ri = lax.broadcasted_iota(jnp.int32, (L, L), 0)
        ci = lax.broadcasted_iota(jnp.int32, (L, L), 1)
        tril = (ri >= ci).astype(F32)
        carry = jnp.zeros((1, Hh), F32)
        for c in range(S // L):
            lf = -_softplus(-(f_ref[c * L:(c + 1) * L, :] + b_ref[...]))
            cum = _dot32(tril, lf) + carry
            o_ref[c * L:(c + 1) * L, :] = cum
            carry = cum[L - 1:L, :]

    return pl.pallas_call(body, out_shape=jax.ShapeDtypeStruct((S, Hh), F32), name="logf_cumsum_fwd")(f_raw, f_bias)


def _logf_cumsum_bwd(f_raw, f_bias, dcum):
    S, Hh = f_raw.shape
    L = CHUNK

    def body(f_ref, b_ref, d_ref, df_ref, db_ref):
        ri = lax.broadcasted_iota(jnp.int32, (L, L), 0)
        ci = lax.broadcasted_iota(jnp.int32, (L, L), 1)
        triu = (ri <= ci).astype(F32)
        carry = jnp.zeros((1, Hh), F32)
        db = jnp.zeros((1, Hh), F32)
        for c in reversed(range(S // L)):
            suf = _dot32(triu, d_ref[c * L:(c + 1) * L, :]) + carry
            df = suf * jax.nn.sigmoid(-(f_ref[c * L:(c + 1) * L, :] + b_ref[...]))
            df_ref[c * L:(c + 1) * L, :] = df
            db = db + jnp.sum(df, axis=0, keepdims=True)
            carry = suf[0:1, :]
        db_ref[...] = db

    return pl.pallas_call(
        body, out_shape=[jax.ShapeDtypeStruct((S, Hh), F32), jax.ShapeDtypeStruct((1, Hh), F32)],
        name="logf_cumsum_bwd")(f_raw, f_bias, dcum)


_NT = (((1,), (1,)), ((), ()))
_TN = (((0,), (0,)), ((), ()))


def _mxu(a, b, dims=(((1,), (0,)), ((), ()))):
    return lax.dot_general(a, b, dims, preferred_element_type=F32)


def _flash_fwd(qs, kn, vb, cq, ck):
    S, W = qs.shape
    tq = tk = _pick(S, (512, 256))
    nmask = max(tq // tk, 1)

    def body(q_ref, k_ref, v_ref, cq_ref, ck_ref, o_ref, of_ref, lse_ref):
        i = pl.program_id(1)
        first = _first_head(tq)
        q2 = q_ref[...]
        zero = jnp.zeros_like(q2)
        qa = (jnp.where(first, q2, zero), jnp.where(first, zero, q2))
        cqa = (cq_ref[0], cq_ref[1])
        row0 = i * tq

        def step(j, carry, masked):
            ms, ls, acc, rem = carry
            off = pl.multiple_of(j * tk, tk)
            k = k_ref[pl.ds(off, tk), :]
            v = v_ref[pl.ds(off, tk), :]
            new_m, new_l, alphas, pvs, prs = [], [], [], [], []
            for a in range(2):
                s = _mxu(qa[a], k, _NT) + cqa[a] - ck_ref[a, :, pl.ds(off, tk)]
                if masked:
                    ri = lax.broadcasted_iota(jnp.int32, (tq, tk), 0) + row0
                    ci = lax.broadcasted_iota(jnp.int32, (tq, tk), 1) + off
                    s = jnp.where(ri >= ci, s, -1e30)
                m_new = jnp.maximum(ms[a], jnp.max(s, axis=-1, keepdims=True))
                alpha = jnp.exp(ms[a] - m_new)
                p = jnp.exp(s - m_new)
                new_l.append(alpha * ls[a] + jnp.sum(p, axis=-1, keepdims=True))
                new_m.append(m_new)
                alphas.append(alpha)
                p_hi = p.astype(BF16)
                pvs.append(_mxu(p_hi, v))
                prs.append(_mxu((p - p_hi.astype(F32)).astype(BF16), v))
            al = jnp.where(first, alphas[0], alphas[1])
            acc = al * acc + jnp.where(first, pvs[0], pvs[1])
            rem = al * rem + jnp.where(first, prs[0], prs[1])
            return tuple(new_m), tuple(new_l), acc, rem

        neg = jnp.full((tq, 1), -1e30, F32)
        z1 = jnp.zeros((tq, 1), F32)
        z2 = jnp.zeros((tq, LANES), F32)
        carry = ((neg, neg), (z1, z1), z2, z2)
        n_full = (i * tq) // tk
        carry = lax.fori_loop(0, n_full, lambda j, c: step(j, c, False), carry)
        for jj in range(nmask):
            carry = step(n_full + jj, carry, True)
        ms, ls, acc, rem = carry
        linv = jnp.where(first, 1.0 / ls[0], 1.0 / ls[1])
        o_ref[...] = (acc * linv).astype(o_ref.dtype)
        of_ref[...] = (acc + rem) * linv
        lse_ref[0] = ms[0] + jnp.log(ls[0])
        lse_ref[1] = ms[1] + jnp.log(ls[1])

    qblk = pl.BlockSpec((tq, LANES), lambda h, i: (i, h))
    full = pl.BlockSpec((S, LANES), lambda h, i: (0, h))
    colb = pl.BlockSpec((2, tq, 1), lambda h, i: (h, i, 0))
    return pl.pallas_call(
        body, grid=(W // LANES, S // tq),
        in_specs=[qblk, full, full, colb, pl.BlockSpec((2, 1, S), lambda h, i: (h, 0, 0))],
        out_specs=[qblk, qblk, colb],
        out_shape=[jax.ShapeDtypeStruct((S, W), BF16), jax.ShapeDtypeStruct((S, W), F32),
                   jax.ShapeDtypeStruct((2 * (W // LANES), S, 1), F32)],
        compiler_params=_params(("parallel", "parallel")), name="flash_fwd")(qs, kn, vb, cq, ck)


def _flash_bwd(qs, kn, vb, cq, ck, o_fine, do, do_col0, lse):
    S, W = qs.shape
    tq = tk = _pick(S, (512, 256))
    nq = S // tq
    nmask = max(tk // tq, 1)

    def body(q_ref, k_ref, v_ref, cq_ref, ck_ref, of_ref, do_ref, lse_ref, dq_ref, dk_ref, dv_ref, dck_ref):
        j = pl.program_id(1)

        @pl.when(j == 0)
        def _():
            dq_ref[...] = jnp.zeros_like(dq_ref)

        firstk = _first_head(tk)
        firstq = _first_head(tq)
        k2 = k_ref[...]
        v2 = v_ref[...]
        zk = jnp.zeros_like(k2)
        ka = (jnp.where(firstk, k2, zk), jnp.where(firstk, zk, k2))
        va = (jnp.where(firstk, v2, zk), jnp.where(firstk, zk, v2))
        cka = (ck_ref[0], ck_ref[1])
        col0 = j * tk

        def step(i, carry, masked):
            dk, dv, dck0, dck1 = carry
            dcks = [dck0, dck1]
            off = pl.multiple_of(i * tq, tq)
            rows = pl.ds(off, tq)
            q2 = q_ref[rows, :]
            dob = do_ref[rows, :].astype(BF16)
            prod = dob.astype(F32) * of_ref[rows, :]
            dkp, dvp, dqp = [], [], []
            for a in range(2):
                s = _mxu(q2, ka[a], _NT) + cq_ref[a, rows, :] - cka[a]
                if masked:
                    ri = lax.broadcasted_iota(jnp.int32, (tq, tk), 0) + off
                    ci = lax.broadcasted_iota(jnp.int32, (tq, tk), 1) + col0
                    s = jnp.where(ri >= ci, s, -1e30)
                p = jnp.exp(s - lse_ref[a, rows, :])
                dp = _mxu(dob, va[a], _NT)
                own = jnp.where(firstq, prod, 0.0) if a == 0 else jnp.where(firstq, 0.0, prod)
                ds = p * (dp - jnp.sum(own, axis=-1, keepdims=True))
                dsb = ds.astype(BF16)
                dvp.append(_mxu(p.astype(BF16), dob, _TN))
                dkp.append(_mxu(dsb, q2, _TN))
                dqp.append(_mxu(dsb, k2))
                dcks[a] = dcks[a] - jnp.sum(ds, axis=0, keepdims=True)
            dq_ref[rows, :] += jnp.where(firstq, dqp[0], dqp[1])
            dk = dk + jnp.where(firstk, dkp[0], dkp[1])
            dv = dv + jnp.where(firstk, dvp[0], dvp[1])
            return dk, dv, dcks[0], dcks[1]

        z2 = jnp.zeros((tk, LANES), F32)
        z1 = jnp.zeros((1, tk), F32)
        carry = (z2, z2, z1, z1)
        i0 = (j * tk) // tq
        for ii in range(nmask):
            carry = step(i0 + ii, carry, True)
        dk, dv, dck0, dck1 = lax.fori_loop(i0 + nmask, nq, lambda i, c: step(i, c, False), carry)
        dk_ref[...] = dk
        dv_ref[...] = dv.astype(dv_ref.dtype)
        dck_ref[0] = dck0
        dck_ref[1] = dck1

    kblk = pl.BlockSpec((tk, LANES), lambda h, j: (j, h))
    full = pl.BlockSpec((S, LANES), lambda h, j: (0, h))
    dofull = pl.BlockSpec((S, LANES), lambda h, j: (0, do_col0 // LANES + h))
    col = pl.BlockSpec((2, S, 1), lambda h, j: (h, 0, 0))
    rowt = pl.BlockSpec((2, 1, tk), lambda h, j: (h, 0, j))
    return pl.pallas_call(
        body, grid=(W // LANES, S // tk),
        in_specs=[full, kblk, kblk, col, rowt, full, dofull, col],
        out_specs=[full, kblk, kblk, rowt],
        out_shape=[jax.ShapeDtypeStruct((S, W), F32), jax.ShapeDtypeStruct((S, W), F32),
                   jax.ShapeDtypeStruct((S, W), BF16), jax.ShapeDtypeStruct((2 * (W // LANES), 1, S), F32)],
        compiler_params=_params(("parallel", "arbitrary")), name="flash_bwd")(qs, kn, vb, cq, ck, o_fine, do, lse)


XATTN_SCALE = XATTN_DIM ** -0.5


def _xq_norm(q, g):
    return _rms(q, g) * XATTN_SCALE


def _xattn_fwd(xq, kv, gq, gk):
    S = xq.shape[0]
    Mm = kv.shape[0]
    Dh = XATTN_DIM
    tq = _pick(S, (512, 256))

    def body(q_ref, k_ref, v_ref, gq_ref, gk_ref, o_ref):
        qn = _xq_norm(q_ref[...], gq_ref[...]).astype(BF16)
        kn = _rms(k_ref[...], gk_ref[...]).astype(BF16)
        s = _mxu(qn, kn, _NT)
        m = jnp.max(s, axis=-1, keepdims=True)
        p = jnp.exp(s - m)
        l = jnp.sum(p, axis=-1, keepdims=True)
        o_ref[...] = (_mxu(p.astype(BF16), v_ref[...].astype(BF16)) / l).astype(o_ref.dtype)

    vec = pl.BlockSpec((1, Dh), lambda h, i: (0, 0))
    return pl.pallas_call(
        body, grid=(XATTN_HEADS, S // tq),
        in_specs=[pl.BlockSpec((tq, Dh), lambda h, i: (i, h)), pl.BlockSpec((Mm, Dh), lambda h, i: (0, h)),
                  pl.BlockSpec((Mm, Dh), lambda h, i: (0, XATTN_HEADS + h)), vec, vec],
        out_specs=pl.BlockSpec((tq, Dh), lambda h, i: (i, h)),
        out_shape=jax.ShapeDtypeStruct((S, XATTN_HEADS * Dh), BF16),
        compiler_params=_params(("parallel", "parallel")), name="xattn_fwd")(xq, kv, kv, gq, gk)


def _xattn_bwd(xq, kv, gq, gk, do):
    S = xq.shape[0]
    Mm = kv.shape[0]
    Dh = XATTN_DIM
    tq = _pick(S, (512, 256))
    nq = S // tq

    def body(q_ref, k_ref, v_ref, gq_ref, gk_ref, do_ref, dq_ref, dk_ref, dv_ref, dgq_ref, dgk_ref, dkn_acc, dv_acc):
        h = pl.program_id(0)
        i = pl.program_id(1)

        @pl.when((h == 0) & (i == 0))
        def _():
            dgq_ref[...] = jnp.zeros_like(dgq_ref)
            dgk_ref[...] = jnp.zeros_like(dgk_ref)

        @pl.when(i == 0)
        def _():
            dkn_acc[...] = jnp.zeros_like(dkn_acc)
            dv_acc[...] = jnp.zeros_like(dv_acc)

        qn32, vq = jax.vjp(_xq_norm, q_ref[...], gq_ref[...])
        kn32, vk = jax.vjp(_rms, k_ref[...], gk_ref[...])
        qn = qn32.astype(BF16)
        kn = kn32.astype(BF16)
        vb = v_ref[...].astype(BF16)
        s = _mxu(qn, kn, _NT)
        m = jnp.max(s, axis=-1, keepdims=True)
        p = jnp.exp(s - m)
        p = p / jnp.sum(p, axis=-1, keepdims=True)
        dob = do_ref[...].astype(BF16)
        dp = _mxu(dob, vb, _NT)
        delta = jnp.sum(p * dp, axis=-1, keepdims=True)
        ds = (p * (dp - delta)).astype(BF16)
        dv_acc[...] += _mxu(p.astype(BF16), dob, _TN)
        dkn_acc[...] += _mxu(ds, qn, _TN)
        dq, dgq = vq(_mxu(ds, kn))
        dq_ref[...] = dq.astype(dq_ref.dtype)
        dgq_ref[...] += dgq

        @pl.when(i == nq - 1)
        def _():
            dk, dgk = vk(dkn_acc[...])
            dk_ref[...] = dk.astype(dk_ref.dtype)
            dv_ref[...] = dv_acc[...].astype(dv_ref.dtype)
            dgk_ref[...] += dgk

    vec = pl.BlockSpec((1, Dh), lambda h, i: (0, 0))
    qblk = pl.BlockSpec((tq, Dh), lambda h, i: (i, h))
    kblk = pl.BlockSpec((Mm, Dh), lambda h, i: (0, h))
    vblk = pl.BlockSpec((Mm, Dh), lambda h, i: (0, XATTN_HEADS + h))
    return pl.pallas_call(
        body, grid=(XATTN_HEADS, nq),
        in_specs=[qblk, kblk, vblk, vec, vec, qblk],
        out_specs=[qblk, kblk, kblk, vec, vec],
        out_shape=[jax.ShapeDtypeStruct((S, XATTN_HEADS * Dh), BF16),
                   jax.ShapeDtypeStruct((Mm, XATTN_HEADS * Dh), BF16),
                   jax.ShapeDtypeStruct((Mm, XATTN_HEADS * Dh), BF16),
                   jax.ShapeDtypeStruct((1, Dh), F32), jax.ShapeDtypeStruct((1, Dh), F32)],
        scratch_shapes=[pltpu.VMEM((Mm, Dh), F32), pltpu.VMEM((Mm, Dh), F32)],
        compiler_params=_params(("arbitrary", "arbitrary")), name="xattn_bwd")(xq, kv, kv, gq, gk, do)


def _loss_head(y, target):
    S, D = y.shape
    tr = _pick(S, (512, 256))

    def body(y_ref, t_ref, dy_ref, loss_ref):
        @pl.when(pl.program_id(0) == 0)
        def _():
            loss_ref[...] = jnp.zeros_like(loss_ref)

        err = y_ref[...] - t_ref[...]
        dy_ref[...] = err * (1.0 / D)
        loss_ref[...] += jnp.sum(err * err) * (0.5 / D)

    row = pl.BlockSpec((tr, D), lambda i: (i, 0))
    return pl.pallas_call(
        body, grid=(S // tr,), in_specs=[row, row],
        out_specs=[row, pl.BlockSpec((1, LANES), lambda i: (0, 0))],
        out_shape=[jax.ShapeDtypeStruct((S, D), F32), jax.ShapeDtypeStruct((1, LANES), F32)],
        compiler_params=_params(("arbitrary",)), name="loss_head")(y, target)


def _row_tile(R, C):
    for tr in (1024, 512, 256, 128, 64, 32, 16, 8):
        if R % tr == 0 and tr * C * 4 <= (1 << 20):
            return tr
    return R


def _nsum(arrs, out_dtypes, name):
    R, C = arrs[0].shape
    tr = _row_tile(R, C)
    n = len(arrs)

    def body(*refs):
        acc = refs[0][...].astype(F32)
        for r in refs[1:n]:
            acc = acc + r[...].astype(F32)
        for o in refs[n:]:
            o[...] = acc.astype(o.dtype)

    blk = pl.BlockSpec((tr, C), lambda i: (i, 0))
    outs = pl.pallas_call(
        body, grid=(R // tr,), in_specs=[blk] * n, out_specs=[blk] * len(out_dtypes),
        out_shape=[jax.ShapeDtypeStruct((R, C), d) for d in out_dtypes],
        compiler_params=_params(("parallel",)), name=name)(*arrs)
    return outs


def _adamw(w, g, m, v, name):
    R, C = w.shape
    tr = _row_tile(R, C)
    c1 = 1.0 - ADAM_B1 ** ADAM_STEP
    c2 = 1.0 - ADAM_B2 ** ADAM_STEP

    def body(w_ref, g_ref, m_ref, v_ref, d_ref, mo_ref, vo_ref):
        g_t = g_ref[...]
        m_new = ADAM_B1 * m_ref[...] + (1.0 - ADAM_B1) * g_t
        v_new = ADAM_B2 * v_ref[...] + (1.0 - ADAM_B2) * (g_t * g_t)
        d_ref[...] = -ADAM_LR * ((m_new / c1) / (jnp.sqrt(v_new / c2) + ADAM_EPS) + ADAM_WD * w_ref[...])
        mo_ref[...] = m_new
        vo_ref[...] = v_new

    blk = pl.BlockSpec((tr, C), lambda i: (i, 0))
    return pl.pallas_call(
        body, grid=(R // tr,), in_specs=[blk] * 4, out_specs=[blk] * 3,
        out_shape=[jax.ShapeDtypeStruct((R, C), F32)] * 3,
        compiler_params=_params(("parallel",)), name=name)(w, g, m, v)


D_MODEL = 1024
SSM_INNER = SSM_HEADS * HEAD_DIM
CONV_DIM = SSM_INNER + 2 * SSM_GROUPS * SSM_STATE
ATTN_WIDTH = ATTN_HEADS * HEAD_DIM
COL_Z = 0
COL_XBC = COL_Z + SSM_INNER
COL_Q = COL_XBC + CONV_DIM
COL_K = COL_Q + ATTN_WIDTH
COL_V = COL_K + ATTN_WIDTH
COL_DT = COL_V + ATTN_WIDTH
COL_F = COL_DT + SSM_HEADS
IN_COLS = COL_F + ATTN_HEADS
IN_COLS_PAD = -(-IN_COLS // LANES) * LANES
REF_COL_DT = COL_Q


def _to_kernel_cols(w):
    return jnp.concatenate([w[:, :REF_COL_DT], w[:, REF_COL_DT + SSM_HEADS:COL_F], w[:, REF_COL_DT:REF_COL_DT + SSM_HEADS],
                            w[:, COL_F:IN_COLS]], axis=1)


def _to_reference_cols(w):
    return jnp.concatenate([w[:, :COL_Q], w[:, COL_DT:COL_DT + SSM_HEADS], w[:, COL_Q:COL_DT], w[:, COL_F:IN_COLS]],
                           axis=1)


def _to_heads(a):
    S = a.shape[0]
    return a.reshape(S, -1, HEAD_DIM).transpose(1, 0, 2)


def _from_heads(a):
    return a.transpose(1, 0, 2).reshape(a.shape[1], -1)


def _add_residual(acc, res):
    return (res + acc,)


def _relu2(acc):
    r = jnp.maximum(acc, 0.0)
    return acc, r * r


def _relu2_bwd(acc, a):
    return (acc * (2.0 * jnp.maximum(a, 0.0)),)


def _layer_fwd_bwd(x, mem, target, W, p):
    S = x.shape[0]
    hd3 = lambda a: a.reshape(SSM_HEADS, 1, 1)

    h1 = _rmsnorm_fwd(x, p["g_mix"], "norm_mix")
    proj = _mm(h1, W["w_in"], "nn", "in_proj")
    xbc = _conv_fwd(proj, COL_XBC, CONV_DIM, p["conv_w"], p["conv_b"])
    xs_hm = _to_heads(xbc[:, :SSM_INNER])
    z_hm = _to_heads(proj[:, COL_Z:COL_Z + SSM_INNER])
    dt_hm = proj[:, COL_DT:COL_DT + SSM_HEADS].T[:, :, None]
    ssd_par = (hd3(p["dt_bias"]), hd3(p["a_log"]), hd3(p["d_skip"]), p["ssm_norm_w"].reshape(SSM_HEADS, 1, HEAD_DIM))
    y_hm, hs = _ssd_fwd(xs_hm, xbc, z_hm, dt_hm, *ssd_par)
    f_raw = proj[:, COL_F:COL_F + ATTN_HEADS]
    gq2 = jnp.tile(p["g_q"], (1, 2))
    gk2 = jnp.tile(p["g_k"], (1, 2))
    qs, kn, vb = _qk_prep_fwd(proj, gq2, gk2)
    cum = _logf_cumsum_fwd(f_raw, p["f_bias"])
    cq = cum.T[:, :, None]
    ck = cum.T[:, None, :]
    o, o_fine, lse = _flash_fwd(qs, kn, vb, cq, ck)
    y = _from_heads(y_hm).astype(BF16)
    x1 = _mm(y, W["w_out"][:SSM_INNER], "nn", "out_proj_ssm", epilogue=_add_residual, extras=(x,))
    x1 = _mm(o, W["w_out"][SSM_INNER:], "nn", "out_proj_attn", epilogue=_add_residual, extras=(x1,))
    h2 = _rmsnorm_fwd(x1, p["g_xattn"], "norm_xattn")
    mem_n = _rmsnorm_fwd(mem, p["g_mem"], "norm_mem")
    xq = _mm(h2, W["xq_w"], "nn", "xq_proj")
    kv = _mm(mem_n, W["xkv_w"], "nn", "xkv_proj", b_chunks=N_CHIPS)
    xo = _xattn_fwd(xq, kv, p["xg_q"], p["xg_k"])
    x2 = _mm(xo, W["xo_w"], "nn", "xo_proj", epilogue=_add_residual, extras=(x1,))
    h3 = _rmsnorm_fwd(x2, p["g_mlp"], "norm_mlp")
    a, act = _mm(h3, W["w_up"], "nn", "mlp_up", out_dtypes=(F32, BF16), epilogue=_relu2, b_chunks=N_CHIPS)
    x3 = _mm(act, W["w_down"], "nn", "mlp_down", epilogue=_add_residual, extras=(x2,))
    dy, loss_row = _loss_head(x3, target)

    gW, gp = {}, {}
    da = _mm(dy, W["w_down"], "nt", "d_act", out_dtypes=(BF16,), epilogue=_relu2_bwd, extras=(a,))
    gW["w_down"] = _mm(act, dy, "tn", "g_w_down", out_dtypes=(BF16,))
    gW["w_up"] = _mm(h3, da, "tn", "g_w_up", out_dtypes=(BF16,), out_chunks=N_CHIPS)
    dh3 = _mm(da, W["w_up"], "nt", "d_h3", b_chunks=N_CHIPS)
    dx2, gp["g_mlp"] = _rmsnorm_bwd(x2, p["g_mlp"], dh3, dy, "norm_mlp_bwd")
    dxo = _mm(dx2, W["xo_w"], "nt", "d_xo", out_dtypes=(BF16,))
    gW["xo_w"] = _mm(xo, dx2, "tn", "g_xo_w", out_dtypes=(BF16,))
    dxq, dk_x, dv_x, gp["xg_q"], gp["xg_k"] = _xattn_bwd(xq, kv, p["xg_q"], p["xg_k"], dxo)
    dkv = jnp.concatenate([dk_x, dv_x], axis=-1)
    gW["xq_w"] = _mm(h2, dxq, "tn", "g_xq_w", out_dtypes=(BF16,))
    dh2 = _mm(dxq, W["xq_w"], "nt", "d_h2")
    gW["xkv_w"] = _mm(mem_n, dkv, "tn", "g_xkv_w", out_dtypes=(BF16,), out_chunks=N_CHIPS)
    dmem_n = _mm(dkv, W["xkv_w"], "nt", "d_mem_n", b_chunks=N_CHIPS)
    _, gp["g_mem"] = _rmsnorm_bwd(mem, p["g_mem"], dmem_n, None, "norm_mem_bwd")
    dx1, gp["g_xattn"] = _rmsnorm_bwd(x1, p["g_xattn"], dh2, dx2, "norm_xattn_bwd")
    dmixed = _mm(dx1, W["w_out"], "nt", "d_mixed")
    gW["w_out"] = jnp.concatenate([_mm(y, dx1, "tn", "g_w_out_ssm", out_dtypes=(BF16,)),
                                   _mm(o, dx1, "tn", "g_w_out_attn", out_dtypes=(BF16,))], axis=0)
    dy_hm = _to_heads(dmixed[:, :SSM_INNER])
    dqs, dkn, dv, dck = _flash_bwd(qs, kn, vb, cq, ck, o_fine, dmixed, SSM_INNER, lse)
    dq_raw, dk_raw, dgq2, dgk2 = _qk_prep_bwd(proj, gq2, gk2, dqs, dkn)
    gp["g_q"] = dgq2[:, :HEAD_DIM] + dgq2[:, HEAD_DIM:]
    gp["g_k"] = dgk2[:, :HEAD_DIM] + dgk2[:, HEAD_DIM:]
    df, gp["f_bias"] = _logf_cumsum_bwd(f_raw, p["f_bias"], dck[:, 0, :].T)
    dxs_hm, dz_hm, dB, dC, ddt, ddtb, dalog, ddsk, dnw = _ssd_bwd(xs_hm, xbc, z_hm, dt_hm, *ssd_par, hs, dy_hm)
    gp["dt_bias"] = ddtb.reshape(1, SSM_HEADS)
    gp["a_log"] = dalog.reshape(1, SSM_HEADS)
    gp["d_skip"] = ddsk.reshape(1, SSM_HEADS)
    gp["ssm_norm_w"] = dnw.reshape(1, SSM_INNER)
    dxbc = jnp.concatenate([_from_heads(dxs_hm), dB, dC], axis=-1)
    dxbc_raw, dconv_w, gp["conv_b"] = _conv_bwd(proj, COL_XBC, CONV_DIM, p["conv_w"], p["conv_b"], dxbc)
    gp["conv_w"] = dconv_w[:CONV_WIDTH]
    dproj = jnp.concatenate(
        [_from_heads(dz_hm).astype(BF16), dxbc_raw, dq_raw, dk_raw, dv, ddt[:, :, 0].T.astype(BF16), df.astype(BF16),
         jnp.zeros((S, IN_COLS_PAD - IN_COLS), BF16)], axis=-1)
    gW["w_in"] = _mm(h1, dproj, "tn", "g_w_in", out_dtypes=(BF16,))
    dh1 = _mm(dproj, W["w_in"], "nt", "d_h1")
    dx, gp["g_mix"] = _rmsnorm_bwd(x, p["g_mix"], dh1, dx1, "norm_mix_bwd")
    return loss_row, dx, gW, gp


_ANY = pl.BlockSpec(memory_space=pl.ANY)


def _place():
    x, y, c = lax.axis_index("x"), lax.axis_index("y"), lax.axis_index("c")
    chips = [(1 - x, y), (x, 1 - y), (1 - x, 1 - y)]
    return x, y, c, chips


def _chip_index(px, py):
    return 2 * px + py


def _all_gather_chips(split, whole):
    ns, nw = len(split), len(whole)
    n = ns + nw

    def body(*refs):
        ins, outs = refs[:n], refs[n:2 * n]
        send_ici, recv_ici, send_d2d, recv_d2d = refs[2 * n:]
        x, y, c, chips = _place()
        me = _chip_index(x, y)
        sib = (x, y, 1 - c)

        def ici(k, j, src, dst):
            return pltpu.make_async_remote_copy(src_ref=src, dst_ref=dst, send_sem=send_ici.at[3 * k + j],
                                                recv_sem=recv_ici.at[3 * k + j], device_id=(*chips[j], c),
                                                device_id_type=MESH)

        def d2d(k, j, piece):
            return pltpu.make_async_remote_copy(src_ref=piece, dst_ref=piece, send_sem=send_d2d.at[3 * k + j],
                                                recv_sem=recv_d2d.at[3 * k + j], device_id=sib, device_id_type=MESH)

        sends = []
        for k in range(n):
            for j in range(3):
                if k < ns:
                    sends.append(ici(k, j, ins[k].at[c], outs[k].at[me, c]))
                else:
                    sends.append(ici(k, j, ins[k], outs[k].at[me]))
                sends[-1].start()
        passed = []
        for k in range(n):
            for j in range(3):
                src_chip = _chip_index(*chips[j])
                if k < ns:
                    ici(k, j, ins[k].at[c], outs[k].at[src_chip, c]).wait_recv()
                    passed.append(d2d(k, j, outs[k].at[src_chip, c]))
                    passed[-1].start()
                else:
                    ici(k, j, ins[k], outs[k].at[src_chip]).wait_recv()
        for k in range(ns):
            for j in range(3):
                d2d(k, j, outs[k].at[_chip_index(*chips[j]), 1 - c]).wait_recv()
        for cp in sends + passed:
            cp.wait_send()

    arrs = list(split) + list(whole)
    return pl.pallas_call(
        body, in_specs=[_ANY] * n, out_specs=[_ANY] * n,
        out_shape=[jax.ShapeDtypeStruct((N_CHIPS,) + a.shape, a.dtype) for a in arrs],
        scratch_shapes=[pltpu.SemaphoreType.DMA((3 * n,)), pltpu.SemaphoreType.DMA((3 * n,)),
                        pltpu.SemaphoreType.DMA((3 * ns,)), pltpu.SemaphoreType.DMA((3 * ns,))],
        name="all_gather_chips")(*arrs)


def _sibling_send_halves(grads):
    n = len(grads)

    def body(*refs):
        ins, outs = refs[:n], refs[n:2 * n]
        send_sem, recv_sem = refs[2 * n:]
        x, y, c, _ = _place()

        def cp(k, j, half):
            return pltpu.make_async_remote_copy(src_ref=ins[k].at[j, half], dst_ref=outs[k].at[j],
                                                send_sem=send_sem.at[N_CHIPS * k + j],
                                                recv_sem=recv_sem.at[N_CHIPS * k + j],
                                                device_id=(x, y, 1 - c), device_id_type=MESH)

        copies = [cp(k, j, 1 - c) for k in range(n) for j in range(N_CHIPS)]
        for q in copies:
            q.start()
        for q in copies:
            q.wait()

    return pl.pallas_call(
        body, in_specs=[_ANY] * n, out_specs=[_ANY] * n,
        out_shape=[jax.ShapeDtypeStruct((N_CHIPS,) + g.shape[2:], g.dtype) for g in grads],
        scratch_shapes=[pltpu.SemaphoreType.DMA((N_CHIPS * n,)), pltpu.SemaphoreType.DMA((N_CHIPS * n,))],
        name="rs_sibling_halves")(*grads)


def _chips_send_shards(parts):
    n = len(parts)

    def body(*refs):
        ins, outs = refs[:n], refs[n:2 * n]
        send_sem, recv_sem = refs[2 * n:]
        x, y, c, chips = _place()

        def cp(k, j):
            return pltpu.make_async_remote_copy(src_ref=ins[k].at[_chip_index(*chips[j])], dst_ref=outs[k].at[j],
                                                send_sem=send_sem.at[3 * k + j], recv_sem=recv_sem.at[3 * k + j],
                                                device_id=(*chips[j], c), device_id_type=MESH)

        copies = [cp(k, j) for k in range(n) for j in range(3)]
        for q in copies:
            q.start()
        for q in copies:
            q.wait()

    return pl.pallas_call(
        body, in_specs=[_ANY] * n, out_specs=[_ANY] * n,
        out_shape=[jax.ShapeDtypeStruct((3,) + g.shape[1:], g.dtype) for g in parts],
        scratch_shapes=[pltpu.SemaphoreType.DMA((3 * n,)), pltpu.SemaphoreType.DMA((3 * n,))],
        name="rs_chip_shards")(*parts)


def _sibling_exchange(halves):
    n = len(halves)

    def body(*refs):
        ins, outs = refs[:n], refs[n:2 * n]
        send_sem, recv_sem = refs[2 * n:]
        x, y, c, _ = _place()

        def cp(k, half):
            return pltpu.make_async_remote_copy(src_ref=ins[k], dst_ref=outs[k].at[half], send_sem=send_sem.at[k],
                                                recv_sem=recv_sem.at[k], device_id=(x, y, 1 - c), device_id_type=MESH)

        sends = [cp(k, c) for k in range(n)]
        for q in sends:
            q.start()
        for k in range(n):
            cp(k, 1 - c).wait_recv()
        for q in sends:
            q.wait_send()

    return pl.pallas_call(
        body, in_specs=[_ANY] * n, out_specs=[_ANY] * n,
        out_shape=[jax.ShapeDtypeStruct((2,) + h.shape, h.dtype) for h in halves],
        scratch_shapes=[pltpu.SemaphoreType.DMA((n,)), pltpu.SemaphoreType.DMA((n,))],
        name="rs_sibling_exchange")(*halves)


def _all_reduce_small(vec):
    R = vec.shape[0]

    def body(v_ref, o_ref, buf, send_sem, recv_sem):
        x, y, c = lax.axis_index("x"), lax.axis_index("y"), lax.axis_index("c")
        me = 4 * x + 2 * y + c
        buf[me] = v_ref[...]
        copies = []
        for r in range(1, N_DEV):
            fx, fy, fc = (r >> 2) & 1, (r >> 1) & 1, r & 1
            peer = (x ^ fx, y ^ fy, c ^ fc)
            copies.append(pltpu.make_async_remote_copy(src_ref=v_ref, dst_ref=buf.at[me], send_sem=send_sem.at[r - 1],
                                                       recv_sem=recv_sem.at[r - 1], device_id=peer, device_id_type=MESH))
        for q in copies:
            q.start()
        for r in range(1, N_DEV):
            fx, fy, fc = (r >> 2) & 1, (r >> 1) & 1, r & 1
            src = 4 * (x ^ fx) + 2 * (y ^ fy) + (c ^ fc)
            pltpu.make_async_remote_copy(src_ref=v_ref, dst_ref=buf.at[src], send_sem=send_sem.at[r - 1],
                                         recv_sem=recv_sem.at[r - 1], device_id=(x, y, c), device_id_type=MESH).wait_recv()
        acc = buf[0]
        for d in range(1, N_DEV):
            acc = acc + buf[d]
        o_ref[...] = acc
        for q in copies:
            q.wait_send()

    vm = pl.BlockSpec(memory_space=pltpu.VMEM)
    return pl.pallas_call(
        body, in_specs=[vm], out_specs=vm, out_shape=jax.ShapeDtypeStruct((R, LANES), F32),
        scratch_shapes=[pltpu.VMEM((N_DEV, R, LANES), F32), pltpu.SemaphoreType.DMA((N_DEV - 1,)),
                        pltpu.SemaphoreType.DMA((N_DEV - 1,))],
        name="all_reduce_small")(vec)


_INPUTS = ["x", "mem", "g_mix", "w_in", "conv_w", "conv_b", "dt_bias", "a_log", "d_skip", "ssm_norm_w", "g_q", "g_k",
           "f_bias", "w_out", "g_xattn", "g_mem", "xq_w", "xkv_w", "xg_q", "xg_k", "xo_w", "g_mlp", "w_up", "w_down"]
_WEIGHTS = _INPUTS[2:]
_BIG = ["w_in", "w_out", "xq_w", "xkv_w", "xo_w", "w_up", "w_down"]
_COL_SHARDED = ["w_in", "xkv_w", "w_up"]
_SMALL = [n for n in _WEIGHTS if n not in _BIG]


def _pack_rows(arrs):
    rows = []
    for a in arrs:
        flat = a.reshape(-1)
        pad = -flat.shape[0] % LANES
        rows.append(jnp.pad(flat, (0, pad)).reshape(-1, LANES))
    out = jnp.concatenate(rows, axis=0)
    return jnp.pad(out, ((0, -out.shape[0] % 8), (0, 0)))


def _unpack_rows(packed, shapes):
    out, r = [], 0
    for s in shapes:
        n = math.prod(s)
        nr = -(-n // LANES)
        out.append(packed[r:r + nr].reshape(-1)[:n].reshape(s))
        r += nr
    return out


def kernel(x, mem, g_mix, w_in, conv_w, conv_b, dt_bias, a_log, d_skip, ssm_norm_w, g_q, g_k, f_bias, w_out, g_xattn, g_mem, xq_w, xkv_w, xg_q, xg_k, xo_w, g_mlp, w_up, w_down, loss_target, m_g_mix, m_w_in, m_conv_w, m_conv_b, m_dt_bias, m_a_log, m_d_skip, m_ssm_norm_w, m_g_q, m_g_k, m_f_bias, m_w_out, m_g_xattn, m_g_mem, m_xq_w, m_xkv_w, m_xg_q, m_xg_k, m_xo_w, m_g_mlp, m_w_up, m_w_down, v_g_mix, v_w_in, v_conv_w, v_conv_b, v_dt_bias, v_a_log, v_d_skip, v_ssm_norm_w, v_g_q, v_g_k, v_f_bias, v_w_out, v_g_xattn, v_g_mem, v_xq_w, v_xkv_w, v_xg_q, v_xg_k, v_xo_w, v_g_mlp, v_w_up, v_w_down):
    args = (x, mem, g_mix, w_in, conv_w, conv_b, dt_bias, a_log, d_skip, ssm_norm_w, g_q, g_k, f_bias, w_out, g_xattn,
            g_mem, xq_w, xkv_w, xg_q, xg_k, xo_w, g_mlp, w_up, w_down)
    w = dict(zip(_INPUTS, args))
    mom1 = dict(zip(_WEIGHTS, (m_g_mix, m_w_in, m_conv_w, m_conv_b, m_dt_bias, m_a_log, m_d_skip, m_ssm_norm_w, m_g_q,
                               m_g_k, m_f_bias, m_w_out, m_g_xattn, m_g_mem, m_xq_w, m_xkv_w, m_xg_q, m_xg_k, m_xo_w,
                               m_g_mlp, m_w_up, m_w_down)))
    mom2 = dict(zip(_WEIGHTS, (v_g_mix, v_w_in, v_conv_w, v_conv_b, v_dt_bias, v_a_log, v_d_skip, v_ssm_norm_w, v_g_q,
                               v_g_k, v_f_bias, v_w_out, v_g_xattn, v_g_mem, v_xq_w, v_xkv_w, v_xg_q, v_xg_k, v_xo_w,
                               v_g_mlp, v_w_up, v_w_down)))
    chip = _chip_index(lax.axis_index("x"), lax.axis_index("y"))
    core = lax.axis_index("c")

    shards = [w[n][0] for n in _BIG]
    halves = [s.astype(BF16).reshape(2, s.shape[0] // 2, s.shape[1]) for s in shards]
    gathered = _all_gather_chips(halves, [w["conv_w"][0]])
    own = halves + [w["conv_w"][0]]
    gathered = [lax.dynamic_update_index_in_dim(g, o, chip, axis=0) for g, o in zip(gathered, own)]
    full = dict(zip(_BIG, gathered[:len(_BIG)]))
    W = {}
    for n in _BIG:
        g = full[n]
        g = g.reshape(N_CHIPS, 2 * g.shape[2], g.shape[3])
        if n == "w_in":
            g = _to_kernel_cols(g.transpose(1, 0, 2).reshape(g.shape[1], IN_COLS))
            W[n] = jnp.pad(g, ((0, 0), (0, IN_COLS_PAD - IN_COLS)))
        elif n in _COL_SHARDED:
            W[n] = g
        else:
            W[n] = g.reshape(N_CHIPS * g.shape[1], g.shape[2])
    conv_w_full = gathered[-1].transpose(1, 0, 2).reshape(CONV_WIDTH, CONV_DIM)
    p = {n: w[n] for n in _SMALL}
    p["conv_w"] = conv_w_full

    loss_row, dx, gW, gp = _layer_fwd_bwd(x[0], mem[0], loss_target[0], W, p)

    grads4 = []
    for n, s in zip(_BIG, shards):
        g = gW[n]
        if n == "w_in":
            g = _to_reference_cols(g).reshape(g.shape[0], N_CHIPS, IN_COLS // N_CHIPS).transpose(1, 0, 2)
        elif n not in _COL_SHARDED:
            g = g.reshape(N_CHIPS, g.shape[0] // N_CHIPS, g.shape[1])
        grads4.append(g.reshape(N_CHIPS, 2, g.shape[1] // 2, g.shape[2]))
    from_sibling = _sibling_send_halves(grads4)
    pair_sums = []
    for k, g in enumerate(grads4):
        mine = lax.dynamic_index_in_dim(g, core, axis=1, keepdims=False)
        flat = lambda a: a.reshape(-1, a.shape[-1])
        (s,) = _nsum([flat(mine), flat(from_sibling[k])], (BF16,), "rs_pair_sum_" + _BIG[k])
        pair_sums.append(s.reshape(mine.shape))
    from_chips = _chips_send_shards(pair_sums)
    reduced = []
    for k, ps in enumerate(pair_sums):
        own = lax.dynamic_index_in_dim(ps, chip, axis=0, keepdims=False)
        (r,) = _nsum([own, from_chips[k][0], from_chips[k][1], from_chips[k][2]], (F32,), "rs_chip_sum_" + _BIG[k])
        reduced.append(r)
    grad_shards = [lax.dynamic_update_index_in_dim(g, r, core, axis=0)
                   for g, r in zip(_sibling_exchange(reduced), reduced)]

    small_shapes = [gp[n].shape for n in _SMALL] + [(1, LANES)]
    packed = _pack_rows([gp[n] for n in _SMALL] + [loss_row])
    summed = _unpack_rows(_all_reduce_small(packed), small_shapes)
    gsmall = dict(zip(_SMALL, summed[:-1]))
    loss = summed[-1][0, 0]
    shard_cols = CONV_DIM // N_CHIPS
    gsmall["conv_w"] = lax.dynamic_slice_in_dim(gsmall["conv_w"], chip * shard_cols, shard_cols, axis=1)

    grad, delta, new_m, new_v = {}, {}, {}, {}
    for k, n in enumerate(_BIG):
        shape = w[n].shape
        g2 = grad_shards[k].reshape(shape[1], shape[2])
        d, m1, v1 = _adamw(w[n][0], g2, mom1[n][0], mom2[n][0], "adamw_" + n)
        grad[n], delta[n], new_m[n], new_v[n] = (a.reshape(shape) for a in (g2, d, m1, v1))
    pk = lambda src: _pack_rows([src[n] for n in _SMALL])
    for n in _SMALL:
        gsmall[n] = gsmall[n].reshape(w[n].shape)
    d, m1, v1 = _adamw(pk(w), pk(gsmall), pk(mom1), pk(mom2), "adamw_small")
    shapes = [w[n].shape for n in _SMALL]
    for n, dn, mn, vn in zip(_SMALL, _unpack_rows(d, shapes), _unpack_rows(m1, shapes), _unpack_rows(v1, shapes)):
        grad[n], delta[n], new_m[n], new_v[n] = gsmall[n], dn, mn, vn

    return (loss, dx[None], *[grad[n] for n in _WEIGHTS], *[delta[n] for n in _WEIGHTS],
            *[new_m[n] for n in _WEIGHTS], *[new_v[n] for n in _WEIGHTS])
```

```python
import math
from typing import NamedTuple

import jax
import jax.numpy as jnp
from jax import lax
from jax.experimental import pallas as pl
from jax.experimental.pallas import tpu as pltpu

F32 = jnp.float32
BF16 = jnp.bfloat16
HI = lax.Precision.HIGHEST
MESH = pl.DeviceIdType.MESH

EPS = 1e-5
CHUNK = 128
SSM_HEADS = 16
SSM_GROUPS = 2
HEADS_PER_GROUP = SSM_HEADS // SSM_GROUPS
HEAD_DIM = 64
SSM_STATE = 128
ATTN_HEADS = 16
XATTN_HEADS = 4
XATTN_DIM = 256
CONV_WIDTH = 4
N_CHIPS = 4
N_DEV = 8
LANES = 128
VMEM_LIMIT = 56 * 1024 * 1024

ADAM_LR = 0.001
ADAM_B1 = 0.9
ADAM_B2 = 0.999
ADAM_EPS = 1e-08
ADAM_WD = 0.01
ADAM_STEP = 10


def _params(sem):
    return pltpu.CompilerParams(dimension_semantics=sem, vmem_limit_bytes=VMEM_LIMIT)


def _pick(n, cands):
    for c in cands:
        if n % c == 0:
            return c
    return n


def _mm(a, b, mode, name, out_dtypes=(F32,), epilogue=None, extras=(), b_chunks=1, out_chunks=1,
        tm=None, tn=None, tk=None):
    if mode == "nn":
        M, K = a.shape
        N = b.shape[-1] * b_chunks
    elif mode == "nt":
        M, K = a.shape
        N = b.shape[-2]
        assert b.shape[-1] * b_chunks == K
    else:
        K, M = a.shape
        N = b.shape[-1] * b_chunks
    tm = tm or _pick(M, (2048, 1024, 512, 256, 128))
    tn = tn or _pick(N // max(b_chunks if mode != "nt" else 1, out_chunks), (512, 640, 384, 256, 128))
    if tk is None:
        kmax = b.shape[-1] if mode == "nt" else K
        tk = kmax if kmax <= 2048 else _pick(kmax, (2048, 1152, 1024, 512))
    nk = K // tk
    assert M % tm == 0 and N % tn == 0 and K % tk == 0
    grid = (M // tm, N // tn, nk)

    if mode == "tn":
        a_spec = pl.BlockSpec((tk, tm), lambda i, j, k: (k, i))
    else:
        a_spec = pl.BlockSpec((tm, tk), lambda i, j, k: (i, k))

    def b_index(t_row, t_last, tile_last):
        if b_chunks == 1:
            return (t_row, t_last)
        q = (b.shape[-1]) // tile_last
        return (t_last // q, t_row, t_last % q)

    if mode == "nn" or mode == "tn":
        bshape = (tk, tn)
        bmap = lambda i, j, k: b_index(k, j, tn)
    else:
        bshape = (tn, tk)
        bmap = lambda i, j, k: b_index(j, k, tk)
    if b_chunks > 1:
        bshape = (None,) + bshape
    b_spec = pl.BlockSpec(bshape, bmap)

    if out_chunks == 1:
        o_spec = pl.BlockSpec((tm, tn), lambda i, j, k: (i, j))
        o_shape = (M, N)
    else:
        qo = (N // out_chunks) // tn
        o_spec = pl.BlockSpec((None, tm, tn), lambda i, j, k: (j // qo, i, j % qo))
        o_shape = (out_chunks, M, N // out_chunks)
    e_spec = pl.BlockSpec((tm, tn), lambda i, j, k: (i, j))

    dims = {"nn": (((1,), (0,)), ((), ())), "nt": (((1,), (1,)), ((), ())), "tn": (((0,), (0,)), ((), ()))}[mode]
    n_ex = len(extras)
    n_out = len(out_dtypes)

    def body(*refs):
        a_ref, b_ref = refs[0], refs[1]
        ex_refs = refs[2:2 + n_ex]
        o_refs = refs[2 + n_ex:2 + n_ex + n_out]

        def finish(acc):
            outs = epilogue(acc, *[r[...] for r in ex_refs]) if epilogue is not None else (acc,)
            for r, o in zip(o_refs, outs):
                r[...] = o.astype(r.dtype)

        part = lax.dot_general(a_ref[...].astype(BF16), b_ref[...].astype(BF16), dims,
                               preferred_element_type=F32)
        if nk == 1:
            finish(part)
        else:
            acc_ref = refs[-1]
            k = pl.program_id(2)

            @pl.when(k == 0)
            def _():
                acc_ref[...] = part

            @pl.when(k > 0)
            def _():
                acc_ref[...] += part

            @pl.when(k == nk - 1)
            def _():
                finish(acc_ref[...])

    outs = pl.pallas_call(
        body,
        grid=grid,
        in_specs=[a_spec, b_spec] + [e_spec] * n_ex,
        out_specs=[o_spec] * n_out,
        out_shape=[jax.ShapeDtypeStruct(o_shape, d) for d in out_dtypes],
        scratch_shapes=[pltpu.VMEM((tm, tn), F32)] if nk > 1 else [],
        compiler_params=_params(("parallel", "parallel", "arbitrary")),
        name=name,
    )(a, b, *extras)
    return outs[0] if n_out == 1 else outs


def _rms(x, g):
    r = lax.rsqrt(jnp.mean(x * x, axis=-1, keepdims=True) + EPS)
    return x * r * g


def _rmsnorm_fwd(x, g, name):
    R, D = x.shape
    tr = _pick(R, (512, 256))

    def body(x_ref, g_ref, o_ref):
        o_ref[...] = _rms(x_ref[...], g_ref[...]).astype(o_ref.dtype)

    return pl.pallas_call(
        body, grid=(R // tr,),
        in_specs=[pl.BlockSpec((tr, D), lambda i: (i, 0)), pl.BlockSpec((1, D), lambda i: (0, 0))],
        out_specs=pl.BlockSpec((tr, D), lambda i: (i, 0)),
        out_shape=jax.ShapeDtypeStruct((R, D), BF16),
        compiler_params=_params(("parallel",)), name=name)(x, g)


def _rmsnorm_bwd(x, g, dh, dres, name):
    R, D = x.shape
    tr = _pick(R, (256,))
    has_res = dres is not None

    def body(*refs):
        if has_res:
            x_ref, g_ref, dh_ref, dres_ref, dx_ref, dg_ref = refs
        else:
            x_ref, g_ref, dh_ref, dx_ref, dg_ref = refs
        _, vjp = jax.vjp(_rms, x_ref[...], g_ref[...])
        dx, dg = vjp(dh_ref[...])
        if has_res:
            dx = dx + dres_ref[...]
        dx_ref[...] = dx

        @pl.when(pl.program_id(0) == 0)
        def _():
            dg_ref[...] = jnp.zeros_like(dg_ref)

        dg_ref[...] += dg

    row = pl.BlockSpec((tr, D), lambda i: (i, 0))
    vec = pl.BlockSpec((1, D), lambda i: (0, 0))
    ins = [x, g, dh] + ([dres] if has_res else [])
    return pl.pallas_call(
        body, grid=(R // tr,),
        in_specs=[row, vec, row] + ([row] if has_res else []),
        out_specs=[row, vec],
        out_shape=[jax.ShapeDtypeStruct((R, D), F32), jax.ShapeDtypeStruct((1, D), F32)],
        compiler_params=_params(("arbitrary",)), name=name)(*ins)


def _shift_down(u, k):
    if k == 0:
        return u
    rows = lax.broadcasted_iota(jnp.int32, u.shape, 0)
    return jnp.where(rows >= k, pltpu.roll(u, k, axis=0), 0.0)


def _shift_up(u, k):
    if k == 0:
        return u
    n = u.shape[0]
    rows = lax.broadcasted_iota(jnp.int32, u.shape, 0)
    return jnp.where(rows < n - k, pltpu.roll(u, n - k, axis=0), 0.0)


def _conv_pre(u, w, b):
    pre = b
    for j in range(CONV_WIDTH):
        pre = pre + w[j:j + 1, :] * _shift_down(u, CONV_WIDTH - 1 - j)
    return pre


def _conv_fwd(proj, col0, ncols, conv_w, conv_b):
    S = proj.shape[0]
    cb0 = col0 // LANES

    def body(u_ref, w_ref, b_ref, o_ref):
        pre = _conv_pre(u_ref[...], w_ref[...], b_ref[...])
        o_ref[...] = pre * jax.nn.sigmoid(pre)

    return pl.pallas_call(
        body, grid=(ncols // LANES,),
        in_specs=[pl.BlockSpec((S, LANES), lambda j: (0, j + cb0)),
                  pl.BlockSpec((CONV_WIDTH, LANES), lambda j: (0, j)),
                  pl.BlockSpec((1, LANES), lambda j: (0, j))],
        out_specs=pl.BlockSpec((S, LANES), lambda j: (0, j)),
        out_shape=jax.ShapeDtypeStruct((S, ncols), F32),
        compiler_params=_params(("parallel",)), name="conv_fwd")(proj, conv_w, conv_b)


def _conv_bwd(proj, col0, ncols, conv_w, conv_b, dout):
    S = proj.shape[0]
    cb0 = col0 // LANES

    def body(u_ref, w_ref, b_ref, d_ref, du_ref, dw_ref, db_ref):
        u = u_ref[...]
        w = w_ref[...]
        pre = _conv_pre(u, w, b_ref[...])
        s = jax.nn.sigmoid(pre)
        dpre = d_ref[...] * (s * (1.0 + pre * (1.0 - s)))
        du = jnp.zeros_like(u)
        rows = []
        for j in range(CONV_WIDTH):
            k = CONV_WIDTH - 1 - j
            du = du + w[j:j + 1, :] * _shift_up(dpre, k)
            rows.append(jnp.sum(dpre * _shift_down(u, k), axis=0, keepdims=True))
        du_ref[...] = du.astype(du_ref.dtype)
        rows.append(jnp.zeros((8 - CONV_WIDTH, LANES), F32))
        dw_ref[...] = jnp.concatenate(rows, axis=0)
        db_ref[...] = jnp.sum(dpre, axis=0, keepdims=True)

    return pl.pallas_call(
        body, grid=(ncols // LANES,),
        in_specs=[pl.BlockSpec((S, LANES), lambda j: (0, j + cb0)),
                  pl.BlockSpec((CONV_WIDTH, LANES), lambda j: (0, j)),
                  pl.BlockSpec((1, LANES), lambda j: (0, j)),
                  pl.BlockSpec((S, LANES), lambda j: (0, j))],
        out_specs=[pl.BlockSpec((S, LANES), lambda j: (0, j)),
                   pl.BlockSpec((8, LANES), lambda j: (0, j)),
                   pl.BlockSpec((1, LANES), lambda j: (0, j))],
        out_shape=[jax.ShapeDtypeStruct((S, ncols), BF16),
                   jax.ShapeDtypeStruct((8, ncols), F32),
                   jax.ShapeDtypeStruct((1, ncols), F32)],
        compiler_params=_params(("parallel",)), name="conv_bwd")(proj, conv_w, conv_b, dout)


def _softplus(x):
    return jnp.maximum(x, 0.0) + jnp.log1p(jnp.exp(-jnp.abs(x)))


def _dot32(a, b, dims=(((1,), (0,)), ((), ()))):
    return lax.dot_general(a, b, dims, precision=HI, preferred_element_type=F32)


def _dotd(a, b, dims=(((1,), (0,)), ((), ()))):
    return lax.dot_general(a, b, dims, preferred_element_type=F32)


def _ssd_chunk(xs, Bm, Cm, z, dtr, dtb, alog, dsk, nw, h):
    L = Bm.shape[0]
    ri = lax.broadcasted_iota(jnp.int32, (L, L), 0)
    ci = lax.broadcasted_iota(jnp.int32, (L, L), 1)
    causal = ri >= ci
    tril = causal.astype(F32)
    CB = _dotd(Cm, Bm, (((1,), (1,)), ((), ())))
    gated, hnew = [], []
    ssq = jnp.zeros((L, 1), F32)
    for r in range(len(xs)):
        dt = _softplus(dtr[r] + dtb[r])
        dA = dt * (-jnp.exp(alog[r]))
        acs = _dot32(tril, dA)
        tot = jnp.sum(dA, axis=0, keepdims=True)
        cc = jnp.broadcast_to(acs, (L, L))
        seg = jnp.where(causal, cc - cc.T, -1e30)
        Lmat = jnp.exp(seg)
        X = xs[r] * dt
        y = _dotd(CB * Lmat, X) + jnp.exp(acs) * _dotd(Cm, h[r]) + dsk[r] * xs[r]
        hnew.append(jnp.exp(tot) * h[r] + _dotd(Bm, X * jnp.exp(tot - acs), (((0,), (0,)), ((), ()))))
        g = y * (z[r] * jax.nn.sigmoid(z[r]))
        ssq = ssq + jnp.sum(g * g, axis=-1, keepdims=True)
        gated.append(g)
    rs = lax.rsqrt(ssq / (len(xs) * xs[0].shape[-1]) + EPS)
    return [g * rs * nw[r] for r, g in enumerate(gated)], hnew


def _ssd_specs(S):
    H, P, N, L = HEADS_PER_GROUP, HEAD_DIM, SSM_STATE, CHUNK
    return dict(
        head=lambda rev: pl.BlockSpec((H, L, P), (lambda g, c: (g, rev(c), 0))),
        bc=lambda rev, off: pl.BlockSpec((L, N), (lambda g, c: (rev(c), off + g))),
        dt=lambda rev: pl.BlockSpec((H, L, 1), (lambda g, c: (g, rev(c), 0))),
        scal=pl.BlockSpec((H, 1, 1), lambda g, c: (g, 0, 0)),
        nw=pl.BlockSpec((H, 1, P), lambda g, c: (g, 0, 0)),
        hs=lambda rev: pl.BlockSpec((None, H, N, P), (lambda g, c: (rev(c), g, 0, 0))),
    )


def _ssd_fwd(xs_hm, xbc, z_hm, dt_hm, dtb, alog, dsk, nw_hm):
    S = xbc.shape[0]
    H, P, N, L = HEADS_PER_GROUP, HEAD_DIM, SSM_STATE, CHUNK
    nc = S // L
    sp = _ssd_specs(S)
    ident = lambda c: c
    xoff = (SSM_HEADS * HEAD_DIM) // LANES

    def body(xs_ref, b_ref, c_ref, z_ref, dt_ref, dtb_ref, al_ref, dsk_ref, nw_ref, y_ref, hs_ref, h_ref):
        @pl.when(pl.program_id(1) == 0)
        def _():
            h_ref[...] = jnp.zeros_like(h_ref)

        hs_ref[...] = h_ref[...]
        hd = range(H)
        out, hnew = _ssd_chunk([xs_ref[r] for r in hd], b_ref[...], c_ref[...], [z_ref[r] for r in hd],
                               [dt_ref[r] for r in hd], [dtb_ref[r] for r in hd], [al_ref[r] for r in hd],
                               [dsk_ref[r] for r in hd], [nw_ref[r] for r in hd], [h_ref[r] for r in hd])
        for r in hd:
            y_ref[r] = out[r]
            h_ref[r] = hnew[r]

    return pl.pallas_call(
        body, grid=(SSM_GROUPS, nc),
        in_specs=[sp["head"](ident), sp["bc"](ident, xoff), sp["bc"](ident, xoff + SSM_GROUPS), sp["head"](ident),
                  sp["dt"](ident), sp["scal"], sp["scal"], sp["scal"], sp["nw"]],
        out_specs=[sp["head"](ident), sp["hs"](ident)],
        out_shape=[jax.ShapeDtypeStruct((SSM_HEADS, S, P), F32),
                   jax.ShapeDtypeStruct((nc, SSM_HEADS, N, P), F32)],
        scratch_shapes=[pltpu.VMEM((H, N, P), F32)],
        compiler_params=_params(("parallel", "arbitrary")), name="ssd_fwd",
    )(xs_hm, xbc, xbc, z_hm, dt_hm, dtb, alog, dsk, nw_hm)


def _ssd_bwd(xs_hm, xbc, z_hm, dt_hm, dtb, alog, dsk, nw_hm, hs, dy_hm):
    S = xbc.shape[0]
    H, P, N, L = HEADS_PER_GROUP, HEAD_DIM, SSM_STATE, CHUNK
    nc = S // L
    sp = _ssd_specs(S)
    rev = lambda c: nc - 1 - c
    xoff = (SSM_HEADS * HEAD_DIM) // LANES

    def body(xs_ref, b_ref, c_ref, z_ref, dt_ref, dtb_ref, al_ref, dsk_ref, nw_ref, hs_ref, dy_ref,
             dxs_ref, dz_ref, db_ref, dc_ref, ddt_ref, ddtb_ref, dal_ref, ddsk_ref, dnw_ref, dh_ref):
        first = pl.program_id(1) == 0

        @pl.when(first)
        def _():
            dh_ref[...] = jnp.zeros_like(dh_ref)
            ddtb_ref[...] = jnp.zeros_like(ddtb_ref)
            dal_ref[...] = jnp.zeros_like(dal_ref)
            ddsk_ref[...] = jnp.zeros_like(ddsk_ref)
            dnw_ref[...] = jnp.zeros_like(dnw_ref)

        hd = range(H)
        args = ([xs_ref[r] for r in hd], b_ref[...], c_ref[...], [z_ref[r] for r in hd],
                [dt_ref[r] for r in hd], [dtb_ref[r] for r in hd], [al_ref[r] for r in hd],
                [dsk_ref[r] for r in hd], [nw_ref[r] for r in hd], [hs_ref[r] for r in hd])
        _, vjp = jax.vjp(_ssd_chunk, *args)
        dxs, dB, dC, dz, ddt, ddtb, dal, ddsk, dnw, dh = vjp(([dy_ref[r] for r in hd], [dh_ref[r] for r in hd]))
        db_ref[...] = dB
        dc_ref[...] = dC
        for r in hd:
            dxs_ref[r] = dxs[r]
            dz_ref[r] = dz[r]
            ddt_ref[r] = ddt[r]
            dh_ref[r] = dh[r]
            ddtb_ref[r] += ddtb[r]
            dal_ref[r] += dal[r]
            ddsk_ref[r] += ddsk[r]
            dnw_ref[r] += dnw[r]

    bc_out = lambda: pl.BlockSpec((L, N), lambda g, c: (rev(c), g))
    return pl.pallas_call(
        body, grid=(SSM_GROUPS, nc),
        in_specs=[sp["head"](rev), sp["bc"](rev, xoff), sp["bc"](rev, xoff + SSM_GROUPS), sp["head"](rev),
                  sp["dt"](rev), sp["scal"], sp["scal"], sp["scal"], sp["nw"], sp["hs"](rev), sp["head"](rev)],
        out_specs=[sp["head"](rev), sp["head"](rev), bc_out(), bc_out(), sp["dt"](rev),
                   sp["scal"], sp["scal"], sp["scal"], sp["nw"]],
        out_shape=[jax.ShapeDtypeStruct((SSM_HEADS, S, P), F32), jax.ShapeDtypeStruct((SSM_HEADS, S, P), F32),
                   jax.ShapeDtypeStruct((S, SSM_GROUPS * N), F32), jax.ShapeDtypeStruct((S, SSM_GROUPS * N), F32),
                   jax.ShapeDtypeStruct((SSM_HEADS, S, 1), F32),
                   jax.ShapeDtypeStruct((SSM_HEADS, 1, 1), F32), jax.ShapeDtypeStruct((SSM_HEADS, 1, 1), F32),
                   jax.ShapeDtypeStruct((SSM_HEADS, 1, 1), F32), jax.ShapeDtypeStruct((SSM_HEADS, 1, P), F32)],
        scratch_shapes=[pltpu.VMEM((H, N, P), F32)],
        compiler_params=_params(("parallel", "arbitrary")), name="ssd_bwd",
    )(xs_hm, xbc, xbc, z_hm, dt_hm, dtb, alog, dsk, nw_hm, hs, dy_hm)


ATTN_SCALE = HEAD_DIM ** -0.5
ATTN_PAIRS = ATTN_HEADS // 2


def _first_head(rows):
    return lax.broadcasted_iota(jnp.int32, (rows, LANES), 1) < HEAD_DIM


def _pair_norm(x, g2, scale):
    first = _first_head(x.shape[0])
    sq = x * x
    ms0 = jnp.sum(jnp.where(first, sq, 0.0), axis=-1, keepdims=True) * (1.0 / HEAD_DIM)
    ms1 = jnp.sum(jnp.where(first, 0.0, sq), axis=-1, keepdims=True) * (1.0 / HEAD_DIM)
    r = jnp.where(first, lax.rsqrt(ms0 + EPS), lax.rsqrt(ms1 + EPS))
    return x * r * g2 * scale


def _qk_prep_fwd(proj, gq2, gk2):
    S = proj.shape[0]
    tq = _pick(S, (512, 256))

    def body(q_ref, k_ref, v_ref, gq_ref, gk_ref, qo_ref, ko_ref, vo_ref):
        qo_ref[...] = _pair_norm(q_ref[...], gq_ref[...], ATTN_SCALE).astype(BF16)
        ko_ref[...] = _pair_norm(k_ref[...], gk_ref[...], 1.0).astype(BF16)
        vo_ref[...] = v_ref[...].astype(BF16)

    col = lambda c0: pl.BlockSpec((tq, LANES), lambda h, i: (i, c0 // LANES + h))
    blk = pl.BlockSpec((tq, LANES), lambda h, i: (i, h))
    vec = pl.BlockSpec((1, LANES), lambda h, i: (0, 0))
    return pl.pallas_call(
        body, grid=(ATTN_PAIRS, S // tq), in_specs=[col(COL_Q), col(COL_K), col(COL_V), vec, vec],
        out_specs=[blk, blk, blk], out_shape=[jax.ShapeDtypeStruct((S, ATTN_WIDTH), BF16)] * 3,
        compiler_params=_params(("parallel", "parallel")), name="qk_prep_fwd")(proj, proj, proj, gq2, gk2)


def _qk_prep_bwd(proj, gq2, gk2, dqs, dkn):
    S = proj.shape[0]
    tq = _pick(S, (512, 256))

    def body(q_ref, k_ref, gq_ref, gk_ref, dqs_ref, dkn_ref, dq_ref, dk_ref, dgq_ref, dgk_ref):
        @pl.when((pl.program_id(0) == 0) & (pl.program_id(1) == 0))
        def _():
            dgq_ref[...] = jnp.zeros_like(dgq_ref)
            dgk_ref[...] = jnp.zeros_like(dgk_ref)

        _, vq = jax.vjp(lambda q, g: _pair_norm(q, g, ATTN_SCALE), q_ref[...], gq_ref[...])
        dq, dgq = vq(dqs_ref[...])
        _, vk = jax.vjp(lambda k, g: _pair_norm(k, g, 1.0), k_ref[...], gk_ref[...])
        dk, dgk = vk(dkn_ref[...])
        dq_ref[...] = dq.astype(dq_ref.dtype)
        dk_ref[...] = dk.astype(dk_ref.dtype)
        dgq_ref[...] += dgq
        dgk_ref[...] += dgk

    col = lambda c0: pl.BlockSpec((tq, LANES), lambda h, i: (i, c0 // LANES + h))
    blk = pl.BlockSpec((tq, LANES), lambda h, i: (i, h))
    vec = pl.BlockSpec((1, LANES), lambda h, i: (0, 0))
    return pl.pallas_call(
        body, grid=(ATTN_PAIRS, S // tq), in_specs=[col(COL_Q), col(COL_K), vec, vec, blk, blk],
        out_specs=[blk, blk, vec, vec],
        out_shape=[jax.ShapeDtypeStruct((S, ATTN_WIDTH), BF16)] * 2 + [jax.ShapeDtypeStruct((1, LANES), F32)] * 2,
        compiler_params=_params(("arbitrary", "arbitrary")), name="qk_prep_bwd")(proj, proj, gq2, gk2, dqs, dkn)


def _logf_cumsum_fwd(f_raw, f_bias):
    S, Hh = f_raw.shape
    L = CHUNK

    def body(f_ref, b_ref, o_ref):
        ri = lax.broadcasted_iota(jnp.int32, (L, L), 0)
        ci = lax.broadcasted_iota(jnp.int32, (L, L), 1)
        tril = (ri >= ci).astype(F32)
        carry = jnp.zeros((1, Hh), F32)
        for c in range(S // L):
            lf = -_softplus(-(f_ref[c * L:(c + 1) * L, :] + b_ref[...]))
            cum = _dot32(tril, lf) + carry
            o_ref[c * L:(c + 1) * L, :] = cum
            carry = cum[L - 1:L, :]

    return pl.pallas_call(body, out_shape=jax.ShapeDtypeStruct((S, Hh), F32), name="logf_cumsum_fwd")(f_raw, f_bias)


def _logf_cumsum_bwd(f_raw, f_bias, dcum):
    S, Hh = f_raw.shape
    L = CHUNK

    def body(f_ref, b_ref, d_ref, df_ref, db_ref):
        ri = lax.broadcasted_iota(jnp.int32, (L, L), 0)
        ci = lax.broadcasted_iota(jnp.int32, (L, L), 1)
        triu = (ri <= ci).astype(F32)
        carry = jnp.zeros((1, Hh), F32)
        db = jnp.zeros((1, Hh), F32)
        for c in reversed(range(S // L)):
            suf = _dot32(triu, d_ref[c * L:(c + 1) * L, :]) + carry
            df = suf * jax.nn.sigmoid(-(f_ref[c * L:(c + 1) * L, :] + b_ref[...]))
            df_ref[c * L:(c + 1) * L, :] = df
            db = db + jnp.sum(df, axis=0, keepdims=True)
            carry = suf[0:1, :]
        db_ref[...] = db

    return pl.pallas_call(
        body, out_shape=[jax.ShapeDtypeStruct((S, Hh), F32), jax.ShapeDtypeStruct((1, Hh), F32)],
        name="logf_cumsum_bwd")(f_raw, f_bias, dcum)


_NT = (((1,), (1,)), ((), ()))
_TN = (((0,), (0,)), ((), ()))


def _mxu(a, b, dims=(((1,), (0,)), ((), ()))):
    return lax.dot_general(a, b, dims, preferred_element_type=F32)


def _flash_fwd(qs, kn, vb, cq, ck):
    S, W = qs.shape
    tq = tk = _pick(S, (512, 256))
    nmask = max(tq // tk, 1)

    def body(q_ref, k_ref, v_ref, cq_ref, ck_ref, o_ref, of_ref, lse_ref):
        i = pl.program_id(1)
        first = _first_head(tq)
        q2 = q_ref[...]
        zero = jnp.zeros_like(q2)
        qa = (jnp.where(first, q2, zero), jnp.where(first, zero, q2))
        cqa = (cq_ref[0], cq_ref[1])
        row0 = i * tq

        def step(j, carry, masked):
            ms, ls, acc, rem = carry
            off = pl.multiple_of(j * tk, tk)
            k = k_ref[pl.ds(off, tk), :]
            v = v_ref[pl.ds(off, tk), :]
            new_m, new_l, alphas, pvs, prs = [], [], [], [], []
            for a in range(2):
                s = _mxu(qa[a], k, _NT) + cqa[a] - ck_ref[a, :, pl.ds(off, tk)]
                if masked:
                    ri = lax.broadcasted_iota(jnp.int32, (tq, tk), 0) + row0
                    ci = lax.broadcasted_iota(jnp.int32, (tq, tk), 1) + off
                    s = jnp.where(ri >= ci, s, -1e30)
                m_new = jnp.maximum(ms[a], jnp.max(s, axis=-1, keepdims=True))
                alpha = jnp.exp(ms[a] - m_new)
                p = jnp.exp(s - m_new)
                new_l.append(alpha * ls[a] + jnp.sum(p, axis=-1, keepdims=True))
                new_m.append(m_new)
                alphas.append(alpha)
                p_hi = p.astype(BF16)
                pvs.append(_mxu(p_hi, v))
                prs.append(_mxu((p - p_hi.astype(F32)).astype(BF16), v))
            al = jnp.where(first, alphas[0], alphas[1])
            acc = al * acc + jnp.where(first, pvs[0], pvs[1])
            rem = al * rem + jnp.where(first, prs[0], prs[1])
            return tuple(new_m), tuple(new_l), acc, rem

        neg = jnp.full((tq, 1), -1e30, F32)
        z1 = jnp.zeros((tq, 1), F32)
        z2 = jnp.zeros((tq, LANES), F32)
        carry = ((neg, neg), (z1, z1), z2, z2)
        n_full = (i * tq) // tk
        carry = lax.fori_loop(0, n_full, lambda j, c: step(j, c, False), carry)
        for jj in range(nmask):
            carry = step(n_full + jj, carry, True)
        ms, ls, acc, rem = carry
        linv = jnp.where(first, 1.0 / ls[0], 1.0 / ls[1])
        o_ref[...] = (acc * linv).astype(o_ref.dtype)
        of_ref[...] = (acc + rem) * linv
        lse_ref[0] = ms[0] + jnp.log(ls[0])
        lse_ref[1] = ms[1] + jnp.log(ls[1])

    qblk = pl.BlockSpec((tq, LANES), lambda h, i: (i, h))
    full = pl.BlockSpec((S, LANES), lambda h, i: (0, h))
    colb = pl.BlockSpec((2, tq, 1), lambda h, i: (h, i, 0))
    return pl.pallas_call(
        body, grid=(W // LANES, S // tq),
        in_specs=[qblk, full, full, colb, pl.BlockSpec((2, 1, S), lambda h, i: (h, 0, 0))],
        out_specs=[qblk, qblk, colb],
        out_shape=[jax.ShapeDtypeStruct((S, W), BF16), jax.ShapeDtypeStruct((S, W), F32),
                   jax.ShapeDtypeStruct((2 * (W // LANES), S, 1), F32)],
        compiler_params=_params(("parallel", "parallel")), name="flash_fwd")(qs, kn, vb, cq, ck)


def _flash_bwd(qs, kn, vb, cq, ck, o_fine, do, do_col0, lse):
    S, W = qs.shape
    tq = tk = _pick(S, (512, 256))
    nq = S // tq
    nmask = max(tk // tq, 1)

    def body(q_ref, k_ref, v_ref, cq_ref, ck_ref, of_ref, do_ref, lse_ref, dq_ref, dk_ref, dv_ref, dck_ref):
        j = pl.program_id(1)

        @pl.when(j == 0)
        def _():
            dq_ref[...] = jnp.zeros_like(dq_ref)

        firstk = _first_head(tk)
        firstq = _first_head(tq)
        k2 = k_ref[...]
        v2 = v_ref[...]
        zk = jnp.zeros_like(k2)
        ka = (jnp.where(firstk, k2, zk), jnp.where(firstk, zk, k2))
        va = (jnp.where(firstk, v2, zk), jnp.where(firstk, zk, v2))
        cka = (ck_ref[0], ck_ref[1])
        col0 = j * tk

        def step(i, carry, masked):
            dk, dv, dck0, dck1 = carry
            dcks = [dck0, dck1]
            off = pl.multiple_of(i * tq, tq)
            rows = pl.ds(off, tq)
            q2 = q_ref[rows, :]
            dob = do_ref[rows, :].astype(BF16)
            prod = dob.astype(F32) * of_ref[rows, :]
            dkp, dvp, dqp = [], [], []
            for a in range(2):
                s = _mxu(q2, ka[a], _NT) + cq_ref[a, rows, :] - cka[a]
                if masked:
                    ri = lax.broadcasted_iota(jnp.int32, (tq, tk), 0) + off
                    ci = lax.broadcasted_iota(jnp.int32, (tq, tk), 1) + col0
                    s = jnp.where(ri >= ci, s, -1e30)
                p = jnp.exp(s - lse_ref[a, rows, :])
                dp = _mxu(dob, va[a], _NT)
                own = jnp.where(firstq, prod, 0.0) if a == 0 else jnp.where(firstq, 0.0, prod)
                ds = p * (dp - jnp.sum(own, axis=-1, keepdims=True))
                dsb = ds.astype(BF16)
                dvp.append(_mxu(p.astype(BF16), dob, _TN))
                dkp.append(_mxu(dsb, q2, _TN))
                dqp.append(_mxu(dsb, k2))
                dcks[a] = dcks[a] - jnp.sum(ds, axis=0, keepdims=True)
            dq_ref[rows, :] += jnp.where(firstq, dqp[0], dqp[1])
            dk = dk + jnp.where(firstk, dkp[0], dkp[1])
            dv = dv + jnp.where(firstk, dvp[0], dvp[1])
            return dk, dv, dcks[0], dcks[1]

        z2 = jnp.zeros((tk, LANES), F32)
        z1 = jnp.zeros((1, tk), F32)
        carry = (z2, z2, z1, z1)
        i0 = (j * tk) // tq
        for ii in range(nmask):
            carry = step(i0 + ii, carry, True)
        dk, dv, dck0, dck1 = lax.fori_loop(i0 + nmask, nq, lambda i, c: step(i, c, False), carry)
        dk_ref[...] = dk
        dv_ref[...] = dv.astype(dv_ref.dtype)
        dck_ref[0] = dck0
        dck_ref[1] = dck1

    kblk = pl.BlockSpec((tk, LANES), lambda h, j: (j, h))
    full = pl.BlockSpec((S, LANES), lambda h, j: (0, h))
    dofull = pl.BlockSpec((S, LANES), lambda h, j: (0, do_col0 // LANES + h))
    col = pl.BlockSpec((2, S, 1), lambda h, j: (h, 0, 0))
    rowt = pl.BlockSpec((2, 1, tk), lambda h, j: (h, 0, j))
    return pl.pallas_call(
        body, grid=(W // LANES, S // tk),
        in_specs=[full, kblk, kblk, col, rowt, full, dofull, col],
        out_specs=[full, kblk, kblk, rowt],
        out_shape=[jax.ShapeDtypeStruct((S, W), F32), jax.ShapeDtypeStruct((S, W), F32),
                   jax.ShapeDtypeStruct((S, W), BF16), jax.ShapeDtypeStruct((2 * (W // LANES), 1, S), F32)],
        compiler_params=_params(("parallel", "arbitrary")), name="flash_bwd")(qs, kn, vb, cq, ck, o_fine, do, lse)


XATTN_SCALE = XATTN_DIM ** -0.5


def _xq_norm(q, g):
    return _rms(q, g) * XATTN_SCALE


def _xattn_fwd(xq, kv, gq, gk):
    S = xq.shape[0]
    Mm = kv.shape[0]
    Dh = XATTN_DIM
    tq = _pick(S, (512, 256))

    def body(q_ref, k_ref, v_ref, gq_ref, gk_ref, o_ref):
        qn = _xq_norm(q_ref[...], gq_ref[...]).astype(BF16)
        kn = _rms(k_ref[...], gk_ref[...]).astype(BF16)
        s = _mxu(qn, kn, _NT)
        m = jnp.max(s, axis=-1, keepdims=True)
        p = jnp.exp(s - m)
        l = jnp.sum(p, axis=-1, keepdims=True)
        o_ref[...] = (_mxu(p.astype(BF16), v_ref[...].astype(BF16)) / l).astype(o_ref.dtype)

    vec = pl.BlockSpec((1, Dh), lambda h, i: (0, 0))
    return pl.pallas_call(
        body, grid=(XATTN_HEADS, S // tq),
        in_specs=[pl.BlockSpec((tq, Dh), lambda h, i: (i, h)), pl.BlockSpec((Mm, Dh), lambda h, i: (0, h)),
                  pl.BlockSpec((Mm, Dh), lambda h, i: (0, XATTN_HEADS + h)), vec, vec],
        out_specs=pl.BlockSpec((tq, Dh), lambda h, i: (i, h)),
        out_shape=jax.ShapeDtypeStruct((S, XATTN_HEADS * Dh), BF16),
        compiler_params=_params(("parallel", "parallel")), name="xattn_fwd")(xq, kv, kv, gq, gk)


def _xattn_bwd(xq, kv, gq, gk, do):
    S = xq.shape[0]
    Mm = kv.shape[0]
    Dh = XATTN_DIM
    tq = _pick(S, (512, 256))
    nq = S // tq

    def body(q_ref, k_ref, v_ref, gq_ref, gk_ref, do_ref, dq_ref, dk_ref, dv_ref, dgq_ref, dgk_ref, dkn_acc, dv_acc):
        h = pl.program_id(0)
        i = pl.program_id(1)

        @pl.when((h == 0) & (i == 0))
        def _():
            dgq_ref[...] = jnp.zeros_like(dgq_ref)
            dgk_ref[...] = jnp.zeros_like(dgk_ref)

        @pl.when(i == 0)
        def _():
            dkn_acc[...] = jnp.zeros_like(dkn_acc)
            dv_acc[...] = jnp.zeros_like(dv_acc)

        qn32, vq = jax.vjp(_xq_norm, q_ref[...], gq_ref[...])
        kn32, vk = jax.vjp(_rms, k_ref[...], gk_ref[...])
        qn = qn32.astype(BF16)
        kn = kn32.astype(BF16)
        vb = v_ref[...].astype(BF16)
        s = _mxu(qn, kn, _NT)
        m = jnp.max(s, axis=-1, keepdims=True)
        p = jnp.exp(s - m)
        p = p / jnp.sum(p, axis=-1, keepdims=True)
        dob = do_ref[...].astype(BF16)
        dp = _mxu(dob, vb, _NT)
        delta = jnp.sum(p * dp, axis=-1, keepdims=True)
        ds = (p * (dp - delta)).astype(BF16)
        dv_acc[...] += _mxu(p.astype(BF16), dob, _TN)
        dkn_acc[...] += _mxu(ds, qn, _TN)
        dq, dgq = vq(_mxu(ds, kn))
        dq_ref[...] = dq.astype(dq_ref.dtype)
        dgq_ref[...] += dgq

        @pl.when(i == nq - 1)
        def _():
            dk, dgk = vk(dkn_acc[...])
            dk_ref[...] = dk.astype(dk_ref.dtype)
            dv_ref[...] = dv_acc[...].astype(dv_ref.dtype)
            dgk_ref[...] += dgk

    vec = pl.BlockSpec((1, Dh), lambda h, i: (0, 0))
    qblk = pl.BlockSpec((tq, Dh), lambda h, i: (i, h))
    kblk = pl.BlockSpec((Mm, Dh), lambda h, i: (0, h))
    vblk = pl.BlockSpec((Mm, Dh), lambda h, i: (0, XATTN_HEADS + h))
    return pl.pallas_call(
        body, grid=(XATTN_HEADS, nq),
        in_specs=[qblk, kblk, vblk, vec, vec, qblk],
        out_specs=[qblk, kblk, kblk, vec, vec],
        out_shape=[jax.ShapeDtypeStruct((S, XATTN_HEADS * Dh), BF16),
                   jax.ShapeDtypeStruct((Mm, XATTN_HEADS * Dh), BF16),
                   jax.ShapeDtypeStruct((Mm, XATTN_HEADS * Dh), BF16),
                   jax.ShapeDtypeStruct((1, Dh), F32), jax.ShapeDtypeStruct((1, Dh), F32)],
        scratch_shapes=[pltpu.VMEM((Mm, Dh), F32), pltpu.VMEM((Mm, Dh), F32)],
        compiler_params=_params(("arbitrary", "arbitrary")), name="xattn_bwd")(xq, kv, kv, gq, gk, do)


def _loss_head(y, target):
    S, D = y.shape
    tr = _pick(S, (512, 256))

    def body(y_ref, t_ref, dy_ref, loss_ref):
        @pl.when(pl.program_id(0) == 0)
        def _():
            loss_ref[...] = jnp.zeros_like(loss_ref)

        err = y_ref[...] - t_ref[...]
        dy_ref[...] = err * (1.0 / D)
        loss_ref[...] += jnp.sum(err * err) * (0.5 / D)

    row = pl.BlockSpec((tr, D), lambda i: (i, 0))
    return pl.pallas_call(
        body, grid=(S // tr,), in_specs=[row, row],
        out_specs=[row, pl.BlockSpec((1, LANES), lambda i: (0, 0))],
        out_shape=[jax.ShapeDtypeStruct((S, D), F32), jax.ShapeDtypeStruct((1, LANES), F32)],
        compiler_params=_params(("arbitrary",)), name="loss_head")(y, target)


def _row_tile(R, C):
    for tr in (1024, 512, 256, 128, 64, 32, 16, 8):
        if R % tr == 0 and tr * C * 4 <= (1 << 20):
            return tr
    return R


def _nsum(arrs, out_dtypes, name):
    R, C = arrs[0].shape
    tr = _row_tile(R, C)
    n = len(arrs)

    def body(*refs):
        acc = refs[0][...].astype(F32)
        for r in refs[1:n]:
            acc = acc + r[...].astype(F32)
        for o in refs[n:]:
            o[...] = acc.astype(o.dtype)

    blk = pl.BlockSpec((tr, C), lambda i: (i, 0))
    outs = pl.pallas_call(
        body, grid=(R // tr,), in_specs=[blk] * n, out_specs=[blk] * len(out_dtypes),
        out_shape=[jax.ShapeDtypeStruct((R, C), d) for d in out_dtypes],
        compiler_params=_params(("parallel",)), name=name)(*arrs)
    return outs


def _adamw(w, g, m, v, name):
    R, C = w.shape
    tr = _row_tile(R, C)
    c1 = 1.0 - ADAM_B1 ** ADAM_STEP
    c2 = 1.0 - ADAM_B2 ** ADAM_STEP

    def body(w_ref, g_ref, m_ref, v_ref, d_ref, mo_ref, vo_ref):
        g_t = g_ref[...]
        m_new = ADAM_B1 * m_ref[...] + (1.0 - ADAM_B1) * g_t
        v_new = ADAM_B2 * v_ref[...] + (1.0 - ADAM_B2) * (g_t * g_t)
        d_ref[...] = -ADAM_LR * ((m_new / c1) / (jnp.sqrt(v_new / c2) + ADAM_EPS) + ADAM_WD * w_ref[...])
        mo_ref[...] = m_new
        vo_ref[...] = v_new

    blk = pl.BlockSpec((tr, C), lambda i: (i, 0))
    return pl.pallas_call(
        body, grid=(R // tr,), in_specs=[blk] * 4, out_specs=[blk] * 3,
        out_shape=[jax.ShapeDtypeStruct((R, C), F32)] * 3,
        compiler_params=_params(("parallel",)), name=name)(w, g, m, v)


D_MODEL = 1024
SSM_INNER = SSM_HEADS * HEAD_DIM
CONV_DIM = SSM_INNER + 2 * SSM_GROUPS * SSM_STATE
ATTN_WIDTH = ATTN_HEADS * HEAD_DIM
COL_Z = 0
COL_XBC = COL_Z + SSM_INNER
COL_Q = COL_XBC + CONV_DIM
COL_K = COL_Q + ATTN_WIDTH
COL_V = COL_K + ATTN_WIDTH
COL_DT = COL_V + ATTN_WIDTH
COL_F = COL_DT + SSM_HEADS
IN_COLS = COL_F + ATTN_HEADS
IN_COLS_PAD = -(-IN_COLS // LANES) * LANES
REF_COL_DT = COL_Q


def _to_kernel_cols(w):
    return jnp.concatenate([w[:, :REF_COL_DT], w[:, REF_COL_DT + SSM_HEADS:COL_F], w[:, REF_COL_DT:REF_COL_DT + SSM_HEADS],
                            w[:, COL_F:IN_COLS]], axis=1)


def _to_reference_cols(w):
    return jnp.concatenate([w[:, :COL_Q], w[:, COL_DT:COL_DT + SSM_HEADS], w[:, COL_Q:COL_DT], w[:, COL_F:IN_COLS]],
                           axis=1)


def _to_heads(a):
    S = a.shape[0]
    return a.reshape(S, -1, HEAD_DIM).transpose(1, 0, 2)


def _from_heads(a):
    return a.transpose(1, 0, 2).reshape(a.shape[1], -1)


def _add_residual(acc, res):
    return (res + acc,)


def _relu2(acc):
    r = jnp.maximum(acc, 0.0)
    return acc, r * r


def _relu2_bwd(acc, a):
    return (acc * (2.0 * jnp.maximum(a, 0.0)),)


def _layer_fwd_bwd(x, mem, target, w_in, p, late_weights, send_late_grads):
    S = x.shape[0]
    hd3 = lambda a: a.reshape(SSM_HEADS, 1, 1)

    h1 = _rmsnorm_fwd(x, p["g_mix"], "norm_mix")
    proj = _mm(h1, w_in, "nn", "in_proj")
    xbc = _conv_fwd(proj, COL_XBC, CONV_DIM, p["conv_w"], p["conv_b"])
    xs_hm = _to_heads(xbc[:, :SSM_INNER])
    z_hm = _to_heads(proj[:, COL_Z:COL_Z + SSM_INNER])
    dt_hm = proj[:, COL_DT:COL_DT + SSM_HEADS].T[:, :, None]
    ssd_par = (hd3(p["dt_bias"]), hd3(p["a_log"]), hd3(p["d_skip"]), p["ssm_norm_w"].reshape(SSM_HEADS, 1, HEAD_DIM))
    y_hm, hs = _ssd_fwd(xs_hm, xbc, z_hm, dt_hm, *ssd_par)
    f_raw = proj[:, COL_F:COL_F + ATTN_HEADS]
    gq2 = jnp.tile(p["g_q"], (1, 2))
    gk2 = jnp.tile(p["g_k"], (1, 2))
    qs, kn, vb = _qk_prep_fwd(proj, gq2, gk2)
    cum = _logf_cumsum_fwd(f_raw, p["f_bias"])
    cq = cum.T[:, :, None]
    ck = cum.T[:, None, :]
    o, o_fine, lse = _flash_fwd(qs, kn, vb, cq, ck)
    y = _from_heads(y_hm).astype(BF16)
    W = late_weights((o_fine, y))
    x1 = _mm(y, W["w_out"][:SSM_INNER], "nn", "out_proj_ssm", epilogue=_add_residual, extras=(x,))
    x1 = _mm(o, W["w_out"][SSM_INNER:], "nn", "out_proj_attn", epilogue=_add_residual, extras=(x1,))
    h2 = _rmsnorm_fwd(x1, p["g_xattn"], "norm_xattn")
    mem_n = _rmsnorm_fwd(mem, p["g_mem"], "norm_mem")
    xq = _mm(h2, W["xq_w"], "nn", "xq_proj")
    kv = _mm(mem_n, W["xkv_w"], "nn", "xkv_proj", b_chunks=N_CHIPS)
    xo = _xattn_fwd(xq, kv, p["xg_q"], p["xg_k"])
    x2 = _mm(xo, W["xo_w"], "nn", "xo_proj", epilogue=_add_residual, extras=(x1,))
    h3 = _rmsnorm_fwd(x2, p["g_mlp"], "norm_mlp")
    a, act = _mm(h3, W["w_up"], "nn", "mlp_up", out_dtypes=(F32, BF16), epilogue=_relu2, b_chunks=N_CHIPS)
    x3 = _mm(act, W["w_down"], "nn", "mlp_down", epilogue=_add_residual, extras=(x2,))
    dy, loss_row = _loss_head(x3, target)

    gW, gp = {}, {}
    da = _mm(dy, W["w_down"], "nt", "d_act", out_dtypes=(BF16,), epilogue=_relu2_bwd, extras=(a,))
    gW["w_down"] = _mm(act, dy, "tn", "g_w_down", out_dtypes=(BF16,))
    gW["w_up"] = _mm(h3, da, "tn", "g_w_up", out_dtypes=(BF16,), out_chunks=N_CHIPS)
    dh3 = _mm(da, W["w_up"], "nt", "d_h3", b_chunks=N_CHIPS)
    dx2, gp["g_mlp"] = _rmsnorm_bwd(x2, p["g_mlp"], dh3, dy, "norm_mlp_bwd")
    dxo = _mm(dx2, W["xo_w"], "nt", "d_xo", out_dtypes=(BF16,))
    gW["xo_w"] = _mm(xo, dx2, "tn", "g_xo_w", out_dtypes=(BF16,))
    dxq, dk_x, dv_x, gp["xg_q"], gp["xg_k"] = _xattn_bwd(xq, kv, p["xg_q"], p["xg_k"], dxo)
    dkv = jnp.concatenate([dk_x, dv_x], axis=-1)
    gW["xq_w"] = _mm(h2, dxq, "tn", "g_xq_w", out_dtypes=(BF16,))
    dh2 = _mm(dxq, W["xq_w"], "nt", "d_h2")
    gW["xkv_w"] = _mm(mem_n, dkv, "tn", "g_xkv_w", out_dtypes=(BF16,), out_chunks=N_CHIPS)
    dmem_n = _mm(dkv, W["xkv_w"], "nt", "d_mem_n", b_chunks=N_CHIPS)
    _, gp["g_mem"] = _rmsnorm_bwd(mem, p["g_mem"], dmem_n, None, "norm_mem_bwd")
    dx1, gp["g_xattn"] = _rmsnorm_bwd(x1, p["g_xattn"], dh2, dx2, "norm_xattn_bwd")
    dmixed = _mm(dx1, W["w_out"], "nt", "d_mixed")
    gW["w_out"] = jnp.concatenate([_mm(y, dx1, "tn", "g_w_out_ssm", out_dtypes=(BF16,)),
                                   _mm(o, dx1, "tn", "g_w_out_attn", out_dtypes=(BF16,))], axis=0)
    token = send_late_grads(gW)
    dy_hm = _to_heads(dmixed[:, :SSM_INNER])
    dqs, dkn, dv, dck = _flash_bwd(qs, kn, vb, cq + token[:1, :1], ck, o_fine, dmixed, SSM_INNER, lse)
    dq_raw, dk_raw, dgq2, dgk2 = _qk_prep_bwd(proj, gq2, gk2, dqs, dkn)
    gp["g_q"] = dgq2[:, :HEAD_DIM] + dgq2[:, HEAD_DIM:]
    gp["g_k"] = dgk2[:, :HEAD_DIM] + dgk2[:, HEAD_DIM:]
    df, gp["f_bias"] = _logf_cumsum_bwd(f_raw, p["f_bias"], dck[:, 0, :].T)
    dxs_hm, dz_hm, dB, dC, ddt, ddtb, dalog, ddsk, dnw = _ssd_bwd(xs_hm, xbc, z_hm, dt_hm, *ssd_par, hs, dy_hm)
    gp["dt_bias"] = ddtb.reshape(1, SSM_HEADS)
    gp["a_log"] = dalog.reshape(1, SSM_HEADS)
    gp["d_skip"] = ddsk.reshape(1, SSM_HEADS)
    gp["ssm_norm_w"] = dnw.reshape(1, SSM_INNER)
    dxbc = jnp.concatenate([_from_heads(dxs_hm), dB, dC], axis=-1)
    dxbc_raw, dconv_w, gp["conv_b"] = _conv_bwd(proj, COL_XBC, CONV_DIM, p["conv_w"], p["conv_b"], dxbc)
    gp["conv_w"] = dconv_w[:CONV_WIDTH]
    dproj = jnp.concatenate(
        [_from_heads(dz_hm).astype(BF16), dxbc_raw, dq_raw, dk_raw, dv, ddt[:, :, 0].T.astype(BF16), df.astype(BF16),
         jnp.zeros((S, IN_COLS_PAD - IN_COLS), BF16)], axis=-1)
    g_w_in = _mm(h1, dproj, "tn", "g_w_in", out_dtypes=(BF16,))
    dh1 = _mm(dproj, w_in, "nt", "d_h1")
    dx, gp["g_mix"] = _rmsnorm_bwd(x, p["g_mix"], dh1, dx1, "norm_mix_bwd")
    return loss_row, dx, g_w_in, gp


_ANY = pl.BlockSpec(memory_space=pl.ANY)


def _place():
    x, y, c = lax.axis_index("x"), lax.axis_index("y"), lax.axis_index("c")
    chips = [(1 - x, y), (x, 1 - y), (1 - x, 1 - y)]
    return x, y, c, chips


def _chip_index(px, py):
    return 2 * px + py


def _all_gather_chips(split, whole):
    ns, nw = len(split), len(whole)
    n = ns + nw

    def body(*refs):
        ins, outs = refs[:n], refs[n:2 * n]
        send_ici, recv_ici, send_d2d, recv_d2d = refs[2 * n:]
        x, y, c, chips = _place()
        me = _chip_index(x, y)
        sib = (x, y, 1 - c)

        def ici(k, j, src, dst):
            return pltpu.make_async_remote_copy(src_ref=src, dst_ref=dst, send_sem=send_ici.at[3 * k + j],
                                                recv_sem=recv_ici.at[3 * k + j], device_id=(*chips[j], c),
                                                device_id_type=MESH)

        def d2d(k, j, piece):
            return pltpu.make_async_remote_copy(src_ref=piece, dst_ref=piece, send_sem=send_d2d.at[3 * k + j],
                                                recv_sem=recv_d2d.at[3 * k + j], device_id=sib, device_id_type=MESH)

        sends = []
        for k in range(n):
            for j in range(3):
                if k < ns:
                    sends.append(ici(k, j, ins[k].at[c], outs[k].at[me, c]))
                else:
                    sends.append(ici(k, j, ins[k], outs[k].at[me]))
                sends[-1].start()
        passed = []
        for k in range(n):
            for j in range(3):
                src_chip = _chip_index(*chips[j])
                if k < ns:
                    ici(k, j, ins[k].at[c], outs[k].at[src_chip, c]).wait_recv()
                    passed.append(d2d(k, j, outs[k].at[src_chip, c]))
                    passed[-1].start()
                else:
                    ici(k, j, ins[k], outs[k].at[src_chip]).wait_recv()
        for k in range(ns):
            for j in range(3):
                d2d(k, j, outs[k].at[_chip_index(*chips[j]), 1 - c]).wait_recv()
        for cp in sends + passed:
            cp.wait_send()

    arrs = list(split) + list(whole)
    return pl.pallas_call(
        body, in_specs=[_ANY] * n, out_specs=[_ANY] * n,
        out_shape=[jax.ShapeDtypeStruct((N_CHIPS,) + a.shape, a.dtype) for a in arrs],
        scratch_shapes=[pltpu.SemaphoreType.DMA((3 * n,)), pltpu.SemaphoreType.DMA((3 * n,)),
                        pltpu.SemaphoreType.DMA((3 * ns,)), pltpu.SemaphoreType.DMA((3 * ns,))],
        name="all_gather_chips")(*arrs)


def _sibling_send_halves(grads, name):
    n = len(grads)

    def body(*refs):
        ins, outs = refs[:n], refs[n:2 * n]
        send_sem, recv_sem = refs[2 * n:]
        x, y, c, _ = _place()

        def cp(k, j, half):
            return pltpu.make_async_remote_copy(src_ref=ins[k].at[j, half], dst_ref=outs[k].at[j],
                                                send_sem=send_sem.at[N_CHIPS * k + j],
                                                recv_sem=recv_sem.at[N_CHIPS * k + j],
                                                device_id=(x, y, 1 - c), device_id_type=MESH)

        copies = [cp(k, j, 1 - c) for k in range(n) for j in range(N_CHIPS)]
        for q in copies:
            q.start()
        for q in copies:
            q.wait()

    return pl.pallas_call(
        body, in_specs=[_ANY] * n, out_specs=[_ANY] * n,
        out_shape=[jax.ShapeDtypeStruct((N_CHIPS,) + g.shape[2:], g.dtype) for g in grads],
        scratch_shapes=[pltpu.SemaphoreType.DMA((N_CHIPS * n,)), pltpu.SemaphoreType.DMA((N_CHIPS * n,))],
        name=name)(*grads)


def _chips_send_shards(parts):
    n = len(parts)

    def body(*refs):
        ins, outs = refs[:n], refs[n:2 * n]
        send_sem, recv_sem = refs[2 * n:]
        x, y, c, chips = _place()

        def cp(k, j):
            return pltpu.make_async_remote_copy(src_ref=ins[k].at[_chip_index(*chips[j])], dst_ref=outs[k].at[j],
                                                send_sem=send_sem.at[3 * k + j], recv_sem=recv_sem.at[3 * k + j],
                                                device_id=(*chips[j], c), device_id_type=MESH)

        copies = [cp(k, j) for k in range(n) for j in range(3)]
        for q in copies:
            q.start()
        for q in copies:
            q.wait()

    return pl.pallas_call(
        body, in_specs=[_ANY] * n, out_specs=[_ANY] * n,
        out_shape=[jax.ShapeDtypeStruct((3,) + g.shape[1:], g.dtype) for g in parts],
        scratch_shapes=[pltpu.SemaphoreType.DMA((3 * n,)), pltpu.SemaphoreType.DMA((3 * n,))],
        name="rs_chip_shards")(*parts)


def _sibling_exchange(halves):
    n = len(halves)

    def body(*refs):
        ins, outs = refs[:n], refs[n:2 * n]
        send_sem, recv_sem = refs[2 * n:]
        x, y, c, _ = _place()

        def cp(k, half):
            return pltpu.make_async_remote_copy(src_ref=ins[k], dst_ref=outs[k].at[half], send_sem=send_sem.at[k],
                                                recv_sem=recv_sem.at[k], device_id=(x, y, 1 - c), device_id_type=MESH)

        sends = [cp(k, c) for k in range(n)]
        for q in sends:
            q.start()
        for k in range(n):
            cp(k, 1 - c).wait_recv()
        for q in sends:
            q.wait_send()

    return pl.pallas_call(
        body, in_specs=[_ANY] * n, out_specs=[_ANY] * n,
        out_shape=[jax.ShapeDtypeStruct((2,) + h.shape, h.dtype) for h in halves],
        scratch_shapes=[pltpu.SemaphoreType.DMA((n,)), pltpu.SemaphoreType.DMA((n,))],
        name="rs_sibling_exchange")(*halves)


_HBM = pl.BlockSpec(memory_space=pltpu.HBM)
_SEM = pl.BlockSpec(memory_space=pltpu.SEMAPHORE)
_SPLIT_EFFECT = pltpu.SideEffectType.DATAFLOW_SIDE_EFFECTING


class _Split(NamedTuple):
    send_sems: jax.Array
    recv_sems: jax.Array
    sources: tuple
    lands: tuple
    token: jax.Array


def _split_copies(kind, srcs, lands, send_sems, recv_sems):
    x, y, c, chips = _place()
    me = _chip_index(x, y)
    copies = []
    for k in range(len(srcs)):
        for j in range(3):
            if kind == "gather":
                src, dst = srcs[k], lands[k].at[me]
            else:
                src, dst = srcs[k].at[_chip_index(*chips[j])], lands[k].at[j]
            copies.append(pltpu.make_async_remote_copy(
                src_ref=src, dst_ref=dst, send_sem=send_sems.at[3 * k + j], recv_sem=recv_sems.at[3 * k + j],
                device_id=(*chips[j], c), device_id_type=MESH))
    return copies


def _split_start(name, sources, kind, after):
    n = len(sources)
    if kind == "gather":
        lands = [lax.empty((N_CHIPS,) + s.shape, s.dtype) for s in sources]
    else:
        lands = [lax.empty((3,) + s.shape[1:], s.dtype) for s in sources]
    deps = [] if after is None else [after]

    def body(*refs):
        srcs, lnds = refs[:n], refs[n:2 * n]
        send_sems, recv_sems = refs[2 * n + len(deps)], refs[2 * n + len(deps) + 1]
        for cp in _split_copies(kind, srcs, lnds, send_sems, recv_sems):
            cp.start()
        refs[-1][...] = jnp.zeros_like(refs[-1])

    hbm = lambda a: pltpu.with_memory_space_constraint(a, pltpu.HBM)
    outs = pl.pallas_call(
        body, name=name,
        in_specs=[_HBM] * (2 * n) + [_ANY] * len(deps),
        out_specs=[_SEM, _SEM] + [_HBM] * (2 * n) + [pl.BlockSpec(memory_space=pltpu.VMEM)],
        out_shape=[pltpu.SemaphoreType.DMA((3 * n,)), pltpu.SemaphoreType.DMA((3 * n,))]
        + [pltpu.HBM(a.shape, a.dtype) for a in list(sources) + lands] + [jax.ShapeDtypeStruct((8, LANES), F32)],
        input_output_aliases={k: 2 + k for k in range(2 * n)},
        compiler_params=pltpu.CompilerParams(has_side_effects=_SPLIT_EFFECT),
    )(*[hbm(s) for s in sources], *[hbm(l) for l in lands], *deps)
    return _Split(outs[0], outs[1], tuple(outs[2:2 + n]), tuple(outs[2 + n:2 + 2 * n]), outs[-1])


def _split_wait(name, h, kind, after):
    n = len(h.sources)

    def body(*refs):
        srcs, lnds = refs[:n], refs[n:2 * n]
        for cp in _split_copies(kind, srcs, lnds, refs[2 * n], refs[2 * n + 1]):
            cp.wait_send()
            cp.wait_recv()

    outs = pl.pallas_call(
        body, name=name,
        in_specs=[_HBM] * (2 * n) + [_SEM, _SEM] + [_ANY] * len(after),
        out_specs=[_HBM] * (2 * n),
        out_shape=[pltpu.HBM(a.shape, a.dtype) for a in h.sources + h.lands],
        input_output_aliases={k: k for k in range(2 * n)},
        compiler_params=pltpu.CompilerParams(has_side_effects=_SPLIT_EFFECT),
    )(*h.sources, *h.lands, h.send_sems, h.recv_sems, *after)
    return outs[:n], outs[n:]


def _all_reduce_small(vec):
    R = vec.shape[0]

    def body(v_ref, o_ref, buf, send_sem, recv_sem):
        x, y, c = lax.axis_index("x"), lax.axis_index("y"), lax.axis_index("c")
        me = 4 * x + 2 * y + c
        buf[me] = v_ref[...]
        copies = []
        for r in range(1, N_DEV):
            fx, fy, fc = (r >> 2) & 1, (r >> 1) & 1, r & 1
            peer = (x ^ fx, y ^ fy, c ^ fc)
            copies.append(pltpu.make_async_remote_copy(src_ref=v_ref, dst_ref=buf.at[me], send_sem=send_sem.at[r - 1],
                                                       recv_sem=recv_sem.at[r - 1], device_id=peer, device_id_type=MESH))
        for q in copies:
            q.start()
        for r in range(1, N_DEV):
            fx, fy, fc = (r >> 2) & 1, (r >> 1) & 1, r & 1
            src = 4 * (x ^ fx) + 2 * (y ^ fy) + (c ^ fc)
            pltpu.make_async_remote_copy(src_ref=v_ref, dst_ref=buf.at[src], send_sem=send_sem.at[r - 1],
                                         recv_sem=recv_sem.at[r - 1], device_id=(x, y, c), device_id_type=MESH).wait_recv()
        acc = buf[0]
        for d in range(1, N_DEV):
            acc = acc + buf[d]
        o_ref[...] = acc
        for q in copies:
            q.wait_send()

    vm = pl.BlockSpec(memory_space=pltpu.VMEM)
    return pl.pallas_call(
        body, in_specs=[vm], out_specs=vm, out_shape=jax.ShapeDtypeStruct((R, LANES), F32),
        scratch_shapes=[pltpu.VMEM((N_DEV, R, LANES), F32), pltpu.SemaphoreType.DMA((N_DEV - 1,)),
                        pltpu.SemaphoreType.DMA((N_DEV - 1,))],
        name="all_reduce_small")(vec)


_INPUTS = ["x", "mem", "g_mix", "w_in", "conv_w", "conv_b", "dt_bias", "a_log", "d_skip", "ssm_norm_w", "g_q", "g_k",
           "f_bias", "w_out", "g_xattn", "g_mem", "xq_w", "xkv_w", "xg_q", "xg_k", "xo_w", "g_mlp", "w_up", "w_down"]
_WEIGHTS = _INPUTS[2:]
_BIG = ["w_in", "w_out", "xq_w", "xkv_w", "xo_w", "w_up", "w_down"]
_LATE = _BIG[1:]
_COL_SHARDED = ["w_in", "xkv_w", "w_up"]
_SMALL = [n for n in _WEIGHTS if n not in _BIG]


def _pack_rows(arrs):
    rows = []
    for a in arrs:
        flat = a.reshape(-1)
        pad = -flat.shape[0] % LANES
        rows.append(jnp.pad(flat, (0, pad)).reshape(-1, LANES))
    out = jnp.concatenate(rows, axis=0)
    return jnp.pad(out, ((0, -out.shape[0] % 8), (0, 0)))


def _unpack_rows(packed, shapes):
    out, r = [], 0
    for s in shapes:
        n = math.prod(s)
        nr = -(-n // LANES)
        out.append(packed[r:r + nr].reshape(-1)[:n].reshape(s))
        r += nr
    return out


def kernel(x, mem, g_mix, w_in, conv_w, conv_b, dt_bias, a_log, d_skip, ssm_norm_w, g_q, g_k, f_bias, w_out, g_xattn, g_mem, xq_w, xkv_w, xg_q, xg_k, xo_w, g_mlp, w_up, w_down, loss_target, m_g_mix, m_w_in, m_conv_w, m_conv_b, m_dt_bias, m_a_log, m_d_skip, m_ssm_norm_w, m_g_q, m_g_k, m_f_bias, m_w_out, m_g_xattn, m_g_mem, m_xq_w, m_xkv_w, m_xg_q, m_xg_k, m_xo_w, m_g_mlp, m_w_up, m_w_down, v_g_mix, v_w_in, v_conv_w, v_conv_b, v_dt_bias, v_a_log, v_d_skip, v_ssm_norm_w, v_g_q, v_g_k, v_f_bias, v_w_out, v_g_xattn, v_g_mem, v_xq_w, v_xkv_w, v_xg_q, v_xg_k, v_xo_w, v_g_mlp, v_w_up, v_w_down):
    args = (x, mem, g_mix, w_in, conv_w, conv_b, dt_bias, a_log, d_skip, ssm_norm_w, g_q, g_k, f_bias, w_out, g_xattn,
            g_mem, xq_w, xkv_w, xg_q, xg_k, xo_w, g_mlp, w_up, w_down)
    w = dict(zip(_INPUTS, args))
    mom1 = dict(zip(_WEIGHTS, (m_g_mix, m_w_in, m_conv_w, m_conv_b, m_dt_bias, m_a_log, m_d_skip, m_ssm_norm_w, m_g_q,
                               m_g_k, m_f_bias, m_w_out, m_g_xattn, m_g_mem, m_xq_w, m_xkv_w, m_xg_q, m_xg_k, m_xo_w,
                               m_g_mlp, m_w_up, m_w_down)))
    mom2 = dict(zip(_WEIGHTS, (v_g_mix, v_w_in, v_conv_w, v_conv_b, v_dt_bias, v_a_log, v_d_skip, v_ssm_norm_w, v_g_q,
                               v_g_k, v_f_bias, v_w_out, v_g_xattn, v_g_mem, v_xq_w, v_xkv_w, v_xg_q, v_xg_k, v_xo_w,
                               v_g_mlp, v_w_up, v_w_down)))
    chip = _chip_index(lax.axis_index("x"), lax.axis_index("y"))
    core = lax.axis_index("c")

    shard_bf = {n: w[n][0].astype(BF16) for n in _BIG}

    def layout_for_compute(n, g):
        if n == "w_in":
            g = _to_kernel_cols(g.transpose(1, 0, 2).reshape(g.shape[1], IN_COLS))
            return jnp.pad(g, ((0, 0), (0, IN_COLS_PAD - IN_COLS)))
        return g if n in _COL_SHARDED else g.reshape(N_CHIPS * g.shape[1], g.shape[2])

    def layout_for_reduction(n, g):
        if n == "w_in":
            g = _to_reference_cols(g).reshape(g.shape[0], N_CHIPS, IN_COLS // N_CHIPS).transpose(1, 0, 2)
        elif n not in _COL_SHARDED:
            g = g.reshape(N_CHIPS, g.shape[0] // N_CHIPS, g.shape[1])
        return g.reshape(N_CHIPS, 2, g.shape[1] // 2, g.shape[2])

    def pair_sums_of(names, grads, tag):
        grads4 = [layout_for_reduction(n, grads[n]) for n in names]
        from_sibling = _sibling_send_halves(grads4, "rs_sibling_halves_" + tag)
        sums = []
        for n, g, fs in zip(names, grads4, from_sibling):
            mine = lax.dynamic_index_in_dim(g, core, axis=1, keepdims=False)
            flat = lambda a: a.reshape(-1, a.shape[-1])
            (s,) = _nsum([flat(mine), flat(fs)], (BF16,), "rs_pair_sum_" + n)
            sums.append(s.reshape(mine.shape))
        return sums

    def chip_sums_of(names, pair_sums, from_chips):
        out = []
        for n, ps, fc in zip(names, pair_sums, from_chips):
            own = lax.dynamic_index_in_dim(ps, chip, axis=0, keepdims=False)
            (r,) = _nsum([own, fc[0], fc[1], fc[2]], (F32,), "rs_chip_sum_" + n)
            out.append(r)
        return out

    halves_in = shard_bf["w_in"].reshape(2, shard_bf["w_in"].shape[0] // 2, -1)
    g_in, g_conv = _all_gather_chips([halves_in], [w["conv_w"][0]])
    g_in = lax.dynamic_update_index_in_dim(g_in, halves_in, chip, axis=0)
    g_conv = lax.dynamic_update_index_in_dim(g_conv, w["conv_w"][0], chip, axis=0)
    w_in_full = layout_for_compute("w_in", g_in.reshape(N_CHIPS, -1, g_in.shape[-1]))
    p = {n: w[n] for n in _SMALL}
    p["conv_w"] = g_conv.transpose(1, 0, 2).reshape(CONV_WIDTH, CONV_DIM)
    gather = _split_start("gather_late", [shard_bf[n] for n in _LATE], "gather", after=g_in)
    p["g_mix"] = p["g_mix"] + gather.token[:1, :1]

    def late_weights(after):
        srcs, lands = _split_wait("gather_late_wait", gather, "gather", after)
        lands = [lax.dynamic_update_index_in_dim(l, s, chip, axis=0) for l, s in zip(lands, srcs)]
        return {n: layout_for_compute(n, l) for n, l in zip(_LATE, lands)}

    scatter = {}

    def send_late_grads(grads):
        sums = pair_sums_of(_LATE, grads, "late")
        scatter["h"] = _split_start("scatter_late", sums, "scatter", after=None)
        return scatter["h"].token

    loss_row, dx, g_w_in, gp = _layer_fwd_bwd(x[0], mem[0], loss_target[0], w_in_full, p, late_weights, send_late_grads)

    sums_late, from_chips_late = _split_wait("scatter_late_wait", scatter["h"], "scatter", (dx,))
    sums_in = pair_sums_of(["w_in"], {"w_in": g_w_in}, "w_in")
    from_chips_in = _chips_send_shards(sums_in)
    reduced = chip_sums_of(["w_in"] + _LATE, sums_in + list(sums_late), list(from_chips_in) + list(from_chips_late))
    grad_shards = [lax.dynamic_update_index_in_dim(g, r, core, axis=0)
                   for g, r in zip(_sibling_exchange(reduced), reduced)]

    small_shapes = [gp[n].shape for n in _SMALL] + [(1, LANES)]
    packed = _pack_rows([gp[n] for n in _SMALL] + [loss_row])
    summed = _unpack_rows(_all_reduce_small(packed), small_shapes)
    gsmall = dict(zip(_SMALL, summed[:-1]))
    loss = summed[-1][0, 0]
    shard_cols = CONV_DIM // N_CHIPS
    gsmall["conv_w"] = lax.dynamic_slice_in_dim(gsmall["conv_w"], chip * shard_cols, shard_cols, axis=1)

    grad, delta, new_m, new_v = {}, {}, {}, {}
    for k, n in enumerate(_BIG):
        shape = w[n].shape
        g2 = grad_shards[k].reshape(shape[1], shape[2])
        d, m1, v1 = _adamw(w[n][0], g2, mom1[n][0], mom2[n][0], "adamw_" + n)
        grad[n], delta[n], new_m[n], new_v[n] = (a.reshape(shape) for a in (g2, d, m1, v1))
    pk = lambda src: _pack_rows([src[n] for n in _SMALL])
    for n in _SMALL:
        gsmall[n] = gsmall[n].reshape(w[n].shape)
    d, m1, v1 = _adamw(pk(w), pk(gsmall), pk(mom1), pk(mom2), "adamw_small")
    shapes = [w[n].shape for n in _SMALL]
    for n, dn, mn, vn in zip(_SMALL, _unpack_rows(d, shapes), _unpack_rows(m1, shapes), _unpack_rows(v1, shapes)):
        grad[n], delta[n], new_m[n], new_v[n] = gsmall[n], dn, mn, vn

    return (loss, dx[None], *[grad[n] for n in _WEIGHTS], *[delta[n] for n in _WEIGHTS],
            *[new_m[n] for n in _WEIGHTS], *[new_v[n] for n in _WEIGHTS])
```

```python
import math
from typing import NamedTuple

import jax
import jax.numpy as jnp
from jax import lax
from jax.experimental import pallas as pl
from jax.experimental.pallas import tpu as pltpu

F32 = jnp.float32
BF16 = jnp.bfloat16
HI = lax.Precision.HIGHEST
MESH = pl.DeviceIdType.MESH

EPS = 1e-5
CHUNK = 128
SSM_HEADS = 16
SSM_GROUPS = 2
HEADS_PER_GROUP = SSM_HEADS // SSM_GROUPS
HEAD_DIM = 64
SSM_STATE = 128
ATTN_HEADS = 16
XATTN_HEADS = 4
XATTN_DIM = 256
CONV_WIDTH = 4
N_CHIPS = 4
N_DEV = 8
LANES = 128
VMEM_LIMIT = 56 * 1024 * 1024

ADAM_LR = 0.001
ADAM_B1 = 0.9
ADAM_B2 = 0.999
ADAM_EPS = 1e-08
ADAM_WD = 0.01
ADAM_STEP = 10


def _params(sem):
    return pltpu.CompilerParams(dimension_semantics=sem, vmem_limit_bytes=VMEM_LIMIT)


def _pick(n, cands):
    for c in cands:
        if n % c == 0:
            return c
    return n


def _mm(a, b, mode, name, out_dtypes=(F32,), epilogue=None, extras=(), b_chunks=1, out_chunks=1,
        tm=None, tn=None, tk=None):
    if mode == "nn":
        M, K = a.shape
        N = b.shape[-1] * b_chunks
    elif mode == "nt":
        M, K = a.shape
        N = b.shape[-2]
        assert b.shape[-1] * b_chunks == K
    else:
        K, M = a.shape
        N = b.shape[-1] * b_chunks
    tm = tm or _pick(M, (2048, 1024, 512, 256, 128))
    tn = tn or _pick(N // max(b_chunks if mode != "nt" else 1, out_chunks), (512, 640, 384, 256, 128))
    if tk is None:
        kmax = b.shape[-1] if mode == "nt" else K
        tk = kmax if kmax <= 2048 else _pick(kmax, (2048, 1152, 1024, 512))
    nk = K // tk
    assert M % tm == 0 and N % tn == 0 and K % tk == 0
    grid = (M // tm, N // tn, nk)

    if mode == "tn":
        a_spec = pl.BlockSpec((tk, tm), lambda i, j, k: (k, i))
    else:
        a_spec = pl.BlockSpec((tm, tk), lambda i, j, k: (i, k))

    def b_index(t_row, t_last, tile_last):
        if b_chunks == 1:
            return (t_row, t_last)
        q = (b.shape[-1]) // tile_last
        return (t_last // q, t_row, t_last % q)

    if mode == "nn" or mode == "tn":
        bshape = (tk, tn)
        bmap = lambda i, j, k: b_index(k, j, tn)
    else:
        bshape = (tn, tk)
        bmap = lambda i, j, k: b_index(j, k, tk)
    if b_chunks > 1:
        bshape = (None,) + bshape
    b_spec = pl.BlockSpec(bshape, bmap)

    if out_chunks == 1:
        o_spec = pl.BlockSpec((tm, tn), lambda i, j, k: (i, j))
        o_shape = (M, N)
    else:
        qo = (N // out_chunks) // tn
        o_spec = pl.BlockSpec((None, tm, tn), lambda i, j, k: (j // qo, i, j % qo))
        o_shape = (out_chunks, M, N // out_chunks)
    e_spec = pl.BlockSpec((tm, tn), lambda i, j, k: (i, j))

    dims = {"nn": (((1,), (0,)), ((), ())), "nt": (((1,), (1,)), ((), ())), "tn": (((0,), (0,)), ((), ()))}[mode]
    n_ex = len(extras)
    n_out = len(out_dtypes)

    def body(*refs):
        a_ref, b_ref = refs[0], refs[1]
        ex_refs = refs[2:2 + n_ex]
        o_refs = refs[2 + n_ex:2 + n_ex + n_out]

        def finish(acc):
            outs = epilogue(acc, *[r[...] for r in ex_refs]) if epilogue is not None else (acc,)
            for r, o in zip(o_refs, outs):
                r[...] = o.astype(r.dtype)

        part = lax.dot_general(a_ref[...].astype(BF16), b_ref[...].astype(BF16), dims,
                               preferred_element_type=F32)
        if nk == 1:
            finish(part)
        else:
            acc_ref = refs[-1]
            k = pl.program_id(2)

            @pl.when(k == 0)
            def _():
                acc_ref[...] = part

            @pl.when(k > 0)
            def _():
                acc_ref[...] += part

            @pl.when(k == nk - 1)
            def _():
                finish(acc_ref[...])

    outs = pl.pallas_call(
        body,
        grid=grid,
        in_specs=[a_spec, b_spec] + [e_spec] * n_ex,
        out_specs=[o_spec] * n_out,
        out_shape=[jax.ShapeDtypeStruct(o_shape, d) for d in out_dtypes],
        scratch_shapes=[pltpu.VMEM((tm, tn), F32)] if nk > 1 else [],
        compiler_params=_params(("parallel", "parallel", "arbitrary")),
        name=name,
    )(a, b, *extras)
    return outs[0] if n_out == 1 else outs


def _rms(x, g):
    r = lax.rsqrt(jnp.mean(x * x, axis=-1, keepdims=True) + EPS)
    return x * r * g


def _rmsnorm_fwd(x, g, name):
    R, D = x.shape
    tr = _pick(R, (512, 256))

    def body(x_ref, g_ref, o_ref):
        o_ref[...] = _rms(x_ref[...], g_ref[...]).astype(o_ref.dtype)

    return pl.pallas_call(
        body, grid=(R // tr,),
        in_specs=[pl.BlockSpec((tr, D), lambda i: (i, 0)), pl.BlockSpec((1, D), lambda i: (0, 0))],
        out_specs=pl.BlockSpec((tr, D), lambda i: (i, 0)),
        out_shape=jax.ShapeDtypeStruct((R, D), BF16),
        compiler_params=_params(("parallel",)), name=name)(x, g)


def _rmsnorm_bwd(x, g, dh, dres, name):
    R, D = x.shape
    tr = _pick(R, (256,))
    has_res = dres is not None

    def body(*refs):
        if has_res:
            x_ref, g_ref, dh_ref, dres_ref, dx_ref, dg_ref = refs
        else:
            x_ref, g_ref, dh_ref, dx_ref, dg_ref = refs
        _, vjp = jax.vjp(_rms, x_ref[...], g_ref[...])
        dx, dg = vjp(dh_ref[...])
        if has_res:
            dx = dx + dres_ref[...]
        dx_ref[...] = dx

        @pl.when(pl.program_id(0) == 0)
        def _():
            dg_ref[...] = jnp.zeros_like(dg_ref)

        dg_ref[...] += dg

    row = pl.BlockSpec((tr, D), lambda i: (i, 0))
    vec = pl.BlockSpec((1, D), lambda i: (0, 0))
    ins = [x, g, dh] + ([dres] if has_res else [])
    return pl.pallas_call(
        body, grid=(R // tr,),
        in_specs=[row, vec, row] + ([row] if has_res else []),
        out_specs=[row, vec],
        out_shape=[jax.ShapeDtypeStruct((R, D), F32), jax.ShapeDtypeStruct((1, D), F32)],
        compiler_params=_params(("arbitrary",)), name=name)(*ins)


def _shift_down(u, k):
    if k == 0:
        return u
    rows = lax.broadcasted_iota(jnp.int32, u.shape, 0)
    return jnp.where(rows >= k, pltpu.roll(u, k, axis=0), 0.0)


def _shift_up(u, k):
    if k == 0:
        return u
    n = u.shape[0]
    rows = lax.broadcasted_iota(jnp.int32, u.shape, 0)
    return jnp.where(rows < n - k, pltpu.roll(u, n - k, axis=0), 0.0)


def _conv_pre(u, w, b):
    pre = b
    for j in range(CONV_WIDTH):
        pre = pre + w[j:j + 1, :] * _shift_down(u, CONV_WIDTH - 1 - j)
    return pre


def _conv_fwd(proj, col0, ncols, conv_w, conv_b):
    S = proj.shape[0]
    cb0 = col0 // LANES

    def body(u_ref, w_ref, b_ref, o_ref):
        pre = _conv_pre(u_ref[...], w_ref[...], b_ref[...])
        o_ref[...] = pre * jax.nn.sigmoid(pre)

    return pl.pallas_call(
        body, grid=(ncols // LANES,),
        in_specs=[pl.BlockSpec((S, LANES), lambda j: (0, j + cb0)),
                  pl.BlockSpec((CONV_WIDTH, LANES), lambda j: (0, j)),
                  pl.BlockSpec((1, LANES), lambda j: (0, j))],
        out_specs=pl.BlockSpec((S, LANES), lambda j: (0, j)),
        out_shape=jax.ShapeDtypeStruct((S, ncols), F32),
        compiler_params=_params(("parallel",)), name="conv_fwd")(proj, conv_w, conv_b)


def _conv_bwd(proj, col0, ncols, conv_w, conv_b, douts):
    S = proj.shape[0]
    cb0 = col0 // LANES
    starts = [0]
    for d in douts:
        starts.append(starts[-1] + d.shape[1] // LANES)
    assert starts[-1] == ncols // LANES
    nd = len(douts)

    def body(u_ref, w_ref, b_ref, *rest):
        d_refs, (du_ref, dw_ref, db_ref) = rest[:nd], rest[nd:]
        j = pl.program_id(0)
        dout = d_refs[-1][...]
        for i in range(nd - 2, -1, -1):
            dout = jnp.where(j < starts[i + 1], d_refs[i][...], dout)
        u = u_ref[...]
        w = w_ref[...]
        pre = _conv_pre(u, w, b_ref[...])
        s = jax.nn.sigmoid(pre)
        dpre = dout * (s * (1.0 + pre * (1.0 - s)))
        du = jnp.zeros_like(u)
        rows = []
        for j in range(CONV_WIDTH):
            k = CONV_WIDTH - 1 - j
            du = du + w[j:j + 1, :] * _shift_up(dpre, k)
            rows.append(jnp.sum(dpre * _shift_down(u, k), axis=0, keepdims=True))
        du_ref[...] = du.astype(du_ref.dtype)
        rows.append(jnp.zeros((8 - CONV_WIDTH, LANES), F32))
        dw_ref[...] = jnp.concatenate(rows, axis=0)
        db_ref[...] = jnp.sum(dpre, axis=0, keepdims=True)

    return pl.pallas_call(
        body, grid=(ncols // LANES,),
        in_specs=[pl.BlockSpec((S, LANES), lambda j: (0, j + cb0)),
                  pl.BlockSpec((CONV_WIDTH, LANES), lambda j: (0, j)),
                  pl.BlockSpec((1, LANES), lambda j: (0, j))]
        + [pl.BlockSpec((S, LANES), lambda j, lo=starts[i], hi=starts[i + 1]: (0, jnp.clip(j - lo, 0, hi - lo - 1)))
           for i in range(nd)],
        out_specs=[pl.BlockSpec((S, LANES), lambda j: (0, j)),
                   pl.BlockSpec((8, LANES), lambda j: (0, j)),
                   pl.BlockSpec((1, LANES), lambda j: (0, j))],
        out_shape=[jax.ShapeDtypeStruct((S, ncols), BF16),
                   jax.ShapeDtypeStruct((8, ncols), F32),
                   jax.ShapeDtypeStruct((1, ncols), F32)],
        compiler_params=_params(("parallel",)), name="conv_bwd")(proj, conv_w, conv_b, *douts)


def _softplus(x):
    return jnp.maximum(x, 0.0) + jnp.log1p(jnp.exp(-jnp.abs(x)))


def _dot32(a, b, dims=(((1,), (0,)), ((), ()))):
    return lax.dot_general(a, b, dims, precision=HI, preferred_element_type=F32)


def _dotd(a, b, dims=(((1,), (0,)), ((), ()))):
    return lax.dot_general(a, b, dims, preferred_element_type=F32)


PAIRS_PER_GROUP = HEADS_PER_GROUP // 2


def _ssd_chunk(xs, Bm, Cm, z, dtr, dtb, alog, dsk, nw, h):
    L = Bm.shape[0]
    ri = lax.broadcasted_iota(jnp.int32, (L, L), 0)
    ci = lax.broadcasted_iota(jnp.int32, (L, L), 1)
    causal = ri >= ci
    tril = causal.astype(F32)
    first = _first_head(L)
    first1 = _first_head(1)
    CB = _dotd(Cm, Bm, _NT)
    gated, hnew = [], []
    ssq = jnp.zeros((L, 1), F32)
    for pp in range(len(xs)):
        dts, cums, tots, decay = [], [], [], []
        for a in range(2):
            r = 2 * pp + a
            dt = _softplus(dtr[r] + dtb[r])
            dA = dt * (-jnp.exp(alog[r]))
            acs = _dot32(tril, dA)
            cc = jnp.broadcast_to(acs, (L, L))
            decay.append(CB * jnp.exp(jnp.where(causal, cc - cc.T, -1e30)))
            dts.append(dt)
            cums.append(acs)
            tots.append(jnp.sum(dA, axis=0, keepdims=True))
        dt2 = jnp.where(first, dts[0], dts[1])
        acs2 = jnp.where(first, cums[0], cums[1])
        tot2 = jnp.where(first1, tots[0], tots[1])
        dsk2 = jnp.where(first1, dsk[2 * pp], dsk[2 * pp + 1])
        X = xs[pp] * dt2
        y = (jnp.where(first, _dotd(decay[0], X), _dotd(decay[1], X)) + jnp.exp(acs2) * _dotd(Cm, h[pp])
             + dsk2 * xs[pp])
        hnew.append(jnp.exp(tot2) * h[pp] + _dotd(Bm, X * jnp.exp(tot2 - acs2), _TN))
        g = y * (z[pp] * jax.nn.sigmoid(z[pp]))
        ssq = ssq + jnp.sum(g * g, axis=-1, keepdims=True)
        gated.append(g)
    rs = lax.rsqrt(ssq / (len(xs) * LANES) + EPS)
    return [g * rs * nw[pp] for pp, g in enumerate(gated)], hnew


def _ssd_args(xs_ref, b_ref, c_ref, z_ref, dt_ref, dtb_ref, al_ref, dsk_ref, nw_ref, h_ref):
    pairs = range(PAIRS_PER_GROUP)
    heads = range(HEADS_PER_GROUP)
    lanes = lambda ref, pp: ref[:, pp * LANES:(pp + 1) * LANES]
    return ([lanes(xs_ref, pp) for pp in pairs], b_ref[...], c_ref[...], [lanes(z_ref, pp) for pp in pairs],
            [dt_ref[r] for r in heads], [dtb_ref[r] for r in heads], [al_ref[r] for r in heads],
            [dsk_ref[r] for r in heads], [lanes(nw_ref, pp) for pp in pairs], [h_ref[pp] for pp in pairs])


def _ssd_specs(rev):
    H, N, L = HEADS_PER_GROUP, SSM_STATE, CHUNK
    gw = H * HEAD_DIM
    return dict(
        cols=lambda col0: pl.BlockSpec((L, gw), lambda g, c: (rev(c), col0 // gw + g)),
        bc=lambda first_block: pl.BlockSpec((L, N), lambda g, c: (rev(c), first_block + g)),
        dt=pl.BlockSpec((H, L, 1), lambda g, c: (g, rev(c), 0)),
        scal=pl.BlockSpec((H, 1, 1), lambda g, c: (g, 0, 0)),
        nw=pl.BlockSpec((1, gw), lambda g, c: (0, g)),
        hs=pl.BlockSpec((None, PAIRS_PER_GROUP, N, LANES), lambda g, c: (rev(c), g, 0, 0)),
        b_block=SSM_INNER // N,
    )


def _ssd_fwd(xbc, proj, dt_hm, dtb, alog, dsk, nw):
    S = xbc.shape[0]
    N, L = SSM_STATE, CHUNK
    nc = S // L
    sp = _ssd_specs(lambda c: c)

    def body(xs_ref, b_ref, c_ref, z_ref, dt_ref, dtb_ref, al_ref, dsk_ref, nw_ref, y_ref, hs_ref, h_ref):
        @pl.when(pl.program_id(1) == 0)
        def _():
            h_ref[...] = jnp.zeros_like(h_ref)

        hs_ref[...] = h_ref[...]
        out, hnew = _ssd_chunk(*_ssd_args(xs_ref, b_ref, c_ref, z_ref, dt_ref, dtb_ref, al_ref, dsk_ref, nw_ref, h_ref))
        for pp in range(PAIRS_PER_GROUP):
            y_ref[:, pp * LANES:(pp + 1) * LANES] = out[pp].astype(y_ref.dtype)
            h_ref[pp] = hnew[pp]

    return pl.pallas_call(
        body, grid=(SSM_GROUPS, nc),
        in_specs=[sp["cols"](0), sp["bc"](sp["b_block"]), sp["bc"](sp["b_block"] + SSM_GROUPS), sp["cols"](COL_Z),
                  sp["dt"], sp["scal"], sp["scal"], sp["scal"], sp["nw"]],
        out_specs=[sp["cols"](0), sp["hs"]],
        out_shape=[jax.ShapeDtypeStruct((S, SSM_INNER), BF16),
                   jax.ShapeDtypeStruct((nc, SSM_HEADS // 2, N, LANES), F32)],
        scratch_shapes=[pltpu.VMEM((PAIRS_PER_GROUP, N, LANES), F32)],
        compiler_params=_params(("parallel", "arbitrary")), name="ssd_fwd",
    )(xbc, xbc, xbc, proj, dt_hm, dtb, alog, dsk, nw)


def _ssd_bwd(xbc, proj, dt_hm, dtb, alog, dsk, nw, hs, dmixed):
    S = xbc.shape[0]
    N, L = SSM_STATE, CHUNK
    nc = S // L
    sp = _ssd_specs(lambda c: nc - 1 - c)

    def body(xs_ref, b_ref, c_ref, z_ref, dt_ref, dtb_ref, al_ref, dsk_ref, nw_ref, hs_ref, dy_ref,
             dxs_ref, dz_ref, db_ref, dc_ref, ddt_ref, ddtb_ref, dal_ref, ddsk_ref, dnw_ref, dh_ref):
        @pl.when(pl.program_id(1) == 0)
        def _():
            dh_ref[...] = jnp.zeros_like(dh_ref)
            ddtb_ref[...] = jnp.zeros_like(ddtb_ref)
            dal_ref[...] = jnp.zeros_like(dal_ref)
            ddsk_ref[...] = jnp.zeros_like(ddsk_ref)
            dnw_ref[...] = jnp.zeros_like(dnw_ref)

        pairs = range(PAIRS_PER_GROUP)
        lanes = lambda pp: slice(pp * LANES, (pp + 1) * LANES)
        _, vjp = jax.vjp(_ssd_chunk, *_ssd_args(xs_ref, b_ref, c_ref, z_ref, dt_ref, dtb_ref, al_ref, dsk_ref, nw_ref,
                                                hs_ref))
        dxs, dB, dC, dz, ddt, ddtb, dal, ddsk, dnw, dh = vjp(([dy_ref[:, lanes(pp)] for pp in pairs],
                                                              [dh_ref[pp] for pp in pairs]))
        db_ref[...] = dB
        dc_ref[...] = dC
        for pp in pairs:
            dxs_ref[:, lanes(pp)] = dxs[pp]
            dz_ref[:, lanes(pp)] = dz[pp].astype(dz_ref.dtype)
            dnw_ref[:, lanes(pp)] += dnw[pp]
            dh_ref[pp] = dh[pp]
        for r in range(HEADS_PER_GROUP):
            ddt_ref[r] = ddt[r]
            ddtb_ref[r] += ddtb[r]
            dal_ref[r] += dal[r]
            ddsk_ref[r] += ddsk[r]

    bc_out = pl.BlockSpec((L, N), lambda g, c: (nc - 1 - c, g))
    return pl.pallas_call(
        body, grid=(SSM_GROUPS, nc),
        in_specs=[sp["cols"](0), sp["bc"](sp["b_block"]), sp["bc"](sp["b_block"] + SSM_GROUPS), sp["cols"](COL_Z),
                  sp["dt"], sp["scal"], sp["scal"], sp["scal"], sp["nw"], sp["hs"], sp["cols"](0)],
        out_specs=[sp["cols"](0), sp["cols"](0), bc_out, bc_out, sp["dt"], sp["scal"], sp["scal"], sp["scal"], sp["nw"]],
        out_shape=[jax.ShapeDtypeStruct((S, SSM_INNER), F32), jax.ShapeDtypeStruct((S, SSM_INNER), BF16),
                   jax.ShapeDtypeStruct((S, SSM_GROUPS * N), F32), jax.ShapeDtypeStruct((S, SSM_GROUPS * N), F32),
                   jax.ShapeDtypeStruct((SSM_HEADS, S, 1), F32),
                   jax.ShapeDtypeStruct((SSM_HEADS, 1, 1), F32), jax.ShapeDtypeStruct((SSM_HEADS, 1, 1), F32),
                   jax.ShapeDtypeStruct((SSM_HEADS, 1, 1), F32), jax.ShapeDtypeStruct((1, SSM_INNER), F32)],
        scratch_shapes=[pltpu.VMEM((PAIRS_PER_GROUP, N, LANES), F32)],
        compiler_params=_params(("parallel", "arbitrary")), name="ssd_bwd",
    )(xbc, xbc, xbc, proj, dt_hm, dtb, alog, dsk, nw, hs, dmixed)


ATTN_SCALE = HEAD_DIM ** -0.5
ATTN_PAIRS = ATTN_HEADS // 2


def _first_head(rows):
    return lax.broadcasted_iota(jnp.int32, (rows, LANES), 1) < HEAD_DIM


def _pair_norm(x, g2, scale):
    first = _first_head(x.shape[0])
    sq = x * x
    ms0 = jnp.sum(jnp.where(first, sq, 0.0), axis=-1, keepdims=True) * (1.0 / HEAD_DIM)
    ms1 = jnp.sum(jnp.where(first, 0.0, sq), axis=-1, keepdims=True) * (1.0 / HEAD_DIM)
    r = jnp.where(first, lax.rsqrt(ms0 + EPS), lax.rsqrt(ms1 + EPS))
    return x * r * g2 * scale


def _qk_prep_fwd(proj, gq2, gk2):
    S = proj.shape[0]
    tq = _pick(S, (512, 256))

    def body(q_ref, k_ref, v_ref, gq_ref, gk_ref, qo_ref, ko_ref, vo_ref):
        qo_ref[...] = _pair_norm(q_ref[...], gq_ref[...], ATTN_SCALE).astype(BF16)
        ko_ref[...] = _pair_norm(k_ref[...], gk_ref[...], 1.0).astype(BF16)
        vo_ref[...] = v_ref[...].astype(BF16)

    col = lambda c0: pl.BlockSpec((tq, LANES), lambda h, i: (i, c0 // LANES + h))
    blk = pl.BlockSpec((tq, LANES), lambda h, i: (i, h))
    vec = pl.BlockSpec((1, LANES), lambda h, i: (0, 0))
    return pl.pallas_call(
        body, grid=(ATTN_PAIRS, S // tq), in_specs=[col(COL_Q), col(COL_K), col(COL_V), vec, vec],
        out_specs=[blk, blk, blk], out_shape=[jax.ShapeDtypeStruct((S, ATTN_WIDTH), BF16)] * 3,
        compiler_params=_params(("parallel", "parallel")), name="qk_prep_fwd")(proj, proj, proj, gq2, gk2)


def _qk_prep_bwd(proj, gq2, gk2, dqs, dkn):
    S = proj.shape[0]
    tq = _pick(S, (512, 256))

    def body(q_ref, k_ref, gq_ref, gk_ref, dqs_ref, dkn_ref, dq_ref, dk_ref, dgq_ref, dgk_ref):
        @pl.when((pl.program_id(0) == 0) & (pl.program_id(1) == 0))
        def _():
            dgq_ref[...] = jnp.zeros_like(dgq_ref)
            dgk_ref[...] = jnp.zeros_like(dgk_ref)

        _, vq = jax.vjp(lambda q, g: _pair_norm(q, g, ATTN_SCALE), q_ref[...], gq_ref[...])
        dq, dgq = vq(dqs_ref[...])
        _, vk = jax.vjp(lambda k, g: _pair_norm(k, g, 1.0), k_ref[...], gk_ref[...])
        dk, dgk = vk(dkn_ref[...])
        dq_ref[...] = dq.astype(dq_ref.dtype)
        dk_ref[...] = dk.astype(dk_ref.dtype)
        dgq_ref[...] += dgq
        dgk_ref[...] += dgk

    col = lambda c0: pl.BlockSpec((tq, LANES), lambda h, i: (i, c0 // LANES + h))
    blk = pl.BlockSpec((tq, LANES), lambda h, i: (i, h))
    vec = pl.BlockSpec((1, LANES), lambda h, i: (0, 0))
    return pl.pallas_call(
        body, grid=(ATTN_PAIRS, S // tq), in_specs=[col(COL_Q), col(COL_K), vec, vec, blk, blk],
        out_specs=[blk, blk, vec, vec],
        out_shape=[jax.ShapeDtypeStruct((S, ATTN_WIDTH), BF16)] * 2 + [jax.ShapeDtypeStruct((1, LANES), F32)] * 2,
        compiler_params=_params(("arbitrary", "arbitrary")), name="qk_prep_bwd")(proj, proj, gq2, gk2, dqs, dkn)


def _logf_cumsum_fwd(f_raw, f_bias):
    S, Hh = f_raw.shape
    L = CHUNK

    def body(f_ref, b_ref, o_ref):
        ri = lax.broadcasted_iota(jnp.int32, (L, L), 0)
        ci = lax.broadcasted_iota(jnp.int32, (L, L), 1)
        tril = (ri >= ci).astype(F32)
        carry = jnp.zeros((1, Hh), F32)
        for c in range(S // L):
            lf = -_softplus(-(f_ref[c * L:(c + 1) * L, :] + b_ref[...]))
            cum = _dot32(tril, lf) + carry
            o_ref[c * L:(c + 1) * L, :] = cum
            carry = cum[L - 1:L, :]

    return pl.pallas_call(body, out_shape=jax.ShapeDtypeStruct((S, Hh), F32), name="logf_cumsum_fwd")(f_raw, f_bias)


def _logf_cumsum_bwd(f_raw, f_bias, dcum):
    S, Hh = f_raw.shape
    L = CHUNK

    def body(f_ref, b_ref, d_ref, df_ref, db_ref):
        ri = lax.broadcasted_iota(jnp.int32, (L, L), 0)
        ci = lax.broadcasted_iota(jnp.int32, (L, L), 1)
        triu = (ri <= ci).astype(F32)
        carry = jnp.zeros((1, Hh), F32)
        db = jnp.zeros((1, Hh), F32)
        for c in reversed(range(S // L)):
            suf = _dot32(triu, d_ref[c * L:(c + 1) * L, :]) + carry
            df = suf * jax.nn.sigmoid(-(f_ref[c * L:(c + 1) * L, :] + b_ref[...]))
            df_ref[c * L:(c + 1) * L, :] = df
            db = db + jnp.sum(df, axis=0, keepdims=True)
            carry = suf[0:1, :]
        db_ref[...] = db

    return pl.pallas_call(
        body, out_shape=[jax.ShapeDtypeStruct((S, Hh), F32), jax.ShapeDtypeStruct((1, Hh), F32)],
        name="logf_cumsum_bwd")(f_raw, f_bias, dcum)


_NT = (((1,), (1,)), ((), ()))
_TN = (((0,), (0,)), ((), ()))


def _mxu(a, b, dims=(((1,), (0,)), ((), ()))):
    return lax.dot_general(a, b, dims, preferred_element_type=F32)


def _flash_fwd(qs, kn, vb, cq, ck):
    S, W = qs.shape
    tq = tk = _pick(S, (512, 256))
    nmask = max(tq // tk, 1)

    def body(q_ref, k_ref, v_ref, cq_ref, ck_ref, o_ref, of_ref, lse_ref):
        i = pl.program_id(1)
        first = _first_head(tq)
        q2 = q_ref[...]
        zero = jnp.zeros_like(q2)
        qa = (jnp.where(first, q2, zero), jnp.where(first, zero, q2))
        cqa = (cq_ref[0], cq_ref[1])
        row0 = i * tq

        def step(j, carry, masked):
            ms, ls, acc, rem = carry
            off = pl.multiple_of(j * tk, tk)
            k = k_ref[pl.ds(off, tk), :]
            v = v_ref[pl.ds(off, tk), :]
            new_m, new_l, alphas, pvs, prs = [], [], [], [], []
            for a in range(2):
                s = _mxu(qa[a], k, _NT) + cqa[a] - ck_ref[a, :, pl.ds(off, tk)]
                if masked:
                    ri = lax.broadcasted_iota(jnp.int32, (tq, tk), 0) + row0
                    ci = lax.broadcasted_iota(jnp.int32, (tq, tk), 1) + off
                    s = jnp.where(ri >= ci, s, -1e30)
                m_new = jnp.maximum(ms[a], jnp.max(s, axis=-1, keepdims=True))
                alpha = jnp.exp(ms[a] - m_new)
                p = jnp.exp(s - m_new)
                new_l.append(alpha * ls[a] + jnp.sum(p, axis=-1, keepdims=True))
                new_m.append(m_new)
                alphas.append(alpha)
                p_hi = p.astype(BF16)
                pvs.append(_mxu(p_hi, v))
                prs.append(_mxu((p - p_hi.astype(F32)).astype(BF16), v))
            al = jnp.where(first, alphas[0], alphas[1])
            acc = al * acc + jnp.where(first, pvs[0], pvs[1])
            rem = al * rem + jnp.where(first, prs[0], prs[1])
            return tuple(new_m), tuple(new_l), acc, rem

        neg = jnp.full((tq, 1), -1e30, F32)
        z1 = jnp.zeros((tq, 1), F32)
        z2 = jnp.zeros((tq, LANES), F32)
        carry = ((neg, neg), (z1, z1), z2, z2)
        n_full = (i * tq) // tk
        carry = lax.fori_loop(0, n_full, lambda j, c: step(j, c, False), carry)
        for jj in range(nmask):
            carry = step(n_full + jj, carry, True)
        ms, ls, acc, rem = carry
        linv = jnp.where(first, 1.0 / ls[0], 1.0 / ls[1])
        o_ref[...] = (acc * linv).astype(o_ref.dtype)
        of_ref[...] = (acc + rem) * linv
        lse_ref[0] = ms[0] + jnp.log(ls[0])
        lse_ref[1] = ms[1] + jnp.log(ls[1])

    qblk = pl.BlockSpec((tq, LANES), lambda h, i: (i, h))
    full = pl.BlockSpec((S, LANES), lambda h, i: (0, h))
    colb = pl.BlockSpec((2, tq, 1), lambda h, i: (h, i, 0))
    return pl.pallas_call(
        body, grid=(W // LANES, S // tq),
        in_specs=[qblk, full, full, colb, pl.BlockSpec((2, 1, S), lambda h, i: (h, 0, 0))],
        out_specs=[qblk, qblk, colb],
        out_shape=[jax.ShapeDtypeStruct((S, W), BF16), jax.ShapeDtypeStruct((S, W), F32),
                   jax.ShapeDtypeStruct((2 * (W // LANES), S, 1), F32)],
        compiler_params=_params(("parallel", "parallel")), name="flash_fwd")(qs, kn, vb, cq, ck)


def _flash_bwd(qs, kn, vb, cq, ck, o_fine, do, do_col0, lse):
    S, W = qs.shape
    tq = tk = _pick(S, (512, 256))
    nq = S // tq
    nmask = max(tk // tq, 1)

    def body(q_ref, k_ref, v_ref, cq_ref, ck_ref, of_ref, do_ref, lse_ref, dq_ref, dk_ref, dv_ref, dck_ref):
        j = pl.program_id(1)

        @pl.when(j == 0)
        def _():
            dq_ref[...] = jnp.zeros_like(dq_ref)

        firstk = _first_head(tk)
        firstq = _first_head(tq)
        k2 = k_ref[...]
        v2 = v_ref[...]
        zk = jnp.zeros_like(k2)
        ka = (jnp.where(firstk, k2, zk), jnp.where(firstk, zk, k2))
        va = (jnp.where(firstk, v2, zk), jnp.where(firstk, zk, v2))
        cka = (ck_ref[0], ck_ref[1])
        col0 = j * tk

        def step(i, carry, masked):
            dk, dv, dck0, dck1 = carry
            dcks = [dck0, dck1]
            off = pl.multiple_of(i * tq, tq)
            rows = pl.ds(off, tq)
            q2 = q_ref[rows, :]
            dob = do_ref[rows, :].astype(BF16)
            prod = dob.astype(F32) * of_ref[rows, :]
            dkp, dvp, dqp = [], [], []
            for a in range(2):
                s = _mxu(q2, ka[a], _NT) + cq_ref[a, rows, :] - cka[a]
                if masked:
                    ri = lax.broadcasted_iota(jnp.int32, (tq, tk), 0) + off
                    ci = lax.broadcasted_iota(jnp.int32, (tq, tk), 1) + col0
                    s = jnp.where(ri >= ci, s, -1e30)
                p = jnp.exp(s - lse_ref[a, rows, :])
                dp = _mxu(dob, va[a], _NT)
                own = jnp.where(firstq, prod, 0.0) if a == 0 else jnp.where(firstq, 0.0, prod)
                ds = p * (dp - jnp.sum(own, axis=-1, keepdims=True))
                dsb = ds.astype(BF16)
                dvp.append(_mxu(p.astype(BF16), dob, _TN))
                dkp.append(_mxu(dsb, q2, _TN))
                dqp.append(_mxu(dsb, k2))
                dcks[a] = dcks[a] - jnp.sum(ds, axis=0, keepdims=True)
            dq_ref[rows, :] += jnp.where(firstq, dqp[0], dqp[1])
            dk = dk + jnp.where(firstk, dkp[0], dkp[1])
            dv = dv + jnp.where(firstk, dvp[0], dvp[1])
            return dk, dv, dcks[0], dcks[1]

        z2 = jnp.zeros((tk, LANES), F32)
        z1 = jnp.zeros((1, tk), F32)
        carry = (z2, z2, z1, z1)
        i0 = (j * tk) // tq
        for ii in range(nmask):
            carry = step(i0 + ii, carry, True)
        dk, dv, dck0, dck1 = lax.fori_loop(i0 + nmask, nq, lambda i, c: step(i, c, False), carry)
        dk_ref[...] = dk
        dv_ref[...] = dv.astype(dv_ref.dtype)
        dck_ref[0] = dck0
        dck_ref[1] = dck1

    kblk = pl.BlockSpec((tk, LANES), lambda h, j: (j, h))
    full = pl.BlockSpec((S, LANES), lambda h, j: (0, h))
    dofull = pl.BlockSpec((S, LANES), lambda h, j: (0, do_col0 // LANES + h))
    col = pl.BlockSpec((2, S, 1), lambda h, j: (h, 0, 0))
    rowt = pl.BlockSpec((2, 1, tk), lambda h, j: (h, 0, j))
    return pl.pallas_call(
        body, grid=(W // LANES, S // tk),
        in_specs=[full, kblk, kblk, col, rowt, full, dofull, col],
        out_specs=[full, kblk, kblk, rowt],
        out_shape=[jax.ShapeDtypeStruct((S, W), F32), jax.ShapeDtypeStruct((S, W), F32),
                   jax.ShapeDtypeStruct((S, W), BF16), jax.ShapeDtypeStruct((2 * (W // LANES), 1, S), F32)],
        compiler_params=_params(("parallel", "arbitrary")), name="flash_bwd")(qs, kn, vb, cq, ck, o_fine, do, lse)


XATTN_SCALE = XATTN_DIM ** -0.5


def _xq_norm(q, g):
    return _rms(q, g) * XATTN_SCALE


def _xattn_fwd(xq, kv, gq, gk):
    S = xq.shape[0]
    Mm = kv.shape[0]
    Dh = XATTN_DIM
    tq = _pick(S, (512, 256))

    def body(q_ref, k_ref, v_ref, gq_ref, gk_ref, o_ref):
        qn = _xq_norm(q_ref[...], gq_ref[...]).astype(BF16)
        kn = _rms(k_ref[...], gk_ref[...]).astype(BF16)
        s = _mxu(qn, kn, _NT)
        m = jnp.max(s, axis=-1, keepdims=True)
        p = jnp.exp(s - m)
        l = jnp.sum(p, axis=-1, keepdims=True)
        o_ref[...] = (_mxu(p.astype(BF16), v_ref[...].astype(BF16)) / l).astype(o_ref.dtype)

    vec = pl.BlockSpec((1, Dh), lambda h, i: (0, 0))
    return pl.pallas_call(
        body, grid=(XATTN_HEADS, S // tq),
        in_specs=[pl.BlockSpec((tq, Dh), lambda h, i: (i, h)), pl.BlockSpec((Mm, Dh), lambda h, i: (0, h)),
                  pl.BlockSpec((Mm, Dh), lambda h, i: (0, XATTN_HEADS + h)), vec, vec],
        out_specs=pl.BlockSpec((tq, Dh), lambda h, i: (i, h)),
        out_shape=jax.ShapeDtypeStruct((S, XATTN_HEADS * Dh), BF16),
        compiler_params=_params(("parallel", "parallel")), name="xattn_fwd")(xq, kv, kv, gq, gk)


def _xattn_bwd(xq, kv, gq, gk, do):
    S = xq.shape[0]
    Mm = kv.shape[0]
    Dh = XATTN_DIM
    tq = _pick(S, (512, 256))
    nq = S // tq

    def body(q_ref, k_ref, v_ref, gq_ref, gk_ref, do_ref, dq_ref, dk_ref, dv_ref, dgq_ref, dgk_ref, dkn_acc, dv_acc):
        h = pl.program_id(0)
        i = pl.program_id(1)

        @pl.when((h == 0) & (i == 0))
        def _():
            dgq_ref[...] = jnp.zeros_like(dgq_ref)
            dgk_ref[...] = jnp.zeros_like(dgk_ref)

        @pl.when(i == 0)
        def _():
            dkn_acc[...] = jnp.zeros_like(dkn_acc)
            dv_acc[...] = jnp.zeros_like(dv_acc)

        qn32, vq = jax.vjp(_xq_norm, q_ref[...], gq_ref[...])
        kn32, vk = jax.vjp(_rms, k_ref[...], gk_ref[...])
        qn = qn32.astype(BF16)
        kn = kn32.astype(BF16)
        vb = v_ref[...].astype(BF16)
        s = _mxu(qn, kn, _NT)
        m = jnp.max(s, axis=-1, keepdims=True)
        p = jnp.exp(s - m)
        p = p / jnp.sum(p, axis=-1, keepdims=True)
        dob = do_ref[...].astype(BF16)
        dp = _mxu(dob, vb, _NT)
        delta = jnp.sum(p * dp, axis=-1, keepdims=True)
        ds = (p * (dp - delta)).astype(BF16)
        dv_acc[...] += _mxu(p.astype(BF16), dob, _TN)
        dkn_acc[...] += _mxu(ds, qn, _TN)
        dq, dgq = vq(_mxu(ds, kn))
        dq_ref[...] = dq.astype(dq_ref.dtype)
        dgq_ref[...] += dgq

        @pl.when(i == nq - 1)
        def _():
            dk, dgk = vk(dkn_acc[...])
            dk_ref[...] = dk.astype(dk_ref.dtype)
            dv_ref[...] = dv_acc[...].astype(dv_ref.dtype)
            dgk_ref[...] += dgk

    vec = pl.BlockSpec((1, Dh), lambda h, i: (0, 0))
    qblk = pl.BlockSpec((tq, Dh), lambda h, i: (i, h))
    kblk = pl.BlockSpec((Mm, Dh), lambda h, i: (0, h))
    vblk = pl.BlockSpec((Mm, Dh), lambda h, i: (0, XATTN_HEADS + h))
    return pl.pallas_call(
        body, grid=(XATTN_HEADS, nq),
        in_specs=[qblk, kblk, vblk, vec, vec, qblk],
        out_specs=[qblk, kblk, kblk, vec, vec],
        out_shape=[jax.ShapeDtypeStruct((S, XATTN_HEADS * Dh), BF16),
                   jax.ShapeDtypeStruct((Mm, XATTN_HEADS * Dh), BF16),
                   jax.ShapeDtypeStruct((Mm, XATTN_HEADS * Dh), BF16),
                   jax.ShapeDtypeStruct((1, Dh), F32), jax.ShapeDtypeStruct((1, Dh), F32)],
        scratch_shapes=[pltpu.VMEM((Mm, Dh), F32), pltpu.VMEM((Mm, Dh), F32)],
        compiler_params=_params(("arbitrary", "arbitrary")), name="xattn_bwd")(xq, kv, kv, gq, gk, do)


def _loss_head(y, target):
    S, D = y.shape
    tr = _pick(S, (512, 256))

    def body(y_ref, t_ref, dy_ref, loss_ref):
        @pl.when(pl.program_id(0) == 0)
        def _():
            loss_ref[...] = jnp.zeros_like(loss_ref)

        err = y_ref[...] - t_ref[...]
        dy_ref[...] = err * (1.0 / D)
        loss_ref[...] += jnp.sum(err * err) * (0.5 / D)

    row = pl.BlockSpec((tr, D), lambda i: (i, 0))
    return pl.pallas_call(
        body, grid=(S // tr,), in_specs=[row, row],
        out_specs=[row, pl.BlockSpec((1, LANES), lambda i: (0, 0))],
        out_shape=[jax.ShapeDtypeStruct((S, D), F32), jax.ShapeDtypeStruct((1, LANES), F32)],
        compiler_params=_params(("arbitrary",)), name="loss_head")(y, target)


def _row_tile(R, C):
    for tr in (1024, 512, 256, 128, 64, 32, 16, 8):
        if R % tr == 0 and tr * C * 4 <= (1 << 20):
            return tr
    return R


def _nsum(arrs, out_dtypes, name):
    R, C = arrs[0].shape
    tr = _row_tile(R, C)
    n = len(arrs)

    def body(*refs):
        acc = refs[0][...].astype(F32)
        for r in refs[1:n]:
            acc = acc + r[...].astype(F32)
        for o in refs[n:]:
            o[...] = acc.astype(o.dtype)

    blk = pl.BlockSpec((tr, C), lambda i: (i, 0))
    outs = pl.pallas_call(
        body, grid=(R // tr,), in_specs=[blk] * n, out_specs=[blk] * len(out_dtypes),
        out_shape=[jax.ShapeDtypeStruct((R, C), d) for d in out_dtypes],
        compiler_params=_params(("parallel",)), name=name)(*arrs)
    return outs


def _adamw(w, g, m, v, name):
    R, C = w.shape
    tr = _row_tile(R, C)
    c1 = 1.0 - ADAM_B1 ** ADAM_STEP
    c2 = 1.0 - ADAM_B2 ** ADAM_STEP

    def body(w_ref, g_ref, m_ref, v_ref, d_ref, mo_ref, vo_ref):
        g_t = g_ref[...]
        m_new = ADAM_B1 * m_ref[...] + (1.0 - ADAM_B1) * g_t
        v_new = ADAM_B2 * v_ref[...] + (1.0 - ADAM_B2) * (g_t * g_t)
        d_ref[...] = -ADAM_LR * ((m_new / c1) / (jnp.sqrt(v_new / c2) + ADAM_EPS) + ADAM_WD * w_ref[...])
        mo_ref[...] = m_new
        vo_ref[...] = v_new

    blk = pl.BlockSpec((tr, C), lambda i: (i, 0))
    return pl.pallas_call(
        body, grid=(R // tr,), in_specs=[blk] * 4, out_specs=[blk] * 3,
        out_shape=[jax.ShapeDtypeStruct((R, C), F32)] * 3,
        compiler_params=_params(("parallel",)), name=name)(w, g, m, v)


D_MODEL = 1024
SSM_INNER = SSM_HEADS * HEAD_DIM
CONV_DIM = SSM_INNER + 2 * SSM_GROUPS * SSM_STATE
ATTN_WIDTH = ATTN_HEADS * HEAD_DIM
COL_Z = 0
COL_XBC = COL_Z + SSM_INNER
COL_Q = COL_XBC + CONV_DIM
COL_K = COL_Q + ATTN_WIDTH
COL_V = COL_K + ATTN_WIDTH
COL_DT = COL_V + ATTN_WIDTH
COL_F = COL_DT + SSM_HEADS
IN_COLS = COL_F + ATTN_HEADS
IN_COLS_PAD = -(-IN_COLS // LANES) * LANES
REF_COL_DT = COL_Q


def _to_kernel_cols(w):
    return jnp.concatenate([w[:, :REF_COL_DT], w[:, REF_COL_DT + SSM_HEADS:COL_F], w[:, REF_COL_DT:REF_COL_DT + SSM_HEADS],
                            w[:, COL_F:IN_COLS]], axis=1)


def _to_reference_cols(w):
    return jnp.concatenate([w[:, :COL_Q], w[:, COL_DT:COL_DT + SSM_HEADS], w[:, COL_Q:COL_DT], w[:, COL_F:IN_COLS]],
                           axis=1)


def _add_residual(acc, res):
    return (res + acc,)


def _relu2(acc):
    r = jnp.maximum(acc, 0.0)
    return acc, r * r


def _relu2_bwd(acc, a):
    return (acc * (2.0 * jnp.maximum(a, 0.0)),)


def _layer_fwd_bwd(x, mem, target, w_in, p, late_weights, send_late_grads):
    S = x.shape[0]
    hd3 = lambda a: a.reshape(SSM_HEADS, 1, 1)

    h1 = _rmsnorm_fwd(x, p["g_mix"], "norm_mix")
    proj = _mm(h1, w_in, "nn", "in_proj")
    xbc = _conv_fwd(proj, COL_XBC, CONV_DIM, p["conv_w"], p["conv_b"])
    dt_hm = proj[:, COL_DT:COL_DT + SSM_HEADS].T[:, :, None]
    ssd_par = (hd3(p["dt_bias"]), hd3(p["a_log"]), hd3(p["d_skip"]), p["ssm_norm_w"])
    y, hs = _ssd_fwd(xbc, proj, dt_hm, *ssd_par)
    f_raw = proj[:, COL_F:COL_F + ATTN_HEADS]
    gq2 = jnp.tile(p["g_q"], (1, 2))
    gk2 = jnp.tile(p["g_k"], (1, 2))
    qs, kn, vb = _qk_prep_fwd(proj, gq2, gk2)
    cum = _logf_cumsum_fwd(f_raw, p["f_bias"])
    cq = cum.T[:, :, None]
    ck = cum.T[:, None, :]
    o, o_fine, lse = _flash_fwd(qs, kn, vb, cq, ck)
    W = late_weights((o_fine, y))
    x1 = _mm(y, W["w_out"][:SSM_INNER], "nn", "out_proj_ssm", epilogue=_add_residual, extras=(x,))
    x1 = _mm(o, W["w_out"][SSM_INNER:], "nn", "out_proj_attn", epilogue=_add_residual, extras=(x1,))
    h2 = _rmsnorm_fwd(x1, p["g_xattn"], "norm_xattn")
    mem_n = _rmsnorm_fwd(mem, p["g_mem"], "norm_mem")
    xq = _mm(h2, W["xq_w"], "nn", "xq_proj")
    kv = _mm(mem_n, W["xkv_w"], "nn", "xkv_proj", b_chunks=N_CHIPS)
    xo = _xattn_fwd(xq, kv, p["xg_q"], p["xg_k"])
    x2 = _mm(xo, W["xo_w"], "nn", "xo_proj", epilogue=_add_residual, extras=(x1,))
    h3 = _rmsnorm_fwd(x2, p["g_mlp"], "norm_mlp")
    a, act = _mm(h3, W["w_up"], "nn", "mlp_up", out_dtypes=(F32, BF16), epilogue=_relu2, b_chunks=N_CHIPS)
    x3 = _mm(act, W["w_down"], "nn", "mlp_down", epilogue=_add_residual, extras=(x2,))
    dy, loss_row = _loss_head(x3, target)

    gW, gp = {}, {}
    da = _mm(dy, W["w_down"], "nt", "d_act", out_dtypes=(BF16,), epilogue=_relu2_bwd, extras=(a,))
    gW["w_down"] = _mm(act, dy, "tn", "g_w_down", out_dtypes=(BF16,))
    gW["w_up"] = _mm(h3, da, "tn", "g_w_up", out_dtypes=(BF16,), out_chunks=N_CHIPS)
    dh3 = _mm(da, W["w_up"], "nt", "d_h3", b_chunks=N_CHIPS)
    dx2, gp["g_mlp"] = _rmsnorm_bwd(x2, p["g_mlp"], dh3, dy, "norm_mlp_bwd")
    dxo = _mm(dx2, W["xo_w"], "nt", "d_xo", out_dtypes=(BF16,))
    gW["xo_w"] = _mm(xo, dx2, "tn", "g_xo_w", out_dtypes=(BF16,))
    dxq, dk_x, dv_x, gp["xg_q"], gp["xg_k"] = _xattn_bwd(xq, kv, p["xg_q"], p["xg_k"], dxo)
    dkv = jnp.concatenate([dk_x, dv_x], axis=-1)
    gW["xq_w"] = _mm(h2, dxq, "tn", "g_xq_w", out_dtypes=(BF16,))
    dh2 = _mm(dxq, W["xq_w"], "nt", "d_h2")
    gW["xkv_w"] = _mm(mem_n, dkv, "tn", "g_xkv_w", out_dtypes=(BF16,), out_chunks=N_CHIPS)
    dmem_n = _mm(dkv, W["xkv_w"], "nt", "d_mem_n", b_chunks=N_CHIPS)
    _, gp["g_mem"] = _rmsnorm_bwd(mem, p["g_mem"], dmem_n, None, "norm_mem_bwd")
    dx1, gp["g_xattn"] = _rmsnorm_bwd(x1, p["g_xattn"], dh2, dx2, "norm_xattn_bwd")
    dmixed = _mm(dx1, W["w_out"], "nt", "d_mixed")
    gW["w_out"] = jnp.concatenate([_mm(y, dx1, "tn", "g_w_out_ssm", out_dtypes=(BF16,)),
                                   _mm(o, dx1, "tn", "g_w_out_attn", out_dtypes=(BF16,))], axis=0)
    token = send_late_grads(gW)
    dqs, dkn, dv, dck = _flash_bwd(qs, kn, vb, cq + token[:1, :1], ck, o_fine, dmixed, SSM_INNER, lse)
    dq_raw, dk_raw, dgq2, dgk2 = _qk_prep_bwd(proj, gq2, gk2, dqs, dkn)
    gp["g_q"] = dgq2[:, :HEAD_DIM] + dgq2[:, HEAD_DIM:]
    gp["g_k"] = dgk2[:, :HEAD_DIM] + dgk2[:, HEAD_DIM:]
    df, gp["f_bias"] = _logf_cumsum_bwd(f_raw, p["f_bias"], dck[:, 0, :].T)
    dxs, dz, dB, dC, ddt, ddtb, dalog, ddsk, gp["ssm_norm_w"] = _ssd_bwd(xbc, proj, dt_hm, *ssd_par, hs, dmixed)
    gp["dt_bias"] = ddtb.reshape(1, SSM_HEADS)
    gp["a_log"] = dalog.reshape(1, SSM_HEADS)
    gp["d_skip"] = ddsk.reshape(1, SSM_HEADS)
    dxbc_raw, dconv_w, gp["conv_b"] = _conv_bwd(proj, COL_XBC, CONV_DIM, p["conv_w"], p["conv_b"], (dxs, dB, dC))
    gp["conv_w"] = dconv_w[:CONV_WIDTH]
    dproj = jnp.concatenate(
        [dz, dxbc_raw, dq_raw, dk_raw, dv, ddt[:, :, 0].T.astype(BF16), df.astype(BF16),
         jnp.zeros((S, IN_COLS_PAD - IN_COLS), BF16)], axis=-1)
    g_w_in = _mm(h1, dproj, "tn", "g_w_in", out_dtypes=(BF16,))
    dh1 = _mm(dproj, w_in, "nt", "d_h1")
    dx, gp["g_mix"] = _rmsnorm_bwd(x, p["g_mix"], dh1, dx1, "norm_mix_bwd")
    return loss_row, dx, g_w_in, gp


_ANY = pl.BlockSpec(memory_space=pl.ANY)


def _place():
    x, y, c = lax.axis_index("x"), lax.axis_index("y"), lax.axis_index("c")
    chips = [(1 - x, y), (x, 1 - y), (1 - x, 1 - y)]
    return x, y, c, chips


def _chip_index(px, py):
    return 2 * px + py


def _all_gather_chips(split, whole):
    ns, nw = len(split), len(whole)
    n = ns + nw

    def body(*refs):
        ins, outs = refs[:n], refs[n:2 * n]
        send_ici, recv_ici, send_d2d, recv_d2d = refs[2 * n:]
        x, y, c, chips = _place()
        me = _chip_index(x, y)
        sib = (x, y, 1 - c)

        def ici(k, j, src, dst):
            return pltpu.make_async_remote_copy(src_ref=src, dst_ref=dst, send_sem=send_ici.at[3 * k + j],
                                                recv_sem=recv_ici.at[3 * k + j], device_id=(*chips[j], c),
                                                device_id_type=MESH)

        def d2d(k, j, piece):
            return pltpu.make_async_remote_copy(src_ref=piece, dst_ref=piece, send_sem=send_d2d.at[3 * k + j],
                                                recv_sem=recv_d2d.at[3 * k + j], device_id=sib, device_id_type=MESH)

        sends = []
        for k in range(n):
            for j in range(3):
                if k < ns:
                    sends.append(ici(k, j, ins[k].at[c], outs[k].at[me, c]))
                else:
                    sends.append(ici(k, j, ins[k], outs[k].at[me]))
                sends[-1].start()
        passed = []
        for k in range(n):
            for j in range(3):
                src_chip = _chip_index(*chips[j])
                if k < ns:
                    ici(k, j, ins[k].at[c], outs[k].at[src_chip, c]).wait_recv()
                    passed.append(d2d(k, j, outs[k].at[src_chip, c]))
                    passed[-1].start()
                else:
                    ici(k, j, ins[k], outs[k].at[src_chip]).wait_recv()
        for k in range(ns):
            for j in range(3):
                d2d(k, j, outs[k].at[_chip_index(*chips[j]), 1 - c]).wait_recv()
        for cp in sends + passed:
            cp.wait_send()

    arrs = list(split) + list(whole)
    return pl.pallas_call(
        body, in_specs=[_ANY] * n, out_specs=[_ANY] * n,
        out_shape=[jax.ShapeDtypeStruct((N_CHIPS,) + a.shape, a.dtype) for a in arrs],
        scratch_shapes=[pltpu.SemaphoreType.DMA((3 * n,)), pltpu.SemaphoreType.DMA((3 * n,)),
                        pltpu.SemaphoreType.DMA((3 * ns,)), pltpu.SemaphoreType.DMA((3 * ns,))],
        name="all_gather_chips")(*arrs)


def _sibling_send_halves(grads, name):
    n = len(grads)

    def body(*refs):
        ins, outs = refs[:n], refs[n:2 * n]
        send_sem, recv_sem = refs[2 * n:]
        x, y, c, _ = _place()

        def cp(k, j, half):
            return pltpu.make_async_remote_copy(src_ref=ins[k].at[j, half], dst_ref=outs[k].at[j],
                                                send_sem=send_sem.at[N_CHIPS * k + j],
                                                recv_sem=recv_sem.at[N_CHIPS * k + j],
                                                device_id=(x, y, 1 - c), device_id_type=MESH)

        copies = [cp(k, j, 1 - c) for k in range(n) for j in range(N_CHIPS)]
        for q in copies:
            q.start()
        for q in copies:
            q.wait()

    return pl.pallas_call(
        body, in_specs=[_ANY] * n, out_specs=[_ANY] * n,
        out_shape=[jax.ShapeDtypeStruct((N_CHIPS,) + g.shape[2:], g.dtype) for g in grads],
        scratch_shapes=[pltpu.SemaphoreType.DMA((N_CHIPS * n,)), pltpu.SemaphoreType.DMA((N_CHIPS * n,))],
        name=name)(*grads)


def _chips_send_shards(parts):
    n = len(parts)

    def body(*refs):
        ins, outs = refs[:n], refs[n:2 * n]
        send_sem, recv_sem = refs[2 * n:]
        x, y, c, chips = _place()

        def cp(k, j):
            return pltpu.make_async_remote_copy(src_ref=ins[k].at[_chip_index(*chips[j])], dst_ref=outs[k].at[j],
                                                send_sem=send_sem.at[3 * k + j], recv_sem=recv_sem.at[3 * k + j],
                                                device_id=(*chips[j], c), device_id_type=MESH)

        copies = [cp(k, j) for k in range(n) for j in range(3)]
        for q in copies:
            q.start()
        for q in copies:
            q.wait()

    return pl.pallas_call(
        body, in_specs=[_ANY] * n, out_specs=[_ANY] * n,
        out_shape=[jax.ShapeDtypeStruct((3,) + g.shape[1:], g.dtype) for g in parts],
        scratch_shapes=[pltpu.SemaphoreType.DMA((3 * n,)), pltpu.SemaphoreType.DMA((3 * n,))],
        name="rs_chip_shards")(*parts)


def _sibling_exchange(halves):
    n = len(halves)

    def body(*refs):
        ins, outs = refs[:n], refs[n:2 * n]
        send_sem, recv_sem = refs[2 * n:]
        x, y, c, _ = _place()

        def cp(k, half):
            return pltpu.make_async_remote_copy(src_ref=ins[k], dst_ref=outs[k].at[half], send_sem=send_sem.at[k],
                                                recv_sem=recv_sem.at[k], device_id=(x, y, 1 - c), device_id_type=MESH)

        sends = [cp(k, c) for k in range(n)]
        for q in sends:
            q.start()
        for k in range(n):
            cp(k, 1 - c).wait_recv()
        for q in sends:
            q.wait_send()

    return pl.pallas_call(
        body, in_specs=[_ANY] * n, out_specs=[_ANY] * n,
        out_shape=[jax.ShapeDtypeStruct((2,) + h.shape, h.dtype) for h in halves],
        scratch_shapes=[pltpu.SemaphoreType.DMA((n,)), pltpu.SemaphoreType.DMA((n,))],
        name="rs_sibling_exchange")(*halves)


_HBM = pl.BlockSpec(memory_space=pltpu.HBM)
_SEM = pl.BlockSpec(memory_space=pltpu.SEMAPHORE)
_SPLIT_EFFECT = pltpu.SideEffectType.DATAFLOW_SIDE_EFFECTING


class _Split(NamedTuple):
    send_sems: jax.Array
    recv_sems: jax.Array
    sources: tuple
    lands: tuple
    token: jax.Array


def _split_copies(kind, srcs, lands, send_sems, recv_sems):
    x, y, c, chips = _place()
    me = _chip_index(x, y)
    copies = []
    for k in range(len(srcs)):
        for j in range(3):
            if kind == "gather":
                src, dst = srcs[k], lands[k].at[me]
            else:
                src, dst = srcs[k].at[_chip_index(*chips[j])], lands[k].at[j]
            copies.append(pltpu.make_async_remote_copy(
                src_ref=src, dst_ref=dst, send_sem=send_sems.at[3 * k + j], recv_sem=recv_sems.at[3 * k + j],
                device_id=(*chips[j], c), device_id_type=MESH))
    return copies


def _split_start(name, sources, kind, after):
    n = len(sources)
    if kind == "gather":
        lands = [lax.empty((N_CHIPS,) + s.shape, s.dtype) for s in sources]
    else:
        lands = [lax.empty((3,) + s.shape[1:], s.dtype) for s in sources]
    deps = [] if after is None else [after]

    def body(*refs):
        srcs, lnds = refs[:n], refs[n:2 * n]
        send_sems, recv_sems = refs[2 * n + len(deps)], refs[2 * n + len(deps) + 1]
        for cp in _split_copies(kind, srcs, lnds, send_sems, recv_sems):
            cp.start()
        refs[-1][...] = jnp.zeros_like(refs[-1])

    hbm = lambda a: pltpu.with_memory_space_constraint(a, pltpu.HBM)
    outs = pl.pallas_call(
        body, name=name,
        in_specs=[_HBM] * (2 * n) + [_ANY] * len(deps),
        out_specs=[_SEM, _SEM] + [_HBM] * (2 * n) + [pl.BlockSpec(memory_space=pltpu.VMEM)],
        out_shape=[pltpu.SemaphoreType.DMA((3 * n,)), pltpu.SemaphoreType.DMA((3 * n,))]
        + [pltpu.HBM(a.shape, a.dtype) for a in list(sources) + lands] + [jax.ShapeDtypeStruct((8, LANES), F32)],
        input_output_aliases={k: 2 + k for k in range(2 * n)},
        compiler_params=pltpu.CompilerParams(has_side_effects=_SPLIT_EFFECT),
    )(*[hbm(s) for s in sources], *[hbm(l) for l in lands], *deps)
    return _Split(outs[0], outs[1], tuple(outs[2:2 + n]), tuple(outs[2 + n:2 + 2 * n]), outs[-1])


def _split_wait(name, h, kind, after):
    n = len(h.sources)

    def body(*refs):
        srcs, lnds = refs[:n], refs[n:2 * n]
        for cp in _split_copies(kind, srcs, lnds, refs[2 * n], refs[2 * n + 1]):
            cp.wait_send()
            cp.wait_recv()

    outs = pl.pallas_call(
        body, name=name,
        in_specs=[_HBM] * (2 * n) + [_SEM, _SEM] + [_ANY] * len(after),
        out_specs=[_HBM] * (2 * n),
        out_shape=[pltpu.HBM(a.shape, a.dtype) for a in h.sources + h.lands],
        input_output_aliases={k: k for k in range(2 * n)},
        compiler_params=pltpu.CompilerParams(has_side_effects=_SPLIT_EFFECT),
    )(*h.sources, *h.lands, h.send_sems, h.recv_sems, *after)
    return outs[:n], outs[n:]


def _all_reduce_small(vec):
    R = vec.shape[0]

    def body(v_ref, o_ref, buf, send_sem, recv_sem):
        x, y, c = lax.axis_index("x"), lax.axis_index("y"), lax.axis_index("c")
        me = 4 * x + 2 * y + c
        buf[me] = v_ref[...]
        copies = []
        for r in range(1, N_DEV):
            fx, fy, fc = (r >> 2) & 1, (r >> 1) & 1, r & 1
            peer = (x ^ fx, y ^ fy, c ^ fc)
            copies.append(pltpu.make_async_remote_copy(src_ref=v_ref, dst_ref=buf.at[me], send_sem=send_sem.at[r - 1],
                                                       recv_sem=recv_sem.at[r - 1], device_id=peer, device_id_type=MESH))
        for q in copies:
            q.start()
        for r in range(1, N_DEV):
            fx, fy, fc = (r >> 2) & 1, (r >> 1) & 1, r & 1
            src = 4 * (x ^ fx) + 2 * (y ^ fy) + (c ^ fc)
            pltpu.make_async_remote_copy(src_ref=v_ref, dst_ref=buf.at[src], send_sem=send_sem.at[r - 1],
                                         recv_sem=recv_sem.at[r - 1], device_id=(x, y, c), device_id_type=MESH).wait_recv()
        acc = buf[0]
        for d in range(1, N_DEV):
            acc = acc + buf[d]
        o_ref[...] = acc
        for q in copies:
            q.wait_send()

    vm = pl.BlockSpec(memory_space=pltpu.VMEM)
    return pl.pallas_call(
        body, in_specs=[vm], out_specs=vm, out_shape=jax.ShapeDtypeStruct((R, LANES), F32),
        scratch_shapes=[pltpu.VMEM((N_DEV, R, LANES), F32), pltpu.SemaphoreType.DMA((N_DEV - 1,)),
                        pltpu.SemaphoreType.DMA((N_DEV - 1,))],
        name="all_reduce_small")(vec)


_INPUTS = ["x", "mem", "g_mix", "w_in", "conv_w", "conv_b", "dt_bias", "a_log", "d_skip", "ssm_norm_w", "g_q", "g_k",
           "f_bias", "w_out", "g_xattn", "g_mem", "xq_w", "xkv_w", "xg_q", "xg_k", "xo_w", "g_mlp", "w_up", "w_down"]
_WEIGHTS = _INPUTS[2:]
_BIG = ["w_in", "w_out", "xq_w", "xkv_w", "xo_w", "w_up", "w_down"]
_LATE = _BIG[1:]
_COL_SHARDED = ["w_in", "xkv_w", "w_up"]
_SMALL = [n for n in _WEIGHTS if n not in _BIG]


def _pack_rows(arrs):
    rows = []
    for a in arrs:
        flat = a.reshape(-1)
        pad = -flat.shape[0] % LANES
        rows.append(jnp.pad(flat, (0, pad)).reshape(-1, LANES))
    out = jnp.concatenate(rows, axis=0)
    return jnp.pad(out, ((0, -out.shape[0] % 8), (0, 0)))


def _unpack_rows(packed, shapes):
    out, r = [], 0
    for s in shapes:
        n = math.prod(s)
        nr = -(-n // LANES)
        out.append(packed[r:r + nr].reshape(-1)[:n].reshape(s))
        r += nr
    return out


def kernel(x, mem, g_mix, w_in, conv_w, conv_b, dt_bias, a_log, d_skip, ssm_norm_w, g_q, g_k, f_bias, w_out, g_xattn, g_mem, xq_w, xkv_w, xg_q, xg_k, xo_w, g_mlp, w_up, w_down, loss_target, m_g_mix, m_w_in, m_conv_w, m_conv_b, m_dt_bias, m_a_log, m_d_skip, m_ssm_norm_w, m_g_q, m_g_k, m_f_bias, m_w_out, m_g_xattn, m_g_mem, m_xq_w, m_xkv_w, m_xg_q, m_xg_k, m_xo_w, m_g_mlp, m_w_up, m_w_down, v_g_mix, v_w_in, v_conv_w, v_conv_b, v_dt_bias, v_a_log, v_d_skip, v_ssm_norm_w, v_g_q, v_g_k, v_f_bias, v_w_out, v_g_xattn, v_g_mem, v_xq_w, v_xkv_w, v_xg_q, v_xg_k, v_xo_w, v_g_mlp, v_w_up, v_w_down):
    args = (x, mem, g_mix, w_in, conv_w, conv_b, dt_bias, a_log, d_skip, ssm_norm_w, g_q, g_k, f_bias, w_out, g_xattn,
            g_mem, xq_w, xkv_w, xg_q, xg_k, xo_w, g_mlp, w_up, w_down)
    w = dict(zip(_INPUTS, args))
    mom1 = dict(zip(_WEIGHTS, (m_g_mix, m_w_in, m_conv_w, m_conv_b, m_dt_bias, m_a_log, m_d_skip, m_ssm_norm_w, m_g_q,
                               m_g_k, m_f_bias, m_w_out, m_g_xattn, m_g_mem, m_xq_w, m_xkv_w, m_xg_q, m_xg_k, m_xo_w,
                               m_g_mlp, m_w_up, m_w_down)))
    mom2 = dict(zip(_WEIGHTS, (v_g_mix, v_w_in, v_conv_w, v_conv_b, v_dt_bias, v_a_log, v_d_skip, v_ssm_norm_w, v_g_q,
                               v_g_k, v_f_bias, v_w_out, v_g_xattn, v_g_mem, v_xq_w, v_xkv_w, v_xg_q, v_xg_k, v_xo_w,
                               v_g_mlp, v_w_up, v_w_down)))
    chip = _chip_index(lax.axis_index("x"), lax.axis_index("y"))
    core = lax.axis_index("c")

    shard_bf = {n: w[n][0].astype(BF16) for n in _BIG}

    def layout_for_compute(n, g):
        if n == "w_in":
            g = _to_kernel_cols(g.transpose(1, 0, 2).reshape(g.shape[1], IN_COLS))
            return jnp.pad(g, ((0, 0), (0, IN_COLS_PAD - IN_COLS)))
        return g if n in _COL_SHARDED else g.reshape(N_CHIPS * g.shape[1], g.shape[2])

    def layout_for_reduction(n, g):
        if n == "w_in":
            g = _to_reference_cols(g).reshape(g.shape[0], N_CHIPS, IN_COLS // N_CHIPS).transpose(1, 0, 2)
        elif n not in _COL_SHARDED:
            g = g.reshape(N_CHIPS, g.shape[0] // N_CHIPS, g.shape[1])
        return g.reshape(N_CHIPS, 2, g.shape[1] // 2, g.shape[2])

    def pair_sums_of(names, grads, tag):
        grads4 = [layout_for_reduction(n, grads[n]) for n in names]
        from_sibling = _sibling_send_halves(grads4, "rs_sibling_halves_" + tag)
        sums = []
        for n, g, fs in zip(names, grads4, from_sibling):
            mine = lax.dynamic_index_in_dim(g, core, axis=1, keepdims=False)
            flat = lambda a: a.reshape(-1, a.shape[-1])
            (s,) = _nsum([flat(mine), flat(fs)], (BF16,), "rs_pair_sum_" + n)
            sums.append(s.reshape(mine.shape))
        return sums

    def chip_sums_of(names, pair_sums, from_chips):
        out = []
        for n, ps, fc in zip(names, pair_sums, from_chips):
            own = lax.dynamic_index_in_dim(ps, chip, axis=0, keepdims=False)
            (r,) = _nsum([own, fc[0], fc[1], fc[2]], (F32,), "rs_chip_sum_" + n)
            out.append(r)
        return out

    halves_in = shard_bf["w_in"].reshape(2, shard_bf["w_in"].shape[0] // 2, -1)
    g_in, g_conv = _all_gather_chips([halves_in], [w["conv_w"][0]])
    g_in = lax.dynamic_update_index_in_dim(g_in, halves_in, chip, axis=0)
    g_conv = lax.dynamic_update_index_in_dim(g_conv, w["conv_w"][0], chip, axis=0)
    w_in_full = layout_for_compute("w_in", g_in.reshape(N_CHIPS, -1, g_in.shape[-1]))
    p = {n: w[n] for n in _SMALL}
    p["conv_w"] = g_conv.transpose(1, 0, 2).reshape(CONV_WIDTH, CONV_DIM)
    gather = _split_start("gather_late", [shard_bf[n] for n in _LATE], "gather", after=g_in)
    p["g_mix"] = p["g_mix"] + gather.token[:1, :1]

    def late_weights(after):
        srcs, lands = _split_wait("gather_late_wait", gather, "gather", after)
        lands = [lax.dynamic_update_index_in_dim(l, s, chip, axis=0) for l, s in zip(lands, srcs)]
        return {n: layout_for_compute(n, l) for n, l in zip(_LATE, lands)}

    scatter = {}

    def send_late_grads(grads):
        sums = pair_sums_of(_LATE, grads, "late")
        scatter["h"] = _split_start("scatter_late", sums, "scatter", after=None)
        return scatter["h"].token

    loss_row, dx, g_w_in, gp = _layer_fwd_bwd(x[0], mem[0], loss_target[0], w_in_full, p, late_weights, send_late_grads)

    sums_late, from_chips_late = _split_wait("scatter_late_wait", scatter["h"], "scatter", (dx,))
    sums_in = pair_sums_of(["w_in"], {"w_in": g_w_in}, "w_in")
    from_chips_in = _chips_send_shards(sums_in)
    reduced = chip_sums_of(["w_in"] + _LATE, sums_in + list(sums_late), list(from_chips_in) + list(from_chips_late))
    grad_shards = [lax.dynamic_update_index_in_dim(g, r, core, axis=0)
                   for g, r in zip(_sibling_exchange(reduced), reduced)]

    small_shapes = [gp[n].shape for n in _SMALL] + [(1, LANES)]
    packed = _pack_rows([gp[n] for n in _SMALL] + [loss_row])
    summed = _unpack_rows(_all_reduce_small(packed), small_shapes)
    gsmall = dict(zip(_SMALL, summed[:-1]))
    loss = summed[-1][0, 0]
    shard_cols = CONV_DIM // N_CHIPS
    gsmall["conv_w"] = lax.dynamic_slice_in_dim(gsmall["conv_w"], chip * shard_cols, shard_cols, axis=1)

    grad, delta, new_m, new_v = {}, {}, {}, {}
    for k, n in enumerate(_BIG):
        shape = w[n].shape
        g2 = grad_shards[k].reshape(shape[1], shape[2])
        d, m1, v1 = _adamw(w[n][0], g2, mom1[n][0], mom2[n][0], "adamw_" + n)
        grad[n], delta[n], new_m[n], new_v[n] = (a.reshape(shape) for a in (g2, d, m1, v1))
    pk = lambda src: _pack_rows([src[n] for n in _SMALL])
    for n in _SMALL:
        gsmall[n] = gsmall[n].reshape(w[n].shape)
    d, m1, v1 = _adamw(pk(w), pk(gsmall), pk(mom1), pk(mom2), "adamw_small")
    shapes = [w[n].shape for n in _SMALL]
    for n, dn, mn, vn in zip(_SMALL, _unpack_rows(d, shapes), _unpack_rows(m1, shapes), _unpack_rows(v1, shapes)):
        grad[n], delta[n], new_m[n], new_v[n] = gsmall[n], dn, mn, vn

    return (loss, dx[None], *[grad[n] for n in _WEIGHTS], *[delta[n] for n in _WEIGHTS],
            *[new_m[n] for n in _WEIGHTS], *[new_v[n] for n in _WEIGHTS])
```

```python
import math
from typing import NamedTuple

import jax
import jax.numpy as jnp
from jax import lax
from jax.experimental import pallas as pl
from jax.experimental.pallas import tpu as pltpu

F32 = jnp.float32
BF16 = jnp.bfloat16
HI = lax.Precision.HIGHEST
MESH = pl.DeviceIdType.MESH

EPS = 1e-5
CHUNK = 128
SSM_HEADS = 16
SSM_GROUPS = 2
HEADS_PER_GROUP = SSM_HEADS // SSM_GROUPS
HEAD_DIM = 64
SSM_STATE = 128
ATTN_HEADS = 16
XATTN_HEADS = 4
XATTN_DIM = 256
CONV_WIDTH = 4
N_CHIPS = 4
N_DEV = 8
LANES = 128
VMEM_LIMIT = 56 * 1024 * 1024

ADAM_LR = 0.001
ADAM_B1 = 0.9
ADAM_B2 = 0.999
ADAM_EPS = 1e-08
ADAM_WD = 0.01
ADAM_STEP = 10


def _params(sem):
    return pltpu.CompilerParams(dimension_semantics=sem, vmem_limit_bytes=VMEM_LIMIT)


def _pick(n, cands):
    for c in cands:
        if n % c == 0:
            return c
    return n


def _mm(a, b, mode, name, out_dtypes=(F32,), epilogue=None, extras=(), b_chunks=1, out_chunks=1,
        tm=None, tn=None, tk=None):
    if mode == "nn":
        M, K = a.shape
        N = b.shape[-1] * b_chunks
    elif mode == "nt":
        M, K = a.shape
        N = b.shape[-2]
        assert b.shape[-1] * b_chunks == K
    else:
        K, M = a.shape
        N = b.shape[-1] * b_chunks
    tm = tm or _pick(M, (2048, 1024, 512, 256, 128))
    tn = tn or _pick(N // max(b_chunks if mode != "nt" else 1, out_chunks), (512, 640, 384, 256, 128))
    if tk is None:
        kmax = b.shape[-1] if mode == "nt" else K
        tk = kmax if kmax <= 2048 else _pick(kmax, (2048, 1152, 1024, 512))
    nk = K // tk
    assert M % tm == 0 and N % tn == 0 and K % tk == 0
    grid = (M // tm, N // tn, nk)

    if mode == "tn":
        a_spec = pl.BlockSpec((tk, tm), lambda i, j, k: (k, i))
    else:
        a_spec = pl.BlockSpec((tm, tk), lambda i, j, k: (i, k))

    def b_index(t_row, t_last, tile_last):
        if b_chunks == 1:
            return (t_row, t_last)
        q = (b.shape[-1]) // tile_last
        return (t_last // q, t_row, t_last % q)

    if mode == "nn" or mode == "tn":
        bshape = (tk, tn)
        bmap = lambda i, j, k: b_index(k, j, tn)
    else:
        bshape = (tn, tk)
        bmap = lambda i, j, k: b_index(j, k, tk)
    if b_chunks > 1:
        bshape = (None,) + bshape
    b_spec = pl.BlockSpec(bshape, bmap)

    if out_chunks == 1:
        o_spec = pl.BlockSpec((tm, tn), lambda i, j, k: (i, j))
        o_shape = (M, N)
    else:
        qo = (N // out_chunks) // tn
        o_spec = pl.BlockSpec((None, tm, tn), lambda i, j, k: (j // qo, i, j % qo))
        o_shape = (out_chunks, M, N // out_chunks)
    e_spec = pl.BlockSpec((tm, tn), lambda i, j, k: (i, j))

    dims = {"nn": (((1,), (0,)), ((), ())), "nt": (((1,), (1,)), ((), ())), "tn": (((0,), (0,)), ((), ()))}[mode]
    n_ex = len(extras)
    n_out = len(out_dtypes)

    def body(*refs):
        a_ref, b_ref = refs[0], refs[1]
        ex_refs = refs[2:2 + n_ex]
        o_refs = refs[2 + n_ex:2 + n_ex + n_out]

        def finish(acc):
            outs = epilogue(acc, *[r[...] for r in ex_refs]) if epilogue is not None else (acc,)
            for r, o in zip(o_refs, outs):
                r[...] = o.astype(r.dtype)

        part = lax.dot_general(a_ref[...].astype(BF16), b_ref[...].astype(BF16), dims,
                               preferred_element_type=F32)
        if nk == 1:
            finish(part)
        else:
            acc_ref = refs[-1]
            k = pl.program_id(2)

            @pl.when(k == 0)
            def _():
                acc_ref[...] = part

            @pl.when(k > 0)
            def _():
                acc_ref[...] += part

            @pl.when(k == nk - 1)
            def _():
                finish(acc_ref[...])

    outs = pl.pallas_call(
        body,
        grid=grid,
        in_specs=[a_spec, b_spec] + [e_spec] * n_ex,
        out_specs=[o_spec] * n_out,
        out_shape=[jax.ShapeDtypeStruct(o_shape, d) for d in out_dtypes],
        scratch_shapes=[pltpu.VMEM((tm, tn), F32)] if nk > 1 else [],
        compiler_params=_params(("parallel", "parallel", "arbitrary")),
        name=name,
    )(a, b, *extras)
    return outs[0] if n_out == 1 else outs


def _rms(x, g):
    r = lax.rsqrt(jnp.mean(x * x, axis=-1, keepdims=True) + EPS)
    return x * r * g


def _rmsnorm_fwd(x, g, name):
    R, D = x.shape
    tr = _pick(R, (512, 256))

    def body(x_ref, g_ref, o_ref):
        o_ref[...] = _rms(x_ref[...], g_ref[...]).astype(o_ref.dtype)

    return pl.pallas_call(
        body, grid=(R // tr,),
        in_specs=[pl.BlockSpec((tr, D), lambda i: (i, 0)), pl.BlockSpec((1, D), lambda i: (0, 0))],
        out_specs=pl.BlockSpec((tr, D), lambda i: (i, 0)),
        out_shape=jax.ShapeDtypeStruct((R, D), BF16),
        compiler_params=_params(("parallel",)), name=name)(x, g)


def _rmsnorm_bwd(x, g, dh, dres, name):
    R, D = x.shape
    tr = _pick(R, (256,))
    has_res = dres is not None

    def body(*refs):
        if has_res:
            x_ref, g_ref, dh_ref, dres_ref, dx_ref, dg_ref = refs
        else:
            x_ref, g_ref, dh_ref, dx_ref, dg_ref = refs
        _, vjp = jax.vjp(_rms, x_ref[...], g_ref[...])
        dx, dg = vjp(dh_ref[...])
        if has_res:
            dx = dx + dres_ref[...]
        dx_ref[...] = dx

        @pl.when(pl.program_id(0) == 0)
        def _():
            dg_ref[...] = jnp.zeros_like(dg_ref)

        dg_ref[...] += dg

    row = pl.BlockSpec((tr, D), lambda i: (i, 0))
    vec = pl.BlockSpec((1, D), lambda i: (0, 0))
    ins = [x, g, dh] + ([dres] if has_res else [])
    return pl.pallas_call(
        body, grid=(R // tr,),
        in_specs=[row, vec, row] + ([row] if has_res else []),
        out_specs=[row, vec],
        out_shape=[jax.ShapeDtypeStruct((R, D), F32), jax.ShapeDtypeStruct((1, D), F32)],
        compiler_params=_params(("arbitrary",)), name=name)(*ins)


def _shift_down(u, k):
    if k == 0:
        return u
    rows = lax.broadcasted_iota(jnp.int32, u.shape, 0)
    return jnp.where(rows >= k, pltpu.roll(u, k, axis=0), 0.0)


def _shift_up(u, k):
    if k == 0:
        return u
    n = u.shape[0]
    rows = lax.broadcasted_iota(jnp.int32, u.shape, 0)
    return jnp.where(rows < n - k, pltpu.roll(u, n - k, axis=0), 0.0)


def _conv_pre(u, w, b):
    pre = b
    for j in range(CONV_WIDTH):
        pre = pre + w[j:j + 1, :] * _shift_down(u, CONV_WIDTH - 1 - j)
    return pre


def _conv_fwd(proj, col0, ncols, conv_w, conv_b):
    S = proj.shape[0]
    cb0 = col0 // LANES

    def body(u_ref, w_ref, b_ref, o_ref):
        pre = _conv_pre(u_ref[...], w_ref[...], b_ref[...])
        o_ref[...] = pre * jax.nn.sigmoid(pre)

    return pl.pallas_call(
        body, grid=(ncols // LANES,),
        in_specs=[pl.BlockSpec((S, LANES), lambda j: (0, j + cb0)),
                  pl.BlockSpec((CONV_WIDTH, LANES), lambda j: (0, j)),
                  pl.BlockSpec((1, LANES), lambda j: (0, j))],
        out_specs=pl.BlockSpec((S, LANES), lambda j: (0, j)),
        out_shape=jax.ShapeDtypeStruct((S, ncols), F32),
        compiler_params=_params(("parallel",)), name="conv_fwd")(proj, conv_w, conv_b)


def _conv_bwd(proj, col0, ncols, conv_w, conv_b, douts):
    S = proj.shape[0]
    cb0 = col0 // LANES
    starts = [0]
    for d in douts:
        starts.append(starts[-1] + d.shape[1] // LANES)
    assert starts[-1] == ncols // LANES
    nd = len(douts)

    def body(u_ref, w_ref, b_ref, *rest):
        d_refs, (du_ref, dw_ref, db_ref) = rest[:nd], rest[nd:]
        j = pl.program_id(0)
        dout = d_refs[-1][...]
        for i in range(nd - 2, -1, -1):
            dout = jnp.where(j < starts[i + 1], d_refs[i][...], dout)
        u = u_ref[...]
        w = w_ref[...]
        pre = _conv_pre(u, w, b_ref[...])
        s = jax.nn.sigmoid(pre)
        dpre = dout * (s * (1.0 + pre * (1.0 - s)))
        du = jnp.zeros_like(u)
        rows = []
        for j in range(CONV_WIDTH):
            k = CONV_WIDTH - 1 - j
            du = du + w[j:j + 1, :] * _shift_up(dpre, k)
            rows.append(jnp.sum(dpre * _shift_down(u, k), axis=0, keepdims=True))
        du_ref[...] = du.astype(du_ref.dtype)
        rows.append(jnp.zeros((8 - CONV_WIDTH, LANES), F32))
        dw_ref[...] = jnp.concatenate(rows, axis=0)
        db_ref[...] = jnp.sum(dpre, axis=0, keepdims=True)

    return pl.pallas_call(
        body, grid=(ncols // LANES,),
        in_specs=[pl.BlockSpec((S, LANES), lambda j: (0, j + cb0)),
                  pl.BlockSpec((CONV_WIDTH, LANES), lambda j: (0, j)),
                  pl.BlockSpec((1, LANES), lambda j: (0, j))]
        + [pl.BlockSpec((S, LANES), lambda j, lo=starts[i], hi=starts[i + 1]: (0, jnp.clip(j - lo, 0, hi - lo - 1)))
           for i in range(nd)],
        out_specs=[pl.BlockSpec((S, LANES), lambda j: (0, j)),
                   pl.BlockSpec((8, LANES), lambda j: (0, j)),
                   pl.BlockSpec((1, LANES), lambda j: (0, j))],
        out_shape=[jax.ShapeDtypeStruct((S, ncols), BF16),
                   jax.ShapeDtypeStruct((8, ncols), F32),
                   jax.ShapeDtypeStruct((1, ncols), F32)],
        compiler_params=_params(("parallel",)), name="conv_bwd")(proj, conv_w, conv_b, *douts)


def _softplus(x):
    return jnp.maximum(x, 0.0) + jnp.log1p(jnp.exp(-jnp.abs(x)))


def _dot32(a, b, dims=(((1,), (0,)), ((), ()))):
    return lax.dot_general(a, b, dims, precision=HI, preferred_element_type=F32)


def _dotd(a, b, dims=(((1,), (0,)), ((), ()))):
    return lax.dot_general(a, b, dims, preferred_element_type=F32)


PAIRS_PER_GROUP = HEADS_PER_GROUP // 2


def _ssd_chunk(xs, Bm, Cm, z, dtr, dtb, alog, dsk, nw, h):
    L = Bm.shape[0]
    ri = lax.broadcasted_iota(jnp.int32, (L, L), 0)
    ci = lax.broadcasted_iota(jnp.int32, (L, L), 1)
    causal = ri >= ci
    tril = causal.astype(F32)
    first = _first_head(L)
    first1 = _first_head(1)
    CB = _dotd(Cm, Bm, _NT)
    gated, hnew = [], []
    ssq = jnp.zeros((L, 1), F32)
    for pp in range(len(xs)):
        dts, cums, tots, decay = [], [], [], []
        for a in range(2):
            r = 2 * pp + a
            dt = _softplus(dtr[r] + dtb[r])
            dA = dt * (-jnp.exp(alog[r]))
            acs = _dot32(tril, dA)
            cc = jnp.broadcast_to(acs, (L, L))
            decay.append(CB * jnp.exp(jnp.where(causal, cc - cc.T, -1e30)))
            dts.append(dt)
            cums.append(acs)
            tots.append(jnp.sum(dA, axis=0, keepdims=True))
        dt2 = jnp.where(first, dts[0], dts[1])
        acs2 = jnp.where(first, cums[0], cums[1])
        tot2 = jnp.where(first1, tots[0], tots[1])
        dsk2 = jnp.where(first1, dsk[2 * pp], dsk[2 * pp + 1])
        X = xs[pp] * dt2
        y = (jnp.where(first, _dotd(decay[0], X), _dotd(decay[1], X)) + jnp.exp(acs2) * _dotd(Cm, h[pp])
             + dsk2 * xs[pp])
        hnew.append(jnp.exp(tot2) * h[pp] + _dotd(Bm, X * jnp.exp(tot2 - acs2), _TN))
        g = y * (z[pp] * jax.nn.sigmoid(z[pp]))
        ssq = ssq + jnp.sum(g * g, axis=-1, keepdims=True)
        gated.append(g)
    rs = lax.rsqrt(ssq / (len(xs) * LANES) + EPS)
    return [g * rs * nw[pp] for pp, g in enumerate(gated)], hnew


def _ssd_args(xs_ref, b_ref, c_ref, z_ref, dt_ref, dtb_ref, al_ref, dsk_ref, nw_ref, h_ref):
    pairs = range(PAIRS_PER_GROUP)
    heads = range(HEADS_PER_GROUP)
    lanes = lambda ref, pp: ref[:, pp * LANES:(pp + 1) * LANES]
    dt_all = dt_ref[...]
    second_group = pl.program_id(0) == 1
    dts = [jnp.where(second_group, dt_all[:, HEADS_PER_GROUP + r:HEADS_PER_GROUP + r + 1], dt_all[:, r:r + 1])
           for r in heads]
    return ([lanes(xs_ref, pp) for pp in pairs], b_ref[...], c_ref[...], [lanes(z_ref, pp) for pp in pairs],
            dts, [dtb_ref[r] for r in heads], [al_ref[r] for r in heads],
            [dsk_ref[r] for r in heads], [lanes(nw_ref, pp) for pp in pairs], [h_ref[pp] for pp in pairs])


def _ssd_specs(rev):
    H, N, L = HEADS_PER_GROUP, SSM_STATE, CHUNK
    gw = H * HEAD_DIM
    return dict(
        cols=lambda col0: pl.BlockSpec((L, gw), lambda g, c: (rev(c), col0 // gw + g)),
        bc=lambda first_block: pl.BlockSpec((L, N), lambda g, c: (rev(c), first_block + g)),
        dt=pl.BlockSpec((L, LANES), lambda g, c: (rev(c), COL_DT // LANES)),
        scal=pl.BlockSpec((H, 1, 1), lambda g, c: (g, 0, 0)),
        nw=pl.BlockSpec((1, gw), lambda g, c: (0, g)),
        hs=pl.BlockSpec((None, PAIRS_PER_GROUP, N, LANES), lambda g, c: (rev(c), g, 0, 0)),
        b_block=SSM_INNER // N,
    )


def _ssd_fwd(xbc, proj, dtb, alog, dsk, nw):
    S = xbc.shape[0]
    N, L = SSM_STATE, CHUNK
    nc = S // L
    sp = _ssd_specs(lambda c: c)

    def body(xs_ref, b_ref, c_ref, z_ref, dt_ref, dtb_ref, al_ref, dsk_ref, nw_ref, y_ref, hs_ref, h_ref):
        @pl.when(pl.program_id(1) == 0)
        def _():
            h_ref[...] = jnp.zeros_like(h_ref)

        hs_ref[...] = h_ref[...]
        out, hnew = _ssd_chunk(*_ssd_args(xs_ref, b_ref, c_ref, z_ref, dt_ref, dtb_ref, al_ref, dsk_ref, nw_ref, h_ref))
        for pp in range(PAIRS_PER_GROUP):
            y_ref[:, pp * LANES:(pp + 1) * LANES] = out[pp].astype(y_ref.dtype)
            h_ref[pp] = hnew[pp]

    return pl.pallas_call(
        body, grid=(SSM_GROUPS, nc),
        in_specs=[sp["cols"](0), sp["bc"](sp["b_block"]), sp["bc"](sp["b_block"] + SSM_GROUPS), sp["cols"](COL_Z),
                  sp["dt"], sp["scal"], sp["scal"], sp["scal"], sp["nw"]],
        out_specs=[sp["cols"](0), sp["hs"]],
        out_shape=[jax.ShapeDtypeStruct((S, SSM_INNER), BF16),
                   jax.ShapeDtypeStruct((nc, SSM_HEADS // 2, N, LANES), F32)],
        scratch_shapes=[pltpu.VMEM((PAIRS_PER_GROUP, N, LANES), F32)],
        compiler_params=_params(("parallel", "arbitrary")), name="ssd_fwd",
    )(xbc, xbc, xbc, proj, proj, dtb, alog, dsk, nw)


def _ssd_bwd(xbc, proj, dtb, alog, dsk, nw, hs, dmixed):
    S = xbc.shape[0]
    N, L = SSM_STATE, CHUNK
    nc = S // L
    sp = _ssd_specs(lambda c: nc - 1 - c)

    def body(xs_ref, b_ref, c_ref, z_ref, dt_ref, dtb_ref, al_ref, dsk_ref, nw_ref, hs_ref, dy_ref,
             dxs_ref, dz_ref, db_ref, dc_ref, ddt_ref, ddtb_ref, dal_ref, ddsk_ref, dnw_ref, dh_ref):
        @pl.when(pl.program_id(1) == 0)
        def _():
            dh_ref[...] = jnp.zeros_like(dh_ref)
            ddtb_ref[...] = jnp.zeros_like(ddtb_ref)
            dal_ref[...] = jnp.zeros_like(dal_ref)
            ddsk_ref[...] = jnp.zeros_like(ddsk_ref)
            dnw_ref[...] = jnp.zeros_like(dnw_ref)

        pairs = range(PAIRS_PER_GROUP)
        lanes = lambda pp: slice(pp * LANES, (pp + 1) * LANES)
        _, vjp = jax.vjp(_ssd_chunk, *_ssd_args(xs_ref, b_ref, c_ref, z_ref, dt_ref, dtb_ref, al_ref, dsk_ref, nw_ref,
                                                hs_ref))
        dxs, dB, dC, dz, ddt, ddtb, dal, ddsk, dnw, dh = vjp(([dy_ref[:, lanes(pp)] for pp in pairs],
                                                              [dh_ref[pp] for pp in pairs]))
        db_ref[...] = dB
        dc_ref[...] = dC
        for pp in pairs:
            dxs_ref[:, lanes(pp)] = dxs[pp]
            dz_ref[:, lanes(pp)] = dz[pp].astype(dz_ref.dtype)
            dnw_ref[:, lanes(pp)] += dnw[pp]
            dh_ref[pp] = dh[pp]
        lane = lax.broadcasted_iota(jnp.int32, (L, LANES), 1)
        ddt_lanes = jnp.zeros((L, LANES), F32)
        for r in range(HEADS_PER_GROUP):
            ddt_lanes = jnp.where(lane == r, ddt[r], ddt_lanes)
            ddtb_ref[r] += ddtb[r]
            dal_ref[r] += dal[r]
            ddsk_ref[r] += ddsk[r]
        ddt_ref[...] = ddt_lanes

    bc_out = pl.BlockSpec((L, N), lambda g, c: (nc - 1 - c, g))
    return pl.pallas_call(
        body, grid=(SSM_GROUPS, nc),
        in_specs=[sp["cols"](0), sp["bc"](sp["b_block"]), sp["bc"](sp["b_block"] + SSM_GROUPS), sp["cols"](COL_Z),
                  sp["dt"], sp["scal"], sp["scal"], sp["scal"], sp["nw"], sp["hs"], sp["cols"](0)],
        out_specs=[sp["cols"](0), sp["cols"](0), bc_out, bc_out, bc_out, sp["scal"], sp["scal"], sp["scal"], sp["nw"]],
        out_shape=[jax.ShapeDtypeStruct((S, SSM_INNER), F32), jax.ShapeDtypeStruct((S, SSM_INNER), BF16),
                   jax.ShapeDtypeStruct((S, SSM_GROUPS * N), F32), jax.ShapeDtypeStruct((S, SSM_GROUPS * N), F32),
                   jax.ShapeDtypeStruct((S, SSM_GROUPS * LANES), F32),
                   jax.ShapeDtypeStruct((SSM_HEADS, 1, 1), F32), jax.ShapeDtypeStruct((SSM_HEADS, 1, 1), F32),
                   jax.ShapeDtypeStruct((SSM_HEADS, 1, 1), F32), jax.ShapeDtypeStruct((1, SSM_INNER), F32)],
        scratch_shapes=[pltpu.VMEM((PAIRS_PER_GROUP, N, LANES), F32)],
        compiler_params=_params(("parallel", "arbitrary")), name="ssd_bwd",
    )(xbc, xbc, xbc, proj, proj, dtb, alog, dsk, nw, hs, dmixed)


ATTN_SCALE = HEAD_DIM ** -0.5
ATTN_PAIRS = ATTN_HEADS // 2


def _first_head(rows):
    return lax.broadcasted_iota(jnp.int32, (rows, LANES), 1) < HEAD_DIM


def _pair_norm(x, g2, scale):
    first = _first_head(x.shape[0])
    sq = x * x
    ms0 = jnp.sum(jnp.where(first, sq, 0.0), axis=-1, keepdims=True) * (1.0 / HEAD_DIM)
    ms1 = jnp.sum(jnp.where(first, 0.0, sq), axis=-1, keepdims=True) * (1.0 / HEAD_DIM)
    r = jnp.where(first, lax.rsqrt(ms0 + EPS), lax.rsqrt(ms1 + EPS))
    return x * r * g2 * scale


def _qk_prep_fwd(proj, gq2, gk2):
    S = proj.shape[0]
    tq = _pick(S, (512, 256))

    def body(q_ref, k_ref, v_ref, gq_ref, gk_ref, qo_ref, ko_ref, vo_ref):
        qo_ref[...] = _pair_norm(q_ref[...], gq_ref[...], ATTN_SCALE).astype(BF16)
        ko_ref[...] = _pair_norm(k_ref[...], gk_ref[...], 1.0).astype(BF16)
        vo_ref[...] = v_ref[...].astype(BF16)

    col = lambda c0: pl.BlockSpec((tq, LANES), lambda h, i: (i, c0 // LANES + h))
    blk = pl.BlockSpec((tq, LANES), lambda h, i: (i, h))
    vec = pl.BlockSpec((1, LANES), lambda h, i: (0, 0))
    return pl.pallas_call(
        body, grid=(ATTN_PAIRS, S // tq), in_specs=[col(COL_Q), col(COL_K), col(COL_V), vec, vec],
        out_specs=[blk, blk, blk], out_shape=[jax.ShapeDtypeStruct((S, ATTN_WIDTH), BF16)] * 3,
        compiler_params=_params(("parallel", "parallel")), name="qk_prep_fwd")(proj, proj, proj, gq2, gk2)


def _qk_prep_bwd(proj, gq2, gk2, dqs, dkn):
    S = proj.shape[0]
    tq = _pick(S, (512, 256))

    def body(q_ref, k_ref, gq_ref, gk_ref, dqs_ref, dkn_ref, dq_ref, dk_ref, dgq_ref, dgk_ref):
        @pl.when((pl.program_id(0) == 0) & (pl.program_id(1) == 0))
        def _():
            dgq_ref[...] = jnp.zeros_like(dgq_ref)
            dgk_ref[...] = jnp.zeros_like(dgk_ref)

        _, vq = jax.vjp(lambda q, g: _pair_norm(q, g, ATTN_SCALE), q_ref[...], gq_ref[...])
        dq, dgq = vq(dqs_ref[...])
        _, vk = jax.vjp(lambda k, g: _pair_norm(k, g, 1.0), k_ref[...], gk_ref[...])
        dk, dgk = vk(dkn_ref[...])
        dq_ref[...] = dq.astype(dq_ref.dtype)
        dk_ref[...] = dk.astype(dk_ref.dtype)
        dgq_ref[...] += dgq
        dgk_ref[...] += dgk

    col = lambda c0: pl.BlockSpec((tq, LANES), lambda h, i: (i, c0 // LANES + h))
    blk = pl.BlockSpec((tq, LANES), lambda h, i: (i, h))
    vec = pl.BlockSpec((1, LANES), lambda h, i: (0, 0))
    return pl.pallas_call(
        body, grid=(ATTN_PAIRS, S // tq), in_specs=[col(COL_Q), col(COL_K), vec, vec, blk, blk],
        out_specs=[blk, blk, vec, vec],
        out_shape=[jax.ShapeDtypeStruct((S, ATTN_WIDTH), BF16)] * 2 + [jax.ShapeDtypeStruct((1, LANES), F32)] * 2,
        compiler_params=_params(("arbitrary", "arbitrary")), name="qk_prep_bwd")(proj, proj, gq2, gk2, dqs, dkn)


def _logf_cumsum_fwd(f_raw, f_bias):
    S, Hh = f_raw.shape
    L = CHUNK

    def body(f_ref, b_ref, o_ref, wide_ref):
        ri = lax.broadcasted_iota(jnp.int32, (L, L), 0)
        ci = lax.broadcasted_iota(jnp.int32, (L, L), 1)
        tril = (ri >= ci).astype(F32)
        carry = jnp.zeros((1, Hh), F32)
        for c in range(S // L):
            rows = slice(c * L, (c + 1) * L)
            lf = -_softplus(-(f_ref[rows, :] + b_ref[...]))
            cum = _dot32(tril, lf) + carry
            o_ref[rows, :] = cum
            for h in range(Hh):
                wide_ref[rows, h * HEAD_DIM:(h + 1) * HEAD_DIM] = jnp.broadcast_to(cum[:, h:h + 1], (L, HEAD_DIM))
            carry = cum[L - 1:L, :]

    return pl.pallas_call(
        body, out_shape=[jax.ShapeDtypeStruct((S, Hh), F32), jax.ShapeDtypeStruct((S, Hh * HEAD_DIM), F32)],
        name="logf_cumsum_fwd")(f_raw, f_bias)


def _logf_cumsum_bwd(f_raw, f_bias, dcum):
    S, Hh = f_raw.shape
    L = CHUNK

    def body(f_ref, b_ref, d_ref, df_ref, db_ref):
        ri = lax.broadcasted_iota(jnp.int32, (L, L), 0)
        ci = lax.broadcasted_iota(jnp.int32, (L, L), 1)
        triu = (ri <= ci).astype(F32)
        carry = jnp.zeros((1, Hh), F32)
        db = jnp.zeros((1, Hh), F32)
        for c in reversed(range(S // L)):
            suf = _dot32(triu, d_ref[c * L:(c + 1) * L, :]) + carry
            df = suf * jax.nn.sigmoid(-(f_ref[c * L:(c + 1) * L, :] + b_ref[...]))
            df_ref[c * L:(c + 1) * L, :] = df
            db = db + jnp.sum(df, axis=0, keepdims=True)
            carry = suf[0:1, :]
        db_ref[...] = db

    return pl.pallas_call(
        body, out_shape=[jax.ShapeDtypeStruct((S, Hh), F32), jax.ShapeDtypeStruct((1, Hh), F32)],
        name="logf_cumsum_bwd")(f_raw, f_bias, dcum)


_NT = (((1,), (1,)), ((), ()))
_TN = (((0,), (0,)), ((), ()))


def _mxu(a, b, dims=(((1,), (0,)), ((), ()))):
    return lax.dot_general(a, b, dims, preferred_element_type=F32)


def _flash_fwd(qs, kn, vb, cq, ck):
    S, W = qs.shape
    tq = tk = _pick(S, (512, 256))
    nmask = max(tq // tk, 1)

    def body(q_ref, k_ref, v_ref, cq_ref, ck_ref, o_ref, of_ref, lse_ref):
        i = pl.program_id(1)
        first = _first_head(tq)
        q2 = q_ref[...]
        zero = jnp.zeros_like(q2)
        qa = (jnp.where(first, q2, zero), jnp.where(first, zero, q2))
        cqa = (cq_ref[:, 0:1], cq_ref[:, HEAD_DIM:HEAD_DIM + 1])
        row0 = i * tq

        def step(j, carry, masked):
            ms, ls, acc, rem = carry
            off = pl.multiple_of(j * tk, tk)
            k = k_ref[pl.ds(off, tk), :]
            v = v_ref[pl.ds(off, tk), :]
            new_m, new_l, alphas, pvs, prs = [], [], [], [], []
            for a in range(2):
                s = _mxu(qa[a], k, _NT) + cqa[a] - ck_ref[a, :, pl.ds(off, tk)]
                if masked:
                    ri = lax.broadcasted_iota(jnp.int32, (tq, tk), 0) + row0
                    ci = lax.broadcasted_iota(jnp.int32, (tq, tk), 1) + off
                    s = jnp.where(ri >= ci, s, -1e30)
                m_new = jnp.maximum(ms[a], jnp.max(s, axis=-1, keepdims=True))
                alpha = jnp.exp(ms[a] - m_new)
                p = jnp.exp(s - m_new)
                new_l.append(alpha * ls[a] + jnp.sum(p, axis=-1, keepdims=True))
                new_m.append(m_new)
                alphas.append(alpha)
                p_hi = p.astype(BF16)
                pvs.append(_mxu(p_hi, v))
                prs.append(_mxu((p - p_hi.astype(F32)).astype(BF16), v))
            al = jnp.where(first, alphas[0], alphas[1])
            acc = al * acc + jnp.where(first, pvs[0], pvs[1])
            rem = al * rem + jnp.where(first, prs[0], prs[1])
            return tuple(new_m), tuple(new_l), acc, rem

        neg = jnp.full((tq, 1), -1e30, F32)
        z1 = jnp.zeros((tq, 1), F32)
        z2 = jnp.zeros((tq, LANES), F32)
        carry = ((neg, neg), (z1, z1), z2, z2)
        n_full = (i * tq) // tk
        carry = lax.fori_loop(0, n_full, lambda j, c: step(j, c, False), carry)
        for jj in range(nmask):
            carry = step(n_full + jj, carry, True)
        ms, ls, acc, rem = carry
        linv = jnp.where(first, 1.0 / ls[0], 1.0 / ls[1])
        o_ref[...] = (acc * linv).astype(o_ref.dtype)
        of_ref[...] = (acc + rem) * linv
        lse_ref[...] = jnp.where(first, ms[0] + jnp.log(ls[0]), ms[1] + jnp.log(ls[1]))

    qblk = pl.BlockSpec((tq, LANES), lambda h, i: (i, h))
    full = pl.BlockSpec((S, LANES), lambda h, i: (0, h))
    return pl.pallas_call(
        body, grid=(W // LANES, S // tq),
        in_specs=[qblk, full, full, qblk, pl.BlockSpec((2, 1, S), lambda h, i: (h, 0, 0))],
        out_specs=[qblk, qblk, qblk],
        out_shape=[jax.ShapeDtypeStruct((S, W), BF16), jax.ShapeDtypeStruct((S, W), F32),
                   jax.ShapeDtypeStruct((S, W), F32)],
        compiler_params=_params(("parallel", "parallel")), name="flash_fwd")(qs, kn, vb, cq, ck)


def _flash_bwd(qs, kn, vb, cq, ck, o_fine, do, do_col0, lse):
    S, W = qs.shape
    tq = tk = _pick(S, (512, 256))
    nq = S // tq
    nmask = max(tk // tq, 1)

    def body(q_ref, k_ref, v_ref, cq_ref, ck_ref, of_ref, do_ref, lse_ref, dq_ref, dk_ref, dv_ref, dck_ref):
        j = pl.program_id(1)

        @pl.when(j == 0)
        def _():
            dq_ref[...] = jnp.zeros_like(dq_ref)

        firstk = _first_head(tk)
        firstq = _first_head(tq)
        k2 = k_ref[...]
        v2 = v_ref[...]
        zk = jnp.zeros_like(k2)
        ka = (jnp.where(firstk, k2, zk), jnp.where(firstk, zk, k2))
        va = (jnp.where(firstk, v2, zk), jnp.where(firstk, zk, v2))
        cka = (ck_ref[0], ck_ref[1])
        col0 = j * tk

        def step(i, carry, masked):
            dk, dv, dck0, dck1 = carry
            dcks = [dck0, dck1]
            off = pl.multiple_of(i * tq, tq)
            rows = pl.ds(off, tq)
            q2 = q_ref[rows, :]
            dob = do_ref[rows, :].astype(BF16)
            prod = dob.astype(F32) * of_ref[rows, :]
            dkp, dvp, dqp = [], [], []
            for a in range(2):
                lane = pl.ds(a * HEAD_DIM, 1)
                s = _mxu(q2, ka[a], _NT) + cq_ref[rows, lane] - cka[a]
                if masked:
                    ri = lax.broadcasted_iota(jnp.int32, (tq, tk), 0) + off
                    ci = lax.broadcasted_iota(jnp.int32, (tq, tk), 1) + col0
                    s = jnp.where(ri >= ci, s, -1e30)
                p = jnp.exp(s - lse_ref[rows, lane])
                dp = _mxu(dob, va[a], _NT)
                own = jnp.where(firstq, prod, 0.0) if a == 0 else jnp.where(firstq, 0.0, prod)
                ds = p * (dp - jnp.sum(own, axis=-1, keepdims=True))
                dsb = ds.astype(BF16)
                dvp.append(_mxu(p.astype(BF16), dob, _TN))
                dkp.append(_mxu(dsb, q2, _TN))
                dqp.append(_mxu(dsb, k2))
                dcks[a] = dcks[a] - jnp.sum(ds, axis=0, keepdims=True)
            dq_ref[rows, :] += jnp.where(firstq, dqp[0], dqp[1])
            dk = dk + jnp.where(firstk, dkp[0], dkp[1])
            dv = dv + jnp.where(firstk, dvp[0], dvp[1])
            return dk, dv, dcks[0], dcks[1]

        z2 = jnp.zeros((tk, LANES), F32)
        z1 = jnp.zeros((1, tk), F32)
        carry = (z2, z2, z1, z1)
        i0 = (j * tk) // tq
        for ii in range(nmask):
            carry = step(i0 + ii, carry, True)
        dk, dv, dck0, dck1 = lax.fori_loop(i0 + nmask, nq, lambda i, c: step(i, c, False), carry)
        dk_ref[...] = dk
        dv_ref[...] = dv.astype(dv_ref.dtype)
        dck_ref[0] = dck0
        dck_ref[1] = dck1

    kblk = pl.BlockSpec((tk, LANES), lambda h, j: (j, h))
    full = pl.BlockSpec((S, LANES), lambda h, j: (0, h))
    dofull = pl.BlockSpec((S, LANES), lambda h, j: (0, do_col0 // LANES + h))
    rowt = pl.BlockSpec((2, 1, tk), lambda h, j: (h, 0, j))
    return pl.pallas_call(
        body, grid=(W // LANES, S // tk),
        in_specs=[full, kblk, kblk, full, rowt, full, dofull, full],
        out_specs=[full, kblk, kblk, rowt],
        out_shape=[jax.ShapeDtypeStruct((S, W), F32), jax.ShapeDtypeStruct((S, W), F32),
                   jax.ShapeDtypeStruct((S, W), BF16), jax.ShapeDtypeStruct((2 * (W // LANES), 1, S), F32)],
        compiler_params=_params(("parallel", "arbitrary")), name="flash_bwd")(qs, kn, vb, cq, ck, o_fine, do, lse)


XATTN_SCALE = XATTN_DIM ** -0.5


def _xq_norm(q, g):
    return _rms(q, g) * XATTN_SCALE


def _xattn_fwd(xq, kv, gq, gk):
    S = xq.shape[0]
    Mm = kv.shape[0]
    Dh = XATTN_DIM
    tq = _pick(S, (512, 256))

    def body(q_ref, k_ref, v_ref, gq_ref, gk_ref, o_ref):
        qn = _xq_norm(q_ref[...], gq_ref[...]).astype(BF16)
        kn = _rms(k_ref[...], gk_ref[...]).astype(BF16)
        s = _mxu(qn, kn, _NT)
        m = jnp.max(s, axis=-1, keepdims=True)
        p = jnp.exp(s - m)
        l = jnp.sum(p, axis=-1, keepdims=True)
        o_ref[...] = (_mxu(p.astype(BF16), v_ref[...].astype(BF16)) / l).astype(o_ref.dtype)

    vec = pl.BlockSpec((1, Dh), lambda h, i: (0, 0))
    return pl.pallas_call(
        body, grid=(XATTN_HEADS, S // tq),
        in_specs=[pl.BlockSpec((tq, Dh), lambda h, i: (i, h)), pl.BlockSpec((Mm, Dh), lambda h, i: (0, h)),
                  pl.BlockSpec((Mm, Dh), lambda h, i: (0, XATTN_HEADS + h)), vec, vec],
        out_specs=pl.BlockSpec((tq, Dh), lambda h, i: (i, h)),
        out_shape=jax.ShapeDtypeStruct((S, XATTN_HEADS * Dh), BF16),
        compiler_params=_params(("parallel", "parallel")), name="xattn_fwd")(xq, kv, kv, gq, gk)


def _xattn_bwd(xq, kv, gq, gk, do):
    S = xq.shape[0]
    Mm = kv.shape[0]
    Dh = XATTN_DIM
    tq = _pick(S, (512, 256))
    nq = S // tq

    def body(q_ref, k_ref, v_ref, gq_ref, gk_ref, do_ref, dq_ref, dk_ref, dv_ref, dgq_ref, dgk_ref, dkn_acc, dv_acc):
        h = pl.program_id(0)
        i = pl.program_id(1)

        @pl.when((h == 0) & (i == 0))
        def _():
            dgq_ref[...] = jnp.zeros_like(dgq_ref)
            dgk_ref[...] = jnp.zeros_like(dgk_ref)

        @pl.when(i == 0)
        def _():
            dkn_acc[...] = jnp.zeros_like(dkn_acc)
            dv_acc[...] = jnp.zeros_like(dv_acc)

        qn32, vq = jax.vjp(_xq_norm, q_ref[...], gq_ref[...])
        kn32, vk = jax.vjp(_rms, k_ref[...], gk_ref[...])
        qn = qn32.astype(BF16)
        kn = kn32.astype(BF16)
        vb = v_ref[...].astype(BF16)
        s = _mxu(qn, kn, _NT)
        m = jnp.max(s, axis=-1, keepdims=True)
        p = jnp.exp(s - m)
        p = p / jnp.sum(p, axis=-1, keepdims=True)
        dob = do_ref[...].astype(BF16)
        dp = _mxu(dob, vb, _NT)
        delta = jnp.sum(p * dp, axis=-1, keepdims=True)
        ds = (p * (dp - delta)).astype(BF16)
        dv_acc[...] += _mxu(p.astype(BF16), dob, _TN)
        dkn_acc[...] += _mxu(ds, qn, _TN)
        dq, dgq = vq(_mxu(ds, kn))
        dq_ref[...] = dq.astype(dq_ref.dtype)
        dgq_ref[...] += dgq

        @pl.when(i == nq - 1)
        def _():
            dk, dgk = vk(dkn_acc[...])
            dk_ref[...] = dk.astype(dk_ref.dtype)
            dv_ref[...] = dv_acc[...].astype(dv_ref.dtype)
            dgk_ref[...] += dgk

    vec = pl.BlockSpec((1, Dh), lambda h, i: (0, 0))
    qblk = pl.BlockSpec((tq, Dh), lambda h, i: (i, h))
    kblk = pl.BlockSpec((Mm, Dh), lambda h, i: (0, h))
    vblk = pl.BlockSpec((Mm, Dh), lambda h, i: (0, XATTN_HEADS + h))
    return pl.pallas_call(
        body, grid=(XATTN_HEADS, nq),
        in_specs=[qblk, kblk, vblk, vec, vec, qblk],
        out_specs=[qblk, kblk, kblk, vec, vec],
        out_shape=[jax.ShapeDtypeStruct((S, XATTN_HEADS * Dh), BF16),
                   jax.ShapeDtypeStruct((Mm, XATTN_HEADS * Dh), BF16),
                   jax.ShapeDtypeStruct((Mm, XATTN_HEADS * Dh), BF16),
                   jax.ShapeDtypeStruct((1, Dh), F32), jax.ShapeDtypeStruct((1, Dh), F32)],
        scratch_shapes=[pltpu.VMEM((Mm, Dh), F32), pltpu.VMEM((Mm, Dh), F32)],
        compiler_params=_params(("arbitrary", "arbitrary")), name="xattn_bwd")(xq, kv, kv, gq, gk, do)


def _loss_head(y, target):
    S, D = y.shape
    tr = _pick(S, (512, 256))

    def body(y_ref, t_ref, dy_ref, loss_ref):
        @pl.when(pl.program_id(0) == 0)
        def _():
            loss_ref[...] = jnp.zeros_like(loss_ref)

        err = y_ref[...] - t_ref[...]
        dy_ref[...] = err * (1.0 / D)
        loss_ref[...] += jnp.sum(err * err) * (0.5 / D)

    row = pl.BlockSpec((tr, D), lambda i: (i, 0))
    return pl.pallas_call(
        body, grid=(S // tr,), in_specs=[row, row],
        out_specs=[row, pl.BlockSpec((1, LANES), lambda i: (0, 0))],
        out_shape=[jax.ShapeDtypeStruct((S, D), F32), jax.ShapeDtypeStruct((1, LANES), F32)],
        compiler_params=_params(("arbitrary",)), name="loss_head")(y, target)


def _row_tile(R, C):
    for tr in (1024, 512, 256, 128, 64, 32, 16, 8):
        if R % tr == 0 and tr * C * 4 <= (1 << 20):
            return tr
    return R


def _nsum(arrs, out_dtypes, name):
    R, C = arrs[0].shape
    tr = _row_tile(R, C)
    n = len(arrs)

    def body(*refs):
        acc = refs[0][...].astype(F32)
        for r in refs[1:n]:
            acc = acc + r[...].astype(F32)
        for o in refs[n:]:
            o[...] = acc.astype(o.dtype)

    blk = pl.BlockSpec((tr, C), lambda i: (i, 0))
    outs = pl.pallas_call(
        body, grid=(R // tr,), in_specs=[blk] * n, out_specs=[blk] * len(out_dtypes),
        out_shape=[jax.ShapeDtypeStruct((R, C), d) for d in out_dtypes],
        compiler_params=_params(("parallel",)), name=name)(*arrs)
    return outs


def _adamw(w, g, m, v, name):
    R, C = w.shape
    tr = _row_tile(R, C)
    c1 = 1.0 - ADAM_B1 ** ADAM_STEP
    c2 = 1.0 - ADAM_B2 ** ADAM_STEP

    def body(w_ref, g_ref, m_ref, v_ref, d_ref, mo_ref, vo_ref):
        g_t = g_ref[...]
        m_new = ADAM_B1 * m_ref[...] + (1.0 - ADAM_B1) * g_t
        v_new = ADAM_B2 * v_ref[...] + (1.0 - ADAM_B2) * (g_t * g_t)
        d_ref[...] = -ADAM_LR * ((m_new / c1) / (jnp.sqrt(v_new / c2) + ADAM_EPS) + ADAM_WD * w_ref[...])
        mo_ref[...] = m_new
        vo_ref[...] = v_new

    blk = pl.BlockSpec((tr, C), lambda i: (i, 0))
    return pl.pallas_call(
        body, grid=(R // tr,), in_specs=[blk] * 4, out_specs=[blk] * 3,
        out_shape=[jax.ShapeDtypeStruct((R, C), F32)] * 3,
        compiler_params=_params(("parallel",)), name=name)(w, g, m, v)


D_MODEL = 1024
SSM_INNER = SSM_HEADS * HEAD_DIM
CONV_DIM = SSM_INNER + 2 * SSM_GROUPS * SSM_STATE
ATTN_WIDTH = ATTN_HEADS * HEAD_DIM
COL_Z = 0
COL_XBC = COL_Z + SSM_INNER
COL_Q = COL_XBC + CONV_DIM
COL_K = COL_Q + ATTN_WIDTH
COL_V = COL_K + ATTN_WIDTH
COL_DT = COL_V + ATTN_WIDTH
COL_F = COL_DT + SSM_HEADS
IN_COLS = COL_F + ATTN_HEADS
IN_COLS_PAD = -(-IN_COLS // LANES) * LANES
REF_COL_DT = COL_Q


def _to_kernel_cols(w):
    return jnp.concatenate([w[:, :REF_COL_DT], w[:, REF_COL_DT + SSM_HEADS:COL_F], w[:, REF_COL_DT:REF_COL_DT + SSM_HEADS],
                            w[:, COL_F:IN_COLS]], axis=1)


def _to_reference_cols(w):
    return jnp.concatenate([w[:, :COL_Q], w[:, COL_DT:COL_DT + SSM_HEADS], w[:, COL_Q:COL_DT], w[:, COL_F:IN_COLS]],
                           axis=1)


def _add_residual(acc, res):
    return (res + acc,)


def _relu2(acc):
    r = jnp.maximum(acc, 0.0)
    return acc, r * r


def _relu2_bwd(acc, a):
    return (acc * (2.0 * jnp.maximum(a, 0.0)),)


def _layer_fwd_bwd(x, mem, target, w_in, p, late_weights, send_late_grads):
    S = x.shape[0]
    hd3 = lambda a: a.reshape(SSM_HEADS, 1, 1)

    h1 = _rmsnorm_fwd(x, p["g_mix"], "norm_mix")
    proj = _mm(h1, w_in, "nn", "in_proj")
    xbc = _conv_fwd(proj, COL_XBC, CONV_DIM, p["conv_w"], p["conv_b"])
    ssd_par = (hd3(p["dt_bias"]), hd3(p["a_log"]), hd3(p["d_skip"]), p["ssm_norm_w"])
    y, hs = _ssd_fwd(xbc, proj, *ssd_par)
    f_raw = proj[:, COL_F:COL_F + ATTN_HEADS]
    gq2 = jnp.tile(p["g_q"], (1, 2))
    gk2 = jnp.tile(p["g_k"], (1, 2))
    qs, kn, vb = _qk_prep_fwd(proj, gq2, gk2)
    cum, cq = _logf_cumsum_fwd(f_raw, p["f_bias"])
    ck = cum.T[:, None, :]
    o, o_fine, lse = _flash_fwd(qs, kn, vb, cq, ck)
    W = late_weights((o_fine, y))
    x1 = _mm(y, W["w_out"][:SSM_INNER], "nn", "out_proj_ssm", epilogue=_add_residual, extras=(x,))
    x1 = _mm(o, W["w_out"][SSM_INNER:], "nn", "out_proj_attn", epilogue=_add_residual, extras=(x1,))
    h2 = _rmsnorm_fwd(x1, p["g_xattn"], "norm_xattn")
    mem_n = _rmsnorm_fwd(mem, p["g_mem"], "norm_mem")
    xq = _mm(h2, W["xq_w"], "nn", "xq_proj")
    kv = _mm(mem_n, W["xkv_w"], "nn", "xkv_proj", b_chunks=N_CHIPS)
    xo = _xattn_fwd(xq, kv, p["xg_q"], p["xg_k"])
    x2 = _mm(xo, W["xo_w"], "nn", "xo_proj", epilogue=_add_residual, extras=(x1,))
    h3 = _rmsnorm_fwd(x2, p["g_mlp"], "norm_mlp")
    a, act = _mm(h3, W["w_up"], "nn", "mlp_up", out_dtypes=(F32, BF16), epilogue=_relu2, b_chunks=N_CHIPS)
    x3 = _mm(act, W["w_down"], "nn", "mlp_down", epilogue=_add_residual, extras=(x2,))
    dy, loss_row = _loss_head(x3, target)

    gW, gp = {}, {}
    da = _mm(dy, W["w_down"], "nt", "d_act", out_dtypes=(BF16,), epilogue=_relu2_bwd, extras=(a,))
    gW["w_down"] = _mm(act, dy, "tn", "g_w_down", out_dtypes=(BF16,))
    gW["w_up"] = _mm(h3, da, "tn", "g_w_up", out_dtypes=(BF16,), out_chunks=N_CHIPS)
    dh3 = _mm(da, W["w_up"], "nt", "d_h3", b_chunks=N_CHIPS)
    dx2, gp["g_mlp"] = _rmsnorm_bwd(x2, p["g_mlp"], dh3, dy, "norm_mlp_bwd")
    dxo = _mm(dx2, W["xo_w"], "nt", "d_xo", out_dtypes=(BF16,))
    gW["xo_w"] = _mm(xo, dx2, "tn", "g_xo_w", out_dtypes=(BF16,))
    dxq, dk_x, dv_x, gp["xg_q"], gp["xg_k"] = _xattn_bwd(xq, kv, p["xg_q"], p["xg_k"], dxo)
    dkv = jnp.concatenate([dk_x, dv_x], axis=-1)
    gW["xq_w"] = _mm(h2, dxq, "tn", "g_xq_w", out_dtypes=(BF16,))
    dh2 = _mm(dxq, W["xq_w"], "nt", "d_h2")
    gW["xkv_w"] = _mm(mem_n, dkv, "tn", "g_xkv_w", out_dtypes=(BF16,), out_chunks=N_CHIPS)
    dmem_n = _mm(dkv, W["xkv_w"], "nt", "d_mem_n", b_chunks=N_CHIPS)
    _, gp["g_mem"] = _rmsnorm_bwd(mem, p["g_mem"], dmem_n, None, "norm_mem_bwd")
    dx1, gp["g_xattn"] = _rmsnorm_bwd(x1, p["g_xattn"], dh2, dx2, "norm_xattn_bwd")
    dmixed = _mm(dx1, W["w_out"], "nt", "d_mixed")
    gW["w_out"] = jnp.concatenate([_mm(y, dx1, "tn", "g_w_out_ssm", out_dtypes=(BF16,)),
                                   _mm(o, dx1, "tn", "g_w_out_attn", out_dtypes=(BF16,))], axis=0)
    token = send_late_grads(gW)
    dqs, dkn, dv, dck = _flash_bwd(qs, kn, vb, cq, ck + token[:1, :1], o_fine, dmixed, SSM_INNER, lse)
    dq_raw, dk_raw, dgq2, dgk2 = _qk_prep_bwd(proj, gq2, gk2, dqs, dkn)
    gp["g_q"] = dgq2[:, :HEAD_DIM] + dgq2[:, HEAD_DIM:]
    gp["g_k"] = dgk2[:, :HEAD_DIM] + dgk2[:, HEAD_DIM:]
    df, gp["f_bias"] = _logf_cumsum_bwd(f_raw, p["f_bias"], dck[:, 0, :].T)
    dxs, dz, dB, dC, ddt, ddtb, dalog, ddsk, gp["ssm_norm_w"] = _ssd_bwd(xbc, proj, *ssd_par, hs, dmixed)
    gp["dt_bias"] = ddtb.reshape(1, SSM_HEADS)
    gp["a_log"] = dalog.reshape(1, SSM_HEADS)
    gp["d_skip"] = ddsk.reshape(1, SSM_HEADS)
    dxbc_raw, dconv_w, gp["conv_b"] = _conv_bwd(proj, COL_XBC, CONV_DIM, p["conv_w"], p["conv_b"], (dxs, dB, dC))
    gp["conv_w"] = dconv_w[:CONV_WIDTH]
    dproj = jnp.concatenate(
        [dz, dxbc_raw, dq_raw, dk_raw, dv, ddt[:, :HEADS_PER_GROUP].astype(BF16),
         ddt[:, LANES:LANES + HEADS_PER_GROUP].astype(BF16), df.astype(BF16),
         jnp.zeros((S, IN_COLS_PAD - IN_COLS), BF16)], axis=-1)
    g_w_in = _mm(h1, dproj, "tn", "g_w_in", out_dtypes=(BF16,))
    dh1 = _mm(dproj, w_in, "nt", "d_h1")
    dx, gp["g_mix"] = _rmsnorm_bwd(x, p["g_mix"], dh1, dx1, "norm_mix_bwd")
    return loss_row, dx, g_w_in, gp


_ANY = pl.BlockSpec(memory_space=pl.ANY)


def _place():
    x, y, c = lax.axis_index("x"), lax.axis_index("y"), lax.axis_index("c")
    chips = [(1 - x, y), (x, 1 - y), (1 - x, 1 - y)]
    return x, y, c, chips


def _chip_index(px, py):
    return 2 * px + py


def _all_gather_chips(split, whole):
    ns, nw = len(split), len(whole)
    n = ns + nw

    def body(*refs):
        ins, outs = refs[:n], refs[n:2 * n]
        send_ici, recv_ici, send_d2d, recv_d2d = refs[2 * n:]
        x, y, c, chips = _place()
        me = _chip_index(x, y)
        sib = (x, y, 1 - c)

        def ici(k, j, src, dst):
            return pltpu.make_async_remote_copy(src_ref=src, dst_ref=dst, send_sem=send_ici.at[3 * k + j],
                                                recv_sem=recv_ici.at[3 * k + j], device_id=(*chips[j], c),
                                                device_id_type=MESH)

        def d2d(k, j, piece):
            return pltpu.make_async_remote_copy(src_ref=piece, dst_ref=piece, send_sem=send_d2d.at[3 * k + j],
                                                recv_sem=recv_d2d.at[3 * k + j], device_id=sib, device_id_type=MESH)

        sends = []
        for k in range(n):
            for j in range(3):
                if k < ns:
                    sends.append(ici(k, j, ins[k].at[c], outs[k].at[me, c]))
                else:
                    sends.append(ici(k, j, ins[k], outs[k].at[me]))
                sends[-1].start()
        passed = []
        for k in range(n):
            for j in range(3):
                src_chip = _chip_index(*chips[j])
                if k < ns:
                    ici(k, j, ins[k].at[c], outs[k].at[src_chip, c]).wait_recv()
                    passed.append(d2d(k, j, outs[k].at[src_chip, c]))
                    passed[-1].start()
                else:
                    ici(k, j, ins[k], outs[k].at[src_chip]).wait_recv()
        for k in range(ns):
            for j in range(3):
                d2d(k, j, outs[k].at[_chip_index(*chips[j]), 1 - c]).wait_recv()
        for cp in sends + passed:
            cp.wait_send()

    arrs = list(split) + list(whole)
    return pl.pallas_call(
        body, in_specs=[_ANY] * n, out_specs=[_ANY] * n,
        out_shape=[jax.ShapeDtypeStruct((N_CHIPS,) + a.shape, a.dtype) for a in arrs],
        scratch_shapes=[pltpu.SemaphoreType.DMA((3 * n,)), pltpu.SemaphoreType.DMA((3 * n,)),
                        pltpu.SemaphoreType.DMA((3 * ns,)), pltpu.SemaphoreType.DMA((3 * ns,))],
        name="all_gather_chips")(*arrs)


def _sibling_send_halves(grads, name):
    n = len(grads)

    def body(*refs):
        ins, outs = refs[:n], refs[n:2 * n]
        send_sem, recv_sem = refs[2 * n:]
        x, y, c, _ = _place()

        def cp(k, j, half):
            return pltpu.make_async_remote_copy(src_ref=ins[k].at[j, half], dst_ref=outs[k].at[j],
                                                send_sem=send_sem.at[N_CHIPS * k + j],
                                                recv_sem=recv_sem.at[N_CHIPS * k + j],
                                                device_id=(x, y, 1 - c), device_id_type=MESH)

        copies = [cp(k, j, 1 - c) for k in range(n) for j in range(N_CHIPS)]
        for q in copies:
            q.start()
        for q in copies:
            q.wait()

    return pl.pallas_call(
        body, in_specs=[_ANY] * n, out_specs=[_ANY] * n,
        out_shape=[jax.ShapeDtypeStruct((N_CHIPS,) + g.shape[2:], g.dtype) for g in grads],
        scratch_shapes=[pltpu.SemaphoreType.DMA((N_CHIPS * n,)), pltpu.SemaphoreType.DMA((N_CHIPS * n,))],
        name=name)(*grads)


def _sibling_exchange(halves, name):
    n = len(halves)

    def body(*refs):
        ins, outs = refs[:n], refs[n:2 * n]
        send_sem, recv_sem = refs[2 * n:]
        x, y, c, _ = _place()

        def cp(k, half):
            return pltpu.make_async_remote_copy(src_ref=ins[k], dst_ref=outs[k].at[half], send_sem=send_sem.at[k],
                                                recv_sem=recv_sem.at[k], device_id=(x, y, 1 - c), device_id_type=MESH)

        sends = [cp(k, c) for k in range(n)]
        for q in sends:
            q.start()
        for k in range(n):
            cp(k, 1 - c).wait_recv()
        for q in sends:
            q.wait_send()

    return pl.pallas_call(
        body, in_specs=[_ANY] * n, out_specs=[_ANY] * n,
        out_shape=[jax.ShapeDtypeStruct((2,) + h.shape, h.dtype) for h in halves],
        scratch_shapes=[pltpu.SemaphoreType.DMA((n,)), pltpu.SemaphoreType.DMA((n,))],
        name=name)(*halves)


_HBM = pl.BlockSpec(memory_space=pltpu.HBM)
_SEM = pl.BlockSpec(memory_space=pltpu.SEMAPHORE)
_SPLIT_EFFECT = pltpu.SideEffectType.DATAFLOW_SIDE_EFFECTING


class _Split(NamedTuple):
    send_sems: jax.Array
    recv_sems: jax.Array
    sources: tuple
    lands: tuple
    token: jax.Array


def _split_copies(kind, srcs, lands, send_sems, recv_sems):
    x, y, c, chips = _place()
    me = _chip_index(x, y)
    copies = []
    for k in range(len(srcs)):
        for j in range(3):
            if kind == "gather":
                src, dst = srcs[k], lands[k].at[me]
            else:
                src, dst = srcs[k].at[_chip_index(*chips[j])], lands[k].at[j]
            copies.append(pltpu.make_async_remote_copy(
                src_ref=src, dst_ref=dst, send_sem=send_sems.at[3 * k + j], recv_sem=recv_sems.at[3 * k + j],
                device_id=(*chips[j], c), device_id_type=MESH))
    return copies


def _split_start(name, sources, kind, after):
    n = len(sources)
    if kind == "gather":
        lands = [lax.empty((N_CHIPS,) + s.shape, s.dtype) for s in sources]
    else:
        lands = [lax.empty((3,) + s.shape[1:], s.dtype) for s in sources]
    deps = [] if after is None else [after]

    def body(*refs):
        srcs, lnds = refs[:n], refs[n:2 * n]
        send_sems, recv_sems = refs[2 * n + len(deps)], refs[2 * n + len(deps) + 1]
        for cp in _split_copies(kind, srcs, lnds, send_sems, recv_sems):
            cp.start()
        refs[-1][...] = jnp.zeros_like(refs[-1])

    hbm = lambda a: pltpu.with_memory_space_constraint(a, pltpu.HBM)
    outs = pl.pallas_call(
        body, name=name,
        in_specs=[_HBM] * (2 * n) + [_ANY] * len(deps),
        out_specs=[_SEM, _SEM] + [_HBM] * (2 * n) + [pl.BlockSpec(memory_space=pltpu.VMEM)],
        out_shape=[pltpu.SemaphoreType.DMA((3 * n,)), pltpu.SemaphoreType.DMA((3 * n,))]
        + [pltpu.HBM(a.shape, a.dtype) for a in list(sources) + lands] + [jax.ShapeDtypeStruct((8, LANES), F32)],
        input_output_aliases={k: 2 + k for k in range(2 * n)},
        compiler_params=pltpu.CompilerParams(has_side_effects=_SPLIT_EFFECT),
    )(*[hbm(s) for s in sources], *[hbm(l) for l in lands], *deps)
    return _Split(outs[0], outs[1], tuple(outs[2:2 + n]), tuple(outs[2 + n:2 + 2 * n]), outs[-1])


def _split_wait(name, h, kind, after):
    n = len(h.sources)

    def body(*refs):
        srcs, lnds = refs[:n], refs[n:2 * n]
        for cp in _split_copies(kind, srcs, lnds, refs[2 * n], refs[2 * n + 1]):
            cp.wait_send()
            cp.wait_recv()

    outs = pl.pallas_call(
        body, name=name,
        in_specs=[_HBM] * (2 * n) + [_SEM, _SEM] + [_ANY] * len(after),
        out_specs=[_HBM] * (2 * n),
        out_shape=[pltpu.HBM(a.shape, a.dtype) for a in h.sources + h.lands],
        input_output_aliases={k: k for k in range(2 * n)},
        compiler_params=pltpu.CompilerParams(has_side_effects=_SPLIT_EFFECT),
    )(*h.sources, *h.lands, h.send_sems, h.recv_sems, *after)
    return outs[:n], outs[n:]


def _all_reduce_small(vec):
    R = vec.shape[0]

    def body(v_ref, o_ref, buf, send_sem, recv_sem):
        x, y, c = lax.axis_index("x"), lax.axis_index("y"), lax.axis_index("c")
        me = 4 * x + 2 * y + c
        buf[me] = v_ref[...]
        copies = []
        for r in range(1, N_DEV):
            fx, fy, fc = (r >> 2) & 1, (r >> 1) & 1, r & 1
            peer = (x ^ fx, y ^ fy, c ^ fc)
            copies.append(pltpu.make_async_remote_copy(src_ref=v_ref, dst_ref=buf.at[me], send_sem=send_sem.at[r - 1],
                                                       recv_sem=recv_sem.at[r - 1], device_id=peer, device_id_type=MESH))
        for q in copies:
            q.start()
        for r in range(1, N_DEV):
            fx, fy, fc = (r >> 2) & 1, (r >> 1) & 1, r & 1
            src = 4 * (x ^ fx) + 2 * (y ^ fy) + (c ^ fc)
            pltpu.make_async_remote_copy(src_ref=v_ref, dst_ref=buf.at[src], send_sem=send_sem.at[r - 1],
                                         recv_sem=recv_sem.at[r - 1], device_id=(x, y, c), device_id_type=MESH).wait_recv()
        acc = buf[0]
        for d in range(1, N_DEV):
            acc = acc + buf[d]
        o_ref[...] = acc
        for q in copies:
            q.wait_send()

    vm = pl.BlockSpec(memory_space=pltpu.VMEM)
    return pl.pallas_call(
        body, in_specs=[vm], out_specs=vm, out_shape=jax.ShapeDtypeStruct((R, LANES), F32),
        scratch_shapes=[pltpu.VMEM((N_DEV, R, LANES), F32), pltpu.SemaphoreType.DMA((N_DEV - 1,)),
                        pltpu.SemaphoreType.DMA((N_DEV - 1,))],
        name="all_reduce_small")(vec)


_INPUTS = ["x", "mem", "g_mix", "w_in", "conv_w", "conv_b", "dt_bias", "a_log", "d_skip", "ssm_norm_w", "g_q", "g_k",
           "f_bias", "w_out", "g_xattn", "g_mem", "xq_w", "xkv_w", "xg_q", "xg_k", "xo_w", "g_mlp", "w_up", "w_down"]
_WEIGHTS = _INPUTS[2:]
_BIG = ["w_in", "w_out", "xq_w", "xkv_w", "xo_w", "w_up", "w_down"]
_LATE = _BIG[1:]
_COL_SHARDED = ["w_in", "xkv_w", "w_up"]
_SMALL = [n for n in _WEIGHTS if n not in _BIG]


def _pack_rows(arrs):
    rows = []
    for a in arrs:
        flat = a.reshape(-1)
        pad = -flat.shape[0] % LANES
        rows.append(jnp.pad(flat, (0, pad)).reshape(-1, LANES))
    out = jnp.concatenate(rows, axis=0)
    return jnp.pad(out, ((0, -out.shape[0] % 8), (0, 0)))


def _unpack_rows(packed, shapes):
    out, r = [], 0
    for s in shapes:
        n = math.prod(s)
        nr = -(-n // LANES)
        out.append(packed[r:r + nr].reshape(-1)[:n].reshape(s))
        r += nr
    return out


def kernel(x, mem, g_mix, w_in, conv_w, conv_b, dt_bias, a_log, d_skip, ssm_norm_w, g_q, g_k, f_bias, w_out, g_xattn, g_mem, xq_w, xkv_w, xg_q, xg_k, xo_w, g_mlp, w_up, w_down, loss_target, m_g_mix, m_w_in, m_conv_w, m_conv_b, m_dt_bias, m_a_log, m_d_skip, m_ssm_norm_w, m_g_q, m_g_k, m_f_bias, m_w_out, m_g_xattn, m_g_mem, m_xq_w, m_xkv_w, m_xg_q, m_xg_k, m_xo_w, m_g_mlp, m_w_up, m_w_down, v_g_mix, v_w_in, v_conv_w, v_conv_b, v_dt_bias, v_a_log, v_d_skip, v_ssm_norm_w, v_g_q, v_g_k, v_f_bias, v_w_out, v_g_xattn, v_g_mem, v_xq_w, v_xkv_w, v_xg_q, v_xg_k, v_xo_w, v_g_mlp, v_w_up, v_w_down):
    args = (x, mem, g_mix, w_in, conv_w, conv_b, dt_bias, a_log, d_skip, ssm_norm_w, g_q, g_k, f_bias, w_out, g_xattn,
            g_mem, xq_w, xkv_w, xg_q, xg_k, xo_w, g_mlp, w_up, w_down)
    w = dict(zip(_INPUTS, args))
    mom1 = dict(zip(_WEIGHTS, (m_g_mix, m_w_in, m_conv_w, m_conv_b, m_dt_bias, m_a_log, m_d_skip, m_ssm_norm_w, m_g_q,
                               m_g_k, m_f_bias, m_w_out, m_g_xattn, m_g_mem, m_xq_w, m_xkv_w, m_xg_q, m_xg_k, m_xo_w,
                               m_g_mlp, m_w_up, m_w_down)))
    mom2 = dict(zip(_WEIGHTS, (v_g_mix, v_w_in, v_conv_w, v_conv_b, v_dt_bias, v_a_log, v_d_skip, v_ssm_norm_w, v_g_q,
                               v_g_k, v_f_bias, v_w_out, v_g_xattn, v_g_mem, v_xq_w, v_xkv_w, v_xg_q, v_xg_k, v_xo_w,
                               v_g_mlp, v_w_up, v_w_down)))
    chip = _chip_index(lax.axis_index("x"), lax.axis_index("y"))
    core = lax.axis_index("c")

    shard_bf = {n: w[n][0].astype(BF16) for n in _BIG}

    def layout_for_compute(n, g):
        if n == "w_in":
            g = _to_kernel_cols(g.transpose(1, 0, 2).reshape(g.shape[1], IN_COLS))
            return jnp.pad(g, ((0, 0), (0, IN_COLS_PAD - IN_COLS)))
        return g if n in _COL_SHARDED else g.reshape(N_CHIPS * g.shape[1], g.shape[2])

    def layout_for_reduction(n, g):
        if n == "w_in":
            g = _to_reference_cols(g).reshape(g.shape[0], N_CHIPS, IN_COLS // N_CHIPS).transpose(1, 0, 2)
        elif n not in _COL_SHARDED:
            g = g.reshape(N_CHIPS, g.shape[0] // N_CHIPS, g.shape[1])
        return g.reshape(N_CHIPS, 2, g.shape[1] // 2, g.shape[2])

    def pair_sums_of(names, grads, tag):
        grads4 = [layout_for_reduction(n, grads[n]) for n in names]
        from_sibling = _sibling_send_halves(grads4, "rs_sibling_halves_" + tag)
        sums = []
        for n, g, fs in zip(names, grads4, from_sibling):
            mine = lax.dynamic_index_in_dim(g, core, axis=1, keepdims=False)
            flat = lambda a: a.reshape(-1, a.shape[-1])
            (s,) = _nsum([flat(mine), flat(fs)], (BF16,), "rs_pair_sum_" + n)
            sums.append(s.reshape(mine.shape))
        return sums

    def chip_sums_of(names, pair_sums, from_chips):
        out = []
        for n, ps, fc in zip(names, pair_sums, from_chips):
            own = lax.dynamic_index_in_dim(ps, chip, axis=0, keepdims=False)
            (r,) = _nsum([own, fc[0], fc[1], fc[2]], (F32,), "rs_chip_sum_" + n)
            out.append(r)
        return out

    halves_in = shard_bf["w_in"].reshape(2, shard_bf["w_in"].shape[0] // 2, -1)
    g_in, g_conv = _all_gather_chips([halves_in], [w["conv_w"][0]])
    g_in = lax.dynamic_update_index_in_dim(g_in, halves_in, chip, axis=0)
    g_conv = lax.dynamic_update_index_in_dim(g_conv, w["conv_w"][0], chip, axis=0)
    w_in_full = layout_for_compute("w_in", g_in.reshape(N_CHIPS, -1, g_in.shape[-1]))
    p = {n: w[n] for n in _SMALL}
    p["conv_w"] = g_conv.transpose(1, 0, 2).reshape(CONV_WIDTH, CONV_DIM)
    gather = _split_start("gather_late", [shard_bf[n] for n in _LATE], "gather", after=g_in)
    p["g_mix"] = p["g_mix"] + gather.token[:1, :1]

    def late_weights(after):
        srcs, lands = _split_wait("gather_late_wait", gather, "gather", after)
        lands = [lax.dynamic_update_index_in_dim(l, s, chip, axis=0) for l, s in zip(lands, srcs)]
        return {n: layout_for_compute(n, l) for n, l in zip(_LATE, lands)}

    scatter = {}

    def send_late_grads(grads):
        sums = pair_sums_of(_LATE, grads, "late")
        scatter["h"] = _split_start("scatter_late", sums, "scatter", after=None)
        return scatter["h"].token

    loss_row, dx, g_w_in, gp = _layer_fwd_bwd(x[0], mem[0], loss_target[0], w_in_full, p, late_weights, send_late_grads)

    sums_in = pair_sums_of(["w_in"], {"w_in": g_w_in}, "w_in")
    scatter_in = _split_start("scatter_w_in", sums_in, "scatter", after=None)

    grad, delta, new_m, new_v = {}, {}, {}, {}

    def finish(names, pair_sums, from_chips, tag):
        reduced = chip_sums_of(names, pair_sums, from_chips)
        for n, g, r in zip(names, _sibling_exchange(reduced, "rs_sibling_exchange_" + tag), reduced):
            shape = w[n].shape
            g2 = lax.dynamic_update_index_in_dim(g, r, core, axis=0).reshape(shape[1], shape[2])
            d, m1, v1 = _adamw(w[n][0], g2, mom1[n][0], mom2[n][0], "adamw_" + n)
            grad[n], delta[n], new_m[n], new_v[n] = (a.reshape(shape) for a in (g2, d, m1, v1))

    sums_late, from_chips_late = _split_wait("scatter_late_wait", scatter["h"], "scatter", (dx, scatter_in.token))
    finish(_LATE, sums_late, from_chips_late, "late")

    small_shapes = [gp[n].shape for n in _SMALL] + [(1, LANES)]
    packed = _pack_rows([gp[n] for n in _SMALL] + [loss_row])
    summed = _unpack_rows(_all_reduce_small(packed), small_shapes)
    gsmall = dict(zip(_SMALL, summed[:-1]))
    loss = summed[-1][0, 0]
    shard_cols = CONV_DIM // N_CHIPS
    gsmall["conv_w"] = lax.dynamic_slice_in_dim(gsmall["conv_w"], chip * shard_cols, shard_cols, axis=1)

    sums_in, from_chips_in = _split_wait("scatter_w_in_wait", scatter_in, "scatter",
                                         (summed[-1], *[new_v[n] for n in _LATE]))
    finish(["w_in"], sums_in, from_chips_in, "w_in")

    pk = lambda src: _pack_rows([src[n] for n in _SMALL])
    for n in _SMALL:
        gsmall[n] = gsmall[n].reshape(w[n].shape)
    d, m1, v1 = _adamw(pk(w), pk(gsmall), pk(mom1), pk(mom2), "adamw_small")
    shapes = [w[n].shape for n in _SMALL]
    for n, dn, mn, vn in zip(_SMALL, _unpack_rows(d, shapes), _unpack_rows(m1, shapes), _unpack_rows(v1, shapes)):
        grad[n], delta[n], new_m[n], new_v[n] = gsmall[n], dn, mn, vn

    return (loss, dx[None], *[grad[n] for n in _WEIGHTS], *[delta[n] for n in _WEIGHTS],
            *[new_m[n] for n in _WEIGHTS], *[new_v[n] for n in _WEIGHTS])
```

```python
import math
from typing import NamedTuple

import jax
import jax.numpy as jnp
from jax import lax
from jax.experimental import pallas as pl
from jax.experimental.pallas import tpu as pltpu

F32 = jnp.float32
BF16 = jnp.bfloat16
HI = lax.Precision.HIGHEST
MESH = pl.DeviceIdType.MESH

EPS = 1e-5
CHUNK = 128
SSM_HEADS = 16
SSM_GROUPS = 2
HEADS_PER_GROUP = SSM_HEADS // SSM_GROUPS
HEAD_DIM = 64
SSM_STATE = 128
ATTN_HEADS = 16
XATTN_HEADS = 4
XATTN_DIM = 256
CONV_WIDTH = 4
N_CHIPS = 4
N_DEV = 8
LANES = 128
VMEM_LIMIT = 56 * 1024 * 1024

ADAM_LR = 0.001
ADAM_B1 = 0.9
ADAM_B2 = 0.999
ADAM_EPS = 1e-08
ADAM_WD = 0.01
ADAM_STEP = 10


def _params(sem):
    return pltpu.CompilerParams(dimension_semantics=sem, vmem_limit_bytes=VMEM_LIMIT)


def _pick(n, cands):
    for c in cands:
        if n % c == 0:
            return c
    return n


def _mm(a, b, mode, name, out_dtypes=(F32,), epilogue=None, extras=(), b_chunks=1, out_chunks=1,
        tm=None, tn=None, tk=None):
    if mode == "nn":
        M, K = a.shape
        N = b.shape[-1] * b_chunks
    elif mode == "nt":
        M, K = a.shape
        N = b.shape[-2]
        assert b.shape[-1] * b_chunks == K
    else:
        K, M = a.shape
        N = b.shape[-1] * b_chunks
    tm = tm or _pick(M, (2048, 1024, 512, 256, 128))
    tn = tn or _pick(N // max(b_chunks if mode != "nt" else 1, out_chunks), (512, 640, 384, 256, 128))
    if tk is None:
        kmax = b.shape[-1] if mode == "nt" else K
        tk = kmax if kmax <= 2048 else _pick(kmax, (2048, 1152, 1024, 512))
    nk = K // tk
    assert M % tm == 0 and N % tn == 0 and K % tk == 0
    grid = (M // tm, N // tn, nk)

    if mode == "tn":
        a_spec = pl.BlockSpec((tk, tm), lambda i, j, k: (k, i))
    else:
        a_spec = pl.BlockSpec((tm, tk), lambda i, j, k: (i, k))

    def b_index(t_row, t_last, tile_last):
        if b_chunks == 1:
            return (t_row, t_last)
        q = (b.shape[-1]) // tile_last
        return (t_last // q, t_row, t_last % q)

    if mode == "nn" or mode == "tn":
        bshape = (tk, tn)
        bmap = lambda i, j, k: b_index(k, j, tn)
    else:
        bshape = (tn, tk)
        bmap = lambda i, j, k: b_index(j, k, tk)
    if b_chunks > 1:
        bshape = (None,) + bshape
    b_spec = pl.BlockSpec(bshape, bmap)

    if out_chunks == 1:
        o_spec = pl.BlockSpec((tm, tn), lambda i, j, k: (i, j))
        o_shape = (M, N)
    else:
        qo = (N // out_chunks) // tn
        o_spec = pl.BlockSpec((None, tm, tn), lambda i, j, k: (j // qo, i, j % qo))
        o_shape = (out_chunks, M, N // out_chunks)
    e_spec = pl.BlockSpec((tm, tn), lambda i, j, k: (i, j))

    dims = {"nn": (((1,), (0,)), ((), ())), "nt": (((1,), (1,)), ((), ())), "tn": (((0,), (0,)), ((), ()))}[mode]
    n_ex = len(extras)
    n_out = len(out_dtypes)

    def body(*refs):
        a_ref, b_ref = refs[0], refs[1]
        ex_refs = refs[2:2 + n_ex]
        o_refs = refs[2 + n_ex:2 + n_ex + n_out]

        def finish(acc):
            outs = epilogue(acc, *[r[...] for r in ex_refs]) if epilogue is not None else (acc,)
            for r, o in zip(o_refs, outs):
                r[...] = o.astype(r.dtype)

        part = lax.dot_general(a_ref[...].astype(BF16), b_ref[...].astype(BF16), dims,
                               preferred_element_type=F32)
        if nk == 1:
            finish(part)
        else:
            acc_ref = refs[-1]
            k = pl.program_id(2)

            @pl.when(k == 0)
            def _():
                acc_ref[...] = part

            @pl.when(k > 0)
            def _():
                acc_ref[...] += part

            @pl.when(k == nk - 1)
            def _():
                finish(acc_ref[...])

    outs = pl.pallas_call(
        body,
        grid=grid,
        in_specs=[a_spec, b_spec] + [e_spec] * n_ex,
        out_specs=[o_spec] * n_out,
        out_shape=[jax.ShapeDtypeStruct(o_shape, d) for d in out_dtypes],
        scratch_shapes=[pltpu.VMEM((tm, tn), F32)] if nk > 1 else [],
        compiler_params=_params(("parallel", "parallel", "arbitrary")),
        name=name,
    )(a, b, *extras)
    return outs[0] if n_out == 1 else outs


def _rms(x, g):
    r = lax.rsqrt(jnp.mean(x * x, axis=-1, keepdims=True) + EPS)
    return x * r * g


def _rmsnorm_fwd(x, g, name):
    R, D = x.shape
    tr = _pick(R, (512, 256))

    def body(x_ref, g_ref, o_ref):
        o_ref[...] = _rms(x_ref[...], g_ref[...]).astype(o_ref.dtype)

    return pl.pallas_call(
        body, grid=(R // tr,),
        in_specs=[pl.BlockSpec((tr, D), lambda i: (i, 0)), pl.BlockSpec((1, D), lambda i: (0, 0))],
        out_specs=pl.BlockSpec((tr, D), lambda i: (i, 0)),
        out_shape=jax.ShapeDtypeStruct((R, D), BF16),
        compiler_params=_params(("parallel",)), name=name)(x, g)


def _rmsnorm_bwd(x, g, dh, dres, name):
    R, D = x.shape
    tr = _pick(R, (256,))
    has_res = dres is not None

    def body(*refs):
        if has_res:
            x_ref, g_ref, dh_ref, dres_ref, dx_ref, dg_ref = refs
        else:
            x_ref, g_ref, dh_ref, dx_ref, dg_ref = refs
        _, vjp = jax.vjp(_rms, x_ref[...], g_ref[...])
        dx, dg = vjp(dh_ref[...])
        if has_res:
            dx = dx + dres_ref[...]
        dx_ref[...] = dx

        @pl.when(pl.program_id(0) == 0)
        def _():
            dg_ref[...] = jnp.zeros_like(dg_ref)

        dg_ref[...] += dg

    row = pl.BlockSpec((tr, D), lambda i: (i, 0))
    vec = pl.BlockSpec((1, D), lambda i: (0, 0))
    ins = [x, g, dh] + ([dres] if has_res else [])
    return pl.pallas_call(
        body, grid=(R // tr,),
        in_specs=[row, vec, row] + ([row] if has_res else []),
        out_specs=[row, vec],
        out_shape=[jax.ShapeDtypeStruct((R, D), F32), jax.ShapeDtypeStruct((1, D), F32)],
        compiler_params=_params(("arbitrary",)), name=name)(*ins)


def _shift_down(u, k):
    if k == 0:
        return u
    rows = lax.broadcasted_iota(jnp.int32, u.shape, 0)
    return jnp.where(rows >= k, pltpu.roll(u, k, axis=0), 0.0)


def _shift_up(u, k):
    if k == 0:
        return u
    n = u.shape[0]
    rows = lax.broadcasted_iota(jnp.int32, u.shape, 0)
    return jnp.where(rows < n - k, pltpu.roll(u, n - k, axis=0), 0.0)


def _conv_pre(u, w, b):
    pre = b
    for j in range(CONV_WIDTH):
        pre = pre + w[j:j + 1, :] * _shift_down(u, CONV_WIDTH - 1 - j)
    return pre


def _conv_fwd(proj, col0, ncols, conv_w, conv_b):
    S = proj.shape[0]
    cb0 = col0 // LANES

    def body(u_ref, w_ref, b_ref, o_ref):
        pre = _conv_pre(u_ref[...], w_ref[...], b_ref[...])
        o_ref[...] = pre * jax.nn.sigmoid(pre)

    return pl.pallas_call(
        body, grid=(ncols // LANES,),
        in_specs=[pl.BlockSpec((S, LANES), lambda j: (0, j + cb0)),
                  pl.BlockSpec((CONV_WIDTH, LANES), lambda j: (0, j)),
                  pl.BlockSpec((1, LANES), lambda j: (0, j))],
        out_specs=pl.BlockSpec((S, LANES), lambda j: (0, j)),
        out_shape=jax.ShapeDtypeStruct((S, ncols), F32),
        compiler_params=_params(("parallel",)), name="conv_fwd")(proj, conv_w, conv_b)


def _conv_bwd(proj, col0, ncols, conv_w, conv_b, douts, dproj):
    S = proj.shape[0]
    cb0 = col0 // LANES
    starts = [0]
    for d in douts:
        starts.append(starts[-1] + d.shape[1] // LANES)
    assert starts[-1] == ncols // LANES
    nd = len(douts)

    def body(u_ref, w_ref, b_ref, *rest):
        d_refs, (du_ref, dw_ref, db_ref) = rest[:nd], rest[nd + 1:]
        j = pl.program_id(0)
        dout = d_refs[-1][...]
        for i in range(nd - 2, -1, -1):
            dout = jnp.where(j < starts[i + 1], d_refs[i][...], dout)
        u = u_ref[...]
        w = w_ref[...]
        pre = _conv_pre(u, w, b_ref[...])
        s = jax.nn.sigmoid(pre)
        dpre = dout * (s * (1.0 + pre * (1.0 - s)))
        du = jnp.zeros_like(u)
        rows = []
        for j in range(CONV_WIDTH):
            k = CONV_WIDTH - 1 - j
            du = du + w[j:j + 1, :] * _shift_up(dpre, k)
            rows.append(jnp.sum(dpre * _shift_down(u, k), axis=0, keepdims=True))
        du_ref[...] = du.astype(du_ref.dtype)
        rows.append(jnp.zeros((8 - CONV_WIDTH, LANES), F32))
        dw_ref[...] = jnp.concatenate(rows, axis=0)
        db_ref[...] = jnp.sum(dpre, axis=0, keepdims=True)

    return pl.pallas_call(
        body, grid=(ncols // LANES,),
        in_specs=[pl.BlockSpec((S, LANES), lambda j: (0, j + cb0)),
                  pl.BlockSpec((CONV_WIDTH, LANES), lambda j: (0, j)),
                  pl.BlockSpec((1, LANES), lambda j: (0, j))]
        + [pl.BlockSpec((S, LANES), lambda j, lo=starts[i], hi=starts[i + 1]: (0, jnp.clip(j - lo, 0, hi - lo - 1)))
           for i in range(nd)] + [_ANY],
        out_specs=[pl.BlockSpec((S, LANES), lambda j: (0, j + cb0)),
                   pl.BlockSpec((8, LANES), lambda j: (0, j)),
                   pl.BlockSpec((1, LANES), lambda j: (0, j))],
        out_shape=[jax.ShapeDtypeStruct(dproj.shape, dproj.dtype),
                   jax.ShapeDtypeStruct((8, ncols), F32),
                   jax.ShapeDtypeStruct((1, ncols), F32)],
        input_output_aliases={3 + nd: 0},
        compiler_params=_params(("parallel",)), name="conv_bwd")(proj, conv_w, conv_b, *douts, dproj)


def _softplus(x):
    return jnp.maximum(x, 0.0) + jnp.log1p(jnp.exp(-jnp.abs(x)))


def _dot32(a, b, dims=(((1,), (0,)), ((), ()))):
    return lax.dot_general(a, b, dims, precision=HI, preferred_element_type=F32)


def _dotd(a, b, dims=(((1,), (0,)), ((), ()))):
    return lax.dot_general(a, b, dims, preferred_element_type=F32)


PAIRS_PER_GROUP = HEADS_PER_GROUP // 2


def _ssd_chunk(xs, Bm, Cm, z, dtr, dtb, alog, dsk, nw, h):
    L = Bm.shape[0]
    ri = lax.broadcasted_iota(jnp.int32, (L, L), 0)
    ci = lax.broadcasted_iota(jnp.int32, (L, L), 1)
    causal = ri >= ci
    tril = causal.astype(F32)
    first = _first_head(L)
    first1 = _first_head(1)
    CB = _dotd(Cm, Bm, _NT)
    gated, hnew = [], []
    ssq = jnp.zeros((L, 1), F32)
    for pp in range(len(xs)):
        dts, cums, tots, decay = [], [], [], []
        for a in range(2):
            r = 2 * pp + a
            dt = _softplus(dtr[r] + dtb[r])
            dA = dt * (-jnp.exp(alog[r]))
            acs = _dot32(tril, dA)
            cc = jnp.broadcast_to(acs, (L, L))
            decay.append(CB * jnp.exp(jnp.where(causal, cc - cc.T, -1e30)))
            dts.append(dt)
            cums.append(acs)
            tots.append(jnp.sum(dA, axis=0, keepdims=True))
        dt2 = jnp.where(first, dts[0], dts[1])
        acs2 = jnp.where(first, cums[0], cums[1])
        tot2 = jnp.where(first1, tots[0], tots[1])
        dsk2 = jnp.where(first1, dsk[2 * pp], dsk[2 * pp + 1])
        X = xs[pp] * dt2
        y = (jnp.where(first, _dotd(decay[0], X), _dotd(decay[1], X)) + jnp.exp(acs2) * _dotd(Cm, h[pp])
             + dsk2 * xs[pp])
        hnew.append(jnp.exp(tot2) * h[pp] + _dotd(Bm, X * jnp.exp(tot2 - acs2), _TN))
        g = y * (z[pp] * jax.nn.sigmoid(z[pp]))
        ssq = ssq + jnp.sum(g * g, axis=-1, keepdims=True)
        gated.append(g)
    rs = lax.rsqrt(ssq / (len(xs) * LANES) + EPS)
    return [g * rs * nw[pp] for pp, g in enumerate(gated)], hnew


def _ssd_args(xs_ref, b_ref, c_ref, z_ref, dt_ref, dtb_ref, al_ref, dsk_ref, nw_ref, h_ref):
    pairs = range(PAIRS_PER_GROUP)
    heads = range(HEADS_PER_GROUP)
    lanes = lambda ref, pp: ref[:, pp * LANES:(pp + 1) * LANES]
    return ([lanes(xs_ref, pp) for pp in pairs], b_ref[...], c_ref[...], [lanes(z_ref, pp) for pp in pairs],
            [dt_ref[r] for r in heads], [dtb_ref[r] for r in heads], [al_ref[r] for r in heads],
            [dsk_ref[r] for r in heads], [lanes(nw_ref, pp) for pp in pairs], [h_ref[pp] for pp in pairs])


def _ssd_specs(rev):
    H, N, L = HEADS_PER_GROUP, SSM_STATE, CHUNK
    gw = H * HEAD_DIM
    return dict(
        cols=lambda col0: pl.BlockSpec((L, gw), lambda g, c: (rev(c), col0 // gw + g)),
        bc=lambda first_block: pl.BlockSpec((L, N), lambda g, c: (rev(c), first_block + g)),
        dt=pl.BlockSpec((H, L, 1), lambda g, c: (g, rev(c), 0)),
        scal=pl.BlockSpec((H, 1, 1), lambda g, c: (g, 0, 0)),
        nw=pl.BlockSpec((1, gw), lambda g, c: (0, g)),
        hs=pl.BlockSpec((None, PAIRS_PER_GROUP, N, LANES), lambda g, c: (rev(c), g, 0, 0)),
        b_block=SSM_INNER // N,
    )


def _ssd_fwd(xbc, proj, dt_hm, dtb, alog, dsk, nw):
    S = xbc.shape[0]
    N, L = SSM_STATE, CHUNK
    nc = S // L
    sp = _ssd_specs(lambda c: c)

    def body(xs_ref, b_ref, c_ref, z_ref, dt_ref, dtb_ref, al_ref, dsk_ref, nw_ref, y_ref, hs_ref, h_ref):
        @pl.when(pl.program_id(1) == 0)
        def _():
            h_ref[...] = jnp.zeros_like(h_ref)

        hs_ref[...] = h_ref[...]
        out, hnew = _ssd_chunk(*_ssd_args(xs_ref, b_ref, c_ref, z_ref, dt_ref, dtb_ref, al_ref, dsk_ref, nw_ref, h_ref))
        for pp in range(PAIRS_PER_GROUP):
            y_ref[:, pp * LANES:(pp + 1) * LANES] = out[pp].astype(y_ref.dtype)
            h_ref[pp] = hnew[pp]

    return pl.pallas_call(
        body, grid=(SSM_GROUPS, nc),
        in_specs=[sp["cols"](0), sp["bc"](sp["b_block"]), sp["bc"](sp["b_block"] + SSM_GROUPS), sp["cols"](COL_Z),
                  sp["dt"], sp["scal"], sp["scal"], sp["scal"], sp["nw"]],
        out_specs=[sp["cols"](0), sp["hs"]],
        out_shape=[jax.ShapeDtypeStruct((S, SSM_INNER), BF16),
                   jax.ShapeDtypeStruct((nc, SSM_HEADS // 2, N, LANES), F32)],
        scratch_shapes=[pltpu.VMEM((PAIRS_PER_GROUP, N, LANES), F32)],
        compiler_params=_params(("parallel", "arbitrary")), name="ssd_fwd",
    )(xbc, xbc, xbc, proj, dt_hm, dtb, alog, dsk, nw)


def _ssd_bwd(xbc, proj, dt_hm, dtb, alog, dsk, nw, hs, dmixed, dproj):
    S = xbc.shape[0]
    N, L = SSM_STATE, CHUNK
    nc = S // L
    sp = _ssd_specs(lambda c: nc - 1 - c)

    def body(xs_ref, b_ref, c_ref, z_ref, dt_ref, dtb_ref, al_ref, dsk_ref, nw_ref, hs_ref, dy_ref, buf_ref,
             dxs_ref, dz_ref, db_ref, dc_ref, ddt_ref, ddtb_ref, dal_ref, ddsk_ref, dnw_ref, dh_ref):
        @pl.when(pl.program_id(1) == 0)
        def _():
            dh_ref[...] = jnp.zeros_like(dh_ref)
            ddtb_ref[...] = jnp.zeros_like(ddtb_ref)
            dal_ref[...] = jnp.zeros_like(dal_ref)
            ddsk_ref[...] = jnp.zeros_like(ddsk_ref)
            dnw_ref[...] = jnp.zeros_like(dnw_ref)

        pairs = range(PAIRS_PER_GROUP)
        lanes = lambda pp: slice(pp * LANES, (pp + 1) * LANES)
        _, vjp = jax.vjp(_ssd_chunk, *_ssd_args(xs_ref, b_ref, c_ref, z_ref, dt_ref, dtb_ref, al_ref, dsk_ref, nw_ref,
                                                hs_ref))
        dxs, dB, dC, dz, ddt, ddtb, dal, ddsk, dnw, dh = vjp(([dy_ref[:, lanes(pp)] for pp in pairs],
                                                              [dh_ref[pp] for pp in pairs]))
        db_ref[...] = dB
        dc_ref[...] = dC
        for pp in pairs:
            dxs_ref[:, lanes(pp)] = dxs[pp]
            dz_ref[:, lanes(pp)] = dz[pp].astype(dz_ref.dtype)
            dnw_ref[:, lanes(pp)] += dnw[pp]
            dh_ref[pp] = dh[pp]
        for r in range(HEADS_PER_GROUP):
            ddt_ref[r] = ddt[r]
            ddtb_ref[r] += ddtb[r]
            dal_ref[r] += dal[r]
            ddsk_ref[r] += ddsk[r]

    bc_out = pl.BlockSpec((L, N), lambda g, c: (nc - 1 - c, g))
    return pl.pallas_call(
        body, grid=(SSM_GROUPS, nc),
        in_specs=[sp["cols"](0), sp["bc"](sp["b_block"]), sp["bc"](sp["b_block"] + SSM_GROUPS), sp["cols"](COL_Z),
                  sp["dt"], sp["scal"], sp["scal"], sp["scal"], sp["nw"], sp["hs"], sp["cols"](0), _ANY],
        out_specs=[sp["cols"](0), sp["cols"](COL_Z), bc_out, bc_out, sp["dt"], sp["scal"], sp["scal"], sp["scal"],
                   sp["nw"]],
        input_output_aliases={11: 1},
        out_shape=[jax.ShapeDtypeStruct((S, SSM_INNER), F32), jax.ShapeDtypeStruct(dproj.shape, dproj.dtype),
                   jax.ShapeDtypeStruct((S, SSM_GROUPS * N), F32), jax.ShapeDtypeStruct((S, SSM_GROUPS * N), F32),
                   jax.ShapeDtypeStruct((SSM_HEADS, S, 1), F32),
                   jax.ShapeDtypeStruct((SSM_HEADS, 1, 1), F32), jax.ShapeDtypeStruct((SSM_HEADS, 1, 1), F32),
                   jax.ShapeDtypeStruct((SSM_HEADS, 1, 1), F32), jax.ShapeDtypeStruct((1, SSM_INNER), F32)],
        scratch_shapes=[pltpu.VMEM((PAIRS_PER_GROUP, N, LANES), F32)],
        compiler_params=_params(("parallel", "arbitrary")), name="ssd_bwd",
    )(xbc, xbc, xbc, proj, dt_hm, dtb, alog, dsk, nw, hs, dmixed, dproj)


ATTN_SCALE = HEAD_DIM ** -0.5
ATTN_PAIRS = ATTN_HEADS // 2


def _first_head(rows):
    return lax.broadcasted_iota(jnp.int32, (rows, LANES), 1) < HEAD_DIM


def _pair_norm(x, g2, scale):
    first = _first_head(x.shape[0])
    sq = x * x
    ms0 = jnp.sum(jnp.where(first, sq, 0.0), axis=-1, keepdims=True) * (1.0 / HEAD_DIM)
    ms1 = jnp.sum(jnp.where(first, 0.0, sq), axis=-1, keepdims=True) * (1.0 / HEAD_DIM)
    r = jnp.where(first, lax.rsqrt(ms0 + EPS), lax.rsqrt(ms1 + EPS))
    return x * r * g2 * scale


def _qk_prep_fwd(proj, gq2, gk2):
    S = proj.shape[0]
    tq = _pick(S, (512, 256))

    def body(q_ref, k_ref, v_ref, gq_ref, gk_ref, qo_ref, ko_ref, vo_ref):
        qo_ref[...] = _pair_norm(q_ref[...], gq_ref[...], ATTN_SCALE).astype(BF16)
        ko_ref[...] = _pair_norm(k_ref[...], gk_ref[...], 1.0).astype(BF16)
        vo_ref[...] = v_ref[...].astype(BF16)

    col = lambda c0: pl.BlockSpec((tq, LANES), lambda h, i: (i, c0 // LANES + h))
    blk = pl.BlockSpec((tq, LANES), lambda h, i: (i, h))
    vec = pl.BlockSpec((1, LANES), lambda h, i: (0, 0))
    return pl.pallas_call(
        body, grid=(ATTN_PAIRS, S // tq), in_specs=[col(COL_Q), col(COL_K), col(COL_V), vec, vec],
        out_specs=[blk, blk, blk], out_shape=[jax.ShapeDtypeStruct((S, ATTN_WIDTH), BF16)] * 3,
        compiler_params=_params(("parallel", "parallel")), name="qk_prep_fwd")(proj, proj, proj, gq2, gk2)


def _qk_prep_bwd(proj, g2, dqs, dkn, dproj):
    S = proj.shape[0]
    tq = _pick(S, (512, 256))
    assert COL_K == COL_Q + ATTN_WIDTH

    def body(u_ref, g_ref, dqs_ref, dkn_ref, buf_ref, du_ref, dg_ref):
        h = pl.program_id(0)
        is_q = h < ATTN_PAIRS

        @pl.when((h % ATTN_PAIRS == 0) & (pl.program_id(1) == 0))
        def _():
            dg_ref[...] = jnp.zeros_like(dg_ref)

        scale = jnp.where(is_q, ATTN_SCALE, 1.0)
        _, vjp = jax.vjp(lambda u, g: _pair_norm(u, g, scale), u_ref[...], g_ref[...])
        du, dg = vjp(jnp.where(is_q, dqs_ref[...], dkn_ref[...]))
        du_ref[...] = du.astype(du_ref.dtype)
        dg_ref[...] += dg

    ublk = pl.BlockSpec((tq, LANES), lambda h, i: (i, COL_Q // LANES + h))
    gblk = pl.BlockSpec((None, 1, LANES), lambda h, i: (h // ATTN_PAIRS, 0, 0))
    qblk = pl.BlockSpec((tq, LANES), lambda h, i: (i, jnp.minimum(h, ATTN_PAIRS - 1)))
    kblk = pl.BlockSpec((tq, LANES), lambda h, i: (i, jnp.maximum(h - ATTN_PAIRS, 0)))
    return pl.pallas_call(
        body, grid=(2 * ATTN_PAIRS, S // tq), in_specs=[ublk, gblk, qblk, kblk, _ANY],
        out_specs=[ublk, gblk],
        out_shape=[jax.ShapeDtypeStruct(dproj.shape, dproj.dtype), jax.ShapeDtypeStruct((2, 1, LANES), F32)],
        input_output_aliases={4: 0},
        compiler_params=_params(("arbitrary", "arbitrary")), name="qk_prep_bwd")(proj, g2, dqs, dkn, dproj)


def _logf_cumsum_fwd(f_raw, f_bias):
    S, Hh = f_raw.shape
    L = CHUNK

    def body(f_ref, b_ref, o_ref, wide_ref):
        ri = lax.broadcasted_iota(jnp.int32, (L, L), 0)
        ci = lax.broadcasted_iota(jnp.int32, (L, L), 1)
        tril = (ri >= ci).astype(F32)
        carry = jnp.zeros((1, Hh), F32)
        for c in range(S // L):
            rows = slice(c * L, (c + 1) * L)
            lf = -_softplus(-(f_ref[rows, :] + b_ref[...]))
            cum = _dot32(tril, lf) + carry
            o_ref[rows, :] = cum
            for h in range(Hh):
                wide_ref[rows, h * HEAD_DIM:(h + 1) * HEAD_DIM] = jnp.broadcast_to(cum[:, h:h + 1], (L, HEAD_DIM))
            carry = cum[L - 1:L, :]

    return pl.pallas_call(
        body, out_shape=[jax.ShapeDtypeStruct((S, Hh), F32), jax.ShapeDtypeStruct((S, Hh * HEAD_DIM), F32)],
        name="logf_cumsum_fwd")(f_raw, f_bias)


def _logf_cumsum_bwd(f_raw, f_bias, dcum):
    S, Hh = f_raw.shape
    L = CHUNK

    def body(f_ref, b_ref, d_ref, df_ref, db_ref):
        ri = lax.broadcasted_iota(jnp.int32, (L, L), 0)
        ci = lax.broadcasted_iota(jnp.int32, (L, L), 1)
        triu = (ri <= ci).astype(F32)
        carry = jnp.zeros((1, Hh), F32)
        db = jnp.zeros((1, Hh), F32)
        for c in reversed(range(S // L)):
            suf = _dot32(triu, d_ref[c * L:(c + 1) * L, :]) + carry
            df = suf * jax.nn.sigmoid(-(f_ref[c * L:(c + 1) * L, :] + b_ref[...]))
            df_ref[c * L:(c + 1) * L, :] = df
            db = db + jnp.sum(df, axis=0, keepdims=True)
            carry = suf[0:1, :]
        db_ref[...] = db

    return pl.pallas_call(
        body, out_shape=[jax.ShapeDtypeStruct((S, Hh), F32), jax.ShapeDtypeStruct((1, Hh), F32)],
        name="logf_cumsum_bwd")(f_raw, f_bias, dcum)


_NT = (((1,), (1,)), ((), ()))
_TN = (((0,), (0,)), ((), ()))


def _mxu(a, b, dims=(((1,), (0,)), ((), ()))):
    return lax.dot_general(a, b, dims, preferred_element_type=F32)


def _flash_fwd(qs, kn, vb, cq, ck):
    S, W = qs.shape
    tq = tk = _pick(S, (512, 256))
    nmask = max(tq // tk, 1)

    def body(q_ref, k_ref, v_ref, cq_ref, ck_ref, o_ref, of_ref, lse_ref):
        i = pl.program_id(1)
        first = _first_head(tq)
        q2 = q_ref[...]
        zero = jnp.zeros_like(q2)
        qa = (jnp.where(first, q2, zero), jnp.where(first, zero, q2))
        cqa = (cq_ref[:, 0:1], cq_ref[:, HEAD_DIM:HEAD_DIM + 1])
        row0 = i * tq

        def step(j, carry, masked):
            ms, ls, acc, rem = carry
            off = pl.multiple_of(j * tk, tk)
            k = k_ref[pl.ds(off, tk), :]
            v = v_ref[pl.ds(off, tk), :]
            new_m, new_l, alphas, pvs, prs = [], [], [], [], []
            for a in range(2):
                s = _mxu(qa[a], k, _NT) + cqa[a] - ck_ref[a, :, pl.ds(off, tk)]
                if masked:
                    ri = lax.broadcasted_iota(jnp.int32, (tq, tk), 0) + row0
                    ci = lax.broadcasted_iota(jnp.int32, (tq, tk), 1) + off
                    s = jnp.where(ri >= ci, s, -1e30)
                m_new = jnp.maximum(ms[a], jnp.max(s, axis=-1, keepdims=True))
                alpha = jnp.exp(ms[a] - m_new)
                p = jnp.exp(s - m_new)
                new_l.append(alpha * ls[a] + jnp.sum(p, axis=-1, keepdims=True))
                new_m.append(m_new)
                alphas.append(alpha)
                p_hi = p.astype(BF16)
                pvs.append(_mxu(p_hi, v))
                prs.append(_mxu((p - p_hi.astype(F32)).astype(BF16), v))
            al = jnp.where(first, alphas[0], alphas[1])
            acc = al * acc + jnp.where(first, pvs[0], pvs[1])
            rem = al * rem + jnp.where(first, prs[0], prs[1])
            return tuple(new_m), tuple(new_l), acc, rem

        neg = jnp.full((tq, 1), -1e30, F32)
        z1 = jnp.zeros((tq, 1), F32)
        z2 = jnp.zeros((tq, LANES), F32)
        carry = ((neg, neg), (z1, z1), z2, z2)
        n_full = (i * tq) // tk
        carry = lax.fori_loop(0, n_full, lambda j, c: step(j, c, False), carry)
        for jj in range(nmask):
            carry = step(n_full + jj, carry, True)
        ms, ls, acc, rem = carry
        linv = jnp.where(first, 1.0 / ls[0], 1.0 / ls[1])
        o_ref[...] = (acc * linv).astype(o_ref.dtype)
        of_ref[...] = (acc + rem) * linv
        lse_ref[...] = jnp.where(first, ms[0] + jnp.log(ls[0]), ms[1] + jnp.log(ls[1]))

    qblk = pl.BlockSpec((tq, LANES), lambda h, i: (i, h))
    full = pl.BlockSpec((S, LANES), lambda h, i: (0, h))
    return pl.pallas_call(
        body, grid=(W // LANES, S // tq),
        in_specs=[qblk, full, full, qblk, pl.BlockSpec((2, 1, S), lambda h, i: (h, 0, 0))],
        out_specs=[qblk, qblk, qblk],
        out_shape=[jax.ShapeDtypeStruct((S, W), BF16), jax.ShapeDtypeStruct((S, W), F32),
                   jax.ShapeDtypeStruct((S, W), F32)],
        compiler_params=_params(("parallel", "parallel")), name="flash_fwd")(qs, kn, vb, cq, ck)


def _flash_bwd(qs, kn, vb, cq, ck, o_fine, do, do_col0, lse):
    S, W = qs.shape
    tq = tk = _pick(S, (512, 256))
    nq = S // tq
    nmask = max(tk // tq, 1)

    def body(q_ref, k_ref, v_ref, cq_ref, ck_ref, of_ref, do_ref, lse_ref, dq_ref, dk_ref, dv_ref, dck_ref):
        j = pl.program_id(1)

        @pl.when(j == 0)
        def _():
            dq_ref[...] = jnp.zeros_like(dq_ref)

        firstk = _first_head(tk)
        firstq = _first_head(tq)
        k2 = k_ref[...]
        v2 = v_ref[...]
        zk = jnp.zeros_like(k2)
        ka = (jnp.where(firstk, k2, zk), jnp.where(firstk, zk, k2))
        va = (jnp.where(firstk, v2, zk), jnp.where(firstk, zk, v2))
        cka = (ck_ref[0], ck_ref[1])
        col0 = j * tk

        def step(i, carry, masked):
            dk, dv, dck0, dck1 = carry
            dcks = [dck0, dck1]
            off = pl.multiple_of(i * tq, tq)
            rows = pl.ds(off, tq)
            q2 = q_ref[rows, :]
            dob = do_ref[rows, :].astype(BF16)
            prod = dob.astype(F32) * of_ref[rows, :]
            dkp, dvp, dqp = [], [], []
            for a in range(2):
                lane = pl.ds(a * HEAD_DIM, 1)
                s = _mxu(q2, ka[a], _NT) + cq_ref[rows, lane] - cka[a]
                if masked:
                    ri = lax.broadcasted_iota(jnp.int32, (tq, tk), 0) + off
                    ci = lax.broadcasted_iota(jnp.int32, (tq, tk), 1) + col0
                    s = jnp.where(ri >= ci, s, -1e30)
                p = jnp.exp(s - lse_ref[rows, lane])
                dp = _mxu(dob, va[a], _NT)
                own = jnp.where(firstq, prod, 0.0) if a == 0 else jnp.where(firstq, 0.0, prod)
                ds = p * (dp - jnp.sum(own, axis=-1, keepdims=True))
                dsb = ds.astype(BF16)
                dvp.append(_mxu(p.astype(BF16), dob, _TN))
                dkp.append(_mxu(dsb, q2, _TN))
                dqp.append(_mxu(dsb, k2))
                dcks[a] = dcks[a] - jnp.sum(ds, axis=0, keepdims=True)
            dq_ref[rows, :] += jnp.where(firstq, dqp[0], dqp[1])
            dk = dk + jnp.where(firstk, dkp[0], dkp[1])
            dv = dv + jnp.where(firstk, dvp[0], dvp[1])
            return dk, dv, dcks[0], dcks[1]

        z2 = jnp.zeros((tk, LANES), F32)
        z1 = jnp.zeros((1, tk), F32)
        carry = (z2, z2, z1, z1)
        i0 = (j * tk) // tq
        for ii in range(nmask):
            carry = step(i0 + ii, carry, True)
        dk, dv, dck0, dck1 = lax.fori_loop(i0 + nmask, nq, lambda i, c: step(i, c, False), carry)
        dk_ref[...] = dk
        dv_ref[...] = dv.astype(dv_ref.dtype)
        dck_ref[0] = dck0
        dck_ref[1] = dck1

    kblk = pl.BlockSpec((tk, LANES), lambda h, j: (j, h))
    full = pl.BlockSpec((S, LANES), lambda h, j: (0, h))
    dofull = pl.BlockSpec((S, LANES), lambda h, j: (0, do_col0 // LANES + h))
    rowt = pl.BlockSpec((2, 1, tk), lambda h, j: (h, 0, j))
    dvblk = pl.BlockSpec((tk, LANES), lambda h, j: (j, COL_V // LANES + h))
    return pl.pallas_call(
        body, grid=(W // LANES, S // tk),
        in_specs=[full, kblk, kblk, full, rowt, full, dofull, full],
        out_specs=[full, kblk, dvblk, rowt],
        out_shape=[jax.ShapeDtypeStruct((S, W), F32), jax.ShapeDtypeStruct((S, W), F32),
                   jax.ShapeDtypeStruct((S, IN_COLS_PAD), BF16), jax.ShapeDtypeStruct((2 * (W // LANES), 1, S), F32)],
        compiler_params=_params(("parallel", "arbitrary")), name="flash_bwd")(qs, kn, vb, cq, ck, o_fine, do, lse)


XATTN_SCALE = XATTN_DIM ** -0.5


def _xq_norm(q, g):
    return _rms(q, g) * XATTN_SCALE


def _xattn_fwd(xq, kv, gq, gk):
    S = xq.shape[0]
    Mm = kv.shape[0]
    Dh = XATTN_DIM
    tq = _pick(S, (512, 256))

    def body(q_ref, k_ref, v_ref, gq_ref, gk_ref, o_ref):
        qn = _xq_norm(q_ref[...], gq_ref[...]).astype(BF16)
        kn = _rms(k_ref[...], gk_ref[...]).astype(BF16)
        s = _mxu(qn, kn, _NT)
        m = jnp.max(s, axis=-1, keepdims=True)
        p = jnp.exp(s - m)
        l = jnp.sum(p, axis=-1, keepdims=True)
        o_ref[...] = (_mxu(p.astype(BF16), v_ref[...].astype(BF16)) / l).astype(o_ref.dtype)

    vec = pl.BlockSpec((1, Dh), lambda h, i: (0, 0))
    return pl.pallas_call(
        body, grid=(XATTN_HEADS, S // tq),
        in_specs=[pl.BlockSpec((tq, Dh), lambda h, i: (i, h)), pl.BlockSpec((Mm, Dh), lambda h, i: (0, h)),
                  pl.BlockSpec((Mm, Dh), lambda h, i: (0, XATTN_HEADS + h)), vec, vec],
        out_specs=pl.BlockSpec((tq, Dh), lambda h, i: (i, h)),
        out_shape=jax.ShapeDtypeStruct((S, XATTN_HEADS * Dh), BF16),
        compiler_params=_params(("parallel", "parallel")), name="xattn_fwd")(xq, kv, kv, gq, gk)


def _xattn_bwd(xq, kv, gq, gk, do):
    S = xq.shape[0]
    Mm = kv.shape[0]
    Dh = XATTN_DIM
    tq = _pick(S, (512, 256))
    nq = S // tq

    def body(q_ref, k_ref, v_ref, gq_ref, gk_ref, do_ref, dq_ref, dk_ref, dv_ref, dgq_ref, dgk_ref, dkn_acc, dv_acc):
        h = pl.program_id(0)
        i = pl.program_id(1)

        @pl.when((h == 0) & (i == 0))
        def _():
            dgq_ref[...] = jnp.zeros_like(dgq_ref)
            dgk_ref[...] = jnp.zeros_like(dgk_ref)

        @pl.when(i == 0)
        def _():
            dkn_acc[...] = jnp.zeros_like(dkn_acc)
            dv_acc[...] = jnp.zeros_like(dv_acc)

        qn32, vq = jax.vjp(_xq_norm, q_ref[...], gq_ref[...])
        kn32, vk = jax.vjp(_rms, k_ref[...], gk_ref[...])
        qn = qn32.astype(BF16)
        kn = kn32.astype(BF16)
        vb = v_ref[...].astype(BF16)
        s = _mxu(qn, kn, _NT)
        m = jnp.max(s, axis=-1, keepdims=True)
        p = jnp.exp(s - m)
        p = p / jnp.sum(p, axis=-1, keepdims=True)
        dob = do_ref[...].astype(BF16)
        dp = _mxu(dob, vb, _NT)
        delta = jnp.sum(p * dp, axis=-1, keepdims=True)
        ds = (p * (dp - delta)).astype(BF16)
        dv_acc[...] += _mxu(p.astype(BF16), dob, _TN)
        dkn_acc[...] += _mxu(ds, qn, _TN)
        dq, dgq = vq(_mxu(ds, kn))
        dq_ref[...] = dq.astype(dq_ref.dtype)
        dgq_ref[...] += dgq

        @pl.when(i == nq - 1)
        def _():
            dk, dgk = vk(dkn_acc[...])
            dk_ref[...] = dk.astype(dk_ref.dtype)
            dv_ref[...] = dv_acc[...].astype(dv_ref.dtype)
            dgk_ref[...] += dgk

    vec = pl.BlockSpec((1, Dh), lambda h, i: (0, 0))
    qblk = pl.BlockSpec((tq, Dh), lambda h, i: (i, h))
    kblk = pl.BlockSpec((Mm, Dh), lambda h, i: (0, h))
    vblk = pl.BlockSpec((Mm, Dh), lambda h, i: (0, XATTN_HEADS + h))
    return pl.pallas_call(
        body, grid=(XATTN_HEADS, nq),
        in_specs=[qblk, kblk, vblk, vec, vec, qblk],
        out_specs=[qblk, kblk, kblk, vec, vec],
        out_shape=[jax.ShapeDtypeStruct((S, XATTN_HEADS * Dh), BF16),
                   jax.ShapeDtypeStruct((Mm, XATTN_HEADS * Dh), BF16),
                   jax.ShapeDtypeStruct((Mm, XATTN_HEADS * Dh), BF16),
                   jax.ShapeDtypeStruct((1, Dh), F32), jax.ShapeDtypeStruct((1, Dh), F32)],
        scratch_shapes=[pltpu.VMEM((Mm, Dh), F32), pltpu.VMEM((Mm, Dh), F32)],
        compiler_params=_params(("arbitrary", "arbitrary")), name="xattn_bwd")(xq, kv, kv, gq, gk, do)


def _loss_head(y, target):
    S, D = y.shape
    tr = _pick(S, (512, 256))

    def body(y_ref, t_ref, dy_ref, loss_ref):
        @pl.when(pl.program_id(0) == 0)
        def _():
            loss_ref[...] = jnp.zeros_like(loss_ref)

        err = y_ref[...] - t_ref[...]
        dy_ref[...] = err * (1.0 / D)
        loss_ref[...] += jnp.sum(err * err) * (0.5 / D)

    row = pl.BlockSpec((tr, D), lambda i: (i, 0))
    return pl.pallas_call(
        body, grid=(S // tr,), in_specs=[row, row],
        out_specs=[row, pl.BlockSpec((1, LANES), lambda i: (0, 0))],
        out_shape=[jax.ShapeDtypeStruct((S, D), F32), jax.ShapeDtypeStruct((1, LANES), F32)],
        compiler_params=_params(("arbitrary",)), name="loss_head")(y, target)


def _row_tile(R, C):
    for tr in (1024, 512, 256, 128, 64, 32, 16, 8):
        if R % tr == 0 and tr * C * 4 <= (1 << 20):
            return tr
    return R


def _nsum(arrs, out_dtypes, name):
    R, C = arrs[0].shape
    tr = _row_tile(R, C)
    n = len(arrs)

    def body(*refs):
        acc = refs[0][...].astype(F32)
        for r in refs[1:n]:
            acc = acc + r[...].astype(F32)
        for o in refs[n:]:
            o[...] = acc.astype(o.dtype)

    blk = pl.BlockSpec((tr, C), lambda i: (i, 0))
    outs = pl.pallas_call(
        body, grid=(R // tr,), in_specs=[blk] * n, out_specs=[blk] * len(out_dtypes),
        out_shape=[jax.ShapeDtypeStruct((R, C), d) for d in out_dtypes],
        compiler_params=_params(("parallel",)), name=name)(*arrs)
    return outs


def _adamw(w, g, m, v, name):
    R, C = w.shape
    tr = _row_tile(R, C)
    c1 = 1.0 - ADAM_B1 ** ADAM_STEP
    c2 = 1.0 - ADAM_B2 ** ADAM_STEP

    def body(w_ref, g_ref, m_ref, v_ref, d_ref, mo_ref, vo_ref):
        g_t = g_ref[...]
        m_new = ADAM_B1 * m_ref[...] + (1.0 - ADAM_B1) * g_t
        v_new = ADAM_B2 * v_ref[...] + (1.0 - ADAM_B2) * (g_t * g_t)
        d_ref[...] = -ADAM_LR * ((m_new / c1) / (jnp.sqrt(v_new / c2) + ADAM_EPS) + ADAM_WD * w_ref[...])
        mo_ref[...] = m_new
        vo_ref[...] = v_new

    blk = pl.BlockSpec((tr, C), lambda i: (i, 0))
    return pl.pallas_call(
        body, grid=(R // tr,), in_specs=[blk] * 4, out_specs=[blk] * 3,
        out_shape=[jax.ShapeDtypeStruct((R, C), F32)] * 3,
        compiler_params=_params(("parallel",)), name=name)(w, g, m, v)


D_MODEL = 1024
SSM_INNER = SSM_HEADS * HEAD_DIM
CONV_DIM = SSM_INNER + 2 * SSM_GROUPS * SSM_STATE
ATTN_WIDTH = ATTN_HEADS * HEAD_DIM
COL_Z = 0
COL_XBC = COL_Z + SSM_INNER
COL_Q = COL_XBC + CONV_DIM
COL_K = COL_Q + ATTN_WIDTH
COL_V = COL_K + ATTN_WIDTH
COL_DT = COL_V + ATTN_WIDTH
COL_F = COL_DT + SSM_HEADS
IN_COLS = COL_F + ATTN_HEADS
IN_COLS_PAD = -(-IN_COLS // LANES) * LANES
REF_COL_DT = COL_Q


def _to_kernel_cols(w):
    return jnp.concatenate([w[:, :REF_COL_DT], w[:, REF_COL_DT + SSM_HEADS:COL_F], w[:, REF_COL_DT:REF_COL_DT + SSM_HEADS],
                            w[:, COL_F:IN_COLS]], axis=1)


def _to_reference_cols(w):
    return jnp.concatenate([w[:, :COL_Q], w[:, COL_DT:COL_DT + SSM_HEADS], w[:, COL_Q:COL_DT], w[:, COL_F:IN_COLS]],
                           axis=1)


def _add_residual(acc, res):
    return (res + acc,)


def _relu2(acc):
    r = jnp.maximum(acc, 0.0)
    return acc, r * r


def _relu2_bwd(acc, a):
    return (acc * (2.0 * jnp.maximum(a, 0.0)),)


def _layer_fwd_bwd(x, mem, target, w_in, p, late_weights, send_late_grads):
    S = x.shape[0]
    hd3 = lambda a: a.reshape(SSM_HEADS, 1, 1)

    h1 = _rmsnorm_fwd(x, p["g_mix"], "norm_mix")
    proj = _mm(h1, w_in, "nn", "in_proj")
    xbc = _conv_fwd(proj, COL_XBC, CONV_DIM, p["conv_w"], p["conv_b"])
    dt_hm = proj[:, COL_DT:COL_DT + SSM_HEADS].T[:, :, None]
    ssd_par = (hd3(p["dt_bias"]), hd3(p["a_log"]), hd3(p["d_skip"]), p["ssm_norm_w"])
    y, hs = _ssd_fwd(xbc, proj, dt_hm, *ssd_par)
    f_raw = proj[:, COL_F:COL_F + ATTN_HEADS]
    gq2 = jnp.tile(p["g_q"], (1, 2))
    gk2 = jnp.tile(p["g_k"], (1, 2))
    qs, kn, vb = _qk_prep_fwd(proj, gq2, gk2)
    cum, cq = _logf_cumsum_fwd(f_raw, p["f_bias"])
    ck = cum.T[:, None, :]
    o, o_fine, lse = _flash_fwd(qs, kn, vb, cq, ck)
    W = late_weights((o_fine, y))
    x1 = _mm(y, W["w_out"][:SSM_INNER], "nn", "out_proj_ssm", epilogue=_add_residual, extras=(x,))
    x1 = _mm(o, W["w_out"][SSM_INNER:], "nn", "out_proj_attn", epilogue=_add_residual, extras=(x1,))
    h2 = _rmsnorm_fwd(x1, p["g_xattn"], "norm_xattn")
    mem_n = _rmsnorm_fwd(mem, p["g_mem"], "norm_mem")
    xq = _mm(h2, W["xq_w"], "nn", "xq_proj")
    kv = _mm(mem_n, W["xkv_w"], "nn", "xkv_proj", b_chunks=N_CHIPS)
    xo = _xattn_fwd(xq, kv, p["xg_q"], p["xg_k"])
    x2 = _mm(xo, W["xo_w"], "nn", "xo_proj", epilogue=_add_residual, extras=(x1,))
    h3 = _rmsnorm_fwd(x2, p["g_mlp"], "norm_mlp")
    a, act = _mm(h3, W["w_up"], "nn", "mlp_up", out_dtypes=(F32, BF16), epilogue=_relu2, b_chunks=N_CHIPS)
    x3 = _mm(act, W["w_down"], "nn", "mlp_down", epilogue=_add_residual, extras=(x2,))
    dy, loss_row = _loss_head(x3, target)

    gW, gp = {}, {}
    da = _mm(dy, W["w_down"], "nt", "d_act", out_dtypes=(BF16,), epilogue=_relu2_bwd, extras=(a,))
    gW["w_down"] = _mm(act, dy, "tn", "g_w_down", out_dtypes=(BF16,))
    gW["w_up"] = _mm(h3, da, "tn", "g_w_up", out_dtypes=(BF16,), out_chunks=N_CHIPS)
    dh3 = _mm(da, W["w_up"], "nt", "d_h3", b_chunks=N_CHIPS)
    dx2, gp["g_mlp"] = _rmsnorm_bwd(x2, p["g_mlp"], dh3, dy, "norm_mlp_bwd")
    dxo = _mm(dx2, W["xo_w"], "nt", "d_xo", out_dtypes=(BF16,))
    gW["xo_w"] = _mm(xo, dx2, "tn", "g_xo_w", out_dtypes=(BF16,))
    dxq, dk_x, dv_x, gp["xg_q"], gp["xg_k"] = _xattn_bwd(xq, kv, p["xg_q"], p["xg_k"], dxo)
    dkv = jnp.concatenate([dk_x, dv_x], axis=-1)
    gW["xq_w"] = _mm(h2, dxq, "tn", "g_xq_w", out_dtypes=(BF16,))
    dh2 = _mm(dxq, W["xq_w"], "nt", "d_h2")
    gW["xkv_w"] = _mm(mem_n, dkv, "tn", "g_xkv_w", out_dtypes=(BF16,), out_chunks=N_CHIPS)
    dmem_n = _mm(dkv, W["xkv_w"], "nt", "d_mem_n", b_chunks=N_CHIPS)
    _, gp["g_mem"] = _rmsnorm_bwd(mem, p["g_mem"], dmem_n, None, "norm_mem_bwd")
    dx1, gp["g_xattn"] = _rmsnorm_bwd(x1, p["g_xattn"], dh2, dx2, "norm_xattn_bwd")
    dmixed = _mm(dx1, W["w_out"], "nt", "d_mixed")
    gW["w_out"] = jnp.concatenate([_mm(y, dx1, "tn", "g_w_out_ssm", out_dtypes=(BF16,)),
                                   _mm(o, dx1, "tn", "g_w_out_attn", out_dtypes=(BF16,))], axis=0)
    token = send_late_grads(gW)
    dqs, dkn, dproj, dck = _flash_bwd(qs, kn, vb, cq, ck + token[:1, :1], o_fine, dmixed, SSM_INNER, lse)
    dproj, dg2 = _qk_prep_bwd(proj, jnp.stack([gq2, gk2]), dqs, dkn, dproj)
    gp["g_q"] = dg2[0, :, :HEAD_DIM] + dg2[0, :, HEAD_DIM:]
    gp["g_k"] = dg2[1, :, :HEAD_DIM] + dg2[1, :, HEAD_DIM:]
    df, gp["f_bias"] = _logf_cumsum_bwd(f_raw, p["f_bias"], dck[:, 0, :].T)
    dxs, dproj, dB, dC, ddt, ddtb, dalog, ddsk, gp["ssm_norm_w"] = _ssd_bwd(xbc, proj, dt_hm, *ssd_par, hs, dmixed, dproj)
    gp["dt_bias"] = ddtb.reshape(1, SSM_HEADS)
    gp["a_log"] = dalog.reshape(1, SSM_HEADS)
    gp["d_skip"] = ddsk.reshape(1, SSM_HEADS)
    dproj, dconv_w, gp["conv_b"] = _conv_bwd(proj, COL_XBC, CONV_DIM, p["conv_w"], p["conv_b"], (dxs, dB, dC), dproj)
    gp["conv_w"] = dconv_w[:CONV_WIDTH]
    tail = jnp.concatenate([ddt[:, :, 0].T, df, jnp.zeros((S, IN_COLS_PAD - IN_COLS), F32)], axis=-1).astype(BF16)
    dproj = lax.dynamic_update_slice(dproj, tail, (0, COL_DT))
    g_w_in = _mm(h1, dproj, "tn", "g_w_in", out_dtypes=(BF16,))
    dh1 = _mm(dproj, w_in, "nt", "d_h1")
    dx, gp["g_mix"] = _rmsnorm_bwd(x, p["g_mix"], dh1, dx1, "norm_mix_bwd")
    return loss_row, dx, g_w_in, gp


_ANY = pl.BlockSpec(memory_space=pl.ANY)


def _place():
    x, y, c = lax.axis_index("x"), lax.axis_index("y"), lax.axis_index("c")
    chips = [(1 - x, y), (x, 1 - y), (1 - x, 1 - y)]
    return x, y, c, chips


def _chip_index(px, py):
    return 2 * px + py


def _all_gather_chips(split, whole):
    ns, nw = len(split), len(whole)
    n = ns + nw

    def body(*refs):
        ins, outs = refs[:n], refs[n:2 * n]
        send_ici, recv_ici, send_d2d, recv_d2d = refs[2 * n:]
        x, y, c, chips = _place()
        me = _chip_index(x, y)
        sib = (x, y, 1 - c)

        def ici(k, j, src, dst):
            return pltpu.make_async_remote_copy(src_ref=src, dst_ref=dst, send_sem=send_ici.at[3 * k + j],
                                                recv_sem=recv_ici.at[3 * k + j], device_id=(*chips[j], c),
                                                device_id_type=MESH)

        def d2d(k, j, piece):
            return pltpu.make_async_remote_copy(src_ref=piece, dst_ref=piece, send_sem=send_d2d.at[3 * k + j],
                                                recv_sem=recv_d2d.at[3 * k + j], device_id=sib, device_id_type=MESH)

        sends = []
        for k in range(n):
            for j in range(3):
                if k < ns:
                    sends.append(ici(k, j, ins[k].at[c], outs[k].at[me, c]))
                else:
                    sends.append(ici(k, j, ins[k], outs[k].at[me]))
                sends[-1].start()
        passed = []
        for k in range(n):
            for j in range(3):
                src_chip = _chip_index(*chips[j])
                if k < ns:
                    ici(k, j, ins[k].at[c], outs[k].at[src_chip, c]).wait_recv()
                    passed.append(d2d(k, j, outs[k].at[src_chip, c]))
                    passed[-1].start()
                else:
                    ici(k, j, ins[k], outs[k].at[src_chip]).wait_recv()
        for k in range(ns):
            for j in range(3):
                d2d(k, j, outs[k].at[_chip_index(*chips[j]), 1 - c]).wait_recv()
        for cp in sends + passed:
            cp.wait_send()

    arrs = list(split) + list(whole)
    return pl.pallas_call(
        body, in_specs=[_ANY] * n, out_specs=[_ANY] * n,
        out_shape=[jax.ShapeDtypeStruct((N_CHIPS,) + a.shape, a.dtype) for a in arrs],
        scratch_shapes=[pltpu.SemaphoreType.DMA((3 * n,)), pltpu.SemaphoreType.DMA((3 * n,)),
                        pltpu.SemaphoreType.DMA((3 * ns,)), pltpu.SemaphoreType.DMA((3 * ns,))],
        name="all_gather_chips")(*arrs)


def _sibling_send_halves(grads, name):
    n = len(grads)

    def body(*refs):
        ins, outs = refs[:n], refs[n:2 * n]
        send_sem, recv_sem = refs[2 * n:]
        x, y, c, _ = _place()

        def cp(k, j, half):
            return pltpu.make_async_remote_copy(src_ref=ins[k].at[j, half], dst_ref=outs[k].at[j],
                                                send_sem=send_sem.at[N_CHIPS * k + j],
                                                recv_sem=recv_sem.at[N_CHIPS * k + j],
                                                device_id=(x, y, 1 - c), device_id_type=MESH)

        copies = [cp(k, j, 1 - c) for k in range(n) for j in range(N_CHIPS)]
        for q in copies:
            q.start()
        for q in copies:
            q.wait()

    return pl.pallas_call(
        body, in_specs=[_ANY] * n, out_specs=[_ANY] * n,
        out_shape=[jax.ShapeDtypeStruct((N_CHIPS,) + g.shape[2:], g.dtype) for g in grads],
        scratch_shapes=[pltpu.SemaphoreType.DMA((N_CHIPS * n,)), pltpu.SemaphoreType.DMA((N_CHIPS * n,))],
        name=name)(*grads)


def _sibling_exchange(halves, name):
    n = len(halves)

    def body(*refs):
        ins, outs = refs[:n], refs[n:2 * n]
        send_sem, recv_sem = refs[2 * n:]
        x, y, c, _ = _place()

        def cp(k, half):
            return pltpu.make_async_remote_copy(src_ref=ins[k], dst_ref=outs[k].at[half], send_sem=send_sem.at[k],
                                                recv_sem=recv_sem.at[k], device_id=(x, y, 1 - c), device_id_type=MESH)

        sends = [cp(k, c) for k in range(n)]
        for q in sends:
            q.start()
        for k in range(n):
            cp(k, 1 - c).wait_recv()
        for q in sends:
            q.wait_send()

    return pl.pallas_call(
        body, in_specs=[_ANY] * n, out_specs=[_ANY] * n,
        out_shape=[jax.ShapeDtypeStruct((2,) + h.shape, h.dtype) for h in halves],
        scratch_shapes=[pltpu.SemaphoreType.DMA((n,)), pltpu.SemaphoreType.DMA((n,))],
        name=name)(*halves)


_HBM = pl.BlockSpec(memory_space=pltpu.HBM)
_SEM = pl.BlockSpec(memory_space=pltpu.SEMAPHORE)
_SPLIT_EFFECT = pltpu.SideEffectType.DATAFLOW_SIDE_EFFECTING


class _Split(NamedTuple):
    send_sems: jax.Array
    recv_sems: jax.Array
    sources: tuple
    lands: tuple
    token: jax.Array


def _split_copies(kind, srcs, lands, send_sems, recv_sems):
    x, y, c, chips = _place()
    me = _chip_index(x, y)
    copies = []
    for k in range(len(srcs)):
        for j in range(3):
            if kind == "gather":
                src, dst = srcs[k], lands[k].at[me]
            else:
                src, dst = srcs[k].at[_chip_index(*chips[j])], lands[k].at[j]
            copies.append(pltpu.make_async_remote_copy(
                src_ref=src, dst_ref=dst, send_sem=send_sems.at[3 * k + j], recv_sem=recv_sems.at[3 * k + j],
                device_id=(*chips[j], c), device_id_type=MESH))
    return copies


def _split_start(name, sources, kind, after):
    n = len(sources)
    if kind == "gather":
        lands = [lax.empty((N_CHIPS,) + s.shape, s.dtype) for s in sources]
    else:
        lands = [lax.empty((3,) + s.shape[1:], s.dtype) for s in sources]
    deps = [] if after is None else [after]

    def body(*refs):
        srcs, lnds = refs[:n], refs[n:2 * n]
        send_sems, recv_sems = refs[2 * n + len(deps)], refs[2 * n + len(deps) + 1]
        for cp in _split_copies(kind, srcs, lnds, send_sems, recv_sems):
            cp.start()
        refs[-1][...] = jnp.zeros_like(refs[-1])

    hbm = lambda a: pltpu.with_memory_space_constraint(a, pltpu.HBM)
    outs = pl.pallas_call(
        body, name=name,
        in_specs=[_HBM] * (2 * n) + [_ANY] * len(deps),
        out_specs=[_SEM, _SEM] + [_HBM] * (2 * n) + [pl.BlockSpec(memory_space=pltpu.VMEM)],
        out_shape=[pltpu.SemaphoreType.DMA((3 * n,)), pltpu.SemaphoreType.DMA((3 * n,))]
        + [pltpu.HBM(a.shape, a.dtype) for a in list(sources) + lands] + [jax.ShapeDtypeStruct((8, LANES), F32)],
        input_output_aliases={k: 2 + k for k in range(2 * n)},
        compiler_params=pltpu.CompilerParams(has_side_effects=_SPLIT_EFFECT),
    )(*[hbm(s) for s in sources], *[hbm(l) for l in lands], *deps)
    return _Split(outs[0], outs[1], tuple(outs[2:2 + n]), tuple(outs[2 + n:2 + 2 * n]), outs[-1])


def _split_wait(name, h, kind, after):
    n = len(h.sources)

    def body(*refs):
        srcs, lnds = refs[:n], refs[n:2 * n]
        for cp in _split_copies(kind, srcs, lnds, refs[2 * n], refs[2 * n + 1]):
            cp.wait_send()
            cp.wait_recv()

    outs = pl.pallas_call(
        body, name=name,
        in_specs=[_HBM] * (2 * n) + [_SEM, _SEM] + [_ANY] * len(after),
        out_specs=[_HBM] * (2 * n),
        out_shape=[pltpu.HBM(a.shape, a.dtype) for a in h.sources + h.lands],
        input_output_aliases={k: k for k in range(2 * n)},
        compiler_params=pltpu.CompilerParams(has_side_effects=_SPLIT_EFFECT),
    )(*h.sources, *h.lands, h.send_sems, h.recv_sems, *after)
    return outs[:n], outs[n:]


def _all_reduce_small(vec):
    R = vec.shape[0]

    def body(v_ref, o_ref, buf, send_sem, recv_sem):
        x, y, c = lax.axis_index("x"), lax.axis_index("y"), lax.axis_index("c")
        me = 4 * x + 2 * y + c
        buf[me] = v_ref[...]
        copies = []
        for r in range(1, N_DEV):
            fx, fy, fc = (r >> 2) & 1, (r >> 1) & 1, r & 1
            peer = (x ^ fx, y ^ fy, c ^ fc)
            copies.append(pltpu.make_async_remote_copy(src_ref=v_ref, dst_ref=buf.at[me], send_sem=send_sem.at[r - 1],
                                                       recv_sem=recv_sem.at[r - 1], device_id=peer, device_id_type=MESH))
        for q in copies:
            q.start()
        for r in range(1, N_DEV):
            fx, fy, fc = (r >> 2) & 1, (r >> 1) & 1, r & 1
            src = 4 * (x ^ fx) + 2 * (y ^ fy) + (c ^ fc)
            pltpu.make_async_remote_copy(src_ref=v_ref, dst_ref=buf.at[src], send_sem=send_sem.at[r - 1],
                                         recv_sem=recv_sem.at[r - 1], device_id=(x, y, c), device_id_type=MESH).wait_recv()
        acc = buf[0]
        for d in range(1, N_DEV):
            acc = acc + buf[d]
        o_ref[...] = acc
        for q in copies:
            q.wait_send()

    vm = pl.BlockSpec(memory_space=pltpu.VMEM)
    return pl.pallas_call(
        body, in_specs=[vm], out_specs=vm, out_shape=jax.ShapeDtypeStruct((R, LANES), F32),
        scratch_shapes=[pltpu.VMEM((N_DEV, R, LANES), F32), pltpu.SemaphoreType.DMA((N_DEV - 1,)),
                        pltpu.SemaphoreType.DMA((N_DEV - 1,))],
        name="all_reduce_small")(vec)


_INPUTS = ["x", "mem", "g_mix", "w_in", "conv_w", "conv_b", "dt_bias", "a_log", "d_skip", "ssm_norm_w", "g_q", "g_k",
           "f_bias", "w_out", "g_xattn", "g_mem", "xq_w", "xkv_w", "xg_q", "xg_k", "xo_w", "g_mlp", "w_up", "w_down"]
_WEIGHTS = _INPUTS[2:]
_BIG = ["w_in", "w_out", "xq_w", "xkv_w", "xo_w", "w_up", "w_down"]
_LATE = _BIG[1:]
_COL_SHARDED = ["w_in", "xkv_w", "w_up"]
_SMALL = [n for n in _WEIGHTS if n not in _BIG]


def _pack_rows(arrs):
    rows = []
    for a in arrs:
        flat = a.reshape(-1)
        pad = -flat.shape[0] % LANES
        rows.append(jnp.pad(flat, (0, pad)).reshape(-1, LANES))
    out = jnp.concatenate(rows, axis=0)
    return jnp.pad(out, ((0, -out.shape[0] % 8), (0, 0)))


def _unpack_rows(packed, shapes):
    out, r = [], 0
    for s in shapes:
        n = math.prod(s)
        nr = -(-n // LANES)
        out.append(packed[r:r + nr].reshape(-1)[:n].reshape(s))
        r += nr
    return out


def kernel(x, mem, g_mix, w_in, conv_w, conv_b, dt_bias, a_log, d_skip, ssm_norm_w, g_q, g_k, f_bias, w_out, g_xattn, g_mem, xq_w, xkv_w, xg_q, xg_k, xo_w, g_mlp, w_up, w_down, loss_target, m_g_mix, m_w_in, m_conv_w, m_conv_b, m_dt_bias, m_a_log, m_d_skip, m_ssm_norm_w, m_g_q, m_g_k, m_f_bias, m_w_out, m_g_xattn, m_g_mem, m_xq_w, m_xkv_w, m_xg_q, m_xg_k, m_xo_w, m_g_mlp, m_w_up, m_w_down, v_g_mix, v_w_in, v_conv_w, v_conv_b, v_dt_bias, v_a_log, v_d_skip, v_ssm_norm_w, v_g_q, v_g_k, v_f_bias, v_w_out, v_g_xattn, v_g_mem, v_xq_w, v_xkv_w, v_xg_q, v_xg_k, v_xo_w, v_g_mlp, v_w_up, v_w_down):
    args = (x, mem, g_mix, w_in, conv_w, conv_b, dt_bias, a_log, d_skip, ssm_norm_w, g_q, g_k, f_bias, w_out, g_xattn,
            g_mem, xq_w, xkv_w, xg_q, xg_k, xo_w, g_mlp, w_up, w_down)
    w = dict(zip(_INPUTS, args))
    mom1 = dict(zip(_WEIGHTS, (m_g_mix, m_w_in, m_conv_w, m_conv_b, m_dt_bias, m_a_log, m_d_skip, m_ssm_norm_w, m_g_q,
                               m_g_k, m_f_bias, m_w_out, m_g_xattn, m_g_mem, m_xq_w, m_xkv_w, m_xg_q, m_xg_k, m_xo_w,
                               m_g_mlp, m_w_up, m_w_down)))
    mom2 = dict(zip(_WEIGHTS, (v_g_mix, v_w_in, v_conv_w, v_conv_b, v_dt_bias, v_a_log, v_d_skip, v_ssm_norm_w, v_g_q,
                               v_g_k, v_f_bias, v_w_out, v_g_xattn, v_g_mem, v_xq_w, v_xkv_w, v_xg_q, v_xg_k, v_xo_w,
                               v_g_mlp, v_w_up, v_w_down)))
    chip = _chip_index(lax.axis_index("x"), lax.axis_index("y"))
    core = lax.axis_index("c")

    shard_bf = {n: w[n][0].astype(BF16) for n in _BIG}

    def layout_for_compute(n, g):
        if n == "w_in":
            g = _to_kernel_cols(g.transpose(1, 0, 2).reshape(g.shape[1], IN_COLS))
            return jnp.pad(g, ((0, 0), (0, IN_COLS_PAD - IN_COLS)))
        return g if n in _COL_SHARDED else g.reshape(N_CHIPS * g.shape[1], g.shape[2])

    def layout_for_reduction(n, g):
        if n == "w_in":
            g = _to_reference_cols(g).reshape(g.shape[0], N_CHIPS, IN_COLS // N_CHIPS).transpose(1, 0, 2)
        elif n not in _COL_SHARDED:
            g = g.reshape(N_CHIPS, g.shape[0] // N_CHIPS, g.shape[1])
        return g.reshape(N_CHIPS, 2, g.shape[1] // 2, g.shape[2])

    def pair_sums_of(names, grads, tag):
        grads4 = [layout_for_reduction(n, grads[n]) for n in names]
        from_sibling = _sibling_send_halves(grads4, "rs_sibling_halves_" + tag)
        sums = []
        for n, g, fs in zip(names, grads4, from_sibling):
            mine = lax.dynamic_index_in_dim(g, core, axis=1, keepdims=False)
            flat = lambda a: a.reshape(-1, a.shape[-1])
            (s,) = _nsum([flat(mine), flat(fs)], (BF16,), "rs_pair_sum_" + n)
            sums.append(s.reshape(mine.shape))
        return sums

    def chip_sums_of(names, pair_sums, from_chips):
        out = []
        for n, ps, fc in zip(names, pair_sums, from_chips):
            own = lax.dynamic_index_in_dim(ps, chip, axis=0, keepdims=False)
            (r,) = _nsum([own, fc[0], fc[1], fc[2]], (F32,), "rs_chip_sum_" + n)
            out.append(r)
        return out

    halves_in = shard_bf["w_in"].reshape(2, shard_bf["w_in"].shape[0] // 2, -1)
    g_in, g_conv = _all_gather_chips([halves_in], [w["conv_w"][0]])
    g_in = lax.dynamic_update_index_in_dim(g_in, halves_in, chip, axis=0)
    g_conv = lax.dynamic_update_index_in_dim(g_conv, w["conv_w"][0], chip, axis=0)
    w_in_full = layout_for_compute("w_in", g_in.reshape(N_CHIPS, -1, g_in.shape[-1]))
    p = {n: w[n] for n in _SMALL}
    p["conv_w"] = g_conv.transpose(1, 0, 2).reshape(CONV_WIDTH, CONV_DIM)
    gather = _split_start("gather_late", [shard_bf[n] for n in _LATE], "gather", after=g_in)
    p["g_mix"] = p["g_mix"] + gather.token[:1, :1]

    def late_weights(after):
        srcs, lands = _split_wait("gather_late_wait", gather, "gather", after)
        lands = [lax.dynamic_update_index_in_dim(l, s, chip, axis=0) for l, s in zip(lands, srcs)]
        return {n: layout_for_compute(n, l) for n, l in zip(_LATE, lands)}

    scatter = {}

    def send_late_grads(grads):
        sums = pair_sums_of(_LATE, grads, "late")
        scatter["h"] = _split_start("scatter_late", sums, "scatter", after=None)
        return scatter["h"].token

    loss_row, dx, g_w_in, gp = _layer_fwd_bwd(x[0], mem[0], loss_target[0], w_in_full, p, late_weights, send_late_grads)

    sums_in = pair_sums_of(["w_in"], {"w_in": g_w_in}, "w_in")
    scatter_in = _split_start("scatter_w_in", sums_in, "scatter", after=None)

    grad, delta, new_m, new_v = {}, {}, {}, {}

    def finish(names, pair_sums, from_chips, tag):
        reduced = chip_sums_of(names, pair_sums, from_chips)
        for n, g, r in zip(names, _sibling_exchange(reduced, "rs_sibling_exchange_" + tag), reduced):
            shape = w[n].shape
            g2 = lax.dynamic_update_index_in_dim(g, r, core, axis=0).reshape(shape[1], shape[2])
            d, m1, v1 = _adamw(w[n][0], g2, mom1[n][0], mom2[n][0], "adamw_" + n)
            grad[n], delta[n], new_m[n], new_v[n] = (a.reshape(shape) for a in (g2, d, m1, v1))

    sums_late, from_chips_late = _split_wait("scatter_late_wait", scatter["h"], "scatter", (dx, scatter_in.token))
    finish(_LATE, sums_late, from_chips_late, "late")

    small_shapes = [gp[n].shape for n in _SMALL] + [(1, LANES)]
    packed = _pack_rows([gp[n] for n in _SMALL] + [loss_row])
    summed = _unpack_rows(_all_reduce_small(packed), small_shapes)
    gsmall = dict(zip(_SMALL, summed[:-1]))
    loss = summed[-1][0, 0]
    shard_cols = CONV_DIM // N_CHIPS
    gsmall["conv_w"] = lax.dynamic_slice_in_dim(gsmall["conv_w"], chip * shard_cols, shard_cols, axis=1)

    sums_in, from_chips_in = _split_wait("scatter_w_in_wait", scatter_in, "scatter",
                                         (summed[-1], *[new_v[n] for n in _LATE]))
    finish(["w_in"], sums_in, from_chips_in, "w_in")

    pk = lambda src: _pack_rows([src[n] for n in _SMALL])
    for n in _SMALL:
        gsmall[n] = gsmall[n].reshape(w[n].shape)
    d, m1, v1 = _adamw(pk(w), pk(gsmall), pk(mom1), pk(mom2), "adamw_small")
    shapes = [w[n].shape for n in _SMALL]
    for n, dn, mn, vn in zip(_SMALL, _unpack_rows(d, shapes), _unpack_rows(m1, shapes), _unpack_rows(v1, shapes)):
        grad[n], delta[n], new_m[n], new_v[n] = gsmall[n], dn, mn, vn

    return (loss, dx[None], *[grad[n] for n in _WEIGHTS], *[delta[n] for n in _WEIGHTS],
            *[new_m[n] for n in _WEIGHTS], *[new_v[n] for n in _WEIGHTS])
```

```python
from typing import NamedTuple

import jax
import jax.numpy as jnp
from jax import lax
from jax.experimental import pallas as pl
from jax.experimental.pallas import tpu as pltpu

F32 = jnp.float32
BF16 = jnp.bfloat16
HI = lax.Precision.HIGHEST
MESH = pl.DeviceIdType.MESH

EPS = 1e-5
CHUNK = 128
SSM_HEADS = 16
SSM_GROUPS = 2
HEADS_PER_GROUP = SSM_HEADS // SSM_GROUPS
HEAD_DIM = 64
SSM_STATE = 128
ATTN_HEADS = 16
XATTN_HEADS = 4
XATTN_DIM = 256
CONV_WIDTH = 4
N_CHIPS = 4
N_DEV = 8
LANES = 128
VMEM_LIMIT = 56 * 1024 * 1024

ADAM_LR = 0.001
ADAM_B1 = 0.9
ADAM_B2 = 0.999
ADAM_EPS = 1e-08
ADAM_WD = 0.01
ADAM_STEP = 10


def _params(sem):
    return pltpu.CompilerParams(dimension_semantics=sem, vmem_limit_bytes=VMEM_LIMIT)


def _pick(n, cands):
    for c in cands:
        if n % c == 0:
            return c
    return n


def _mm(a, b, mode, name, out_dtypes=(F32,), epilogue=None, extras=(), b_chunks=1, out_chunks=1,
        tm=None, tn=None, tk=None):
    if mode == "nn":
        M, K = a.shape
        N = b.shape[-1] * b_chunks
    elif mode == "nt":
        M, K = a.shape
        N = b.shape[-2]
        assert b.shape[-1] * b_chunks == K
    else:
        K, M = a.shape
        N = b.shape[-1] * b_chunks
    tm = tm or _pick(M, (2048, 1024, 512, 256, 128))
    tn = tn or _pick(N // max(b_chunks if mode != "nt" else 1, out_chunks), (512, 640, 384, 256, 128))
    if tk is None:
        kmax = b.shape[-1] if mode == "nt" else K
        tk = kmax if kmax <= 2048 else _pick(kmax, (2048, 1152, 1024, 512))
    nk = K // tk
    assert M % tm == 0 and N % tn == 0 and K % tk == 0
    grid = (M // tm, N // tn, nk)

    if mode == "tn":
        a_spec = pl.BlockSpec((tk, tm), lambda i, j, k: (k, i))
    else:
        a_spec = pl.BlockSpec((tm, tk), lambda i, j, k: (i, k))

    def b_index(t_row, t_last, tile_last):
        if b_chunks == 1:
            return (t_row, t_last)
        q = (b.shape[-1]) // tile_last
        return (t_last // q, t_row, t_last % q)

    if mode == "nn" or mode == "tn":
        bshape = (tk, tn)
        bmap = lambda i, j, k: b_index(k, j, tn)
    else:
        bshape = (tn, tk)
        bmap = lambda i, j, k: b_index(j, k, tk)
    if b_chunks > 1:
        bshape = (None,) + bshape
    b_spec = pl.BlockSpec(bshape, bmap)

    if out_chunks == 1:
        o_spec = pl.BlockSpec((tm, tn), lambda i, j, k: (i, j))
        o_shape = (M, N)
    else:
        qo = (N // out_chunks) // tn
        o_spec = pl.BlockSpec((None, tm, tn), lambda i, j, k: (j // qo, i, j % qo))
        o_shape = (out_chunks, M, N // out_chunks)
    e_spec = pl.BlockSpec((tm, tn), lambda i, j, k: (i, j))

    dims = {"nn": (((1,), (0,)), ((), ())), "nt": (((1,), (1,)), ((), ())), "tn": (((0,), (0,)), ((), ()))}[mode]
    n_ex = len(extras)
    n_out = len(out_dtypes)

    def body(*refs):
        a_ref, b_ref = refs[0], refs[1]
        ex_refs = refs[2:2 + n_ex]
        o_refs = refs[2 + n_ex:2 + n_ex + n_out]

        def finish(acc):
            outs = epilogue(acc, *[r[...] for r in ex_refs]) if epilogue is not None else (acc,)
            for r, o in zip(o_refs, outs):
                r[...] = o.astype(r.dtype)

        part = lax.dot_general(a_ref[...].astype(BF16), b_ref[...].astype(BF16), dims,
                               preferred_element_type=F32)
        if nk == 1:
            finish(part)
        else:
            acc_ref = refs[-1]
            k = pl.program_id(2)

            @pl.when(k == 0)
            def _():
                acc_ref[...] = part

            @pl.when(k > 0)
            def _():
                acc_ref[...] += part

            @pl.when(k == nk - 1)
            def _():
                finish(acc_ref[...])

    outs = pl.pallas_call(
        body,
        grid=grid,
        in_specs=[a_spec, b_spec] + [e_spec] * n_ex,
        out_specs=[o_spec] * n_out,
        out_shape=[jax.ShapeDtypeStruct(o_shape, d) for d in out_dtypes],
        scratch_shapes=[pltpu.VMEM((tm, tn), F32)] if nk > 1 else [],
        compiler_params=_params(("parallel", "parallel", "arbitrary")),
        name=name,
    )(a, b, *extras)
    return outs[0] if n_out == 1 else outs


def _rms(x, g):
    r = lax.rsqrt(jnp.mean(x * x, axis=-1, keepdims=True) + EPS)
    return x * r * g


def _rmsnorm_fwd(x, g, name):
    R, D = x.shape
    tr = _pick(R, (512, 256))

    def body(x_ref, g_ref, o_ref):
        o_ref[...] = _rms(x_ref[...], g_ref[...]).astype(o_ref.dtype)

    return pl.pallas_call(
        body, grid=(R // tr,),
        in_specs=[pl.BlockSpec((tr, D), lambda i: (i, 0)), pl.BlockSpec((1, D), lambda i: (0, 0))],
        out_specs=pl.BlockSpec((tr, D), lambda i: (i, 0)),
        out_shape=jax.ShapeDtypeStruct((R, D), BF16),
        compiler_params=_params(("parallel",)), name=name)(x, g)


def _rmsnorm_bwd(x, g, dh, dres, name):
    R, D = x.shape
    tr = _pick(R, (256,))
    has_res = dres is not None

    def body(*refs):
        if has_res:
            x_ref, g_ref, dh_ref, dres_ref, dx_ref, dg_ref = refs
        else:
            x_ref, g_ref, dh_ref, dx_ref, dg_ref = refs
        _, vjp = jax.vjp(_rms, x_ref[...], g_ref[...])
        dx, dg = vjp(dh_ref[...])
        if has_res:
            dx = dx + dres_ref[...]
        dx_ref[...] = dx

        @pl.when(pl.program_id(0) == 0)
        def _():
            dg_ref[...] = jnp.zeros_like(dg_ref)

        dg_ref[...] += dg

    row = pl.BlockSpec((tr, D), lambda i: (i, 0))
    vec = pl.BlockSpec((1, D), lambda i: (0, 0))
    ins = [x, g, dh] + ([dres] if has_res else [])
    return pl.pallas_call(
        body, grid=(R // tr,),
        in_specs=[row, vec, row] + ([row] if has_res else []),
        out_specs=[row, vec],
        out_shape=[jax.ShapeDtypeStruct((R, D), F32), jax.ShapeDtypeStruct((1, D), F32)],
        compiler_params=_params(("arbitrary",)), name=name)(*ins)


def _shift_down(u, k):
    if k == 0:
        return u
    rows = lax.broadcasted_iota(jnp.int32, u.shape, 0)
    return jnp.where(rows >= k, pltpu.roll(u, k, axis=0), 0.0)


def _shift_up(u, k):
    if k == 0:
        return u
    n = u.shape[0]
    rows = lax.broadcasted_iota(jnp.int32, u.shape, 0)
    return jnp.where(rows < n - k, pltpu.roll(u, n - k, axis=0), 0.0)


def _conv_pre(u, w, b):
    pre = b
    for j in range(CONV_WIDTH):
        pre = pre + w[j:j + 1, :] * _shift_down(u, CONV_WIDTH - 1 - j)
    return pre


def _conv_fwd(proj, col0, ncols, conv_w, conv_b):
    S = proj.shape[0]
    cb0 = col0 // LANES

    def body(u_ref, w_ref, b_ref, o_ref):
        pre = _conv_pre(u_ref[...], w_ref[...], b_ref[...])
        o_ref[...] = pre * jax.nn.sigmoid(pre)

    return pl.pallas_call(
        body, grid=(ncols // LANES,),
        in_specs=[pl.BlockSpec((S, LANES), lambda j: (0, j + cb0)),
                  pl.BlockSpec((CONV_WIDTH, LANES), lambda j: (0, j)),
                  pl.BlockSpec((1, LANES), lambda j: (0, j))],
        out_specs=pl.BlockSpec((S, LANES), lambda j: (0, j)),
        out_shape=jax.ShapeDtypeStruct((S, ncols), F32),
        compiler_params=_params(("parallel",)), name="conv_fwd")(proj, conv_w, conv_b)


def _conv_bwd(proj, col0, ncols, conv_w, conv_b, douts, dproj):
    S = proj.shape[0]
    cb0 = col0 // LANES
    starts = [0]
    for d in douts:
        starts.append(starts[-1] + d.shape[1] // LANES)
    assert starts[-1] == ncols // LANES
    nd = len(douts)

    def body(u_ref, w_ref, b_ref, *rest):
        d_refs, (du_ref, dw_ref, db_ref) = rest[:nd], rest[nd + 1:]
        j = pl.program_id(0)
        dout = d_refs[-1][...]
        for i in range(nd - 2, -1, -1):
            dout = jnp.where(j < starts[i + 1], d_refs[i][...], dout)
        u = u_ref[...]
        w = w_ref[...]
        pre = _conv_pre(u, w, b_ref[...])
        s = jax.nn.sigmoid(pre)
        dpre = dout * (s * (1.0 + pre * (1.0 - s)))
        du = jnp.zeros_like(u)
        rows = []
        for j in range(CONV_WIDTH):
            k = CONV_WIDTH - 1 - j
            du = du + w[j:j + 1, :] * _shift_up(dpre, k)
            rows.append(jnp.sum(dpre * _shift_down(u, k), axis=0, keepdims=True))
        du_ref[...] = du.astype(du_ref.dtype)
        rows.append(jnp.zeros((8 - CONV_WIDTH, LANES), F32))
        dw_ref[...] = jnp.concatenate(rows, axis=0)
        db_ref[...] = jnp.sum(dpre, axis=0, keepdims=True)

    return pl.pallas_call(
        body, grid=(ncols // LANES,),
        in_specs=[pl.BlockSpec((S, LANES), lambda j: (0, j + cb0)),
                  pl.BlockSpec((CONV_WIDTH, LANES), lambda j: (0, j)),
                  pl.BlockSpec((1, LANES), lambda j: (0, j))]
        + [pl.BlockSpec((S, LANES), lambda j, lo=starts[i], hi=starts[i + 1]: (0, jnp.clip(j - lo, 0, hi - lo - 1)))
           for i in range(nd)] + [_ANY],
        out_specs=[pl.BlockSpec((S, LANES), lambda j: (0, j + cb0)),
                   pl.BlockSpec((8, LANES), lambda j: (0, j)),
                   pl.BlockSpec((1, LANES), lambda j: (0, j))],
        out_shape=[jax.ShapeDtypeStruct(dproj.shape, dproj.dtype),
                   jax.ShapeDtypeStruct((8, ncols), F32),
                   jax.ShapeDtypeStruct((1, ncols), F32)],
        input_output_aliases={3 + nd: 0},
        compiler_params=_params(("parallel",)), name="conv_bwd")(proj, conv_w, conv_b, *douts, dproj)


def _softplus(x):
    return jnp.maximum(x, 0.0) + jnp.log1p(jnp.exp(-jnp.abs(x)))


def _dot32(a, b, dims=(((1,), (0,)), ((), ()))):
    return lax.dot_general(a, b, dims, precision=HI, preferred_element_type=F32)


def _dotd(a, b, dims=(((1,), (0,)), ((), ()))):
    return lax.dot_general(a, b, dims, preferred_element_type=F32)


PAIRS_PER_GROUP = HEADS_PER_GROUP // 2


def _ssd_chunk(xs, Bm, Cm, z, dtr, dtb, alog, dsk, nw, h):
    L = Bm.shape[0]
    ri = lax.broadcasted_iota(jnp.int32, (L, L), 0)
    ci = lax.broadcasted_iota(jnp.int32, (L, L), 1)
    causal = ri >= ci
    tril = causal.astype(F32)
    first = _first_head(L)
    first1 = _first_head(1)
    CB = _dotd(Cm, Bm, _NT)
    gated, hnew = [], []
    ssq = jnp.zeros((L, 1), F32)
    for pp in range(len(xs)):
        dts, cums, tots, decay = [], [], [], []
        for a in range(2):
            r = 2 * pp + a
            dt = _softplus(dtr[r] + dtb[r])
            dA = dt * (-jnp.exp(alog[r]))
            acs = _dot32(tril, dA)
            cc = jnp.broadcast_to(acs, (L, L))
            decay.append(CB * jnp.exp(jnp.where(causal, cc - cc.T, -1e30)))
            dts.append(dt)
            cums.append(acs)
            tots.append(jnp.sum(dA, axis=0, keepdims=True))
        dt2 = jnp.where(first, dts[0], dts[1])
        acs2 = jnp.where(first, cums[0], cums[1])
        tot2 = jnp.where(first1, tots[0], tots[1])
        dsk2 = jnp.where(first1, dsk[2 * pp], dsk[2 * pp + 1])
        X = xs[pp] * dt2
        y = (jnp.where(first, _dotd(decay[0], X), _dotd(decay[1], X)) + jnp.exp(acs2) * _dotd(Cm, h[pp])
             + dsk2 * xs[pp])
        hnew.append(jnp.exp(tot2) * h[pp] + _dotd(Bm, X * jnp.exp(tot2 - acs2), _TN))
        g = y * (z[pp] * jax.nn.sigmoid(z[pp]))
        ssq = ssq + jnp.sum(g * g, axis=-1, keepdims=True)
        gated.append(g)
    rs = lax.rsqrt(ssq / (len(xs) * LANES) + EPS)
    return [g * rs * nw[pp] for pp, g in enumerate(gated)], hnew


def _ssd_args(xs_ref, b_ref, c_ref, z_ref, dt_ref, dtb_ref, al_ref, dsk_ref, nw_ref, h_ref):
    pairs = range(PAIRS_PER_GROUP)
    heads = range(HEADS_PER_GROUP)
    lanes = lambda ref, pp: ref[:, pp * LANES:(pp + 1) * LANES]
    return ([lanes(xs_ref, pp) for pp in pairs], b_ref[...], c_ref[...], [lanes(z_ref, pp) for pp in pairs],
            [dt_ref[r] for r in heads], [dtb_ref[r] for r in heads], [al_ref[r] for r in heads],
            [dsk_ref[r] for r in heads], [lanes(nw_ref, pp) for pp in pairs], [h_ref[pp] for pp in pairs])


def _ssd_specs(rev):
    H, N, L = HEADS_PER_GROUP, SSM_STATE, CHUNK
    gw = H * HEAD_DIM
    return dict(
        cols=lambda col0: pl.BlockSpec((L, gw), lambda g, c: (rev(c), col0 // gw + g)),
        bc=lambda first_block: pl.BlockSpec((L, N), lambda g, c: (rev(c), first_block + g)),
        dt=pl.BlockSpec((H, L, 1), lambda g, c: (g, rev(c), 0)),
        scal=pl.BlockSpec((H, 1, 1), lambda g, c: (g, 0, 0)),
        nw=pl.BlockSpec((1, gw), lambda g, c: (0, g)),
        hs=pl.BlockSpec((None, PAIRS_PER_GROUP, N, LANES), lambda g, c: (rev(c), g, 0, 0)),
        b_block=SSM_INNER // N,
    )


def _ssd_fwd(xbc, proj, dt_hm, dtb, alog, dsk, nw):
    S = xbc.shape[0]
    N, L = SSM_STATE, CHUNK
    nc = S // L
    sp = _ssd_specs(lambda c: c)

    def body(xs_ref, b_ref, c_ref, z_ref, dt_ref, dtb_ref, al_ref, dsk_ref, nw_ref, y_ref, hs_ref, h_ref):
        @pl.when(pl.program_id(1) == 0)
        def _():
            h_ref[...] = jnp.zeros_like(h_ref)

        hs_ref[...] = h_ref[...]
        out, hnew = _ssd_chunk(*_ssd_args(xs_ref, b_ref, c_ref, z_ref, dt_ref, dtb_ref, al_ref, dsk_ref, nw_ref, h_ref))
        for pp in range(PAIRS_PER_GROUP):
            y_ref[:, pp * LANES:(pp + 1) * LANES] = out[pp].astype(y_ref.dtype)
            h_ref[pp] = hnew[pp]

    return pl.pallas_call(
        body, grid=(SSM_GROUPS, nc),
        in_specs=[sp["cols"](0), sp["bc"](sp["b_block"]), sp["bc"](sp["b_block"] + SSM_GROUPS), sp["cols"](COL_Z),
                  sp["dt"], sp["scal"], sp["scal"], sp["scal"], sp["nw"]],
        out_specs=[sp["cols"](0), sp["hs"]],
        out_shape=[jax.ShapeDtypeStruct((S, SSM_INNER), BF16),
                   jax.ShapeDtypeStruct((nc, SSM_HEADS // 2, N, LANES), F32)],
        scratch_shapes=[pltpu.VMEM((PAIRS_PER_GROUP, N, LANES), F32)],
        compiler_params=_params(("parallel", "arbitrary")), name="ssd_fwd",
    )(xbc, xbc, xbc, proj, dt_hm, dtb, alog, dsk, nw)


def _ssd_bwd(xbc, proj, dt_hm, dtb, alog, dsk, nw, hs, dmixed, dproj):
    S = xbc.shape[0]
    N, L = SSM_STATE, CHUNK
    nc = S // L
    sp = _ssd_specs(lambda c: nc - 1 - c)

    def body(xs_ref, b_ref, c_ref, z_ref, dt_ref, dtb_ref, al_ref, dsk_ref, nw_ref, hs_ref, dy_ref, buf_ref,
             dxs_ref, dz_ref, db_ref, dc_ref, ddt_ref, ddtb_ref, dal_ref, ddsk_ref, dnw_ref, dh_ref):
        @pl.when(pl.program_id(1) == 0)
        def _():
            dh_ref[...] = jnp.zeros_like(dh_ref)
            ddtb_ref[...] = jnp.zeros_like(ddtb_ref)
            dal_ref[...] = jnp.zeros_like(dal_ref)
            ddsk_ref[...] = jnp.zeros_like(ddsk_ref)
            dnw_ref[...] = jnp.zeros_like(dnw_ref)

        pairs = range(PAIRS_PER_GROUP)
        lanes = lambda pp: slice(pp * LANES, (pp + 1) * LANES)
        _, vjp = jax.vjp(_ssd_chunk, *_ssd_args(xs_ref, b_ref, c_ref, z_ref, dt_ref, dtb_ref, al_ref, dsk_ref, nw_ref,
                                                hs_ref))
        dxs, dB, dC, dz, ddt, ddtb, dal, ddsk, dnw, dh = vjp(([dy_ref[:, lanes(pp)] for pp in pairs],
                                                              [dh_ref[pp] for pp in pairs]))
        db_ref[...] = dB
        dc_ref[...] = dC
        for pp in pairs:
            dxs_ref[:, lanes(pp)] = dxs[pp]
            dz_ref[:, lanes(pp)] = dz[pp].astype(dz_ref.dtype)
            dnw_ref[:, lanes(pp)] += dnw[pp]
            dh_ref[pp] = dh[pp]
        for r in range(HEADS_PER_GROUP):
            ddt_ref[r] = ddt[r]
            ddtb_ref[r] += ddtb[r]
            dal_ref[r] += dal[r]
            ddsk_ref[r] += ddsk[r]

    bc_out = pl.BlockSpec((L, N), lambda g, c: (nc - 1 - c, g))
    return pl.pallas_call(
        body, grid=(SSM_GROUPS, nc),
        in_specs=[sp["cols"](0), sp["bc"](sp["b_block"]), sp["bc"](sp["b_block"] + SSM_GROUPS), sp["cols"](COL_Z),
                  sp["dt"], sp["scal"], sp["scal"], sp["scal"], sp["nw"], sp["hs"], sp["cols"](0), _ANY],
        out_specs=[sp["cols"](0), sp["cols"](COL_Z), bc_out, bc_out, sp["dt"], sp["scal"], sp["scal"], sp["scal"],
                   sp["nw"]],
        input_output_aliases={11: 1},
        out_shape=[jax.ShapeDtypeStruct((S, SSM_INNER), F32), jax.ShapeDtypeStruct(dproj.shape, dproj.dtype),
                   jax.ShapeDtypeStruct((S, SSM_GROUPS * N), F32), jax.ShapeDtypeStruct((S, SSM_GROUPS * N), F32),
                   jax.ShapeDtypeStruct((SSM_HEADS, S, 1), F32),
                   jax.ShapeDtypeStruct((SSM_HEADS, 1, 1), F32), jax.ShapeDtypeStruct((SSM_HEADS, 1, 1), F32),
                   jax.ShapeDtypeStruct((SSM_HEADS, 1, 1), F32), jax.ShapeDtypeStruct((1, SSM_INNER), F32)],
        scratch_shapes=[pltpu.VMEM((PAIRS_PER_GROUP, N, LANES), F32)],
        compiler_params=_params(("parallel", "arbitrary")), name="ssd_bwd",
    )(xbc, xbc, xbc, proj, dt_hm, dtb, alog, dsk, nw, hs, dmixed, dproj)


ATTN_SCALE = HEAD_DIM ** -0.5
ATTN_PAIRS = ATTN_HEADS // 2


def _first_head(rows):
    return lax.broadcasted_iota(jnp.int32, (rows, LANES), 1) < HEAD_DIM


def _pair_norm(x, g2, scale):
    first = _first_head(x.shape[0])
    sq = x * x
    ms0 = jnp.sum(jnp.where(first, sq, 0.0), axis=-1, keepdims=True) * (1.0 / HEAD_DIM)
    ms1 = jnp.sum(jnp.where(first, 0.0, sq), axis=-1, keepdims=True) * (1.0 / HEAD_DIM)
    r = jnp.where(first, lax.rsqrt(ms0 + EPS), lax.rsqrt(ms1 + EPS))
    return x * r * g2 * scale


def _qk_prep_fwd(proj, gq2, gk2):
    S = proj.shape[0]
    tq = _pick(S, (512, 256))

    def body(q_ref, k_ref, v_ref, gq_ref, gk_ref, qo_ref, ko_ref, vo_ref):
        qo_ref[...] = _pair_norm(q_ref[...], gq_ref[...], ATTN_SCALE).astype(BF16)
        ko_ref[...] = _pair_norm(k_ref[...], gk_ref[...], 1.0).astype(BF16)
        vo_ref[...] = v_ref[...].astype(BF16)

    col = lambda c0: pl.BlockSpec((tq, LANES), lambda h, i: (i, c0 // LANES + h))
    blk = pl.BlockSpec((tq, LANES), lambda h, i: (i, h))
    vec = pl.BlockSpec((1, LANES), lambda h, i: (0, 0))
    return pl.pallas_call(
        body, grid=(ATTN_PAIRS, S // tq), in_specs=[col(COL_Q), col(COL_K), col(COL_V), vec, vec],
        out_specs=[blk, blk, blk], out_shape=[jax.ShapeDtypeStruct((S, ATTN_WIDTH), BF16)] * 3,
        compiler_params=_params(("parallel", "parallel")), name="qk_prep_fwd")(proj, proj, proj, gq2, gk2)


def _pair_norm_bwd(proj, col0, g2, scale, dn, dproj, name):
    S = proj.shape[0]
    tq = _pick(S, (512, 256))

    def body(u_ref, g_ref, dn_ref, buf_ref, du_ref, dg_ref):
        @pl.when((pl.program_id(0) == 0) & (pl.program_id(1) == 0))
        def _():
            dg_ref[...] = jnp.zeros_like(dg_ref)

        _, vjp = jax.vjp(lambda u, g: _pair_norm(u, g, scale), u_ref[...], g_ref[...])
        du, dg = vjp(dn_ref[...])
        du_ref[...] = du.astype(du_ref.dtype)
        dg_ref[...] += dg

    ublk = pl.BlockSpec((tq, LANES), lambda h, i: (i, col0 // LANES + h))
    blk = pl.BlockSpec((tq, LANES), lambda h, i: (i, h))
    vec = pl.BlockSpec((1, LANES), lambda h, i: (0, 0))
    return pl.pallas_call(
        body, grid=(ATTN_PAIRS, S // tq), in_specs=[ublk, vec, blk, _ANY], out_specs=[ublk, vec],
        out_shape=[jax.ShapeDtypeStruct(dproj.shape, dproj.dtype), jax.ShapeDtypeStruct((1, LANES), F32)],
        input_output_aliases={3: 0},
        compiler_params=_params(("arbitrary", "arbitrary")), name=name)(proj, g2, dn, dproj)


def _logf_cumsum_fwd(f_raw, f_bias):
    S, Hh = f_raw.shape
    L = CHUNK

    def body(f_ref, b_ref, o_ref, wide_ref):
        ri = lax.broadcasted_iota(jnp.int32, (L, L), 0)
        ci = lax.broadcasted_iota(jnp.int32, (L, L), 1)
        tril = (ri >= ci).astype(F32)
        carry = jnp.zeros((1, Hh), F32)
        for c in range(S // L):
            rows = slice(c * L, (c + 1) * L)
            lf = -_softplus(-(f_ref[rows, :] + b_ref[...]))
            cum = _dot32(tril, lf) + carry
            o_ref[rows, :] = cum
            for h in range(Hh):
                wide_ref[rows, h * HEAD_DIM:(h + 1) * HEAD_DIM] = jnp.broadcast_to(cum[:, h:h + 1], (L, HEAD_DIM))
            carry = cum[L - 1:L, :]

    return pl.pallas_call(
        body, out_shape=[jax.ShapeDtypeStruct((S, Hh), F32), jax.ShapeDtypeStruct((S, Hh * HEAD_DIM), F32)],
        name="logf_cumsum_fwd")(f_raw, f_bias)


def _logf_cumsum_bwd(f_raw, f_bias, dcum):
    S, Hh = f_raw.shape
    L = CHUNK

    def body(f_ref, b_ref, d_ref, df_ref, db_ref):
        ri = lax.broadcasted_iota(jnp.int32, (L, L), 0)
        ci = lax.broadcasted_iota(jnp.int32, (L, L), 1)
        triu = (ri <= ci).astype(F32)
        carry = jnp.zeros((1, Hh), F32)
        db = jnp.zeros((1, Hh), F32)
        for c in reversed(range(S // L)):
            suf = _dot32(triu, d_ref[c * L:(c + 1) * L, :]) + carry
            df = suf * jax.nn.sigmoid(-(f_ref[c * L:(c + 1) * L, :] + b_ref[...]))
            df_ref[c * L:(c + 1) * L, :] = df
            db = db + jnp.sum(df, axis=0, keepdims=True)
            carry = suf[0:1, :]
        db_ref[...] = db

    return pl.pallas_call(
        body, out_shape=[jax.ShapeDtypeStruct((S, Hh), F32), jax.ShapeDtypeStruct((1, Hh), F32)],
        name="logf_cumsum_bwd")(f_raw, f_bias, dcum)


_NT = (((1,), (1,)), ((), ()))
_TN = (((0,), (0,)), ((), ()))


def _mxu(a, b, dims=(((1,), (0,)), ((), ()))):
    return lax.dot_general(a, b, dims, preferred_element_type=F32)


def _flash_fwd(qs, kn, vb, cq, ck):
    S, W = qs.shape
    tq = tk = _pick(S, (512, 256))
    nmask = max(tq // tk, 1)

    def body(q_ref, k_ref, v_ref, cq_ref, ck_ref, o_ref, of_ref, lse_ref):
        i = pl.program_id(1)
        first = _first_head(tq)
        q2 = q_ref[...]
        zero = jnp.zeros_like(q2)
        qa = (jnp.where(first, q2, zero), jnp.where(first, zero, q2))
        cqa = (cq_ref[:, 0:1], cq_ref[:, HEAD_DIM:HEAD_DIM + 1])
        row0 = i * tq

        def step(j, carry, masked):
            ms, ls, acc, rem = carry
            off = pl.multiple_of(j * tk, tk)
            k = k_ref[pl.ds(off, tk), :]
            v = v_ref[pl.ds(off, tk), :]
            new_m, new_l, alphas, pvs, prs = [], [], [], [], []
            for a in range(2):
                s = _mxu(qa[a], k, _NT) + cqa[a] - ck_ref[a, :, pl.ds(off, tk)]
                if masked:
                    ri = lax.broadcasted_iota(jnp.int32, (tq, tk), 0) + row0
                    ci = lax.broadcasted_iota(jnp.int32, (tq, tk), 1) + off
                    s = jnp.where(ri >= ci, s, -1e30)
                m_new = jnp.maximum(ms[a], jnp.max(s, axis=-1, keepdims=True))
                alpha = jnp.exp(ms[a] - m_new)
                p = jnp.exp(s - m_new)
                new_l.append(alpha * ls[a] + jnp.sum(p, axis=-1, keepdims=True))
                new_m.append(m_new)
                alphas.append(alpha)
                p_hi = p.astype(BF16)
                pvs.append(_mxu(p_hi, v))
                prs.append(_mxu((p - p_hi.astype(F32)).astype(BF16), v))
            al = jnp.where(first, alphas[0], alphas[1])
            acc = al * acc + jnp.where(first, pvs[0], pvs[1])
            rem = al * rem + jnp.where(first, prs[0], prs[1])
            return tuple(new_m), tuple(new_l), acc, rem

        neg = jnp.full((tq, 1), -1e30, F32)
        z1 = jnp.zeros((tq, 1), F32)
        z2 = jnp.zeros((tq, LANES), F32)
        carry = ((neg, neg), (z1, z1), z2, z2)
        n_full = (i * tq) // tk
        carry = lax.fori_loop(0, n_full, lambda j, c: step(j, c, False), carry)
        for jj in range(nmask):
            carry = step(n_full + jj, carry, True)
        ms, ls, acc, rem = carry
        linv = jnp.where(first, 1.0 / ls[0], 1.0 / ls[1])
        o_ref[...] = (acc * linv).astype(o_ref.dtype)
        of_ref[...] = (acc + rem) * linv
        lse_ref[...] = jnp.where(first, ms[0] + jnp.log(ls[0]), ms[1] + jnp.log(ls[1]))

    qblk = pl.BlockSpec((tq, LANES), lambda h, i: (i, h))
    full = pl.BlockSpec((S, LANES), lambda h, i: (0, h))
    return pl.pallas_call(
        body, grid=(W // LANES, S // tq),
        in_specs=[qblk, full, full, qblk, pl.BlockSpec((2, 1, S), lambda h, i: (h, 0, 0))],
        out_specs=[qblk, qblk, qblk],
        out_shape=[jax.ShapeDtypeStruct((S, W), BF16), jax.ShapeDtypeStruct((S, W), F32),
                   jax.ShapeDtypeStruct((S, W), F32)],
        compiler_params=_params(("parallel", "parallel")), name="flash_fwd")(qs, kn, vb, cq, ck)


def _flash_bwd(qs, kn, vb, cq, ck, o_fine, do, do_col0, lse):
    S, W = qs.shape
    tq = tk = _pick(S, (512, 256))
    nq = S // tq
    nmask = max(tk // tq, 1)

    def body(q_ref, k_ref, v_ref, cq_ref, ck_ref, of_ref, do_ref, lse_ref, dq_ref, dk_ref, dv_ref, dck_ref):
        j = pl.program_id(1)

        @pl.when(j == 0)
        def _():
            dq_ref[...] = jnp.zeros_like(dq_ref)

        firstk = _first_head(tk)
        firstq = _first_head(tq)
        k2 = k_ref[...]
        v2 = v_ref[...]
        zk = jnp.zeros_like(k2)
        ka = (jnp.where(firstk, k2, zk), jnp.where(firstk, zk, k2))
        va = (jnp.where(firstk, v2, zk), jnp.where(firstk, zk, v2))
        cka = (ck_ref[0], ck_ref[1])
        col0 = j * tk

        def step(i, carry, masked):
            dk, dv, dck0, dck1 = carry
            dcks = [dck0, dck1]
            off = pl.multiple_of(i * tq, tq)
            rows = pl.ds(off, tq)
            q2 = q_ref[rows, :]
            dob = do_ref[rows, :].astype(BF16)
            prod = dob.astype(F32) * of_ref[rows, :]
            dkp, dvp, dqp = [], [], []
            for a in range(2):
                lane = pl.ds(a * HEAD_DIM, 1)
                s = _mxu(q2, ka[a], _NT) + cq_ref[rows, lane] - cka[a]
                if masked:
                    ri = lax.broadcasted_iota(jnp.int32, (tq, tk), 0) + off
                    ci = lax.broadcasted_iota(jnp.int32, (tq, tk), 1) + col0
                    s = jnp.where(ri >= ci, s, -1e30)
                p = jnp.exp(s - lse_ref[rows, lane])
                dp = _mxu(dob, va[a], _NT)
                own = jnp.where(firstq, prod, 0.0) if a == 0 else jnp.where(firstq, 0.0, prod)
                ds = p * (dp - jnp.sum(own, axis=-1, keepdims=True))
                dsb = ds.astype(BF16)
                dvp.append(_mxu(p.astype(BF16), dob, _TN))
                dkp.append(_mxu(dsb, q2, _TN))
                dqp.append(_mxu(dsb, k2))
                dcks[a] = dcks[a] - jnp.sum(ds, axis=0, keepdims=True)
            dq_ref[rows, :] += jnp.where(firstq, dqp[0], dqp[1])
            dk = dk + jnp.where(firstk, dkp[0], dkp[1])
            dv = dv + jnp.where(firstk, dvp[0], dvp[1])
            return dk, dv, dcks[0], dcks[1]

        z2 = jnp.zeros((tk, LANES), F32)
        z1 = jnp.zeros((1, tk), F32)
        carry = (z2, z2, z1, z1)
        i0 = (j * tk) // tq
        for ii in range(nmask):
            carry = step(i0 + ii, carry, True)
        dk, dv, dck0, dck1 = lax.fori_loop(i0 + nmask, nq, lambda i, c: step(i, c, False), carry)
        dk_ref[...] = dk
        dv_ref[...] = dv.astype(dv_ref.dtype)
        dck_ref[0] = dck0
        dck_ref[1] = dck1

    kblk = pl.BlockSpec((tk, LANES), lambda h, j: (j, h))
    full = pl.BlockSpec((S, LANES), lambda h, j: (0, h))
    dofull = pl.BlockSpec((S, LANES), lambda h, j: (0, do_col0 // LANES + h))
    rowt = pl.BlockSpec((2, 1, tk), lambda h, j: (h, 0, j))
    dvblk = pl.BlockSpec((tk, LANES), lambda h, j: (j, COL_V // LANES + h))
    return pl.pallas_call(
        body, grid=(W // LANES, S // tk),
        in_specs=[full, kblk, kblk, full, rowt, full, dofull, full],
        out_specs=[full, kblk, dvblk, rowt],
        out_shape=[jax.ShapeDtypeStruct((S, W), F32), jax.ShapeDtypeStruct((S, W), F32),
                   jax.ShapeDtypeStruct((S, IN_COLS_PAD), BF16), jax.ShapeDtypeStruct((2 * (W // LANES), 1, S), F32)],
        compiler_params=_params(("parallel", "arbitrary")), name="flash_bwd")(qs, kn, vb, cq, ck, o_fine, do, lse)


XATTN_SCALE = XATTN_DIM ** -0.5


def _xq_norm(q, g):
    return _rms(q, g) * XATTN_SCALE


def _xattn_fwd(xq, kv, gq, gk):
    S = xq.shape[0]
    Mm = kv.shape[0]
    Dh = XATTN_DIM
    tq = _pick(S, (512, 256))

    def body(q_ref, k_ref, v_ref, gq_ref, gk_ref, o_ref):
        qn = _xq_norm(q_ref[...], gq_ref[...]).astype(BF16)
        kn = _rms(k_ref[...], gk_ref[...]).astype(BF16)
        s = _mxu(qn, kn, _NT)
        m = jnp.max(s, axis=-1, keepdims=True)
        p = jnp.exp(s - m)
        l = jnp.sum(p, axis=-1, keepdims=True)
        o_ref[...] = (_mxu(p.astype(BF16), v_ref[...].astype(BF16)) / l).astype(o_ref.dtype)

    vec = pl.BlockSpec((1, Dh), lambda h, i: (0, 0))
    return pl.pallas_call(
        body, grid=(XATTN_HEADS, S // tq),
        in_specs=[pl.BlockSpec((tq, Dh), lambda h, i: (i, h)), pl.BlockSpec((Mm, Dh), lambda h, i: (0, h)),
                  pl.BlockSpec((Mm, Dh), lambda h, i: (0, XATTN_HEADS + h)), vec, vec],
        out_specs=pl.BlockSpec((tq, Dh), lambda h, i: (i, h)),
        out_shape=jax.ShapeDtypeStruct((S, XATTN_HEADS * Dh), BF16),
        compiler_params=_params(("parallel", "parallel")), name="xattn_fwd")(xq, kv, kv, gq, gk)


def _xattn_bwd(xq, kv, gq, gk, do):
    S = xq.shape[0]
    Mm = kv.shape[0]
    Dh = XATTN_DIM
    tq = _pick(S, (512, 256))
    nq = S // tq

    def body(q_ref, k_ref, v_ref, gq_ref, gk_ref, do_ref, dq_ref, dk_ref, dv_ref, dgq_ref, dgk_ref, dkn_acc, dv_acc):
        h = pl.program_id(0)
        i = pl.program_id(1)

        @pl.when((h == 0) & (i == 0))
        def _():
            dgq_ref[...] = jnp.zeros_like(dgq_ref)
            dgk_ref[...] = jnp.zeros_like(dgk_ref)

        @pl.when(i == 0)
        def _():
            dkn_acc[...] = jnp.zeros_like(dkn_acc)
            dv_acc[...] = jnp.zeros_like(dv_acc)

        qn32, vq = jax.vjp(_xq_norm, q_ref[...], gq_ref[...])
        kn32, vk = jax.vjp(_rms, k_ref[...], gk_ref[...])
        qn = qn32.astype(BF16)
        kn = kn32.astype(BF16)
        vb = v_ref[...].astype(BF16)
        s = _mxu(qn, kn, _NT)
        m = jnp.max(s, axis=-1, keepdims=True)
        p = jnp.exp(s - m)
        p = p / jnp.sum(p, axis=-1, keepdims=True)
        dob = do_ref[...].astype(BF16)
        dp = _mxu(dob, vb, _NT)
        delta = jnp.sum(p * dp, axis=-1, keepdims=True)
        ds = (p * (dp - delta)).astype(BF16)
        dv_acc[...] += _mxu(p.astype(BF16), dob, _TN)
        dkn_acc[...] += _mxu(ds, qn, _TN)
        dq, dgq = vq(_mxu(ds, kn))
        dq_ref[...] = dq.astype(dq_ref.dtype)
        dgq_ref[...] += dgq

        @pl.when(i == nq - 1)
        def _():
            dk, dgk = vk(dkn_acc[...])
            dk_ref[...] = dk.astype(dk_ref.dtype)
            dv_ref[...] = dv_acc[...].astype(dv_ref.dtype)
            dgk_ref[...] += dgk

    vec = pl.BlockSpec((1, Dh), lambda h, i: (0, 0))
    qblk = pl.BlockSpec((tq, Dh), lambda h, i: (i, h))
    kblk = pl.BlockSpec((Mm, Dh), lambda h, i: (0, h))
    vblk = pl.BlockSpec((Mm, Dh), lambda h, i: (0, XATTN_HEADS + h))
    return pl.pallas_call(
        body, grid=(XATTN_HEADS, nq),
        in_specs=[qblk, kblk, vblk, vec, vec, qblk],
        out_specs=[qblk, kblk, kblk, vec, vec],
        out_shape=[jax.ShapeDtypeStruct((S, XATTN_HEADS * Dh), BF16),
                   jax.ShapeDtypeStruct((Mm, XATTN_HEADS * Dh), BF16),
                   jax.ShapeDtypeStruct((Mm, XATTN_HEADS * Dh), BF16),
                   jax.ShapeDtypeStruct((1, Dh), F32), jax.ShapeDtypeStruct((1, Dh), F32)],
        scratch_shapes=[pltpu.VMEM((Mm, Dh), F32), pltpu.VMEM((Mm, Dh), F32)],
        compiler_params=_params(("arbitrary", "arbitrary")), name="xattn_bwd")(xq, kv, kv, gq, gk, do)


def _loss_head(y, target):
    S, D = y.shape
    tr = _pick(S, (512, 256))

    def body(y_ref, t_ref, dy_ref, loss_ref):
        @pl.when(pl.program_id(0) == 0)
        def _():
            loss_ref[...] = jnp.zeros_like(loss_ref)

        err = y_ref[...] - t_ref[...]
        dy_ref[...] = err * (1.0 / D)
        loss_ref[...] += jnp.sum(err * err) * (0.5 / D)

    row = pl.BlockSpec((tr, D), lambda i: (i, 0))
    return pl.pallas_call(
        body, grid=(S // tr,), in_specs=[row, row],
        out_specs=[row, pl.BlockSpec((1, LANES), lambda i: (0, 0))],
        out_shape=[jax.ShapeDtypeStruct((S, D), F32), jax.ShapeDtypeStruct((1, LANES), F32)],
        compiler_params=_params(("arbitrary",)), name="loss_head")(y, target)


def _row_tile(R, C):
    for tr in (1024, 512, 256, 128, 64, 32, 16, 8):
        if R % tr == 0 and tr * C * 4 <= (1 << 20):
            return tr
    return R


def _nsum(arrs, out_dtypes, name):
    R, C = arrs[0].shape
    tr = _row_tile(R, C)
    n = len(arrs)

    def body(*refs):
        acc = refs[0][...].astype(F32)
        for r in refs[1:n]:
            acc = acc + r[...].astype(F32)
        for o in refs[n:]:
            o[...] = acc.astype(o.dtype)

    blk = pl.BlockSpec((tr, C), lambda i: (i, 0))
    outs = pl.pallas_call(
        body, grid=(R // tr,), in_specs=[blk] * n, out_specs=[blk] * len(out_dtypes),
        out_shape=[jax.ShapeDtypeStruct((R, C), d) for d in out_dtypes],
        compiler_params=_params(("parallel",)), name=name)(*arrs)
    return outs


def _adamw(w, g, m, v, name):
    R, C = w.shape
    tr = _row_tile(R, C)
    c1 = 1.0 - ADAM_B1 ** ADAM_STEP
    c2 = 1.0 - ADAM_B2 ** ADAM_STEP

    def body(w_ref, g_ref, m_ref, v_ref, d_ref, mo_ref, vo_ref):
        g_t = g_ref[...]
        m_new = ADAM_B1 * m_ref[...] + (1.0 - ADAM_B1) * g_t
        v_new = ADAM_B2 * v_ref[...] + (1.0 - ADAM_B2) * (g_t * g_t)
        d_ref[...] = -ADAM_LR * ((m_new / c1) / (jnp.sqrt(v_new / c2) + ADAM_EPS) + ADAM_WD * w_ref[...])
        mo_ref[...] = m_new
        vo_ref[...] = v_new

    blk = pl.BlockSpec((tr, C), lambda i: (i, 0))
    return pl.pallas_call(
        body, grid=(R // tr,), in_specs=[blk] * 4, out_specs=[blk] * 3,
        out_shape=[jax.ShapeDtypeStruct((R, C), F32)] * 3,
        compiler_params=_params(("parallel",)), name=name)(w, g, m, v)


D_MODEL = 1024
SSM_INNER = SSM_HEADS * HEAD_DIM
CONV_DIM = SSM_INNER + 2 * SSM_GROUPS * SSM_STATE
ATTN_WIDTH = ATTN_HEADS * HEAD_DIM
COL_Z = 0
COL_XBC = COL_Z + SSM_INNER
COL_Q = COL_XBC + CONV_DIM
COL_K = COL_Q + ATTN_WIDTH
COL_V = COL_K + ATTN_WIDTH
COL_DT = COL_V + ATTN_WIDTH
COL_F = COL_DT + SSM_HEADS
IN_COLS = COL_F + ATTN_HEADS
IN_COLS_PAD = -(-IN_COLS // LANES) * LANES
REF_COL_DT = COL_Q
SHARD_COLS = IN_COLS // N_CHIPS
_COL_RANGES = ((0, REF_COL_DT, 0), (REF_COL_DT + SSM_HEADS, COL_F, COL_Q), (REF_COL_DT, REF_COL_DT + SSM_HEADS, COL_DT),
               (COL_F, IN_COLS, COL_F))


def _w_in_from_shards(g):
    parts = []
    for lo, hi, _ in _COL_RANGES:
        while lo < hi:
            j = lo // SHARD_COLS
            end = min(hi, (j + 1) * SHARD_COLS)
            parts.append(g[j][:, lo - j * SHARD_COLS:end - j * SHARD_COLS])
            lo = end
    parts.append(jnp.zeros((g.shape[1], IN_COLS_PAD - IN_COLS), g.dtype))
    return jnp.concatenate(parts, axis=1)


def _w_in_to_shards(w):
    shards = []
    for j in range(N_CHIPS):
        parts = []
        for lo, hi, here in sorted(_COL_RANGES):
            a, b = max(lo, j * SHARD_COLS), min(hi, (j + 1) * SHARD_COLS)
            if a < b:
                parts.append(w[:, here + a - lo:here + b - lo])
        shards.append(jnp.concatenate(parts, axis=1))
    return jnp.stack(shards)


def _add_residual(acc, res):
    return (res + acc,)


def _relu2(acc):
    r = jnp.maximum(acc, 0.0)
    return acc, r * r


def _relu2_bwd(acc, a):
    return (acc * (2.0 * jnp.maximum(a, 0.0)),)


def _layer_fwd_bwd(x, mem, target, w_in, p, late_weights, send_late_grads):
    S = x.shape[0]
    hd3 = lambda a: a.reshape(SSM_HEADS, 1, 1)

    h1 = _rmsnorm_fwd(x, p["g_mix"], "norm_mix")
    proj = _mm(h1, w_in, "nn", "in_proj")
    xbc = _conv_fwd(proj, COL_XBC, CONV_DIM, p["conv_w"], p["conv_b"])
    dt_hm = proj[:, COL_DT:COL_DT + SSM_HEADS].T[:, :, None]
    ssd_par = (hd3(p["dt_bias"]), hd3(p["a_log"]), hd3(p["d_skip"]), p["ssm_norm_w"])
    y, hs = _ssd_fwd(xbc, proj, dt_hm, *ssd_par)
    f_raw = proj[:, COL_F:COL_F + ATTN_HEADS]
    gq2 = jnp.tile(p["g_q"], (1, 2))
    gk2 = jnp.tile(p["g_k"], (1, 2))
    qs, kn, vb = _qk_prep_fwd(proj, gq2, gk2)
    cum, cq = _logf_cumsum_fwd(f_raw, p["f_bias"])
    ck = cum.T[:, None, :]
    o, o_fine, lse = _flash_fwd(qs, kn, vb, cq, ck)
    W = late_weights((o_fine, y))
    x1 = _mm(y, W["w_out"][:SSM_INNER], "nn", "out_proj_ssm", epilogue=_add_residual, extras=(x,))
    x1 = _mm(o, W["w_out"][SSM_INNER:], "nn", "out_proj_attn", epilogue=_add_residual, extras=(x1,))
    h2 = _rmsnorm_fwd(x1, p["g_xattn"], "norm_xattn")
    mem_n = _rmsnorm_fwd(mem, p["g_mem"], "norm_mem")
    xq = _mm(h2, W["xq_w"], "nn", "xq_proj")
    kv = _mm(mem_n, W["xkv_w"], "nn", "xkv_proj", b_chunks=N_CHIPS)
    xo = _xattn_fwd(xq, kv, p["xg_q"], p["xg_k"])
    x2 = _mm(xo, W["xo_w"], "nn", "xo_proj", epilogue=_add_residual, extras=(x1,))
    h3 = _rmsnorm_fwd(x2, p["g_mlp"], "norm_mlp")
    a, act = _mm(h3, W["w_up"], "nn", "mlp_up", out_dtypes=(F32, BF16), epilogue=_relu2, b_chunks=N_CHIPS)
    x3 = _mm(act, W["w_down"], "nn", "mlp_down", epilogue=_add_residual, extras=(x2,))
    dy, loss_row = _loss_head(x3, target)

    gW, gp = {}, {}
    da = _mm(dy, W["w_down"], "nt", "d_act", out_dtypes=(BF16,), epilogue=_relu2_bwd, extras=(a,))
    gW["w_down"] = _mm(act, dy, "tn", "g_w_down", out_dtypes=(BF16,))
    gW["w_up"] = _mm(h3, da, "tn", "g_w_up", out_dtypes=(BF16,), out_chunks=N_CHIPS)
    dh3 = _mm(da, W["w_up"], "nt", "d_h3", b_chunks=N_CHIPS)
    dx2, gp["g_mlp"] = _rmsnorm_bwd(x2, p["g_mlp"], dh3, dy, "norm_mlp_bwd")
    dxo = _mm(dx2, W["xo_w"], "nt", "d_xo", out_dtypes=(BF16,))
    gW["xo_w"] = _mm(xo, dx2, "tn", "g_xo_w", out_dtypes=(BF16,))
    dxq, dk_x, dv_x, gp["xg_q"], gp["xg_k"] = _xattn_bwd(xq, kv, p["xg_q"], p["xg_k"], dxo)
    dkv = jnp.concatenate([dk_x, dv_x], axis=-1)
    gW["xq_w"] = _mm(h2, dxq, "tn", "g_xq_w", out_dtypes=(BF16,))
    dh2 = _mm(dxq, W["xq_w"], "nt", "d_h2")
    gW["xkv_w"] = _mm(mem_n, dkv, "tn", "g_xkv_w", out_dtypes=(BF16,), out_chunks=N_CHIPS)
    dmem_n = _mm(dkv, W["xkv_w"], "nt", "d_mem_n", b_chunks=N_CHIPS)
    _, gp["g_mem"] = _rmsnorm_bwd(mem, p["g_mem"], dmem_n, None, "norm_mem_bwd")
    dx1, gp["g_xattn"] = _rmsnorm_bwd(x1, p["g_xattn"], dh2, dx2, "norm_xattn_bwd")
    dmixed = _mm(dx1, W["w_out"], "nt", "d_mixed")
    gW["w_out"] = jnp.concatenate([_mm(y, dx1, "tn", "g_w_out_ssm", out_dtypes=(BF16,)),
                                   _mm(o, dx1, "tn", "g_w_out_attn", out_dtypes=(BF16,))], axis=0)
    token = send_late_grads(gW)
    dqs, dkn, dproj, dck = _flash_bwd(qs, kn, vb, cq, ck + token[:1, :1], o_fine, dmixed, SSM_INNER, lse)
    dproj, dgq2 = _pair_norm_bwd(proj, COL_Q, gq2, ATTN_SCALE, dqs, dproj, "q_norm_bwd")
    dproj, dgk2 = _pair_norm_bwd(proj, COL_K, gk2, 1.0, dkn, dproj, "k_norm_bwd")
    gp["g_q"] = dgq2[:, :HEAD_DIM] + dgq2[:, HEAD_DIM:]
    gp["g_k"] = dgk2[:, :HEAD_DIM] + dgk2[:, HEAD_DIM:]
    df, gp["f_bias"] = _logf_cumsum_bwd(f_raw, p["f_bias"], dck[:, 0, :].T)
    dxs, dproj, dB, dC, ddt, ddtb, dalog, ddsk, gp["ssm_norm_w"] = _ssd_bwd(xbc, proj, dt_hm, *ssd_par, hs, dmixed, dproj)
    gp["dt_bias"] = ddtb.reshape(1, SSM_HEADS)
    gp["a_log"] = dalog.reshape(1, SSM_HEADS)
    gp["d_skip"] = ddsk.reshape(1, SSM_HEADS)
    dproj, dconv_w, gp["conv_b"] = _conv_bwd(proj, COL_XBC, CONV_DIM, p["conv_w"], p["conv_b"], (dxs, dB, dC), dproj)
    gp["conv_w"] = dconv_w[:CONV_WIDTH]
    tail = jnp.concatenate([ddt[:, :, 0].T, df, jnp.zeros((S, IN_COLS_PAD - IN_COLS), F32)], axis=-1).astype(BF16)
    dproj = lax.dynamic_update_slice(dproj, tail, (0, COL_DT))
    g_w_in = _mm(h1, dproj, "tn", "g_w_in", out_dtypes=(BF16,))
    dh1 = _mm(dproj, w_in, "nt", "d_h1")
    dx, gp["g_mix"] = _rmsnorm_bwd(x, p["g_mix"], dh1, dx1, "norm_mix_bwd")
    return loss_row, dx, g_w_in, gp


_ANY = pl.BlockSpec(memory_space=pl.ANY)


def _place():
    x, y, c = lax.axis_index("x"), lax.axis_index("y"), lax.axis_index("c")
    chips = [(1 - x, y), (x, 1 - y), (1 - x, 1 - y)]
    return x, y, c, chips


def _chip_index(px, py):
    return 2 * px + py


def _all_gather_chips(split, whole):
    ns, nw = len(split), len(whole)
    n = ns + nw

    def body(*refs):
        ins, outs = refs[:n], refs[n:2 * n]
        send_ici, recv_ici, send_d2d, recv_d2d = refs[2 * n:]
        x, y, c, chips = _place()
        me = _chip_index(x, y)
        sib = (x, y, 1 - c)

        def ici(k, j, src, dst):
            return pltpu.make_async_remote_copy(src_ref=src, dst_ref=dst, send_sem=send_ici.at[3 * k + j],
                                                recv_sem=recv_ici.at[3 * k + j], device_id=(*chips[j], c),
                                                device_id_type=MESH)

        def d2d(k, j, piece):
            return pltpu.make_async_remote_copy(src_ref=piece, dst_ref=piece, send_sem=send_d2d.at[3 * k + j],
                                                recv_sem=recv_d2d.at[3 * k + j], device_id=sib, device_id_type=MESH)

        sends = []
        for k in range(n):
            for j in range(3):
                if k < ns:
                    sends.append(ici(k, j, ins[k].at[c], outs[k].at[me, c]))
                else:
                    sends.append(ici(k, j, ins[k], outs[k].at[me]))
                sends[-1].start()
        passed = []
        for k in range(n):
            for j in range(3):
                src_chip = _chip_index(*chips[j])
                if k < ns:
                    ici(k, j, ins[k].at[c], outs[k].at[src_chip, c]).wait_recv()
                    passed.append(d2d(k, j, outs[k].at[src_chip, c]))
                    passed[-1].start()
                else:
                    ici(k, j, ins[k], outs[k].at[src_chip]).wait_recv()
        for k in range(ns):
            for j in range(3):
                d2d(k, j, outs[k].at[_chip_index(*chips[j]), 1 - c]).wait_recv()
        for cp in sends + passed:
            cp.wait_send()

    arrs = list(split) + list(whole)
    return pl.pallas_call(
        body, in_specs=[_ANY] * n, out_specs=[_ANY] * n,
        out_shape=[jax.ShapeDtypeStruct((N_CHIPS,) + a.shape, a.dtype) for a in arrs],
        scratch_shapes=[pltpu.SemaphoreType.DMA((3 * n,)), pltpu.SemaphoreType.DMA((3 * n,)),
                        pltpu.SemaphoreType.DMA((3 * ns,)), pltpu.SemaphoreType.DMA((3 * ns,))],
        name="all_gather_chips")(*arrs)


def _sibling_send_halves(grads, name):
    n = len(grads)

    def body(*refs):
        ins, outs = refs[:n], refs[n:2 * n]
        send_sem, recv_sem = refs[2 * n:]
        x, y, c, _ = _place()

        def cp(k, j, half):
            return pltpu.make_async_remote_copy(src_ref=ins[k].at[j, half], dst_ref=outs[k].at[j],
                                                send_sem=send_sem.at[N_CHIPS * k + j],
                                                recv_sem=recv_sem.at[N_CHIPS * k + j],
                                                device_id=(x, y, 1 - c), device_id_type=MESH)

        copies = [cp(k, j, 1 - c) for k in range(n) for j in range(N_CHIPS)]
        for q in copies:
            q.start()
        for q in copies:
            q.wait()

    return pl.pallas_call(
        body, in_specs=[_ANY] * n, out_specs=[_ANY] * n,
        out_shape=[jax.ShapeDtypeStruct((N_CHIPS,) + g.shape[2:], g.dtype) for g in grads],
        scratch_shapes=[pltpu.SemaphoreType.DMA((N_CHIPS * n,)), pltpu.SemaphoreType.DMA((N_CHIPS * n,))],
        name=name)(*grads)


def _sibling_exchange(halves, name):
    n = len(halves)

    def body(*refs):
        ins, outs = refs[:n], refs[n:2 * n]
        send_sem, recv_sem = refs[2 * n:]
        x, y, c, _ = _place()

        def cp(k, half):
            return pltpu.make_async_remote_copy(src_ref=ins[k], dst_ref=outs[k].at[half], send_sem=send_sem.at[k],
                                                recv_sem=recv_sem.at[k], device_id=(x, y, 1 - c), device_id_type=MESH)

        sends = [cp(k, c) for k in range(n)]
        for q in sends:
            q.start()
        for k in range(n):
            cp(k, 1 - c).wait_recv()
        for q in sends:
            q.wait_send()

    return pl.pallas_call(
        body, in_specs=[_ANY] * n, out_specs=[_ANY] * n,
        out_shape=[jax.ShapeDtypeStruct((2,) + h.shape, h.dtype) for h in halves],
        scratch_shapes=[pltpu.SemaphoreType.DMA((n,)), pltpu.SemaphoreType.DMA((n,))],
        name=name)(*halves)


_HBM = pl.BlockSpec(memory_space=pltpu.HBM)
_SEM = pl.BlockSpec(memory_space=pltpu.SEMAPHORE)
_SPLIT_EFFECT = pltpu.SideEffectType.DATAFLOW_SIDE_EFFECTING


class _Split(NamedTuple):
    send_sems: jax.Array
    recv_sems: jax.Array
    sources: tuple
    lands: tuple
    token: jax.Array


def _split_copies(kind, srcs, lands, send_sems, recv_sems):
    x, y, c, chips = _place()
    me = _chip_index(x, y)
    copies = []
    for k in range(len(srcs)):
        for j in range(3):
            if kind == "gather":
                src, dst = srcs[k], lands[k].at[me]
            else:
                src, dst = srcs[k].at[_chip_index(*chips[j])], lands[k].at[j]
            copies.append(pltpu.make_async_remote_copy(
                src_ref=src, dst_ref=dst, send_sem=send_sems.at[3 * k + j], recv_sem=recv_sems.at[3 * k + j],
                device_id=(*chips[j], c), device_id_type=MESH))
    return copies


def _split_start(name, sources, kind, after):
    n = len(sources)
    if kind == "gather":
        lands = [lax.empty((N_CHIPS,) + s.shape, s.dtype) for s in sources]
    else:
        lands = [lax.empty((3,) + s.shape[1:], s.dtype) for s in sources]
    deps = [] if after is None else [after]

    def body(*refs):
        srcs, lnds = refs[:n], refs[n:2 * n]
        send_sems, recv_sems = refs[2 * n + len(deps)], refs[2 * n + len(deps) + 1]
        for cp in _split_copies(kind, srcs, lnds, send_sems, recv_sems):
            cp.start()
        refs[-1][...] = jnp.zeros_like(refs[-1])

    hbm = lambda a: pltpu.with_memory_space_constraint(a, pltpu.HBM)
    outs = pl.pallas_call(
        body, name=name,
        in_specs=[_HBM] * (2 * n) + [_ANY] * len(deps),
        out_specs=[_SEM, _SEM] + [_HBM] * (2 * n) + [pl.BlockSpec(memory_space=pltpu.VMEM)],
        out_shape=[pltpu.SemaphoreType.DMA((3 * n,)), pltpu.SemaphoreType.DMA((3 * n,))]
        + [pltpu.HBM(a.shape, a.dtype) for a in list(sources) + lands] + [jax.ShapeDtypeStruct((8, LANES), F32)],
        input_output_aliases={k: 2 + k for k in range(2 * n)},
        compiler_params=pltpu.CompilerParams(has_side_effects=_SPLIT_EFFECT),
    )(*[hbm(s) for s in sources], *[hbm(l) for l in lands], *deps)
    return _Split(outs[0], outs[1], tuple(outs[2:2 + n]), tuple(outs[2 + n:2 + 2 * n]), outs[-1])


def _split_wait(name, h, kind, after):
    n = len(h.sources)

    def body(*refs):
        srcs, lnds = refs[:n], refs[n:2 * n]
        for cp in _split_copies(kind, srcs, lnds, refs[2 * n], refs[2 * n + 1]):
            cp.wait_send()
            cp.wait_recv()

    outs = pl.pallas_call(
        body, name=name,
        in_specs=[_HBM] * (2 * n) + [_SEM, _SEM] + [_ANY] * len(after),
        out_specs=[_HBM] * (2 * n),
        out_shape=[pltpu.HBM(a.shape, a.dtype) for a in h.sources + h.lands],
        input_output_aliases={k: k for k in range(2 * n)},
        compiler_params=pltpu.CompilerParams(has_side_effects=_SPLIT_EFFECT),
    )(*h.sources, *h.lands, h.send_sems, h.recv_sems, *after)
    return outs[:n], outs[n:]


def _all_reduce_small(vec):
    R, C = vec.shape

    def body(v_ref, o_ref, buf, send_sem, recv_sem):
        x, y, c = lax.axis_index("x"), lax.axis_index("y"), lax.axis_index("c")
        me = 4 * x + 2 * y + c
        buf[me] = v_ref[...]
        copies = []
        for r in range(1, N_DEV):
            fx, fy, fc = (r >> 2) & 1, (r >> 1) & 1, r & 1
            peer = (x ^ fx, y ^ fy, c ^ fc)
            copies.append(pltpu.make_async_remote_copy(src_ref=v_ref, dst_ref=buf.at[me], send_sem=send_sem.at[r - 1],
                                                       recv_sem=recv_sem.at[r - 1], device_id=peer, device_id_type=MESH))
        for q in copies:
            q.start()
        for r in range(1, N_DEV):
            fx, fy, fc = (r >> 2) & 1, (r >> 1) & 1, r & 1
            src = 4 * (x ^ fx) + 2 * (y ^ fy) + (c ^ fc)
            pltpu.make_async_remote_copy(src_ref=v_ref, dst_ref=buf.at[src], send_sem=send_sem.at[r - 1],
                                         recv_sem=recv_sem.at[r - 1], device_id=(x, y, c), device_id_type=MESH).wait_recv()
        acc = buf[0]
        for d in range(1, N_DEV):
            acc = acc + buf[d]
        o_ref[...] = acc
        for q in copies:
            q.wait_send()

    vm = pl.BlockSpec(memory_space=pltpu.VMEM)
    return pl.pallas_call(
        body, in_specs=[vm], out_specs=vm, out_shape=jax.ShapeDtypeStruct((R, C), F32),
        scratch_shapes=[pltpu.VMEM((N_DEV, R, C), F32), pltpu.SemaphoreType.DMA((N_DEV - 1,)),
                        pltpu.SemaphoreType.DMA((N_DEV - 1,))],
        name="all_reduce_small")(vec)


_INPUTS = ["x", "mem", "g_mix", "w_in", "conv_w", "conv_b", "dt_bias", "a_log", "d_skip", "ssm_norm_w", "g_q", "g_k",
           "f_bias", "w_out", "g_xattn", "g_mem", "xq_w", "xkv_w", "xg_q", "xg_k", "xo_w", "g_mlp", "w_up", "w_down"]
_WEIGHTS = _INPUTS[2:]
_BIG = ["w_in", "w_out", "xq_w", "xkv_w", "xo_w", "w_up", "w_down"]
_LATE = _BIG[1:]
_COL_SHARDED = ["w_in", "xkv_w", "w_up"]
_SMALL = [n for n in _WEIGHTS if n not in _BIG]


def _pack_rows(arrs, width):
    starts, r = [], 0
    for a in arrs:
        starts.append(r)
        r += a.shape[0]
    out = jnp.concatenate([jnp.pad(a, ((0, 0), (0, width - a.shape[1]))) for a in arrs], axis=0)
    return jnp.pad(out, ((0, -r % 8), (0, 0))), starts


def _adamw_small(summed, starts, ws, ms, vs, conv_w_index):
    n = len(ws)
    c1 = 1.0 - ADAM_B1 ** ADAM_STEP
    c2 = 1.0 - ADAM_B2 ** ADAM_STEP

    def body(s_ref, *refs):
        w_refs, m_refs, v_refs = refs[:n], refs[n:2 * n], refs[2 * n:3 * n]
        outs = refs[3 * n:]
        chip = _chip_index(lax.axis_index("x"), lax.axis_index("y"))
        for k in range(n):
            rows, cols = w_refs[k].shape
            if k == conv_w_index:
                g = s_ref[starts[k]:starts[k] + rows, pl.ds(pl.multiple_of(chip * cols, LANES), cols)]
            else:
                g = s_ref[starts[k]:starts[k] + rows, 0:cols]
            m_new = ADAM_B1 * m_refs[k][...] + (1.0 - ADAM_B1) * g
            v_new = ADAM_B2 * v_refs[k][...] + (1.0 - ADAM_B2) * (g * g)
            outs[4 * k][...] = g
            outs[4 * k + 1][...] = -ADAM_LR * ((m_new / c1) / (jnp.sqrt(v_new / c2) + ADAM_EPS) + ADAM_WD * w_refs[k][...])
            outs[4 * k + 2][...] = m_new
            outs[4 * k + 3][...] = v_new

    vm = pl.BlockSpec(memory_space=pltpu.VMEM)
    outs = pl.pallas_call(
        body, in_specs=[vm] * (1 + 3 * n), out_specs=[vm] * (4 * n),
        out_shape=[jax.ShapeDtypeStruct(a.shape, F32) for a in ws for _ in range(4)],
        name="adamw_small")(summed, *ws, *ms, *vs)
    return [outs[4 * k:4 * k + 4] for k in range(n)]


def kernel(x, mem, g_mix, w_in, conv_w, conv_b, dt_bias, a_log, d_skip, ssm_norm_w, g_q, g_k, f_bias, w_out, g_xattn, g_mem, xq_w, xkv_w, xg_q, xg_k, xo_w, g_mlp, w_up, w_down, loss_target, m_g_mix, m_w_in, m_conv_w, m_conv_b, m_dt_bias, m_a_log, m_d_skip, m_ssm_norm_w, m_g_q, m_g_k, m_f_bias, m_w_out, m_g_xattn, m_g_mem, m_xq_w, m_xkv_w, m_xg_q, m_xg_k, m_xo_w, m_g_mlp, m_w_up, m_w_down, v_g_mix, v_w_in, v_conv_w, v_conv_b, v_dt_bias, v_a_log, v_d_skip, v_ssm_norm_w, v_g_q, v_g_k, v_f_bias, v_w_out, v_g_xattn, v_g_mem, v_xq_w, v_xkv_w, v_xg_q, v_xg_k, v_xo_w, v_g_mlp, v_w_up, v_w_down):
    args = (x, mem, g_mix, w_in, conv_w, conv_b, dt_bias, a_log, d_skip, ssm_norm_w, g_q, g_k, f_bias, w_out, g_xattn,
            g_mem, xq_w, xkv_w, xg_q, xg_k, xo_w, g_mlp, w_up, w_down)
    w = dict(zip(_INPUTS, args))
    mom1 = dict(zip(_WEIGHTS, (m_g_mix, m_w_in, m_conv_w, m_conv_b, m_dt_bias, m_a_log, m_d_skip, m_ssm_norm_w, m_g_q,
                               m_g_k, m_f_bias, m_w_out, m_g_xattn, m_g_mem, m_xq_w, m_xkv_w, m_xg_q, m_xg_k, m_xo_w,
                               m_g_mlp, m_w_up, m_w_down)))
    mom2 = dict(zip(_WEIGHTS, (v_g_mix, v_w_in, v_conv_w, v_conv_b, v_dt_bias, v_a_log, v_d_skip, v_ssm_norm_w, v_g_q,
                               v_g_k, v_f_bias, v_w_out, v_g_xattn, v_g_mem, v_xq_w, v_xkv_w, v_xg_q, v_xg_k, v_xo_w,
                               v_g_mlp, v_w_up, v_w_down)))
    chip = _chip_index(lax.axis_index("x"), lax.axis_index("y"))
    core = lax.axis_index("c")

    shard_bf = {n: w[n][0].astype(BF16) for n in _BIG}

    def layout_for_compute(n, g):
        if n == "w_in":
            return _w_in_from_shards(g)
        return g if n in _COL_SHARDED else g.reshape(N_CHIPS * g.shape[1], g.shape[2])

    def layout_for_reduction(n, g):
        if n == "w_in":
            g = _w_in_to_shards(g)
        elif n not in _COL_SHARDED:
            g = g.reshape(N_CHIPS, g.shape[0] // N_CHIPS, g.shape[1])
        return g.reshape(N_CHIPS, 2, g.shape[1] // 2, g.shape[2])

    def pair_sums_of(names, grads, tag):
        grads4 = [layout_for_reduction(n, grads[n]) for n in names]
        from_sibling = _sibling_send_halves(grads4, "rs_sibling_halves_" + tag)
        sums = []
        for n, g, fs in zip(names, grads4, from_sibling):
            mine = lax.dynamic_index_in_dim(g, core, axis=1, keepdims=False)
            flat = lambda a: a.reshape(-1, a.shape[-1])
            (s,) = _nsum([flat(mine), flat(fs)], (BF16,), "rs_pair_sum_" + n)
            sums.append(s.reshape(mine.shape))
        return sums

    def chip_sums_of(names, pair_sums, from_chips):
        out = []
        for n, ps, fc in zip(names, pair_sums, from_chips):
            own = lax.dynamic_index_in_dim(ps, chip, axis=0, keepdims=False)
            (r,) = _nsum([own, fc[0], fc[1], fc[2]], (F32,), "rs_chip_sum_" + n)
            out.append(r)
        return out

    halves_in = shard_bf["w_in"].reshape(2, shard_bf["w_in"].shape[0] // 2, -1)
    g_in, g_conv = _all_gather_chips([halves_in], [w["conv_w"][0]])
    g_in = lax.dynamic_update_index_in_dim(g_in, halves_in, chip, axis=0)
    g_conv = lax.dynamic_update_index_in_dim(g_conv, w["conv_w"][0], chip, axis=0)
    w_in_full = layout_for_compute("w_in", g_in.reshape(N_CHIPS, -1, g_in.shape[-1]))
    p = {n: w[n] for n in _SMALL}
    p["conv_w"] = g_conv.transpose(1, 0, 2).reshape(CONV_WIDTH, CONV_DIM)
    gather = _split_start("gather_late", [shard_bf[n] for n in _LATE], "gather", after=g_in)
    p["g_mix"] = p["g_mix"] + gather.token[:1, :1]

    def late_weights(after):
        srcs, lands = _split_wait("gather_late_wait", gather, "gather", after)
        lands = [lax.dynamic_update_index_in_dim(l, s, chip, axis=0) for l, s in zip(lands, srcs)]
        return {n: layout_for_compute(n, l) for n, l in zip(_LATE, lands)}

    scatter = {}

    def send_late_grads(grads):
        sums = pair_sums_of(_LATE, grads, "late")
        scatter["h"] = _split_start("scatter_late", sums, "scatter", after=None)
        return scatter["h"].token

    loss_row, dx, g_w_in, gp = _layer_fwd_bwd(x[0], mem[0], loss_target[0], w_in_full, p, late_weights, send_late_grads)

    sums_in = pair_sums_of(["w_in"], {"w_in": g_w_in}, "w_in")
    scatter_in = _split_start("scatter_w_in", sums_in, "scatter", after=None)

    grad, delta, new_m, new_v = {}, {}, {}, {}

    def finish(names, pair_sums, from_chips, tag):
        reduced = chip_sums_of(names, pair_sums, from_chips)
        for n, g, r in zip(names, _sibling_exchange(reduced, "rs_sibling_exchange_" + tag), reduced):
            shape = w[n].shape
            g2 = lax.dynamic_update_index_in_dim(g, r, core, axis=0).reshape(shape[1], shape[2])
            d, m1, v1 = _adamw(w[n][0], g2, mom1[n][0], mom2[n][0], "adamw_" + n)
            grad[n], delta[n], new_m[n], new_v[n] = (a.reshape(shape) for a in (g2, d, m1, v1))

    sums_late, from_chips_late = _split_wait("scatter_late_wait", scatter["h"], "scatter", (dx, scatter_in.token))
    finish(_LATE, sums_late, from_chips_late, "late")

    packed, starts = _pack_rows([gp[n] for n in _SMALL] + [loss_row], CONV_DIM)
    summed = _all_reduce_small(packed)
    loss = summed[starts[-1], 0]

    sums_in, from_chips_in = _split_wait("scatter_w_in_wait", scatter_in, "scatter",
                                         (summed, *[new_v[n] for n in _LATE]))
    finish(["w_in"], sums_in, from_chips_in, "w_in")

    as_rows = lambda a: a.reshape(-1, a.shape[-1])
    results = _adamw_small(summed, starts, [as_rows(w[n]) for n in _SMALL], [as_rows(mom1[n]) for n in _SMALL],
                           [as_rows(mom2[n]) for n in _SMALL], _SMALL.index("conv_w"))
    for n, res in zip(_SMALL, results):
        grad[n], delta[n], new_m[n], new_v[n] = (a.reshape(w[n].shape) for a in res)

    return (loss, dx[None], *[grad[n] for n in _WEIGHTS], *[delta[n] for n in _WEIGHTS],
            *[new_m[n] for n in _WEIGHTS], *[new_v[n] for n in _WEIGHTS])
```

```python
from typing import NamedTuple

import jax
import jax.numpy as jnp
from jax import lax
from jax.experimental import pallas as pl
from jax.experimental.pallas import tpu as pltpu

F32 = jnp.float32
BF16 = jnp.bfloat16
HI = lax.Precision.HIGHEST
MESH = pl.DeviceIdType.MESH

EPS = 1e-5
CHUNK = 128
SSM_HEADS = 16
SSM_GROUPS = 2
HEADS_PER_GROUP = SSM_HEADS // SSM_GROUPS
HEAD_DIM = 64
SSM_STATE = 128
ATTN_HEADS = 16
XATTN_HEADS = 4
XATTN_DIM = 256
CONV_WIDTH = 4
N_CHIPS = 4
N_DEV = 8
LANES = 128
VMEM_LIMIT = 56 * 1024 * 1024

ADAM_LR = 0.001
ADAM_B1 = 0.9
ADAM_B2 = 0.999
ADAM_EPS = 1e-08
ADAM_WD = 0.01
ADAM_STEP = 10


def _params(sem):
    return pltpu.CompilerParams(dimension_semantics=sem, vmem_limit_bytes=VMEM_LIMIT)


def _pick(n, cands):
    for c in cands:
        if n % c == 0:
            return c
    return n


def _mm(a, b, mode, name, out_dtypes=(F32,), epilogue=None, extras=(), b_chunks=1, out_chunks=1,
        tm=None, tn=None, tk=None):
    if mode == "nn":
        M, K = a.shape
        N = b.shape[-1] * b_chunks
    elif mode == "nt":
        M, K = a.shape
        N = b.shape[-2]
        assert b.shape[-1] * b_chunks == K
    else:
        K, M = a.shape
        N = b.shape[-1] * b_chunks
    tm = tm or _pick(M, (2048, 1024, 512, 256, 128))
    tn = tn or _pick(N // max(b_chunks if mode != "nt" else 1, out_chunks), (512, 640, 384, 256, 128))
    if tk is None:
        kmax = b.shape[-1] if mode == "nt" else K
        tk = kmax if kmax <= 2048 else _pick(kmax, (2048, 1152, 1024, 512))
    nk = K // tk
    assert M % tm == 0 and N % tn == 0 and K % tk == 0
    grid = (M // tm, N // tn, nk)

    if mode == "tn":
        a_spec = pl.BlockSpec((tk, tm), lambda i, j, k: (k, i))
    else:
        a_spec = pl.BlockSpec((tm, tk), lambda i, j, k: (i, k))

    def b_index(t_row, t_last, tile_last):
        if b_chunks == 1:
            return (t_row, t_last)
        q = (b.shape[-1]) // tile_last
        return (t_last // q, t_row, t_last % q)

    if mode == "nn" or mode == "tn":
        bshape = (tk, tn)
        bmap = lambda i, j, k: b_index(k, j, tn)
    else:
        bshape = (tn, tk)
        bmap = lambda i, j, k: b_index(j, k, tk)
    if b_chunks > 1:
        bshape = (None,) + bshape
    b_spec = pl.BlockSpec(bshape, bmap)

    if out_chunks == 1:
        o_spec = pl.BlockSpec((tm, tn), lambda i, j, k: (i, j))
        o_shape = (M, N)
    else:
        qo = (N // out_chunks) // tn
        o_spec = pl.BlockSpec((None, tm, tn), lambda i, j, k: (j // qo, i, j % qo))
        o_shape = (out_chunks, M, N // out_chunks)
    e_spec = pl.BlockSpec((tm, tn), lambda i, j, k: (i, j))

    dims = {"nn": (((1,), (0,)), ((), ())), "nt": (((1,), (1,)), ((), ())), "tn": (((0,), (0,)), ((), ()))}[mode]
    n_ex = len(extras)
    n_out = len(out_dtypes)

    def body(*refs):
        a_ref, b_ref = refs[0], refs[1]
        ex_refs = refs[2:2 + n_ex]
        o_refs = refs[2 + n_ex:2 + n_ex + n_out]

        def finish(acc):
            outs = epilogue(acc, *[r[...] for r in ex_refs]) if epilogue is not None else (acc,)
            for r, o in zip(o_refs, outs):
                r[...] = o.astype(r.dtype)

        part = lax.dot_general(a_ref[...].astype(BF16), b_ref[...].astype(BF16), dims,
                               preferred_element_type=F32)
        if nk == 1:
            finish(part)
        else:
            acc_ref = refs[-1]
            k = pl.program_id(2)

            @pl.when(k == 0)
            def _():
                acc_ref[...] = part

            @pl.when(k > 0)
            def _():
                acc_ref[...] += part

            @pl.when(k == nk - 1)
            def _():
                finish(acc_ref[...])

    outs = pl.pallas_call(
        body,
        grid=grid,
        in_specs=[a_spec, b_spec] + [e_spec] * n_ex,
        out_specs=[o_spec] * n_out,
        out_shape=[jax.ShapeDtypeStruct(o_shape, d) for d in out_dtypes],
        scratch_shapes=[pltpu.VMEM((tm, tn), F32)] if nk > 1 else [],
        compiler_params=_params(("parallel", "parallel", "arbitrary")),
        name=name,
    )(a, b, *extras)
    return outs[0] if n_out == 1 else outs


def _rms(x, g):
    r = lax.rsqrt(jnp.mean(x * x, axis=-1, keepdims=True) + EPS)
    return x * r * g


def _rmsnorm_fwd(x, g, name):
    R, D = x.shape
    tr = _pick(R, (512, 256))

    def body(x_ref, g_ref, o_ref):
        o_ref[...] = _rms(x_ref[...], g_ref[...]).astype(o_ref.dtype)

    return pl.pallas_call(
        body, grid=(R // tr,),
        in_specs=[pl.BlockSpec((tr, D), lambda i: (i, 0)), pl.BlockSpec((1, D), lambda i: (0, 0))],
        out_specs=pl.BlockSpec((tr, D), lambda i: (i, 0)),
        out_shape=jax.ShapeDtypeStruct((R, D), BF16),
        compiler_params=_params(("parallel",)), name=name)(x, g)


def _rmsnorm_bwd(x, g, dh, dres, name):
    R, D = x.shape
    tr = _pick(R, (256,))
    has_res = dres is not None

    def body(*refs):
        if has_res:
            x_ref, g_ref, dh_ref, dres_ref, dx_ref, dg_ref = refs
        else:
            x_ref, g_ref, dh_ref, dx_ref, dg_ref = refs
        _, vjp = jax.vjp(_rms, x_ref[...], g_ref[...])
        dx, dg = vjp(dh_ref[...])
        if has_res:
            dx = dx + dres_ref[...]
        dx_ref[...] = dx

        @pl.when(pl.program_id(0) == 0)
        def _():
            dg_ref[...] = jnp.zeros_like(dg_ref)

        dg_ref[...] += dg

    row = pl.BlockSpec((tr, D), lambda i: (i, 0))
    vec = pl.BlockSpec((1, D), lambda i: (0, 0))
    ins = [x, g, dh] + ([dres] if has_res else [])
    return pl.pallas_call(
        body, grid=(R // tr,),
        in_specs=[row, vec, row] + ([row] if has_res else []),
        out_specs=[row, vec],
        out_shape=[jax.ShapeDtypeStruct((R, D), F32), jax.ShapeDtypeStruct((1, D), F32)],
        compiler_params=_params(("arbitrary",)), name=name)(*ins)


def _shift_down(u, k):
    if k == 0:
        return u
    rows = lax.broadcasted_iota(jnp.int32, u.shape, 0)
    return jnp.where(rows >= k, pltpu.roll(u, k, axis=0), 0.0)


def _shift_up(u, k):
    if k == 0:
        return u
    n = u.shape[0]
    rows = lax.broadcasted_iota(jnp.int32, u.shape, 0)
    return jnp.where(rows < n - k, pltpu.roll(u, n - k, axis=0), 0.0)


def _conv_pre(u, w, b):
    pre = b
    for j in range(CONV_WIDTH):
        pre = pre + w[j:j + 1, :] * _shift_down(u, CONV_WIDTH - 1 - j)
    return pre


def _conv_fwd(proj, col0, ncols, conv_w, conv_b):
    S = proj.shape[0]
    cb0 = col0 // LANES

    def body(u_ref, w_ref, b_ref, o_ref):
        pre = _conv_pre(u_ref[...], w_ref[...], b_ref[...])
        o_ref[...] = pre * jax.nn.sigmoid(pre)

    return pl.pallas_call(
        body, grid=(ncols // LANES,),
        in_specs=[pl.BlockSpec((S, LANES), lambda j: (0, j + cb0)),
                  pl.BlockSpec((CONV_WIDTH, LANES), lambda j: (0, j)),
                  pl.BlockSpec((1, LANES), lambda j: (0, j))],
        out_specs=pl.BlockSpec((S, LANES), lambda j: (0, j)),
        out_shape=jax.ShapeDtypeStruct((S, ncols), F32),
        compiler_params=_params(("parallel",)), name="conv_fwd")(proj, conv_w, conv_b)


def _conv_bwd(proj, col0, ncols, conv_w, conv_b, douts, dproj):
    S = proj.shape[0]
    cb0 = col0 // LANES
    starts = [0]
    for d in douts:
        starts.append(starts[-1] + d.shape[1] // LANES)
    assert starts[-1] == ncols // LANES
    nd = len(douts)

    def body(u_ref, w_ref, b_ref, *rest):
        d_refs, (du_ref, dw_ref, db_ref) = rest[:nd], rest[nd + 1:]
        j = pl.program_id(0)
        dout = d_refs[-1][...]
        for i in range(nd - 2, -1, -1):
            dout = jnp.where(j < starts[i + 1], d_refs[i][...], dout)
        u = u_ref[...]
        w = w_ref[...]
        pre = _conv_pre(u, w, b_ref[...])
        s = jax.nn.sigmoid(pre)
        dpre = dout * (s * (1.0 + pre * (1.0 - s)))
        du = jnp.zeros_like(u)
        rows = []
        for j in range(CONV_WIDTH):
            k = CONV_WIDTH - 1 - j
            du = du + w[j:j + 1, :] * _shift_up(dpre, k)
            rows.append(jnp.sum(dpre * _shift_down(u, k), axis=0, keepdims=True))
        du_ref[...] = du.astype(du_ref.dtype)
        rows.append(jnp.zeros((8 - CONV_WIDTH, LANES), F32))
        dw_ref[...] = jnp.concatenate(rows, axis=0)
        db_ref[...] = jnp.sum(dpre, axis=0, keepdims=True)

    return pl.pallas_call(
        body, grid=(ncols // LANES,),
        in_specs=[pl.BlockSpec((S, LANES), lambda j: (0, j + cb0)),
                  pl.BlockSpec((CONV_WIDTH, LANES), lambda j: (0, j)),
                  pl.BlockSpec((1, LANES), lambda j: (0, j))]
        + [pl.BlockSpec((S, LANES), lambda j, lo=starts[i], hi=starts[i + 1]: (0, jnp.clip(j - lo, 0, hi - lo - 1)))
           for i in range(nd)] + [_ANY],
        out_specs=[pl.BlockSpec((S, LANES), lambda j: (0, j + cb0)),
                   pl.BlockSpec((8, LANES), lambda j: (0, j)),
                   pl.BlockSpec((1, LANES), lambda j: (0, j))],
        out_shape=[jax.ShapeDtypeStruct(dproj.shape, dproj.dtype),
                   jax.ShapeDtypeStruct((8, ncols), F32),
                   jax.ShapeDtypeStruct((1, ncols), F32)],
        input_output_aliases={3 + nd: 0},
        compiler_params=_params(("parallel",)), name="conv_bwd")(proj, conv_w, conv_b, *douts, dproj)


def _softplus(x):
    return jnp.maximum(x, 0.0) + jnp.log1p(jnp.exp(-jnp.abs(x)))


def _dot32(a, b, dims=(((1,), (0,)), ((), ()))):
    return lax.dot_general(a, b, dims, precision=HI, preferred_element_type=F32)


def _dotd(a, b, dims=(((1,), (0,)), ((), ()))):
    return lax.dot_general(a, b, dims, preferred_element_type=F32)


PAIRS_PER_GROUP = HEADS_PER_GROUP // 2


def _ssd_chunk(xs, Bm, Cm, z, dtr, dtb, alog, dsk, nw, h):
    L = Bm.shape[0]
    ri = lax.broadcasted_iota(jnp.int32, (L, L), 0)
    ci = lax.broadcasted_iota(jnp.int32, (L, L), 1)
    causal = ri >= ci
    tril = causal.astype(F32)
    first = _first_head(L)
    first1 = _first_head(1)
    CB = _dotd(Cm, Bm, _NT)
    gated, hnew = [], []
    ssq = jnp.zeros((L, 1), F32)
    for pp in range(len(xs)):
        dts, cums, tots, decay = [], [], [], []
        for a in range(2):
            r = 2 * pp + a
            dt = _softplus(dtr[r] + dtb[r])
            dA = dt * (-jnp.exp(alog[r]))
            acs = _dot32(tril, dA)
            cc = jnp.broadcast_to(acs, (L, L))
            decay.append(CB * jnp.exp(jnp.where(causal, cc - cc.T, -1e30)))
            dts.append(dt)
            cums.append(acs)
            tots.append(jnp.sum(dA, axis=0, keepdims=True))
        dt2 = jnp.where(first, dts[0], dts[1])
        acs2 = jnp.where(first, cums[0], cums[1])
        tot2 = jnp.where(first1, tots[0], tots[1])
        dsk2 = jnp.where(first1, dsk[2 * pp], dsk[2 * pp + 1])
        X = xs[pp] * dt2
        y = (jnp.where(first, _dotd(decay[0], X), _dotd(decay[1], X)) + jnp.exp(acs2) * _dotd(Cm, h[pp])
             + dsk2 * xs[pp])
        hnew.append(jnp.exp(tot2) * h[pp] + _dotd(Bm, X * jnp.exp(tot2 - acs2), _TN))
        g = y * (z[pp] * jax.nn.sigmoid(z[pp]))
        ssq = ssq + jnp.sum(g * g, axis=-1, keepdims=True)
        gated.append(g)
    rs = lax.rsqrt(ssq / (len(xs) * LANES) + EPS)
    return [g * rs * nw[pp] for pp, g in enumerate(gated)], hnew


def _ssd_args(xs_ref, b_ref, c_ref, z_ref, dt_ref, dtb_ref, al_ref, dsk_ref, nw_ref, h_ref):
    pairs = range(PAIRS_PER_GROUP)
    heads = range(HEADS_PER_GROUP)
    lanes = lambda ref, pp: ref[:, pp * LANES:(pp + 1) * LANES]
    return ([lanes(xs_ref, pp) for pp in pairs], b_ref[...], c_ref[...], [lanes(z_ref, pp) for pp in pairs],
            [dt_ref[r] for r in heads], [dtb_ref[r] for r in heads], [al_ref[r] for r in heads],
            [dsk_ref[r] for r in heads], [lanes(nw_ref, pp) for pp in pairs], [h_ref[pp] for pp in pairs])


def _ssd_specs(rev):
    H, N, L = HEADS_PER_GROUP, SSM_STATE, CHUNK
    gw = H * HEAD_DIM
    return dict(
        cols=lambda col0: pl.BlockSpec((L, gw), lambda g, c: (rev(c), col0 // gw + g)),
        bc=lambda first_block: pl.BlockSpec((L, N), lambda g, c: (rev(c), first_block + g)),
        dt=pl.BlockSpec((H, L, 1), lambda g, c: (g, rev(c), 0)),
        scal=pl.BlockSpec((H, 1, 1), lambda g, c: (g, 0, 0)),
        nw=pl.BlockSpec((1, gw), lambda g, c: (0, g)),
        hs=pl.BlockSpec((None, PAIRS_PER_GROUP, N, LANES), lambda g, c: (rev(c), g, 0, 0)),
        b_block=SSM_INNER // N,
    )


def _ssd_fwd(xbc, proj, dt_hm, dtb, alog, dsk, nw):
    S = xbc.shape[0]
    N, L = SSM_STATE, CHUNK
    nc = S // L
    sp = _ssd_specs(lambda c: c)

    def body(xs_ref, b_ref, c_ref, z_ref, dt_ref, dtb_ref, al_ref, dsk_ref, nw_ref, y_ref, hs_ref, h_ref):
        @pl.when(pl.program_id(1) == 0)
        def _():
            h_ref[...] = jnp.zeros_like(h_ref)

        hs_ref[...] = h_ref[...]
        out, hnew = _ssd_chunk(*_ssd_args(xs_ref, b_ref, c_ref, z_ref, dt_ref, dtb_ref, al_ref, dsk_ref, nw_ref, h_ref))
        for pp in range(PAIRS_PER_GROUP):
            y_ref[:, pp * LANES:(pp + 1) * LANES] = out[pp].astype(y_ref.dtype)
            h_ref[pp] = hnew[pp]

    return pl.pallas_call(
        body, grid=(SSM_GROUPS, nc),
        in_specs=[sp["cols"](0), sp["bc"](sp["b_block"]), sp["bc"](sp["b_block"] + SSM_GROUPS), sp["cols"](COL_Z),
                  sp["dt"], sp["scal"], sp["scal"], sp["scal"], sp["nw"]],
        out_specs=[sp["cols"](0), sp["hs"]],
        out_shape=[jax.ShapeDtypeStruct((S, SSM_INNER), BF16),
                   jax.ShapeDtypeStruct((nc, SSM_HEADS // 2, N, LANES), F32)],
        scratch_shapes=[pltpu.VMEM((PAIRS_PER_GROUP, N, LANES), F32)],
        compiler_params=_params(("parallel", "arbitrary")), name="ssd_fwd",
    )(xbc, xbc, xbc, proj, dt_hm, dtb, alog, dsk, nw)


def _ssd_bwd(xbc, proj, dt_hm, dtb, alog, dsk, nw, hs, dmixed, dproj):
    S = xbc.shape[0]
    N, L = SSM_STATE, CHUNK
    nc = S // L
    sp = _ssd_specs(lambda c: nc - 1 - c)

    def body(xs_ref, b_ref, c_ref, z_ref, dt_ref, dtb_ref, al_ref, dsk_ref, nw_ref, hs_ref, dy_ref, buf_ref,
             dxs_ref, dz_ref, db_ref, dc_ref, ddt_ref, ddtb_ref, dal_ref, ddsk_ref, dnw_ref, dh_ref):
        @pl.when(pl.program_id(1) == 0)
        def _():
            dh_ref[...] = jnp.zeros_like(dh_ref)
            ddtb_ref[...] = jnp.zeros_like(ddtb_ref)
            dal_ref[...] = jnp.zeros_like(dal_ref)
            ddsk_ref[...] = jnp.zeros_like(ddsk_ref)
            dnw_ref[...] = jnp.zeros_like(dnw_ref)

        pairs = range(PAIRS_PER_GROUP)
        lanes = lambda pp: slice(pp * LANES, (pp + 1) * LANES)
        _, vjp = jax.vjp(_ssd_chunk, *_ssd_args(xs_ref, b_ref, c_ref, z_ref, dt_ref, dtb_ref, al_ref, dsk_ref, nw_ref,
                                                hs_ref))
        dxs, dB, dC, dz, ddt, ddtb, dal, ddsk, dnw, dh = vjp(([dy_ref[:, lanes(pp)] for pp in pairs],
                                                              [dh_ref[pp] for pp in pairs]))
        db_ref[...] = dB
        dc_ref[...] = dC
        for pp in pairs:
            dxs_ref[:, lanes(pp)] = dxs[pp]
            dz_ref[:, lanes(pp)] = dz[pp].astype(dz_ref.dtype)
            dnw_ref[:, lanes(pp)] += dnw[pp]
            dh_ref[pp] = dh[pp]
        for r in range(HEADS_PER_GROUP):
            ddt_ref[r] = ddt[r]
            ddtb_ref[r] += ddtb[r]
            dal_ref[r] += dal[r]
            ddsk_ref[r] += ddsk[r]

    bc_out = pl.BlockSpec((L, N), lambda g, c: (nc - 1 - c, g))
    return pl.pallas_call(
        body, grid=(SSM_GROUPS, nc),
        in_specs=[sp["cols"](0), sp["bc"](sp["b_block"]), sp["bc"](sp["b_block"] + SSM_GROUPS), sp["cols"](COL_Z),
                  sp["dt"], sp["scal"], sp["scal"], sp["scal"], sp["nw"], sp["hs"], sp["cols"](0), _ANY],
        out_specs=[sp["cols"](0), sp["cols"](COL_Z), bc_out, bc_out, sp["dt"], sp["scal"], sp["scal"], sp["scal"],
                   sp["nw"]],
        input_output_aliases={11: 1},
        out_shape=[jax.ShapeDtypeStruct((S, SSM_INNER), F32), jax.ShapeDtypeStruct(dproj.shape, dproj.dtype),
                   jax.ShapeDtypeStruct((S, SSM_GROUPS * N), F32), jax.ShapeDtypeStruct((S, SSM_GROUPS * N), F32),
                   jax.ShapeDtypeStruct((SSM_HEADS, S, 1), F32),
                   jax.ShapeDtypeStruct((SSM_HEADS, 1, 1), F32), jax.ShapeDtypeStruct((SSM_HEADS, 1, 1), F32),
                   jax.ShapeDtypeStruct((SSM_HEADS, 1, 1), F32), jax.ShapeDtypeStruct((1, SSM_INNER), F32)],
        scratch_shapes=[pltpu.VMEM((PAIRS_PER_GROUP, N, LANES), F32)],
        compiler_params=_params(("parallel", "arbitrary")), name="ssd_bwd",
    )(xbc, xbc, xbc, proj, dt_hm, dtb, alog, dsk, nw, hs, dmixed, dproj)


ATTN_SCALE = HEAD_DIM ** -0.5
ATTN_PAIRS = ATTN_HEADS // 2


def _first_head(rows):
    return lax.broadcasted_iota(jnp.int32, (rows, LANES), 1) < HEAD_DIM


def _pair_norm(x, g2, scale):
    first = _first_head(x.shape[0])
    sq = x * x
    ms0 = jnp.sum(jnp.where(first, sq, 0.0), axis=-1, keepdims=True) * (1.0 / HEAD_DIM)
    ms1 = jnp.sum(jnp.where(first, 0.0, sq), axis=-1, keepdims=True) * (1.0 / HEAD_DIM)
    r = jnp.where(first, lax.rsqrt(ms0 + EPS), lax.rsqrt(ms1 + EPS))
    return x * r * g2 * scale


def _qk_prep_fwd(proj, gq2, gk2):
    S = proj.shape[0]
    tq = _pick(S, (512, 256))

    def body(q_ref, k_ref, v_ref, gq_ref, gk_ref, qo_ref, ko_ref, vo_ref):
        qo_ref[...] = _pair_norm(q_ref[...], gq_ref[...], ATTN_SCALE).astype(BF16)
        ko_ref[...] = _pair_norm(k_ref[...], gk_ref[...], 1.0).astype(BF16)
        vo_ref[...] = v_ref[...].astype(BF16)

    col = lambda c0: pl.BlockSpec((tq, LANES), lambda h, i: (i, c0 // LANES + h))
    blk = pl.BlockSpec((tq, LANES), lambda h, i: (i, h))
    vec = pl.BlockSpec((1, LANES), lambda h, i: (0, 0))
    return pl.pallas_call(
        body, grid=(ATTN_PAIRS, S // tq), in_specs=[col(COL_Q), col(COL_K), col(COL_V), vec, vec],
        out_specs=[blk, blk, blk], out_shape=[jax.ShapeDtypeStruct((S, ATTN_WIDTH), BF16)] * 3,
        compiler_params=_params(("parallel", "parallel")), name="qk_prep_fwd")(proj, proj, proj, gq2, gk2)


def _pair_norm_bwd(proj, col0, g2, scale, dn, dproj, name):
    S = proj.shape[0]
    tq = _pick(S, (512, 256))

    def body(u_ref, g_ref, dn_ref, buf_ref, du_ref, dg_ref):
        @pl.when((pl.program_id(0) == 0) & (pl.program_id(1) == 0))
        def _():
            dg_ref[...] = jnp.zeros_like(dg_ref)

        _, vjp = jax.vjp(lambda u, g: _pair_norm(u, g, scale), u_ref[...], g_ref[...])
        du, dg = vjp(dn_ref[...])
        du_ref[...] = du.astype(du_ref.dtype)
        dg_ref[...] += dg

    ublk = pl.BlockSpec((tq, LANES), lambda h, i: (i, col0 // LANES + h))
    blk = pl.BlockSpec((tq, LANES), lambda h, i: (i, h))
    vec = pl.BlockSpec((1, LANES), lambda h, i: (0, 0))
    return pl.pallas_call(
        body, grid=(ATTN_PAIRS, S // tq), in_specs=[ublk, vec, blk, _ANY], out_specs=[ublk, vec],
        out_shape=[jax.ShapeDtypeStruct(dproj.shape, dproj.dtype), jax.ShapeDtypeStruct((1, LANES), F32)],
        input_output_aliases={3: 0},
        compiler_params=_params(("arbitrary", "arbitrary")), name=name)(proj, g2, dn, dproj)


def _logf_cumsum_fwd(f_raw, f_bias):
    S, Hh = f_raw.shape
    L = CHUNK

    def body(f_ref, b_ref, o_ref, wide_ref):
        ri = lax.broadcasted_iota(jnp.int32, (L, L), 0)
        ci = lax.broadcasted_iota(jnp.int32, (L, L), 1)
        tril = (ri >= ci).astype(F32)
        carry = jnp.zeros((1, Hh), F32)
        for c in range(S // L):
            rows = slice(c * L, (c + 1) * L)
            lf = -_softplus(-(f_ref[rows, :] + b_ref[...]))
            cum = _dot32(tril, lf) + carry
            o_ref[rows, :] = cum
            for h in range(Hh):
                wide_ref[rows, h * HEAD_DIM:(h + 1) * HEAD_DIM] = jnp.broadcast_to(cum[:, h:h + 1], (L, HEAD_DIM))
            carry = cum[L - 1:L, :]

    return pl.pallas_call(
        body, out_shape=[jax.ShapeDtypeStruct((S, Hh), F32), jax.ShapeDtypeStruct((S, Hh * HEAD_DIM), F32)],
        name="logf_cumsum_fwd")(f_raw, f_bias)


def _logf_cumsum_bwd(f_raw, f_bias, dcum):
    S, Hh = f_raw.shape
    L = CHUNK

    def body(f_ref, b_ref, d_ref, df_ref, db_ref):
        ri = lax.broadcasted_iota(jnp.int32, (L, L), 0)
        ci = lax.broadcasted_iota(jnp.int32, (L, L), 1)
        triu = (ri <= ci).astype(F32)
        carry = jnp.zeros((1, Hh), F32)
        db = jnp.zeros((1, Hh), F32)
        for c in reversed(range(S // L)):
            suf = _dot32(triu, d_ref[c * L:(c + 1) * L, :]) + carry
            df = suf * jax.nn.sigmoid(-(f_ref[c * L:(c + 1) * L, :] + b_ref[...]))
            df_ref[c * L:(c + 1) * L, :] = df
            db = db + jnp.sum(df, axis=0, keepdims=True)
            carry = suf[0:1, :]
        db_ref[...] = db

    return pl.pallas_call(
        body, out_shape=[jax.ShapeDtypeStruct((S, Hh), F32), jax.ShapeDtypeStruct((1, Hh), F32)],
        name="logf_cumsum_bwd")(f_raw, f_bias, dcum)


_NT = (((1,), (1,)), ((), ()))
_TN = (((0,), (0,)), ((), ()))


def _mxu(a, b, dims=(((1,), (0,)), ((), ()))):
    return lax.dot_general(a, b, dims, preferred_element_type=F32)


def _flash_fwd(qs, kn, vb, cq, ck):
    S, W = qs.shape
    tq = tk = _pick(S, (512, 256))
    nmask = max(tq // tk, 1)

    def body(q_ref, k_ref, v_ref, cq_ref, ck_ref, o_ref, of_ref, lse_ref):
        i = pl.program_id(1)
        first = _first_head(tq)
        q2 = q_ref[...]
        zero = jnp.zeros_like(q2)
        qa = (jnp.where(first, q2, zero), jnp.where(first, zero, q2))
        cqa = (cq_ref[:, 0:1], cq_ref[:, HEAD_DIM:HEAD_DIM + 1])
        row0 = i * tq

        def step(j, carry, masked):
            ms, ls, acc, rem = carry
            off = pl.multiple_of(j * tk, tk)
            k = k_ref[pl.ds(off, tk), :]
            v = v_ref[pl.ds(off, tk), :]
            new_m, new_l, alphas, pvs, prs = [], [], [], [], []
            for a in range(2):
                s = _mxu(qa[a], k, _NT) + cqa[a] - ck_ref[a, :, pl.ds(off, tk)]
                if masked:
                    ri = lax.broadcasted_iota(jnp.int32, (tq, tk), 0) + row0
                    ci = lax.broadcasted_iota(jnp.int32, (tq, tk), 1) + off
                    s = jnp.where(ri >= ci, s, -1e30)
                m_new = jnp.maximum(ms[a], jnp.max(s, axis=-1, keepdims=True))
                alpha = jnp.exp(ms[a] - m_new)
                p = jnp.exp(s - m_new)
                new_l.append(alpha * ls[a] + jnp.sum(p, axis=-1, keepdims=True))
                new_m.append(m_new)
                alphas.append(alpha)
                p_hi = p.astype(BF16)
                pvs.append(_mxu(p_hi, v))
                prs.append(_mxu((p - p_hi.astype(F32)).astype(BF16), v))
            al = jnp.where(first, alphas[0], alphas[1])
            acc = al * acc + jnp.where(first, pvs[0], pvs[1])
            rem = al * rem + jnp.where(first, prs[0], prs[1])
            return tuple(new_m), tuple(new_l), acc, rem

        neg = jnp.full((tq, 1), -1e30, F32)
        z1 = jnp.zeros((tq, 1), F32)
        z2 = jnp.zeros((tq, LANES), F32)
        carry = ((neg, neg), (z1, z1), z2, z2)
        n_full = (i * tq) // tk
        carry = lax.fori_loop(0, n_full, lambda j, c: step(j, c, False), carry)
        for jj in range(nmask):
            carry = step(n_full + jj, carry, True)
        ms, ls, acc, rem = carry
        linv = jnp.where(first, 1.0 / ls[0], 1.0 / ls[1])
        o_ref[...] = (acc * linv).astype(o_ref.dtype)
        of_ref[...] = (acc + rem) * linv
        lse_ref[...] = jnp.where(first, ms[0] + jnp.log(ls[0]), ms[1] + jnp.log(ls[1]))

    qblk = pl.BlockSpec((tq, LANES), lambda h, i: (i, h))
    full = pl.BlockSpec((S, LANES), lambda h, i: (0, h))
    return pl.pallas_call(
        body, grid=(W // LANES, S // tq),
        in_specs=[qblk, full, full, qblk, pl.BlockSpec((2, 1, S), lambda h, i: (h, 0, 0))],
        out_specs=[qblk, qblk, qblk],
        out_shape=[jax.ShapeDtypeStruct((S, W), BF16), jax.ShapeDtypeStruct((S, W), F32),
                   jax.ShapeDtypeStruct((S, W), F32)],
        compiler_params=_params(("parallel", "parallel")), name="flash_fwd")(qs, kn, vb, cq, ck)


def _flash_bwd(qs, kn, vb, cq, ck, o_fine, do, do_col0, lse):
    S, W = qs.shape
    tq = tk = _pick(S, (512, 256))
    nq = S // tq
    nmask = max(tk // tq, 1)

    def body(q_ref, k_ref, v_ref, cq_ref, ck_ref, of_ref, do_ref, lse_ref, dq_ref, dk_ref, dv_ref, dck_ref):
        j = pl.program_id(1)

        @pl.when(j == 0)
        def _():
            dq_ref[...] = jnp.zeros_like(dq_ref)

        firstk = _first_head(tk)
        firstq = _first_head(tq)
        k2 = k_ref[...]
        v2 = v_ref[...]
        zk = jnp.zeros_like(k2)
        ka = (jnp.where(firstk, k2, zk), jnp.where(firstk, zk, k2))
        va = (jnp.where(firstk, v2, zk), jnp.where(firstk, zk, v2))
        cka = (ck_ref[0], ck_ref[1])
        col0 = j * tk

        def step(i, carry, masked):
            dk, dv, dck0, dck1 = carry
            dcks = [dck0, dck1]
            off = pl.multiple_of(i * tq, tq)
            rows = pl.ds(off, tq)
            q2 = q_ref[rows, :]
            dob = do_ref[rows, :].astype(BF16)
            prod = dob.astype(F32) * of_ref[rows, :]
            dkp, dvp, dqp = [], [], []
            for a in range(2):
                lane = pl.ds(a * HEAD_DIM, 1)
                s = _mxu(q2, ka[a], _NT) + cq_ref[rows, lane] - cka[a]
                if masked:
                    ri = lax.broadcasted_iota(jnp.int32, (tq, tk), 0) + off
                    ci = lax.broadcasted_iota(jnp.int32, (tq, tk), 1) + col0
                    s = jnp.where(ri >= ci, s, -1e30)
                p = jnp.exp(s - lse_ref[rows, lane])
                dp = _mxu(dob, va[a], _NT)
                own = jnp.where(firstq, prod, 0.0) if a == 0 else jnp.where(firstq, 0.0, prod)
                ds = p * (dp - jnp.sum(own, axis=-1, keepdims=True))
                dsb = ds.astype(BF16)
                dvp.append(_mxu(p.astype(BF16), dob, _TN))
                dkp.append(_mxu(dsb, q2, _TN))
                dqp.append(_mxu(dsb, k2))
                dcks[a] = dcks[a] - jnp.sum(ds, axis=0, keepdims=True)
            dq_ref[rows, :] += jnp.where(firstq, dqp[0], dqp[1])
            dk = dk + jnp.where(firstk, dkp[0], dkp[1])
            dv = dv + jnp.where(firstk, dvp[0], dvp[1])
            return dk, dv, dcks[0], dcks[1]

        z2 = jnp.zeros((tk, LANES), F32)
        z1 = jnp.zeros((1, tk), F32)
        carry = (z2, z2, z1, z1)
        i0 = (j * tk) // tq
        for ii in range(nmask):
            carry = step(i0 + ii, carry, True)
        dk, dv, dck0, dck1 = lax.fori_loop(i0 + nmask, nq, lambda i, c: step(i, c, False), carry)
        dk_ref[...] = dk
        dv_ref[...] = dv.astype(dv_ref.dtype)
        dck_ref[0] = dck0
        dck_ref[1] = dck1

    kblk = pl.BlockSpec((tk, LANES), lambda h, j: (j, h))
    full = pl.BlockSpec((S, LANES), lambda h, j: (0, h))
    dofull = pl.BlockSpec((S, LANES), lambda h, j: (0, do_col0 // LANES + h))
    rowt = pl.BlockSpec((2, 1, tk), lambda h, j: (h, 0, j))
    dvblk = pl.BlockSpec((tk, LANES), lambda h, j: (j, COL_V // LANES + h))
    return pl.pallas_call(
        body, grid=(W // LANES, S // tk),
        in_specs=[full, kblk, kblk, full, rowt, full, dofull, full],
        out_specs=[full, kblk, dvblk, rowt],
        out_shape=[jax.ShapeDtypeStruct((S, W), F32), jax.ShapeDtypeStruct((S, W), F32),
                   jax.ShapeDtypeStruct((S, IN_COLS_PAD), BF16), jax.ShapeDtypeStruct((2 * (W // LANES), 1, S), F32)],
        compiler_params=_params(("parallel", "arbitrary")), name="flash_bwd")(qs, kn, vb, cq, ck, o_fine, do, lse)


XATTN_SCALE = XATTN_DIM ** -0.5


def _xq_norm(q, g):
    return _rms(q, g) * XATTN_SCALE


def _xattn_fwd(xq, kv, gq, gk):
    S = xq.shape[0]
    Mm = kv.shape[0]
    Dh = XATTN_DIM
    tq = _pick(S, (512, 256))

    def body(q_ref, k_ref, v_ref, gq_ref, gk_ref, o_ref):
        qn = _xq_norm(q_ref[...], gq_ref[...]).astype(BF16)
        kn = _rms(k_ref[...], gk_ref[...]).astype(BF16)
        s = _mxu(qn, kn, _NT)
        m = jnp.max(s, axis=-1, keepdims=True)
        p = jnp.exp(s - m)
        l = jnp.sum(p, axis=-1, keepdims=True)
        o_ref[...] = (_mxu(p.astype(BF16), v_ref[...].astype(BF16)) / l).astype(o_ref.dtype)

    vec = pl.BlockSpec((1, Dh), lambda h, i: (0, 0))
    return pl.pallas_call(
        body, grid=(XATTN_HEADS, S // tq),
        in_specs=[pl.BlockSpec((tq, Dh), lambda h, i: (i, h)), pl.BlockSpec((Mm, Dh), lambda h, i: (0, h)),
                  pl.BlockSpec((Mm, Dh), lambda h, i: (0, XATTN_HEADS + h)), vec, vec],
        out_specs=pl.BlockSpec((tq, Dh), lambda h, i: (i, h)),
        out_shape=jax.ShapeDtypeStruct((S, XATTN_HEADS * Dh), BF16),
        compiler_params=_params(("parallel", "parallel")), name="xattn_fwd")(xq, kv, kv, gq, gk)


def _xattn_bwd(xq, kv, gq, gk, do):
    S = xq.shape[0]
    Mm = kv.shape[0]
    Dh = XATTN_DIM
    tq = _pick(S, (512, 256))
    nq = S // tq

    def body(q_ref, k_ref, v_ref, gq_ref, gk_ref, do_ref, dq_ref, dk_ref, dv_ref, dgq_ref, dgk_ref, dkn_acc, dv_acc):
        h = pl.program_id(0)
        i = pl.program_id(1)

        @pl.when((h == 0) & (i == 0))
        def _():
            dgq_ref[...] = jnp.zeros_like(dgq_ref)
            dgk_ref[...] = jnp.zeros_like(dgk_ref)

        @pl.when(i == 0)
        def _():
            dkn_acc[...] = jnp.zeros_like(dkn_acc)
            dv_acc[...] = jnp.zeros_like(dv_acc)

        qn32, vq = jax.vjp(_xq_norm, q_ref[...], gq_ref[...])
        kn32, vk = jax.vjp(_rms, k_ref[...], gk_ref[...])
        qn = qn32.astype(BF16)
        kn = kn32.astype(BF16)
        vb = v_ref[...].astype(BF16)
        s = _mxu(qn, kn, _NT)
        m = jnp.max(s, axis=-1, keepdims=True)
        p = jnp.exp(s - m)
        p = p / jnp.sum(p, axis=-1, keepdims=True)
        dob = do_ref[...].astype(BF16)
        dp = _mxu(dob, vb, _NT)
        delta = jnp.sum(p * dp, axis=-1, keepdims=True)
        ds = (p * (dp - delta)).astype(BF16)
        dv_acc[...] += _mxu(p.astype(BF16), dob, _TN)
        dkn_acc[...] += _mxu(ds, qn, _TN)
        dq, dgq = vq(_mxu(ds, kn))
        dq_ref[...] = dq.astype(dq_ref.dtype)
        dgq_ref[...] += dgq

        @pl.when(i == nq - 1)
        def _():
            dk, dgk = vk(dkn_acc[...])
            dk_ref[...] = dk.astype(dk_ref.dtype)
            dv_ref[...] = dv_acc[...].astype(dv_ref.dtype)
            dgk_ref[...] += dgk

    vec = pl.BlockSpec((1, Dh), lambda h, i: (0, 0))
    qblk = pl.BlockSpec((tq, Dh), lambda h, i: (i, h))
    kblk = pl.BlockSpec((Mm, Dh), lambda h, i: (0, h))
    vblk = pl.BlockSpec((Mm, Dh), lambda h, i: (0, XATTN_HEADS + h))
    return pl.pallas_call(
        body, grid=(XATTN_HEADS, nq),
        in_specs=[qblk, kblk, vblk, vec, vec, qblk],
        out_specs=[qblk, kblk, kblk, vec, vec],
        out_shape=[jax.ShapeDtypeStruct((S, XATTN_HEADS * Dh), BF16),
                   jax.ShapeDtypeStruct((Mm, XATTN_HEADS * Dh), BF16),
                   jax.ShapeDtypeStruct((Mm, XATTN_HEADS * Dh), BF16),
                   jax.ShapeDtypeStruct((1, Dh), F32), jax.ShapeDtypeStruct((1, Dh), F32)],
        scratch_shapes=[pltpu.VMEM((Mm, Dh), F32), pltpu.VMEM((Mm, Dh), F32)],
        compiler_params=_params(("arbitrary", "arbitrary")), name="xattn_bwd")(xq, kv, kv, gq, gk, do)


def _loss_head(y, target):
    S, D = y.shape
    tr = _pick(S, (512, 256))

    def body(y_ref, t_ref, dy_ref, loss_ref):
        @pl.when(pl.program_id(0) == 0)
        def _():
            loss_ref[...] = jnp.zeros_like(loss_ref)

        err = y_ref[...] - t_ref[...]
        dy_ref[...] = err * (1.0 / D)
        loss_ref[...] += jnp.sum(err * err) * (0.5 / D)

    row = pl.BlockSpec((tr, D), lambda i: (i, 0))
    return pl.pallas_call(
        body, grid=(S // tr,), in_specs=[row, row],
        out_specs=[row, pl.BlockSpec((1, LANES), lambda i: (0, 0))],
        out_shape=[jax.ShapeDtypeStruct((S, D), F32), jax.ShapeDtypeStruct((1, LANES), F32)],
        compiler_params=_params(("arbitrary",)), name="loss_head")(y, target)


def _row_tile(R, C):
    for tr in (1024, 512, 256, 128, 64, 32, 16, 8):
        if R % tr == 0 and tr * C * 4 <= (1 << 20):
            return tr
    return R


def _chip_sum(own, from_chips, name):
    R, C = own.shape
    tr = _row_tile(R, C)

    def body(own_ref, a_ref, b_ref, c_ref, o_ref):
        o_ref[...] = ((own_ref[...].astype(F32) + a_ref[...].astype(F32)) + b_ref[...].astype(F32)) + c_ref[...].astype(F32)

    blk = pl.BlockSpec((tr, C), lambda i: (i, 0))
    slab = lambda s: pl.BlockSpec((None, tr, C), lambda i: (s, i, 0))
    return pl.pallas_call(
        body, grid=(R // tr,), in_specs=[blk, slab(0), slab(1), slab(2)], out_specs=blk,
        out_shape=jax.ShapeDtypeStruct((R, C), F32),
        compiler_params=_params(("parallel",)), name=name)(own, from_chips, from_chips, from_chips)


def _adamw(w, g_mine, g_sibling, m, v, name):
    R, C = w.shape
    tr = _row_tile(R, C)
    c1 = 1.0 - ADAM_B1 ** ADAM_STEP
    c2 = 1.0 - ADAM_B2 ** ADAM_STEP

    def body(w_ref, ga_ref, gb_ref, m_ref, v_ref, g_ref, d_ref, mo_ref, vo_ref):
        g_t = ga_ref[...] + gb_ref[...]
        m_new = ADAM_B1 * m_ref[...] + (1.0 - ADAM_B1) * g_t
        v_new = ADAM_B2 * v_ref[...] + (1.0 - ADAM_B2) * (g_t * g_t)
        g_ref[...] = g_t
        d_ref[...] = -ADAM_LR * ((m_new / c1) / (jnp.sqrt(v_new / c2) + ADAM_EPS) + ADAM_WD * w_ref[...])
        mo_ref[...] = m_new
        vo_ref[...] = v_new

    blk = pl.BlockSpec((tr, C), lambda i: (i, 0))
    return pl.pallas_call(
        body, grid=(R // tr,), in_specs=[blk] * 5, out_specs=[blk] * 4,
        out_shape=[jax.ShapeDtypeStruct((R, C), F32)] * 4,
        compiler_params=_params(("parallel",)), name=name)(w, g_mine, g_sibling, m, v)


D_MODEL = 1024
SSM_INNER = SSM_HEADS * HEAD_DIM
CONV_DIM = SSM_INNER + 2 * SSM_GROUPS * SSM_STATE
ATTN_WIDTH = ATTN_HEADS * HEAD_DIM
COL_Z = 0
COL_XBC = COL_Z + SSM_INNER
COL_Q = COL_XBC + CONV_DIM
COL_K = COL_Q + ATTN_WIDTH
COL_V = COL_K + ATTN_WIDTH
COL_DT = COL_V + ATTN_WIDTH
COL_F = COL_DT + SSM_HEADS
IN_COLS = COL_F + ATTN_HEADS
IN_COLS_PAD = -(-IN_COLS // LANES) * LANES
REF_COL_DT = COL_Q
SHARD_COLS = IN_COLS // N_CHIPS
_COL_RANGES = ((0, REF_COL_DT, 0), (REF_COL_DT + SSM_HEADS, COL_F, COL_Q), (REF_COL_DT, REF_COL_DT + SSM_HEADS, COL_DT),
               (COL_F, IN_COLS, COL_F))


def _w_in_from_shards(g):
    parts = []
    for lo, hi, _ in _COL_RANGES:
        while lo < hi:
            j = lo // SHARD_COLS
            end = min(hi, (j + 1) * SHARD_COLS)
            parts.append(g[j][:, lo - j * SHARD_COLS:end - j * SHARD_COLS])
            lo = end
    parts.append(jnp.zeros((g.shape[1], IN_COLS_PAD - IN_COLS), g.dtype))
    return jnp.concatenate(parts, axis=1)


def _w_in_to_shards(w):
    shards = []
    for j in range(N_CHIPS):
        parts = []
        for lo, hi, here in sorted(_COL_RANGES):
            a, b = max(lo, j * SHARD_COLS), min(hi, (j + 1) * SHARD_COLS)
            if a < b:
                parts.append(w[:, here + a - lo:here + b - lo])
        shards.append(jnp.concatenate(parts, axis=1))
    return jnp.stack(shards)


def _add_residual(acc, res):
    return (res + acc,)


def _relu2(acc):
    r = jnp.maximum(acc, 0.0)
    return acc, r * r


def _relu2_bwd(acc, a):
    return (acc * (2.0 * jnp.maximum(a, 0.0)),)


def _layer_fwd_bwd(x, mem, target, w_in, p, late_weights, send_late_grads, send_w_in_grad):
    S = x.shape[0]
    hd3 = lambda a: a.reshape(SSM_HEADS, 1, 1)

    h1 = _rmsnorm_fwd(x, p["g_mix"], "norm_mix")
    proj = _mm(h1, w_in, "nn", "in_proj")
    xbc = _conv_fwd(proj, COL_XBC, CONV_DIM, p["conv_w"], p["conv_b"])
    dt_hm = proj[:, COL_DT:COL_DT + SSM_HEADS].T[:, :, None]
    ssd_par = (hd3(p["dt_bias"]), hd3(p["a_log"]), hd3(p["d_skip"]), p["ssm_norm_w"])
    y, hs = _ssd_fwd(xbc, proj, dt_hm, *ssd_par)
    f_raw = proj[:, COL_F:COL_F + ATTN_HEADS]
    gq2 = jnp.tile(p["g_q"], (1, 2))
    gk2 = jnp.tile(p["g_k"], (1, 2))
    qs, kn, vb = _qk_prep_fwd(proj, gq2, gk2)
    cum, cq = _logf_cumsum_fwd(f_raw, p["f_bias"])
    ck = cum.T[:, None, :]
    o, o_fine, lse = _flash_fwd(qs, kn, vb, cq, ck)
    W = late_weights((o_fine, y))
    x1 = _mm(y, W["w_out"][:SSM_INNER], "nn", "out_proj_ssm", epilogue=_add_residual, extras=(x,))
    x1 = _mm(o, W["w_out"][SSM_INNER:], "nn", "out_proj_attn", epilogue=_add_residual, extras=(x1,))
    h2 = _rmsnorm_fwd(x1, p["g_xattn"], "norm_xattn")
    mem_n = _rmsnorm_fwd(mem, p["g_mem"], "norm_mem")
    xq = _mm(h2, W["xq_w"], "nn", "xq_proj")
    kv = _mm(mem_n, W["xkv_w"], "nn", "xkv_proj", b_chunks=N_CHIPS)
    xo = _xattn_fwd(xq, kv, p["xg_q"], p["xg_k"])
    x2 = _mm(xo, W["xo_w"], "nn", "xo_proj", epilogue=_add_residual, extras=(x1,))
    h3 = _rmsnorm_fwd(x2, p["g_mlp"], "norm_mlp")
    a, act = _mm(h3, W["w_up"], "nn", "mlp_up", out_dtypes=(F32, BF16), epilogue=_relu2, b_chunks=N_CHIPS)
    x3 = _mm(act, W["w_down"], "nn", "mlp_down", epilogue=_add_residual, extras=(x2,))
    dy, loss_row = _loss_head(x3, target)

    gW, gp = {}, {}
    da = _mm(dy, W["w_down"], "nt", "d_act", out_dtypes=(BF16,), epilogue=_relu2_bwd, extras=(a,))
    gW["w_down"] = _mm(act, dy, "tn", "g_w_down", out_dtypes=(BF16,))
    gW["w_up"] = _mm(h3, da, "tn", "g_w_up", out_dtypes=(BF16,), out_chunks=N_CHIPS)
    dh3 = _mm(da, W["w_up"], "nt", "d_h3", b_chunks=N_CHIPS)
    dx2, gp["g_mlp"] = _rmsnorm_bwd(x2, p["g_mlp"], dh3, dy, "norm_mlp_bwd")
    dxo = _mm(dx2, W["xo_w"], "nt", "d_xo", out_dtypes=(BF16,))
    gW["xo_w"] = _mm(xo, dx2, "tn", "g_xo_w", out_dtypes=(BF16,))
    dxq, dk_x, dv_x, gp["xg_q"], gp["xg_k"] = _xattn_bwd(xq, kv, p["xg_q"], p["xg_k"], dxo)
    dkv = jnp.concatenate([dk_x, dv_x], axis=-1)
    gW["xq_w"] = _mm(h2, dxq, "tn", "g_xq_w", out_dtypes=(BF16,))
    dh2 = _mm(dxq, W["xq_w"], "nt", "d_h2")
    gW["xkv_w"] = _mm(mem_n, dkv, "tn", "g_xkv_w", out_dtypes=(BF16,), out_chunks=N_CHIPS)
    dmem_n = _mm(dkv, W["xkv_w"], "nt", "d_mem_n", b_chunks=N_CHIPS)
    _, gp["g_mem"] = _rmsnorm_bwd(mem, p["g_mem"], dmem_n, None, "norm_mem_bwd")
    dx1, gp["g_xattn"] = _rmsnorm_bwd(x1, p["g_xattn"], dh2, dx2, "norm_xattn_bwd")
    dmixed = _mm(dx1, W["w_out"], "nt", "d_mixed")
    gW["w_out"] = jnp.concatenate([_mm(y, dx1, "tn", "g_w_out_ssm", out_dtypes=(BF16,)),
                                   _mm(o, dx1, "tn", "g_w_out_attn", out_dtypes=(BF16,))], axis=0)
    token = send_late_grads(gW)
    dqs, dkn, dproj, dck = _flash_bwd(qs, kn, vb, cq, ck + token[:1, :1], o_fine, dmixed, SSM_INNER, lse)
    dproj, dgq2 = _pair_norm_bwd(proj, COL_Q, gq2, ATTN_SCALE, dqs, dproj, "q_norm_bwd")
    dproj, dgk2 = _pair_norm_bwd(proj, COL_K, gk2, 1.0, dkn, dproj, "k_norm_bwd")
    gp["g_q"] = dgq2[:, :HEAD_DIM] + dgq2[:, HEAD_DIM:]
    gp["g_k"] = dgk2[:, :HEAD_DIM] + dgk2[:, HEAD_DIM:]
    df, gp["f_bias"] = _logf_cumsum_bwd(f_raw, p["f_bias"], dck[:, 0, :].T)
    dxs, dproj, dB, dC, ddt, ddtb, dalog, ddsk, gp["ssm_norm_w"] = _ssd_bwd(xbc, proj, dt_hm, *ssd_par, hs, dmixed, dproj)
    gp["dt_bias"] = ddtb.reshape(1, SSM_HEADS)
    gp["a_log"] = dalog.reshape(1, SSM_HEADS)
    gp["d_skip"] = ddsk.reshape(1, SSM_HEADS)
    dproj, dconv_w, gp["conv_b"] = _conv_bwd(proj, COL_XBC, CONV_DIM, p["conv_w"], p["conv_b"], (dxs, dB, dC), dproj)
    gp["conv_w"] = dconv_w[:CONV_WIDTH]
    tail = jnp.concatenate([ddt[:, :, 0].T, df, jnp.zeros((S, IN_COLS_PAD - IN_COLS), F32)], axis=-1).astype(BF16)
    dproj = lax.dynamic_update_slice(dproj, tail, (0, COL_DT))
    token = send_w_in_grad(_mm(h1, dproj, "tn", "g_w_in", out_dtypes=(BF16,)))
    dh1 = _mm(dproj, w_in, "nt", "d_h1")
    dx, gp["g_mix"] = _rmsnorm_bwd(x, p["g_mix"] + token[:1, :1], dh1, dx1, "norm_mix_bwd")
    return loss_row, dx, gp


_ANY = pl.BlockSpec(memory_space=pl.ANY)


def _place():
    x, y, c = lax.axis_index("x"), lax.axis_index("y"), lax.axis_index("c")
    chips = [(1 - x, y), (x, 1 - y), (1 - x, 1 - y)]
    return x, y, c, chips


def _chip_index(px, py):
    return 2 * px + py


def _all_gather_chips(split, whole):
    ns, nw = len(split), len(whole)
    n = ns + nw

    def body(*refs):
        ins, outs = refs[:n], refs[n:2 * n]
        send_ici, recv_ici, send_d2d, recv_d2d = refs[2 * n:]
        x, y, c, chips = _place()
        me = _chip_index(x, y)
        sib = (x, y, 1 - c)

        def ici(k, j, src, dst):
            return pltpu.make_async_remote_copy(src_ref=src, dst_ref=dst, send_sem=send_ici.at[3 * k + j],
                                                recv_sem=recv_ici.at[3 * k + j], device_id=(*chips[j], c),
                                                device_id_type=MESH)

        def d2d(k, j, piece):
            return pltpu.make_async_remote_copy(src_ref=piece, dst_ref=piece, send_sem=send_d2d.at[3 * k + j],
                                                recv_sem=recv_d2d.at[3 * k + j], device_id=sib, device_id_type=MESH)

        sends = []
        for k in range(n):
            for j in range(3):
                if k < ns:
                    sends.append(ici(k, j, ins[k].at[c], outs[k].at[me, c]))
                else:
                    sends.append(ici(k, j, ins[k], outs[k].at[me]))
                sends[-1].start()
        passed = []
        for k in range(n):
            for j in range(3):
                src_chip = _chip_index(*chips[j])
                if k < ns:
                    ici(k, j, ins[k].at[c], outs[k].at[src_chip, c]).wait_recv()
                    passed.append(d2d(k, j, outs[k].at[src_chip, c]))
                    passed[-1].start()
                else:
                    ici(k, j, ins[k], outs[k].at[src_chip]).wait_recv()
        for k in range(ns):
            for j in range(3):
                d2d(k, j, outs[k].at[_chip_index(*chips[j]), 1 - c]).wait_recv()
        for cp in sends + passed:
            cp.wait_send()

    arrs = list(split) + list(whole)
    return pl.pallas_call(
        body, in_specs=[_ANY] * n, out_specs=[_ANY] * n,
        out_shape=[jax.ShapeDtypeStruct((N_CHIPS,) + a.shape, a.dtype) for a in arrs],
        scratch_shapes=[pltpu.SemaphoreType.DMA((3 * n,)), pltpu.SemaphoreType.DMA((3 * n,)),
                        pltpu.SemaphoreType.DMA((3 * ns,)), pltpu.SemaphoreType.DMA((3 * ns,))],
        name="all_gather_chips")(*arrs)


def _sibling_swap(arrs, name):
    n = len(arrs)

    def body(*refs):
        ins, outs = refs[:n], refs[n:2 * n]
        send_sem, recv_sem = refs[2 * n:]
        x, y, c, _ = _place()
        copies = [pltpu.make_async_remote_copy(src_ref=ins[k], dst_ref=outs[k], send_sem=send_sem.at[k],
                                               recv_sem=recv_sem.at[k], device_id=(x, y, 1 - c), device_id_type=MESH)
                  for k in range(n)]
        for q in copies:
            q.start()
        for q in copies:
            q.wait()

    return pl.pallas_call(
        body, in_specs=[_ANY] * n, out_specs=[_ANY] * n,
        out_shape=[jax.ShapeDtypeStruct(a.shape, a.dtype) for a in arrs],
        scratch_shapes=[pltpu.SemaphoreType.DMA((n,)), pltpu.SemaphoreType.DMA((n,))],
        name=name)(*arrs)


_HBM = pl.BlockSpec(memory_space=pltpu.HBM)
_SEM = pl.BlockSpec(memory_space=pltpu.SEMAPHORE)
_SPLIT_EFFECT = pltpu.SideEffectType.DATAFLOW_SIDE_EFFECTING


class _Split(NamedTuple):
    send_sems: jax.Array
    recv_sems: jax.Array
    sources: tuple
    lands: tuple
    token: jax.Array


def _split_copies(kind, srcs, lands, send_sems, recv_sems):
    x, y, c, chips = _place()
    me = _chip_index(x, y)
    copies = []
    for k in range(len(srcs)):
        for j in range(3):
            if kind == "gather":
                src, dst = srcs[k], lands[k].at[me]
            else:
                src, dst = srcs[k].at[_chip_index(*chips[j])], lands[k].at[j]
            copies.append(pltpu.make_async_remote_copy(
                src_ref=src, dst_ref=dst, send_sem=send_sems.at[3 * k + j], recv_sem=recv_sems.at[3 * k + j],
                device_id=(*chips[j], c), device_id_type=MESH))
    return copies


def _split_start(name, sources, kind, after):
    n = len(sources)
    if kind == "gather":
        lands = [lax.empty((N_CHIPS,) + s.shape, s.dtype) for s in sources]
    else:
        lands = [lax.empty((3,) + s.shape[1:], s.dtype) for s in sources]
    deps = [] if after is None else [after]

    def body(*refs):
        srcs, lnds = refs[:n], refs[n:2 * n]
        send_sems, recv_sems = refs[2 * n + len(deps)], refs[2 * n + len(deps) + 1]
        for cp in _split_copies(kind, srcs, lnds, send_sems, recv_sems):
            cp.start()
        refs[-1][...] = jnp.zeros_like(refs[-1])

    hbm = lambda a: pltpu.with_memory_space_constraint(a, pltpu.HBM)
    outs = pl.pallas_call(
        body, name=name,
        in_specs=[_HBM] * (2 * n) + [_ANY] * len(deps),
        out_specs=[_SEM, _SEM] + [_HBM] * (2 * n) + [pl.BlockSpec(memory_space=pltpu.VMEM)],
        out_shape=[pltpu.SemaphoreType.DMA((3 * n,)), pltpu.SemaphoreType.DMA((3 * n,))]
        + [pltpu.HBM(a.shape, a.dtype) for a in list(sources) + lands] + [jax.ShapeDtypeStruct((8, LANES), F32)],
        input_output_aliases={k: 2 + k for k in range(2 * n)},
        compiler_params=pltpu.CompilerParams(has_side_effects=_SPLIT_EFFECT),
    )(*[hbm(s) for s in sources], *[hbm(l) for l in lands], *deps)
    return _Split(outs[0], outs[1], tuple(outs[2:2 + n]), tuple(outs[2 + n:2 + 2 * n]), outs[-1])


def _split_wait(name, h, kind, after):
    n = len(h.sources)

    def body(*refs):
        srcs, lnds = refs[:n], refs[n:2 * n]
        for cp in _split_copies(kind, srcs, lnds, refs[2 * n], refs[2 * n + 1]):
            cp.wait_send()
            cp.wait_recv()

    outs = pl.pallas_call(
        body, name=name,
        in_specs=[_HBM] * (2 * n) + [_SEM, _SEM] + [_ANY] * len(after),
        out_specs=[_HBM] * (2 * n),
        out_shape=[pltpu.HBM(a.shape, a.dtype) for a in h.sources + h.lands],
        input_output_aliases={k: k for k in range(2 * n)},
        compiler_params=pltpu.CompilerParams(has_side_effects=_SPLIT_EFFECT),
    )(*h.sources, *h.lands, h.send_sems, h.recv_sems, *after)
    return outs[:n], outs[n:]


def _all_reduce_small(vec):
    R, C = vec.shape

    def body(v_ref, o_ref, buf, send_sem, recv_sem):
        x, y, c = lax.axis_index("x"), lax.axis_index("y"), lax.axis_index("c")
        me = 4 * x + 2 * y + c
        buf[me] = v_ref[...]
        copies = []
        for r in range(1, N_DEV):
            fx, fy, fc = (r >> 2) & 1, (r >> 1) & 1, r & 1
            peer = (x ^ fx, y ^ fy, c ^ fc)
            copies.append(pltpu.make_async_remote_copy(src_ref=v_ref, dst_ref=buf.at[me], send_sem=send_sem.at[r - 1],
                                                       recv_sem=recv_sem.at[r - 1], device_id=peer, device_id_type=MESH))
        for q in copies:
            q.start()
        for r in range(1, N_DEV):
            fx, fy, fc = (r >> 2) & 1, (r >> 1) & 1, r & 1
            src = 4 * (x ^ fx) + 2 * (y ^ fy) + (c ^ fc)
            pltpu.make_async_remote_copy(src_ref=v_ref, dst_ref=buf.at[src], send_sem=send_sem.at[r - 1],
                                         recv_sem=recv_sem.at[r - 1], device_id=(x, y, c), device_id_type=MESH).wait_recv()
        acc = buf[0]
        for d in range(1, N_DEV):
            acc = acc + buf[d]
        o_ref[...] = acc
        for q in copies:
            q.wait_send()

    vm = pl.BlockSpec(memory_space=pltpu.VMEM)
    return pl.pallas_call(
        body, in_specs=[vm], out_specs=vm, out_shape=jax.ShapeDtypeStruct((R, C), F32),
        scratch_shapes=[pltpu.VMEM((N_DEV, R, C), F32), pltpu.SemaphoreType.DMA((N_DEV - 1,)),
                        pltpu.SemaphoreType.DMA((N_DEV - 1,))],
        name="all_reduce_small")(vec)


_INPUTS = ["x", "mem", "g_mix", "w_in", "conv_w", "conv_b", "dt_bias", "a_log", "d_skip", "ssm_norm_w", "g_q", "g_k",
           "f_bias", "w_out", "g_xattn", "g_mem", "xq_w", "xkv_w", "xg_q", "xg_k", "xo_w", "g_mlp", "w_up", "w_down"]
_WEIGHTS = _INPUTS[2:]
_BIG = ["w_in", "w_out", "xq_w", "xkv_w", "xo_w", "w_up", "w_down"]
_LATE = _BIG[1:]
_COL_SHARDED = ["w_in", "xkv_w", "w_up"]
_SMALL = [n for n in _WEIGHTS if n not in _BIG]


def _pack_rows(arrs, width):
    starts, r = [], 0
    for a in arrs:
        starts.append(r)
        r += a.shape[0]
    out = jnp.concatenate([jnp.pad(a, ((0, 0), (0, width - a.shape[1]))) for a in arrs], axis=0)
    return jnp.pad(out, ((0, -r % 8), (0, 0))), starts


def _adamw_small(summed, starts, ws, ms, vs, conv_w_index):
    n = len(ws)
    c1 = 1.0 - ADAM_B1 ** ADAM_STEP
    c2 = 1.0 - ADAM_B2 ** ADAM_STEP

    def body(s_ref, *refs):
        w_refs, m_refs, v_refs = refs[:n], refs[n:2 * n], refs[2 * n:3 * n]
        outs = refs[3 * n:]
        chip = _chip_index(lax.axis_index("x"), lax.axis_index("y"))
        for k in range(n):
            rows, cols = w_refs[k].shape
            if k == conv_w_index:
                g = s_ref[starts[k]:starts[k] + rows, pl.ds(pl.multiple_of(chip * cols, LANES), cols)]
            else:
                g = s_ref[starts[k]:starts[k] + rows, 0:cols]
            m_new = ADAM_B1 * m_refs[k][...] + (1.0 - ADAM_B1) * g
            v_new = ADAM_B2 * v_refs[k][...] + (1.0 - ADAM_B2) * (g * g)
            outs[4 * k][...] = g
            outs[4 * k + 1][...] = -ADAM_LR * ((m_new / c1) / (jnp.sqrt(v_new / c2) + ADAM_EPS) + ADAM_WD * w_refs[k][...])
            outs[4 * k + 2][...] = m_new
            outs[4 * k + 3][...] = v_new

    vm = pl.BlockSpec(memory_space=pltpu.VMEM)
    outs = pl.pallas_call(
        body, in_specs=[vm] * (1 + 3 * n), out_specs=[vm] * (4 * n),
        out_shape=[jax.ShapeDtypeStruct(a.shape, F32) for a in ws for _ in range(4)],
        name="adamw_small")(summed, *ws, *ms, *vs)
    return [outs[4 * k:4 * k + 4] for k in range(n)]


def kernel(x, mem, g_mix, w_in, conv_w, conv_b, dt_bias, a_log, d_skip, ssm_norm_w, g_q, g_k, f_bias, w_out, g_xattn, g_mem, xq_w, xkv_w, xg_q, xg_k, xo_w, g_mlp, w_up, w_down, loss_target, m_g_mix, m_w_in, m_conv_w, m_conv_b, m_dt_bias, m_a_log, m_d_skip, m_ssm_norm_w, m_g_q, m_g_k, m_f_bias, m_w_out, m_g_xattn, m_g_mem, m_xq_w, m_xkv_w, m_xg_q, m_xg_k, m_xo_w, m_g_mlp, m_w_up, m_w_down, v_g_mix, v_w_in, v_conv_w, v_conv_b, v_dt_bias, v_a_log, v_d_skip, v_ssm_norm_w, v_g_q, v_g_k, v_f_bias, v_w_out, v_g_xattn, v_g_mem, v_xq_w, v_xkv_w, v_xg_q, v_xg_k, v_xo_w, v_g_mlp, v_w_up, v_w_down):
    args = (x, mem, g_mix, w_in, conv_w, conv_b, dt_bias, a_log, d_skip, ssm_norm_w, g_q, g_k, f_bias, w_out, g_xattn,
            g_mem, xq_w, xkv_w, xg_q, xg_k, xo_w, g_mlp, w_up, w_down)
    w = dict(zip(_INPUTS, args))
    mom1 = dict(zip(_WEIGHTS, (m_g_mix, m_w_in, m_conv_w, m_conv_b, m_dt_bias, m_a_log, m_d_skip, m_ssm_norm_w, m_g_q,
                               m_g_k, m_f_bias, m_w_out, m_g_xattn, m_g_mem, m_xq_w, m_xkv_w, m_xg_q, m_xg_k, m_xo_w,
                               m_g_mlp, m_w_up, m_w_down)))
    mom2 = dict(zip(_WEIGHTS, (v_g_mix, v_w_in, v_conv_w, v_conv_b, v_dt_bias, v_a_log, v_d_skip, v_ssm_norm_w, v_g_q,
                               v_g_k, v_f_bias, v_w_out, v_g_xattn, v_g_mem, v_xq_w, v_xkv_w, v_xg_q, v_xg_k, v_xo_w,
                               v_g_mlp, v_w_up, v_w_down)))
    chip = _chip_index(lax.axis_index("x"), lax.axis_index("y"))

    shard_bf = {n: w[n][0].astype(BF16) for n in _BIG}

    def layout_for_compute(n, g):
        if n == "w_in":
            return _w_in_from_shards(g)
        return g if n in _COL_SHARDED else g.reshape(N_CHIPS * g.shape[1], g.shape[2])

    def layout_for_reduction(n, g):
        if n == "w_in":
            return _w_in_to_shards(g)
        return g if n in _COL_SHARDED else g.reshape(N_CHIPS, g.shape[0] // N_CHIPS, g.shape[1])

    halves_in = shard_bf["w_in"].reshape(2, shard_bf["w_in"].shape[0] // 2, -1)
    g_in, g_conv = _all_gather_chips([halves_in], [w["conv_w"][0]])
    g_in = lax.dynamic_update_index_in_dim(g_in, halves_in, chip, axis=0)
    g_conv = lax.dynamic_update_index_in_dim(g_conv, w["conv_w"][0], chip, axis=0)
    w_in_full = layout_for_compute("w_in", g_in.reshape(N_CHIPS, -1, g_in.shape[-1]))
    p = {n: w[n] for n in _SMALL}
    p["conv_w"] = g_conv.transpose(1, 0, 2).reshape(CONV_WIDTH, CONV_DIM)
    gather = _split_start("gather_late", [shard_bf[n] for n in _LATE], "gather", after=g_in)
    p["g_mix"] = p["g_mix"] + gather.token[:1, :1]

    def late_weights(after):
        srcs, lands = _split_wait("gather_late_wait", gather, "gather", after)
        lands = [lax.dynamic_update_index_in_dim(l, s, chip, axis=0) for l, s in zip(lands, srcs)]
        return {n: layout_for_compute(n, l) for n, l in zip(_LATE, lands)}

    scatter = {}

    def send_late_grads(grads):
        scatter["late"] = _split_start("scatter_late", [layout_for_reduction(n, grads[n]) for n in _LATE], "scatter",
                                       after=None)
        return scatter["late"].token

    def send_w_in_grad(g):
        scatter["w_in"] = _split_start("scatter_w_in", [layout_for_reduction("w_in", g)], "scatter", after=None)
        return scatter["w_in"].token

    loss_row, dx, gp = _layer_fwd_bwd(x[0], mem[0], loss_target[0], w_in_full, p, late_weights, send_late_grads,
                                      send_w_in_grad)

    grad, delta, new_m, new_v = {}, {}, {}, {}

    def finish(names, sources, from_chips, tag):
        mine = [_chip_sum(lax.dynamic_index_in_dim(s, chip, axis=0, keepdims=False), fc, "rs_chip_sum_" + n)
                for n, s, fc in zip(names, sources, from_chips)]
        for n, a, b in zip(names, mine, _sibling_swap(mine, "rs_sibling_swap_" + tag)):
            shape = w[n].shape
            res = _adamw(w[n][0], a, b, mom1[n][0], mom2[n][0], "adamw_" + n)
            grad[n], delta[n], new_m[n], new_v[n] = (r.reshape(shape) for r in res)

    finish(_LATE, *_split_wait("scatter_late_wait", scatter["late"], "scatter", (dx,)), "late")

    packed, starts = _pack_rows([gp[n] for n in _SMALL] + [loss_row], CONV_DIM)
    summed = _all_reduce_small(packed)
    loss = summed[starts[-1], 0]

    finish(["w_in"], *_split_wait("scatter_w_in_wait", scatter["w_in"], "scatter",
                                  (summed, *[new_v[n] for n in _LATE])), "w_in")

    as_rows = lambda a: a.reshape(-1, a.shape[-1])
    results = _adamw_small(summed, starts, [as_rows(w[n]) for n in _SMALL], [as_rows(mom1[n]) for n in _SMALL],
                           [as_rows(mom2[n]) for n in _SMALL], _SMALL.index("conv_w"))
    for n, res in zip(_SMALL, results):
        grad[n], delta[n], new_m[n], new_v[n] = (a.reshape(w[n].shape) for a in res)

    return (loss, dx[None], *[grad[n] for n in _WEIGHTS], *[delta[n] for n in _WEIGHTS],
            *[new_m[n] for n in _WEIGHTS], *[new_v[n] for n in _WEIGHTS])
```

```python
from typing import NamedTuple

import jax
import jax.numpy as jnp
from jax import lax
from jax.experimental import pallas as pl
from jax.experimental.pallas import tpu as pltpu

F32 = jnp.float32
BF16 = jnp.bfloat16
HI = lax.Precision.HIGHEST
MESH = pl.DeviceIdType.MESH

EPS = 1e-5
CHUNK = 128
SSM_HEADS = 16
SSM_GROUPS = 2
HEADS_PER_GROUP = SSM_HEADS // SSM_GROUPS
HEAD_DIM = 64
SSM_STATE = 128
ATTN_HEADS = 16
XATTN_HEADS = 4
XATTN_DIM = 256
CONV_WIDTH = 4
N_CHIPS = 4
N_DEV = 8
LANES = 128
VMEM_LIMIT = 56 * 1024 * 1024

ADAM_LR = 0.001
ADAM_B1 = 0.9
ADAM_B2 = 0.999
ADAM_EPS = 1e-08
ADAM_WD = 0.01
ADAM_STEP = 10


def _params(sem):
    return pltpu.CompilerParams(dimension_semantics=sem, vmem_limit_bytes=VMEM_LIMIT)


def _pick(n, cands):
    for c in cands:
        if n % c == 0:
            return c
    return n


def _mm(a, b, mode, name, out_dtypes=(F32,), epilogue=None, extras=(), b_chunks=1, out_chunks=1,
        tm=None, tn=None, tk=None):
    if mode == "nn":
        M, K = a.shape
        N = b.shape[-1] * b_chunks
    elif mode == "nt":
        M, K = a.shape
        N = b.shape[-2]
        assert b.shape[-1] * b_chunks == K
    else:
        K, M = a.shape
        N = b.shape[-1] * b_chunks
    tm = tm or _pick(M, (2048, 1024, 512, 256, 128))
    tn = tn or _pick(N // max(b_chunks if mode != "nt" else 1, out_chunks), (512, 640, 384, 256, 128))
    if tk is None:
        kmax = b.shape[-1] if mode == "nt" else K
        tk = kmax if kmax <= 2048 else _pick(kmax, (2048, 1152, 1024, 512))
    nk = K // tk
    assert M % tm == 0 and N % tn == 0 and K % tk == 0
    grid = (M // tm, N // tn, nk)

    if mode == "tn":
        a_spec = pl.BlockSpec((tk, tm), lambda i, j, k: (k, i))
    else:
        a_spec = pl.BlockSpec((tm, tk), lambda i, j, k: (i, k))

    def b_index(t_row, t_last, tile_last):
        if b_chunks == 1:
            return (t_row, t_last)
        q = (b.shape[-1]) // tile_last
        return (t_last // q, t_row, t_last % q)

    if mode == "nn" or mode == "tn":
        bshape = (tk, tn)
        bmap = lambda i, j, k: b_index(k, j, tn)
    else:
        bshape = (tn, tk)
        bmap = lambda i, j, k: b_index(j, k, tk)
    if b_chunks > 1:
        bshape = (None,) + bshape
    b_spec = pl.BlockSpec(bshape, bmap)

    if out_chunks == 1:
        o_spec = pl.BlockSpec((tm, tn), lambda i, j, k: (i, j))
        o_shape = (M, N)
    else:
        qo = (N // out_chunks) // tn
        o_spec = pl.BlockSpec((None, tm, tn), lambda i, j, k: (j // qo, i, j % qo))
        o_shape = (out_chunks, M, N // out_chunks)
    e_spec = pl.BlockSpec((tm, tn), lambda i, j, k: (i, j))

    dims = {"nn": (((1,), (0,)), ((), ())), "nt": (((1,), (1,)), ((), ())), "tn": (((0,), (0,)), ((), ()))}[mode]
    n_ex = len(extras)
    n_out = len(out_dtypes)

    def body(*refs):
        a_ref, b_ref = refs[0], refs[1]
        ex_refs = refs[2:2 + n_ex]
        o_refs = refs[2 + n_ex:2 + n_ex + n_out]

        def finish(acc):
            outs = epilogue(acc, *[r[...] for r in ex_refs]) if epilogue is not None else (acc,)
            for r, o in zip(o_refs, outs):
                r[...] = o.astype(r.dtype)

        part = lax.dot_general(a_ref[...].astype(BF16), b_ref[...].astype(BF16), dims,
                               preferred_element_type=F32)
        if nk == 1:
            finish(part)
        else:
            acc_ref = refs[-1]
            k = pl.program_id(2)

            @pl.when(k == 0)
            def _():
                acc_ref[...] = part

            @pl.when(k > 0)
            def _():
                acc_ref[...] += part

            @pl.when(k == nk - 1)
            def _():
                finish(acc_ref[...])

    outs = pl.pallas_call(
        body,
        grid=grid,
        in_specs=[a_spec, b_spec] + [e_spec] * n_ex,
        out_specs=[o_spec] * n_out,
        out_shape=[jax.ShapeDtypeStruct(o_shape, d) for d in out_dtypes],
        scratch_shapes=[pltpu.VMEM((tm, tn), F32)] if nk > 1 else [],
        compiler_params=_params(("parallel", "parallel", "arbitrary")),
        name=name,
    )(a, b, *extras)
    return outs[0] if n_out == 1 else outs


def _rms(x, g):
    r = lax.rsqrt(jnp.mean(x * x, axis=-1, keepdims=True) + EPS)
    return x * r * g


def _rmsnorm_fwd(x, g, name):
    R, D = x.shape
    tr = _pick(R, (512, 256))

    def body(x_ref, g_ref, o_ref):
        o_ref[...] = _rms(x_ref[...], g_ref[...]).astype(o_ref.dtype)

    return pl.pallas_call(
        body, grid=(R // tr,),
        in_specs=[pl.BlockSpec((tr, D), lambda i: (i, 0)), pl.BlockSpec((1, D), lambda i: (0, 0))],
        out_specs=pl.BlockSpec((tr, D), lambda i: (i, 0)),
        out_shape=jax.ShapeDtypeStruct((R, D), BF16),
        compiler_params=_params(("parallel",)), name=name)(x, g)


def _rmsnorm_bwd(x, g, dh, dres, name):
    R, D = x.shape
    tr = _pick(R, (256,))
    has_res = dres is not None

    def body(*refs):
        if has_res:
            x_ref, g_ref, dh_ref, dres_ref, dx_ref, dg_ref = refs
        else:
            x_ref, g_ref, dh_ref, dx_ref, dg_ref = refs
        _, vjp = jax.vjp(_rms, x_ref[...], g_ref[...])
        dx, dg = vjp(dh_ref[...])
        if has_res:
            dx = dx + dres_ref[...]
        dx_ref[...] = dx

        @pl.when(pl.program_id(0) == 0)
        def _():
            dg_ref[...] = jnp.zeros_like(dg_ref)

        dg_ref[...] += dg

    row = pl.BlockSpec((tr, D), lambda i: (i, 0))
    vec = pl.BlockSpec((1, D), lambda i: (0, 0))
    ins = [x, g, dh] + ([dres] if has_res else [])
    return pl.pallas_call(
        body, grid=(R // tr,),
        in_specs=[row, vec, row] + ([row] if has_res else []),
        out_specs=[row, vec],
        out_shape=[jax.ShapeDtypeStruct((R, D), F32), jax.ShapeDtypeStruct((1, D), F32)],
        compiler_params=_params(("arbitrary",)), name=name)(*ins)


def _shift_down(u, k):
    if k == 0:
        return u
    rows = lax.broadcasted_iota(jnp.int32, u.shape, 0)
    return jnp.where(rows >= k, pltpu.roll(u, k, axis=0), 0.0)


def _shift_up(u, k):
    if k == 0:
        return u
    n = u.shape[0]
    rows = lax.broadcasted_iota(jnp.int32, u.shape, 0)
    return jnp.where(rows < n - k, pltpu.roll(u, n - k, axis=0), 0.0)


def _conv_pre(u, w, b):
    pre = b
    for j in range(CONV_WIDTH):
        pre = pre + w[j:j + 1, :] * _shift_down(u, CONV_WIDTH - 1 - j)
    return pre


def _conv_fwd(proj, col0, ncols, conv_w, conv_b):
    S = proj.shape[0]
    cb0 = col0 // LANES

    def body(u_ref, w_ref, b_ref, o_ref):
        pre = _conv_pre(u_ref[...], w_ref[...], b_ref[...])
        o_ref[...] = pre * jax.nn.sigmoid(pre)

    return pl.pallas_call(
        body, grid=(ncols // LANES,),
        in_specs=[pl.BlockSpec((S, LANES), lambda j: (0, j + cb0)),
                  pl.BlockSpec((CONV_WIDTH, LANES), lambda j: (0, j)),
                  pl.BlockSpec((1, LANES), lambda j: (0, j))],
        out_specs=pl.BlockSpec((S, LANES), lambda j: (0, j)),
        out_shape=jax.ShapeDtypeStruct((S, ncols), F32),
        compiler_params=_params(("parallel",)), name="conv_fwd")(proj, conv_w, conv_b)


def _conv_bwd(proj, col0, ncols, conv_w, conv_b, douts, dproj):
    S = proj.shape[0]
    cb0 = col0 // LANES
    starts = [0]
    for d in douts:
        starts.append(starts[-1] + d.shape[1] // LANES)
    assert starts[-1] == ncols // LANES
    nd = len(douts)

    def body(u_ref, w_ref, b_ref, *rest):
        d_refs, (du_ref, dw_ref, db_ref) = rest[:nd], rest[nd + 1:]
        j = pl.program_id(0)
        dout = d_refs[-1][...]
        for i in range(nd - 2, -1, -1):
            dout = jnp.where(j < starts[i + 1], d_refs[i][...], dout)
        u = u_ref[...]
        w = w_ref[...]
        pre = _conv_pre(u, w, b_ref[...])
        s = jax.nn.sigmoid(pre)
        dpre = dout * (s * (1.0 + pre * (1.0 - s)))
        du = jnp.zeros_like(u)
        rows = []
        for j in range(CONV_WIDTH):
            k = CONV_WIDTH - 1 - j
            du = du + w[j:j + 1, :] * _shift_up(dpre, k)
            rows.append(jnp.sum(dpre * _shift_down(u, k), axis=0, keepdims=True))
        du_ref[...] = du.astype(du_ref.dtype)
        rows.append(jnp.zeros((8 - CONV_WIDTH, LANES), F32))
        dw_ref[...] = jnp.concatenate(rows, axis=0)
        db_ref[...] = jnp.sum(dpre, axis=0, keepdims=True)

    return pl.pallas_call(
        body, grid=(ncols // LANES,),
        in_specs=[pl.BlockSpec((S, LANES), lambda j: (0, j + cb0)),
                  pl.BlockSpec((CONV_WIDTH, LANES), lambda j: (0, j)),
                  pl.BlockSpec((1, LANES), lambda j: (0, j))]
        + [pl.BlockSpec((S, LANES), lambda j, lo=starts[i], hi=starts[i + 1]: (0, jnp.clip(j - lo, 0, hi - lo - 1)))
           for i in range(nd)] + [_ANY],
        out_specs=[pl.BlockSpec((S, LANES), lambda j: (0, j + cb0)),
                   pl.BlockSpec((8, LANES), lambda j: (0, j)),
                   pl.BlockSpec((1, LANES), lambda j: (0, j))],
        out_shape=[jax.ShapeDtypeStruct(dproj.shape, dproj.dtype),
                   jax.ShapeDtypeStruct((8, ncols), F32),
                   jax.ShapeDtypeStruct((1, ncols), F32)],
        input_output_aliases={3 + nd: 0},
        compiler_params=_params(("parallel",)), name="conv_bwd")(proj, conv_w, conv_b, *douts, dproj)


def _softplus(x):
    return jnp.maximum(x, 0.0) + jnp.log1p(jnp.exp(-jnp.abs(x)))


def _dot32(a, b, dims=(((1,), (0,)), ((), ()))):
    return lax.dot_general(a, b, dims, precision=HI, preferred_element_type=F32)


def _dotd(a, b, dims=(((1,), (0,)), ((), ()))):
    return lax.dot_general(a, b, dims, preferred_element_type=F32)


PAIRS_PER_GROUP = HEADS_PER_GROUP // 2


def _ssd_chunk(xs, Bm, Cm, z, dtr, dtb, alog, dsk, nw, h):
    L = Bm.shape[0]
    ri = lax.broadcasted_iota(jnp.int32, (L, L), 0)
    ci = lax.broadcasted_iota(jnp.int32, (L, L), 1)
    causal = ri >= ci
    tril = causal.astype(F32)
    first = _first_head(L)
    first1 = _first_head(1)
    CB = _dotd(Cm, Bm, _NT)
    gated, hnew = [], []
    ssq = jnp.zeros((L, 1), F32)
    for pp in range(len(xs)):
        dts, cums, tots, decay = [], [], [], []
        for a in range(2):
            r = 2 * pp + a
            dt = _softplus(dtr[r] + dtb[r])
            dA = dt * (-jnp.exp(alog[r]))
            acs = _dot32(tril, dA)
            cc = jnp.broadcast_to(acs, (L, L))
            decay.append(CB * jnp.exp(jnp.where(causal, cc - cc.T, -1e30)))
            dts.append(dt)
            cums.append(acs)
            tots.append(jnp.sum(dA, axis=0, keepdims=True))
        dt2 = jnp.where(first, dts[0], dts[1])
        acs2 = jnp.where(first, cums[0], cums[1])
        tot2 = jnp.where(first1, tots[0], tots[1])
        dsk2 = jnp.where(first1, dsk[2 * pp], dsk[2 * pp + 1])
        X = xs[pp] * dt2
        y = (jnp.where(first, _dotd(decay[0], X), _dotd(decay[1], X)) + jnp.exp(acs2) * _dotd(Cm, h[pp])
             + dsk2 * xs[pp])
        hnew.append(jnp.exp(tot2) * h[pp] + _dotd(Bm, X * jnp.exp(tot2 - acs2), _TN))
        g = y * (z[pp] * jax.nn.sigmoid(z[pp]))
        ssq = ssq + jnp.sum(g * g, axis=-1, keepdims=True)
        gated.append(g)
    rs = lax.rsqrt(ssq / (len(xs) * LANES) + EPS)
    return [g * rs * nw[pp] for pp, g in enumerate(gated)], hnew


def _ssd_args(xs_ref, b_ref, c_ref, z_ref, dt_ref, dtb_ref, al_ref, dsk_ref, nw_ref, h_ref):
    pairs = range(PAIRS_PER_GROUP)
    heads = range(HEADS_PER_GROUP)
    lanes = lambda ref, pp: ref[:, pp * LANES:(pp + 1) * LANES]
    return ([lanes(xs_ref, pp) for pp in pairs], b_ref[...], c_ref[...], [lanes(z_ref, pp) for pp in pairs],
            [dt_ref[r] for r in heads], [dtb_ref[r] for r in heads], [al_ref[r] for r in heads],
            [dsk_ref[r] for r in heads], [lanes(nw_ref, pp) for pp in pairs], [h_ref[pp] for pp in pairs])


def _ssd_specs(rev):
    H, N, L = HEADS_PER_GROUP, SSM_STATE, CHUNK
    gw = H * HEAD_DIM
    return dict(
        cols=lambda col0: pl.BlockSpec((L, gw), lambda g, c: (rev(c), col0 // gw + g)),
        bc=lambda first_block: pl.BlockSpec((L, N), lambda g, c: (rev(c), first_block + g)),
        dt=pl.BlockSpec((H, L, 1), lambda g, c: (g, rev(c), 0)),
        scal=pl.BlockSpec((H, 1, 1), lambda g, c: (g, 0, 0)),
        nw=pl.BlockSpec((1, gw), lambda g, c: (0, g)),
        hs=pl.BlockSpec((None, PAIRS_PER_GROUP, N, LANES), lambda g, c: (rev(c), g, 0, 0)),
        b_block=SSM_INNER // N,
    )


def _ssd_fwd(xbc, proj, dt_hm, dtb, alog, dsk, nw):
    S = xbc.shape[0]
    N, L = SSM_STATE, CHUNK
    nc = S // L
    sp = _ssd_specs(lambda c: c)

    def body(xs_ref, b_ref, c_ref, z_ref, dt_ref, dtb_ref, al_ref, dsk_ref, nw_ref, y_ref, hs_ref, h_ref):
        @pl.when(pl.program_id(1) == 0)
        def _():
            h_ref[...] = jnp.zeros_like(h_ref)

        hs_ref[...] = h_ref[...]
        out, hnew = _ssd_chunk(*_ssd_args(xs_ref, b_ref, c_ref, z_ref, dt_ref, dtb_ref, al_ref, dsk_ref, nw_ref, h_ref))
        for pp in range(PAIRS_PER_GROUP):
            y_ref[:, pp * LANES:(pp + 1) * LANES] = out[pp].astype(y_ref.dtype)
            h_ref[pp] = hnew[pp]

    return pl.pallas_call(
        body, grid=(SSM_GROUPS, nc),
        in_specs=[sp["cols"](0), sp["bc"](sp["b_block"]), sp["bc"](sp["b_block"] + SSM_GROUPS), sp["cols"](COL_Z),
                  sp["dt"], sp["scal"], sp["scal"], sp["scal"], sp["nw"]],
        out_specs=[sp["cols"](0), sp["hs"]],
        out_shape=[jax.ShapeDtypeStruct((S, SSM_INNER), BF16),
                   jax.ShapeDtypeStruct((nc, SSM_HEADS // 2, N, LANES), F32)],
        scratch_shapes=[pltpu.VMEM((PAIRS_PER_GROUP, N, LANES), F32)],
        compiler_params=_params(("parallel", "arbitrary")), name="ssd_fwd",
    )(xbc, xbc, xbc, proj, dt_hm, dtb, alog, dsk, nw)


def _ssd_bwd(xbc, proj, dt_hm, dtb, alog, dsk, nw, hs, dmixed, dproj):
    S = xbc.shape[0]
    N, L = SSM_STATE, CHUNK
    nc = S // L
    sp = _ssd_specs(lambda c: nc - 1 - c)

    def body(xs_ref, b_ref, c_ref, z_ref, dt_ref, dtb_ref, al_ref, dsk_ref, nw_ref, hs_ref, dy_ref, buf_ref,
             dxs_ref, dz_ref, db_ref, dc_ref, ddt_ref, ddtb_ref, dal_ref, ddsk_ref, dnw_ref, dh_ref):
        @pl.when(pl.program_id(1) == 0)
        def _():
            dh_ref[...] = jnp.zeros_like(dh_ref)
            ddtb_ref[...] = jnp.zeros_like(ddtb_ref)
            dal_ref[...] = jnp.zeros_like(dal_ref)
            ddsk_ref[...] = jnp.zeros_like(ddsk_ref)
            dnw_ref[...] = jnp.zeros_like(dnw_ref)

        pairs = range(PAIRS_PER_GROUP)
        lanes = lambda pp: slice(pp * LANES, (pp + 1) * LANES)
        _, vjp = jax.vjp(_ssd_chunk, *_ssd_args(xs_ref, b_ref, c_ref, z_ref, dt_ref, dtb_ref, al_ref, dsk_ref, nw_ref,
                                                hs_ref))
        dxs, dB, dC, dz, ddt, ddtb, dal, ddsk, dnw, dh = vjp(([dy_ref[:, lanes(pp)] for pp in pairs],
                                                              [dh_ref[pp] for pp in pairs]))
        db_ref[...] = dB
        dc_ref[...] = dC
        for pp in pairs:
            dxs_ref[:, lanes(pp)] = dxs[pp]
            dz_ref[:, lanes(pp)] = dz[pp].astype(dz_ref.dtype)
            dnw_ref[:, lanes(pp)] += dnw[pp]
            dh_ref[pp] = dh[pp]
        for r in range(HEADS_PER_GROUP):
            ddt_ref[r] = ddt[r]
            ddtb_ref[r] += ddtb[r]
            dal_ref[r] += dal[r]
            ddsk_ref[r] += ddsk[r]

    bc_out = pl.BlockSpec((L, N), lambda g, c: (nc - 1 - c, g))
    return pl.pallas_call(
        body, grid=(SSM_GROUPS, nc),
        in_specs=[sp["cols"](0), sp["bc"](sp["b_block"]), sp["bc"](sp["b_block"] + SSM_GROUPS), sp["cols"](COL_Z),
                  sp["dt"], sp["scal"], sp["scal"], sp["scal"], sp["nw"], sp["hs"], sp["cols"](0), _ANY],
        out_specs=[sp["cols"](0), sp["cols"](COL_Z), bc_out, bc_out, sp["dt"], sp["scal"], sp["scal"], sp["scal"],
                   sp["nw"]],
        input_output_aliases={11: 1},
        out_shape=[jax.ShapeDtypeStruct((S, SSM_INNER), F32), jax.ShapeDtypeStruct(dproj.shape, dproj.dtype),
                   jax.ShapeDtypeStruct((S, SSM_GROUPS * N), F32), jax.ShapeDtypeStruct((S, SSM_GROUPS * N), F32),
                   jax.ShapeDtypeStruct((SSM_HEADS, S, 1), F32),
                   jax.ShapeDtypeStruct((SSM_HEADS, 1, 1), F32), jax.ShapeDtypeStruct((SSM_HEADS, 1, 1), F32),
                   jax.ShapeDtypeStruct((SSM_HEADS, 1, 1), F32), jax.ShapeDtypeStruct((1, SSM_INNER), F32)],
        scratch_shapes=[pltpu.VMEM((PAIRS_PER_GROUP, N, LANES), F32)],
        compiler_params=_params(("parallel", "arbitrary")), name="ssd_bwd",
    )(xbc, xbc, xbc, proj, dt_hm, dtb, alog, dsk, nw, hs, dmixed, dproj)


ATTN_SCALE = HEAD_DIM ** -0.5
ATTN_PAIRS = ATTN_HEADS // 2


def _first_head(rows):
    return lax.broadcasted_iota(jnp.int32, (rows, LANES), 1) < HEAD_DIM


def _pair_norm(x, g2, scale):
    first = _first_head(x.shape[0])
    sq = x * x
    ms0 = jnp.sum(jnp.where(first, sq, 0.0), axis=-1, keepdims=True) * (1.0 / HEAD_DIM)
    ms1 = jnp.sum(jnp.where(first, 0.0, sq), axis=-1, keepdims=True) * (1.0 / HEAD_DIM)
    r = jnp.where(first, lax.rsqrt(ms0 + EPS), lax.rsqrt(ms1 + EPS))
    return x * r * g2 * scale


def _qk_prep_fwd(proj, gq2, gk2):
    S = proj.shape[0]
    tq = _pick(S, (512, 256))

    def body(q_ref, k_ref, v_ref, gq_ref, gk_ref, qo_ref, ko_ref, vo_ref):
        qo_ref[...] = _pair_norm(q_ref[...], gq_ref[...], ATTN_SCALE).astype(BF16)
        ko_ref[...] = _pair_norm(k_ref[...], gk_ref[...], 1.0).astype(BF16)
        vo_ref[...] = v_ref[...].astype(BF16)

    col = lambda c0: pl.BlockSpec((tq, LANES), lambda h, i: (i, c0 // LANES + h))
    blk = pl.BlockSpec((tq, LANES), lambda h, i: (i, h))
    vec = pl.BlockSpec((1, LANES), lambda h, i: (0, 0))
    return pl.pallas_call(
        body, grid=(ATTN_PAIRS, S // tq), in_specs=[col(COL_Q), col(COL_K), col(COL_V), vec, vec],
        out_specs=[blk, blk, blk], out_shape=[jax.ShapeDtypeStruct((S, ATTN_WIDTH), BF16)] * 3,
        compiler_params=_params(("parallel", "parallel")), name="qk_prep_fwd")(proj, proj, proj, gq2, gk2)


def _pair_norm_bwd(proj, col0, g2, scale, dn, dproj, name):
    S = proj.shape[0]
    tq = _pick(S, (512, 256))

    def body(u_ref, g_ref, dn_ref, buf_ref, du_ref, dg_ref):
        @pl.when((pl.program_id(0) == 0) & (pl.program_id(1) == 0))
        def _():
            dg_ref[...] = jnp.zeros_like(dg_ref)

        _, vjp = jax.vjp(lambda u, g: _pair_norm(u, g, scale), u_ref[...], g_ref[...])
        du, dg = vjp(dn_ref[...])
        du_ref[...] = du.astype(du_ref.dtype)
        dg_ref[...] += dg

    ublk = pl.BlockSpec((tq, LANES), lambda h, i: (i, col0 // LANES + h))
    blk = pl.BlockSpec((tq, LANES), lambda h, i: (i, h))
    vec = pl.BlockSpec((1, LANES), lambda h, i: (0, 0))
    return pl.pallas_call(
        body, grid=(ATTN_PAIRS, S // tq), in_specs=[ublk, vec, blk, _ANY], out_specs=[ublk, vec],
        out_shape=[jax.ShapeDtypeStruct(dproj.shape, dproj.dtype), jax.ShapeDtypeStruct((1, LANES), F32)],
        input_output_aliases={3: 0},
        compiler_params=_params(("arbitrary", "arbitrary")), name=name)(proj, g2, dn, dproj)


def _logf_cumsum_fwd(f_raw, f_bias):
    S, Hh = f_raw.shape
    L = CHUNK

    def body(f_ref, b_ref, o_ref, wide_ref):
        ri = lax.broadcasted_iota(jnp.int32, (L, L), 0)
        ci = lax.broadcasted_iota(jnp.int32, (L, L), 1)
        tril = (ri >= ci).astype(F32)
        carry = jnp.zeros((1, Hh), F32)
        for c in range(S // L):
            rows = slice(c * L, (c + 1) * L)
            lf = -_softplus(-(f_ref[rows, :] + b_ref[...]))
            cum = _dot32(tril, lf) + carry
            o_ref[rows, :] = cum
            for h in range(Hh):
                wide_ref[rows, h * HEAD_DIM:(h + 1) * HEAD_DIM] = jnp.broadcast_to(cum[:, h:h + 1], (L, HEAD_DIM))
            carry = cum[L - 1:L, :]

    return pl.pallas_call(
        body, out_shape=[jax.ShapeDtypeStruct((S, Hh), F32), jax.ShapeDtypeStruct((S, Hh * HEAD_DIM), F32)],
        name="logf_cumsum_fwd")(f_raw, f_bias)


def _logf_cumsum_bwd(f_raw, f_bias, dcum):
    S, Hh = f_raw.shape
    L = CHUNK

    def body(f_ref, b_ref, d_ref, df_ref, db_ref):
        ri = lax.broadcasted_iota(jnp.int32, (L, L), 0)
        ci = lax.broadcasted_iota(jnp.int32, (L, L), 1)
        triu = (ri <= ci).astype(F32)
        carry = jnp.zeros((1, Hh), F32)
        db = jnp.zeros((1, Hh), F32)
        for c in reversed(range(S // L)):
            suf = _dot32(triu, d_ref[c * L:(c + 1) * L, :]) + carry
            df = suf * jax.nn.sigmoid(-(f_ref[c * L:(c + 1) * L, :] + b_ref[...]))
            df_ref[c * L:(c + 1) * L, :] = df
            db = db + jnp.sum(df, axis=0, keepdims=True)
            carry = suf[0:1, :]
        db_ref[...] = db

    return pl.pallas_call(
        body, out_shape=[jax.ShapeDtypeStruct((S, Hh), F32), jax.ShapeDtypeStruct((1, Hh), F32)],
        name="logf_cumsum_bwd")(f_raw, f_bias, dcum)


_NT = (((1,), (1,)), ((), ()))
_TN = (((0,), (0,)), ((), ()))


def _mxu(a, b, dims=(((1,), (0,)), ((), ()))):
    return lax.dot_general(a, b, dims, preferred_element_type=F32)


def _flash_fwd(qs, kn, vb, cq, ck):
    S, W = qs.shape
    tq = tk = _pick(S, (512, 256))
    nmask = max(tq // tk, 1)

    def body(q_ref, k_ref, v_ref, cq_ref, ck_ref, o_ref, of_ref, lse_ref):
        i = pl.program_id(1)
        first = _first_head(tq)
        q2 = q_ref[...]
        zero = jnp.zeros_like(q2)
        qa = (jnp.where(first, q2, zero), jnp.where(first, zero, q2))
        cqa = (cq_ref[:, 0:1], cq_ref[:, HEAD_DIM:HEAD_DIM + 1])
        row0 = i * tq

        def step(j, carry, masked):
            ms, ls, acc, rem = carry
            off = pl.multiple_of(j * tk, tk)
            k = k_ref[pl.ds(off, tk), :]
            v = v_ref[pl.ds(off, tk), :]
            new_m, new_l, alphas, pvs, prs = [], [], [], [], []
            for a in range(2):
                s = _mxu(qa[a], k, _NT) + cqa[a] - ck_ref[a, :, pl.ds(off, tk)]
                if masked:
                    ri = lax.broadcasted_iota(jnp.int32, (tq, tk), 0) + row0
                    ci = lax.broadcasted_iota(jnp.int32, (tq, tk), 1) + off
                    s = jnp.where(ri >= ci, s, -1e30)
                m_new = jnp.maximum(ms[a], jnp.max(s, axis=-1, keepdims=True))
                alpha = jnp.exp(ms[a] - m_new)
                p = jnp.exp(s - m_new)
                new_l.append(alpha * ls[a] + jnp.sum(p, axis=-1, keepdims=True))
                new_m.append(m_new)
                alphas.append(alpha)
                p_hi = p.astype(BF16)
                pvs.append(_mxu(p_hi, v))
                prs.append(_mxu((p - p_hi.astype(F32)).astype(BF16), v))
            al = jnp.where(first, alphas[0], alphas[1])
            acc = al * acc + jnp.where(first, pvs[0], pvs[1])
            rem = al * rem + jnp.where(first, prs[0], prs[1])
            return tuple(new_m), tuple(new_l), acc, rem

        neg = jnp.full((tq, 1), -1e30, F32)
        z1 = jnp.zeros((tq, 1), F32)
        z2 = jnp.zeros((tq, LANES), F32)
        carry = ((neg, neg), (z1, z1), z2, z2)
        n_full = (i * tq) // tk
        carry = lax.fori_loop(0, n_full, lambda j, c: step(j, c, False), carry)
        for jj in range(nmask):
            carry = step(n_full + jj, carry, True)
        ms, ls, acc, rem = carry
        linv = jnp.where(first, 1.0 / ls[0], 1.0 / ls[1])
        o_ref[...] = (acc * linv).astype(o_ref.dtype)
        of_ref[...] = (acc + rem) * linv
        lse_ref[...] = jnp.where(first, ms[0] + jnp.log(ls[0]), ms[1] + jnp.log(ls[1]))

    qblk = pl.BlockSpec((tq, LANES), lambda h, i: (i, h))
    full = pl.BlockSpec((S, LANES), lambda h, i: (0, h))
    return pl.pallas_call(
        body, grid=(W // LANES, S // tq),
        in_specs=[qblk, full, full, qblk, pl.BlockSpec((2, 1, S), lambda h, i: (h, 0, 0))],
        out_specs=[qblk, qblk, qblk],
        out_shape=[jax.ShapeDtypeStruct((S, W), BF16), jax.ShapeDtypeStruct((S, W), F32),
                   jax.ShapeDtypeStruct((S, W), F32)],
        compiler_params=_params(("parallel", "parallel")), name="flash_fwd")(qs, kn, vb, cq, ck)


def _flash_bwd(qs, kn, vb, cq, ck, o_fine, do, do_col0, lse):
    S, W = qs.shape
    tq = tk = _pick(S, (512, 256))
    nq = S // tq
    nmask = max(tk // tq, 1)

    def body(q_ref, k_ref, v_ref, cq_ref, ck_ref, of_ref, do_ref, lse_ref, dq_ref, dk_ref, dv_ref, dck_ref):
        j = pl.program_id(1)

        @pl.when(j == 0)
        def _():
            dq_ref[...] = jnp.zeros_like(dq_ref)

        firstk = _first_head(tk)
        firstq = _first_head(tq)
        k2 = k_ref[...]
        v2 = v_ref[...]
        zk = jnp.zeros_like(k2)
        ka = (jnp.where(firstk, k2, zk), jnp.where(firstk, zk, k2))
        va = (jnp.where(firstk, v2, zk), jnp.where(firstk, zk, v2))
        cka = (ck_ref[0], ck_ref[1])
        col0 = j * tk

        def step(i, carry, masked):
            dk, dv, dck0, dck1 = carry
            dcks = [dck0, dck1]
            off = pl.multiple_of(i * tq, tq)
            rows = pl.ds(off, tq)
            q2 = q_ref[rows, :]
            dob = do_ref[rows, :].astype(BF16)
            prod = dob.astype(F32) * of_ref[rows, :]
            dkp, dvp, dqp = [], [], []
            for a in range(2):
                lane = pl.ds(a * HEAD_DIM, 1)
                s = _mxu(q2, ka[a], _NT) + cq_ref[rows, lane] - cka[a]
                if masked:
                    ri = lax.broadcasted_iota(jnp.int32, (tq, tk), 0) + off
                    ci = lax.broadcasted_iota(jnp.int32, (tq, tk), 1) + col0
                    s = jnp.where(ri >= ci, s, -1e30)
                p = jnp.exp(s - lse_ref[rows, lane])
                dp = _mxu(dob, va[a], _NT)
                own = jnp.where(firstq, prod, 0.0) if a == 0 else jnp.where(firstq, 0.0, prod)
                ds = p * (dp - jnp.sum(own, axis=-1, keepdims=True))
                dsb = ds.astype(BF16)
                dvp.append(_mxu(p.astype(BF16), dob, _TN))
                dkp.append(_mxu(dsb, q2, _TN))
                dqp.append(_mxu(dsb, k2))
                dcks[a] = dcks[a] - jnp.sum(ds, axis=0, keepdims=True)
            dq_ref[rows, :] += jnp.where(firstq, dqp[0], dqp[1])
            dk = dk + jnp.where(firstk, dkp[0], dkp[1])
            dv = dv + jnp.where(firstk, dvp[0], dvp[1])
            return dk, dv, dcks[0], dcks[1]

        z2 = jnp.zeros((tk, LANES), F32)
        z1 = jnp.zeros((1, tk), F32)
        carry = (z2, z2, z1, z1)
        i0 = (j * tk) // tq
        for ii in range(nmask):
            carry = step(i0 + ii, carry, True)
        dk, dv, dck0, dck1 = lax.fori_loop(i0 + nmask, nq, lambda i, c: step(i, c, False), carry)
        dk_ref[...] = dk
        dv_ref[...] = dv.astype(dv_ref.dtype)
        dck_ref[0] = dck0
        dck_ref[1] = dck1

    kblk = pl.BlockSpec((tk, LANES), lambda h, j: (j, h))
    full = pl.BlockSpec((S, LANES), lambda h, j: (0, h))
    dofull = pl.BlockSpec((S, LANES), lambda h, j: (0, do_col0 // LANES + h))
    rowt = pl.BlockSpec((2, 1, tk), lambda h, j: (h, 0, j))
    dvblk = pl.BlockSpec((tk, LANES), lambda h, j: (j, COL_V // LANES + h))
    return pl.pallas_call(
        body, grid=(W // LANES, S // tk),
        in_specs=[full, kblk, kblk, full, rowt, full, dofull, full],
        out_specs=[full, kblk, dvblk, rowt],
        out_shape=[jax.ShapeDtypeStruct((S, W), F32), jax.ShapeDtypeStruct((S, W), F32),
                   jax.ShapeDtypeStruct((S, IN_COLS_PAD), BF16), jax.ShapeDtypeStruct((2 * (W // LANES), 1, S), F32)],
        compiler_params=_params(("parallel", "arbitrary")), name="flash_bwd")(qs, kn, vb, cq, ck, o_fine, do, lse)


XATTN_SCALE = XATTN_DIM ** -0.5


def _xq_norm(q, g):
    return _rms(q, g) * XATTN_SCALE


def _xattn_fwd(xq, kv, gq, gk):
    S = xq.shape[0]
    Mm = kv.shape[0]
    Dh = XATTN_DIM
    tq = _pick(S, (512, 256))

    def body(q_ref, k_ref, v_ref, gq_ref, gk_ref, o_ref):
        qn = _xq_norm(q_ref[...], gq_ref[...]).astype(BF16)
        kn = _rms(k_ref[...], gk_ref[...]).astype(BF16)
        s = _mxu(qn, kn, _NT)
        m = jnp.max(s, axis=-1, keepdims=True)
        p = jnp.exp(s - m)
        l = jnp.sum(p, axis=-1, keepdims=True)
        o_ref[...] = (_mxu(p.astype(BF16), v_ref[...].astype(BF16)) / l).astype(o_ref.dtype)

    vec = pl.BlockSpec((1, Dh), lambda h, i: (0, 0))
    return pl.pallas_call(
        body, grid=(XATTN_HEADS, S // tq),
        in_specs=[pl.BlockSpec((tq, Dh), lambda h, i: (i, h)), pl.BlockSpec((Mm, Dh), lambda h, i: (0, h)),
                  pl.BlockSpec((Mm, Dh), lambda h, i: (0, XATTN_HEADS + h)), vec, vec],
        out_specs=pl.BlockSpec((tq, Dh), lambda h, i: (i, h)),
        out_shape=jax.ShapeDtypeStruct((S, XATTN_HEADS * Dh), BF16),
        compiler_params=_params(("parallel", "parallel")), name="xattn_fwd")(xq, kv, kv, gq, gk)


def _xattn_bwd(xq, kv, gq, gk, do):
    S = xq.shape[0]
    Mm = kv.shape[0]
    Dh = XATTN_DIM
    tq = _pick(S, (512, 256))
    nq = S // tq

    def body(q_ref, k_ref, v_ref, gq_ref, gk_ref, do_ref, dq_ref, dk_ref, dv_ref, dgq_ref, dgk_ref, dkn_acc, dv_acc):
        h = pl.program_id(0)
        i = pl.program_id(1)

        @pl.when((h == 0) & (i == 0))
        def _():
            dgq_ref[...] = jnp.zeros_like(dgq_ref)
            dgk_ref[...] = jnp.zeros_like(dgk_ref)

        @pl.when(i == 0)
        def _():
            dkn_acc[...] = jnp.zeros_like(dkn_acc)
            dv_acc[...] = jnp.zeros_like(dv_acc)

        qn32, vq = jax.vjp(_xq_norm, q_ref[...], gq_ref[...])
        kn32, vk = jax.vjp(_rms, k_ref[...], gk_ref[...])
        qn = qn32.astype(BF16)
        kn = kn32.astype(BF16)
        vb = v_ref[...].astype(BF16)
        s = _mxu(qn, kn, _NT)
        m = jnp.max(s, axis=-1, keepdims=True)
        p = jnp.exp(s - m)
        p = p / jnp.sum(p, axis=-1, keepdims=True)
        dob = do_ref[...].astype(BF16)
        dp = _mxu(dob, vb, _NT)
        delta = jnp.sum(p * dp, axis=-1, keepdims=True)
        ds = (p * (dp - delta)).astype(BF16)
        dv_acc[...] += _mxu(p.astype(BF16), dob, _TN)
        dkn_acc[...] += _mxu(ds, qn, _TN)
        dq, dgq = vq(_mxu(ds, kn))
        dq_ref[...] = dq.astype(dq_ref.dtype)
        dgq_ref[...] += dgq

        @pl.when(i == nq - 1)
        def _():
            dk, dgk = vk(dkn_acc[...])
            dk_ref[...] = dk.astype(dk_ref.dtype)
            dv_ref[...] = dv_acc[...].astype(dv_ref.dtype)
            dgk_ref[...] += dgk

    vec = pl.BlockSpec((1, Dh), lambda h, i: (0, 0))
    qblk = pl.BlockSpec((tq, Dh), lambda h, i: (i, h))
    kblk = pl.BlockSpec((Mm, Dh), lambda h, i: (0, h))
    vblk = pl.BlockSpec((Mm, Dh), lambda h, i: (0, XATTN_HEADS + h))
    return pl.pallas_call(
        body, grid=(XATTN_HEADS, nq),
        in_specs=[qblk, kblk, vblk, vec, vec, qblk],
        out_specs=[qblk, kblk, kblk, vec, vec],
        out_shape=[jax.ShapeDtypeStruct((S, XATTN_HEADS * Dh), BF16),
                   jax.ShapeDtypeStruct((Mm, XATTN_HEADS * Dh), BF16),
                   jax.ShapeDtypeStruct((Mm, XATTN_HEADS * Dh), BF16),
                   jax.ShapeDtypeStruct((1, Dh), F32), jax.ShapeDtypeStruct((1, Dh), F32)],
        scratch_shapes=[pltpu.VMEM((Mm, Dh), F32), pltpu.VMEM((Mm, Dh), F32)],
        compiler_params=_params(("arbitrary", "arbitrary")), name="xattn_bwd")(xq, kv, kv, gq, gk, do)


def _loss_head(y, target):
    S, D = y.shape
    tr = _pick(S, (512, 256))

    def body(y_ref, t_ref, dy_ref, loss_ref):
        @pl.when(pl.program_id(0) == 0)
        def _():
            loss_ref[...] = jnp.zeros_like(loss_ref)

        err = y_ref[...] - t_ref[...]
        dy_ref[...] = err * (1.0 / D)
        loss_ref[...] += jnp.sum(err * err) * (0.5 / D)

    row = pl.BlockSpec((tr, D), lambda i: (i, 0))
    return pl.pallas_call(
        body, grid=(S // tr,), in_specs=[row, row],
        out_specs=[row, pl.BlockSpec((1, LANES), lambda i: (0, 0))],
        out_shape=[jax.ShapeDtypeStruct((S, D), F32), jax.ShapeDtypeStruct((1, LANES), F32)],
        compiler_params=_params(("arbitrary",)), name="loss_head")(y, target)


def _row_tile(R, C):
    for tr in (1024, 512, 256, 128, 64, 32, 16, 8):
        if R % tr == 0 and tr * C * 4 <= (1 << 20):
            return tr
    return R


def _chip_sum(own, from_chips, name):
    R, C = own.shape
    tr = _row_tile(R, C)

    def body(own_ref, a_ref, b_ref, c_ref, o_ref):
        o_ref[...] = ((own_ref[...].astype(F32) + a_ref[...].astype(F32)) + b_ref[...].astype(F32)) + c_ref[...].astype(F32)

    blk = pl.BlockSpec((tr, C), lambda i: (i, 0))
    slab = lambda s: pl.BlockSpec((None, tr, C), lambda i: (s, i, 0))
    return pl.pallas_call(
        body, grid=(R // tr,), in_specs=[blk, slab(0), slab(1), slab(2)], out_specs=blk,
        out_shape=jax.ShapeDtypeStruct((R, C), F32),
        compiler_params=_params(("parallel",)), name=name)(own, from_chips, from_chips, from_chips)


def _adamw(w, g_mine, g_sibling, m, v, name):
    R, C = w.shape
    tr = _row_tile(R, C)
    c1 = 1.0 - ADAM_B1 ** ADAM_STEP
    c2 = 1.0 - ADAM_B2 ** ADAM_STEP

    def body(w_ref, ga_ref, gb_ref, m_ref, v_ref, g_ref, d_ref, mo_ref, vo_ref):
        g_t = ga_ref[...] + gb_ref[...]
        m_new = ADAM_B1 * m_ref[...] + (1.0 - ADAM_B1) * g_t
        v_new = ADAM_B2 * v_ref[...] + (1.0 - ADAM_B2) * (g_t * g_t)
        g_ref[...] = g_t
        d_ref[...] = -ADAM_LR * ((m_new / c1) / (jnp.sqrt(v_new / c2) + ADAM_EPS) + ADAM_WD * w_ref[...])
        mo_ref[...] = m_new
        vo_ref[...] = v_new

    blk = pl.BlockSpec((tr, C), lambda i: (i, 0))
    return pl.pallas_call(
        body, grid=(R // tr,), in_specs=[blk] * 5, out_specs=[blk] * 4,
        out_shape=[jax.ShapeDtypeStruct((R, C), F32)] * 4,
        compiler_params=_params(("parallel",)), name=name)(w, g_mine, g_sibling, m, v)


D_MODEL = 1024
SSM_INNER = SSM_HEADS * HEAD_DIM
CONV_DIM = SSM_INNER + 2 * SSM_GROUPS * SSM_STATE
ATTN_WIDTH = ATTN_HEADS * HEAD_DIM
COL_Z = 0
COL_XBC = COL_Z + SSM_INNER
COL_Q = COL_XBC + CONV_DIM
COL_K = COL_Q + ATTN_WIDTH
COL_V = COL_K + ATTN_WIDTH
COL_DT = COL_V + ATTN_WIDTH
COL_F = COL_DT + SSM_HEADS
IN_COLS = COL_F + ATTN_HEADS
IN_COLS_PAD = -(-IN_COLS // LANES) * LANES
REF_COL_DT = COL_Q
SHARD_COLS = IN_COLS // N_CHIPS
_COL_RANGES = ((0, REF_COL_DT, 0), (REF_COL_DT + SSM_HEADS, COL_F, COL_Q), (REF_COL_DT, REF_COL_DT + SSM_HEADS, COL_DT),
               (COL_F, IN_COLS, COL_F))


def _w_in_from_shards(g):
    parts = []
    for lo, hi, _ in _COL_RANGES:
        while lo < hi:
            j = lo // SHARD_COLS
            end = min(hi, (j + 1) * SHARD_COLS)
            parts.append(g[j][:, lo - j * SHARD_COLS:end - j * SHARD_COLS])
            lo = end
    parts.append(jnp.zeros((g.shape[1], IN_COLS_PAD - IN_COLS), g.dtype))
    return jnp.concatenate(parts, axis=1)


def _w_in_to_shards(w):
    shards = []
    for j in range(N_CHIPS):
        parts = []
        for lo, hi, here in sorted(_COL_RANGES):
            a, b = max(lo, j * SHARD_COLS), min(hi, (j + 1) * SHARD_COLS)
            if a < b:
                parts.append(w[:, here + a - lo:here + b - lo])
        shards.append(jnp.concatenate(parts, axis=1))
    return jnp.stack(shards)


def _add_residual(acc, res):
    return (res + acc,)


def _relu2(acc):
    r = jnp.maximum(acc, 0.0)
    return acc, r * r


def _relu2_bwd(acc, a):
    return (acc * (2.0 * jnp.maximum(a, 0.0)),)


def _layer_fwd_bwd(x, mem, target, w_in, p, late_weights, send_late_grads, send_w_in_grad):
    S = x.shape[0]
    hd3 = lambda a: a.reshape(SSM_HEADS, 1, 1)

    h1 = _rmsnorm_fwd(x, p["g_mix"], "norm_mix")
    proj = _mm(h1, w_in, "nn", "in_proj")
    xbc = _conv_fwd(proj, COL_XBC, CONV_DIM, p["conv_w"], p["conv_b"])
    dt_hm = proj[:, COL_DT:COL_DT + SSM_HEADS].T[:, :, None]
    ssd_par = (hd3(p["dt_bias"]), hd3(p["a_log"]), hd3(p["d_skip"]), p["ssm_norm_w"])
    y, hs = _ssd_fwd(xbc, proj, dt_hm, *ssd_par)
    f_raw = proj[:, COL_F:COL_F + ATTN_HEADS]
    gq2 = jnp.tile(p["g_q"], (1, 2))
    gk2 = jnp.tile(p["g_k"], (1, 2))
    qs, kn, vb = _qk_prep_fwd(proj, gq2, gk2)
    cum, cq = _logf_cumsum_fwd(f_raw, p["f_bias"])
    ck = cum.T[:, None, :]
    o, o_fine, lse = _flash_fwd(qs, kn, vb, cq, ck)
    W = late_weights((o_fine, y))
    x1 = _mm(y, W["w_out"][:SSM_INNER], "nn", "out_proj_ssm", epilogue=_add_residual, extras=(x,))
    x1 = _mm(o, W["w_out"][SSM_INNER:], "nn", "out_proj_attn", epilogue=_add_residual, extras=(x1,))
    h2 = _rmsnorm_fwd(x1, p["g_xattn"], "norm_xattn")
    mem_n = _rmsnorm_fwd(mem, p["g_mem"], "norm_mem")
    xq = _mm(h2, W["xq_w"], "nn", "xq_proj")
    kv = _mm(mem_n, W["xkv_w"], "nn", "xkv_proj", b_chunks=N_CHIPS)
    xo = _xattn_fwd(xq, kv, p["xg_q"], p["xg_k"])
    x2 = _mm(xo, W["xo_w"], "nn", "xo_proj", epilogue=_add_residual, extras=(x1,))
    h3 = _rmsnorm_fwd(x2, p["g_mlp"], "norm_mlp")
    a, act = _mm(h3, W["w_up"], "nn", "mlp_up", out_dtypes=(F32, BF16), epilogue=_relu2, b_chunks=N_CHIPS)
    x3 = _mm(act, W["w_down"], "nn", "mlp_down", epilogue=_add_residual, extras=(x2,))
    dy, loss_row = _loss_head(x3, target)

    gW, gp = {}, {}
    da = _mm(dy, W["w_down"], "nt", "d_act", out_dtypes=(BF16,), epilogue=_relu2_bwd, extras=(a,))
    gW["w_down"] = _mm(act, dy, "tn", "g_w_down", out_dtypes=(BF16,))
    gW["w_up"] = _mm(h3, da, "tn", "g_w_up", out_dtypes=(BF16,), out_chunks=N_CHIPS)
    dh3 = _mm(da, W["w_up"], "nt", "d_h3", b_chunks=N_CHIPS)
    dx2, gp["g_mlp"] = _rmsnorm_bwd(x2, p["g_mlp"], dh3, dy, "norm_mlp_bwd")
    dxo = _mm(dx2, W["xo_w"], "nt", "d_xo", out_dtypes=(BF16,))
    gW["xo_w"] = _mm(xo, dx2, "tn", "g_xo_w", out_dtypes=(BF16,))
    dxq, dk_x, dv_x, gp["xg_q"], gp["xg_k"] = _xattn_bwd(xq, kv, p["xg_q"], p["xg_k"], dxo)
    dkv = jnp.concatenate([dk_x, dv_x], axis=-1)
    gW["xq_w"] = _mm(h2, dxq, "tn", "g_xq_w", out_dtypes=(BF16,))
    dh2 = _mm(dxq, W["xq_w"], "nt", "d_h2")
    gW["xkv_w"] = _mm(mem_n, dkv, "tn", "g_xkv_w", out_dtypes=(BF16,), out_chunks=N_CHIPS)
    dmem_n = _mm(dkv, W["xkv_w"], "nt", "d_mem_n", b_chunks=N_CHIPS)
    _, gp["g_mem"] = _rmsnorm_bwd(mem, p["g_mem"], dmem_n, None, "norm_mem_bwd")
    dx1, gp["g_xattn"] = _rmsnorm_bwd(x1, p["g_xattn"], dh2, dx2, "norm_xattn_bwd")
    dmixed = _mm(dx1, W["w_out"], "nt", "d_mixed")
    gW["w_out"] = jnp.concatenate([_mm(y, dx1, "tn", "g_w_out_ssm", out_dtypes=(BF16,)),
                                   _mm(o, dx1, "tn", "g_w_out_attn", out_dtypes=(BF16,))], axis=0)
    token = send_late_grads(gW)
    dqs, dkn, dproj, dck = _flash_bwd(qs, kn, vb, cq, ck + token[:1, :1], o_fine, dmixed, SSM_INNER, lse)
    dproj, dgq2 = _pair_norm_bwd(proj, COL_Q, gq2, ATTN_SCALE, dqs, dproj, "q_norm_bwd")
    dproj, dgk2 = _pair_norm_bwd(proj, COL_K, gk2, 1.0, dkn, dproj, "k_norm_bwd")
    gp["g_q"] = dgq2[:, :HEAD_DIM] + dgq2[:, HEAD_DIM:]
    gp["g_k"] = dgk2[:, :HEAD_DIM] + dgk2[:, HEAD_DIM:]
    df, gp["f_bias"] = _logf_cumsum_bwd(f_raw, p["f_bias"], dck[:, 0, :].T)
    dxs, dproj, dB, dC, ddt, ddtb, dalog, ddsk, gp["ssm_norm_w"] = _ssd_bwd(xbc, proj, dt_hm, *ssd_par, hs, dmixed, dproj)
    gp["dt_bias"] = ddtb.reshape(1, SSM_HEADS)
    gp["a_log"] = dalog.reshape(1, SSM_HEADS)
    gp["d_skip"] = ddsk.reshape(1, SSM_HEADS)
    dproj, dconv_w, gp["conv_b"] = _conv_bwd(proj, COL_XBC, CONV_DIM, p["conv_w"], p["conv_b"], (dxs, dB, dC), dproj)
    gp["conv_w"] = dconv_w[:CONV_WIDTH]
    tail = jnp.concatenate([ddt[:, :, 0].T, df, jnp.zeros((S, IN_COLS_PAD - IN_COLS), F32)], axis=-1).astype(BF16)
    dproj = lax.dynamic_update_slice(dproj, tail, (0, COL_DT))
    token = send_w_in_grad(_mm(h1, dproj, "tn", "g_w_in", out_dtypes=(BF16,)))
    dh1 = _mm(dproj, w_in, "nt", "d_h1")
    dx, gp["g_mix"] = _rmsnorm_bwd(x, p["g_mix"] + token[:1, :1], dh1, dx1, "norm_mix_bwd")
    return loss_row, dx, gp


_ANY = pl.BlockSpec(memory_space=pl.ANY)


def _place():
    x, y, c = lax.axis_index("x"), lax.axis_index("y"), lax.axis_index("c")
    chips = [(1 - x, y), (x, 1 - y), (1 - x, 1 - y)]
    return x, y, c, chips


def _chip_index(px, py):
    return 2 * px + py


def _all_gather_chips(split, whole):
    ns, nw = len(split), len(whole)
    n = ns + nw

    def body(*refs):
        ins, outs = refs[:n], refs[n:2 * n]
        send_ici, recv_ici, send_d2d, recv_d2d = refs[2 * n:]
        x, y, c, chips = _place()
        me = _chip_index(x, y)
        sib = (x, y, 1 - c)

        def ici(k, j, src, dst):
            return pltpu.make_async_remote_copy(src_ref=src, dst_ref=dst, send_sem=send_ici.at[3 * k + j],
                                                recv_sem=recv_ici.at[3 * k + j], device_id=(*chips[j], c),
                                                device_id_type=MESH)

        def d2d(k, j, piece):
            return pltpu.make_async_remote_copy(src_ref=piece, dst_ref=piece, send_sem=send_d2d.at[3 * k + j],
                                                recv_sem=recv_d2d.at[3 * k + j], device_id=sib, device_id_type=MESH)

        sends = []
        for k in range(n):
            for j in range(3):
                if k < ns:
                    sends.append(ici(k, j, ins[k].at[c], outs[k].at[me, c]))
                else:
                    sends.append(ici(k, j, ins[k], outs[k].at[me]))
                sends[-1].start()
        passed = []
        for k in range(n):
            for j in range(3):
                src_chip = _chip_index(*chips[j])
                if k < ns:
                    ici(k, j, ins[k].at[c], outs[k].at[src_chip, c]).wait_recv()
                    passed.append(d2d(k, j, outs[k].at[src_chip, c]))
                    passed[-1].start()
                else:
                    ici(k, j, ins[k], outs[k].at[src_chip]).wait_recv()
        for k in range(ns):
            for j in range(3):
                d2d(k, j, outs[k].at[_chip_index(*chips[j]), 1 - c]).wait_recv()
        for cp in sends + passed:
            cp.wait_send()

    arrs = list(split) + list(whole)
    return pl.pallas_call(
        body, in_specs=[_ANY] * n, out_specs=[_ANY] * n,
        out_shape=[jax.ShapeDtypeStruct((N_CHIPS,) + a.shape, a.dtype) for a in arrs],
        scratch_shapes=[pltpu.SemaphoreType.DMA((3 * n,)), pltpu.SemaphoreType.DMA((3 * n,)),
                        pltpu.SemaphoreType.DMA((3 * ns,)), pltpu.SemaphoreType.DMA((3 * ns,))],
        name="all_gather_chips")(*arrs)


def _sibling_swap(arrs, name):
    n = len(arrs)

    def body(*refs):
        ins, outs = refs[:n], refs[n:2 * n]
        send_sem, recv_sem = refs[2 * n:]
        x, y, c, _ = _place()
        copies = [pltpu.make_async_remote_copy(src_ref=ins[k], dst_ref=outs[k], send_sem=send_sem.at[k],
                                               recv_sem=recv_sem.at[k], device_id=(x, y, 1 - c), device_id_type=MESH)
                  for k in range(n)]
        for q in copies:
            q.start()
        for q in copies:
            q.wait()

    return pl.pallas_call(
        body, in_specs=[_ANY] * n, out_specs=[_ANY] * n,
        out_shape=[jax.ShapeDtypeStruct(a.shape, a.dtype) for a in arrs],
        scratch_shapes=[pltpu.SemaphoreType.DMA((n,)), pltpu.SemaphoreType.DMA((n,))],
        name=name)(*arrs)


_HBM = pl.BlockSpec(memory_space=pltpu.HBM)
_SEM = pl.BlockSpec(memory_space=pltpu.SEMAPHORE)
_SPLIT_EFFECT = pltpu.SideEffectType.DATAFLOW_SIDE_EFFECTING


class _Split(NamedTuple):
    send_sems: jax.Array
    recv_sems: jax.Array
    sources: tuple
    lands: tuple
    token: jax.Array


def _split_copies(kind, srcs, lands, send_sems, recv_sems):
    x, y, c, chips = _place()
    me = _chip_index(x, y)
    copies = []
    for k in range(len(srcs)):
        for j in range(3):
            if kind == "gather":
                src, dst = srcs[k], lands[k].at[me]
            else:
                src, dst = srcs[k].at[_chip_index(*chips[j])], lands[k].at[j]
            copies.append(pltpu.make_async_remote_copy(
                src_ref=src, dst_ref=dst, send_sem=send_sems.at[3 * k + j], recv_sem=recv_sems.at[3 * k + j],
                device_id=(*chips[j], c), device_id_type=MESH))
    return copies


def _split_start(name, sources, kind, after):
    n = len(sources)
    if kind == "gather":
        lands = [lax.empty((N_CHIPS,) + s.shape, s.dtype) for s in sources]
    else:
        lands = [lax.empty((3,) + s.shape[1:], s.dtype) for s in sources]
    deps = [] if after is None else [after]

    def body(*refs):
        srcs, lnds = refs[:n], refs[n:2 * n]
        send_sems, recv_sems = refs[2 * n + len(deps)], refs[2 * n + len(deps) + 1]
        for cp in _split_copies(kind, srcs, lnds, send_sems, recv_sems):
            cp.start()
        refs[-1][...] = jnp.zeros_like(refs[-1])

    hbm = lambda a: pltpu.with_memory_space_constraint(a, pltpu.HBM)
    outs = pl.pallas_call(
        body, name=name,
        in_specs=[_HBM] * (2 * n) + [_ANY] * len(deps),
        out_specs=[_SEM, _SEM] + [_HBM] * (2 * n) + [pl.BlockSpec(memory_space=pltpu.VMEM)],
        out_shape=[pltpu.SemaphoreType.DMA((3 * n,)), pltpu.SemaphoreType.DMA((3 * n,))]
        + [pltpu.HBM(a.shape, a.dtype) for a in list(sources) + lands] + [jax.ShapeDtypeStruct((8, LANES), F32)],
        input_output_aliases={k: 2 + k for k in range(2 * n)},
        compiler_params=pltpu.CompilerParams(has_side_effects=_SPLIT_EFFECT),
    )(*[hbm(s) for s in sources], *[hbm(l) for l in lands], *deps)
    return _Split(outs[0], outs[1], tuple(outs[2:2 + n]), tuple(outs[2 + n:2 + 2 * n]), outs[-1])


def _split_wait(name, h, kind, after):
    n = len(h.sources)

    def body(*refs):
        srcs, lnds = refs[:n], refs[n:2 * n]
        for cp in _split_copies(kind, srcs, lnds, refs[2 * n], refs[2 * n + 1]):
            cp.wait_send()
            cp.wait_recv()

    outs = pl.pallas_call(
        body, name=name,
        in_specs=[_HBM] * (2 * n) + [_SEM, _SEM] + [_ANY] * len(after),
        out_specs=[_HBM] * (2 * n),
        out_shape=[pltpu.HBM(a.shape, a.dtype) for a in h.sources + h.lands],
        input_output_aliases={k: k for k in range(2 * n)},
        compiler_params=pltpu.CompilerParams(has_side_effects=_SPLIT_EFFECT),
    )(*h.sources, *h.lands, h.send_sems, h.recv_sems, *after)
    return outs[:n], outs[n:]


def _all_reduce_small(vec, after):
    R, C = vec.shape

    def body(v_ref, after_ref, o_ref, buf, send_sem, recv_sem):
        x, y, c = lax.axis_index("x"), lax.axis_index("y"), lax.axis_index("c")
        me = 4 * x + 2 * y + c
        buf[me] = v_ref[...]
        copies = []
        for r in range(1, N_DEV):
            fx, fy, fc = (r >> 2) & 1, (r >> 1) & 1, r & 1
            peer = (x ^ fx, y ^ fy, c ^ fc)
            copies.append(pltpu.make_async_remote_copy(src_ref=v_ref, dst_ref=buf.at[me], send_sem=send_sem.at[r - 1],
                                                       recv_sem=recv_sem.at[r - 1], device_id=peer, device_id_type=MESH))
        for q in copies:
            q.start()
        for r in range(1, N_DEV):
            fx, fy, fc = (r >> 2) & 1, (r >> 1) & 1, r & 1
            src = 4 * (x ^ fx) + 2 * (y ^ fy) + (c ^ fc)
            pltpu.make_async_remote_copy(src_ref=v_ref, dst_ref=buf.at[src], send_sem=send_sem.at[r - 1],
                                         recv_sem=recv_sem.at[r - 1], device_id=(x, y, c), device_id_type=MESH).wait_recv()
        acc = buf[0]
        for d in range(1, N_DEV):
            acc = acc + buf[d]
        o_ref[...] = acc
        for q in copies:
            q.wait_send()

    vm = pl.BlockSpec(memory_space=pltpu.VMEM)
    return pl.pallas_call(
        body, in_specs=[vm, _ANY], out_specs=vm, out_shape=jax.ShapeDtypeStruct((R, C), F32),
        scratch_shapes=[pltpu.VMEM((N_DEV, R, C), F32), pltpu.SemaphoreType.DMA((N_DEV - 1,)),
                        pltpu.SemaphoreType.DMA((N_DEV - 1,))],
        name="all_reduce_small")(vec, after)


_INPUTS = ["x", "mem", "g_mix", "w_in", "conv_w", "conv_b", "dt_bias", "a_log", "d_skip", "ssm_norm_w", "g_q", "g_k",
           "f_bias", "w_out", "g_xattn", "g_mem", "xq_w", "xkv_w", "xg_q", "xg_k", "xo_w", "g_mlp", "w_up", "w_down"]
_WEIGHTS = _INPUTS[2:]
_BIG = ["w_in", "w_out", "xq_w", "xkv_w", "xo_w", "w_up", "w_down"]
_LATE = _BIG[1:]
_COL_SHARDED = ["w_in", "xkv_w", "w_up"]
_SMALL = [n for n in _WEIGHTS if n not in _BIG]


def _pack_rows(arrs, width):
    starts, r = [], 0
    for a in arrs:
        starts.append(r)
        r += a.shape[0]
    out = jnp.concatenate([jnp.pad(a, ((0, 0), (0, width - a.shape[1]))) for a in arrs], axis=0)
    return jnp.pad(out, ((0, -r % 8), (0, 0))), starts


def _adamw_small(summed, starts, ws, ms, vs, conv_w_index):
    n = len(ws)
    c1 = 1.0 - ADAM_B1 ** ADAM_STEP
    c2 = 1.0 - ADAM_B2 ** ADAM_STEP

    def body(s_ref, *refs):
        w_refs, m_refs, v_refs = refs[:n], refs[n:2 * n], refs[2 * n:3 * n]
        outs = refs[3 * n:]
        chip = _chip_index(lax.axis_index("x"), lax.axis_index("y"))
        for k in range(n):
            rows, cols = w_refs[k].shape
            if k == conv_w_index:
                g = s_ref[starts[k]:starts[k] + rows, pl.ds(pl.multiple_of(chip * cols, LANES), cols)]
            else:
                g = s_ref[starts[k]:starts[k] + rows, 0:cols]
            m_new = ADAM_B1 * m_refs[k][...] + (1.0 - ADAM_B1) * g
            v_new = ADAM_B2 * v_refs[k][...] + (1.0 - ADAM_B2) * (g * g)
            outs[4 * k][...] = g
            outs[4 * k + 1][...] = -ADAM_LR * ((m_new / c1) / (jnp.sqrt(v_new / c2) + ADAM_EPS) + ADAM_WD * w_refs[k][...])
            outs[4 * k + 2][...] = m_new
            outs[4 * k + 3][...] = v_new

    vm = pl.BlockSpec(memory_space=pltpu.VMEM)
    outs = pl.pallas_call(
        body, in_specs=[vm] * (1 + 3 * n), out_specs=[vm] * (4 * n),
        out_shape=[jax.ShapeDtypeStruct(a.shape, F32) for a in ws for _ in range(4)],
        name="adamw_small")(summed, *ws, *ms, *vs)
    return [outs[4 * k:4 * k + 4] for k in range(n)]


def kernel(x, mem, g_mix, w_in, conv_w, conv_b, dt_bias, a_log, d_skip, ssm_norm_w, g_q, g_k, f_bias, w_out, g_xattn, g_mem, xq_w, xkv_w, xg_q, xg_k, xo_w, g_mlp, w_up, w_down, loss_target, m_g_mix, m_w_in, m_conv_w, m_conv_b, m_dt_bias, m_a_log, m_d_skip, m_ssm_norm_w, m_g_q, m_g_k, m_f_bias, m_w_out, m_g_xattn, m_g_mem, m_xq_w, m_xkv_w, m_xg_q, m_xg_k, m_xo_w, m_g_mlp, m_w_up, m_w_down, v_g_mix, v_w_in, v_conv_w, v_conv_b, v_dt_bias, v_a_log, v_d_skip, v_ssm_norm_w, v_g_q, v_g_k, v_f_bias, v_w_out, v_g_xattn, v_g_mem, v_xq_w, v_xkv_w, v_xg_q, v_xg_k, v_xo_w, v_g_mlp, v_w_up, v_w_down):
    args = (x, mem, g_mix, w_in, conv_w, conv_b, dt_bias, a_log, d_skip, ssm_norm_w, g_q, g_k, f_bias, w_out, g_xattn,
            g_mem, xq_w, xkv_w, xg_q, xg_k, xo_w, g_mlp, w_up, w_down)
    w = dict(zip(_INPUTS, args))
    mom1 = dict(zip(_WEIGHTS, (m_g_mix, m_w_in, m_conv_w, m_conv_b, m_dt_bias, m_a_log, m_d_skip, m_ssm_norm_w, m_g_q,
                               m_g_k, m_f_bias, m_w_out, m_g_xattn, m_g_mem, m_xq_w, m_xkv_w, m_xg_q, m_xg_k, m_xo_w,
                               m_g_mlp, m_w_up, m_w_down)))
    mom2 = dict(zip(_WEIGHTS, (v_g_mix, v_w_in, v_conv_w, v_conv_b, v_dt_bias, v_a_log, v_d_skip, v_ssm_norm_w, v_g_q,
                               v_g_k, v_f_bias, v_w_out, v_g_xattn, v_g_mem, v_xq_w, v_xkv_w, v_xg_q, v_xg_k, v_xo_w,
                               v_g_mlp, v_w_up, v_w_down)))
    chip = _chip_index(lax.axis_index("x"), lax.axis_index("y"))

    shard_bf = {n: w[n][0].astype(BF16) for n in _BIG}

    def layout_for_compute(n, g):
        if n == "w_in":
            return _w_in_from_shards(g)
        return g if n in _COL_SHARDED else g.reshape(N_CHIPS * g.shape[1], g.shape[2])

    def layout_for_reduction(n, g):
        if n == "w_in":
            return _w_in_to_shards(g)
        return g if n in _COL_SHARDED else g.reshape(N_CHIPS, g.shape[0] // N_CHIPS, g.shape[1])

    halves_in = shard_bf["w_in"].reshape(2, shard_bf["w_in"].shape[0] // 2, -1)
    g_in, g_conv = _all_gather_chips([halves_in], [w["conv_w"][0]])
    g_in = lax.dynamic_update_index_in_dim(g_in, halves_in, chip, axis=0)
    g_conv = lax.dynamic_update_index_in_dim(g_conv, w["conv_w"][0], chip, axis=0)
    w_in_full = layout_for_compute("w_in", g_in.reshape(N_CHIPS, -1, g_in.shape[-1]))
    p = {n: w[n] for n in _SMALL}
    p["conv_w"] = g_conv.transpose(1, 0, 2).reshape(CONV_WIDTH, CONV_DIM)
    gather = _split_start("gather_late", [shard_bf[n] for n in _LATE], "gather", after=g_in)
    p["g_mix"] = p["g_mix"] + gather.token[:1, :1]

    def late_weights(after):
        srcs, lands = _split_wait("gather_late_wait", gather, "gather", after)
        lands = [lax.dynamic_update_index_in_dim(l, s, chip, axis=0) for l, s in zip(lands, srcs)]
        return {n: layout_for_compute(n, l) for n, l in zip(_LATE, lands)}

    scatter = {}

    def send_late_grads(grads):
        scatter["late"] = _split_start("scatter_late", [layout_for_reduction(n, grads[n]) for n in _LATE], "scatter",
                                       after=None)
        return scatter["late"].token

    def send_w_in_grad(g):
        scatter["w_in"] = _split_start("scatter_w_in", [layout_for_reduction("w_in", g)], "scatter", after=None)
        return scatter["w_in"].token

    loss_row, dx, gp = _layer_fwd_bwd(x[0], mem[0], loss_target[0], w_in_full, p, late_weights, send_late_grads,
                                      send_w_in_grad)

    grad, delta, new_m, new_v = {}, {}, {}, {}

    def finish(names, sources, from_chips, tag):
        mine = [_chip_sum(lax.dynamic_index_in_dim(s, chip, axis=0, keepdims=False), fc, "rs_chip_sum_" + n)
                for n, s, fc in zip(names, sources, from_chips)]
        for n, a, b in zip(names, mine, _sibling_swap(mine, "rs_sibling_swap_" + tag)):
            shape = w[n].shape
            res = _adamw(w[n][0], a, b, mom1[n][0], mom2[n][0], "adamw_" + n)
            grad[n], delta[n], new_m[n], new_v[n] = (r.reshape(shape) for r in res)

    finish(_LATE, *_split_wait("scatter_late_wait", scatter["late"], "scatter", (dx,)), "late")

    sources_in, from_chips_in = _split_wait("scatter_w_in_wait", scatter["w_in"], "scatter",
                                            tuple(new_v[n] for n in _LATE))

    packed, starts = _pack_rows([gp[n] for n in _SMALL] + [loss_row], CONV_DIM)
    summed = _all_reduce_small(packed, from_chips_in[0])
    loss = summed[starts[-1], 0]
    finish(["w_in"], sources_in, from_chips_in, "w_in")

    as_rows = lambda a: a.reshape(-1, a.shape[-1])
    results = _adamw_small(summed, starts, [as_rows(w[n]) for n in _SMALL], [as_rows(mom1[n]) for n in _SMALL],
                           [as_rows(mom2[n]) for n in _SMALL], _SMALL.index("conv_w"))
    for n, res in zip(_SMALL, results):
        grad[n], delta[n], new_m[n], new_v[n] = (a.reshape(w[n].shape) for a in res)

    return (loss, dx[None], *[grad[n] for n in _WEIGHTS], *[delta[n] for n in _WEIGHTS],
            *[new_m[n] for n in _WEIGHTS], *[new_v[n] for n in _WEIGHTS])
```

```python
from typing import NamedTuple

import jax
import jax.numpy as jnp
from jax import lax
from jax.experimental import pallas as pl
from jax.experimental.pallas import tpu as pltpu

F32 = jnp.float32
BF16 = jnp.bfloat16
HI = lax.Precision.HIGHEST
MESH = pl.DeviceIdType.MESH

EPS = 1e-5
CHUNK = 128
SSM_HEADS = 16
SSM_GROUPS = 2
HEADS_PER_GROUP = SSM_HEADS // SSM_GROUPS
HEAD_DIM = 64
SSM_STATE = 128
ATTN_HEADS = 16
XATTN_HEADS = 4
XATTN_DIM = 256
CONV_WIDTH = 4
N_CHIPS = 4
N_DEV = 8
LANES = 128
VMEM_LIMIT = 56 * 1024 * 1024

ADAM_LR = 0.001
ADAM_B1 = 0.9
ADAM_B2 = 0.999
ADAM_EPS = 1e-08
ADAM_WD = 0.01
ADAM_STEP = 10


def _params(sem):
    return pltpu.CompilerParams(dimension_semantics=sem, vmem_limit_bytes=VMEM_LIMIT)


def _pick(n, cands):
    for c in cands:
        if n % c == 0:
            return c
    return n


def _mm(a, b, mode, name, out_dtypes=(F32,), epilogue=None, extras=(), b_chunks=1, out_chunks=1,
        tm=None, tn=None, tk=None):
    if mode == "nn":
        M, K = a.shape
        N = b.shape[-1] * b_chunks
    elif mode == "nt":
        M, K = a.shape
        N = b.shape[-2]
        assert b.shape[-1] * b_chunks == K
    else:
        K, M = a.shape
        N = b.shape[-1] * b_chunks
    tm = tm or _pick(M, (2048, 1024, 512, 256, 128))
    tn = tn or _pick(N // max(b_chunks if mode != "nt" else 1, out_chunks), (512, 640, 384, 256, 128))
    if tk is None:
        kmax = b.shape[-1] if mode == "nt" else K
        tk = kmax if kmax <= 2048 else _pick(kmax, (2048, 1152, 1024, 512))
    nk = K // tk
    assert M % tm == 0 and N % tn == 0 and K % tk == 0
    grid = (M // tm, N // tn, nk)

    if mode == "tn":
        a_spec = pl.BlockSpec((tk, tm), lambda i, j, k: (k, i))
    else:
        a_spec = pl.BlockSpec((tm, tk), lambda i, j, k: (i, k))

    def b_index(t_row, t_last, tile_last):
        if b_chunks == 1:
            return (t_row, t_last)
        q = (b.shape[-1]) // tile_last
        return (t_last // q, t_row, t_last % q)

    if mode == "nn" or mode == "tn":
        bshape = (tk, tn)
        bmap = lambda i, j, k: b_index(k, j, tn)
    else:
        bshape = (tn, tk)
        bmap = lambda i, j, k: b_index(j, k, tk)
    if b_chunks > 1:
        bshape = (None,) + bshape
    b_spec = pl.BlockSpec(bshape, bmap)

    if out_chunks == 1:
        o_spec = pl.BlockSpec((tm, tn), lambda i, j, k: (i, j))
        o_shape = (M, N)
    else:
        qo = (N // out_chunks) // tn
        o_spec = pl.BlockSpec((None, tm, tn), lambda i, j, k: (j // qo, i, j % qo))
        o_shape = (out_chunks, M, N // out_chunks)
    e_spec = pl.BlockSpec((tm, tn), lambda i, j, k: (i, j))

    dims = {"nn": (((1,), (0,)), ((), ())), "nt": (((1,), (1,)), ((), ())), "tn": (((0,), (0,)), ((), ()))}[mode]
    n_ex = len(extras)
    n_out = len(out_dtypes)

    def body(*refs):
        a_ref, b_ref = refs[0], refs[1]
        ex_refs = refs[2:2 + n_ex]
        o_refs = refs[2 + n_ex:2 + n_ex + n_out]

        def finish(acc):
            outs = epilogue(acc, *[r[...] for r in ex_refs]) if epilogue is not None else (acc,)
            for r, o in zip(o_refs, outs):
                r[...] = o.astype(r.dtype)

        part = lax.dot_general(a_ref[...].astype(BF16), b_ref[...].astype(BF16), dims,
                               preferred_element_type=F32)
        if nk == 1:
            finish(part)
        else:
            acc_ref = refs[-1]
            k = pl.program_id(2)

            @pl.when(k == 0)
            def _():
                acc_ref[...] = part

            @pl.when(k > 0)
            def _():
                acc_ref[...] += part

            @pl.when(k == nk - 1)
            def _():
                finish(acc_ref[...])

    outs = pl.pallas_call(
        body,
        grid=grid,
        in_specs=[a_spec, b_spec] + [e_spec] * n_ex,
        out_specs=[o_spec] * n_out,
        out_shape=[jax.ShapeDtypeStruct(o_shape, d) for d in out_dtypes],
        scratch_shapes=[pltpu.VMEM((tm, tn), F32)] if nk > 1 else [],
        compiler_params=_params(("parallel", "parallel", "arbitrary")),
        name=name,
    )(a, b, *extras)
    return outs[0] if n_out == 1 else outs


def _rms(x, g):
    r = lax.rsqrt(jnp.mean(x * x, axis=-1, keepdims=True) + EPS)
    return x * r * g


def _rmsnorm_fwd(x, g, name):
    R, D = x.shape
    tr = _pick(R, (512, 256))

    def body(x_ref, g_ref, o_ref):
        o_ref[...] = _rms(x_ref[...], g_ref[...]).astype(o_ref.dtype)

    return pl.pallas_call(
        body, grid=(R // tr,),
        in_specs=[pl.BlockSpec((tr, D), lambda i: (i, 0)), pl.BlockSpec((1, D), lambda i: (0, 0))],
        out_specs=pl.BlockSpec((tr, D), lambda i: (i, 0)),
        out_shape=jax.ShapeDtypeStruct((R, D), BF16),
        compiler_params=_params(("parallel",)), name=name)(x, g)


def _rmsnorm_bwd(x, g, dh, dres, name):
    R, D = x.shape
    tr = _pick(R, (256,))
    has_res = dres is not None

    def body(*refs):
        if has_res:
            x_ref, g_ref, dh_ref, dres_ref, dx_ref, dg_ref = refs
        else:
            x_ref, g_ref, dh_ref, dx_ref, dg_ref = refs
        _, vjp = jax.vjp(_rms, x_ref[...], g_ref[...])
        dx, dg = vjp(dh_ref[...])
        if has_res:
            dx = dx + dres_ref[...]
        dx_ref[...] = dx

        @pl.when(pl.program_id(0) == 0)
        def _():
            dg_ref[...] = jnp.zeros_like(dg_ref)

        dg_ref[...] += dg

    row = pl.BlockSpec((tr, D), lambda i: (i, 0))
    vec = pl.BlockSpec((1, D), lambda i: (0, 0))
    ins = [x, g, dh] + ([dres] if has_res else [])
    return pl.pallas_call(
        body, grid=(R // tr,),
        in_specs=[row, vec, row] + ([row] if has_res else []),
        out_specs=[row, vec],
        out_shape=[jax.ShapeDtypeStruct((R, D), F32), jax.ShapeDtypeStruct((1, D), F32)],
        compiler_params=_params(("arbitrary",)), name=name)(*ins)


def _shift_down(u, k):
    if k == 0:
        return u
    rows = lax.broadcasted_iota(jnp.int32, u.shape, 0)
    return jnp.where(rows >= k, pltpu.roll(u, k, axis=0), 0.0)


def _shift_up(u, k):
    if k == 0:
        return u
    n = u.shape[0]
    rows = lax.broadcasted_iota(jnp.int32, u.shape, 0)
    return jnp.where(rows < n - k, pltpu.roll(u, n - k, axis=0), 0.0)


def _conv_pre(u, w, b):
    pre = b
    for j in range(CONV_WIDTH):
        pre = pre + w[j:j + 1, :] * _shift_down(u, CONV_WIDTH - 1 - j)
    return pre


def _conv_fwd(proj, col0, ncols, conv_w, conv_b):
    S = proj.shape[0]
    cb0 = col0 // LANES

    def body(u_ref, w_ref, b_ref, o_ref):
        pre = _conv_pre(u_ref[...], w_ref[...], b_ref[...])
        o_ref[...] = pre * jax.nn.sigmoid(pre)

    return pl.pallas_call(
        body, grid=(ncols // LANES,),
        in_specs=[pl.BlockSpec((S, LANES), lambda j: (0, j + cb0)),
                  pl.BlockSpec((CONV_WIDTH, LANES), lambda j: (0, j)),
                  pl.BlockSpec((1, LANES), lambda j: (0, j))],
        out_specs=pl.BlockSpec((S, LANES), lambda j: (0, j)),
        out_shape=jax.ShapeDtypeStruct((S, ncols), F32),
        compiler_params=_params(("parallel",)), name="conv_fwd")(proj, conv_w, conv_b)


def _conv_bwd(proj, col0, ncols, conv_w, conv_b, douts, dproj):
    S = proj.shape[0]
    cb0 = col0 // LANES
    starts = [0]
    for d in douts:
        starts.append(starts[-1] + d.shape[1] // LANES)
    assert starts[-1] == ncols // LANES
    nd = len(douts)

    def body(u_ref, w_ref, b_ref, *rest):
        d_refs, (du_ref, dw_ref, db_ref) = rest[:nd], rest[nd + 1:]
        j = pl.program_id(0)
        dout = d_refs[-1][...]
        for i in range(nd - 2, -1, -1):
            dout = jnp.where(j < starts[i + 1], d_refs[i][...], dout)
        u = u_ref[...]
        w = w_ref[...]
        pre = _conv_pre(u, w, b_ref[...])
        s = jax.nn.sigmoid(pre)
        dpre = dout * (s * (1.0 + pre * (1.0 - s)))
        du = jnp.zeros_like(u)
        rows = []
        for j in range(CONV_WIDTH):
            k = CONV_WIDTH - 1 - j
            du = du + w[j:j + 1, :] * _shift_up(dpre, k)
            rows.append(jnp.sum(dpre * _shift_down(u, k), axis=0, keepdims=True))
        du_ref[...] = du.astype(du_ref.dtype)
        rows.append(jnp.zeros((8 - CONV_WIDTH, LANES), F32))
        dw_ref[...] = jnp.concatenate(rows, axis=0)
        db_ref[...] = jnp.sum(dpre, axis=0, keepdims=True)

    return pl.pallas_call(
        body, grid=(ncols // LANES,),
        in_specs=[pl.BlockSpec((S, LANES), lambda j: (0, j + cb0)),
                  pl.BlockSpec((CONV_WIDTH, LANES), lambda j: (0, j)),
                  pl.BlockSpec((1, LANES), lambda j: (0, j))]
        + [pl.BlockSpec((S, LANES), lambda j, lo=starts[i], hi=starts[i + 1]: (0, jnp.clip(j - lo, 0, hi - lo - 1)))
           for i in range(nd)] + [_ANY],
        out_specs=[pl.BlockSpec((S, LANES), lambda j: (0, j + cb0)),
                   pl.BlockSpec((8, LANES), lambda j: (0, j)),
                   pl.BlockSpec((1, LANES), lambda j: (0, j))],
        out_shape=[jax.ShapeDtypeStruct(dproj.shape, dproj.dtype),
                   jax.ShapeDtypeStruct((8, ncols), F32),
                   jax.ShapeDtypeStruct((1, ncols), F32)],
        input_output_aliases={3 + nd: 0},
        compiler_params=_params(("parallel",)), name="conv_bwd")(proj, conv_w, conv_b, *douts, dproj)


def _softplus(x):
    return jnp.maximum(x, 0.0) + jnp.log1p(jnp.exp(-jnp.abs(x)))


def _dot32(a, b, dims=(((1,), (0,)), ((), ()))):
    return lax.dot_general(a, b, dims, precision=HI, preferred_element_type=F32)


def _dotd(a, b, dims=(((1,), (0,)), ((), ()))):
    return lax.dot_general(a, b, dims, preferred_element_type=F32)


PAIRS_PER_GROUP = HEADS_PER_GROUP // 2


def _ssd_chunk(xs, Bm, Cm, z, dtr, dtb, alog, dsk, nw, h):
    L = Bm.shape[0]
    ri = lax.broadcasted_iota(jnp.int32, (L, L), 0)
    ci = lax.broadcasted_iota(jnp.int32, (L, L), 1)
    causal = ri >= ci
    tril = causal.astype(F32)
    first = _first_head(L)
    first1 = _first_head(1)
    CB = _dotd(Cm, Bm, _NT)
    gated, hnew = [], []
    ssq = jnp.zeros((L, 1), F32)
    for pp in range(len(xs)):
        dts, cums, tots, decay = [], [], [], []
        for a in range(2):
            r = 2 * pp + a
            dt = _softplus(dtr[r] + dtb[r])
            dA = dt * (-jnp.exp(alog[r]))
            acs = _dot32(tril, dA)
            cc = jnp.broadcast_to(acs, (L, L))
            decay.append(CB * jnp.exp(jnp.where(causal, cc - cc.T, -1e30)))
            dts.append(dt)
            cums.append(acs)
            tots.append(jnp.sum(dA, axis=0, keepdims=True))
        dt2 = jnp.where(first, dts[0], dts[1])
        acs2 = jnp.where(first, cums[0], cums[1])
        tot2 = jnp.where(first1, tots[0], tots[1])
        dsk2 = jnp.where(first1, dsk[2 * pp], dsk[2 * pp + 1])
        X = xs[pp] * dt2
        y = (jnp.where(first, _dotd(decay[0], X), _dotd(decay[1], X)) + jnp.exp(acs2) * _dotd(Cm, h[pp])
             + dsk2 * xs[pp])
        hnew.append(jnp.exp(tot2) * h[pp] + _dotd(Bm, X * jnp.exp(tot2 - acs2), _TN))
        g = y * (z[pp] * jax.nn.sigmoid(z[pp]))
        ssq = ssq + jnp.sum(g * g, axis=-1, keepdims=True)
        gated.append(g)
    rs = lax.rsqrt(ssq / (len(xs) * LANES) + EPS)
    return [g * rs * nw[pp] for pp, g in enumerate(gated)], hnew


def _ssd_args(xs_ref, b_ref, c_ref, z_ref, dt_ref, dtb_ref, al_ref, dsk_ref, nw_ref, h_ref):
    pairs = range(PAIRS_PER_GROUP)
    heads = range(HEADS_PER_GROUP)
    lanes = lambda ref, pp: ref[:, pp * LANES:(pp + 1) * LANES]
    return ([lanes(xs_ref, pp) for pp in pairs], b_ref[...], c_ref[...], [lanes(z_ref, pp) for pp in pairs],
            [dt_ref[r] for r in heads], [dtb_ref[r] for r in heads], [al_ref[r] for r in heads],
            [dsk_ref[r] for r in heads], [lanes(nw_ref, pp) for pp in pairs], [h_ref[pp] for pp in pairs])


def _ssd_specs(rev):
    H, N, L = HEADS_PER_GROUP, SSM_STATE, CHUNK
    gw = H * HEAD_DIM
    return dict(
        cols=lambda col0: pl.BlockSpec((L, gw), lambda g, c: (rev(c), col0 // gw + g)),
        bc=lambda first_block: pl.BlockSpec((L, N), lambda g, c: (rev(c), first_block + g)),
        dt=pl.BlockSpec((H, L, 1), lambda g, c: (g, rev(c), 0)),
        scal=pl.BlockSpec((H, 1, 1), lambda g, c: (g, 0, 0)),
        nw=pl.BlockSpec((1, gw), lambda g, c: (0, g)),
        hs=pl.BlockSpec((None, PAIRS_PER_GROUP, N, LANES), lambda g, c: (rev(c), g, 0, 0)),
        b_block=SSM_INNER // N,
    )


def _ssd_fwd(xbc, proj, dt_hm, dtb, alog, dsk, nw):
    S = xbc.shape[0]
    N, L = SSM_STATE, CHUNK
    nc = S // L
    sp = _ssd_specs(lambda c: c)

    def body(xs_ref, b_ref, c_ref, z_ref, dt_ref, dtb_ref, al_ref, dsk_ref, nw_ref, y_ref, hs_ref, h_ref):
        @pl.when(pl.program_id(1) == 0)
        def _():
            h_ref[...] = jnp.zeros_like(h_ref)

        hs_ref[...] = h_ref[...]
        out, hnew = _ssd_chunk(*_ssd_args(xs_ref, b_ref, c_ref, z_ref, dt_ref, dtb_ref, al_ref, dsk_ref, nw_ref, h_ref))
        for pp in range(PAIRS_PER_GROUP):
            y_ref[:, pp * LANES:(pp + 1) * LANES] = out[pp].astype(y_ref.dtype)
            h_ref[pp] = hnew[pp]

    return pl.pallas_call(
        body, grid=(SSM_GROUPS, nc),
        in_specs=[sp["cols"](0), sp["bc"](sp["b_block"]), sp["bc"](sp["b_block"] + SSM_GROUPS), sp["cols"](COL_Z),
                  sp["dt"], sp["scal"], sp["scal"], sp["scal"], sp["nw"]],
        out_specs=[sp["cols"](0), sp["hs"]],
        out_shape=[jax.ShapeDtypeStruct((S, SSM_INNER), BF16),
                   jax.ShapeDtypeStruct((nc, SSM_HEADS // 2, N, LANES), F32)],
        scratch_shapes=[pltpu.VMEM((PAIRS_PER_GROUP, N, LANES), F32)],
        compiler_params=_params(("parallel", "arbitrary")), name="ssd_fwd",
    )(xbc, xbc, xbc, proj, dt_hm, dtb, alog, dsk, nw)


def _ssd_bwd(xbc, proj, dt_hm, dtb, alog, dsk, nw, hs, dmixed, dproj):
    S = xbc.shape[0]
    N, L = SSM_STATE, CHUNK
    nc = S // L
    sp = _ssd_specs(lambda c: nc - 1 - c)

    def body(xs_ref, b_ref, c_ref, z_ref, dt_ref, dtb_ref, al_ref, dsk_ref, nw_ref, hs_ref, dy_ref, buf_ref,
             dxs_ref, dz_ref, db_ref, dc_ref, ddt_ref, ddtb_ref, dal_ref, ddsk_ref, dnw_ref, dh_ref):
        @pl.when(pl.program_id(1) == 0)
        def _():
            dh_ref[...] = jnp.zeros_like(dh_ref)
            ddtb_ref[...] = jnp.zeros_like(ddtb_ref)
            dal_ref[...] = jnp.zeros_like(dal_ref)
            ddsk_ref[...] = jnp.zeros_like(ddsk_ref)
            dnw_ref[...] = jnp.zeros_like(dnw_ref)

        pairs = range(PAIRS_PER_GROUP)
        lanes = lambda pp: slice(pp * LANES, (pp + 1) * LANES)
        _, vjp = jax.vjp(_ssd_chunk, *_ssd_args(xs_ref, b_ref, c_ref, z_ref, dt_ref, dtb_ref, al_ref, dsk_ref, nw_ref,
                                                hs_ref))
        dxs, dB, dC, dz, ddt, ddtb, dal, ddsk, dnw, dh = vjp(([dy_ref[:, lanes(pp)] for pp in pairs],
                                                              [dh_ref[pp] for pp in pairs]))
        db_ref[...] = dB
        dc_ref[...] = dC
        for pp in pairs:
            dxs_ref[:, lanes(pp)] = dxs[pp]
            dz_ref[:, lanes(pp)] = dz[pp].astype(dz_ref.dtype)
            dnw_ref[:, lanes(pp)] += dnw[pp]
            dh_ref[pp] = dh[pp]
        for r in range(HEADS_PER_GROUP):
            ddt_ref[r] = ddt[r]
            ddtb_ref[r] += ddtb[r]
            dal_ref[r] += dal[r]
            ddsk_ref[r] += ddsk[r]

    bc_out = pl.BlockSpec((L, N), lambda g, c: (nc - 1 - c, g))
    return pl.pallas_call(
        body, grid=(SSM_GROUPS, nc),
        in_specs=[sp["cols"](0), sp["bc"](sp["b_block"]), sp["bc"](sp["b_block"] + SSM_GROUPS), sp["cols"](COL_Z),
                  sp["dt"], sp["scal"], sp["scal"], sp["scal"], sp["nw"], sp["hs"], sp["cols"](0), _ANY],
        out_specs=[sp["cols"](0), sp["cols"](COL_Z), bc_out, bc_out, sp["dt"], sp["scal"], sp["scal"], sp["scal"],
                   sp["nw"]],
        input_output_aliases={11: 1},
        out_shape=[jax.ShapeDtypeStruct((S, SSM_INNER), F32), jax.ShapeDtypeStruct(dproj.shape, dproj.dtype),
                   jax.ShapeDtypeStruct((S, SSM_GROUPS * N), F32), jax.ShapeDtypeStruct((S, SSM_GROUPS * N), F32),
                   jax.ShapeDtypeStruct((SSM_HEADS, S, 1), F32),
                   jax.ShapeDtypeStruct((SSM_HEADS, 1, 1), F32), jax.ShapeDtypeStruct((SSM_HEADS, 1, 1), F32),
                   jax.ShapeDtypeStruct((SSM_HEADS, 1, 1), F32), jax.ShapeDtypeStruct((1, SSM_INNER), F32)],
        scratch_shapes=[pltpu.VMEM((PAIRS_PER_GROUP, N, LANES), F32)],
        compiler_params=_params(("parallel", "arbitrary")), name="ssd_bwd",
    )(xbc, xbc, xbc, proj, dt_hm, dtb, alog, dsk, nw, hs, dmixed, dproj)


ATTN_SCALE = HEAD_DIM ** -0.5
ATTN_PAIRS = ATTN_HEADS // 2


def _first_head(rows):
    return lax.broadcasted_iota(jnp.int32, (rows, LANES), 1) < HEAD_DIM


def _pair_norm(x, g2, scale):
    first = _first_head(x.shape[0])
    sq = x * x
    ms0 = jnp.sum(jnp.where(first, sq, 0.0), axis=-1, keepdims=True) * (1.0 / HEAD_DIM)
    ms1 = jnp.sum(jnp.where(first, 0.0, sq), axis=-1, keepdims=True) * (1.0 / HEAD_DIM)
    r = jnp.where(first, lax.rsqrt(ms0 + EPS), lax.rsqrt(ms1 + EPS))
    return x * r * g2 * scale


def _qk_prep_fwd(proj, gq2, gk2):
    S = proj.shape[0]
    tq = _pick(S, (512, 256))

    def body(q_ref, k_ref, v_ref, gq_ref, gk_ref, qo_ref, ko_ref, vo_ref):
        qo_ref[...] = _pair_norm(q_ref[...], gq_ref[...], ATTN_SCALE).astype(BF16)
        ko_ref[...] = _pair_norm(k_ref[...], gk_ref[...], 1.0).astype(BF16)
        vo_ref[...] = v_ref[...].astype(BF16)

    col = lambda c0: pl.BlockSpec((tq, LANES), lambda h, i: (i, c0 // LANES + h))
    blk = pl.BlockSpec((tq, LANES), lambda h, i: (i, h))
    vec = pl.BlockSpec((1, LANES), lambda h, i: (0, 0))
    return pl.pallas_call(
        body, grid=(ATTN_PAIRS, S // tq), in_specs=[col(COL_Q), col(COL_K), col(COL_V), vec, vec],
        out_specs=[blk, blk, blk], out_shape=[jax.ShapeDtypeStruct((S, ATTN_WIDTH), BF16)] * 3,
        compiler_params=_params(("parallel", "parallel")), name="qk_prep_fwd")(proj, proj, proj, gq2, gk2)


def _pair_norm_bwd(proj, col0, g2, scale, dn, dproj, name):
    S = proj.shape[0]
    tq = _pick(S, (512, 256))

    def body(u_ref, g_ref, dn_ref, buf_ref, du_ref, dg_ref):
        @pl.when((pl.program_id(0) == 0) & (pl.program_id(1) == 0))
        def _():
            dg_ref[...] = jnp.zeros_like(dg_ref)

        _, vjp = jax.vjp(lambda u, g: _pair_norm(u, g, scale), u_ref[...], g_ref[...])
        du, dg = vjp(dn_ref[...])
        du_ref[...] = du.astype(du_ref.dtype)
        dg_ref[...] += dg

    ublk = pl.BlockSpec((tq, LANES), lambda h, i: (i, col0 // LANES + h))
    blk = pl.BlockSpec((tq, LANES), lambda h, i: (i, h))
    vec = pl.BlockSpec((1, LANES), lambda h, i: (0, 0))
    return pl.pallas_call(
        body, grid=(ATTN_PAIRS, S // tq), in_specs=[ublk, vec, blk, _ANY], out_specs=[ublk, vec],
        out_shape=[jax.ShapeDtypeStruct(dproj.shape, dproj.dtype), jax.ShapeDtypeStruct((1, LANES), F32)],
        input_output_aliases={3: 0},
        compiler_params=_params(("arbitrary", "arbitrary")), name=name)(proj, g2, dn, dproj)


def _logf_cumsum_fwd(f_raw, f_bias):
    S, Hh = f_raw.shape
    L = CHUNK

    def body(f_ref, b_ref, o_ref, wide_ref):
        ri = lax.broadcasted_iota(jnp.int32, (L, L), 0)
        ci = lax.broadcasted_iota(jnp.int32, (L, L), 1)
        tril = (ri >= ci).astype(F32)
        carry = jnp.zeros((1, Hh), F32)
        for c in range(S // L):
            rows = slice(c * L, (c + 1) * L)
            lf = -_softplus(-(f_ref[rows, :] + b_ref[...]))
            cum = _dot32(tril, lf) + carry
            o_ref[rows, :] = cum
            for h in range(Hh):
                wide_ref[rows, h * HEAD_DIM:(h + 1) * HEAD_DIM] = jnp.broadcast_to(cum[:, h:h + 1], (L, HEAD_DIM))
            carry = cum[L - 1:L, :]

    return pl.pallas_call(
        body, out_shape=[jax.ShapeDtypeStruct((S, Hh), F32), jax.ShapeDtypeStruct((S, Hh * HEAD_DIM), F32)],
        name="logf_cumsum_fwd")(f_raw, f_bias)


def _logf_cumsum_bwd(f_raw, f_bias, dcum):
    S, Hh = f_raw.shape
    L = CHUNK

    def body(f_ref, b_ref, d_ref, df_ref, db_ref):
        ri = lax.broadcasted_iota(jnp.int32, (L, L), 0)
        ci = lax.broadcasted_iota(jnp.int32, (L, L), 1)
        triu = (ri <= ci).astype(F32)
        carry = jnp.zeros((1, Hh), F32)
        db = jnp.zeros((1, Hh), F32)
        for c in reversed(range(S // L)):
            suf = _dot32(triu, d_ref[c * L:(c + 1) * L, :]) + carry
            df = suf * jax.nn.sigmoid(-(f_ref[c * L:(c + 1) * L, :] + b_ref[...]))
            df_ref[c * L:(c + 1) * L, :] = df
            db = db + jnp.sum(df, axis=0, keepdims=True)
            carry = suf[0:1, :]
        db_ref[...] = db

    return pl.pallas_call(
        body, out_shape=[jax.ShapeDtypeStruct((S, Hh), F32), jax.ShapeDtypeStruct((1, Hh), F32)],
        name="logf_cumsum_bwd")(f_raw, f_bias, dcum)


_NT = (((1,), (1,)), ((), ()))
_TN = (((0,), (0,)), ((), ()))


def _mxu(a, b, dims=(((1,), (0,)), ((), ()))):
    return lax.dot_general(a, b, dims, preferred_element_type=F32)


def _flash_fwd(qs, kn, vb, cq, ck):
    S, W = qs.shape
    tq = tk = _pick(S, (512, 256))
    nmask = max(tq // tk, 1)

    def body(q_ref, k_ref, v_ref, cq_ref, ck_ref, o_ref, of_ref, lse_ref):
        i = pl.program_id(1)
        first = _first_head(tq)
        q2 = q_ref[...]
        zero = jnp.zeros_like(q2)
        qa = (jnp.where(first, q2, zero), jnp.where(first, zero, q2))
        cqa = (cq_ref[:, 0:1], cq_ref[:, HEAD_DIM:HEAD_DIM + 1])
        row0 = i * tq

        def step(j, carry, masked):
            ms, ls, acc, rem = carry
            off = pl.multiple_of(j * tk, tk)
            k = k_ref[pl.ds(off, tk), :]
            v = v_ref[pl.ds(off, tk), :]
            new_m, new_l, alphas, pvs, prs = [], [], [], [], []
            for a in range(2):
                s = _mxu(qa[a], k, _NT) + cqa[a] - ck_ref[a, :, pl.ds(off, tk)]
                if masked:
                    ri = lax.broadcasted_iota(jnp.int32, (tq, tk), 0) + row0
                    ci = lax.broadcasted_iota(jnp.int32, (tq, tk), 1) + off
                    s = jnp.where(ri >= ci, s, -1e30)
                m_new = jnp.maximum(ms[a], jnp.max(s, axis=-1, keepdims=True))
                alpha = jnp.exp(ms[a] - m_new)
                p = jnp.exp(s - m_new)
                new_l.append(alpha * ls[a] + jnp.sum(p, axis=-1, keepdims=True))
                new_m.append(m_new)
                alphas.append(alpha)
                p_hi = p.astype(BF16)
                pvs.append(_mxu(p_hi, v))
                prs.append(_mxu((p - p_hi.astype(F32)).astype(BF16), v))
            al = jnp.where(first, alphas[0], alphas[1])
            acc = al * acc + jnp.where(first, pvs[0], pvs[1])
            rem = al * rem + jnp.where(first, prs[0], prs[1])
            return tuple(new_m), tuple(new_l), acc, rem

        neg = jnp.full((tq, 1), -1e30, F32)
        z1 = jnp.zeros((tq, 1), F32)
        z2 = jnp.zeros((tq, LANES), F32)
        carry = ((neg, neg), (z1, z1), z2, z2)
        n_full = (i * tq) // tk
        carry = lax.fori_loop(0, n_full, lambda j, c: step(j, c, False), carry)
        for jj in range(nmask):
            carry = step(n_full + jj, carry, True)
        ms, ls, acc, rem = carry
        linv = jnp.where(first, 1.0 / ls[0], 1.0 / ls[1])
        o_ref[...] = (acc * linv).astype(o_ref.dtype)
        of_ref[...] = (acc + rem) * linv
        lse_ref[...] = jnp.where(first, ms[0] + jnp.log(ls[0]), ms[1] + jnp.log(ls[1]))

    qblk = pl.BlockSpec((tq, LANES), lambda h, i: (i, h))
    full = pl.BlockSpec((S, LANES), lambda h, i: (0, h))
    return pl.pallas_call(
        body, grid=(W // LANES, S // tq),
        in_specs=[qblk, full, full, qblk, pl.BlockSpec((2, 1, S), lambda h, i: (h, 0, 0))],
        out_specs=[qblk, qblk, qblk],
        out_shape=[jax.ShapeDtypeStruct((S, W), BF16), jax.ShapeDtypeStruct((S, W), F32),
                   jax.ShapeDtypeStruct((S, W), F32)],
        compiler_params=_params(("parallel", "parallel")), name="flash_fwd")(qs, kn, vb, cq, ck)


def _flash_bwd(qs, kn, vb, cq, ck, o_fine, do, do_col0, lse):
    S, W = qs.shape
    tq = tk = _pick(S, (512, 256))
    nq = S // tq
    nmask = max(tk // tq, 1)

    def body(q_ref, k_ref, v_ref, cq_ref, ck_ref, of_ref, do_ref, lse_ref, dq_ref, dk_ref, dv_ref, dck_ref):
        j = pl.program_id(1)

        @pl.when(j == 0)
        def _():
            dq_ref[...] = jnp.zeros_like(dq_ref)

        firstk = _first_head(tk)
        firstq = _first_head(tq)
        k2 = k_ref[...]
        v2 = v_ref[...]
        zk = jnp.zeros_like(k2)
        ka = (jnp.where(firstk, k2, zk), jnp.where(firstk, zk, k2))
        va = (jnp.where(firstk, v2, zk), jnp.where(firstk, zk, v2))
        cka = (ck_ref[0], ck_ref[1])
        col0 = j * tk

        def step(i, carry, masked):
            dk, dv, dck0, dck1 = carry
            dcks = [dck0, dck1]
            off = pl.multiple_of(i * tq, tq)
            rows = pl.ds(off, tq)
            q2 = q_ref[rows, :]
            dob = do_ref[rows, :].astype(BF16)
            prod = dob.astype(F32) * of_ref[rows, :]
            dkp, dvp, dqp = [], [], []
            for a in range(2):
                lane = pl.ds(a * HEAD_DIM, 1)
                s = _mxu(q2, ka[a], _NT) + cq_ref[rows, lane] - cka[a]
                if masked:
                    ri = lax.broadcasted_iota(jnp.int32, (tq, tk), 0) + off
                    ci = lax.broadcasted_iota(jnp.int32, (tq, tk), 1) + col0
                    s = jnp.where(ri >= ci, s, -1e30)
                p = jnp.exp(s - lse_ref[rows, lane])
                dp = _mxu(dob, va[a], _NT)
                own = jnp.where(firstq, prod, 0.0) if a == 0 else jnp.where(firstq, 0.0, prod)
                ds = p * (dp - jnp.sum(own, axis=-1, keepdims=True))
                dsb = ds.astype(BF16)
                dvp.append(_mxu(p.astype(BF16), dob, _TN))
                dkp.append(_mxu(dsb, q2, _TN))
                dqp.append(_mxu(dsb, k2))
                dcks[a] = dcks[a] - jnp.sum(ds, axis=0, keepdims=True)
            dq_ref[rows, :] += jnp.where(firstq, dqp[0], dqp[1])
            dk = dk + jnp.where(firstk, dkp[0], dkp[1])
            dv = dv + jnp.where(firstk, dvp[0], dvp[1])
            return dk, dv, dcks[0], dcks[1]

        z2 = jnp.zeros((tk, LANES), F32)
        z1 = jnp.zeros((1, tk), F32)
        carry = (z2, z2, z1, z1)
        i0 = (j * tk) // tq
        for ii in range(nmask):
            carry = step(i0 + ii, carry, True)
        dk, dv, dck0, dck1 = lax.fori_loop(i0 + nmask, nq, lambda i, c: step(i, c, False), carry)
        dk_ref[...] = dk
        dv_ref[...] = dv.astype(dv_ref.dtype)
        dck_ref[0] = dck0
        dck_ref[1] = dck1

    kblk = pl.BlockSpec((tk, LANES), lambda h, j: (j, h))
    full = pl.BlockSpec((S, LANES), lambda h, j: (0, h))
    dofull = pl.BlockSpec((S, LANES), lambda h, j: (0, do_col0 // LANES + h))
    rowt = pl.BlockSpec((2, 1, tk), lambda h, j: (h, 0, j))
    dvblk = pl.BlockSpec((tk, LANES), lambda h, j: (j, COL_V // LANES + h))
    return pl.pallas_call(
        body, grid=(W // LANES, S // tk),
        in_specs=[full, kblk, kblk, full, rowt, full, dofull, full],
        out_specs=[full, kblk, dvblk, rowt],
        out_shape=[jax.ShapeDtypeStruct((S, W), F32), jax.ShapeDtypeStruct((S, W), F32),
                   jax.ShapeDtypeStruct((S, IN_COLS_PAD), BF16), jax.ShapeDtypeStruct((2 * (W // LANES), 1, S), F32)],
        compiler_params=_params(("parallel", "arbitrary")), name="flash_bwd")(qs, kn, vb, cq, ck, o_fine, do, lse)


XATTN_SCALE = XATTN_DIM ** -0.5


def _xq_norm(q, g):
    return _rms(q, g) * XATTN_SCALE


def _xattn_fwd(xq, kv, gq, gk):
    S = xq.shape[0]
    Mm = kv.shape[0]
    Dh = XATTN_DIM
    tq = _pick(S, (512, 256))

    def body(q_ref, k_ref, v_ref, gq_ref, gk_ref, o_ref):
        qn = _xq_norm(q_ref[...], gq_ref[...]).astype(BF16)
        kn = _rms(k_ref[...], gk_ref[...]).astype(BF16)
        s = _mxu(qn, kn, _NT)
        m = jnp.max(s, axis=-1, keepdims=True)
        p = jnp.exp(s - m)
        l = jnp.sum(p, axis=-1, keepdims=True)
        o_ref[...] = (_mxu(p.astype(BF16), v_ref[...].astype(BF16)) / l).astype(o_ref.dtype)

    vec = pl.BlockSpec((1, Dh), lambda h, i: (0, 0))
    return pl.pallas_call(
        body, grid=(XATTN_HEADS, S // tq),
        in_specs=[pl.BlockSpec((tq, Dh), lambda h, i: (i, h)), pl.BlockSpec((Mm, Dh), lambda h, i: (0, h)),
                  pl.BlockSpec((Mm, Dh), lambda h, i: (0, XATTN_HEADS + h)), vec, vec],
        out_specs=pl.BlockSpec((tq, Dh), lambda h, i: (i, h)),
        out_shape=jax.ShapeDtypeStruct((S, XATTN_HEADS * Dh), BF16),
        compiler_params=_params(("parallel", "parallel")), name="xattn_fwd")(xq, kv, kv, gq, gk)


def _xattn_bwd(xq, kv, gq, gk, do):
    S = xq.shape[0]
    Mm = kv.shape[0]
    Dh = XATTN_DIM
    tq = _pick(S, (512, 256))
    nq = S // tq

    def body(q_ref, k_ref, v_ref, gq_ref, gk_ref, do_ref, dq_ref, dk_ref, dv_ref, dgq_ref, dgk_ref, dkn_acc, dv_acc):
        h = pl.program_id(0)
        i = pl.program_id(1)

        @pl.when((h == 0) & (i == 0))
        def _():
            dgq_ref[...] = jnp.zeros_like(dgq_ref)
            dgk_ref[...] = jnp.zeros_like(dgk_ref)

        @pl.when(i == 0)
        def _():
            dkn_acc[...] = jnp.zeros_like(dkn_acc)
            dv_acc[...] = jnp.zeros_like(dv_acc)

        qn32, vq = jax.vjp(_xq_norm, q_ref[...], gq_ref[...])
        kn32, vk = jax.vjp(_rms, k_ref[...], gk_ref[...])
        qn = qn32.astype(BF16)
        kn = kn32.astype(BF16)
        vb = v_ref[...].astype(BF16)
        s = _mxu(qn, kn, _NT)
        m = jnp.max(s, axis=-1, keepdims=True)
        p = jnp.exp(s - m)
        p = p / jnp.sum(p, axis=-1, keepdims=True)
        dob = do_ref[...].astype(BF16)
        dp = _mxu(dob, vb, _NT)
        delta = jnp.sum(p * dp, axis=-1, keepdims=True)
        ds = (p * (dp - delta)).astype(BF16)
        dv_acc[...] += _mxu(p.astype(BF16), dob, _TN)
        dkn_acc[...] += _mxu(ds, qn, _TN)
        dq, dgq = vq(_mxu(ds, kn))
        dq_ref[...] = dq.astype(dq_ref.dtype)
        dgq_ref[...] += dgq

        @pl.when(i == nq - 1)
        def _():
            dk, dgk = vk(dkn_acc[...])
            dk_ref[...] = dk.astype(dk_ref.dtype)
            dv_ref[...] = dv_acc[...].astype(dv_ref.dtype)
            dgk_ref[...] += dgk

    vec = pl.BlockSpec((1, Dh), lambda h, i: (0, 0))
    qblk = pl.BlockSpec((tq, Dh), lambda h, i: (i, h))
    kblk = pl.BlockSpec((Mm, Dh), lambda h, i: (0, h))
    vblk = pl.BlockSpec((Mm, Dh), lambda h, i: (0, XATTN_HEADS + h))
    return pl.pallas_call(
        body, grid=(XATTN_HEADS, nq),
        in_specs=[qblk, kblk, vblk, vec, vec, qblk],
        out_specs=[qblk, kblk, kblk, vec, vec],
        out_shape=[jax.ShapeDtypeStruct((S, XATTN_HEADS * Dh), BF16),
                   jax.ShapeDtypeStruct((Mm, XATTN_HEADS * Dh), BF16),
                   jax.ShapeDtypeStruct((Mm, XATTN_HEADS * Dh), BF16),
                   jax.ShapeDtypeStruct((1, Dh), F32), jax.ShapeDtypeStruct((1, Dh), F32)],
        scratch_shapes=[pltpu.VMEM((Mm, Dh), F32), pltpu.VMEM((Mm, Dh), F32)],
        compiler_params=_params(("arbitrary", "arbitrary")), name="xattn_bwd")(xq, kv, kv, gq, gk, do)


def _loss_head(y, target):
    S, D = y.shape
    tr = _pick(S, (512, 256))

    def body(y_ref, t_ref, dy_ref, loss_ref):
        @pl.when(pl.program_id(0) == 0)
        def _():
            loss_ref[...] = jnp.zeros_like(loss_ref)

        err = y_ref[...] - t_ref[...]
        dy_ref[...] = err * (1.0 / D)
        loss_ref[...] += jnp.sum(err * err) * (0.5 / D)

    row = pl.BlockSpec((tr, D), lambda i: (i, 0))
    return pl.pallas_call(
        body, grid=(S // tr,), in_specs=[row, row],
        out_specs=[row, pl.BlockSpec((1, LANES), lambda i: (0, 0))],
        out_shape=[jax.ShapeDtypeStruct((S, D), F32), jax.ShapeDtypeStruct((1, LANES), F32)],
        compiler_params=_params(("arbitrary",)), name="loss_head")(y, target)


def _row_tile(R, C):
    for tr in (1024, 512, 256, 128, 64, 32, 16, 8):
        if R % tr == 0 and tr * C * 4 <= (1 << 20):
            return tr
    return R


def _chip_sum(own, from_chips, name):
    R, C = own.shape
    tr = _row_tile(R, C)

    def body(own_ref, a_ref, b_ref, c_ref, o_ref):
        o_ref[...] = ((own_ref[...].astype(F32) + a_ref[...].astype(F32)) + b_ref[...].astype(F32)) + c_ref[...].astype(F32)

    blk = pl.BlockSpec((tr, C), lambda i: (i, 0))
    slab = lambda s: pl.BlockSpec((None, tr, C), lambda i: (s, i, 0))
    return pl.pallas_call(
        body, grid=(R // tr,), in_specs=[blk, slab(0), slab(1), slab(2)], out_specs=blk,
        out_shape=jax.ShapeDtypeStruct((R, C), F32),
        compiler_params=_params(("parallel",)), name=name)(own, from_chips, from_chips, from_chips)


def _adamw(w, g_mine, g_sibling, m, v, name):
    _, R, C = w.shape
    tr = _row_tile(R, C)
    c1 = 1.0 - ADAM_B1 ** ADAM_STEP
    c2 = 1.0 - ADAM_B2 ** ADAM_STEP

    def body(w_ref, ga_ref, gb_ref, m_ref, v_ref, g_ref, d_ref, mo_ref, vo_ref):
        g_t = ga_ref[...] + gb_ref[...]
        m_new = ADAM_B1 * m_ref[...] + (1.0 - ADAM_B1) * g_t
        v_new = ADAM_B2 * v_ref[...] + (1.0 - ADAM_B2) * (g_t * g_t)
        g_ref[...] = g_t
        d_ref[...] = -ADAM_LR * ((m_new / c1) / (jnp.sqrt(v_new / c2) + ADAM_EPS) + ADAM_WD * w_ref[...])
        mo_ref[...] = m_new
        vo_ref[...] = v_new

    blk = pl.BlockSpec((tr, C), lambda i: (i, 0))
    blk3 = pl.BlockSpec((None, tr, C), lambda i: (0, i, 0))
    return pl.pallas_call(
        body, grid=(R // tr,), in_specs=[blk3, blk, blk, blk3, blk3], out_specs=[blk3] * 4,
        out_shape=[jax.ShapeDtypeStruct((1, R, C), F32)] * 4,
        compiler_params=_params(("parallel",)), name=name)(w, g_mine, g_sibling, m, v)


D_MODEL = 1024
SSM_INNER = SSM_HEADS * HEAD_DIM
CONV_DIM = SSM_INNER + 2 * SSM_GROUPS * SSM_STATE
ATTN_WIDTH = ATTN_HEADS * HEAD_DIM
COL_Z = 0
COL_XBC = COL_Z + SSM_INNER
COL_Q = COL_XBC + CONV_DIM
COL_K = COL_Q + ATTN_WIDTH
COL_V = COL_K + ATTN_WIDTH
COL_DT = COL_V + ATTN_WIDTH
COL_F = COL_DT + SSM_HEADS
IN_COLS = COL_F + ATTN_HEADS
IN_COLS_PAD = -(-IN_COLS // LANES) * LANES
REF_COL_DT = COL_Q
SHARD_COLS = IN_COLS // N_CHIPS
_COL_RANGES = ((0, REF_COL_DT, 0), (REF_COL_DT + SSM_HEADS, COL_F, COL_Q), (REF_COL_DT, REF_COL_DT + SSM_HEADS, COL_DT),
               (COL_F, IN_COLS, COL_F))


def _w_in_from_shards(g):
    parts = []
    for lo, hi, _ in _COL_RANGES:
        while lo < hi:
            j = lo // SHARD_COLS
            end = min(hi, (j + 1) * SHARD_COLS)
            parts.append(g[j][:, lo - j * SHARD_COLS:end - j * SHARD_COLS])
            lo = end
    parts.append(jnp.zeros((g.shape[1], IN_COLS_PAD - IN_COLS), g.dtype))
    return jnp.concatenate(parts, axis=1)


def _w_in_to_shards(w):
    shards = []
    for j in range(N_CHIPS):
        parts = []
        for lo, hi, here in sorted(_COL_RANGES):
            a, b = max(lo, j * SHARD_COLS), min(hi, (j + 1) * SHARD_COLS)
            if a < b:
                parts.append(w[:, here + a - lo:here + b - lo])
        shards.append(jnp.concatenate(parts, axis=1))
    return jnp.stack(shards)


def _add_residual(acc, res):
    return (res + acc,)


def _relu2(acc):
    r = jnp.maximum(acc, 0.0)
    return acc, r * r


def _relu2_bwd(acc, a):
    return (acc * (2.0 * jnp.maximum(a, 0.0)),)


def _layer_fwd_bwd(x, mem, target, w_in, p, late_weights, send_late_grads, send_w_in_grad):
    S = x.shape[0]
    hd3 = lambda a: a.reshape(SSM_HEADS, 1, 1)

    h1 = _rmsnorm_fwd(x, p["g_mix"], "norm_mix")
    proj = _mm(h1, w_in, "nn", "in_proj")
    xbc = _conv_fwd(proj, COL_XBC, CONV_DIM, p["conv_w"], p["conv_b"])
    dt_hm = proj[:, COL_DT:COL_DT + SSM_HEADS].T[:, :, None]
    ssd_par = (hd3(p["dt_bias"]), hd3(p["a_log"]), hd3(p["d_skip"]), p["ssm_norm_w"])
    y, hs = _ssd_fwd(xbc, proj, dt_hm, *ssd_par)
    f_raw = proj[:, COL_F:COL_F + ATTN_HEADS]
    gq2 = jnp.tile(p["g_q"], (1, 2))
    gk2 = jnp.tile(p["g_k"], (1, 2))
    qs, kn, vb = _qk_prep_fwd(proj, gq2, gk2)
    cum, cq = _logf_cumsum_fwd(f_raw, p["f_bias"])
    ck = cum.T[:, None, :]
    o, o_fine, lse = _flash_fwd(qs, kn, vb, cq, ck)
    W = late_weights((o_fine, y))
    x1 = _mm(y, W["w_out"][:SSM_INNER], "nn", "out_proj_ssm", epilogue=_add_residual, extras=(x,))
    x1 = _mm(o, W["w_out"][SSM_INNER:], "nn", "out_proj_attn", epilogue=_add_residual, extras=(x1,))
    h2 = _rmsnorm_fwd(x1, p["g_xattn"], "norm_xattn")
    mem_n = _rmsnorm_fwd(mem, p["g_mem"], "norm_mem")
    xq = _mm(h2, W["xq_w"], "nn", "xq_proj")
    kv = _mm(mem_n, W["xkv_w"], "nn", "xkv_proj", b_chunks=N_CHIPS)
    xo = _xattn_fwd(xq, kv, p["xg_q"], p["xg_k"])
    x2 = _mm(xo, W["xo_w"], "nn", "xo_proj", epilogue=_add_residual, extras=(x1,))
    h3 = _rmsnorm_fwd(x2, p["g_mlp"], "norm_mlp")
    a, act = _mm(h3, W["w_up"], "nn", "mlp_up", out_dtypes=(F32, BF16), epilogue=_relu2, b_chunks=N_CHIPS)
    x3 = _mm(act, W["w_down"], "nn", "mlp_down", epilogue=_add_residual, extras=(x2,))
    dy, loss_row = _loss_head(x3, target)

    gW, gp = {}, {}
    da = _mm(dy, W["w_down"], "nt", "d_act", out_dtypes=(BF16,), epilogue=_relu2_bwd, extras=(a,))
    gW["w_down"] = _mm(act, dy, "tn", "g_w_down", out_dtypes=(BF16,))
    gW["w_up"] = _mm(h3, da, "tn", "g_w_up", out_dtypes=(BF16,), out_chunks=N_CHIPS)
    dh3 = _mm(da, W["w_up"], "nt", "d_h3", b_chunks=N_CHIPS)
    dx2, gp["g_mlp"] = _rmsnorm_bwd(x2, p["g_mlp"], dh3, dy, "norm_mlp_bwd")
    dxo = _mm(dx2, W["xo_w"], "nt", "d_xo", out_dtypes=(BF16,))
    gW["xo_w"] = _mm(xo, dx2, "tn", "g_xo_w", out_dtypes=(BF16,))
    dxq, dk_x, dv_x, gp["xg_q"], gp["xg_k"] = _xattn_bwd(xq, kv, p["xg_q"], p["xg_k"], dxo)
    dkv = jnp.concatenate([dk_x, dv_x], axis=-1)
    gW["xq_w"] = _mm(h2, dxq, "tn", "g_xq_w", out_dtypes=(BF16,))
    dh2 = _mm(dxq, W["xq_w"], "nt", "d_h2")
    gW["xkv_w"] = _mm(mem_n, dkv, "tn", "g_xkv_w", out_dtypes=(BF16,), out_chunks=N_CHIPS)
    dmem_n = _mm(dkv, W["xkv_w"], "nt", "d_mem_n", b_chunks=N_CHIPS)
    _, gp["g_mem"] = _rmsnorm_bwd(mem, p["g_mem"], dmem_n, None, "norm_mem_bwd")
    dx1, gp["g_xattn"] = _rmsnorm_bwd(x1, p["g_xattn"], dh2, dx2, "norm_xattn_bwd")
    dmixed = _mm(dx1, W["w_out"], "nt", "d_mixed")
    gW["w_out"] = jnp.concatenate([_mm(y, dx1, "tn", "g_w_out_ssm", out_dtypes=(BF16,)),
                                   _mm(o, dx1, "tn", "g_w_out_attn", out_dtypes=(BF16,))], axis=0)
    token = send_late_grads(gW)
    dqs, dkn, dproj, dck = _flash_bwd(qs, kn, vb, cq, ck + token[:1, :1], o_fine, dmixed, SSM_INNER, lse)
    dproj, dgq2 = _pair_norm_bwd(proj, COL_Q, gq2, ATTN_SCALE, dqs, dproj, "q_norm_bwd")
    dproj, dgk2 = _pair_norm_bwd(proj, COL_K, gk2, 1.0, dkn, dproj, "k_norm_bwd")
    gp["g_q"] = dgq2[:, :HEAD_DIM] + dgq2[:, HEAD_DIM:]
    gp["g_k"] = dgk2[:, :HEAD_DIM] + dgk2[:, HEAD_DIM:]
    df, gp["f_bias"] = _logf_cumsum_bwd(f_raw, p["f_bias"], dck[:, 0, :].T)
    dxs, dproj, dB, dC, ddt, ddtb, dalog, ddsk, gp["ssm_norm_w"] = _ssd_bwd(xbc, proj, dt_hm, *ssd_par, hs, dmixed, dproj)
    gp["dt_bias"] = ddtb.reshape(1, SSM_HEADS)
    gp["a_log"] = dalog.reshape(1, SSM_HEADS)
    gp["d_skip"] = ddsk.reshape(1, SSM_HEADS)
    dproj, dconv_w, gp["conv_b"] = _conv_bwd(proj, COL_XBC, CONV_DIM, p["conv_w"], p["conv_b"], (dxs, dB, dC), dproj)
    gp["conv_w"] = dconv_w[:CONV_WIDTH]
    tail = jnp.concatenate([ddt[:, :, 0].T, df, jnp.zeros((S, IN_COLS_PAD - IN_COLS), F32)], axis=-1).astype(BF16)
    dproj = lax.dynamic_update_slice(dproj, tail, (0, COL_DT))
    token = send_w_in_grad(_mm(h1, dproj, "tn", "g_w_in", out_dtypes=(BF16,)))
    dh1 = _mm(dproj, w_in, "nt", "d_h1")
    dx, gp["g_mix"] = _rmsnorm_bwd(x, p["g_mix"] + token[:1, :1], dh1, dx1, "norm_mix_bwd")
    return loss_row, dx, gp


_ANY = pl.BlockSpec(memory_space=pl.ANY)


def _place():
    x, y, c = lax.axis_index("x"), lax.axis_index("y"), lax.axis_index("c")
    chips = [(1 - x, y), (x, 1 - y), (1 - x, 1 - y)]
    return x, y, c, chips


def _chip_index(px, py):
    return 2 * px + py


def _all_gather_chips(split, whole):
    ns, nw = len(split), len(whole)
    n = ns + nw

    def body(*refs):
        ins, outs = refs[:n], refs[n:2 * n]
        send_ici, recv_ici, send_d2d, recv_d2d = refs[2 * n:]
        x, y, c, chips = _place()
        me = _chip_index(x, y)
        sib = (x, y, 1 - c)

        def ici(k, j, src, dst):
            return pltpu.make_async_remote_copy(src_ref=src, dst_ref=dst, send_sem=send_ici.at[3 * k + j],
                                                recv_sem=recv_ici.at[3 * k + j], device_id=(*chips[j], c),
                                                device_id_type=MESH)

        def d2d(k, j, piece):
            return pltpu.make_async_remote_copy(src_ref=piece, dst_ref=piece, send_sem=send_d2d.at[3 * k + j],
                                                recv_sem=recv_d2d.at[3 * k + j], device_id=sib, device_id_type=MESH)

        sends = []
        for k in range(n):
            for j in range(3):
                if k < ns:
                    sends.append(ici(k, j, ins[k].at[c], outs[k].at[me, c]))
                else:
                    sends.append(ici(k, j, ins[k], outs[k].at[me]))
                sends[-1].start()
        passed = []
        for k in range(n):
            for j in range(3):
                src_chip = _chip_index(*chips[j])
                if k < ns:
                    ici(k, j, ins[k].at[c], outs[k].at[src_chip, c]).wait_recv()
                    passed.append(d2d(k, j, outs[k].at[src_chip, c]))
                    passed[-1].start()
                else:
                    ici(k, j, ins[k], outs[k].at[src_chip]).wait_recv()
        for k in range(ns):
            for j in range(3):
                d2d(k, j, outs[k].at[_chip_index(*chips[j]), 1 - c]).wait_recv()
        for cp in sends + passed:
            cp.wait_send()

    arrs = list(split) + list(whole)
    return pl.pallas_call(
        body, in_specs=[_ANY] * n, out_specs=[_ANY] * n,
        out_shape=[jax.ShapeDtypeStruct((N_CHIPS,) + a.shape, a.dtype) for a in arrs],
        scratch_shapes=[pltpu.SemaphoreType.DMA((3 * n,)), pltpu.SemaphoreType.DMA((3 * n,)),
                        pltpu.SemaphoreType.DMA((3 * ns,)), pltpu.SemaphoreType.DMA((3 * ns,))],
        name="all_gather_chips")(*arrs)


def _sibling_swap(arrs, name):
    n = len(arrs)

    def body(*refs):
        ins, outs = refs[:n], refs[n:2 * n]
        send_sem, recv_sem = refs[2 * n:]
        x, y, c, _ = _place()
        copies = [pltpu.make_async_remote_copy(src_ref=ins[k], dst_ref=outs[k], send_sem=send_sem.at[k],
                                               recv_sem=recv_sem.at[k], device_id=(x, y, 1 - c), device_id_type=MESH)
                  for k in range(n)]
        for q in copies:
            q.start()
        for q in copies:
            q.wait()

    return pl.pallas_call(
        body, in_specs=[_ANY] * n, out_specs=[_ANY] * n,
        out_shape=[jax.ShapeDtypeStruct(a.shape, a.dtype) for a in arrs],
        scratch_shapes=[pltpu.SemaphoreType.DMA((n,)), pltpu.SemaphoreType.DMA((n,))],
        name=name)(*arrs)


_HBM = pl.BlockSpec(memory_space=pltpu.HBM)
_SEM = pl.BlockSpec(memory_space=pltpu.SEMAPHORE)
_SPLIT_EFFECT = pltpu.SideEffectType.DATAFLOW_SIDE_EFFECTING


class _Split(NamedTuple):
    send_sems: jax.Array
    recv_sems: jax.Array
    sources: tuple
    lands: tuple
    token: jax.Array


def _split_copies(kind, srcs, lands, send_sems, recv_sems):
    x, y, c, chips = _place()
    me = _chip_index(x, y)
    copies = []
    for k in range(len(srcs)):
        for j in range(3):
            if kind == "gather":
                src, dst = srcs[k], lands[k].at[me]
            else:
                src, dst = srcs[k].at[_chip_index(*chips[j])], lands[k].at[j]
            copies.append(pltpu.make_async_remote_copy(
                src_ref=src, dst_ref=dst, send_sem=send_sems.at[3 * k + j], recv_sem=recv_sems.at[3 * k + j],
                device_id=(*chips[j], c), device_id_type=MESH))
    return copies


def _split_start(name, sources, kind, after):
    n = len(sources)
    if kind == "gather":
        lands = [lax.empty((N_CHIPS,) + s.shape, s.dtype) for s in sources]
    else:
        lands = [lax.empty((3,) + s.shape[1:], s.dtype) for s in sources]
    deps = [] if after is None else [after]

    def body(*refs):
        srcs, lnds = refs[:n], refs[n:2 * n]
        send_sems, recv_sems = refs[2 * n + len(deps)], refs[2 * n + len(deps) + 1]
        for cp in _split_copies(kind, srcs, lnds, send_sems, recv_sems):
            cp.start()
        refs[-1][...] = jnp.zeros_like(refs[-1])

    hbm = lambda a: pltpu.with_memory_space_constraint(a, pltpu.HBM)
    outs = pl.pallas_call(
        body, name=name,
        in_specs=[_HBM] * (2 * n) + [_ANY] * len(deps),
        out_specs=[_SEM, _SEM] + [_HBM] * (2 * n) + [pl.BlockSpec(memory_space=pltpu.VMEM)],
        out_shape=[pltpu.SemaphoreType.DMA((3 * n,)), pltpu.SemaphoreType.DMA((3 * n,))]
        + [pltpu.HBM(a.shape, a.dtype) for a in list(sources) + lands] + [jax.ShapeDtypeStruct((8, LANES), F32)],
        input_output_aliases={k: 2 + k for k in range(2 * n)},
        compiler_params=pltpu.CompilerParams(has_side_effects=_SPLIT_EFFECT),
    )(*[hbm(s) for s in sources], *[hbm(l) for l in lands], *deps)
    return _Split(outs[0], outs[1], tuple(outs[2:2 + n]), tuple(outs[2 + n:2 + 2 * n]), outs[-1])


def _split_wait(name, h, kind, after):
    n = len(h.sources)

    def body(*refs):
        srcs, lnds = refs[:n], refs[n:2 * n]
        for cp in _split_copies(kind, srcs, lnds, refs[2 * n], refs[2 * n + 1]):
            cp.wait_send()
            cp.wait_recv()

    outs = pl.pallas_call(
        body, name=name,
        in_specs=[_HBM] * (2 * n) + [_SEM, _SEM] + [_ANY] * len(after),
        out_specs=[_HBM] * (2 * n),
        out_shape=[pltpu.HBM(a.shape, a.dtype) for a in h.sources + h.lands],
        input_output_aliases={k: k for k in range(2 * n)},
        compiler_params=pltpu.CompilerParams(has_side_effects=_SPLIT_EFFECT),
    )(*h.sources, *h.lands, h.send_sems, h.recv_sems, *after)
    return outs[:n], outs[n:]


def _all_reduce_small(vec, after):
    R, C = vec.shape

    def body(v_ref, after_ref, o_ref, buf, send_sem, recv_sem):
        x, y, c = lax.axis_index("x"), lax.axis_index("y"), lax.axis_index("c")
        me = 4 * x + 2 * y + c
        buf[me] = v_ref[...]
        copies = []
        for r in range(1, N_DEV):
            fx, fy, fc = (r >> 2) & 1, (r >> 1) & 1, r & 1
            peer = (x ^ fx, y ^ fy, c ^ fc)
            copies.append(pltpu.make_async_remote_copy(src_ref=v_ref, dst_ref=buf.at[me], send_sem=send_sem.at[r - 1],
                                                       recv_sem=recv_sem.at[r - 1], device_id=peer, device_id_type=MESH))
        for q in copies:
            q.start()
        for r in range(1, N_DEV):
            fx, fy, fc = (r >> 2) & 1, (r >> 1) & 1, r & 1
            src = 4 * (x ^ fx) + 2 * (y ^ fy) + (c ^ fc)
            pltpu.make_async_remote_copy(src_ref=v_ref, dst_ref=buf.at[src], send_sem=send_sem.at[r - 1],
                                         recv_sem=recv_sem.at[r - 1], device_id=(x, y, c), device_id_type=MESH).wait_recv()
        acc = buf[0]
        for d in range(1, N_DEV):
            acc = acc + buf[d]
        o_ref[...] = acc
        for q in copies:
            q.wait_send()

    vm = pl.BlockSpec(memory_space=pltpu.VMEM)
    return pl.pallas_call(
        body, in_specs=[vm, _ANY], out_specs=vm, out_shape=jax.ShapeDtypeStruct((R, C), F32),
        scratch_shapes=[pltpu.VMEM((N_DEV, R, C), F32), pltpu.SemaphoreType.DMA((N_DEV - 1,)),
                        pltpu.SemaphoreType.DMA((N_DEV - 1,))],
        name="all_reduce_small")(vec, after)


_INPUTS = ["x", "mem", "g_mix", "w_in", "conv_w", "conv_b", "dt_bias", "a_log", "d_skip", "ssm_norm_w", "g_q", "g_k",
           "f_bias", "w_out", "g_xattn", "g_mem", "xq_w", "xkv_w", "xg_q", "xg_k", "xo_w", "g_mlp", "w_up", "w_down"]
_WEIGHTS = _INPUTS[2:]
_BIG = ["w_in", "w_out", "xq_w", "xkv_w", "xo_w", "w_up", "w_down"]
_LATE = _BIG[1:]
_COL_SHARDED = ["w_in", "xkv_w", "w_up"]
_SMALL = [n for n in _WEIGHTS if n not in _BIG]


def _pack_rows(arrs, width):
    starts, r = [], 0
    for a in arrs:
        starts.append(r)
        r += a.shape[0]
    out = jnp.concatenate([jnp.pad(a, ((0, 0), (0, width - a.shape[1]))) for a in arrs], axis=0)
    return jnp.pad(out, ((0, -r % 8), (0, 0))), starts


def _adamw_small(summed, starts, ws, ms, vs, conv_w_index):
    n = len(ws)
    c1 = 1.0 - ADAM_B1 ** ADAM_STEP
    c2 = 1.0 - ADAM_B2 ** ADAM_STEP

    def body(s_ref, *refs):
        w_refs, m_refs, v_refs = refs[:n], refs[n:2 * n], refs[2 * n:3 * n]
        outs = refs[3 * n:]
        chip = _chip_index(lax.axis_index("x"), lax.axis_index("y"))
        for k in range(n):
            rows, cols = w_refs[k].shape
            if k == conv_w_index:
                g = s_ref[starts[k]:starts[k] + rows, pl.ds(pl.multiple_of(chip * cols, LANES), cols)]
            else:
                g = s_ref[starts[k]:starts[k] + rows, 0:cols]
            m_new = ADAM_B1 * m_refs[k][...] + (1.0 - ADAM_B1) * g
            v_new = ADAM_B2 * v_refs[k][...] + (1.0 - ADAM_B2) * (g * g)
            outs[4 * k][...] = g
            outs[4 * k + 1][...] = -ADAM_LR * ((m_new / c1) / (jnp.sqrt(v_new / c2) + ADAM_EPS) + ADAM_WD * w_refs[k][...])
            outs[4 * k + 2][...] = m_new
            outs[4 * k + 3][...] = v_new

    vm = pl.BlockSpec(memory_space=pltpu.VMEM)
    outs = pl.pallas_call(
        body, in_specs=[vm] * (1 + 3 * n), out_specs=[vm] * (4 * n),
        out_shape=[jax.ShapeDtypeStruct(a.shape, F32) for a in ws for _ in range(4)],
        name="adamw_small")(summed, *ws, *ms, *vs)
    return [outs[4 * k:4 * k + 4] for k in range(n)]


def kernel(x, mem, g_mix, w_in, conv_w, conv_b, dt_bias, a_log, d_skip, ssm_norm_w, g_q, g_k, f_bias, w_out, g_xattn, g_mem, xq_w, xkv_w, xg_q, xg_k, xo_w, g_mlp, w_up, w_down, loss_target, m_g_mix, m_w_in, m_conv_w, m_conv_b, m_dt_bias, m_a_log, m_d_skip, m_ssm_norm_w, m_g_q, m_g_k, m_f_bias, m_w_out, m_g_xattn, m_g_mem, m_xq_w, m_xkv_w, m_xg_q, m_xg_k, m_xo_w, m_g_mlp, m_w_up, m_w_down, v_g_mix, v_w_in, v_conv_w, v_conv_b, v_dt_bias, v_a_log, v_d_skip, v_ssm_norm_w, v_g_q, v_g_k, v_f_bias, v_w_out, v_g_xattn, v_g_mem, v_xq_w, v_xkv_w, v_xg_q, v_xg_k, v_xo_w, v_g_mlp, v_w_up, v_w_down):
    args = (x, mem, g_mix, w_in, conv_w, conv_b, dt_bias, a_log, d_skip, ssm_norm_w, g_q, g_k, f_bias, w_out, g_xattn,
            g_mem, xq_w, xkv_w, xg_q, xg_k, xo_w, g_mlp, w_up, w_down)
    w = dict(zip(_INPUTS, args))
    mom1 = dict(zip(_WEIGHTS, (m_g_mix, m_w_in, m_conv_w, m_conv_b, m_dt_bias, m_a_log, m_d_skip, m_ssm_norm_w, m_g_q,
                               m_g_k, m_f_bias, m_w_out, m_g_xattn, m_g_mem, m_xq_w, m_xkv_w, m_xg_q, m_xg_k, m_xo_w,
                               m_g_mlp, m_w_up, m_w_down)))
    mom2 = dict(zip(_WEIGHTS, (v_g_mix, v_w_in, v_conv_w, v_conv_b, v_dt_bias, v_a_log, v_d_skip, v_ssm_norm_w, v_g_q,
                               v_g_k, v_f_bias, v_w_out, v_g_xattn, v_g_mem, v_xq_w, v_xkv_w, v_xg_q, v_xg_k, v_xo_w,
                               v_g_mlp, v_w_up, v_w_down)))
    chip = _chip_index(lax.axis_index("x"), lax.axis_index("y"))

    shard_bf = {n: w[n][0].astype(BF16) for n in _BIG}

    def layout_for_compute(n, g):
        if n == "w_in":
            return _w_in_from_shards(g)
        return g if n in _COL_SHARDED else g.reshape(N_CHIPS * g.shape[1], g.shape[2])

    def layout_for_reduction(n, g):
        if n == "w_in":
            return _w_in_to_shards(g)
        return g if n in _COL_SHARDED else g.reshape(N_CHIPS, g.shape[0] // N_CHIPS, g.shape[1])

    halves_in = shard_bf["w_in"].reshape(2, shard_bf["w_in"].shape[0] // 2, -1)
    g_in, g_conv = _all_gather_chips([halves_in], [w["conv_w"][0]])
    g_in = lax.dynamic_update_index_in_dim(g_in, halves_in, chip, axis=0)
    g_conv = lax.dynamic_update_index_in_dim(g_conv, w["conv_w"][0], chip, axis=0)
    w_in_full = layout_for_compute("w_in", g_in.reshape(N_CHIPS, -1, g_in.shape[-1]))
    p = {n: w[n] for n in _SMALL}
    p["conv_w"] = g_conv.transpose(1, 0, 2).reshape(CONV_WIDTH, CONV_DIM)
    gather = _split_start("gather_late", [shard_bf[n] for n in _LATE], "gather", after=g_in)
    p["g_mix"] = p["g_mix"] + gather.token[:1, :1]

    def late_weights(after):
        srcs, lands = _split_wait("gather_late_wait", gather, "gather", after)
        lands = [lax.dynamic_update_index_in_dim(l, s, chip, axis=0) for l, s in zip(lands, srcs)]
        return {n: layout_for_compute(n, l) for n, l in zip(_LATE, lands)}

    scatter = {}

    def send_late_grads(grads):
        scatter["late"] = _split_start("scatter_late", [layout_for_reduction(n, grads[n]) for n in _LATE], "scatter",
                                       after=None)
        return scatter["late"].token

    def send_w_in_grad(g):
        scatter["w_in"] = _split_start("scatter_w_in", [layout_for_reduction("w_in", g)], "scatter", after=None)
        return scatter["w_in"].token

    loss_row, dx, gp = _layer_fwd_bwd(x[0], mem[0], loss_target[0], w_in_full, p, late_weights, send_late_grads,
                                      send_w_in_grad)

    grad, delta, new_m, new_v = {}, {}, {}, {}

    def finish(names, sources, from_chips, tag):
        mine = [_chip_sum(lax.dynamic_index_in_dim(s, chip, axis=0, keepdims=False), fc, "rs_chip_sum_" + n)
                for n, s, fc in zip(names, sources, from_chips)]
        for n, a, b in zip(names, mine, _sibling_swap(mine, "rs_sibling_swap_" + tag)):
            grad[n], delta[n], new_m[n], new_v[n] = _adamw(w[n], a, b, mom1[n], mom2[n], "adamw_" + n)

    finish(_LATE, *_split_wait("scatter_late_wait", scatter["late"], "scatter", (dx,)), "late")

    sources_in, from_chips_in = _split_wait("scatter_w_in_wait", scatter["w_in"], "scatter",
                                            tuple(new_v[n] for n in _LATE))

    packed, starts = _pack_rows([gp[n] for n in _SMALL] + [loss_row], CONV_DIM)
    summed = _all_reduce_small(packed, from_chips_in[0])
    loss = summed[starts[-1], 0]
    finish(["w_in"], sources_in, from_chips_in, "w_in")

    as_rows = lambda a: a.reshape(-1, a.shape[-1])
    results = _adamw_small(summed, starts, [as_rows(w[n]) for n in _SMALL], [as_rows(mom1[n]) for n in _SMALL],
                           [as_rows(mom2[n]) for n in _SMALL], _SMALL.index("conv_w"))
    for n, res in zip(_SMALL, results):
        grad[n], delta[n], new_m[n], new_v[n] = (a.reshape(w[n].shape) for a in res)

    return (loss, dx[None], *[grad[n] for n in _WEIGHTS], *[delta[n] for n in _WEIGHTS],
            *[new_m[n] for n in _WEIGHTS], *[new_v[n] for n in _WEIGHTS])
```

```python
from typing import NamedTuple

import jax
import jax.numpy as jnp
from jax import lax
from jax.experimental import pallas as pl
from jax.experimental.pallas import tpu as pltpu

F32 = jnp.float32
BF16 = jnp.bfloat16
HI = lax.Precision.HIGHEST
MESH = pl.DeviceIdType.MESH

EPS = 1e-5
CHUNK = 128
SSM_HEADS = 16
SSM_GROUPS = 2
HEADS_PER_GROUP = SSM_HEADS // SSM_GROUPS
HEAD_DIM = 64
SSM_STATE = 128
ATTN_HEADS = 16
XATTN_HEADS = 4
XATTN_DIM = 256
CONV_WIDTH = 4
CONV_COLS = 256
N_CHIPS = 4
N_DEV = 8
LANES = 128
VMEM_LIMIT = 56 * 1024 * 1024

ADAM_LR = 0.001
ADAM_B1 = 0.9
ADAM_B2 = 0.999
ADAM_EPS = 1e-08
ADAM_WD = 0.01
ADAM_STEP = 10


def _params(sem):
    return pltpu.CompilerParams(dimension_semantics=sem, vmem_limit_bytes=VMEM_LIMIT)


def _pick(n, cands):
    for c in cands:
        if n % c == 0:
            return c
    return n


def _mm(a, b, mode, name, out_dtypes=(F32,), epilogue=None, extras=(), b_chunks=1, out_chunks=1,
        tm=None, tn=None, tk=None):
    if mode == "nn":
        M, K = a.shape
        N = b.shape[-1] * b_chunks
    elif mode == "nt":
        M, K = a.shape
        N = b.shape[-2]
        assert b.shape[-1] * b_chunks == K
    else:
        K, M = a.shape
        N = b.shape[-1] * b_chunks
    tm = tm or _pick(M, (2048, 1024, 512, 256, 128))
    tn = tn or _pick(N // max(b_chunks if mode != "nt" else 1, out_chunks), (512, 640, 384, 256, 128))
    if tk is None:
        kmax = b.shape[-1] if mode == "nt" else K
        tk = kmax if kmax <= 2048 else _pick(kmax, (2048, 1152, 1024, 512))
    nk = K // tk
    assert M % tm == 0 and N % tn == 0 and K % tk == 0
    grid = (M // tm, N // tn, nk)

    if mode == "tn":
        a_spec = pl.BlockSpec((tk, tm), lambda i, j, k: (k, i))
    else:
        a_spec = pl.BlockSpec((tm, tk), lambda i, j, k: (i, k))

    def b_index(t_row, t_last, tile_last):
        if b_chunks == 1:
            return (t_row, t_last)
        q = (b.shape[-1]) // tile_last
        return (t_last // q, t_row, t_last % q)

    if mode == "nn" or mode == "tn":
        bshape = (tk, tn)
        bmap = lambda i, j, k: b_index(k, j, tn)
    else:
        bshape = (tn, tk)
        bmap = lambda i, j, k: b_index(j, k, tk)
    if b_chunks > 1:
        bshape = (None,) + bshape
    b_spec = pl.BlockSpec(bshape, bmap)

    if out_chunks == 1:
        o_spec = pl.BlockSpec((tm, tn), lambda i, j, k: (i, j))
        o_shape = (M, N)
    else:
        qo = (N // out_chunks) // tn
        o_spec = pl.BlockSpec((None, tm, tn), lambda i, j, k: (j // qo, i, j % qo))
        o_shape = (out_chunks, M, N // out_chunks)
    e_spec = pl.BlockSpec((tm, tn), lambda i, j, k: (i, j))

    dims = {"nn": (((1,), (0,)), ((), ())), "nt": (((1,), (1,)), ((), ())), "tn": (((0,), (0,)), ((), ()))}[mode]
    n_ex = len(extras)
    n_out = len(out_dtypes)

    def body(*refs):
        a_ref, b_ref = refs[0], refs[1]
        ex_refs = refs[2:2 + n_ex]
        o_refs = refs[2 + n_ex:2 + n_ex + n_out]

        def finish(acc):
            outs = epilogue(acc, *[r[...] for r in ex_refs]) if epilogue is not None else (acc,)
            for r, o in zip(o_refs, outs):
                r[...] = o.astype(r.dtype)

        part = lax.dot_general(a_ref[...].astype(BF16), b_ref[...].astype(BF16), dims,
                               preferred_element_type=F32)
        if nk == 1:
            finish(part)
        else:
            acc_ref = refs[-1]
            k = pl.program_id(2)

            @pl.when(k == 0)
            def _():
                acc_ref[...] = part

            @pl.when(k > 0)
            def _():
                acc_ref[...] += part

            @pl.when(k == nk - 1)
            def _():
                finish(acc_ref[...])

    outs = pl.pallas_call(
        body,
        grid=grid,
        in_specs=[a_spec, b_spec] + [e_spec] * n_ex,
        out_specs=[o_spec] * n_out,
        out_shape=[jax.ShapeDtypeStruct(o_shape, d) for d in out_dtypes],
        scratch_shapes=[pltpu.VMEM((tm, tn), F32)] if nk > 1 else [],
        compiler_params=_params(("parallel", "parallel", "arbitrary")),
        name=name,
    )(a, b, *extras)
    return outs[0] if n_out == 1 else outs


def _rms(x, g):
    r = lax.rsqrt(jnp.mean(x * x, axis=-1, keepdims=True) + EPS)
    return x * r * g


def _rmsnorm_fwd(x, g, name):
    R, D = x.shape
    tr = _pick(R, (512, 256))

    def body(x_ref, g_ref, o_ref):
        o_ref[...] = _rms(x_ref[...], g_ref[...]).astype(o_ref.dtype)

    return pl.pallas_call(
        body, grid=(R // tr,),
        in_specs=[pl.BlockSpec((tr, D), lambda i: (i, 0)), pl.BlockSpec((1, D), lambda i: (0, 0))],
        out_specs=pl.BlockSpec((tr, D), lambda i: (i, 0)),
        out_shape=jax.ShapeDtypeStruct((R, D), BF16),
        compiler_params=_params(("parallel",)), name=name)(x, g)


def _rmsnorm_bwd(x, g, dh, dres, name):
    R, D = x.shape
    tr = _pick(R, (256,))
    has_res = dres is not None

    def body(*refs):
        if has_res:
            x_ref, g_ref, dh_ref, dres_ref, dx_ref, dg_ref = refs
        else:
            x_ref, g_ref, dh_ref, dx_ref, dg_ref = refs
        _, vjp = jax.vjp(_rms, x_ref[...], g_ref[...])
        dx, dg = vjp(dh_ref[...])
        if has_res:
            dx = dx + dres_ref[...]
        dx_ref[...] = dx

        @pl.when(pl.program_id(0) == 0)
        def _():
            dg_ref[...] = jnp.zeros_like(dg_ref)

        dg_ref[...] += dg

    row = pl.BlockSpec((tr, D), lambda i: (i, 0))
    vec = pl.BlockSpec((1, D), lambda i: (0, 0))
    ins = [x, g, dh] + ([dres] if has_res else [])
    return pl.pallas_call(
        body, grid=(R // tr,),
        in_specs=[row, vec, row] + ([row] if has_res else []),
        out_specs=[row, vec],
        out_shape=[jax.ShapeDtypeStruct((R, D), F32), jax.ShapeDtypeStruct((1, D), F32)],
        compiler_params=_params(("arbitrary",)), name=name)(*ins)


def _shift_down(u, k):
    if k == 0:
        return u
    rows = lax.broadcasted_iota(jnp.int32, u.shape, 0)
    return jnp.where(rows >= k, pltpu.roll(u, k, axis=0), 0.0)


def _shift_up(u, k):
    if k == 0:
        return u
    n = u.shape[0]
    rows = lax.broadcasted_iota(jnp.int32, u.shape, 0)
    return jnp.where(rows < n - k, pltpu.roll(u, n - k, axis=0), 0.0)


def _conv_pre(u, w, b):
    pre = b
    for j in range(CONV_WIDTH):
        pre = pre + w[j:j + 1, :] * _shift_down(u, CONV_WIDTH - 1 - j)
    return pre


def _conv_fwd(proj, col0, ncols, conv_w, conv_b):
    S = proj.shape[0]
    cb0 = col0 // CONV_COLS

    def body(u_ref, w_ref, b_ref, o_ref):
        pre = _conv_pre(u_ref[...], w_ref[...], b_ref[...])
        o_ref[...] = pre * jax.nn.sigmoid(pre)

    return pl.pallas_call(
        body, grid=(ncols // CONV_COLS,),
        in_specs=[pl.BlockSpec((S, CONV_COLS), lambda j: (0, j + cb0)),
                  pl.BlockSpec((CONV_WIDTH, CONV_COLS), lambda j: (0, j)),
                  pl.BlockSpec((1, CONV_COLS), lambda j: (0, j))],
        out_specs=pl.BlockSpec((S, CONV_COLS), lambda j: (0, j)),
        out_shape=jax.ShapeDtypeStruct((S, ncols), F32),
        compiler_params=_params(("parallel",)), name="conv_fwd")(proj, conv_w, conv_b)


def _conv_bwd(proj, col0, ncols, conv_w, conv_b, douts, dproj):
    S = proj.shape[0]
    cb0 = col0 // CONV_COLS
    starts = [0]
    for d in douts:
        starts.append(starts[-1] + d.shape[1] // CONV_COLS)
    assert starts[-1] == ncols // CONV_COLS
    nd = len(douts)

    def body(u_ref, w_ref, b_ref, *rest):
        d_refs, (du_ref, dw_ref, db_ref) = rest[:nd], rest[nd + 1:]
        j = pl.program_id(0)
        dout = d_refs[-1][...]
        for i in range(nd - 2, -1, -1):
            dout = jnp.where(j < starts[i + 1], d_refs[i][...], dout)
        u = u_ref[...]
        w = w_ref[...]
        pre = _conv_pre(u, w, b_ref[...])
        s = jax.nn.sigmoid(pre)
        dpre = dout * (s * (1.0 + pre * (1.0 - s)))
        du = jnp.zeros_like(u)
        rows = []
        for j in range(CONV_WIDTH):
            k = CONV_WIDTH - 1 - j
            du = du + w[j:j + 1, :] * _shift_up(dpre, k)
            rows.append(jnp.sum(dpre * _shift_down(u, k), axis=0, keepdims=True))
        du_ref[...] = du.astype(du_ref.dtype)
        rows.append(jnp.zeros((8 - CONV_WIDTH, CONV_COLS), F32))
        dw_ref[...] = jnp.concatenate(rows, axis=0)
        db_ref[...] = jnp.sum(dpre, axis=0, keepdims=True)

    return pl.pallas_call(
        body, grid=(ncols // CONV_COLS,),
        in_specs=[pl.BlockSpec((S, CONV_COLS), lambda j: (0, j + cb0)),
                  pl.BlockSpec((CONV_WIDTH, CONV_COLS), lambda j: (0, j)),
                  pl.BlockSpec((1, CONV_COLS), lambda j: (0, j))]
        + [pl.BlockSpec((S, CONV_COLS), lambda j, lo=starts[i], hi=starts[i + 1]: (0, jnp.clip(j - lo, 0, hi - lo - 1)))
           for i in range(nd)] + [_ANY],
        out_specs=[pl.BlockSpec((S, CONV_COLS), lambda j: (0, j + cb0)),
                   pl.BlockSpec((8, CONV_COLS), lambda j: (0, j)),
                   pl.BlockSpec((1, CONV_COLS), lambda j: (0, j))],
        out_shape=[jax.ShapeDtypeStruct(dproj.shape, dproj.dtype),
                   jax.ShapeDtypeStruct((8, ncols), F32),
                   jax.ShapeDtypeStruct((1, ncols), F32)],
        input_output_aliases={3 + nd: 0},
        compiler_params=_params(("parallel",)), name="conv_bwd")(proj, conv_w, conv_b, *douts, dproj)


def _softplus(x):
    return jnp.maximum(x, 0.0) + jnp.log1p(jnp.exp(-jnp.abs(x)))


def _dot32(a, b, dims=(((1,), (0,)), ((), ()))):
    return lax.dot_general(a, b, dims, precision=HI, preferred_element_type=F32)


def _dotd(a, b, dims=(((1,), (0,)), ((), ()))):
    return lax.dot_general(a, b, dims, preferred_element_type=F32)


PAIRS_PER_GROUP = HEADS_PER_GROUP // 2


def _ssd_chunk(xs, Bm, Cm, z, dtr, dtb, alog, dsk, nw, h):
    L = Bm.shape[0]
    ri = lax.broadcasted_iota(jnp.int32, (L, L), 0)
    ci = lax.broadcasted_iota(jnp.int32, (L, L), 1)
    causal = ri >= ci
    tril = causal.astype(F32)
    first = _first_head(L)
    first1 = _first_head(1)
    CB = _dotd(Cm, Bm, _NT)
    gated, hnew = [], []
    ssq = jnp.zeros((L, 1), F32)
    for pp in range(len(xs)):
        dts, cums, tots, decay = [], [], [], []
        for a in range(2):
            r = 2 * pp + a
            dt = _softplus(dtr[r] + dtb[r])
            dA = dt * (-jnp.exp(alog[r]))
            acs = _dot32(tril, dA)
            cc = jnp.broadcast_to(acs, (L, L))
            decay.append(CB * jnp.exp(jnp.where(causal, cc - cc.T, -1e30)))
            dts.append(dt)
            cums.append(acs)
            tots.append(jnp.sum(dA, axis=0, keepdims=True))
        dt2 = jnp.where(first, dts[0], dts[1])
        acs2 = jnp.where(first, cums[0], cums[1])
        tot2 = jnp.where(first1, tots[0], tots[1])
        dsk2 = jnp.where(first1, dsk[2 * pp], dsk[2 * pp + 1])
        X = xs[pp] * dt2
        y = (jnp.where(first, _dotd(decay[0], X), _dotd(decay[1], X)) + jnp.exp(acs2) * _dotd(Cm, h[pp])
             + dsk2 * xs[pp])
        hnew.append(jnp.exp(tot2) * h[pp] + _dotd(Bm, X * jnp.exp(tot2 - acs2), _TN))
        g = y * (z[pp] * jax.nn.sigmoid(z[pp]))
        ssq = ssq + jnp.sum(g * g, axis=-1, keepdims=True)
        gated.append(g)
    rs = lax.rsqrt(ssq / (len(xs) * LANES) + EPS)
    return [g * rs * nw[pp] for pp, g in enumerate(gated)], hnew


def _ssd_args(xs_ref, b_ref, c_ref, z_ref, dt_ref, dtb_ref, al_ref, dsk_ref, nw_ref, h_ref):
    pairs = range(PAIRS_PER_GROUP)
    heads = range(HEADS_PER_GROUP)
    lanes = lambda ref, pp: ref[:, pp * LANES:(pp + 1) * LANES]
    return ([lanes(xs_ref, pp) for pp in pairs], b_ref[...], c_ref[...], [lanes(z_ref, pp) for pp in pairs],
            [dt_ref[r] for r in heads], [dtb_ref[r] for r in heads], [al_ref[r] for r in heads],
            [dsk_ref[r] for r in heads], [lanes(nw_ref, pp) for pp in pairs], [h_ref[pp] for pp in pairs])


def _ssd_specs(rev):
    H, N, L = HEADS_PER_GROUP, SSM_STATE, CHUNK
    gw = H * HEAD_DIM
    return dict(
        cols=lambda col0: pl.BlockSpec((L, gw), lambda g, c: (rev(c), col0 // gw + g)),
        bc=lambda first_block: pl.BlockSpec((L, N), lambda g, c: (rev(c), first_block + g)),
        dt=pl.BlockSpec((H, L, 1), lambda g, c: (g, rev(c), 0)),
        scal=pl.BlockSpec((H, 1, 1), lambda g, c: (g, 0, 0)),
        nw=pl.BlockSpec((1, gw), lambda g, c: (0, g)),
        hs=pl.BlockSpec((None, PAIRS_PER_GROUP, N, LANES), lambda g, c: (rev(c), g, 0, 0)),
        b_block=SSM_INNER // N,
    )


def _ssd_fwd(xbc, proj, dt_hm, dtb, alog, dsk, nw):
    S = xbc.shape[0]
    N, L = SSM_STATE, CHUNK
    nc = S // L
    sp = _ssd_specs(lambda c: c)

    def body(xs_ref, b_ref, c_ref, z_ref, dt_ref, dtb_ref, al_ref, dsk_ref, nw_ref, y_ref, hs_ref, h_ref):
        @pl.when(pl.program_id(1) == 0)
        def _():
            h_ref[...] = jnp.zeros_like(h_ref)

        hs_ref[...] = h_ref[...]
        out, hnew = _ssd_chunk(*_ssd_args(xs_ref, b_ref, c_ref, z_ref, dt_ref, dtb_ref, al_ref, dsk_ref, nw_ref, h_ref))
        for pp in range(PAIRS_PER_GROUP):
            y_ref[:, pp * LANES:(pp + 1) * LANES] = out[pp].astype(y_ref.dtype)
            h_ref[pp] = hnew[pp]

    return pl.pallas_call(
        body, grid=(SSM_GROUPS, nc),
        in_specs=[sp["cols"](0), sp["bc"](sp["b_block"]), sp["bc"](sp["b_block"] + SSM_GROUPS), sp["cols"](COL_Z),
                  sp["dt"], sp["scal"], sp["scal"], sp["scal"], sp["nw"]],
        out_specs=[sp["cols"](0), sp["hs"]],
        out_shape=[jax.ShapeDtypeStruct((S, SSM_INNER), BF16),
                   jax.ShapeDtypeStruct((nc, SSM_HEADS // 2, N, LANES), F32)],
        scratch_shapes=[pltpu.VMEM((PAIRS_PER_GROUP, N, LANES), F32)],
        compiler_params=_params(("parallel", "arbitrary")), name="ssd_fwd",
    )(xbc, xbc, xbc, proj, dt_hm, dtb, alog, dsk, nw)


def _ssd_bwd(xbc, proj, dt_hm, dtb, alog, dsk, nw, hs, dmixed, dproj):
    S = xbc.shape[0]
    N, L = SSM_STATE, CHUNK
    nc = S // L
    sp = _ssd_specs(lambda c: nc - 1 - c)

    def body(xs_ref, b_ref, c_ref, z_ref, dt_ref, dtb_ref, al_ref, dsk_ref, nw_ref, hs_ref, dy_ref, buf_ref,
             dxs_ref, dz_ref, db_ref, dc_ref, ddt_ref, ddtb_ref, dal_ref, ddsk_ref, dnw_ref, dh_ref):
        @pl.when(pl.program_id(1) == 0)
        def _():
            dh_ref[...] = jnp.zeros_like(dh_ref)
            ddtb_ref[...] = jnp.zeros_like(ddtb_ref)
            dal_ref[...] = jnp.zeros_like(dal_ref)
            ddsk_ref[...] = jnp.zeros_like(ddsk_ref)
            dnw_ref[...] = jnp.zeros_like(dnw_ref)

        pairs = range(PAIRS_PER_GROUP)
        lanes = lambda pp: slice(pp * LANES, (pp + 1) * LANES)
        _, vjp = jax.vjp(_ssd_chunk, *_ssd_args(xs_ref, b_ref, c_ref, z_ref, dt_ref, dtb_ref, al_ref, dsk_ref, nw_ref,
                                                hs_ref))
        dxs, dB, dC, dz, ddt, ddtb, dal, ddsk, dnw, dh = vjp(([dy_ref[:, lanes(pp)] for pp in pairs],
                                                              [dh_ref[pp] for pp in pairs]))
        db_ref[...] = dB
        dc_ref[...] = dC
        for pp in pairs:
            dxs_ref[:, lanes(pp)] = dxs[pp]
            dz_ref[:, lanes(pp)] = dz[pp].astype(dz_ref.dtype)
            dnw_ref[:, lanes(pp)] += dnw[pp]
            dh_ref[pp] = dh[pp]
        for r in range(HEADS_PER_GROUP):
            ddt_ref[r] = ddt[r]
            ddtb_ref[r] += ddtb[r]
            dal_ref[r] += dal[r]
            ddsk_ref[r] += ddsk[r]

    bc_out = pl.BlockSpec((L, N), lambda g, c: (nc - 1 - c, g))
    return pl.pallas_call(
        body, grid=(SSM_GROUPS, nc),
        in_specs=[sp["cols"](0), sp["bc"](sp["b_block"]), sp["bc"](sp["b_block"] + SSM_GROUPS), sp["cols"](COL_Z),
                  sp["dt"], sp["scal"], sp["scal"], sp["scal"], sp["nw"], sp["hs"], sp["cols"](0), _ANY],
        out_specs=[sp["cols"](0), sp["cols"](COL_Z), bc_out, bc_out, sp["dt"], sp["scal"], sp["scal"], sp["scal"],
                   sp["nw"]],
        input_output_aliases={11: 1},
        out_shape=[jax.ShapeDtypeStruct((S, SSM_INNER), F32), jax.ShapeDtypeStruct(dproj.shape, dproj.dtype),
                   jax.ShapeDtypeStruct((S, SSM_GROUPS * N), F32), jax.ShapeDtypeStruct((S, SSM_GROUPS * N), F32),
                   jax.ShapeDtypeStruct((SSM_HEADS, S, 1), F32),
                   jax.ShapeDtypeStruct((SSM_HEADS, 1, 1), F32), jax.ShapeDtypeStruct((SSM_HEADS, 1, 1), F32),
                   jax.ShapeDtypeStruct((SSM_HEADS, 1, 1), F32), jax.ShapeDtypeStruct((1, SSM_INNER), F32)],
        scratch_shapes=[pltpu.VMEM((PAIRS_PER_GROUP, N, LANES), F32)],
        compiler_params=_params(("parallel", "arbitrary")), name="ssd_bwd",
    )(xbc, xbc, xbc, proj, dt_hm, dtb, alog, dsk, nw, hs, dmixed, dproj)


ATTN_SCALE = HEAD_DIM ** -0.5
ATTN_PAIRS = ATTN_HEADS // 2
PREP_COLS = 512


def _first_head(rows):
    return lax.broadcasted_iota(jnp.int32, (rows, LANES), 1) < HEAD_DIM


def _pair_norm(x, g2, scale):
    first = _first_head(x.shape[0])
    sq = x * x
    ms0 = jnp.sum(jnp.where(first, sq, 0.0), axis=-1, keepdims=True) * (1.0 / HEAD_DIM)
    ms1 = jnp.sum(jnp.where(first, 0.0, sq), axis=-1, keepdims=True) * (1.0 / HEAD_DIM)
    r = jnp.where(first, lax.rsqrt(ms0 + EPS), lax.rsqrt(ms1 + EPS))
    return x * r * g2 * scale


def _qk_prep_fwd(proj, gq2, gk2):
    S = proj.shape[0]
    tq = _pick(S, (512, 256))

    def body(q_ref, k_ref, v_ref, gq_ref, gk_ref, qo_ref, ko_ref, vo_ref):
        for b in range(PREP_COLS // LANES):
            pair = slice(b * LANES, (b + 1) * LANES)
            qo_ref[:, pair] = _pair_norm(q_ref[:, pair], gq_ref[...], ATTN_SCALE).astype(BF16)
            ko_ref[:, pair] = _pair_norm(k_ref[:, pair], gk_ref[...], 1.0).astype(BF16)
        vo_ref[...] = v_ref[...].astype(BF16)

    col = lambda c0: pl.BlockSpec((tq, PREP_COLS), lambda h, i: (i, c0 // PREP_COLS + h))
    blk = pl.BlockSpec((tq, PREP_COLS), lambda h, i: (i, h))
    vec = pl.BlockSpec((1, LANES), lambda h, i: (0, 0))
    return pl.pallas_call(
        body, grid=(ATTN_WIDTH // PREP_COLS, S // tq), in_specs=[col(COL_Q), col(COL_K), col(COL_V), vec, vec],
        out_specs=[blk, blk, blk], out_shape=[jax.ShapeDtypeStruct((S, ATTN_WIDTH), BF16)] * 3,
        compiler_params=_params(("parallel", "parallel")), name="qk_prep_fwd")(proj, proj, proj, gq2, gk2)


def _pair_norm_bwd(proj, col0, g2, scale, dn, dproj, name):
    S = proj.shape[0]
    tq = _pick(S, (512, 256))

    def body(u_ref, g_ref, dn_ref, buf_ref, du_ref, dg_ref):
        @pl.when((pl.program_id(0) == 0) & (pl.program_id(1) == 0))
        def _():
            dg_ref[...] = jnp.zeros_like(dg_ref)

        for b in range(PREP_COLS // LANES):
            pair = slice(b * LANES, (b + 1) * LANES)
            _, vjp = jax.vjp(lambda u, g: _pair_norm(u, g, scale), u_ref[:, pair], g_ref[...])
            du, dg = vjp(dn_ref[:, pair])
            du_ref[:, pair] = du.astype(du_ref.dtype)
            dg_ref[...] += dg

    ublk = pl.BlockSpec((tq, PREP_COLS), lambda h, i: (i, col0 // PREP_COLS + h))
    blk = pl.BlockSpec((tq, PREP_COLS), lambda h, i: (i, h))
    vec = pl.BlockSpec((1, LANES), lambda h, i: (0, 0))
    return pl.pallas_call(
        body, grid=(ATTN_WIDTH // PREP_COLS, S // tq), in_specs=[ublk, vec, blk, _ANY], out_specs=[ublk, vec],
        out_shape=[jax.ShapeDtypeStruct(dproj.shape, dproj.dtype), jax.ShapeDtypeStruct((1, LANES), F32)],
        input_output_aliases={3: 0},
        compiler_params=_params(("arbitrary", "arbitrary")), name=name)(proj, g2, dn, dproj)


def _logf_cumsum_fwd(f_raw, f_bias):
    S, Hh = f_raw.shape
    L = CHUNK

    def body(f_ref, b_ref, o_ref, wide_ref):
        ri = lax.broadcasted_iota(jnp.int32, (L, L), 0)
        ci = lax.broadcasted_iota(jnp.int32, (L, L), 1)
        tril = (ri >= ci).astype(F32)
        carry = jnp.zeros((1, Hh), F32)
        for c in range(S // L):
            rows = slice(c * L, (c + 1) * L)
            lf = -_softplus(-(f_ref[rows, :] + b_ref[...]))
            cum = _dot32(tril, lf) + carry
            o_ref[rows, :] = cum
            for h in range(Hh):
                wide_ref[rows, h * HEAD_DIM:(h + 1) * HEAD_DIM] = jnp.broadcast_to(cum[:, h:h + 1], (L, HEAD_DIM))
            carry = cum[L - 1:L, :]

    return pl.pallas_call(
        body, out_shape=[jax.ShapeDtypeStruct((S, Hh), F32), jax.ShapeDtypeStruct((S, Hh * HEAD_DIM), F32)],
        name="logf_cumsum_fwd")(f_raw, f_bias)


def _logf_cumsum_bwd(f_raw, f_bias, dcum):
    S, Hh = f_raw.shape
    L = CHUNK

    def body(f_ref, b_ref, d_ref, df_ref, db_ref):
        ri = lax.broadcasted_iota(jnp.int32, (L, L), 0)
        ci = lax.broadcasted_iota(jnp.int32, (L, L), 1)
        triu = (ri <= ci).astype(F32)
        carry = jnp.zeros((1, Hh), F32)
        db = jnp.zeros((1, Hh), F32)
        for c in reversed(range(S // L)):
            suf = _dot32(triu, d_ref[c * L:(c + 1) * L, :]) + carry
            df = suf * jax.nn.sigmoid(-(f_ref[c * L:(c + 1) * L, :] + b_ref[...]))
            df_ref[c * L:(c + 1) * L, :] = df
            db = db + jnp.sum(df, axis=0, keepdims=True)
            carry = suf[0:1, :]
        db_ref[...] = db

    return pl.pallas_call(
        body, out_shape=[jax.ShapeDtypeStruct((S, Hh), F32), jax.ShapeDtypeStruct((1, Hh), F32)],
        name="logf_cumsum_bwd")(f_raw, f_bias, dcum)


_NT = (((1,), (1,)), ((), ()))
_TN = (((0,), (0,)), ((), ()))


def _mxu(a, b, dims=(((1,), (0,)), ((), ()))):
    return lax.dot_general(a, b, dims, preferred_element_type=F32)


def _flash_fwd(qs, kn, vb, cq, ck):
    S, W = qs.shape
    tq = tk = _pick(S, (512, 256))
    nmask = max(tq // tk, 1)

    def body(q_ref, k_ref, v_ref, cq_ref, ck_ref, o_ref, of_ref, lse_ref):
        i = pl.program_id(1)
        first = _first_head(tq)
        q2 = q_ref[...]
        zero = jnp.zeros_like(q2)
        qa = (jnp.where(first, q2, zero), jnp.where(first, zero, q2))
        cqa = (cq_ref[:, 0:1], cq_ref[:, HEAD_DIM:HEAD_DIM + 1])
        row0 = i * tq

        def step(j, carry, masked):
            ms, ls, acc, rem = carry
            off = pl.multiple_of(j * tk, tk)
            k = k_ref[pl.ds(off, tk), :]
            v = v_ref[pl.ds(off, tk), :]
            new_m, new_l, alphas, pvs, prs = [], [], [], [], []
            for a in range(2):
                s = _mxu(qa[a], k, _NT) + cqa[a] - ck_ref[a, :, pl.ds(off, tk)]
                if masked:
                    ri = lax.broadcasted_iota(jnp.int32, (tq, tk), 0) + row0
                    ci = lax.broadcasted_iota(jnp.int32, (tq, tk), 1) + off
                    s = jnp.where(ri >= ci, s, -1e30)
                m_new = jnp.maximum(ms[a], jnp.max(s, axis=-1, keepdims=True))
                alpha = jnp.exp(ms[a] - m_new)
                p = jnp.exp(s - m_new)
                new_l.append(alpha * ls[a] + jnp.sum(p, axis=-1, keepdims=True))
                new_m.append(m_new)
                alphas.append(alpha)
                p_hi = p.astype(BF16)
                pvs.append(_mxu(p_hi, v))
                prs.append(_mxu((p - p_hi.astype(F32)).astype(BF16), v))
            al = jnp.where(first, alphas[0], alphas[1])
            acc = al * acc + jnp.where(first, pvs[0], pvs[1])
            rem = al * rem + jnp.where(first, prs[0], prs[1])
            return tuple(new_m), tuple(new_l), acc, rem

        neg = jnp.full((tq, 1), -1e30, F32)
        z1 = jnp.zeros((tq, 1), F32)
        z2 = jnp.zeros((tq, LANES), F32)
        carry = ((neg, neg), (z1, z1), z2, z2)
        n_full = (i * tq) // tk
        carry = lax.fori_loop(0, n_full, lambda j, c: step(j, c, False), carry)
        for jj in range(nmask):
            carry = step(n_full + jj, carry, True)
        ms, ls, acc, rem = carry
        linv = jnp.where(first, 1.0 / ls[0], 1.0 / ls[1])
        o_ref[...] = (acc * linv).astype(o_ref.dtype)
        of_ref[...] = (acc + rem) * linv
        lse_ref[...] = jnp.where(first, ms[0] + jnp.log(ls[0]), ms[1] + jnp.log(ls[1]))

    qblk = pl.BlockSpec((tq, LANES), lambda h, i: (i, h))
    full = pl.BlockSpec((S, LANES), lambda h, i: (0, h))
    return pl.pallas_call(
        body, grid=(W // LANES, S // tq),
        in_specs=[qblk, full, full, qblk, pl.BlockSpec((2, 1, S), lambda h, i: (h, 0, 0))],
        out_specs=[qblk, qblk, qblk],
        out_shape=[jax.ShapeDtypeStruct((S, W), BF16), jax.ShapeDtypeStruct((S, W), F32),
                   jax.ShapeDtypeStruct((S, W), F32)],
        compiler_params=_params(("parallel", "parallel")), name="flash_fwd")(qs, kn, vb, cq, ck)


def _flash_bwd(qs, kn, vb, cq, ck, o_fine, do, do_col0, lse):
    S, W = qs.shape
    tq = tk = _pick(S, (512, 256))
    nq = S // tq
    nmask = max(tk // tq, 1)

    def body(q_ref, k_ref, v_ref, cq_ref, ck_ref, of_ref, do_ref, lse_ref, dq_ref, dk_ref, dv_ref, dck_ref):
        j = pl.program_id(1)

        @pl.when(j == 0)
        def _():
            dq_ref[...] = jnp.zeros_like(dq_ref)

        firstk = _first_head(tk)
        firstq = _first_head(tq)
        k2 = k_ref[...]
        v2 = v_ref[...]
        zk = jnp.zeros_like(k2)
        ka = (jnp.where(firstk, k2, zk), jnp.where(firstk, zk, k2))
        va = (jnp.where(firstk, v2, zk), jnp.where(firstk, zk, v2))
        cka = (ck_ref[0], ck_ref[1])
        col0 = j * tk

        def step(i, carry, masked):
            dk, dv, dck0, dck1 = carry
            dcks = [dck0, dck1]
            off = pl.multiple_of(i * tq, tq)
            rows = pl.ds(off, tq)
            q2 = q_ref[rows, :]
            dob = do_ref[rows, :].astype(BF16)
            prod = dob.astype(F32) * of_ref[rows, :]
            dkp, dvp, dqp = [], [], []
            for a in range(2):
                lane = pl.ds(a * HEAD_DIM, 1)
                s = _mxu(q2, ka[a], _NT) + cq_ref[rows, lane] - cka[a]
                if masked:
                    ri = lax.broadcasted_iota(jnp.int32, (tq, tk), 0) + off
                    ci = lax.broadcasted_iota(jnp.int32, (tq, tk), 1) + col0
                    s = jnp.where(ri >= ci, s, -1e30)
                p = jnp.exp(s - lse_ref[rows, lane])
                dp = _mxu(dob, va[a], _NT)
                own = jnp.where(firstq, prod, 0.0) if a == 0 else jnp.where(firstq, 0.0, prod)
                ds = p * (dp - jnp.sum(own, axis=-1, keepdims=True))
                dsb = ds.astype(BF16)
                dvp.append(_mxu(p.astype(BF16), dob, _TN))
                dkp.append(_mxu(dsb, q2, _TN))
                dqp.append(_mxu(dsb, k2))
                dcks[a] = dcks[a] - jnp.sum(ds, axis=0, keepdims=True)
            dq_ref[rows, :] += jnp.where(firstq, dqp[0], dqp[1])
            dk = dk + jnp.where(firstk, dkp[0], dkp[1])
            dv = dv + jnp.where(firstk, dvp[0], dvp[1])
            return dk, dv, dcks[0], dcks[1]

        z2 = jnp.zeros((tk, LANES), F32)
        z1 = jnp.zeros((1, tk), F32)
        carry = (z2, z2, z1, z1)
        i0 = (j * tk) // tq
        for ii in range(nmask):
            carry = step(i0 + ii, carry, True)
        dk, dv, dck0, dck1 = lax.fori_loop(i0 + nmask, nq, lambda i, c: step(i, c, False), carry)
        dk_ref[...] = dk
        dv_ref[...] = dv.astype(dv_ref.dtype)
        dck_ref[0] = dck0
        dck_ref[1] = dck1

    kblk = pl.BlockSpec((tk, LANES), lambda h, j: (j, h))
    full = pl.BlockSpec((S, LANES), lambda h, j: (0, h))
    dofull = pl.BlockSpec((S, LANES), lambda h, j: (0, do_col0 // LANES + h))
    rowt = pl.BlockSpec((2, 1, tk), lambda h, j: (h, 0, j))
    dvblk = pl.BlockSpec((tk, LANES), lambda h, j: (j, COL_V // LANES + h))
    return pl.pallas_call(
        body, grid=(W // LANES, S // tk),
        in_specs=[full, kblk, kblk, full, rowt, full, dofull, full],
        out_specs=[full, kblk, dvblk, rowt],
        out_shape=[jax.ShapeDtypeStruct((S, W), F32), jax.ShapeDtypeStruct((S, W), F32),
                   jax.ShapeDtypeStruct((S, IN_COLS_PAD), BF16), jax.ShapeDtypeStruct((2 * (W // LANES), 1, S), F32)],
        compiler_params=_params(("parallel", "arbitrary")), name="flash_bwd")(qs, kn, vb, cq, ck, o_fine, do, lse)


XATTN_SCALE = XATTN_DIM ** -0.5


def _xq_norm(q, g):
    return _rms(q, g) * XATTN_SCALE


def _xattn_fwd(xq, kv, gq, gk):
    S = xq.shape[0]
    Mm = kv.shape[0]
    Dh = XATTN_DIM
    tq = _pick(S, (512, 256))

    def body(q_ref, k_ref, v_ref, gq_ref, gk_ref, o_ref):
        qn = _xq_norm(q_ref[...], gq_ref[...]).astype(BF16)
        kn = _rms(k_ref[...], gk_ref[...]).astype(BF16)
        s = _mxu(qn, kn, _NT)
        m = jnp.max(s, axis=-1, keepdims=True)
        p = jnp.exp(s - m)
        l = jnp.sum(p, axis=-1, keepdims=True)
        o_ref[...] = (_mxu(p.astype(BF16), v_ref[...].astype(BF16)) / l).astype(o_ref.dtype)

    vec = pl.BlockSpec((1, Dh), lambda h, i: (0, 0))
    return pl.pallas_call(
        body, grid=(XATTN_HEADS, S // tq),
        in_specs=[pl.BlockSpec((tq, Dh), lambda h, i: (i, h)), pl.BlockSpec((Mm, Dh), lambda h, i: (0, h)),
                  pl.BlockSpec((Mm, Dh), lambda h, i: (0, XATTN_HEADS + h)), vec, vec],
        out_specs=pl.BlockSpec((tq, Dh), lambda h, i: (i, h)),
        out_shape=jax.ShapeDtypeStruct((S, XATTN_HEADS * Dh), BF16),
        compiler_params=_params(("parallel", "parallel")), name="xattn_fwd")(xq, kv, kv, gq, gk)


def _xattn_bwd(xq, kv, gq, gk, do):
    S = xq.shape[0]
    Mm = kv.shape[0]
    Dh = XATTN_DIM
    tq = _pick(S, (512, 256))
    nq = S // tq

    def body(q_ref, k_ref, v_ref, gq_ref, gk_ref, do_ref, dq_ref, dk_ref, dv_ref, dgq_ref, dgk_ref, dkn_acc, dv_acc):
        h = pl.program_id(0)
        i = pl.program_id(1)

        @pl.when((h == 0) & (i == 0))
        def _():
            dgq_ref[...] = jnp.zeros_like(dgq_ref)
            dgk_ref[...] = jnp.zeros_like(dgk_ref)

        @pl.when(i == 0)
        def _():
            dkn_acc[...] = jnp.zeros_like(dkn_acc)
            dv_acc[...] = jnp.zeros_like(dv_acc)

        qn32, vq = jax.vjp(_xq_norm, q_ref[...], gq_ref[...])
        kn32, vk = jax.vjp(_rms, k_ref[...], gk_ref[...])
        qn = qn32.astype(BF16)
        kn = kn32.astype(BF16)
        vb = v_ref[...].astype(BF16)
        s = _mxu(qn, kn, _NT)
        m = jnp.max(s, axis=-1, keepdims=True)
        p = jnp.exp(s - m)
        p = p / jnp.sum(p, axis=-1, keepdims=True)
        dob = do_ref[...].astype(BF16)
        dp = _mxu(dob, vb, _NT)
        delta = jnp.sum(p * dp, axis=-1, keepdims=True)
        ds = (p * (dp - delta)).astype(BF16)
        dv_acc[...] += _mxu(p.astype(BF16), dob, _TN)
        dkn_acc[...] += _mxu(ds, qn, _TN)
        dq, dgq = vq(_mxu(ds, kn))
        dq_ref[...] = dq.astype(dq_ref.dtype)
        dgq_ref[...] += dgq

        @pl.when(i == nq - 1)
        def _():
            dk, dgk = vk(dkn_acc[...])
            dk_ref[...] = dk.astype(dk_ref.dtype)
            dv_ref[...] = dv_acc[...].astype(dv_ref.dtype)
            dgk_ref[...] += dgk

    vec = pl.BlockSpec((1, Dh), lambda h, i: (0, 0))
    qblk = pl.BlockSpec((tq, Dh), lambda h, i: (i, h))
    kblk = pl.BlockSpec((Mm, Dh), lambda h, i: (0, h))
    vblk = pl.BlockSpec((Mm, Dh), lambda h, i: (0, XATTN_HEADS + h))
    return pl.pallas_call(
        body, grid=(XATTN_HEADS, nq),
        in_specs=[qblk, kblk, vblk, vec, vec, qblk],
        out_specs=[qblk, kblk, kblk, vec, vec],
        out_shape=[jax.ShapeDtypeStruct((S, XATTN_HEADS * Dh), BF16),
                   jax.ShapeDtypeStruct((Mm, XATTN_HEADS * Dh), BF16),
                   jax.ShapeDtypeStruct((Mm, XATTN_HEADS * Dh), BF16),
                   jax.ShapeDtypeStruct((1, Dh), F32), jax.ShapeDtypeStruct((1, Dh), F32)],
        scratch_shapes=[pltpu.VMEM((Mm, Dh), F32), pltpu.VMEM((Mm, Dh), F32)],
        compiler_params=_params(("arbitrary", "arbitrary")), name="xattn_bwd")(xq, kv, kv, gq, gk, do)


def _loss_head(y, target):
    S, D = y.shape
    tr = _pick(S, (512, 256))

    def body(y_ref, t_ref, dy_ref, loss_ref):
        @pl.when(pl.program_id(0) == 0)
        def _():
            loss_ref[...] = jnp.zeros_like(loss_ref)

        err = y_ref[...] - t_ref[...]
        dy_ref[...] = err * (1.0 / D)
        loss_ref[...] += jnp.sum(err * err) * (0.5 / D)

    row = pl.BlockSpec((tr, D), lambda i: (i, 0))
    return pl.pallas_call(
        body, grid=(S // tr,), in_specs=[row, row],
        out_specs=[row, pl.BlockSpec((1, LANES), lambda i: (0, 0))],
        out_shape=[jax.ShapeDtypeStruct((S, D), F32), jax.ShapeDtypeStruct((1, LANES), F32)],
        compiler_params=_params(("arbitrary",)), name="loss_head")(y, target)


def _row_tile(R, C):
    for tr in (1024, 512, 256, 128, 64, 32, 16, 8):
        if R % tr == 0 and tr * C * 4 <= (1 << 20):
            return tr
    return R


def _chip_sum(own, from_chips, name):
    R, C = own.shape
    tr = _row_tile(R, C)

    def body(own_ref, a_ref, b_ref, c_ref, o_ref):
        o_ref[...] = ((own_ref[...].astype(F32) + a_ref[...].astype(F32)) + b_ref[...].astype(F32)) + c_ref[...].astype(F32)

    blk = pl.BlockSpec((tr, C), lambda i: (i, 0))
    slab = lambda s: pl.BlockSpec((None, tr, C), lambda i: (s, i, 0))
    return pl.pallas_call(
        body, grid=(R // tr,), in_specs=[blk, slab(0), slab(1), slab(2)], out_specs=blk,
        out_shape=jax.ShapeDtypeStruct((R, C), F32),
        compiler_params=_params(("parallel",)), name=name)(own, from_chips, from_chips, from_chips)


def _adamw(w, g_mine, g_sibling, m, v, name):
    R, C = w.shape
    tr = _row_tile(R, C)
    c1 = 1.0 - ADAM_B1 ** ADAM_STEP
    c2 = 1.0 - ADAM_B2 ** ADAM_STEP

    def body(w_ref, ga_ref, gb_ref, m_ref, v_ref, g_ref, d_ref, mo_ref, vo_ref):
        g_t = ga_ref[...] + gb_ref[...]
        m_new = ADAM_B1 * m_ref[...] + (1.0 - ADAM_B1) * g_t
        v_new = ADAM_B2 * v_ref[...] + (1.0 - ADAM_B2) * (g_t * g_t)
        g_ref[...] = g_t
        d_ref[...] = -ADAM_LR * ((m_new / c1) / (jnp.sqrt(v_new / c2) + ADAM_EPS) + ADAM_WD * w_ref[...])
        mo_ref[...] = m_new
        vo_ref[...] = v_new

    blk = pl.BlockSpec((tr, C), lambda i: (i, 0))
    return pl.pallas_call(
        body, grid=(R // tr,), in_specs=[blk] * 5, out_specs=[blk] * 4,
        out_shape=[jax.ShapeDtypeStruct((R, C), F32)] * 4,
        compiler_params=_params(("parallel",)), name=name)(w, g_mine, g_sibling, m, v)


D_MODEL = 1024
SSM_INNER = SSM_HEADS * HEAD_DIM
CONV_DIM = SSM_INNER + 2 * SSM_GROUPS * SSM_STATE
ATTN_WIDTH = ATTN_HEADS * HEAD_DIM
COL_Z = 0
COL_XBC = COL_Z + SSM_INNER
COL_Q = COL_XBC + CONV_DIM
COL_K = COL_Q + ATTN_WIDTH
COL_V = COL_K + ATTN_WIDTH
COL_DT = COL_V + ATTN_WIDTH
COL_F = COL_DT + SSM_HEADS
IN_COLS = COL_F + ATTN_HEADS
IN_COLS_PAD = -(-IN_COLS // LANES) * LANES
REF_COL_DT = COL_Q
SHARD_COLS = IN_COLS // N_CHIPS
_COL_RANGES = ((0, REF_COL_DT, 0), (REF_COL_DT + SSM_HEADS, COL_F, COL_Q), (REF_COL_DT, REF_COL_DT + SSM_HEADS, COL_DT),
               (COL_F, IN_COLS, COL_F))


def _w_in_from_shards(g):
    parts = []
    for lo, hi, _ in _COL_RANGES:
        while lo < hi:
            j = lo // SHARD_COLS
            end = min(hi, (j + 1) * SHARD_COLS)
            parts.append(g[j][:, lo - j * SHARD_COLS:end - j * SHARD_COLS])
            lo = end
    parts.append(jnp.zeros((g.shape[1], IN_COLS_PAD - IN_COLS), g.dtype))
    return jnp.concatenate(parts, axis=1)


def _w_in_to_shards(w):
    shards = []
    for j in range(N_CHIPS):
        parts = []
        for lo, hi, here in sorted(_COL_RANGES):
            a, b = max(lo, j * SHARD_COLS), min(hi, (j + 1) * SHARD_COLS)
            if a < b:
                parts.append(w[:, here + a - lo:here + b - lo])
        shards.append(jnp.concatenate(parts, axis=1))
    return jnp.stack(shards)


def _add_residual(acc, res):
    return (res + acc,)


def _relu2(acc):
    r = jnp.maximum(acc, 0.0)
    return acc, r * r


def _relu2_bwd(acc, a):
    return (acc * (2.0 * jnp.maximum(a, 0.0)),)


def _layer_fwd_bwd(x, mem, target, w_in, p, late_weights, send_late_grads, send_w_in_grad):
    S = x.shape[0]
    hd3 = lambda a: a.reshape(SSM_HEADS, 1, 1)

    h1 = _rmsnorm_fwd(x, p["g_mix"], "norm_mix")
    proj = _mm(h1, w_in, "nn", "in_proj")
    xbc = _conv_fwd(proj, COL_XBC, CONV_DIM, p["conv_w"], p["conv_b"])
    dt_hm = proj[:, COL_DT:COL_DT + SSM_HEADS].T[:, :, None]
    ssd_par = (hd3(p["dt_bias"]), hd3(p["a_log"]), hd3(p["d_skip"]), p["ssm_norm_w"])
    y, hs = _ssd_fwd(xbc, proj, dt_hm, *ssd_par)
    f_raw = proj[:, COL_F:COL_F + ATTN_HEADS]
    gq2 = jnp.tile(p["g_q"], (1, 2))
    gk2 = jnp.tile(p["g_k"], (1, 2))
    qs, kn, vb = _qk_prep_fwd(proj, gq2, gk2)
    cum, cq = _logf_cumsum_fwd(f_raw, p["f_bias"])
    ck = cum.T[:, None, :]
    o, o_fine, lse = _flash_fwd(qs, kn, vb, cq, ck)
    W = late_weights((o_fine, y))
    x1 = _mm(y, W["w_out"][:SSM_INNER], "nn", "out_proj_ssm", epilogue=_add_residual, extras=(x,))
    x1 = _mm(o, W["w_out"][SSM_INNER:], "nn", "out_proj_attn", epilogue=_add_residual, extras=(x1,))
    h2 = _rmsnorm_fwd(x1, p["g_xattn"], "norm_xattn")
    mem_n = _rmsnorm_fwd(mem, p["g_mem"], "norm_mem")
    xq = _mm(h2, W["xq_w"], "nn", "xq_proj")
    kv = _mm(mem_n, W["xkv_w"], "nn", "xkv_proj", b_chunks=N_CHIPS)
    xo = _xattn_fwd(xq, kv, p["xg_q"], p["xg_k"])
    x2 = _mm(xo, W["xo_w"], "nn", "xo_proj", epilogue=_add_residual, extras=(x1,))
    h3 = _rmsnorm_fwd(x2, p["g_mlp"], "norm_mlp")
    a, act = _mm(h3, W["w_up"], "nn", "mlp_up", out_dtypes=(F32, BF16), epilogue=_relu2, b_chunks=N_CHIPS)
    x3 = _mm(act, W["w_down"], "nn", "mlp_down", epilogue=_add_residual, extras=(x2,))
    dy, loss_row = _loss_head(x3, target)

    gW, gp = {}, {}
    da = _mm(dy, W["w_down"], "nt", "d_act", out_dtypes=(BF16,), epilogue=_relu2_bwd, extras=(a,))
    gW["w_down"] = _mm(act, dy, "tn", "g_w_down", out_dtypes=(BF16,))
    gW["w_up"] = _mm(h3, da, "tn", "g_w_up", out_dtypes=(BF16,), out_chunks=N_CHIPS)
    dh3 = _mm(da, W["w_up"], "nt", "d_h3", b_chunks=N_CHIPS)
    dx2, gp["g_mlp"] = _rmsnorm_bwd(x2, p["g_mlp"], dh3, dy, "norm_mlp_bwd")
    dxo = _mm(dx2, W["xo_w"], "nt", "d_xo", out_dtypes=(BF16,))
    gW["xo_w"] = _mm(xo, dx2, "tn", "g_xo_w", out_dtypes=(BF16,))
    dxq, dk_x, dv_x, gp["xg_q"], gp["xg_k"] = _xattn_bwd(xq, kv, p["xg_q"], p["xg_k"], dxo)
    dkv = jnp.concatenate([dk_x, dv_x], axis=-1)
    gW["xq_w"] = _mm(h2, dxq, "tn", "g_xq_w", out_dtypes=(BF16,))
    dh2 = _mm(dxq, W["xq_w"], "nt", "d_h2")
    gW["xkv_w"] = _mm(mem_n, dkv, "tn", "g_xkv_w", out_dtypes=(BF16,), out_chunks=N_CHIPS)
    dmem_n = _mm(dkv, W["xkv_w"], "nt", "d_mem_n", b_chunks=N_CHIPS)
    _, gp["g_mem"] = _rmsnorm_bwd(mem, p["g_mem"], dmem_n, None, "norm_mem_bwd")
    dx1, gp["g_xattn"] = _rmsnorm_bwd(x1, p["g_xattn"], dh2, dx2, "norm_xattn_bwd")
    dmixed = _mm(dx1, W["w_out"], "nt", "d_mixed")
    gW["w_out"] = jnp.concatenate([_mm(y, dx1, "tn", "g_w_out_ssm", out_dtypes=(BF16,)),
                                   _mm(o, dx1, "tn", "g_w_out_attn", out_dtypes=(BF16,))], axis=0)
    token = send_late_grads(gW)
    dqs, dkn, dproj, dck = _flash_bwd(qs, kn, vb, cq, ck + token[:1, :1], o_fine, dmixed, SSM_INNER, lse)
    dproj, dgq2 = _pair_norm_bwd(proj, COL_Q, gq2, ATTN_SCALE, dqs, dproj, "q_norm_bwd")
    dproj, dgk2 = _pair_norm_bwd(proj, COL_K, gk2, 1.0, dkn, dproj, "k_norm_bwd")
    gp["g_q"] = dgq2[:, :HEAD_DIM] + dgq2[:, HEAD_DIM:]
    gp["g_k"] = dgk2[:, :HEAD_DIM] + dgk2[:, HEAD_DIM:]
    df, gp["f_bias"] = _logf_cumsum_bwd(f_raw, p["f_bias"], dck[:, 0, :].T)
    dxs, dproj, dB, dC, ddt, ddtb, dalog, ddsk, gp["ssm_norm_w"] = _ssd_bwd(xbc, proj, dt_hm, *ssd_par, hs, dmixed, dproj)
    gp["dt_bias"] = ddtb.reshape(1, SSM_HEADS)
    gp["a_log"] = dalog.reshape(1, SSM_HEADS)
    gp["d_skip"] = ddsk.reshape(1, SSM_HEADS)
    dproj, dconv_w, gp["conv_b"] = _conv_bwd(proj, COL_XBC, CONV_DIM, p["conv_w"], p["conv_b"], (dxs, dB, dC), dproj)
    gp["conv_w"] = dconv_w[:CONV_WIDTH]
    tail = jnp.concatenate([ddt[:, :, 0].T, df, jnp.zeros((S, IN_COLS_PAD - IN_COLS), F32)], axis=-1).astype(BF16)
    dproj = lax.dynamic_update_slice(dproj, tail, (0, COL_DT))
    token = send_w_in_grad(_mm(h1, dproj, "tn", "g_w_in", out_dtypes=(BF16,)))
    dh1 = _mm(dproj, w_in, "nt", "d_h1")
    dx, gp["g_mix"] = _rmsnorm_bwd(x, p["g_mix"] + token[:1, :1], dh1, dx1, "norm_mix_bwd")
    return loss_row, dx, gp


_ANY = pl.BlockSpec(memory_space=pl.ANY)


def _place():
    x, y, c = lax.axis_index("x"), lax.axis_index("y"), lax.axis_index("c")
    chips = [(1 - x, y), (x, 1 - y), (1 - x, 1 - y)]
    return x, y, c, chips


def _chip_index(px, py):
    return 2 * px + py


def _all_gather_chips(split, whole):
    ns, nw = len(split), len(whole)
    n = ns + nw

    def body(*refs):
        ins, outs = refs[:n], refs[n:2 * n]
        send_ici, recv_ici, send_d2d, recv_d2d = refs[2 * n:]
        x, y, c, chips = _place()
        me = _chip_index(x, y)
        sib = (x, y, 1 - c)

        def ici(k, j, src, dst):
            return pltpu.make_async_remote_copy(src_ref=src, dst_ref=dst, send_sem=send_ici.at[3 * k + j],
                                                recv_sem=recv_ici.at[3 * k + j], device_id=(*chips[j], c),
                                                device_id_type=MESH)

        def d2d(k, j, piece):
            return pltpu.make_async_remote_copy(src_ref=piece, dst_ref=piece, send_sem=send_d2d.at[3 * k + j],
                                                recv_sem=recv_d2d.at[3 * k + j], device_id=sib, device_id_type=MESH)

        sends = []
        for k in range(n):
            for j in range(3):
                if k < ns:
                    sends.append(ici(k, j, ins[k].at[c], outs[k].at[me, c]))
                else:
                    sends.append(ici(k, j, ins[k], outs[k].at[me]))
                sends[-1].start()
        passed = []
        for k in range(n):
            for j in range(3):
                src_chip = _chip_index(*chips[j])
                if k < ns:
                    ici(k, j, ins[k].at[c], outs[k].at[src_chip, c]).wait_recv()
                    passed.append(d2d(k, j, outs[k].at[src_chip, c]))
                    passed[-1].start()
                else:
                    ici(k, j, ins[k], outs[k].at[src_chip]).wait_recv()
        for k in range(ns):
            for j in range(3):
                d2d(k, j, outs[k].at[_chip_index(*chips[j]), 1 - c]).wait_recv()
        for cp in sends + passed:
            cp.wait_send()

    arrs = list(split) + list(whole)
    return pl.pallas_call(
        body, in_specs=[_ANY] * n, out_specs=[_ANY] * n,
        out_shape=[jax.ShapeDtypeStruct((N_CHIPS,) + a.shape, a.dtype) for a in arrs],
        scratch_shapes=[pltpu.SemaphoreType.DMA((3 * n,)), pltpu.SemaphoreType.DMA((3 * n,)),
                        pltpu.SemaphoreType.DMA((3 * ns,)), pltpu.SemaphoreType.DMA((3 * ns,))],
        name="all_gather_chips")(*arrs)


def _sibling_swap(arrs, name):
    n = len(arrs)

    def body(*refs):
        ins, outs = refs[:n], refs[n:2 * n]
        send_sem, recv_sem = refs[2 * n:]
        x, y, c, _ = _place()
        copies = [pltpu.make_async_remote_copy(src_ref=ins[k], dst_ref=outs[k], send_sem=send_sem.at[k],
                                               recv_sem=recv_sem.at[k], device_id=(x, y, 1 - c), device_id_type=MESH)
                  for k in range(n)]
        for q in copies:
            q.start()
        for q in copies:
            q.wait()

    return pl.pallas_call(
        body, in_specs=[_ANY] * n, out_specs=[_ANY] * n,
        out_shape=[jax.ShapeDtypeStruct(a.shape, a.dtype) for a in arrs],
        scratch_shapes=[pltpu.SemaphoreType.DMA((n,)), pltpu.SemaphoreType.DMA((n,))],
        name=name)(*arrs)


_HBM = pl.BlockSpec(memory_space=pltpu.HBM)
_SEM = pl.BlockSpec(memory_space=pltpu.SEMAPHORE)
_SPLIT_EFFECT = pltpu.SideEffectType.DATAFLOW_SIDE_EFFECTING


class _Split(NamedTuple):
    send_sems: jax.Array
    recv_sems: jax.Array
    sources: tuple
    lands: tuple
    token: jax.Array


def _split_copies(kind, srcs, lands, send_sems, recv_sems):
    x, y, c, chips = _place()
    me = _chip_index(x, y)
    copies = []
    for k in range(len(srcs)):
        for j in range(3):
            if kind == "gather":
                src, dst = srcs[k], lands[k].at[me]
            else:
                src, dst = srcs[k].at[_chip_index(*chips[j])], lands[k].at[j]
            copies.append(pltpu.make_async_remote_copy(
                src_ref=src, dst_ref=dst, send_sem=send_sems.at[3 * k + j], recv_sem=recv_sems.at[3 * k + j],
                device_id=(*chips[j], c), device_id_type=MESH))
    return copies


def _split_start(name, sources, kind, after):
    n = len(sources)
    if kind == "gather":
        lands = [lax.empty((N_CHIPS,) + s.shape, s.dtype) for s in sources]
    else:
        lands = [lax.empty((3,) + s.shape[1:], s.dtype) for s in sources]
    deps = [] if after is None else [after]

    def body(*refs):
        srcs, lnds = refs[:n], refs[n:2 * n]
        send_sems, recv_sems = refs[2 * n + len(deps)], refs[2 * n + len(deps) + 1]
        for cp in _split_copies(kind, srcs, lnds, send_sems, recv_sems):
            cp.start()
        refs[-1][...] = jnp.zeros_like(refs[-1])

    hbm = lambda a: pltpu.with_memory_space_constraint(a, pltpu.HBM)
    outs = pl.pallas_call(
        body, name=name,
        in_specs=[_HBM] * (2 * n) + [_ANY] * len(deps),
        out_specs=[_SEM, _SEM] + [_HBM] * (2 * n) + [pl.BlockSpec(memory_space=pltpu.VMEM)],
        out_shape=[pltpu.SemaphoreType.DMA((3 * n,)), pltpu.SemaphoreType.DMA((3 * n,))]
        + [pltpu.HBM(a.shape, a.dtype) for a in list(sources) + lands] + [jax.ShapeDtypeStruct((8, LANES), F32)],
        input_output_aliases={k: 2 + k for k in range(2 * n)},
        compiler_params=pltpu.CompilerParams(has_side_effects=_SPLIT_EFFECT),
    )(*[hbm(s) for s in sources], *[hbm(l) for l in lands], *deps)
    return _Split(outs[0], outs[1], tuple(outs[2:2 + n]), tuple(outs[2 + n:2 + 2 * n]), outs[-1])


def _split_wait(name, h, kind, after):
    n = len(h.sources)

    def body(*refs):
        srcs, lnds = refs[:n], refs[n:2 * n]
        for cp in _split_copies(kind, srcs, lnds, refs[2 * n], refs[2 * n + 1]):
            cp.wait_send()
            cp.wait_recv()

    outs = pl.pallas_call(
        body, name=name,
        in_specs=[_HBM] * (2 * n) + [_SEM, _SEM] + [_ANY] * len(after),
        out_specs=[_HBM] * (2 * n),
        out_shape=[pltpu.HBM(a.shape, a.dtype) for a in h.sources + h.lands],
        input_output_aliases={k: k for k in range(2 * n)},
        compiler_params=pltpu.CompilerParams(has_side_effects=_SPLIT_EFFECT),
    )(*h.sources, *h.lands, h.send_sems, h.recv_sems, *after)
    return outs[:n], outs[n:]


def _all_reduce_small(vec, after):
    R, C = vec.shape

    def body(v_ref, after_ref, o_ref, buf, send_sem, recv_sem):
        x, y, c = lax.axis_index("x"), lax.axis_index("y"), lax.axis_index("c")
        me = 4 * x + 2 * y + c
        buf[me] = v_ref[...]
        copies = []
        for r in range(1, N_DEV):
            fx, fy, fc = (r >> 2) & 1, (r >> 1) & 1, r & 1
            peer = (x ^ fx, y ^ fy, c ^ fc)
            copies.append(pltpu.make_async_remote_copy(src_ref=v_ref, dst_ref=buf.at[me], send_sem=send_sem.at[r - 1],
                                                       recv_sem=recv_sem.at[r - 1], device_id=peer, device_id_type=MESH))
        for q in copies:
            q.start()
        for r in range(1, N_DEV):
            fx, fy, fc = (r >> 2) & 1, (r >> 1) & 1, r & 1
            src = 4 * (x ^ fx) + 2 * (y ^ fy) + (c ^ fc)
            pltpu.make_async_remote_copy(src_ref=v_ref, dst_ref=buf.at[src], send_sem=send_sem.at[r - 1],
                                         recv_sem=recv_sem.at[r - 1], device_id=(x, y, c), device_id_type=MESH).wait_recv()
        acc = buf[0]
        for d in range(1, N_DEV):
            acc = acc + buf[d]
        o_ref[...] = acc
        for q in copies:
            q.wait_send()

    vm = pl.BlockSpec(memory_space=pltpu.VMEM)
    return pl.pallas_call(
        body, in_specs=[vm, _ANY], out_specs=vm, out_shape=jax.ShapeDtypeStruct((R, C), F32),
        scratch_shapes=[pltpu.VMEM((N_DEV, R, C), F32), pltpu.SemaphoreType.DMA((N_DEV - 1,)),
                        pltpu.SemaphoreType.DMA((N_DEV - 1,))],
        name="all_reduce_small")(vec, after)


_INPUTS = ["x", "mem", "g_mix", "w_in", "conv_w", "conv_b", "dt_bias", "a_log", "d_skip", "ssm_norm_w", "g_q", "g_k",
           "f_bias", "w_out", "g_xattn", "g_mem", "xq_w", "xkv_w", "xg_q", "xg_k", "xo_w", "g_mlp", "w_up", "w_down"]
_WEIGHTS = _INPUTS[2:]
_BIG = ["w_in", "w_out", "xq_w", "xkv_w", "xo_w", "w_up", "w_down"]
_LATE = _BIG[1:]
_COL_SHARDED = ["w_in", "xkv_w", "w_up"]
_SMALL = [n for n in _WEIGHTS if n not in _BIG]


def _pack_rows(arrs, width):
    starts, r = [], 0
    for a in arrs:
        starts.append(r)
        r += a.shape[0]
    out = jnp.concatenate([jnp.pad(a, ((0, 0), (0, width - a.shape[1]))) for a in arrs], axis=0)
    return jnp.pad(out, ((0, -r % 8), (0, 0))), starts


def _adamw_small(summed, starts, ws, ms, vs, conv_w_index):
    n = len(ws)
    c1 = 1.0 - ADAM_B1 ** ADAM_STEP
    c2 = 1.0 - ADAM_B2 ** ADAM_STEP

    def body(s_ref, *refs):
        w_refs, m_refs, v_refs = refs[:n], refs[n:2 * n], refs[2 * n:3 * n]
        outs = refs[3 * n:]
        chip = _chip_index(lax.axis_index("x"), lax.axis_index("y"))
        for k in range(n):
            rows, cols = w_refs[k].shape
            if k == conv_w_index:
                g = s_ref[starts[k]:starts[k] + rows, pl.ds(pl.multiple_of(chip * cols, LANES), cols)]
            else:
                g = s_ref[starts[k]:starts[k] + rows, 0:cols]
            m_new = ADAM_B1 * m_refs[k][...] + (1.0 - ADAM_B1) * g
            v_new = ADAM_B2 * v_refs[k][...] + (1.0 - ADAM_B2) * (g * g)
            outs[4 * k][...] = g
            outs[4 * k + 1][...] = -ADAM_LR * ((m_new / c1) / (jnp.sqrt(v_new / c2) + ADAM_EPS) + ADAM_WD * w_refs[k][...])
            outs[4 * k + 2][...] = m_new
            outs[4 * k + 3][...] = v_new

    vm = pl.BlockSpec(memory_space=pltpu.VMEM)
    outs = pl.pallas_call(
        body, in_specs=[vm] * (1 + 3 * n), out_specs=[vm] * (4 * n),
        out_shape=[jax.ShapeDtypeStruct(a.shape, F32) for a in ws for _ in range(4)],
        name="adamw_small")(summed, *ws, *ms, *vs)
    return [outs[4 * k:4 * k + 4] for k in range(n)]


def kernel(x, mem, g_mix, w_in, conv_w, conv_b, dt_bias, a_log, d_skip, ssm_norm_w, g_q, g_k, f_bias, w_out, g_xattn, g_mem, xq_w, xkv_w, xg_q, xg_k, xo_w, g_mlp, w_up, w_down, loss_target, m_g_mix, m_w_in, m_conv_w, m_conv_b, m_dt_bias, m_a_log, m_d_skip, m_ssm_norm_w, m_g_q, m_g_k, m_f_bias, m_w_out, m_g_xattn, m_g_mem, m_xq_w, m_xkv_w, m_xg_q, m_xg_k, m_xo_w, m_g_mlp, m_w_up, m_w_down, v_g_mix, v_w_in, v_conv_w, v_conv_b, v_dt_bias, v_a_log, v_d_skip, v_ssm_norm_w, v_g_q, v_g_k, v_f_bias, v_w_out, v_g_xattn, v_g_mem, v_xq_w, v_xkv_w, v_xg_q, v_xg_k, v_xo_w, v_g_mlp, v_w_up, v_w_down):
    args = (x, mem, g_mix, w_in, conv_w, conv_b, dt_bias, a_log, d_skip, ssm_norm_w, g_q, g_k, f_bias, w_out, g_xattn,
            g_mem, xq_w, xkv_w, xg_q, xg_k, xo_w, g_mlp, w_up, w_down)
    w = dict(zip(_INPUTS, args))
    mom1 = dict(zip(_WEIGHTS, (m_g_mix, m_w_in, m_conv_w, m_conv_b, m_dt_bias, m_a_log, m_d_skip, m_ssm_norm_w, m_g_q,
                               m_g_k, m_f_bias, m_w_out, m_g_xattn, m_g_mem, m_xq_w, m_xkv_w, m_xg_q, m_xg_k, m_xo_w,
                               m_g_mlp, m_w_up, m_w_down)))
    mom2 = dict(zip(_WEIGHTS, (v_g_mix, v_w_in, v_conv_w, v_conv_b, v_dt_bias, v_a_log, v_d_skip, v_ssm_norm_w, v_g_q,
                               v_g_k, v_f_bias, v_w_out, v_g_xattn, v_g_mem, v_xq_w, v_xkv_w, v_xg_q, v_xg_k, v_xo_w,
                               v_g_mlp, v_w_up, v_w_down)))
    chip = _chip_index(lax.axis_index("x"), lax.axis_index("y"))

    shard_bf = {n: w[n][0].astype(BF16) for n in _BIG}

    def layout_for_compute(n, g):
        if n == "w_in":
            return _w_in_from_shards(g)
        return g if n in _COL_SHARDED else g.reshape(N_CHIPS * g.shape[1], g.shape[2])

    def layout_for_reduction(n, g):
        if n == "w_in":
            return _w_in_to_shards(g)
        return g if n in _COL_SHARDED else g.reshape(N_CHIPS, g.shape[0] // N_CHIPS, g.shape[1])

    halves_in = shard_bf["w_in"].reshape(2, shard_bf["w_in"].shape[0] // 2, -1)
    g_in, g_conv = _all_gather_chips([halves_in], [w["conv_w"][0]])
    g_in = lax.dynamic_update_index_in_dim(g_in, halves_in, chip, axis=0)
    g_conv = lax.dynamic_update_index_in_dim(g_conv, w["conv_w"][0], chip, axis=0)
    w_in_full = layout_for_compute("w_in", g_in.reshape(N_CHIPS, -1, g_in.shape[-1]))
    p = {n: w[n] for n in _SMALL}
    p["conv_w"] = g_conv.transpose(1, 0, 2).reshape(CONV_WIDTH, CONV_DIM)
    gather = _split_start("gather_late", [shard_bf[n] for n in _LATE], "gather", after=g_in)
    p["g_mix"] = p["g_mix"] + gather.token[:1, :1]

    def late_weights(after):
        srcs, lands = _split_wait("gather_late_wait", gather, "gather", after)
        lands = [lax.dynamic_update_index_in_dim(l, s, chip, axis=0) for l, s in zip(lands, srcs)]
        return {n: layout_for_compute(n, l) for n, l in zip(_LATE, lands)}

    scatter = {}

    def send_late_grads(grads):
        scatter["late"] = _split_start("scatter_late", [layout_for_reduction(n, grads[n]) for n in _LATE], "scatter",
                                       after=None)
        return scatter["late"].token

    def send_w_in_grad(g):
        scatter["w_in"] = _split_start("scatter_w_in", [layout_for_reduction("w_in", g)], "scatter", after=None)
        return scatter["w_in"].token

    loss_row, dx, gp = _layer_fwd_bwd(x[0], mem[0], loss_target[0], w_in_full, p, late_weights, send_late_grads,
                                      send_w_in_grad)

    grad, delta, new_m, new_v = {}, {}, {}, {}

    def finish(names, sources, from_chips, tag):
        mine = [_chip_sum(lax.dynamic_index_in_dim(s, chip, axis=0, keepdims=False), fc, "rs_chip_sum_" + n)
                for n, s, fc in zip(names, sources, from_chips)]
        for n, a, b in zip(names, mine, _sibling_swap(mine, "rs_sibling_swap_" + tag)):
            shape = w[n].shape
            res = _adamw(w[n][0], a, b, mom1[n][0], mom2[n][0], "adamw_" + n)
            grad[n], delta[n], new_m[n], new_v[n] = (r.reshape(shape) for r in res)

    finish(_LATE, *_split_wait("scatter_late_wait", scatter["late"], "scatter", (dx,)), "late")

    sources_in, from_chips_in = _split_wait("scatter_w_in_wait", scatter["w_in"], "scatter",
                                            tuple(new_v[n] for n in _LATE))

    packed, starts = _pack_rows([gp[n] for n in _SMALL] + [loss_row], CONV_DIM)
    summed = _all_reduce_small(packed, from_chips_in[0])
    loss = summed[starts[-1], 0]
    finish(["w_in"], sources_in, from_chips_in, "w_in")

    as_rows = lambda a: a.reshape(-1, a.shape[-1])
    results = _adamw_small(summed, starts, [as_rows(w[n]) for n in _SMALL], [as_rows(mom1[n]) for n in _SMALL],
                           [as_rows(mom2[n]) for n in _SMALL], _SMALL.index("conv_w"))
    for n, res in zip(_SMALL, results):
        grad[n], delta[n], new_m[n], new_v[n] = (a.reshape(w[n].shape) for a in res)

    return (loss, dx[None], *[grad[n] for n in _WEIGHTS], *[delta[n] for n in _WEIGHTS],
            *[new_m[n] for n in _WEIGHTS], *[new_v[n] for n in _WEIGHTS])
```

```python
from typing import NamedTuple

import jax
import jax.numpy as jnp
from jax import lax
from jax.experimental import pallas as pl
from jax.experimental.pallas import tpu as pltpu

F32 = jnp.float32
BF16 = jnp.bfloat16
HI = lax.Precision.HIGHEST
MESH = pl.DeviceIdType.MESH

EPS = 1e-5
CHUNK = 128
SSM_HEADS = 16
SSM_GROUPS = 2
HEADS_PER_GROUP = SSM_HEADS // SSM_GROUPS
HEAD_DIM = 64
SSM_STATE = 128
ATTN_HEADS = 16
XATTN_HEADS = 4
XATTN_DIM = 256
CONV_WIDTH = 4
CONV_COLS = 256
N_CHIPS = 4
N_DEV = 8
LANES = 128
VMEM_LIMIT = 56 * 1024 * 1024

ADAM_LR = 0.001
ADAM_B1 = 0.9
ADAM_B2 = 0.999
ADAM_EPS = 1e-08
ADAM_WD = 0.01
ADAM_STEP = 10


def _params(sem):
    return pltpu.CompilerParams(dimension_semantics=sem, vmem_limit_bytes=VMEM_LIMIT)


def _pick(n, cands):
    for c in cands:
        if n % c == 0:
            return c
    return n


def _mm(a, b, mode, name, out_dtypes=(F32,), epilogue=None, extras=(), b_chunks=1, out_chunks=1,
        tm=None, tn=None, tk=None):
    if mode == "nn":
        M, K = a.shape
        N = b.shape[-1] * b_chunks
    elif mode == "nt":
        M, K = a.shape
        N = b.shape[-2]
        assert b.shape[-1] * b_chunks == K
    else:
        K, M = a.shape
        N = b.shape[-1] * b_chunks
    tm = tm or _pick(M, (2048, 1024, 512, 256, 128))
    tn = tn or _pick(N // max(b_chunks if mode != "nt" else 1, out_chunks), (512, 640, 384, 256, 128))
    if tk is None:
        kmax = b.shape[-1] if mode == "nt" else K
        tk = kmax if kmax <= 2048 else _pick(kmax, (2048, 1152, 1024, 512))
    nk = K // tk
    assert M % tm == 0 and N % tn == 0 and K % tk == 0
    grid = (M // tm, N // tn, nk)

    if mode == "tn":
        a_spec = pl.BlockSpec((tk, tm), lambda i, j, k: (k, i))
    else:
        a_spec = pl.BlockSpec((tm, tk), lambda i, j, k: (i, k))

    def b_index(t_row, t_last, tile_last):
        if b_chunks == 1:
            return (t_row, t_last)
        q = (b.shape[-1]) // tile_last
        return (t_last // q, t_row, t_last % q)

    if mode == "nn" or mode == "tn":
        bshape = (tk, tn)
        bmap = lambda i, j, k: b_index(k, j, tn)
    else:
        bshape = (tn, tk)
        bmap = lambda i, j, k: b_index(j, k, tk)
    if b_chunks > 1:
        bshape = (None,) + bshape
    b_spec = pl.BlockSpec(bshape, bmap)

    if out_chunks == 1:
        o_spec = pl.BlockSpec((tm, tn), lambda i, j, k: (i, j))
        o_shape = (M, N)
    else:
        qo = (N // out_chunks) // tn
        o_spec = pl.BlockSpec((None, tm, tn), lambda i, j, k: (j // qo, i, j % qo))
        o_shape = (out_chunks, M, N // out_chunks)
    e_spec = pl.BlockSpec((tm, tn), lambda i, j, k: (i, j))

    dims = {"nn": (((1,), (0,)), ((), ())), "nt": (((1,), (1,)), ((), ())), "tn": (((0,), (0,)), ((), ()))}[mode]
    n_ex = len(extras)
    n_out = len(out_dtypes)

    def body(*refs):
        a_ref, b_ref = refs[0], refs[1]
        ex_refs = refs[2:2 + n_ex]
        o_refs = refs[2 + n_ex:2 + n_ex + n_out]

        def finish(acc):
            outs = epilogue(acc, *[r[...] for r in ex_refs]) if epilogue is not None else (acc,)
            for r, o in zip(o_refs, outs):
                r[...] = o.astype(r.dtype)

        part = lax.dot_general(a_ref[...].astype(BF16), b_ref[...].astype(BF16), dims,
                               preferred_element_type=F32)
        if nk == 1:
            finish(part)
        else:
            acc_ref = refs[-1]
            k = pl.program_id(2)

            @pl.when(k == 0)
            def _():
                acc_ref[...] = part

            @pl.when(k > 0)
            def _():
                acc_ref[...] += part

            @pl.when(k == nk - 1)
            def _():
                finish(acc_ref[...])

    outs = pl.pallas_call(
        body,
        grid=grid,
        in_specs=[a_spec, b_spec] + [e_spec] * n_ex,
        out_specs=[o_spec] * n_out,
        out_shape=[jax.ShapeDtypeStruct(o_shape, d) for d in out_dtypes],
        scratch_shapes=[pltpu.VMEM((tm, tn), F32)] if nk > 1 else [],
        compiler_params=_params(("parallel", "parallel", "arbitrary")),
        name=name,
    )(a, b, *extras)
    return outs[0] if n_out == 1 else outs


def _rms(x, g):
    r = lax.rsqrt(jnp.mean(x * x, axis=-1, keepdims=True) + EPS)
    return x * r * g


def _rmsnorm_fwd(x, g, name):
    R, D = x.shape
    tr = _pick(R, (512, 256))

    def body(x_ref, g_ref, o_ref):
        o_ref[...] = _rms(x_ref[...], g_ref[...]).astype(o_ref.dtype)

    return pl.pallas_call(
        body, grid=(R // tr,),
        in_specs=[pl.BlockSpec((tr, D), lambda i: (i, 0)), pl.BlockSpec((1, D), lambda i: (0, 0))],
        out_specs=pl.BlockSpec((tr, D), lambda i: (i, 0)),
        out_shape=jax.ShapeDtypeStruct((R, D), BF16),
        compiler_params=_params(("parallel",)), name=name)(x, g)


def _rmsnorm_bwd(x, g, dh, dres, name):
    R, D = x.shape
    tr = _pick(R, (256,))
    has_res = dres is not None

    def body(*refs):
        if has_res:
            x_ref, g_ref, dh_ref, dres_ref, dx_ref, dg_ref = refs
        else:
            x_ref, g_ref, dh_ref, dx_ref, dg_ref = refs
        _, vjp = jax.vjp(_rms, x_ref[...], g_ref[...])
        dx, dg = vjp(dh_ref[...])
        if has_res:
            dx = dx + dres_ref[...]
        dx_ref[...] = dx

        @pl.when(pl.program_id(0) == 0)
        def _():
            dg_ref[...] = jnp.zeros_like(dg_ref)

        dg_ref[...] += dg

    row = pl.BlockSpec((tr, D), lambda i: (i, 0))
    vec = pl.BlockSpec((1, D), lambda i: (0, 0))
    ins = [x, g, dh] + ([dres] if has_res else [])
    return pl.pallas_call(
        body, grid=(R // tr,),
        in_specs=[row, vec, row] + ([row] if has_res else []),
        out_specs=[row, vec],
        out_shape=[jax.ShapeDtypeStruct((R, D), F32), jax.ShapeDtypeStruct((1, D), F32)],
        compiler_params=_params(("arbitrary",)), name=name)(*ins)


def _shift_down(u, k):
    if k == 0:
        return u
    rows = lax.broadcasted_iota(jnp.int32, u.shape, 0)
    return jnp.where(rows >= k, pltpu.roll(u, k, axis=0), 0.0)


def _shift_up(u, k):
    if k == 0:
        return u
    n = u.shape[0]
    rows = lax.broadcasted_iota(jnp.int32, u.shape, 0)
    return jnp.where(rows < n - k, pltpu.roll(u, n - k, axis=0), 0.0)


def _conv_pre(u, w, b):
    pre = b
    for j in range(CONV_WIDTH):
        pre = pre + w[j:j + 1, :] * _shift_down(u, CONV_WIDTH - 1 - j)
    return pre


def _conv_fwd(proj, col0, ncols, conv_w, conv_b):
    S = proj.shape[0]
    cb0 = col0 // CONV_COLS

    def body(u_ref, w_ref, b_ref, o_ref):
        pre = _conv_pre(u_ref[...], w_ref[...], b_ref[...])
        o_ref[...] = pre * jax.nn.sigmoid(pre)

    return pl.pallas_call(
        body, grid=(ncols // CONV_COLS,),
        in_specs=[pl.BlockSpec((S, CONV_COLS), lambda j: (0, j + cb0)),
                  pl.BlockSpec((CONV_WIDTH, CONV_COLS), lambda j: (0, j)),
                  pl.BlockSpec((1, CONV_COLS), lambda j: (0, j))],
        out_specs=pl.BlockSpec((S, CONV_COLS), lambda j: (0, j)),
        out_shape=jax.ShapeDtypeStruct((S, ncols), F32),
        compiler_params=_params(("parallel",)), name="conv_fwd")(proj, conv_w, conv_b)


def _conv_bwd(proj, col0, ncols, conv_w, conv_b, douts, dproj):
    S = proj.shape[0]
    cb0 = col0 // CONV_COLS
    starts = [0]
    for d in douts:
        starts.append(starts[-1] + d.shape[1] // CONV_COLS)
    assert starts[-1] == ncols // CONV_COLS
    nd = len(douts)

    def body(u_ref, w_ref, b_ref, *rest):
        d_refs, (du_ref, dw_ref, db_ref) = rest[:nd], rest[nd + 1:]
        j = pl.program_id(0)
        dout = d_refs[-1][...]
        for i in range(nd - 2, -1, -1):
            dout = jnp.where(j < starts[i + 1], d_refs[i][...], dout)
        u = u_ref[...]
        w = w_ref[...]
        pre = _conv_pre(u, w, b_ref[...])
        s = jax.nn.sigmoid(pre)
        dpre = dout * (s * (1.0 + pre * (1.0 - s)))
        du = jnp.zeros_like(u)
        rows = []
        for j in range(CONV_WIDTH):
            k = CONV_WIDTH - 1 - j
            du = du + w[j:j + 1, :] * _shift_up(dpre, k)
            rows.append(jnp.sum(dpre * _shift_down(u, k), axis=0, keepdims=True))
        du_ref[...] = du.astype(du_ref.dtype)
        rows.append(jnp.zeros((8 - CONV_WIDTH, CONV_COLS), F32))
        dw_ref[...] = jnp.concatenate(rows, axis=0)
        db_ref[...] = jnp.sum(dpre, axis=0, keepdims=True)

    return pl.pallas_call(
        body, grid=(ncols // CONV_COLS,),
        in_specs=[pl.BlockSpec((S, CONV_COLS), lambda j: (0, j + cb0)),
                  pl.BlockSpec((CONV_WIDTH, CONV_COLS), lambda j: (0, j)),
                  pl.BlockSpec((1, CONV_COLS), lambda j: (0, j))]
        + [pl.BlockSpec((S, CONV_COLS), lambda j, lo=starts[i], hi=starts[i + 1]: (0, jnp.clip(j - lo, 0, hi - lo - 1)))
           for i in range(nd)] + [_ANY],
        out_specs=[pl.BlockSpec((S, CONV_COLS), lambda j: (0, j + cb0)),
                   pl.BlockSpec((8, CONV_COLS), lambda j: (0, j)),
                   pl.BlockSpec((1, CONV_COLS), lambda j: (0, j))],
        out_shape=[jax.ShapeDtypeStruct(dproj.shape, dproj.dtype),
                   jax.ShapeDtypeStruct((8, ncols), F32),
                   jax.ShapeDtypeStruct((1, ncols), F32)],
        input_output_aliases={3 + nd: 0},
        compiler_params=_params(("parallel",)), name="conv_bwd")(proj, conv_w, conv_b, *douts, dproj)


def _softplus(x):
    return jnp.maximum(x, 0.0) + jnp.log1p(jnp.exp(-jnp.abs(x)))


def _dot32(a, b, dims=(((1,), (0,)), ((), ()))):
    return lax.dot_general(a, b, dims, precision=HI, preferred_element_type=F32)


def _dotd(a, b, dims=(((1,), (0,)), ((), ()))):
    return lax.dot_general(a, b, dims, preferred_element_type=F32)


PAIRS_PER_GROUP = HEADS_PER_GROUP // 2


def _ssd_chunk(xs, Bm, Cm, z, dtr, dtb, alog, dsk, nw, h):
    L = Bm.shape[0]
    ri = lax.broadcasted_iota(jnp.int32, (L, L), 0)
    ci = lax.broadcasted_iota(jnp.int32, (L, L), 1)
    causal = ri >= ci
    tril = causal.astype(F32)
    first = _first_head(L)
    first1 = _first_head(1)
    CB = _dotd(Cm, Bm, _NT)
    gated, hnew = [], []
    ssq = jnp.zeros((L, 1), F32)
    for pp in range(len(xs)):
        dts, cums, tots, decay = [], [], [], []
        for a in range(2):
            r = 2 * pp + a
            dt = _softplus(dtr[r] + dtb[r])
            dA = dt * (-jnp.exp(alog[r]))
            acs = _dot32(tril, dA)
            cc = jnp.broadcast_to(acs, (L, L))
            decay.append(CB * jnp.exp(jnp.where(causal, cc - cc.T, -1e30)))
            dts.append(dt)
            cums.append(acs)
            tots.append(jnp.sum(dA, axis=0, keepdims=True))
        dt2 = jnp.where(first, dts[0], dts[1])
        acs2 = jnp.where(first, cums[0], cums[1])
        tot2 = jnp.where(first1, tots[0], tots[1])
        dsk2 = jnp.where(first1, dsk[2 * pp], dsk[2 * pp + 1])
        X = xs[pp] * dt2
        y = (jnp.where(first, _dotd(decay[0], X), _dotd(decay[1], X)) + jnp.exp(acs2) * _dotd(Cm, h[pp])
             + dsk2 * xs[pp])
        hnew.append(jnp.exp(tot2) * h[pp] + _dotd(Bm, X * jnp.exp(tot2 - acs2), _TN))
        g = y * (z[pp] * jax.nn.sigmoid(z[pp]))
        ssq = ssq + jnp.sum(g * g, axis=-1, keepdims=True)
        gated.append(g)
    rs = lax.rsqrt(ssq / (len(xs) * LANES) + EPS)
    return [g * rs * nw[pp] for pp, g in enumerate(gated)], hnew


def _ssd_args(xs_ref, b_ref, c_ref, z_ref, dt_ref, dtb_ref, al_ref, dsk_ref, nw_ref, h_ref):
    pairs = range(PAIRS_PER_GROUP)
    heads = range(HEADS_PER_GROUP)
    lanes = lambda ref, pp: ref[:, pp * LANES:(pp + 1) * LANES]
    return ([lanes(xs_ref, pp) for pp in pairs], b_ref[...], c_ref[...], [lanes(z_ref, pp) for pp in pairs],
            [dt_ref[r] for r in heads], [dtb_ref[r] for r in heads], [al_ref[r] for r in heads],
            [dsk_ref[r] for r in heads], [lanes(nw_ref, pp) for pp in pairs], [h_ref[pp] for pp in pairs])


def _ssd_specs(rev):
    H, N, L = HEADS_PER_GROUP, SSM_STATE, CHUNK
    gw = H * HEAD_DIM
    return dict(
        cols=lambda col0: pl.BlockSpec((L, gw), lambda g, c: (rev(c), col0 // gw + g)),
        bc=lambda first_block: pl.BlockSpec((L, N), lambda g, c: (rev(c), first_block + g)),
        dt=pl.BlockSpec((H, L, 1), lambda g, c: (g, rev(c), 0)),
        scal=pl.BlockSpec((H, 1, 1), lambda g, c: (g, 0, 0)),
        nw=pl.BlockSpec((1, gw), lambda g, c: (0, g)),
        hs=pl.BlockSpec((None, PAIRS_PER_GROUP, N, LANES), lambda g, c: (rev(c), g, 0, 0)),
        b_block=SSM_INNER // N,
    )


def _ssd_fwd(xbc, proj, dt_hm, dtb, alog, dsk, nw):
    S = xbc.shape[0]
    N, L = SSM_STATE, CHUNK
    nc = S // L
    sp = _ssd_specs(lambda c: c)

    def body(xs_ref, b_ref, c_ref, z_ref, dt_ref, dtb_ref, al_ref, dsk_ref, nw_ref, y_ref, hs_ref, h_ref):
        @pl.when(pl.program_id(1) == 0)
        def _():
            h_ref[...] = jnp.zeros_like(h_ref)

        hs_ref[...] = h_ref[...]
        out, hnew = _ssd_chunk(*_ssd_args(xs_ref, b_ref, c_ref, z_ref, dt_ref, dtb_ref, al_ref, dsk_ref, nw_ref, h_ref))
        for pp in range(PAIRS_PER_GROUP):
            y_ref[:, pp * LANES:(pp + 1) * LANES] = out[pp].astype(y_ref.dtype)
            h_ref[pp] = hnew[pp]

    return pl.pallas_call(
        body, grid=(SSM_GROUPS, nc),
        in_specs=[sp["cols"](0), sp["bc"](sp["b_block"]), sp["bc"](sp["b_block"] + SSM_GROUPS), sp["cols"](COL_Z),
                  sp["dt"], sp["scal"], sp["scal"], sp["scal"], sp["nw"]],
        out_specs=[sp["cols"](0), sp["hs"]],
        out_shape=[jax.ShapeDtypeStruct((S, MIX_WIDTH), BF16),
                   jax.ShapeDtypeStruct((nc, SSM_HEADS // 2, N, LANES), F32)],
        scratch_shapes=[pltpu.VMEM((PAIRS_PER_GROUP, N, LANES), F32)],
        compiler_params=_params(("parallel", "arbitrary")), name="ssd_fwd",
    )(xbc, xbc, xbc, proj, dt_hm, dtb, alog, dsk, nw)


def _ssd_bwd(xbc, proj, dt_hm, dtb, alog, dsk, nw, hs, dmixed, dproj):
    S = xbc.shape[0]
    N, L = SSM_STATE, CHUNK
    nc = S // L
    sp = _ssd_specs(lambda c: nc - 1 - c)

    def body(xs_ref, b_ref, c_ref, z_ref, dt_ref, dtb_ref, al_ref, dsk_ref, nw_ref, hs_ref, dy_ref, buf_ref,
             dxs_ref, dz_ref, db_ref, dc_ref, ddt_ref, ddtb_ref, dal_ref, ddsk_ref, dnw_ref, dh_ref):
        @pl.when(pl.program_id(1) == 0)
        def _():
            dh_ref[...] = jnp.zeros_like(dh_ref)
            ddtb_ref[...] = jnp.zeros_like(ddtb_ref)
            dal_ref[...] = jnp.zeros_like(dal_ref)
            ddsk_ref[...] = jnp.zeros_like(ddsk_ref)
            dnw_ref[...] = jnp.zeros_like(dnw_ref)

        pairs = range(PAIRS_PER_GROUP)
        lanes = lambda pp: slice(pp * LANES, (pp + 1) * LANES)
        _, vjp = jax.vjp(_ssd_chunk, *_ssd_args(xs_ref, b_ref, c_ref, z_ref, dt_ref, dtb_ref, al_ref, dsk_ref, nw_ref,
                                                hs_ref))
        dxs, dB, dC, dz, ddt, ddtb, dal, ddsk, dnw, dh = vjp(([dy_ref[:, lanes(pp)] for pp in pairs],
                                                              [dh_ref[pp] for pp in pairs]))
        db_ref[...] = dB
        dc_ref[...] = dC
        for pp in pairs:
            dxs_ref[:, lanes(pp)] = dxs[pp]
            dz_ref[:, lanes(pp)] = dz[pp].astype(dz_ref.dtype)
            dnw_ref[:, lanes(pp)] += dnw[pp]
            dh_ref[pp] = dh[pp]
        for r in range(HEADS_PER_GROUP):
            ddt_ref[r] = ddt[r]
            ddtb_ref[r] += ddtb[r]
            dal_ref[r] += dal[r]
            ddsk_ref[r] += ddsk[r]

    bc_out = pl.BlockSpec((L, N), lambda g, c: (nc - 1 - c, g))
    return pl.pallas_call(
        body, grid=(SSM_GROUPS, nc),
        in_specs=[sp["cols"](0), sp["bc"](sp["b_block"]), sp["bc"](sp["b_block"] + SSM_GROUPS), sp["cols"](COL_Z),
                  sp["dt"], sp["scal"], sp["scal"], sp["scal"], sp["nw"], sp["hs"], sp["cols"](0), _ANY],
        out_specs=[sp["cols"](0), sp["cols"](COL_Z), bc_out, bc_out, sp["dt"], sp["scal"], sp["scal"], sp["scal"],
                   sp["nw"]],
        input_output_aliases={11: 1},
        out_shape=[jax.ShapeDtypeStruct((S, SSM_INNER), F32), jax.ShapeDtypeStruct(dproj.shape, dproj.dtype),
                   jax.ShapeDtypeStruct((S, SSM_GROUPS * N), F32), jax.ShapeDtypeStruct((S, SSM_GROUPS * N), F32),
                   jax.ShapeDtypeStruct((SSM_HEADS, S, 1), F32),
                   jax.ShapeDtypeStruct((SSM_HEADS, 1, 1), F32), jax.ShapeDtypeStruct((SSM_HEADS, 1, 1), F32),
                   jax.ShapeDtypeStruct((SSM_HEADS, 1, 1), F32), jax.ShapeDtypeStruct((1, SSM_INNER), F32)],
        scratch_shapes=[pltpu.VMEM((PAIRS_PER_GROUP, N, LANES), F32)],
        compiler_params=_params(("parallel", "arbitrary")), name="ssd_bwd",
    )(xbc, xbc, xbc, proj, dt_hm, dtb, alog, dsk, nw, hs, dmixed, dproj)


ATTN_SCALE = HEAD_DIM ** -0.5
PREP_COLS = 512


def _first_head(rows):
    return lax.broadcasted_iota(jnp.int32, (rows, LANES), 1) < HEAD_DIM


def _pair_norm(x, g2, scale):
    first = _first_head(x.shape[0])
    sq = x * x
    ms0 = jnp.sum(jnp.where(first, sq, 0.0), axis=-1, keepdims=True) * (1.0 / HEAD_DIM)
    ms1 = jnp.sum(jnp.where(first, 0.0, sq), axis=-1, keepdims=True) * (1.0 / HEAD_DIM)
    r = jnp.where(first, lax.rsqrt(ms0 + EPS), lax.rsqrt(ms1 + EPS))
    return x * r * g2 * scale


def _qk_prep_fwd(proj, gq2, gk2):
    S = proj.shape[0]
    tq = _pick(S, (512, 256))

    def body(q_ref, k_ref, v_ref, gq_ref, gk_ref, qo_ref, ko_ref, vo_ref):
        for b in range(PREP_COLS // LANES):
            pair = slice(b * LANES, (b + 1) * LANES)
            qo_ref[:, pair] = _pair_norm(q_ref[:, pair], gq_ref[...], ATTN_SCALE).astype(BF16)
            ko_ref[:, pair] = _pair_norm(k_ref[:, pair], gk_ref[...], 1.0).astype(BF16)
        vo_ref[...] = v_ref[...].astype(BF16)

    col = lambda c0: pl.BlockSpec((tq, PREP_COLS), lambda h, i: (i, c0 // PREP_COLS + h))
    blk = pl.BlockSpec((tq, PREP_COLS), lambda h, i: (i, h))
    vec = pl.BlockSpec((1, LANES), lambda h, i: (0, 0))
    return pl.pallas_call(
        body, grid=(ATTN_WIDTH // PREP_COLS, S // tq), in_specs=[col(COL_Q), col(COL_K), col(COL_V), vec, vec],
        out_specs=[blk, blk, blk], out_shape=[jax.ShapeDtypeStruct((S, ATTN_WIDTH), BF16)] * 3,
        compiler_params=_params(("parallel", "parallel")), name="qk_prep_fwd")(proj, proj, proj, gq2, gk2)


def _pair_norm_bwd(proj, col0, g2, scale, dn, dproj, name):
    S = proj.shape[0]
    tq = _pick(S, (512, 256))

    def body(u_ref, g_ref, dn_ref, buf_ref, du_ref, dg_ref):
        @pl.when((pl.program_id(0) == 0) & (pl.program_id(1) == 0))
        def _():
            dg_ref[...] = jnp.zeros_like(dg_ref)

        for b in range(PREP_COLS // LANES):
            pair = slice(b * LANES, (b + 1) * LANES)
            _, vjp = jax.vjp(lambda u, g: _pair_norm(u, g, scale), u_ref[:, pair], g_ref[...])
            du, dg = vjp(dn_ref[:, pair])
            du_ref[:, pair] = du.astype(du_ref.dtype)
            dg_ref[...] += dg

    ublk = pl.BlockSpec((tq, PREP_COLS), lambda h, i: (i, col0 // PREP_COLS + h))
    blk = pl.BlockSpec((tq, PREP_COLS), lambda h, i: (i, h))
    vec = pl.BlockSpec((1, LANES), lambda h, i: (0, 0))
    return pl.pallas_call(
        body, grid=(ATTN_WIDTH // PREP_COLS, S // tq), in_specs=[ublk, vec, blk, _ANY], out_specs=[ublk, vec],
        out_shape=[jax.ShapeDtypeStruct(dproj.shape, dproj.dtype), jax.ShapeDtypeStruct((1, LANES), F32)],
        input_output_aliases={3: 0},
        compiler_params=_params(("arbitrary", "arbitrary")), name=name)(proj, g2, dn, dproj)


def _logf_cumsum_fwd(f_raw, f_bias):
    S, Hh = f_raw.shape
    L = CHUNK

    def body(f_ref, b_ref, o_ref, wide_ref):
        ri = lax.broadcasted_iota(jnp.int32, (L, L), 0)
        ci = lax.broadcasted_iota(jnp.int32, (L, L), 1)
        tril = (ri >= ci).astype(F32)
        carry = jnp.zeros((1, Hh), F32)
        for c in range(S // L):
            rows = slice(c * L, (c + 1) * L)
            lf = -_softplus(-(f_ref[rows, :] + b_ref[...]))
            cum = _dot32(tril, lf) + carry
            o_ref[rows, :] = cum
            for h in range(Hh):
                wide_ref[rows, h * HEAD_DIM:(h + 1) * HEAD_DIM] = jnp.broadcast_to(cum[:, h:h + 1], (L, HEAD_DIM))
            carry = cum[L - 1:L, :]

    return pl.pallas_call(
        body, out_shape=[jax.ShapeDtypeStruct((S, Hh), F32), jax.ShapeDtypeStruct((S, Hh * HEAD_DIM), F32)],
        name="logf_cumsum_fwd")(f_raw, f_bias)


def _logf_cumsum_bwd(f_raw, f_bias, dcum):
    S, Hh = f_raw.shape
    L = CHUNK

    def body(f_ref, b_ref, d_ref, df_ref, db_ref):
        ri = lax.broadcasted_iota(jnp.int32, (L, L), 0)
        ci = lax.broadcasted_iota(jnp.int32, (L, L), 1)
        triu = (ri <= ci).astype(F32)
        carry = jnp.zeros((1, Hh), F32)
        db = jnp.zeros((1, Hh), F32)
        for c in reversed(range(S // L)):
            suf = _dot32(triu, d_ref[c * L:(c + 1) * L, :]) + carry
            df = suf * jax.nn.sigmoid(-(f_ref[c * L:(c + 1) * L, :] + b_ref[...]))
            df_ref[c * L:(c + 1) * L, :] = df
            db = db + jnp.sum(df, axis=0, keepdims=True)
            carry = suf[0:1, :]
        db_ref[...] = db

    return pl.pallas_call(
        body, out_shape=[jax.ShapeDtypeStruct((S, Hh), F32), jax.ShapeDtypeStruct((1, Hh), F32)],
        name="logf_cumsum_bwd")(f_raw, f_bias, dcum)


_NT = (((1,), (1,)), ((), ()))
_TN = (((0,), (0,)), ((), ()))


def _mxu(a, b, dims=(((1,), (0,)), ((), ()))):
    return lax.dot_general(a, b, dims, preferred_element_type=F32)


def _flash_fwd(qs, kn, vb, cq, ck, mixed):
    S, W = qs.shape
    tq = tk = _pick(S, (512, 256))
    nmask = max(tq // tk, 1)

    def body(q_ref, k_ref, v_ref, cq_ref, ck_ref, buf_ref, o_ref, of_ref, lse_ref):
        i = pl.program_id(1)
        first = _first_head(tq)
        q2 = q_ref[...]
        zero = jnp.zeros_like(q2)
        qa = (jnp.where(first, q2, zero), jnp.where(first, zero, q2))
        cqa = (cq_ref[:, 0:1], cq_ref[:, HEAD_DIM:HEAD_DIM + 1])
        row0 = i * tq

        def step(j, carry, masked):
            ms, ls, acc, rem = carry
            off = pl.multiple_of(j * tk, tk)
            k = k_ref[pl.ds(off, tk), :]
            v = v_ref[pl.ds(off, tk), :]
            new_m, new_l, alphas, pvs, prs = [], [], [], [], []
            for a in range(2):
                s = _mxu(qa[a], k, _NT) + cqa[a] - ck_ref[a, :, pl.ds(off, tk)]
                if masked:
                    ri = lax.broadcasted_iota(jnp.int32, (tq, tk), 0) + row0
                    ci = lax.broadcasted_iota(jnp.int32, (tq, tk), 1) + off
                    s = jnp.where(ri >= ci, s, -1e30)
                m_new = jnp.maximum(ms[a], jnp.max(s, axis=-1, keepdims=True))
                alpha = jnp.exp(ms[a] - m_new)
                p = jnp.exp(s - m_new)
                new_l.append(alpha * ls[a] + jnp.sum(p, axis=-1, keepdims=True))
                new_m.append(m_new)
                alphas.append(alpha)
                p_hi = p.astype(BF16)
                pvs.append(_mxu(p_hi, v))
                prs.append(_mxu((p - p_hi.astype(F32)).astype(BF16), v))
            al = jnp.where(first, alphas[0], alphas[1])
            acc = al * acc + jnp.where(first, pvs[0], pvs[1])
            rem = al * rem + jnp.where(first, prs[0], prs[1])
            return tuple(new_m), tuple(new_l), acc, rem

        neg = jnp.full((tq, 1), -1e30, F32)
        z1 = jnp.zeros((tq, 1), F32)
        z2 = jnp.zeros((tq, LANES), F32)
        carry = ((neg, neg), (z1, z1), z2, z2)
        n_full = (i * tq) // tk
        carry = lax.fori_loop(0, n_full, lambda j, c: step(j, c, False), carry)
        for jj in range(nmask):
            carry = step(n_full + jj, carry, True)
        ms, ls, acc, rem = carry
        linv = jnp.where(first, 1.0 / ls[0], 1.0 / ls[1])
        o_ref[...] = (acc * linv).astype(o_ref.dtype)
        of_ref[...] = (acc + rem) * linv
        lse_ref[...] = jnp.where(first, ms[0] + jnp.log(ls[0]), ms[1] + jnp.log(ls[1]))

    qblk = pl.BlockSpec((tq, LANES), lambda h, i: (i, h))
    full = pl.BlockSpec((S, LANES), lambda h, i: (0, h))
    return pl.pallas_call(
        body, grid=(W // LANES, S // tq),
        in_specs=[qblk, full, full, qblk, pl.BlockSpec((2, 1, S), lambda h, i: (h, 0, 0)), _ANY],
        out_specs=[pl.BlockSpec((tq, LANES), lambda h, i: (i, SSM_INNER // LANES + h)), qblk, qblk],
        out_shape=[jax.ShapeDtypeStruct(mixed.shape, mixed.dtype), jax.ShapeDtypeStruct((S, W), F32),
                   jax.ShapeDtypeStruct((S, W), F32)],
        input_output_aliases={5: 0},
        compiler_params=_params(("parallel", "parallel")), name="flash_fwd")(qs, kn, vb, cq, ck, mixed)


def _flash_bwd(qs, kn, vb, cq, ck, o_fine, do, do_col0, lse):
    S, W = qs.shape
    tq = tk = _pick(S, (512, 256))
    nq = S // tq
    nmask = max(tk // tq, 1)

    def body(q_ref, k_ref, v_ref, cq_ref, ck_ref, of_ref, do_ref, lse_ref, dq_ref, dk_ref, dv_ref, dck_ref):
        j = pl.program_id(1)

        @pl.when(j == 0)
        def _():
            dq_ref[...] = jnp.zeros_like(dq_ref)

        firstk = _first_head(tk)
        firstq = _first_head(tq)
        k2 = k_ref[...]
        v2 = v_ref[...]
        zk = jnp.zeros_like(k2)
        ka = (jnp.where(firstk, k2, zk), jnp.where(firstk, zk, k2))
        va = (jnp.where(firstk, v2, zk), jnp.where(firstk, zk, v2))
        cka = (ck_ref[0], ck_ref[1])
        col0 = j * tk

        def step(i, carry, masked):
            dk, dv, dck0, dck1 = carry
            dcks = [dck0, dck1]
            off = pl.multiple_of(i * tq, tq)
            rows = pl.ds(off, tq)
            q2 = q_ref[rows, :]
            dob = do_ref[rows, :].astype(BF16)
            prod = dob.astype(F32) * of_ref[rows, :]
            dkp, dvp, dqp = [], [], []
            for a in range(2):
                lane = pl.ds(a * HEAD_DIM, 1)
                s = _mxu(q2, ka[a], _NT) + cq_ref[rows, lane] - cka[a]
                if masked:
                    ri = lax.broadcasted_iota(jnp.int32, (tq, tk), 0) + off
                    ci = lax.broadcasted_iota(jnp.int32, (tq, tk), 1) + col0
                    s = jnp.where(ri >= ci, s, -1e30)
                p = jnp.exp(s - lse_ref[rows, lane])
                dp = _mxu(dob, va[a], _NT)
                own = jnp.where(firstq, prod, 0.0) if a == 0 else jnp.where(firstq, 0.0, prod)
                ds = p * (dp - jnp.sum(own, axis=-1, keepdims=True))
                dsb = ds.astype(BF16)
                dvp.append(_mxu(p.astype(BF16), dob, _TN))
                dkp.append(_mxu(dsb, q2, _TN))
                dqp.append(_mxu(dsb, k2))
                dcks[a] = dcks[a] - jnp.sum(ds, axis=0, keepdims=True)
            dq_ref[rows, :] += jnp.where(firstq, dqp[0], dqp[1])
            dk = dk + jnp.where(firstk, dkp[0], dkp[1])
            dv = dv + jnp.where(firstk, dvp[0], dvp[1])
            return dk, dv, dcks[0], dcks[1]

        z2 = jnp.zeros((tk, LANES), F32)
        z1 = jnp.zeros((1, tk), F32)
        carry = (z2, z2, z1, z1)
        i0 = (j * tk) // tq
        for ii in range(nmask):
            carry = step(i0 + ii, carry, True)
        dk, dv, dck0, dck1 = lax.fori_loop(i0 + nmask, nq, lambda i, c: step(i, c, False), carry)
        dk_ref[...] = dk
        dv_ref[...] = dv.astype(dv_ref.dtype)
        dck_ref[0] = dck0
        dck_ref[1] = dck1

    kblk = pl.BlockSpec((tk, LANES), lambda h, j: (j, h))
    full = pl.BlockSpec((S, LANES), lambda h, j: (0, h))
    dofull = pl.BlockSpec((S, LANES), lambda h, j: (0, do_col0 // LANES + h))
    rowt = pl.BlockSpec((2, 1, tk), lambda h, j: (h, 0, j))
    dvblk = pl.BlockSpec((tk, LANES), lambda h, j: (j, COL_V // LANES + h))
    return pl.pallas_call(
        body, grid=(W // LANES, S // tk),
        in_specs=[full, kblk, kblk, full, rowt, full, dofull, full],
        out_specs=[full, kblk, dvblk, rowt],
        out_shape=[jax.ShapeDtypeStruct((S, W), F32), jax.ShapeDtypeStruct((S, W), F32),
                   jax.ShapeDtypeStruct((S, IN_COLS_PAD), BF16), jax.ShapeDtypeStruct((2 * (W // LANES), 1, S), F32)],
        compiler_params=_params(("parallel", "arbitrary")), name="flash_bwd")(qs, kn, vb, cq, ck, o_fine, do, lse)


XATTN_SCALE = XATTN_DIM ** -0.5


def _xq_norm(q, g):
    return _rms(q, g) * XATTN_SCALE


def _xattn_fwd(xq, kv, gq, gk):
    S = xq.shape[0]
    Mm = kv.shape[0]
    Dh = XATTN_DIM
    tq = _pick(S, (512, 256))

    def body(q_ref, k_ref, v_ref, gq_ref, gk_ref, o_ref):
        qn = _xq_norm(q_ref[...], gq_ref[...]).astype(BF16)
        kn = _rms(k_ref[...], gk_ref[...]).astype(BF16)
        s = _mxu(qn, kn, _NT)
        m = jnp.max(s, axis=-1, keepdims=True)
        p = jnp.exp(s - m)
        l = jnp.sum(p, axis=-1, keepdims=True)
        o_ref[...] = (_mxu(p.astype(BF16), v_ref[...].astype(BF16)) / l).astype(o_ref.dtype)

    vec = pl.BlockSpec((1, Dh), lambda h, i: (0, 0))
    return pl.pallas_call(
        body, grid=(XATTN_HEADS, S // tq),
        in_specs=[pl.BlockSpec((tq, Dh), lambda h, i: (i, h)), pl.BlockSpec((Mm, Dh), lambda h, i: (0, h)),
                  pl.BlockSpec((Mm, Dh), lambda h, i: (0, XATTN_HEADS + h)), vec, vec],
        out_specs=pl.BlockSpec((tq, Dh), lambda h, i: (i, h)),
        out_shape=jax.ShapeDtypeStruct((S, XATTN_HEADS * Dh), BF16),
        compiler_params=_params(("parallel", "parallel")), name="xattn_fwd")(xq, kv, kv, gq, gk)


def _xattn_bwd(xq, kv, gq, gk, do):
    S = xq.shape[0]
    Mm = kv.shape[0]
    Dh = XATTN_DIM
    tq = _pick(S, (512, 256))
    nq = S // tq

    def body(q_ref, k_ref, v_ref, gq_ref, gk_ref, do_ref, dq_ref, dk_ref, dv_ref, dgq_ref, dgk_ref, dkn_acc, dv_acc):
        h = pl.program_id(0)
        i = pl.program_id(1)

        @pl.when((h == 0) & (i == 0))
        def _():
            dgq_ref[...] = jnp.zeros_like(dgq_ref)
            dgk_ref[...] = jnp.zeros_like(dgk_ref)

        @pl.when(i == 0)
        def _():
            dkn_acc[...] = jnp.zeros_like(dkn_acc)
            dv_acc[...] = jnp.zeros_like(dv_acc)

        qn32, vq = jax.vjp(_xq_norm, q_ref[...], gq_ref[...])
        kn32, vk = jax.vjp(_rms, k_ref[...], gk_ref[...])
        qn = qn32.astype(BF16)
        kn = kn32.astype(BF16)
        vb = v_ref[...].astype(BF16)
        s = _mxu(qn, kn, _NT)
        m = jnp.max(s, axis=-1, keepdims=True)
        p = jnp.exp(s - m)
        p = p / jnp.sum(p, axis=-1, keepdims=True)
        dob = do_ref[...].astype(BF16)
        dp = _mxu(dob, vb, _NT)
        delta = jnp.sum(p * dp, axis=-1, keepdims=True)
        ds = (p * (dp - delta)).astype(BF16)
        dv_acc[...] += _mxu(p.astype(BF16), dob, _TN)
        dkn_acc[...] += _mxu(ds, qn, _TN)
        dq, dgq = vq(_mxu(ds, kn))
        dq_ref[...] = dq.astype(dq_ref.dtype)
        dgq_ref[...] += dgq

        @pl.when(i == nq - 1)
        def _():
            dk, dgk = vk(dkn_acc[...])
            dk_ref[...] = dk.astype(dk_ref.dtype)
            dv_ref[...] = dv_acc[...].astype(dv_ref.dtype)
            dgk_ref[...] += dgk

    vec = pl.BlockSpec((1, Dh), lambda h, i: (0, 0))
    qblk = pl.BlockSpec((tq, Dh), lambda h, i: (i, h))
    kblk = pl.BlockSpec((Mm, Dh), lambda h, i: (0, h))
    vblk = pl.BlockSpec((Mm, Dh), lambda h, i: (0, XATTN_HEADS + h))
    return pl.pallas_call(
        body, grid=(XATTN_HEADS, nq),
        in_specs=[qblk, kblk, vblk, vec, vec, qblk],
        out_specs=[qblk, kblk, kblk, vec, vec],
        out_shape=[jax.ShapeDtypeStruct((S, XATTN_HEADS * Dh), BF16),
                   jax.ShapeDtypeStruct((Mm, XATTN_HEADS * Dh), BF16),
                   jax.ShapeDtypeStruct((Mm, XATTN_HEADS * Dh), BF16),
                   jax.ShapeDtypeStruct((1, Dh), F32), jax.ShapeDtypeStruct((1, Dh), F32)],
        scratch_shapes=[pltpu.VMEM((Mm, Dh), F32), pltpu.VMEM((Mm, Dh), F32)],
        compiler_params=_params(("arbitrary", "arbitrary")), name="xattn_bwd")(xq, kv, kv, gq, gk, do)


def _loss_head(y, target):
    S, D = y.shape
    tr = _pick(S, (512, 256))

    def body(y_ref, t_ref, dy_ref, loss_ref):
        @pl.when(pl.program_id(0) == 0)
        def _():
            loss_ref[...] = jnp.zeros_like(loss_ref)

        err = y_ref[...] - t_ref[...]
        dy_ref[...] = err * (1.0 / D)
        loss_ref[...] += jnp.sum(err * err) * (0.5 / D)

    row = pl.BlockSpec((tr, D), lambda i: (i, 0))
    return pl.pallas_call(
        body, grid=(S // tr,), in_specs=[row, row],
        out_specs=[row, pl.BlockSpec((1, LANES), lambda i: (0, 0))],
        out_shape=[jax.ShapeDtypeStruct((S, D), F32), jax.ShapeDtypeStruct((1, LANES), F32)],
        compiler_params=_params(("arbitrary",)), name="loss_head")(y, target)


def _row_tile(R, C):
    for tr in (1024, 512, 256, 128, 64, 32, 16, 8):
        if R % tr == 0 and tr * C * 4 <= (1 << 20):
            return tr
    return R


def _chip_sum(own, from_chips, name):
    R, C = own.shape
    tr = _row_tile(R, C)

    def body(own_ref, a_ref, b_ref, c_ref, o_ref):
        o_ref[...] = ((own_ref[...].astype(F32) + a_ref[...].astype(F32)) + b_ref[...].astype(F32)) + c_ref[...].astype(F32)

    blk = pl.BlockSpec((tr, C), lambda i: (i, 0))
    slab = lambda s: pl.BlockSpec((None, tr, C), lambda i: (s, i, 0))
    return pl.pallas_call(
        body, grid=(R // tr,), in_specs=[blk, slab(0), slab(1), slab(2)], out_specs=blk,
        out_shape=jax.ShapeDtypeStruct((R, C), F32),
        compiler_params=_params(("parallel",)), name=name)(own, from_chips, from_chips, from_chips)


def _adamw(w, g_mine, g_sibling, m, v, name):
    R, C = w.shape
    tr = _row_tile(R, C)
    c1 = 1.0 - ADAM_B1 ** ADAM_STEP
    c2 = 1.0 - ADAM_B2 ** ADAM_STEP

    def body(w_ref, ga_ref, gb_ref, m_ref, v_ref, g_ref, d_ref, mo_ref, vo_ref):
        g_t = ga_ref[...] + gb_ref[...]
        m_new = ADAM_B1 * m_ref[...] + (1.0 - ADAM_B1) * g_t
        v_new = ADAM_B2 * v_ref[...] + (1.0 - ADAM_B2) * (g_t * g_t)
        g_ref[...] = g_t
        d_ref[...] = -ADAM_LR * ((m_new / c1) / (jnp.sqrt(v_new / c2) + ADAM_EPS) + ADAM_WD * w_ref[...])
        mo_ref[...] = m_new
        vo_ref[...] = v_new

    blk = pl.BlockSpec((tr, C), lambda i: (i, 0))
    return pl.pallas_call(
        body, grid=(R // tr,), in_specs=[blk] * 5, out_specs=[blk] * 4,
        out_shape=[jax.ShapeDtypeStruct((R, C), F32)] * 4,
        compiler_params=_params(("parallel",)), name=name)(w, g_mine, g_sibling, m, v)


SSM_INNER = SSM_HEADS * HEAD_DIM
CONV_DIM = SSM_INNER + 2 * SSM_GROUPS * SSM_STATE
ATTN_WIDTH = ATTN_HEADS * HEAD_DIM
MIX_WIDTH = SSM_INNER + ATTN_WIDTH
COL_Z = 0
COL_XBC = COL_Z + SSM_INNER
COL_Q = COL_XBC + CONV_DIM
COL_K = COL_Q + ATTN_WIDTH
COL_V = COL_K + ATTN_WIDTH
COL_DT = COL_V + ATTN_WIDTH
COL_F = COL_DT + SSM_HEADS
IN_COLS = COL_F + ATTN_HEADS
IN_COLS_PAD = -(-IN_COLS // LANES) * LANES
REF_COL_DT = COL_Q
SHARD_COLS = IN_COLS // N_CHIPS
_COL_RANGES = ((0, REF_COL_DT, 0), (REF_COL_DT + SSM_HEADS, COL_F, COL_Q), (REF_COL_DT, REF_COL_DT + SSM_HEADS, COL_DT),
               (COL_F, IN_COLS, COL_F))


def _w_in_from_shards(g):
    parts = []
    for lo, hi, _ in _COL_RANGES:
        while lo < hi:
            j = lo // SHARD_COLS
            end = min(hi, (j + 1) * SHARD_COLS)
            parts.append(g[j][:, lo - j * SHARD_COLS:end - j * SHARD_COLS])
            lo = end
    parts.append(jnp.zeros((g.shape[1], IN_COLS_PAD - IN_COLS), g.dtype))
    return jnp.concatenate(parts, axis=1)


def _w_in_to_shards(w):
    shards = []
    for j in range(N_CHIPS):
        parts = []
        for lo, hi, here in sorted(_COL_RANGES):
            a, b = max(lo, j * SHARD_COLS), min(hi, (j + 1) * SHARD_COLS)
            if a < b:
                parts.append(w[:, here + a - lo:here + b - lo])
        shards.append(jnp.concatenate(parts, axis=1))
    return jnp.stack(shards)


def _add_residual(acc, res):
    return (res + acc,)


def _relu2(acc):
    r = jnp.maximum(acc, 0.0)
    return acc, r * r


def _relu2_bwd(acc, a):
    return (acc * (2.0 * jnp.maximum(a, 0.0)),)


def _layer_fwd_bwd(x, mem, target, w_in, p, late_weights, send_late_grads, send_w_in_grad):
    S = x.shape[0]
    hd3 = lambda a: a.reshape(SSM_HEADS, 1, 1)

    h1 = _rmsnorm_fwd(x, p["g_mix"], "norm_mix")
    proj = _mm(h1, w_in, "nn", "in_proj")
    xbc = _conv_fwd(proj, COL_XBC, CONV_DIM, p["conv_w"], p["conv_b"])
    dt_hm = proj[:, COL_DT:COL_DT + SSM_HEADS].T[:, :, None]
    ssd_par = (hd3(p["dt_bias"]), hd3(p["a_log"]), hd3(p["d_skip"]), p["ssm_norm_w"])
    mixed, hs = _ssd_fwd(xbc, proj, dt_hm, *ssd_par)
    f_raw = proj[:, COL_F:COL_F + ATTN_HEADS]
    gq2 = jnp.tile(p["g_q"], (1, 2))
    gk2 = jnp.tile(p["g_k"], (1, 2))
    qs, kn, vb = _qk_prep_fwd(proj, gq2, gk2)
    cum, cq = _logf_cumsum_fwd(f_raw, p["f_bias"])
    ck = cum.T[:, None, :]
    mixed, o_fine, lse = _flash_fwd(qs, kn, vb, cq, ck, mixed)
    W = late_weights((mixed,))
    x1 = _mm(mixed, W["w_out"], "nn", "out_proj", epilogue=_add_residual, extras=(x,))
    h2 = _rmsnorm_fwd(x1, p["g_xattn"], "norm_xattn")
    mem_n = _rmsnorm_fwd(mem, p["g_mem"], "norm_mem")
    xq = _mm(h2, W["xq_w"], "nn", "xq_proj")
    kv = _mm(mem_n, W["xkv_w"], "nn", "xkv_proj", b_chunks=N_CHIPS)
    xo = _xattn_fwd(xq, kv, p["xg_q"], p["xg_k"])
    x2 = _mm(xo, W["xo_w"], "nn", "xo_proj", epilogue=_add_residual, extras=(x1,))
    h3 = _rmsnorm_fwd(x2, p["g_mlp"], "norm_mlp")
    a, act = _mm(h3, W["w_up"], "nn", "mlp_up", out_dtypes=(F32, BF16), epilogue=_relu2, b_chunks=N_CHIPS)
    x3 = _mm(act, W["w_down"], "nn", "mlp_down", epilogue=_add_residual, extras=(x2,))
    dy, loss_row = _loss_head(x3, target)

    gW, gp = {}, {}
    da = _mm(dy, W["w_down"], "nt", "d_act", out_dtypes=(BF16,), epilogue=_relu2_bwd, extras=(a,))
    gW["w_down"] = _mm(act, dy, "tn", "g_w_down", out_dtypes=(BF16,))
    gW["w_up"] = _mm(h3, da, "tn", "g_w_up", out_dtypes=(BF16,), out_chunks=N_CHIPS)
    dh3 = _mm(da, W["w_up"], "nt", "d_h3", b_chunks=N_CHIPS)
    dx2, gp["g_mlp"] = _rmsnorm_bwd(x2, p["g_mlp"], dh3, dy, "norm_mlp_bwd")
    dxo = _mm(dx2, W["xo_w"], "nt", "d_xo", out_dtypes=(BF16,))
    gW["xo_w"] = _mm(xo, dx2, "tn", "g_xo_w", out_dtypes=(BF16,))
    dxq, dk_x, dv_x, gp["xg_q"], gp["xg_k"] = _xattn_bwd(xq, kv, p["xg_q"], p["xg_k"], dxo)
    dkv = jnp.concatenate([dk_x, dv_x], axis=-1)
    gW["xq_w"] = _mm(h2, dxq, "tn", "g_xq_w", out_dtypes=(BF16,))
    dh2 = _mm(dxq, W["xq_w"], "nt", "d_h2")
    gW["xkv_w"] = _mm(mem_n, dkv, "tn", "g_xkv_w", out_dtypes=(BF16,), out_chunks=N_CHIPS)
    dmem_n = _mm(dkv, W["xkv_w"], "nt", "d_mem_n", b_chunks=N_CHIPS)
    _, gp["g_mem"] = _rmsnorm_bwd(mem, p["g_mem"], dmem_n, None, "norm_mem_bwd")
    dx1, gp["g_xattn"] = _rmsnorm_bwd(x1, p["g_xattn"], dh2, dx2, "norm_xattn_bwd")
    dmixed = _mm(dx1, W["w_out"], "nt", "d_mixed")
    gW["w_out"] = _mm(mixed, dx1, "tn", "g_w_out", out_dtypes=(BF16,))
    token = send_late_grads(gW)
    dqs, dkn, dproj, dck = _flash_bwd(qs, kn, vb, cq, ck + token[:1, :1], o_fine, dmixed, SSM_INNER, lse)
    dproj, dgq2 = _pair_norm_bwd(proj, COL_Q, gq2, ATTN_SCALE, dqs, dproj, "q_norm_bwd")
    dproj, dgk2 = _pair_norm_bwd(proj, COL_K, gk2, 1.0, dkn, dproj, "k_norm_bwd")
    gp["g_q"] = dgq2[:, :HEAD_DIM] + dgq2[:, HEAD_DIM:]
    gp["g_k"] = dgk2[:, :HEAD_DIM] + dgk2[:, HEAD_DIM:]
    df, gp["f_bias"] = _logf_cumsum_bwd(f_raw, p["f_bias"], dck[:, 0, :].T)
    dxs, dproj, dB, dC, ddt, ddtb, dalog, ddsk, gp["ssm_norm_w"] = _ssd_bwd(xbc, proj, dt_hm, *ssd_par, hs, dmixed, dproj)
    gp["dt_bias"] = ddtb.reshape(1, SSM_HEADS)
    gp["a_log"] = dalog.reshape(1, SSM_HEADS)
    gp["d_skip"] = ddsk.reshape(1, SSM_HEADS)
    dproj, dconv_w, gp["conv_b"] = _conv_bwd(proj, COL_XBC, CONV_DIM, p["conv_w"], p["conv_b"], (dxs, dB, dC), dproj)
    gp["conv_w"] = dconv_w[:CONV_WIDTH]
    tail = jnp.concatenate([ddt[:, :, 0].T, df, jnp.zeros((S, IN_COLS_PAD - IN_COLS), F32)], axis=-1).astype(BF16)
    dproj = lax.dynamic_update_slice(dproj, tail, (0, COL_DT))
    token = send_w_in_grad(_mm(h1, dproj, "tn", "g_w_in", out_dtypes=(BF16,)))
    dh1 = _mm(dproj, w_in, "nt", "d_h1")
    dx, gp["g_mix"] = _rmsnorm_bwd(x, p["g_mix"] + token[:1, :1], dh1, dx1, "norm_mix_bwd")
    return loss_row, dx, gp


_ANY = pl.BlockSpec(memory_space=pl.ANY)


def _place():
    x, y, c = lax.axis_index("x"), lax.axis_index("y"), lax.axis_index("c")
    chips = [(1 - x, y), (x, 1 - y), (1 - x, 1 - y)]
    return x, y, c, chips


def _chip_index(px, py):
    return 2 * px + py


def _all_gather_chips(split, whole):
    ns, nw = len(split), len(whole)
    n = ns + nw

    def body(*refs):
        ins, outs = refs[:n], refs[n:2 * n]
        send_ici, recv_ici, send_d2d, recv_d2d = refs[2 * n:]
        x, y, c, chips = _place()
        me = _chip_index(x, y)
        sib = (x, y, 1 - c)

        def ici(k, j, src, dst):
            return pltpu.make_async_remote_copy(src_ref=src, dst_ref=dst, send_sem=send_ici.at[3 * k + j],
                                                recv_sem=recv_ici.at[3 * k + j], device_id=(*chips[j], c),
                                                device_id_type=MESH)

        def d2d(k, j, piece):
            return pltpu.make_async_remote_copy(src_ref=piece, dst_ref=piece, send_sem=send_d2d.at[3 * k + j],
                                                recv_sem=recv_d2d.at[3 * k + j], device_id=sib, device_id_type=MESH)

        sends = []
        for k in range(n):
            for j in range(3):
                if k < ns:
                    sends.append(ici(k, j, ins[k].at[c], outs[k].at[me, c]))
                else:
                    sends.append(ici(k, j, ins[k], outs[k].at[me]))
                sends[-1].start()
        passed = []
        for k in range(n):
            for j in range(3):
                src_chip = _chip_index(*chips[j])
                if k < ns:
                    ici(k, j, ins[k].at[c], outs[k].at[src_chip, c]).wait_recv()
                    passed.append(d2d(k, j, outs[k].at[src_chip, c]))
                    passed[-1].start()
                else:
                    ici(k, j, ins[k], outs[k].at[src_chip]).wait_recv()
        for k in range(ns):
            for j in range(3):
                d2d(k, j, outs[k].at[_chip_index(*chips[j]), 1 - c]).wait_recv()
        for cp in sends + passed:
            cp.wait_send()

    arrs = list(split) + list(whole)
    return pl.pallas_call(
        body, in_specs=[_ANY] * n, out_specs=[_ANY] * n,
        out_shape=[jax.ShapeDtypeStruct((N_CHIPS,) + a.shape, a.dtype) for a in arrs],
        scratch_shapes=[pltpu.SemaphoreType.DMA((3 * n,)), pltpu.SemaphoreType.DMA((3 * n,)),
                        pltpu.SemaphoreType.DMA((3 * ns,)), pltpu.SemaphoreType.DMA((3 * ns,))],
        name="all_gather_chips")(*arrs)


def _sibling_swap(arrs, name):
    n = len(arrs)

    def body(*refs):
        ins, outs = refs[:n], refs[n:2 * n]
        send_sem, recv_sem = refs[2 * n:]
        x, y, c, _ = _place()
        copies = [pltpu.make_async_remote_copy(src_ref=ins[k], dst_ref=outs[k], send_sem=send_sem.at[k],
                                               recv_sem=recv_sem.at[k], device_id=(x, y, 1 - c), device_id_type=MESH)
                  for k in range(n)]
        for q in copies:
            q.start()
        for q in copies:
            q.wait()

    return pl.pallas_call(
        body, in_specs=[_ANY] * n, out_specs=[_ANY] * n,
        out_shape=[jax.ShapeDtypeStruct(a.shape, a.dtype) for a in arrs],
        scratch_shapes=[pltpu.SemaphoreType.DMA((n,)), pltpu.SemaphoreType.DMA((n,))],
        name=name)(*arrs)


_HBM = pl.BlockSpec(memory_space=pltpu.HBM)
_SEM = pl.BlockSpec(memory_space=pltpu.SEMAPHORE)
_SPLIT_EFFECT = pltpu.SideEffectType.DATAFLOW_SIDE_EFFECTING


class _Split(NamedTuple):
    send_sems: jax.Array
    recv_sems: jax.Array
    sources: tuple
    lands: tuple
    token: jax.Array


def _split_copies(kind, srcs, lands, send_sems, recv_sems):
    x, y, c, chips = _place()
    me = _chip_index(x, y)
    copies = []
    for k in range(len(srcs)):
        for j in range(3):
            if kind == "gather":
                src, dst = srcs[k], lands[k].at[me]
            else:
                src, dst = srcs[k].at[_chip_index(*chips[j])], lands[k].at[j]
            copies.append(pltpu.make_async_remote_copy(
                src_ref=src, dst_ref=dst, send_sem=send_sems.at[3 * k + j], recv_sem=recv_sems.at[3 * k + j],
                device_id=(*chips[j], c), device_id_type=MESH))
    return copies


def _split_start(name, sources, kind, after):
    n = len(sources)
    if kind == "gather":
        lands = [lax.empty((N_CHIPS,) + s.shape, s.dtype) for s in sources]
    else:
        lands = [lax.empty((3,) + s.shape[1:], s.dtype) for s in sources]
    deps = [] if after is None else [after]

    def body(*refs):
        srcs, lnds = refs[:n], refs[n:2 * n]
        send_sems, recv_sems = refs[2 * n + len(deps)], refs[2 * n + len(deps) + 1]
        for cp in _split_copies(kind, srcs, lnds, send_sems, recv_sems):
            cp.start()
        refs[-1][...] = jnp.zeros_like(refs[-1])

    hbm = lambda a: pltpu.with_memory_space_constraint(a, pltpu.HBM)
    outs = pl.pallas_call(
        body, name=name,
        in_specs=[_HBM] * (2 * n) + [_ANY] * len(deps),
        out_specs=[_SEM, _SEM] + [_HBM] * (2 * n) + [pl.BlockSpec(memory_space=pltpu.VMEM)],
        out_shape=[pltpu.SemaphoreType.DMA((3 * n,)), pltpu.SemaphoreType.DMA((3 * n,))]
        + [pltpu.HBM(a.shape, a.dtype) for a in list(sources) + lands] + [jax.ShapeDtypeStruct((8, LANES), F32)],
        input_output_aliases={k: 2 + k for k in range(2 * n)},
        compiler_params=pltpu.CompilerParams(has_side_effects=_SPLIT_EFFECT),
    )(*[hbm(s) for s in sources], *[hbm(l) for l in lands], *deps)
    return _Split(outs[0], outs[1], tuple(outs[2:2 + n]), tuple(outs[2 + n:2 + 2 * n]), outs[-1])


def _split_wait(name, h, kind, after):
    n = len(h.sources)

    def body(*refs):
        srcs, lnds = refs[:n], refs[n:2 * n]
        for cp in _split_copies(kind, srcs, lnds, refs[2 * n], refs[2 * n + 1]):
            cp.wait_send()
            cp.wait_recv()

    outs = pl.pallas_call(
        body, name=name,
        in_specs=[_HBM] * (2 * n) + [_SEM, _SEM] + [_ANY] * len(after),
        out_specs=[_HBM] * (2 * n),
        out_shape=[pltpu.HBM(a.shape, a.dtype) for a in h.sources + h.lands],
        input_output_aliases={k: k for k in range(2 * n)},
        compiler_params=pltpu.CompilerParams(has_side_effects=_SPLIT_EFFECT),
    )(*h.sources, *h.lands, h.send_sems, h.recv_sems, *after)
    return outs[:n], outs[n:]


def _all_reduce_small(vec, after):
    R, C = vec.shape

    def body(v_ref, after_ref, o_ref, buf, send_sem, recv_sem):
        x, y, c = lax.axis_index("x"), lax.axis_index("y"), lax.axis_index("c")
        me = 4 * x + 2 * y + c
        buf[me] = v_ref[...]
        copies = []
        for r in range(1, N_DEV):
            fx, fy, fc = (r >> 2) & 1, (r >> 1) & 1, r & 1
            peer = (x ^ fx, y ^ fy, c ^ fc)
            copies.append(pltpu.make_async_remote_copy(src_ref=v_ref, dst_ref=buf.at[me], send_sem=send_sem.at[r - 1],
                                                       recv_sem=recv_sem.at[r - 1], device_id=peer, device_id_type=MESH))
        for q in copies:
            q.start()
        for r in range(1, N_DEV):
            fx, fy, fc = (r >> 2) & 1, (r >> 1) & 1, r & 1
            src = 4 * (x ^ fx) + 2 * (y ^ fy) + (c ^ fc)
            pltpu.make_async_remote_copy(src_ref=v_ref, dst_ref=buf.at[src], send_sem=send_sem.at[r - 1],
                                         recv_sem=recv_sem.at[r - 1], device_id=(x, y, c), device_id_type=MESH).wait_recv()
        acc = buf[0]
        for d in range(1, N_DEV):
            acc = acc + buf[d]
        o_ref[...] = acc
        for q in copies:
            q.wait_send()

    vm = pl.BlockSpec(memory_space=pltpu.VMEM)
    return pl.pallas_call(
        body, in_specs=[vm, _ANY], out_specs=vm, out_shape=jax.ShapeDtypeStruct((R, C), F32),
        scratch_shapes=[pltpu.VMEM((N_DEV, R, C), F32), pltpu.SemaphoreType.DMA((N_DEV - 1,)),
                        pltpu.SemaphoreType.DMA((N_DEV - 1,))],
        name="all_reduce_small")(vec, after)


_INPUTS = ["x", "mem", "g_mix", "w_in", "conv_w", "conv_b", "dt_bias", "a_log", "d_skip", "ssm_norm_w", "g_q", "g_k",
           "f_bias", "w_out", "g_xattn", "g_mem", "xq_w", "xkv_w", "xg_q", "xg_k", "xo_w", "g_mlp", "w_up", "w_down"]
_WEIGHTS = _INPUTS[2:]
_BIG = ["w_in", "w_out", "xq_w", "xkv_w", "xo_w", "w_up", "w_down"]
_LATE = _BIG[1:]
_COL_SHARDED = ["w_in", "xkv_w", "w_up"]
_SMALL = [n for n in _WEIGHTS if n not in _BIG]


def _pack_rows(arrs, width):
    starts, r = [], 0
    for a in arrs:
        starts.append(r)
        r += a.shape[0]
    out = jnp.concatenate([jnp.pad(a, ((0, 0), (0, width - a.shape[1]))) for a in arrs], axis=0)
    return jnp.pad(out, ((0, -r % 8), (0, 0))), starts


def _adamw_small(summed, starts, ws, ms, vs, conv_w_index):
    n = len(ws)
    c1 = 1.0 - ADAM_B1 ** ADAM_STEP
    c2 = 1.0 - ADAM_B2 ** ADAM_STEP

    def body(s_ref, *refs):
        w_refs, m_refs, v_refs = refs[:n], refs[n:2 * n], refs[2 * n:3 * n]
        outs = refs[3 * n:]
        chip = _chip_index(lax.axis_index("x"), lax.axis_index("y"))
        for k in range(n):
            rows, cols = w_refs[k].shape
            if k == conv_w_index:
                g = s_ref[starts[k]:starts[k] + rows, pl.ds(pl.multiple_of(chip * cols, LANES), cols)]
            else:
                g = s_ref[starts[k]:starts[k] + rows, 0:cols]
            m_new = ADAM_B1 * m_refs[k][...] + (1.0 - ADAM_B1) * g
            v_new = ADAM_B2 * v_refs[k][...] + (1.0 - ADAM_B2) * (g * g)
            outs[4 * k][...] = g
            outs[4 * k + 1][...] = -ADAM_LR * ((m_new / c1) / (jnp.sqrt(v_new / c2) + ADAM_EPS) + ADAM_WD * w_refs[k][...])
            outs[4 * k + 2][...] = m_new
            outs[4 * k + 3][...] = v_new

    vm = pl.BlockSpec(memory_space=pltpu.VMEM)
    outs = pl.pallas_call(
        body, in_specs=[vm] * (1 + 3 * n), out_specs=[vm] * (4 * n),
        out_shape=[jax.ShapeDtypeStruct(a.shape, F32) for a in ws for _ in range(4)],
        name="adamw_small")(summed, *ws, *ms, *vs)
    return [outs[4 * k:4 * k + 4] for k in range(n)]


def kernel(x, mem, g_mix, w_in, conv_w, conv_b, dt_bias, a_log, d_skip, ssm_norm_w, g_q, g_k, f_bias, w_out, g_xattn, g_mem, xq_w, xkv_w, xg_q, xg_k, xo_w, g_mlp, w_up, w_down, loss_target, m_g_mix, m_w_in, m_conv_w, m_conv_b, m_dt_bias, m_a_log, m_d_skip, m_ssm_norm_w, m_g_q, m_g_k, m_f_bias, m_w_out, m_g_xattn, m_g_mem, m_xq_w, m_xkv_w, m_xg_q, m_xg_k, m_xo_w, m_g_mlp, m_w_up, m_w_down, v_g_mix, v_w_in, v_conv_w, v_conv_b, v_dt_bias, v_a_log, v_d_skip, v_ssm_norm_w, v_g_q, v_g_k, v_f_bias, v_w_out, v_g_xattn, v_g_mem, v_xq_w, v_xkv_w, v_xg_q, v_xg_k, v_xo_w, v_g_mlp, v_w_up, v_w_down):
    args = (x, mem, g_mix, w_in, conv_w, conv_b, dt_bias, a_log, d_skip, ssm_norm_w, g_q, g_k, f_bias, w_out, g_xattn,
            g_mem, xq_w, xkv_w, xg_q, xg_k, xo_w, g_mlp, w_up, w_down)
    w = dict(zip(_INPUTS, args))
    mom1 = dict(zip(_WEIGHTS, (m_g_mix, m_w_in, m_conv_w, m_conv_b, m_dt_bias, m_a_log, m_d_skip, m_ssm_norm_w, m_g_q,
                               m_g_k, m_f_bias, m_w_out, m_g_xattn, m_g_mem, m_xq_w, m_xkv_w, m_xg_q, m_xg_k, m_xo_w,
                               m_g_mlp, m_w_up, m_w_down)))
    mom2 = dict(zip(_WEIGHTS, (v_g_mix, v_w_in, v_conv_w, v_conv_b, v_dt_bias, v_a_log, v_d_skip, v_ssm_norm_w, v_g_q,
                               v_g_k, v_f_bias, v_w_out, v_g_xattn, v_g_mem, v_xq_w, v_xkv_w, v_xg_q, v_xg_k, v_xo_w,
                               v_g_mlp, v_w_up, v_w_down)))
    chip = _chip_index(lax.axis_index("x"), lax.axis_index("y"))

    shard_bf = {n: w[n][0].astype(BF16) for n in _BIG}

    def layout_for_compute(n, g):
        if n == "w_in":
            return _w_in_from_shards(g)
        return g if n in _COL_SHARDED else g.reshape(N_CHIPS * g.shape[1], g.shape[2])

    def layout_for_reduction(n, g):
        if n == "w_in":
            return _w_in_to_shards(g)
        return g if n in _COL_SHARDED else g.reshape(N_CHIPS, g.shape[0] // N_CHIPS, g.shape[1])

    halves_in = shard_bf["w_in"].reshape(2, shard_bf["w_in"].shape[0] // 2, -1)
    g_in, g_conv = _all_gather_chips([halves_in], [w["conv_w"][0]])
    g_in = lax.dynamic_update_index_in_dim(g_in, halves_in, chip, axis=0)
    g_conv = lax.dynamic_update_index_in_dim(g_conv, w["conv_w"][0], chip, axis=0)
    w_in_full = layout_for_compute("w_in", g_in.reshape(N_CHIPS, -1, g_in.shape[-1]))
    p = {n: w[n] for n in _SMALL}
    p["conv_w"] = g_conv.transpose(1, 0, 2).reshape(CONV_WIDTH, CONV_DIM)
    gather = _split_start("gather_late", [shard_bf[n] for n in _LATE], "gather", after=g_in)
    p["g_mix"] = p["g_mix"] + gather.token[:1, :1]

    def late_weights(after):
        srcs, lands = _split_wait("gather_late_wait", gather, "gather", after)
        lands = [lax.dynamic_update_index_in_dim(l, s, chip, axis=0) for l, s in zip(lands, srcs)]
        return {n: layout_for_compute(n, l) for n, l in zip(_LATE, lands)}

    scatter = {}

    def send_late_grads(grads):
        scatter["late"] = _split_start("scatter_late", [layout_for_reduction(n, grads[n]) for n in _LATE], "scatter",
                                       after=None)
        return scatter["late"].token

    def send_w_in_grad(g):
        scatter["w_in"] = _split_start("scatter_w_in", [layout_for_reduction("w_in", g)], "scatter", after=None)
        return scatter["w_in"].token

    loss_row, dx, gp = _layer_fwd_bwd(x[0], mem[0], loss_target[0], w_in_full, p, late_weights, send_late_grads,
                                      send_w_in_grad)

    grad, delta, new_m, new_v = {}, {}, {}, {}

    def finish(names, sources, from_chips, tag):
        mine = [_chip_sum(lax.dynamic_index_in_dim(s, chip, axis=0, keepdims=False), fc, "rs_chip_sum_" + n)
                for n, s, fc in zip(names, sources, from_chips)]
        for n, a, b in zip(names, mine, _sibling_swap(mine, "rs_sibling_swap_" + tag)):
            shape = w[n].shape
            res = _adamw(w[n][0], a, b, mom1[n][0], mom2[n][0], "adamw_" + n)
            grad[n], delta[n], new_m[n], new_v[n] = (r.reshape(shape) for r in res)

    finish(_LATE, *_split_wait("scatter_late_wait", scatter["late"], "scatter", (dx,)), "late")

    sources_in, from_chips_in = _split_wait("scatter_w_in_wait", scatter["w_in"], "scatter",
                                            tuple(new_v[n] for n in _LATE))

    packed, starts = _pack_rows([gp[n] for n in _SMALL] + [loss_row], CONV_DIM)
    summed = _all_reduce_small(packed, from_chips_in[0])
    loss = summed[starts[-1], 0]
    finish(["w_in"], sources_in, from_chips_in, "w_in")

    as_rows = lambda a: a.reshape(-1, a.shape[-1])
    results = _adamw_small(summed, starts, [as_rows(w[n]) for n in _SMALL], [as_rows(mom1[n]) for n in _SMALL],
                           [as_rows(mom2[n]) for n in _SMALL], _SMALL.index("conv_w"))
    for n, res in zip(_SMALL, results):
        grad[n], delta[n], new_m[n], new_v[n] = (a.reshape(w[n].shape) for a in res)

    return (loss, dx[None], *[grad[n] for n in _WEIGHTS], *[delta[n] for n in _WEIGHTS],
            *[new_m[n] for n in _WEIGHTS], *[new_v[n] for n in _WEIGHTS])
```

```python
from typing import NamedTuple

import jax
import jax.numpy as jnp
from jax import lax
from jax.experimental import pallas as pl
from jax.experimental.pallas import tpu as pltpu

F32 = jnp.float32
BF16 = jnp.bfloat16
HI = lax.Precision.HIGHEST
MESH = pl.DeviceIdType.MESH

EPS = 1e-5
CHUNK = 128
SSM_HEADS = 16
SSM_GROUPS = 2
HEADS_PER_GROUP = SSM_HEADS // SSM_GROUPS
HEAD_DIM = 64
SSM_STATE = 128
ATTN_HEADS = 16
XATTN_HEADS = 4
XATTN_DIM = 256
CONV_WIDTH = 4
CONV_COLS = 256
N_CHIPS = 4
N_DEV = 8
LANES = 128
VMEM_LIMIT = 56 * 1024 * 1024

ADAM_LR = 0.001
ADAM_B1 = 0.9
ADAM_B2 = 0.999
ADAM_EPS = 1e-08
ADAM_WD = 0.01
ADAM_STEP = 10


def _params(sem):
    return pltpu.CompilerParams(dimension_semantics=sem, vmem_limit_bytes=VMEM_LIMIT)


def _pick(n, cands):
    for c in cands:
        if n % c == 0:
            return c
    return n


def _mm(a, b, mode, name, out_dtypes=(F32,), epilogue=None, extras=(), b_chunks=1, out_chunks=1,
        tm=None, tn=None, tk=None):
    if mode == "nn":
        M, K = a.shape
        N = b.shape[-1] * b_chunks
    elif mode == "nt":
        M, K = a.shape
        N = b.shape[-2]
        assert b.shape[-1] * b_chunks == K
    else:
        K, M = a.shape
        N = b.shape[-1] * b_chunks
    tm = tm or _pick(M, (2048, 1024, 512, 256, 128))
    tn = tn or _pick(N // max(b_chunks if mode != "nt" else 1, out_chunks), (512, 640, 384, 256, 128))
    if tk is None:
        kmax = b.shape[-1] if mode == "nt" else K
        tk = kmax if kmax <= 2048 else _pick(kmax, (2048, 1152, 1024, 512))
    nk = K // tk
    assert M % tm == 0 and N % tn == 0 and K % tk == 0
    grid = (M // tm, N // tn, nk)

    if mode == "tn":
        a_spec = pl.BlockSpec((tk, tm), lambda i, j, k: (k, i))
    else:
        a_spec = pl.BlockSpec((tm, tk), lambda i, j, k: (i, k))

    def b_index(t_row, t_last, tile_last):
        if b_chunks == 1:
            return (t_row, t_last)
        q = (b.shape[-1]) // tile_last
        return (t_last // q, t_row, t_last % q)

    if mode == "nn" or mode == "tn":
        bshape = (tk, tn)
        bmap = lambda i, j, k: b_index(k, j, tn)
    else:
        bshape = (tn, tk)
        bmap = lambda i, j, k: b_index(j, k, tk)
    if b_chunks > 1:
        bshape = (None,) + bshape
    b_spec = pl.BlockSpec(bshape, bmap)

    if out_chunks == 1:
        o_spec = pl.BlockSpec((tm, tn), lambda i, j, k: (i, j))
        o_shape = (M, N)
    else:
        qo = (N // out_chunks) // tn
        o_spec = pl.BlockSpec((None, tm, tn), lambda i, j, k: (j // qo, i, j % qo))
        o_shape = (out_chunks, M, N // out_chunks)
    e_spec = pl.BlockSpec((tm, tn), lambda i, j, k: (i, j))

    dims = {"nn": (((1,), (0,)), ((), ())), "nt": (((1,), (1,)), ((), ())), "tn": (((0,), (0,)), ((), ()))}[mode]
    n_ex = len(extras)
    n_out = len(out_dtypes)

    def body(*refs):
        a_ref, b_ref = refs[0], refs[1]
        ex_refs = refs[2:2 + n_ex]
        o_refs = refs[2 + n_ex:2 + n_ex + n_out]

        def finish(acc):
            outs = epilogue(acc, *[r[...] for r in ex_refs]) if epilogue is not None else (acc,)
            for r, o in zip(o_refs, outs):
                r[...] = o.astype(r.dtype)

        part = lax.dot_general(a_ref[...].astype(BF16), b_ref[...].astype(BF16), dims,
                               preferred_element_type=F32)
        if nk == 1:
            finish(part)
        else:
            acc_ref = refs[-1]
            k = pl.program_id(2)

            @pl.when(k == 0)
            def _():
                acc_ref[...] = part

            @pl.when(k > 0)
            def _():
                acc_ref[...] += part

            @pl.when(k == nk - 1)
            def _():
                finish(acc_ref[...])

    outs = pl.pallas_call(
        body,
        grid=grid,
        in_specs=[a_spec, b_spec] + [e_spec] * n_ex,
        out_specs=[o_spec] * n_out,
        out_shape=[jax.ShapeDtypeStruct(o_shape, d) for d in out_dtypes],
        scratch_shapes=[pltpu.VMEM((tm, tn), F32)] if nk > 1 else [],
        compiler_params=_params(("parallel", "parallel", "arbitrary")),
        name=name,
    )(a, b, *extras)
    return outs[0] if n_out == 1 else outs


def _rms(x, g):
    r = lax.rsqrt(jnp.mean(x * x, axis=-1, keepdims=True) + EPS)
    return x * r * g


def _rmsnorm_fwd(x, g, name):
    R, D = x.shape
    tr = _pick(R, (512, 256))

    def body(x_ref, g_ref, o_ref):
        o_ref[...] = _rms(x_ref[...], g_ref[...]).astype(o_ref.dtype)

    return pl.pallas_call(
        body, grid=(R // tr,),
        in_specs=[pl.BlockSpec((tr, D), lambda i: (i, 0)), pl.BlockSpec((1, D), lambda i: (0, 0))],
        out_specs=pl.BlockSpec((tr, D), lambda i: (i, 0)),
        out_shape=jax.ShapeDtypeStruct((R, D), BF16),
        compiler_params=_params(("parallel",)), name=name)(x, g)


def _rmsnorm_bwd(x, g, dh, dres, name):
    R, D = x.shape
    tr = _pick(R, (256,))
    has_res = dres is not None

    def body(*refs):
        if has_res:
            x_ref, g_ref, dh_ref, dres_ref, dx_ref, dg_ref = refs
        else:
            x_ref, g_ref, dh_ref, dx_ref, dg_ref = refs
        _, vjp = jax.vjp(_rms, x_ref[...], g_ref[...])
        dx, dg = vjp(dh_ref[...])
        if has_res:
            dx = dx + dres_ref[...]
        dx_ref[...] = dx

        @pl.when(pl.program_id(0) == 0)
        def _():
            dg_ref[...] = jnp.zeros_like(dg_ref)

        dg_ref[...] += dg

    row = pl.BlockSpec((tr, D), lambda i: (i, 0))
    vec = pl.BlockSpec((1, D), lambda i: (0, 0))
    ins = [x, g, dh] + ([dres] if has_res else [])
    return pl.pallas_call(
        body, grid=(R // tr,),
        in_specs=[row, vec, row] + ([row] if has_res else []),
        out_specs=[row, vec],
        out_shape=[jax.ShapeDtypeStruct((R, D), F32), jax.ShapeDtypeStruct((1, D), F32)],
        compiler_params=_params(("arbitrary",)), name=name)(*ins)


def _shift_down(u, k):
    if k == 0:
        return u
    rows = lax.broadcasted_iota(jnp.int32, u.shape, 0)
    return jnp.where(rows >= k, pltpu.roll(u, k, axis=0), 0.0)


def _shift_up(u, k):
    if k == 0:
        return u
    n = u.shape[0]
    rows = lax.broadcasted_iota(jnp.int32, u.shape, 0)
    return jnp.where(rows < n - k, pltpu.roll(u, n - k, axis=0), 0.0)


def _conv_pre(u, w, b):
    pre = b
    for j in range(CONV_WIDTH):
        pre = pre + w[j:j + 1, :] * _shift_down(u, CONV_WIDTH - 1 - j)
    return pre


def _conv_fwd(proj, col0, ncols, conv_w, conv_b):
    S = proj.shape[0]
    cb0 = col0 // CONV_COLS

    def body(u_ref, w_ref, b_ref, o_ref):
        pre = _conv_pre(u_ref[...], w_ref[...], b_ref[...])
        o_ref[...] = pre * jax.nn.sigmoid(pre)

    return pl.pallas_call(
        body, grid=(ncols // CONV_COLS,),
        in_specs=[pl.BlockSpec((S, CONV_COLS), lambda j: (0, j + cb0)),
                  pl.BlockSpec((CONV_WIDTH, CONV_COLS), lambda j: (0, j)),
                  pl.BlockSpec((1, CONV_COLS), lambda j: (0, j))],
        out_specs=pl.BlockSpec((S, CONV_COLS), lambda j: (0, j)),
        out_shape=jax.ShapeDtypeStruct((S, ncols), F32),
        compiler_params=_params(("parallel",)), name="conv_fwd")(proj, conv_w, conv_b)


def _conv_bwd(proj, col0, ncols, conv_w, conv_b, douts, dproj):
    S = proj.shape[0]
    cb0 = col0 // CONV_COLS
    starts = [0]
    for d in douts:
        starts.append(starts[-1] + d.shape[1] // CONV_COLS)
    assert starts[-1] == ncols // CONV_COLS
    nd = len(douts)

    def body(u_ref, w_ref, b_ref, *rest):
        d_refs, (du_ref, dw_ref, db_ref) = rest[:nd], rest[nd + 1:]
        j = pl.program_id(0)
        dout = d_refs[-1][...]
        for i in range(nd - 2, -1, -1):
            dout = jnp.where(j < starts[i + 1], d_refs[i][...], dout)
        u = u_ref[...]
        w = w_ref[...]
        pre = _conv_pre(u, w, b_ref[...])
        s = jax.nn.sigmoid(pre)
        dpre = dout * (s * (1.0 + pre * (1.0 - s)))
        du = jnp.zeros_like(u)
        rows = []
        for j in range(CONV_WIDTH):
            k = CONV_WIDTH - 1 - j
            du = du + w[j:j + 1, :] * _shift_up(dpre, k)
            rows.append(jnp.sum(dpre * _shift_down(u, k), axis=0, keepdims=True))
        du_ref[...] = du.astype(du_ref.dtype)
        rows.append(jnp.zeros((8 - CONV_WIDTH, CONV_COLS), F32))
        dw_ref[...] = jnp.concatenate(rows, axis=0)
        db_ref[...] = jnp.sum(dpre, axis=0, keepdims=True)

    return pl.pallas_call(
        body, grid=(ncols // CONV_COLS,),
        in_specs=[pl.BlockSpec((S, CONV_COLS), lambda j: (0, j + cb0)),
                  pl.BlockSpec((CONV_WIDTH, CONV_COLS), lambda j: (0, j)),
                  pl.BlockSpec((1, CONV_COLS), lambda j: (0, j))]
        + [pl.BlockSpec((S, CONV_COLS), lambda j, lo=starts[i], hi=starts[i + 1]: (0, jnp.clip(j - lo, 0, hi - lo - 1)))
           for i in range(nd)] + [_ANY],
        out_specs=[pl.BlockSpec((S, CONV_COLS), lambda j: (0, j + cb0)),
                   pl.BlockSpec((8, CONV_COLS), lambda j: (0, j)),
                   pl.BlockSpec((1, CONV_COLS), lambda j: (0, j))],
        out_shape=[jax.ShapeDtypeStruct(dproj.shape, dproj.dtype),
                   jax.ShapeDtypeStruct((8, ncols), F32),
                   jax.ShapeDtypeStruct((1, ncols), F32)],
        input_output_aliases={3 + nd: 0},
        compiler_params=_params(("parallel",)), name="conv_bwd")(proj, conv_w, conv_b, *douts, dproj)


def _softplus(x):
    return jnp.maximum(x, 0.0) + jnp.log1p(jnp.exp(-jnp.abs(x)))


def _dot32(a, b, dims=(((1,), (0,)), ((), ()))):
    return lax.dot_general(a, b, dims, precision=HI, preferred_element_type=F32)


def _dotd(a, b, dims=(((1,), (0,)), ((), ()))):
    return lax.dot_general(a, b, dims, preferred_element_type=F32)


PAIRS_PER_GROUP = HEADS_PER_GROUP // 2


def _ssd_chunk(xs, Bm, Cm, z, dtr, dtb, alog, dsk, nw, h):
    L = Bm.shape[0]
    ri = lax.broadcasted_iota(jnp.int32, (L, L), 0)
    ci = lax.broadcasted_iota(jnp.int32, (L, L), 1)
    causal = ri >= ci
    tril = causal.astype(F32)
    first = _first_head(L)
    first1 = _first_head(1)
    CB = _dotd(Cm, Bm, _NT)
    gated, hnew = [], []
    ssq = jnp.zeros((L, 1), F32)
    for pp in range(len(xs)):
        dts, cums, tots, decay = [], [], [], []
        for a in range(2):
            r = 2 * pp + a
            dt = _softplus(dtr[r] + dtb[r])
            dA = dt * (-jnp.exp(alog[r]))
            acs = _dot32(tril, dA)
            cc = jnp.broadcast_to(acs, (L, L))
            decay.append(CB * jnp.exp(jnp.where(causal, cc - cc.T, -1e30)))
            dts.append(dt)
            cums.append(acs)
            tots.append(jnp.sum(dA, axis=0, keepdims=True))
        dt2 = jnp.where(first, dts[0], dts[1])
        acs2 = jnp.where(first, cums[0], cums[1])
        tot2 = jnp.where(first1, tots[0], tots[1])
        dsk2 = jnp.where(first1, dsk[2 * pp], dsk[2 * pp + 1])
        X = xs[pp] * dt2
        y = (jnp.where(first, _dotd(decay[0], X), _dotd(decay[1], X)) + jnp.exp(acs2) * _dotd(Cm, h[pp])
             + dsk2 * xs[pp])
        hnew.append(jnp.exp(tot2) * h[pp] + _dotd(Bm, X * jnp.exp(tot2 - acs2), _TN))
        g = y * (z[pp] * jax.nn.sigmoid(z[pp]))
        ssq = ssq + jnp.sum(g * g, axis=-1, keepdims=True)
        gated.append(g)
    rs = lax.rsqrt(ssq / (len(xs) * LANES) + EPS)
    return [g * rs * nw[pp] for pp, g in enumerate(gated)], hnew


def _ssd_args(xs_ref, b_ref, c_ref, z_ref, dt_ref, dtb_ref, al_ref, dsk_ref, nw_ref, h_ref):
    pairs = range(PAIRS_PER_GROUP)
    heads = range(HEADS_PER_GROUP)
    lanes = lambda ref, pp: ref[:, pp * LANES:(pp + 1) * LANES]
    return ([lanes(xs_ref, pp) for pp in pairs], b_ref[...], c_ref[...], [lanes(z_ref, pp) for pp in pairs],
            [dt_ref[r] for r in heads], [dtb_ref[r] for r in heads], [al_ref[r] for r in heads],
            [dsk_ref[r] for r in heads], [lanes(nw_ref, pp) for pp in pairs], [h_ref[pp] for pp in pairs])


def _ssd_specs(rev):
    H, N, L = HEADS_PER_GROUP, SSM_STATE, CHUNK
    gw = H * HEAD_DIM
    return dict(
        cols=lambda col0: pl.BlockSpec((L, gw), lambda g, c: (rev(c), col0 // gw + g)),
        bc=lambda first_block: pl.BlockSpec((L, N), lambda g, c: (rev(c), first_block + g)),
        dt=pl.BlockSpec((H, L, 1), lambda g, c: (g, rev(c), 0)),
        scal=pl.BlockSpec((H, 1, 1), lambda g, c: (g, 0, 0)),
        nw=pl.BlockSpec((1, gw), lambda g, c: (0, g)),
        hs=pl.BlockSpec((None, PAIRS_PER_GROUP, N, LANES), lambda g, c: (rev(c), g, 0, 0)),
        b_block=SSM_INNER // N,
    )


def _ssd_fwd(xbc, proj, dt_hm, dtb, alog, dsk, nw):
    S = xbc.shape[0]
    N, L = SSM_STATE, CHUNK
    nc = S // L
    sp = _ssd_specs(lambda c: c)

    def body(xs_ref, b_ref, c_ref, z_ref, dt_ref, dtb_ref, al_ref, dsk_ref, nw_ref, y_ref, hs_ref, h_ref):
        @pl.when(pl.program_id(1) == 0)
        def _():
            h_ref[...] = jnp.zeros_like(h_ref)

        hs_ref[...] = h_ref[...]
        out, hnew = _ssd_chunk(*_ssd_args(xs_ref, b_ref, c_ref, z_ref, dt_ref, dtb_ref, al_ref, dsk_ref, nw_ref, h_ref))
        for pp in range(PAIRS_PER_GROUP):
            y_ref[:, pp * LANES:(pp + 1) * LANES] = out[pp].astype(y_ref.dtype)
            h_ref[pp] = hnew[pp]

    return pl.pallas_call(
        body, grid=(SSM_GROUPS, nc),
        in_specs=[sp["cols"](0), sp["bc"](sp["b_block"]), sp["bc"](sp["b_block"] + SSM_GROUPS), sp["cols"](COL_Z),
                  sp["dt"], sp["scal"], sp["scal"], sp["scal"], sp["nw"]],
        out_specs=[sp["cols"](0), sp["hs"]],
        out_shape=[jax.ShapeDtypeStruct((S, MIX_WIDTH), BF16),
                   jax.ShapeDtypeStruct((nc, SSM_HEADS // 2, N, LANES), F32)],
        scratch_shapes=[pltpu.VMEM((PAIRS_PER_GROUP, N, LANES), F32)],
        compiler_params=_params(("parallel", "arbitrary")), name="ssd_fwd",
    )(xbc, xbc, xbc, proj, dt_hm, dtb, alog, dsk, nw)


def _ssd_bwd(xbc, proj, dt_hm, dtb, alog, dsk, nw, hs, dmixed, dproj):
    S = xbc.shape[0]
    N, L = SSM_STATE, CHUNK
    nc = S // L
    sp = _ssd_specs(lambda c: nc - 1 - c)

    def body(xs_ref, b_ref, c_ref, z_ref, dt_ref, dtb_ref, al_ref, dsk_ref, nw_ref, hs_ref, dy_ref, buf_ref,
             dxs_ref, dz_ref, db_ref, dc_ref, ddt_ref, ddtb_ref, dal_ref, ddsk_ref, dnw_ref, dh_ref):
        @pl.when(pl.program_id(1) == 0)
        def _():
            dh_ref[...] = jnp.zeros_like(dh_ref)
            ddtb_ref[...] = jnp.zeros_like(ddtb_ref)
            dal_ref[...] = jnp.zeros_like(dal_ref)
            ddsk_ref[...] = jnp.zeros_like(ddsk_ref)
            dnw_ref[...] = jnp.zeros_like(dnw_ref)

        pairs = range(PAIRS_PER_GROUP)
        lanes = lambda pp: slice(pp * LANES, (pp + 1) * LANES)
        _, vjp = jax.vjp(_ssd_chunk, *_ssd_args(xs_ref, b_ref, c_ref, z_ref, dt_ref, dtb_ref, al_ref, dsk_ref, nw_ref,
                                                hs_ref))
        dxs, dB, dC, dz, ddt, ddtb, dal, ddsk, dnw, dh = vjp(([dy_ref[:, lanes(pp)] for pp in pairs],
                                                              [dh_ref[pp] for pp in pairs]))
        db_ref[...] = dB
        dc_ref[...] = dC
        for pp in pairs:
            dxs_ref[:, lanes(pp)] = dxs[pp]
            dz_ref[:, lanes(pp)] = dz[pp].astype(dz_ref.dtype)
            dnw_ref[:, lanes(pp)] += dnw[pp]
            dh_ref[pp] = dh[pp]
        for r in range(HEADS_PER_GROUP):
            ddt_ref[r] = ddt[r]
            ddtb_ref[r] += ddtb[r]
            dal_ref[r] += dal[r]
            ddsk_ref[r] += ddsk[r]

    bc_out = pl.BlockSpec((L, N), lambda g, c: (nc - 1 - c, g))
    return pl.pallas_call(
        body, grid=(SSM_GROUPS, nc),
        in_specs=[sp["cols"](0), sp["bc"](sp["b_block"]), sp["bc"](sp["b_block"] + SSM_GROUPS), sp["cols"](COL_Z),
                  sp["dt"], sp["scal"], sp["scal"], sp["scal"], sp["nw"], sp["hs"], sp["cols"](0), _ANY],
        out_specs=[sp["cols"](0), sp["cols"](COL_Z), bc_out, bc_out, sp["dt"], sp["scal"], sp["scal"], sp["scal"],
                   sp["nw"]],
        input_output_aliases={11: 1},
        out_shape=[jax.ShapeDtypeStruct((S, SSM_INNER), F32), jax.ShapeDtypeStruct(dproj.shape, dproj.dtype),
                   jax.ShapeDtypeStruct((S, SSM_GROUPS * N), F32), jax.ShapeDtypeStruct((S, SSM_GROUPS * N), F32),
                   jax.ShapeDtypeStruct((SSM_HEADS, S, 1), F32),
                   jax.ShapeDtypeStruct((SSM_HEADS, 1, 1), F32), jax.ShapeDtypeStruct((SSM_HEADS, 1, 1), F32),
                   jax.ShapeDtypeStruct((SSM_HEADS, 1, 1), F32), jax.ShapeDtypeStruct((1, SSM_INNER), F32)],
        scratch_shapes=[pltpu.VMEM((PAIRS_PER_GROUP, N, LANES), F32)],
        compiler_params=_params(("parallel", "arbitrary")), name="ssd_bwd",
    )(xbc, xbc, xbc, proj, dt_hm, dtb, alog, dsk, nw, hs, dmixed, dproj)


ATTN_SCALE = HEAD_DIM ** -0.5
PREP_COLS = 512


def _first_head(rows):
    return lax.broadcasted_iota(jnp.int32, (rows, LANES), 1) < HEAD_DIM


def _pair_norm(x, g2, scale):
    first = _first_head(x.shape[0])
    sq = x * x
    ms0 = jnp.sum(jnp.where(first, sq, 0.0), axis=-1, keepdims=True) * (1.0 / HEAD_DIM)
    ms1 = jnp.sum(jnp.where(first, 0.0, sq), axis=-1, keepdims=True) * (1.0 / HEAD_DIM)
    r = jnp.where(first, lax.rsqrt(ms0 + EPS), lax.rsqrt(ms1 + EPS))
    return x * r * g2 * scale


def _qk_prep_fwd(proj, gq2, gk2):
    S = proj.shape[0]
    tq = _pick(S, (512, 256))

    def body(q_ref, k_ref, v_ref, gq_ref, gk_ref, qo_ref, ko_ref, vo_ref):
        for b in range(PREP_COLS // LANES):
            pair = slice(b * LANES, (b + 1) * LANES)
            qo_ref[:, pair] = _pair_norm(q_ref[:, pair], gq_ref[...], ATTN_SCALE).astype(BF16)
            ko_ref[:, pair] = _pair_norm(k_ref[:, pair], gk_ref[...], 1.0).astype(BF16)
        vo_ref[...] = v_ref[...].astype(BF16)

    col = lambda c0: pl.BlockSpec((tq, PREP_COLS), lambda h, i: (i, c0 // PREP_COLS + h))
    blk = pl.BlockSpec((tq, PREP_COLS), lambda h, i: (i, h))
    vec = pl.BlockSpec((1, LANES), lambda h, i: (0, 0))
    return pl.pallas_call(
        body, grid=(ATTN_WIDTH // PREP_COLS, S // tq), in_specs=[col(COL_Q), col(COL_K), col(COL_V), vec, vec],
        out_specs=[blk, blk, blk], out_shape=[jax.ShapeDtypeStruct((S, ATTN_WIDTH), BF16)] * 3,
        compiler_params=_params(("parallel", "parallel")), name="qk_prep_fwd")(proj, proj, proj, gq2, gk2)


def _pair_norm_bwd(proj, col0, g2, scale, dn, dproj, name):
    S = proj.shape[0]
    tq = _pick(S, (512, 256))

    def body(u_ref, g_ref, dn_ref, buf_ref, du_ref, dg_ref):
        @pl.when((pl.program_id(0) == 0) & (pl.program_id(1) == 0))
        def _():
            dg_ref[...] = jnp.zeros_like(dg_ref)

        for b in range(PREP_COLS // LANES):
            pair = slice(b * LANES, (b + 1) * LANES)
            _, vjp = jax.vjp(lambda u, g: _pair_norm(u, g, scale), u_ref[:, pair], g_ref[...])
            du, dg = vjp(dn_ref[:, pair])
            du_ref[:, pair] = du.astype(du_ref.dtype)
            dg_ref[...] += dg

    ublk = pl.BlockSpec((tq, PREP_COLS), lambda h, i: (i, col0 // PREP_COLS + h))
    blk = pl.BlockSpec((tq, PREP_COLS), lambda h, i: (i, h))
    vec = pl.BlockSpec((1, LANES), lambda h, i: (0, 0))
    return pl.pallas_call(
        body, grid=(ATTN_WIDTH // PREP_COLS, S // tq), in_specs=[ublk, vec, blk, _ANY], out_specs=[ublk, vec],
        out_shape=[jax.ShapeDtypeStruct(dproj.shape, dproj.dtype), jax.ShapeDtypeStruct((1, LANES), F32)],
        input_output_aliases={3: 0},
        compiler_params=_params(("arbitrary", "arbitrary")), name=name)(proj, g2, dn, dproj)


def _logf_cumsum_fwd(f_raw, f_bias):
    S, Hh = f_raw.shape
    L = CHUNK

    def body(f_ref, b_ref, o_ref, wide_ref):
        ri = lax.broadcasted_iota(jnp.int32, (L, L), 0)
        ci = lax.broadcasted_iota(jnp.int32, (L, L), 1)
        tril = (ri >= ci).astype(F32)
        carry = jnp.zeros((1, Hh), F32)
        for c in range(S // L):
            rows = slice(c * L, (c + 1) * L)
            lf = -_softplus(-(f_ref[rows, :] + b_ref[...]))
            cum = _dot32(tril, lf) + carry
            o_ref[rows, :] = cum
            for h in range(Hh):
                wide_ref[rows, h * HEAD_DIM:(h + 1) * HEAD_DIM] = jnp.broadcast_to(cum[:, h:h + 1], (L, HEAD_DIM))
            carry = cum[L - 1:L, :]

    return pl.pallas_call(
        body, out_shape=[jax.ShapeDtypeStruct((S, Hh), F32), jax.ShapeDtypeStruct((S, Hh * HEAD_DIM), F32)],
        name="logf_cumsum_fwd")(f_raw, f_bias)


def _logf_cumsum_bwd(f_raw, f_bias, dcum):
    S, Hh = f_raw.shape
    L = CHUNK

    def body(f_ref, b_ref, d_ref, df_ref, db_ref):
        ri = lax.broadcasted_iota(jnp.int32, (L, L), 0)
        ci = lax.broadcasted_iota(jnp.int32, (L, L), 1)
        triu = (ri <= ci).astype(F32)
        carry = jnp.zeros((1, Hh), F32)
        db = jnp.zeros((1, Hh), F32)
        for c in reversed(range(S // L)):
            suf = _dot32(triu, d_ref[c * L:(c + 1) * L, :]) + carry
            df = suf * jax.nn.sigmoid(-(f_ref[c * L:(c + 1) * L, :] + b_ref[...]))
            df_ref[c * L:(c + 1) * L, :] = df
            db = db + jnp.sum(df, axis=0, keepdims=True)
            carry = suf[0:1, :]
        db_ref[...] = db

    return pl.pallas_call(
        body, out_shape=[jax.ShapeDtypeStruct((S, Hh), F32), jax.ShapeDtypeStruct((1, Hh), F32)],
        name="logf_cumsum_bwd")(f_raw, f_bias, dcum)


_NT = (((1,), (1,)), ((), ()))
_TN = (((0,), (0,)), ((), ()))


def _mxu(a, b, dims=(((1,), (0,)), ((), ()))):
    return lax.dot_general(a, b, dims, preferred_element_type=F32)


def _flash_fwd(qs, kn, vb, cq, ck, mixed):
    S, W = qs.shape
    tq = tk = _pick(S, (512, 256))
    nmask = max(tq // tk, 1)

    def body(q_ref, k_ref, v_ref, cq_ref, ck_ref, buf_ref, o_ref, of_ref, lse_ref):
        i = pl.program_id(1)
        first = _first_head(tq)
        q2 = q_ref[...]
        zero = jnp.zeros_like(q2)
        qa = (jnp.where(first, q2, zero), jnp.where(first, zero, q2))
        cqa = (cq_ref[:, 0:1], cq_ref[:, HEAD_DIM:HEAD_DIM + 1])
        row0 = i * tq

        def step(j, carry, masked):
            ms, ls, acc, rem = carry
            off = pl.multiple_of(j * tk, tk)
            k = k_ref[pl.ds(off, tk), :]
            v = v_ref[pl.ds(off, tk), :]
            new_m, new_l, alphas, pvs, prs = [], [], [], [], []
            for a in range(2):
                s = _mxu(qa[a], k, _NT) + cqa[a] - ck_ref[a, :, pl.ds(off, tk)]
                if masked:
                    ri = lax.broadcasted_iota(jnp.int32, (tq, tk), 0) + row0
                    ci = lax.broadcasted_iota(jnp.int32, (tq, tk), 1) + off
                    s = jnp.where(ri >= ci, s, -1e30)
                m_new = jnp.maximum(ms[a], jnp.max(s, axis=-1, keepdims=True))
                alpha = jnp.exp(ms[a] - m_new)
                p = jnp.exp(s - m_new)
                new_l.append(alpha * ls[a] + jnp.sum(p, axis=-1, keepdims=True))
                new_m.append(m_new)
                alphas.append(alpha)
                p_hi = p.astype(BF16)
                pvs.append(_mxu(p_hi, v))
                prs.append(_mxu((p - p_hi.astype(F32)).astype(BF16), v))
            al = jnp.where(first, alphas[0], alphas[1])
            acc = al * acc + jnp.where(first, pvs[0], pvs[1])
            rem = al * rem + jnp.where(first, prs[0], prs[1])
            return tuple(new_m), tuple(new_l), acc, rem

        neg = jnp.full((tq, 1), -1e30, F32)
        z1 = jnp.zeros((tq, 1), F32)
        z2 = jnp.zeros((tq, LANES), F32)
        carry = ((neg, neg), (z1, z1), z2, z2)
        n_full = (i * tq) // tk
        carry = lax.fori_loop(0, n_full, lambda j, c: step(j, c, False), carry)
        for jj in range(nmask):
            carry = step(n_full + jj, carry, True)
        ms, ls, acc, rem = carry
        linv = jnp.where(first, 1.0 / ls[0], 1.0 / ls[1])
        o_ref[...] = (acc * linv).astype(o_ref.dtype)
        of_ref[...] = (acc + rem) * linv
        lse_ref[...] = jnp.where(first, ms[0] + jnp.log(ls[0]), ms[1] + jnp.log(ls[1]))

    qblk = pl.BlockSpec((tq, LANES), lambda h, i: (i, h))
    full = pl.BlockSpec((S, LANES), lambda h, i: (0, h))
    return pl.pallas_call(
        body, grid=(W // LANES, S // tq),
        in_specs=[qblk, full, full, qblk, pl.BlockSpec((2, 1, S), lambda h, i: (h, 0, 0)), _ANY],
        out_specs=[pl.BlockSpec((tq, LANES), lambda h, i: (i, SSM_INNER // LANES + h)), qblk, qblk],
        out_shape=[jax.ShapeDtypeStruct(mixed.shape, mixed.dtype), jax.ShapeDtypeStruct((S, W), F32),
                   jax.ShapeDtypeStruct((S, W), F32)],
        input_output_aliases={5: 0},
        compiler_params=_params(("parallel", "parallel")), name="flash_fwd")(qs, kn, vb, cq, ck, mixed)


def _flash_bwd(qs, kn, vb, cq, ck, o_fine, do, do_col0, lse):
    S, W = qs.shape
    tq = tk = _pick(S, (512, 256))
    nq = S // tq
    nmask = max(tk // tq, 1)

    def body(q_ref, k_ref, v_ref, cq_ref, ck_ref, of_ref, do_ref, lse_ref, dq_ref, dk_ref, dv_ref, dck_ref):
        j = pl.program_id(1)

        @pl.when(j == 0)
        def _():
            dq_ref[...] = jnp.zeros_like(dq_ref)

        firstk = _first_head(tk)
        firstq = _first_head(tq)
        k2 = k_ref[...]
        v2 = v_ref[...]
        zk = jnp.zeros_like(k2)
        ka = (jnp.where(firstk, k2, zk), jnp.where(firstk, zk, k2))
        va = (jnp.where(firstk, v2, zk), jnp.where(firstk, zk, v2))
        cka = (ck_ref[0], ck_ref[1])
        col0 = j * tk

        def step(i, carry, masked):
            dk, dv, dck0, dck1 = carry
            dcks = [dck0, dck1]
            off = pl.multiple_of(i * tq, tq)
            rows = pl.ds(off, tq)
            q2 = q_ref[rows, :]
            dob = do_ref[rows, :].astype(BF16)
            prod = dob.astype(F32) * of_ref[rows, :]
            dkp, dvp, dqp = [], [], []
            for a in range(2):
                lane = pl.ds(a * HEAD_DIM, 1)
                s = _mxu(q2, ka[a], _NT) + cq_ref[rows, lane] - cka[a]
                if masked:
                    ri = lax.broadcasted_iota(jnp.int32, (tq, tk), 0) + off
                    ci = lax.broadcasted_iota(jnp.int32, (tq, tk), 1) + col0
                    s = jnp.where(ri >= ci, s, -1e30)
                p = jnp.exp(s - lse_ref[rows, lane])
                dp = _mxu(dob, va[a], _NT)
                own = jnp.where(firstq, prod, 0.0) if a == 0 else jnp.where(firstq, 0.0, prod)
                ds = p * (dp - jnp.sum(own, axis=-1, keepdims=True))
                dsb = ds.astype(BF16)
                dvp.append(_mxu(p.astype(BF16), dob, _TN))
                dkp.append(_mxu(dsb, q2, _TN))
                dqp.append(_mxu(dsb, k2))
                dcks[a] = dcks[a] - jnp.sum(ds, axis=0, keepdims=True)
            dq_ref[rows, :] += jnp.where(firstq, dqp[0], dqp[1])
            dk = dk + jnp.where(firstk, dkp[0], dkp[1])
            dv = dv + jnp.where(firstk, dvp[0], dvp[1])
            return dk, dv, dcks[0], dcks[1]

        z2 = jnp.zeros((tk, LANES), F32)
        z1 = jnp.zeros((1, tk), F32)
        carry = (z2, z2, z1, z1)
        i0 = (j * tk) // tq
        for ii in range(nmask):
            carry = step(i0 + ii, carry, True)
        dk, dv, dck0, dck1 = lax.fori_loop(i0 + nmask, nq, lambda i, c: step(i, c, False), carry)
        dk_ref[...] = dk
        dv_ref[...] = dv.astype(dv_ref.dtype)
        dck_ref[0] = dck0
        dck_ref[1] = dck1

    kblk = pl.BlockSpec((tk, LANES), lambda h, j: (j, h))
    full = pl.BlockSpec((S, LANES), lambda h, j: (0, h))
    dofull = pl.BlockSpec((S, LANES), lambda h, j: (0, do_col0 // LANES + h))
    rowt = pl.BlockSpec((2, 1, tk), lambda h, j: (h, 0, j))
    dvblk = pl.BlockSpec((tk, LANES), lambda h, j: (j, COL_V // LANES + h))
    return pl.pallas_call(
        body, grid=(W // LANES, S // tk),
        in_specs=[full, kblk, kblk, full, rowt, full, dofull, full],
        out_specs=[full, kblk, dvblk, rowt],
        out_shape=[jax.ShapeDtypeStruct((S, W), F32), jax.ShapeDtypeStruct((S, W), F32),
                   jax.ShapeDtypeStruct((S, IN_COLS_PAD), BF16), jax.ShapeDtypeStruct((2 * (W // LANES), 1, S), F32)],
        compiler_params=_params(("parallel", "arbitrary")), name="flash_bwd")(qs, kn, vb, cq, ck, o_fine, do, lse)


XATTN_SCALE = XATTN_DIM ** -0.5


def _xq_norm(q, g):
    return _rms(q, g) * XATTN_SCALE


def _xattn_fwd(xq, kv, gq, gk):
    S = xq.shape[0]
    Mm = kv.shape[0]
    Dh = XATTN_DIM
    tq = _pick(S, (512, 256))

    def body(q_ref, k_ref, v_ref, gq_ref, gk_ref, o_ref):
        qn = _xq_norm(q_ref[...], gq_ref[...]).astype(BF16)
        kn = _rms(k_ref[...], gk_ref[...]).astype(BF16)
        s = _mxu(qn, kn, _NT)
        m = jnp.max(s, axis=-1, keepdims=True)
        p = jnp.exp(s - m)
        l = jnp.sum(p, axis=-1, keepdims=True)
        o_ref[...] = (_mxu(p.astype(BF16), v_ref[...].astype(BF16)) / l).astype(o_ref.dtype)

    vec = pl.BlockSpec((1, Dh), lambda h, i: (0, 0))
    return pl.pallas_call(
        body, grid=(XATTN_HEADS, S // tq),
        in_specs=[pl.BlockSpec((tq, Dh), lambda h, i: (i, h)), pl.BlockSpec((Mm, Dh), lambda h, i: (0, h)),
                  pl.BlockSpec((Mm, Dh), lambda h, i: (0, XATTN_HEADS + h)), vec, vec],
        out_specs=pl.BlockSpec((tq, Dh), lambda h, i: (i, h)),
        out_shape=jax.ShapeDtypeStruct((S, XATTN_HEADS * Dh), BF16),
        compiler_params=_params(("parallel", "parallel")), name="xattn_fwd")(xq, kv, kv, gq, gk)


def _xattn_bwd(xq, kv, gq, gk, do):
    S = xq.shape[0]
    Mm = kv.shape[0]
    Dh = XATTN_DIM
    tq = _pick(S, (512, 256))
    nq = S // tq

    def body(q_ref, k_ref, v_ref, gq_ref, gk_ref, do_ref, dq_ref, dk_ref, dv_ref, dgq_ref, dgk_ref, dkn_acc, dv_acc):
        h = pl.program_id(0)
        i = pl.program_id(1)

        @pl.when((h == 0) & (i == 0))
        def _():
            dgq_ref[...] = jnp.zeros_like(dgq_ref)
            dgk_ref[...] = jnp.zeros_like(dgk_ref)

        @pl.when(i == 0)
        def _():
            dkn_acc[...] = jnp.zeros_like(dkn_acc)
            dv_acc[...] = jnp.zeros_like(dv_acc)

        qn32, vq = jax.vjp(_xq_norm, q_ref[...], gq_ref[...])
        kn32, vk = jax.vjp(_rms, k_ref[...], gk_ref[...])
        qn = qn32.astype(BF16)
        kn = kn32.astype(BF16)
        vb = v_ref[...].astype(BF16)
        s = _mxu(qn, kn, _NT)
        m = jnp.max(s, axis=-1, keepdims=True)
        p = jnp.exp(s - m)
        p = p / jnp.sum(p, axis=-1, keepdims=True)
        dob = do_ref[...].astype(BF16)
        dp = _mxu(dob, vb, _NT)
        delta = jnp.sum(p * dp, axis=-1, keepdims=True)
        ds = (p * (dp - delta)).astype(BF16)
        dv_acc[...] += _mxu(p.astype(BF16), dob, _TN)
        dkn_acc[...] += _mxu(ds, qn, _TN)
        dq, dgq = vq(_mxu(ds, kn))
        dq_ref[...] = dq.astype(dq_ref.dtype)
        dgq_ref[...] += dgq

        @pl.when(i == nq - 1)
        def _():
            dk, dgk = vk(dkn_acc[...])
            dk_ref[...] = dk.astype(dk_ref.dtype)
            dv_ref[...] = dv_acc[...].astype(dv_ref.dtype)
            dgk_ref[...] += dgk

    vec = pl.BlockSpec((1, Dh), lambda h, i: (0, 0))
    qblk = pl.BlockSpec((tq, Dh), lambda h, i: (i, h))
    kblk = pl.BlockSpec((Mm, Dh), lambda h, i: (0, h))
    vblk = pl.BlockSpec((Mm, Dh), lambda h, i: (0, XATTN_HEADS + h))
    return pl.pallas_call(
        body, grid=(XATTN_HEADS, nq),
        in_specs=[qblk, kblk, vblk, vec, vec, qblk],
        out_specs=[qblk, kblk, kblk, vec, vec],
        out_shape=[jax.ShapeDtypeStruct((S, XATTN_HEADS * Dh), BF16),
                   jax.ShapeDtypeStruct((Mm, XATTN_HEADS * Dh), BF16),
                   jax.ShapeDtypeStruct((Mm, XATTN_HEADS * Dh), BF16),
                   jax.ShapeDtypeStruct((1, Dh), F32), jax.ShapeDtypeStruct((1, Dh), F32)],
        scratch_shapes=[pltpu.VMEM((Mm, Dh), F32), pltpu.VMEM((Mm, Dh), F32)],
        compiler_params=_params(("arbitrary", "arbitrary")), name="xattn_bwd")(xq, kv, kv, gq, gk, do)


def _loss_head(y, target):
    S, D = y.shape
    tr = _pick(S, (512, 256))

    def body(y_ref, t_ref, dy_ref, loss_ref):
        @pl.when(pl.program_id(0) == 0)
        def _():
            loss_ref[...] = jnp.zeros_like(loss_ref)

        err = y_ref[...] - t_ref[...]
        dy_ref[...] = err * (1.0 / D)
        loss_ref[...] += jnp.sum(err * err) * (0.5 / D)

    row = pl.BlockSpec((tr, D), lambda i: (i, 0))
    return pl.pallas_call(
        body, grid=(S // tr,), in_specs=[row, row],
        out_specs=[row, pl.BlockSpec((1, LANES), lambda i: (0, 0))],
        out_shape=[jax.ShapeDtypeStruct((S, D), F32), jax.ShapeDtypeStruct((1, LANES), F32)],
        compiler_params=_params(("arbitrary",)), name="loss_head")(y, target)


def _row_tile(R, C):
    for tr in (1024, 512, 256, 128, 64, 32, 16, 8):
        if R % tr == 0 and tr * C * 4 <= (1 << 20):
            return tr
    return R


def _chip_sum(own, from_chips, name):
    R, C = own.shape
    tr = _row_tile(R, C)

    def body(own_ref, a_ref, b_ref, c_ref, o_ref):
        total = ((own_ref[...].astype(F32) + a_ref[...].astype(F32)) + b_ref[...].astype(F32)) + c_ref[...].astype(F32)
        o_ref[...] = total.astype(o_ref.dtype)

    blk = pl.BlockSpec((tr, C), lambda i: (i, 0))
    slab = lambda s: pl.BlockSpec((None, tr, C), lambda i: (s, i, 0))
    return pl.pallas_call(
        body, grid=(R // tr,), in_specs=[blk, slab(0), slab(1), slab(2)], out_specs=blk,
        out_shape=jax.ShapeDtypeStruct((R, C), BF16),
        compiler_params=_params(("parallel",)), name=name)(own, from_chips, from_chips, from_chips)


def _adamw(w, g_mine, g_sibling, m, v, name):
    R, C = w.shape
    tr = _row_tile(R, C)
    c1 = 1.0 - ADAM_B1 ** ADAM_STEP
    c2 = 1.0 - ADAM_B2 ** ADAM_STEP

    def body(w_ref, ga_ref, gb_ref, m_ref, v_ref, g_ref, d_ref, mo_ref, vo_ref):
        g_t = ga_ref[...].astype(F32) + gb_ref[...].astype(F32)
        m_new = ADAM_B1 * m_ref[...] + (1.0 - ADAM_B1) * g_t
        v_new = ADAM_B2 * v_ref[...] + (1.0 - ADAM_B2) * (g_t * g_t)
        g_ref[...] = g_t
        d_ref[...] = -ADAM_LR * ((m_new / c1) / (jnp.sqrt(v_new / c2) + ADAM_EPS) + ADAM_WD * w_ref[...])
        mo_ref[...] = m_new
        vo_ref[...] = v_new

    blk = pl.BlockSpec((tr, C), lambda i: (i, 0))
    return pl.pallas_call(
        body, grid=(R // tr,), in_specs=[blk] * 5, out_specs=[blk] * 4,
        out_shape=[jax.ShapeDtypeStruct((R, C), F32)] * 4,
        compiler_params=_params(("parallel",)), name=name)(w, g_mine, g_sibling, m, v)


SSM_INNER = SSM_HEADS * HEAD_DIM
CONV_DIM = SSM_INNER + 2 * SSM_GROUPS * SSM_STATE
ATTN_WIDTH = ATTN_HEADS * HEAD_DIM
MIX_WIDTH = SSM_INNER + ATTN_WIDTH
COL_Z = 0
COL_XBC = COL_Z + SSM_INNER
COL_Q = COL_XBC + CONV_DIM
COL_K = COL_Q + ATTN_WIDTH
COL_V = COL_K + ATTN_WIDTH
COL_DT = COL_V + ATTN_WIDTH
COL_F = COL_DT + SSM_HEADS
IN_COLS = COL_F + ATTN_HEADS
IN_COLS_PAD = -(-IN_COLS // LANES) * LANES
REF_COL_DT = COL_Q
SHARD_COLS = IN_COLS // N_CHIPS
_COL_RANGES = ((0, REF_COL_DT, 0), (REF_COL_DT + SSM_HEADS, COL_F, COL_Q), (REF_COL_DT, REF_COL_DT + SSM_HEADS, COL_DT),
               (COL_F, IN_COLS, COL_F))


def _w_in_from_shards(g):
    parts = []
    for lo, hi, _ in _COL_RANGES:
        while lo < hi:
            j = lo // SHARD_COLS
            end = min(hi, (j + 1) * SHARD_COLS)
            parts.append(g[j][:, lo - j * SHARD_COLS:end - j * SHARD_COLS])
            lo = end
    parts.append(jnp.zeros((g.shape[1], IN_COLS_PAD - IN_COLS), g.dtype))
    return jnp.concatenate(parts, axis=1)


def _w_in_to_shards(w):
    shards = []
    for j in range(N_CHIPS):
        parts = []
        for lo, hi, here in sorted(_COL_RANGES):
            a, b = max(lo, j * SHARD_COLS), min(hi, (j + 1) * SHARD_COLS)
            if a < b:
                parts.append(w[:, here + a - lo:here + b - lo])
        shards.append(jnp.concatenate(parts, axis=1))
    return jnp.stack(shards)


def _add_residual(acc, res):
    return (res + acc,)


def _relu2(acc):
    r = jnp.maximum(acc, 0.0)
    return acc, r * r


def _relu2_bwd(acc, a):
    return (acc * (2.0 * jnp.maximum(a, 0.0)),)


def _layer_fwd_bwd(x, mem, target, w_in, p, late_weights, send_late_grads, send_w_in_grad):
    S = x.shape[0]
    hd3 = lambda a: a.reshape(SSM_HEADS, 1, 1)

    h1 = _rmsnorm_fwd(x, p["g_mix"], "norm_mix")
    proj = _mm(h1, w_in, "nn", "in_proj")
    xbc = _conv_fwd(proj, COL_XBC, CONV_DIM, p["conv_w"], p["conv_b"])
    dt_hm = proj[:, COL_DT:COL_DT + SSM_HEADS].T[:, :, None]
    ssd_par = (hd3(p["dt_bias"]), hd3(p["a_log"]), hd3(p["d_skip"]), p["ssm_norm_w"])
    mixed, hs = _ssd_fwd(xbc, proj, dt_hm, *ssd_par)
    f_raw = proj[:, COL_F:COL_F + ATTN_HEADS]
    gq2 = jnp.tile(p["g_q"], (1, 2))
    gk2 = jnp.tile(p["g_k"], (1, 2))
    qs, kn, vb = _qk_prep_fwd(proj, gq2, gk2)
    cum, cq = _logf_cumsum_fwd(f_raw, p["f_bias"])
    ck = cum.T[:, None, :]
    mixed, o_fine, lse = _flash_fwd(qs, kn, vb, cq, ck, mixed)
    W = late_weights((mixed,))
    x1 = _mm(mixed, W["w_out"], "nn", "out_proj", epilogue=_add_residual, extras=(x,))
    h2 = _rmsnorm_fwd(x1, p["g_xattn"], "norm_xattn")
    mem_n = _rmsnorm_fwd(mem, p["g_mem"], "norm_mem")
    xq = _mm(h2, W["xq_w"], "nn", "xq_proj")
    kv = _mm(mem_n, W["xkv_w"], "nn", "xkv_proj", b_chunks=N_CHIPS)
    xo = _xattn_fwd(xq, kv, p["xg_q"], p["xg_k"])
    x2 = _mm(xo, W["xo_w"], "nn", "xo_proj", epilogue=_add_residual, extras=(x1,))
    h3 = _rmsnorm_fwd(x2, p["g_mlp"], "norm_mlp")
    a, act = _mm(h3, W["w_up"], "nn", "mlp_up", out_dtypes=(F32, BF16), epilogue=_relu2, b_chunks=N_CHIPS)
    x3 = _mm(act, W["w_down"], "nn", "mlp_down", epilogue=_add_residual, extras=(x2,))
    dy, loss_row = _loss_head(x3, target)

    gW, gp = {}, {}
    da = _mm(dy, W["w_down"], "nt", "d_act", out_dtypes=(BF16,), epilogue=_relu2_bwd, extras=(a,))
    gW["w_down"] = _mm(act, dy, "tn", "g_w_down", out_dtypes=(BF16,))
    gW["w_up"] = _mm(h3, da, "tn", "g_w_up", out_dtypes=(BF16,), out_chunks=N_CHIPS)
    dh3 = _mm(da, W["w_up"], "nt", "d_h3", b_chunks=N_CHIPS)
    dx2, gp["g_mlp"] = _rmsnorm_bwd(x2, p["g_mlp"], dh3, dy, "norm_mlp_bwd")
    dxo = _mm(dx2, W["xo_w"], "nt", "d_xo", out_dtypes=(BF16,))
    gW["xo_w"] = _mm(xo, dx2, "tn", "g_xo_w", out_dtypes=(BF16,))
    dxq, dk_x, dv_x, gp["xg_q"], gp["xg_k"] = _xattn_bwd(xq, kv, p["xg_q"], p["xg_k"], dxo)
    dkv = jnp.concatenate([dk_x, dv_x], axis=-1)
    gW["xq_w"] = _mm(h2, dxq, "tn", "g_xq_w", out_dtypes=(BF16,))
    dh2 = _mm(dxq, W["xq_w"], "nt", "d_h2")
    gW["xkv_w"] = _mm(mem_n, dkv, "tn", "g_xkv_w", out_dtypes=(BF16,), out_chunks=N_CHIPS)
    dmem_n = _mm(dkv, W["xkv_w"], "nt", "d_mem_n", b_chunks=N_CHIPS)
    _, gp["g_mem"] = _rmsnorm_bwd(mem, p["g_mem"], dmem_n, None, "norm_mem_bwd")
    dx1, gp["g_xattn"] = _rmsnorm_bwd(x1, p["g_xattn"], dh2, dx2, "norm_xattn_bwd")
    dmixed = _mm(dx1, W["w_out"], "nt", "d_mixed")
    gW["w_out"] = _mm(mixed, dx1, "tn", "g_w_out", out_dtypes=(BF16,))
    token = send_late_grads(gW)
    dqs, dkn, dproj, dck = _flash_bwd(qs, kn, vb, cq, ck + token[:1, :1], o_fine, dmixed, SSM_INNER, lse)
    dproj, dgq2 = _pair_norm_bwd(proj, COL_Q, gq2, ATTN_SCALE, dqs, dproj, "q_norm_bwd")
    dproj, dgk2 = _pair_norm_bwd(proj, COL_K, gk2, 1.0, dkn, dproj, "k_norm_bwd")
    gp["g_q"] = dgq2[:, :HEAD_DIM] + dgq2[:, HEAD_DIM:]
    gp["g_k"] = dgk2[:, :HEAD_DIM] + dgk2[:, HEAD_DIM:]
    df, gp["f_bias"] = _logf_cumsum_bwd(f_raw, p["f_bias"], dck[:, 0, :].T)
    dxs, dproj, dB, dC, ddt, ddtb, dalog, ddsk, gp["ssm_norm_w"] = _ssd_bwd(xbc, proj, dt_hm, *ssd_par, hs, dmixed, dproj)
    gp["dt_bias"] = ddtb.reshape(1, SSM_HEADS)
    gp["a_log"] = dalog.reshape(1, SSM_HEADS)
    gp["d_skip"] = ddsk.reshape(1, SSM_HEADS)
    dproj, dconv_w, gp["conv_b"] = _conv_bwd(proj, COL_XBC, CONV_DIM, p["conv_w"], p["conv_b"], (dxs, dB, dC), dproj)
    gp["conv_w"] = dconv_w[:CONV_WIDTH]
    tail = jnp.concatenate([ddt[:, :, 0].T, df, jnp.zeros((S, IN_COLS_PAD - IN_COLS), F32)], axis=-1).astype(BF16)
    dproj = lax.dynamic_update_slice(dproj, tail, (0, COL_DT))
    token = send_w_in_grad(_mm(h1, dproj, "tn", "g_w_in", out_dtypes=(BF16,)))
    dh1 = _mm(dproj, w_in, "nt", "d_h1")
    dx, gp["g_mix"] = _rmsnorm_bwd(x, p["g_mix"] + token[:1, :1], dh1, dx1, "norm_mix_bwd")
    return loss_row, dx, gp


_ANY = pl.BlockSpec(memory_space=pl.ANY)


def _place():
    x, y, c = lax.axis_index("x"), lax.axis_index("y"), lax.axis_index("c")
    chips = [(1 - x, y), (x, 1 - y), (1 - x, 1 - y)]
    return x, y, c, chips


def _chip_index(px, py):
    return 2 * px + py


def _all_gather_chips(split, whole):
    ns, nw = len(split), len(whole)
    n = ns + nw

    def body(*refs):
        ins, outs = refs[:n], refs[n:2 * n]
        send_ici, recv_ici, send_d2d, recv_d2d = refs[2 * n:]
        x, y, c, chips = _place()
        me = _chip_index(x, y)
        sib = (x, y, 1 - c)

        def ici(k, j, src, dst):
            return pltpu.make_async_remote_copy(src_ref=src, dst_ref=dst, send_sem=send_ici.at[3 * k + j],
                                                recv_sem=recv_ici.at[3 * k + j], device_id=(*chips[j], c),
                                                device_id_type=MESH)

        def d2d(k, j, piece):
            return pltpu.make_async_remote_copy(src_ref=piece, dst_ref=piece, send_sem=send_d2d.at[3 * k + j],
                                                recv_sem=recv_d2d.at[3 * k + j], device_id=sib, device_id_type=MESH)

        sends = []
        for k in range(n):
            for j in range(3):
                if k < ns:
                    sends.append(ici(k, j, ins[k].at[c], outs[k].at[me, c]))
                else:
                    sends.append(ici(k, j, ins[k], outs[k].at[me]))
                sends[-1].start()
        passed = []
        for k in range(n):
            for j in range(3):
                src_chip = _chip_index(*chips[j])
                if k < ns:
                    ici(k, j, ins[k].at[c], outs[k].at[src_chip, c]).wait_recv()
                    passed.append(d2d(k, j, outs[k].at[src_chip, c]))
                    passed[-1].start()
                else:
                    ici(k, j, ins[k], outs[k].at[src_chip]).wait_recv()
        for k in range(ns):
            for j in range(3):
                d2d(k, j, outs[k].at[_chip_index(*chips[j]), 1 - c]).wait_recv()
        for cp in sends + passed:
            cp.wait_send()

    arrs = list(split) + list(whole)
    return pl.pallas_call(
        body, in_specs=[_ANY] * n, out_specs=[_ANY] * n,
        out_shape=[jax.ShapeDtypeStruct((N_CHIPS,) + a.shape, a.dtype) for a in arrs],
        scratch_shapes=[pltpu.SemaphoreType.DMA((3 * n,)), pltpu.SemaphoreType.DMA((3 * n,)),
                        pltpu.SemaphoreType.DMA((3 * ns,)), pltpu.SemaphoreType.DMA((3 * ns,))],
        name="all_gather_chips")(*arrs)


def _sibling_swap(arrs, name):
    n = len(arrs)

    def body(*refs):
        ins, outs = refs[:n], refs[n:2 * n]
        send_sem, recv_sem = refs[2 * n:]
        x, y, c, _ = _place()
        copies = [pltpu.make_async_remote_copy(src_ref=ins[k], dst_ref=outs[k], send_sem=send_sem.at[k],
                                               recv_sem=recv_sem.at[k], device_id=(x, y, 1 - c), device_id_type=MESH)
                  for k in range(n)]
        for q in copies:
            q.start()
        for q in copies:
            q.wait()

    return pl.pallas_call(
        body, in_specs=[_ANY] * n, out_specs=[_ANY] * n,
        out_shape=[jax.ShapeDtypeStruct(a.shape, a.dtype) for a in arrs],
        scratch_shapes=[pltpu.SemaphoreType.DMA((n,)), pltpu.SemaphoreType.DMA((n,))],
        name=name)(*arrs)


_HBM = pl.BlockSpec(memory_space=pltpu.HBM)
_SEM = pl.BlockSpec(memory_space=pltpu.SEMAPHORE)
_SPLIT_EFFECT = pltpu.SideEffectType.DATAFLOW_SIDE_EFFECTING


class _Split(NamedTuple):
    send_sems: jax.Array
    recv_sems: jax.Array
    sources: tuple
    lands: tuple
    token: jax.Array


def _split_copies(kind, srcs, lands, send_sems, recv_sems):
    x, y, c, chips = _place()
    me = _chip_index(x, y)
    copies = []
    for k in range(len(srcs)):
        for j in range(3):
            if kind == "gather":
                src, dst = srcs[k], lands[k].at[me]
            else:
                src, dst = srcs[k].at[_chip_index(*chips[j])], lands[k].at[j]
            copies.append(pltpu.make_async_remote_copy(
                src_ref=src, dst_ref=dst, send_sem=send_sems.at[3 * k + j], recv_sem=recv_sems.at[3 * k + j],
                device_id=(*chips[j], c), device_id_type=MESH))
    return copies


def _split_start(name, sources, kind, after):
    n = len(sources)
    if kind == "gather":
        lands = [lax.empty((N_CHIPS,) + s.shape, s.dtype) for s in sources]
    else:
        lands = [lax.empty((3,) + s.shape[1:], s.dtype) for s in sources]
    deps = [] if after is None else [after]

    def body(*refs):
        srcs, lnds = refs[:n], refs[n:2 * n]
        send_sems, recv_sems = refs[2 * n + len(deps)], refs[2 * n + len(deps) + 1]
        for cp in _split_copies(kind, srcs, lnds, send_sems, recv_sems):
            cp.start()
        refs[-1][...] = jnp.zeros_like(refs[-1])

    hbm = lambda a: pltpu.with_memory_space_constraint(a, pltpu.HBM)
    outs = pl.pallas_call(
        body, name=name,
        in_specs=[_HBM] * (2 * n) + [_ANY] * len(deps),
        out_specs=[_SEM, _SEM] + [_HBM] * (2 * n) + [pl.BlockSpec(memory_space=pltpu.VMEM)],
        out_shape=[pltpu.SemaphoreType.DMA((3 * n,)), pltpu.SemaphoreType.DMA((3 * n,))]
        + [pltpu.HBM(a.shape, a.dtype) for a in list(sources) + lands] + [jax.ShapeDtypeStruct((8, LANES), F32)],
        input_output_aliases={k: 2 + k for k in range(2 * n)},
        compiler_params=pltpu.CompilerParams(has_side_effects=_SPLIT_EFFECT),
    )(*[hbm(s) for s in sources], *[hbm(l) for l in lands], *deps)
    return _Split(outs[0], outs[1], tuple(outs[2:2 + n]), tuple(outs[2 + n:2 + 2 * n]), outs[-1])


def _split_wait(name, h, kind, after):
    n = len(h.sources)

    def body(*refs):
        srcs, lnds = refs[:n], refs[n:2 * n]
        for cp in _split_copies(kind, srcs, lnds, refs[2 * n], refs[2 * n + 1]):
            cp.wait_send()
            cp.wait_recv()

    outs = pl.pallas_call(
        body, name=name,
        in_specs=[_HBM] * (2 * n) + [_SEM, _SEM] + [_ANY] * len(after),
        out_specs=[_HBM] * (2 * n),
        out_shape=[pltpu.HBM(a.shape, a.dtype) for a in h.sources + h.lands],
        input_output_aliases={k: k for k in range(2 * n)},
        compiler_params=pltpu.CompilerParams(has_side_effects=_SPLIT_EFFECT),
    )(*h.sources, *h.lands, h.send_sems, h.recv_sems, *after)
    return outs[:n], outs[n:]


def _all_reduce_small(vec, after):
    R, C = vec.shape

    def body(v_ref, after_ref, o_ref, buf, send_sem, recv_sem):
        x, y, c = lax.axis_index("x"), lax.axis_index("y"), lax.axis_index("c")
        me = 4 * x + 2 * y + c
        buf[me] = v_ref[...]
        copies = []
        for r in range(1, N_DEV):
            fx, fy, fc = (r >> 2) & 1, (r >> 1) & 1, r & 1
            peer = (x ^ fx, y ^ fy, c ^ fc)
            copies.append(pltpu.make_async_remote_copy(src_ref=v_ref, dst_ref=buf.at[me], send_sem=send_sem.at[r - 1],
                                                       recv_sem=recv_sem.at[r - 1], device_id=peer, device_id_type=MESH))
        for q in copies:
            q.start()
        for r in range(1, N_DEV):
            fx, fy, fc = (r >> 2) & 1, (r >> 1) & 1, r & 1
            src = 4 * (x ^ fx) + 2 * (y ^ fy) + (c ^ fc)
            pltpu.make_async_remote_copy(src_ref=v_ref, dst_ref=buf.at[src], send_sem=send_sem.at[r - 1],
                                         recv_sem=recv_sem.at[r - 1], device_id=(x, y, c), device_id_type=MESH).wait_recv()
        acc = buf[0]
        for d in range(1, N_DEV):
            acc = acc + buf[d]
        o_ref[...] = acc
        for q in copies:
            q.wait_send()

    vm = pl.BlockSpec(memory_space=pltpu.VMEM)
    return pl.pallas_call(
        body, in_specs=[vm, _ANY], out_specs=vm, out_shape=jax.ShapeDtypeStruct((R, C), F32),
        scratch_shapes=[pltpu.VMEM((N_DEV, R, C), F32), pltpu.SemaphoreType.DMA((N_DEV - 1,)),
                        pltpu.SemaphoreType.DMA((N_DEV - 1,))],
        name="all_reduce_small")(vec, after)


_INPUTS = ["x", "mem", "g_mix", "w_in", "conv_w", "conv_b", "dt_bias", "a_log", "d_skip", "ssm_norm_w", "g_q", "g_k",
           "f_bias", "w_out", "g_xattn", "g_mem", "xq_w", "xkv_w", "xg_q", "xg_k", "xo_w", "g_mlp", "w_up", "w_down"]
_WEIGHTS = _INPUTS[2:]
_BIG = ["w_in", "w_out", "xq_w", "xkv_w", "xo_w", "w_up", "w_down"]
_LATE = _BIG[1:]
_COL_SHARDED = ["w_in", "xkv_w", "w_up"]
_SMALL = [n for n in _WEIGHTS if n not in _BIG]


def _pack_rows(arrs, width):
    starts, r = [], 0
    for a in arrs:
        starts.append(r)
        r += a.shape[0]
    out = jnp.concatenate([jnp.pad(a, ((0, 0), (0, width - a.shape[1]))) for a in arrs], axis=0)
    return jnp.pad(out, ((0, -r % 8), (0, 0))), starts


def _adamw_small(summed, starts, ws, ms, vs, conv_w_index):
    n = len(ws)
    c1 = 1.0 - ADAM_B1 ** ADAM_STEP
    c2 = 1.0 - ADAM_B2 ** ADAM_STEP

    def body(s_ref, *refs):
        w_refs, m_refs, v_refs = refs[:n], refs[n:2 * n], refs[2 * n:3 * n]
        outs = refs[3 * n:]
        chip = _chip_index(lax.axis_index("x"), lax.axis_index("y"))
        for k in range(n):
            rows, cols = w_refs[k].shape
            if k == conv_w_index:
                g = s_ref[starts[k]:starts[k] + rows, pl.ds(pl.multiple_of(chip * cols, LANES), cols)]
            else:
                g = s_ref[starts[k]:starts[k] + rows, 0:cols]
            m_new = ADAM_B1 * m_refs[k][...] + (1.0 - ADAM_B1) * g
            v_new = ADAM_B2 * v_refs[k][...] + (1.0 - ADAM_B2) * (g * g)
            outs[4 * k][...] = g
            outs[4 * k + 1][...] = -ADAM_LR * ((m_new / c1) / (jnp.sqrt(v_new / c2) + ADAM_EPS) + ADAM_WD * w_refs[k][...])
            outs[4 * k + 2][...] = m_new
            outs[4 * k + 3][...] = v_new

    vm = pl.BlockSpec(memory_space=pltpu.VMEM)
    outs = pl.pallas_call(
        body, in_specs=[vm] * (1 + 3 * n), out_specs=[vm] * (4 * n),
        out_shape=[jax.ShapeDtypeStruct(a.shape, F32) for a in ws for _ in range(4)],
        name="adamw_small")(summed, *ws, *ms, *vs)
    return [outs[4 * k:4 * k + 4] for k in range(n)]


def kernel(x, mem, g_mix, w_in, conv_w, conv_b, dt_bias, a_log, d_skip, ssm_norm_w, g_q, g_k, f_bias, w_out, g_xattn, g_mem, xq_w, xkv_w, xg_q, xg_k, xo_w, g_mlp, w_up, w_down, loss_target, m_g_mix, m_w_in, m_conv_w, m_conv_b, m_dt_bias, m_a_log, m_d_skip, m_ssm_norm_w, m_g_q, m_g_k, m_f_bias, m_w_out, m_g_xattn, m_g_mem, m_xq_w, m_xkv_w, m_xg_q, m_xg_k, m_xo_w, m_g_mlp, m_w_up, m_w_down, v_g_mix, v_w_in, v_conv_w, v_conv_b, v_dt_bias, v_a_log, v_d_skip, v_ssm_norm_w, v_g_q, v_g_k, v_f_bias, v_w_out, v_g_xattn, v_g_mem, v_xq_w, v_xkv_w, v_xg_q, v_xg_k, v_xo_w, v_g_mlp, v_w_up, v_w_down):
    args = (x, mem, g_mix, w_in, conv_w, conv_b, dt_bias, a_log, d_skip, ssm_norm_w, g_q, g_k, f_bias, w_out, g_xattn,
            g_mem, xq_w, xkv_w, xg_q, xg_k, xo_w, g_mlp, w_up, w_down)
    w = dict(zip(_INPUTS, args))
    mom1 = dict(zip(_WEIGHTS, (m_g_mix, m_w_in, m_conv_w, m_conv_b, m_dt_bias, m_a_log, m_d_skip, m_ssm_norm_w, m_g_q,
                               m_g_k, m_f_bias, m_w_out, m_g_xattn, m_g_mem, m_xq_w, m_xkv_w, m_xg_q, m_xg_k, m_xo_w,
                               m_g_mlp, m_w_up, m_w_down)))
    mom2 = dict(zip(_WEIGHTS, (v_g_mix, v_w_in, v_conv_w, v_conv_b, v_dt_bias, v_a_log, v_d_skip, v_ssm_norm_w, v_g_q,
                               v_g_k, v_f_bias, v_w_out, v_g_xattn, v_g_mem, v_xq_w, v_xkv_w, v_xg_q, v_xg_k, v_xo_w,
                               v_g_mlp, v_w_up, v_w_down)))
    chip = _chip_index(lax.axis_index("x"), lax.axis_index("y"))

    shard_bf = {n: w[n][0].astype(BF16) for n in _BIG}

    def layout_for_compute(n, g):
        if n == "w_in":
            return _w_in_from_shards(g)
        return g if n in _COL_SHARDED else g.reshape(N_CHIPS * g.shape[1], g.shape[2])

    def layout_for_reduction(n, g):
        if n == "w_in":
            return _w_in_to_shards(g)
        return g if n in _COL_SHARDED else g.reshape(N_CHIPS, g.shape[0] // N_CHIPS, g.shape[1])

    halves_in = shard_bf["w_in"].reshape(2, shard_bf["w_in"].shape[0] // 2, -1)
    g_in, g_conv = _all_gather_chips([halves_in], [w["conv_w"][0]])
    g_in = lax.dynamic_update_index_in_dim(g_in, halves_in, chip, axis=0)
    g_conv = lax.dynamic_update_index_in_dim(g_conv, w["conv_w"][0], chip, axis=0)
    w_in_full = layout_for_compute("w_in", g_in.reshape(N_CHIPS, -1, g_in.shape[-1]))
    p = {n: w[n] for n in _SMALL}
    p["conv_w"] = g_conv.transpose(1, 0, 2).reshape(CONV_WIDTH, CONV_DIM)
    gather = _split_start("gather_late", [shard_bf[n] for n in _LATE], "gather", after=g_in)
    p["g_mix"] = p["g_mix"] + gather.token[:1, :1]

    def late_weights(after):
        srcs, lands = _split_wait("gather_late_wait", gather, "gather", after)
        lands = [lax.dynamic_update_index_in_dim(l, s, chip, axis=0) for l, s in zip(lands, srcs)]
        return {n: layout_for_compute(n, l) for n, l in zip(_LATE, lands)}

    scatter = {}

    def send_late_grads(grads):
        scatter["late"] = _split_start("scatter_late", [layout_for_reduction(n, grads[n]) for n in _LATE], "scatter",
                                       after=None)
        return scatter["late"].token

    def send_w_in_grad(g):
        scatter["w_in"] = _split_start("scatter_w_in", [layout_for_reduction("w_in", g)], "scatter", after=None)
        return scatter["w_in"].token

    loss_row, dx, gp = _layer_fwd_bwd(x[0], mem[0], loss_target[0], w_in_full, p, late_weights, send_late_grads,
                                      send_w_in_grad)

    grad, delta, new_m, new_v = {}, {}, {}, {}

    def finish(names, sources, from_chips, tag):
        mine = [_chip_sum(lax.dynamic_index_in_dim(s, chip, axis=0, keepdims=False), fc, "rs_chip_sum_" + n)
                for n, s, fc in zip(names, sources, from_chips)]
        for n, a, b in zip(names, mine, _sibling_swap(mine, "rs_sibling_swap_" + tag)):
            shape = w[n].shape
            res = _adamw(w[n][0], a, b, mom1[n][0], mom2[n][0], "adamw_" + n)
            grad[n], delta[n], new_m[n], new_v[n] = (r.reshape(shape) for r in res)

    finish(_LATE, *_split_wait("scatter_late_wait", scatter["late"], "scatter", (dx,)), "late")

    sources_in, from_chips_in = _split_wait("scatter_w_in_wait", scatter["w_in"], "scatter",
                                            tuple(new_v[n] for n in _LATE))

    packed, starts = _pack_rows([gp[n] for n in _SMALL] + [loss_row], CONV_DIM)
    summed = _all_reduce_small(packed, from_chips_in[0])
    loss = summed[starts[-1], 0]
    finish(["w_in"], sources_in, from_chips_in, "w_in")

    as_rows = lambda a: a.reshape(-1, a.shape[-1])
    results = _adamw_small(summed, starts, [as_rows(w[n]) for n in _SMALL], [as_rows(mom1[n]) for n in _SMALL],
                           [as_rows(mom2[n]) for n in _SMALL], _SMALL.index("conv_w"))
    for n, res in zip(_SMALL, results):
        grad[n], delta[n], new_m[n], new_v[n] = (a.reshape(w[n].shape) for a in res)

    return (loss, dx[None], *[grad[n] for n in _WEIGHTS], *[delta[n] for n in _WEIGHTS],
            *[new_m[n] for n in _WEIGHTS], *[new_v[n] for n in _WEIGHTS])
```

```python
from typing import NamedTuple

import jax
import jax.numpy as jnp
from jax import lax
from jax.experimental import pallas as pl
from jax.experimental.pallas import tpu as pltpu

F32 = jnp.float32
BF16 = jnp.bfloat16
HI = lax.Precision.HIGHEST
MESH = pl.DeviceIdType.MESH

EPS = 1e-5
CHUNK = 128
SSM_HEADS = 16
SSM_GROUPS = 2
HEADS_PER_GROUP = SSM_HEADS // SSM_GROUPS
HEAD_DIM = 64
SSM_STATE = 128
ATTN_HEADS = 16
XATTN_HEADS = 4
XATTN_DIM = 256
CONV_WIDTH = 4
CONV_COLS = 256
N_CHIPS = 4
N_DEV = 8
LANES = 128
VMEM_LIMIT = 56 * 1024 * 1024

ADAM_LR = 0.001
ADAM_B1 = 0.9
ADAM_B2 = 0.999
ADAM_EPS = 1e-08
ADAM_WD = 0.01
ADAM_STEP = 10


def _params(sem):
    return pltpu.CompilerParams(dimension_semantics=sem, vmem_limit_bytes=VMEM_LIMIT)


def _pick(n, cands):
    for c in cands:
        if n % c == 0:
            return c
    return n


def _mm(a, b, mode, name, out_dtypes=(F32,), epilogue=None, extras=(), b_chunks=1, out_chunks=1,
        tm=None, tn=None, tk=None):
    if mode == "nn":
        M, K = a.shape
        N = b.shape[-1] * b_chunks
    elif mode == "nt":
        M, K = a.shape
        N = b.shape[-2]
        assert b.shape[-1] * b_chunks == K
    else:
        K, M = a.shape
        N = b.shape[-1] * b_chunks
    tm = tm or _pick(M, (2048, 1024, 512, 256, 128))
    tn = tn or _pick(N // max(b_chunks if mode != "nt" else 1, out_chunks), (512, 640, 384, 256, 128))
    if tk is None:
        kmax = b.shape[-1] if mode == "nt" else K
        tk = kmax if kmax <= 2048 else _pick(kmax, (2048, 1920, 1152, 1024, 512))
    nk = K // tk
    assert M % tm == 0 and N % tn == 0 and K % tk == 0
    grid = (M // tm, N // tn, nk)

    if mode == "tn":
        a_spec = pl.BlockSpec((tk, tm), lambda i, j, k: (k, i))
    else:
        a_spec = pl.BlockSpec((tm, tk), lambda i, j, k: (i, k))

    def b_index(t_row, t_last, tile_last):
        if b_chunks == 1:
            return (t_row, t_last)
        q = (b.shape[-1]) // tile_last
        return (t_last // q, t_row, t_last % q)

    if mode == "nn" or mode == "tn":
        bshape = (tk, tn)
        bmap = lambda i, j, k: b_index(k, j, tn)
    else:
        bshape = (tn, tk)
        bmap = lambda i, j, k: b_index(j, k, tk)
    if b_chunks > 1:
        bshape = (None,) + bshape
    b_spec = pl.BlockSpec(bshape, bmap)

    if out_chunks == 1:
        o_spec = pl.BlockSpec((tm, tn), lambda i, j, k: (i, j))
        o_shape = (M, N)
    else:
        qo = (N // out_chunks) // tn
        o_spec = pl.BlockSpec((None, tm, tn), lambda i, j, k: (j // qo, i, j % qo))
        o_shape = (out_chunks, M, N // out_chunks)
    e_spec = pl.BlockSpec((tm, tn), lambda i, j, k: (i, j))

    dims = {"nn": (((1,), (0,)), ((), ())), "nt": (((1,), (1,)), ((), ())), "tn": (((0,), (0,)), ((), ()))}[mode]
    n_ex = len(extras)
    n_out = len(out_dtypes)

    def body(*refs):
        a_ref, b_ref = refs[0], refs[1]
        ex_refs = refs[2:2 + n_ex]
        o_refs = refs[2 + n_ex:2 + n_ex + n_out]

        def finish(acc):
            outs = epilogue(acc, *[r[...] for r in ex_refs]) if epilogue is not None else (acc,)
            for r, o in zip(o_refs, outs):
                r[...] = o.astype(r.dtype)

        part = lax.dot_general(a_ref[...].astype(BF16), b_ref[...].astype(BF16), dims,
                               preferred_element_type=F32)
        if nk == 1:
            finish(part)
        else:
            acc_ref = refs[-1]
            k = pl.program_id(2)

            @pl.when(k == 0)
            def _():
                acc_ref[...] = part

            @pl.when(k > 0)
            def _():
                acc_ref[...] += part

            @pl.when(k == nk - 1)
            def _():
                finish(acc_ref[...])

    outs = pl.pallas_call(
        body,
        grid=grid,
        in_specs=[a_spec, b_spec] + [e_spec] * n_ex,
        out_specs=[o_spec] * n_out,
        out_shape=[jax.ShapeDtypeStruct(o_shape, d) for d in out_dtypes],
        scratch_shapes=[pltpu.VMEM((tm, tn), F32)] if nk > 1 else [],
        compiler_params=_params(("parallel", "parallel", "arbitrary")),
        name=name,
    )(a, b, *extras)
    return outs[0] if n_out == 1 else outs


def _rms(x, g):
    r = lax.rsqrt(jnp.mean(x * x, axis=-1, keepdims=True) + EPS)
    return x * r * g


def _rmsnorm_fwd(x, g, name):
    R, D = x.shape
    tr = _pick(R, (512, 256))

    def body(x_ref, g_ref, o_ref):
        o_ref[...] = _rms(x_ref[...], g_ref[...]).astype(o_ref.dtype)

    return pl.pallas_call(
        body, grid=(R // tr,),
        in_specs=[pl.BlockSpec((tr, D), lambda i: (i, 0)), pl.BlockSpec((1, D), lambda i: (0, 0))],
        out_specs=pl.BlockSpec((tr, D), lambda i: (i, 0)),
        out_shape=jax.ShapeDtypeStruct((R, D), BF16),
        compiler_params=_params(("parallel",)), name=name)(x, g)


def _rmsnorm_bwd(x, g, dh, dres, name):
    R, D = x.shape
    tr = _pick(R, (256,))
    has_res = dres is not None

    def body(*refs):
        if has_res:
            x_ref, g_ref, dh_ref, dres_ref, dx_ref, dg_ref = refs
        else:
            x_ref, g_ref, dh_ref, dx_ref, dg_ref = refs
        _, vjp = jax.vjp(_rms, x_ref[...], g_ref[...])
        dx, dg = vjp(dh_ref[...])
        if has_res:
            dx = dx + dres_ref[...]
        dx_ref[...] = dx

        @pl.when(pl.program_id(0) == 0)
        def _():
            dg_ref[...] = jnp.zeros_like(dg_ref)

        dg_ref[...] += dg

    row = pl.BlockSpec((tr, D), lambda i: (i, 0))
    vec = pl.BlockSpec((1, D), lambda i: (0, 0))
    ins = [x, g, dh] + ([dres] if has_res else [])
    return pl.pallas_call(
        body, grid=(R // tr,),
        in_specs=[row, vec, row] + ([row] if has_res else []),
        out_specs=[row, vec],
        out_shape=[jax.ShapeDtypeStruct((R, D), F32), jax.ShapeDtypeStruct((1, D), F32)],
        compiler_params=_params(("arbitrary",)), name=name)(*ins)


def _shift_down(u, k):
    if k == 0:
        return u
    rows = lax.broadcasted_iota(jnp.int32, u.shape, 0)
    return jnp.where(rows >= k, pltpu.roll(u, k, axis=0), 0.0)


def _shift_up(u, k):
    if k == 0:
        return u
    n = u.shape[0]
    rows = lax.broadcasted_iota(jnp.int32, u.shape, 0)
    return jnp.where(rows < n - k, pltpu.roll(u, n - k, axis=0), 0.0)


def _conv_pre(u, w, b):
    pre = b
    for j in range(CONV_WIDTH):
        pre = pre + w[j:j + 1, :] * _shift_down(u, CONV_WIDTH - 1 - j)
    return pre


def _conv_fwd(proj, col0, ncols, conv_w, conv_b):
    S = proj.shape[0]
    cb0 = col0 // CONV_COLS

    def body(u_ref, w_ref, b_ref, o_ref):
        pre = _conv_pre(u_ref[...], w_ref[...], b_ref[...])
        o_ref[...] = pre * jax.nn.sigmoid(pre)

    return pl.pallas_call(
        body, grid=(ncols // CONV_COLS,),
        in_specs=[pl.BlockSpec((S, CONV_COLS), lambda j: (0, j + cb0)),
                  pl.BlockSpec((CONV_WIDTH, CONV_COLS), lambda j: (0, j)),
                  pl.BlockSpec((1, CONV_COLS), lambda j: (0, j))],
        out_specs=pl.BlockSpec((S, CONV_COLS), lambda j: (0, j)),
        out_shape=jax.ShapeDtypeStruct((S, ncols), F32),
        compiler_params=_params(("parallel",)), name="conv_fwd")(proj, conv_w, conv_b)


def _conv_bwd(proj, col0, ncols, conv_w, conv_b, douts, dproj):
    S = proj.shape[0]
    cb0 = col0 // CONV_COLS
    starts = [0]
    for d in douts:
        starts.append(starts[-1] + d.shape[1] // CONV_COLS)
    assert starts[-1] == ncols // CONV_COLS
    nd = len(douts)

    def body(u_ref, w_ref, b_ref, *rest):
        d_refs, (du_ref, dw_ref, db_ref) = rest[:nd], rest[nd + 1:]
        j = pl.program_id(0)
        dout = d_refs[-1][...]
        for i in range(nd - 2, -1, -1):
            dout = jnp.where(j < starts[i + 1], d_refs[i][...], dout)
        u = u_ref[...]
        w = w_ref[...]
        pre = _conv_pre(u, w, b_ref[...])
        s = jax.nn.sigmoid(pre)
        dpre = dout * (s * (1.0 + pre * (1.0 - s)))
        du = jnp.zeros_like(u)
        rows = []
        for j in range(CONV_WIDTH):
            k = CONV_WIDTH - 1 - j
            du = du + w[j:j + 1, :] * _shift_up(dpre, k)
            rows.append(jnp.sum(dpre * _shift_down(u, k), axis=0, keepdims=True))
        du_ref[...] = du.astype(du_ref.dtype)
        rows.append(jnp.zeros((8 - CONV_WIDTH, CONV_COLS), F32))
        dw_ref[...] = jnp.concatenate(rows, axis=0)
        db_ref[...] = jnp.sum(dpre, axis=0, keepdims=True)

    return pl.pallas_call(
        body, grid=(ncols // CONV_COLS,),
        in_specs=[pl.BlockSpec((S, CONV_COLS), lambda j: (0, j + cb0)),
                  pl.BlockSpec((CONV_WIDTH, CONV_COLS), lambda j: (0, j)),
                  pl.BlockSpec((1, CONV_COLS), lambda j: (0, j))]
        + [pl.BlockSpec((S, CONV_COLS), lambda j, lo=starts[i], hi=starts[i + 1]: (0, jnp.clip(j - lo, 0, hi - lo - 1)))
           for i in range(nd)] + [_ANY],
        out_specs=[pl.BlockSpec((S, CONV_COLS), lambda j: (0, j + cb0)),
                   pl.BlockSpec((8, CONV_COLS), lambda j: (0, j)),
                   pl.BlockSpec((1, CONV_COLS), lambda j: (0, j))],
        out_shape=[jax.ShapeDtypeStruct(dproj.shape, dproj.dtype),
                   jax.ShapeDtypeStruct((8, ncols), F32),
                   jax.ShapeDtypeStruct((1, ncols), F32)],
        input_output_aliases={3 + nd: 0},
        compiler_params=_params(("parallel",)), name="conv_bwd")(proj, conv_w, conv_b, *douts, dproj)


def _softplus(x):
    return jnp.maximum(x, 0.0) + jnp.log1p(jnp.exp(-jnp.abs(x)))


def _dot32(a, b, dims=(((1,), (0,)), ((), ()))):
    return lax.dot_general(a, b, dims, precision=HI, preferred_element_type=F32)


def _dotd(a, b, dims=(((1,), (0,)), ((), ()))):
    return lax.dot_general(a, b, dims, preferred_element_type=F32)


PAIRS_PER_GROUP = HEADS_PER_GROUP // 2


def _ssd_chunk(xs, Bm, Cm, z, dtr, dtb, alog, dsk, nw, h):
    L = Bm.shape[0]
    ri = lax.broadcasted_iota(jnp.int32, (L, L), 0)
    ci = lax.broadcasted_iota(jnp.int32, (L, L), 1)
    causal = ri >= ci
    tril = causal.astype(F32)
    first = _first_head(L)
    first1 = _first_head(1)
    CB = _dotd(Cm, Bm, _NT)
    gated, hnew = [], []
    ssq = jnp.zeros((L, 1), F32)
    for pp in range(len(xs)):
        dts, cums, tots, decay = [], [], [], []
        for a in range(2):
            r = 2 * pp + a
            dt = _softplus(dtr[r] + dtb[r])
            dA = dt * (-jnp.exp(alog[r]))
            acs = _dot32(tril, dA)
            cc = jnp.broadcast_to(acs, (L, L))
            decay.append(CB * jnp.exp(jnp.where(causal, cc - cc.T, -1e30)))
            dts.append(dt)
            cums.append(acs)
            tots.append(jnp.sum(dA, axis=0, keepdims=True))
        dt2 = jnp.where(first, dts[0], dts[1])
        acs2 = jnp.where(first, cums[0], cums[1])
        tot2 = jnp.where(first1, tots[0], tots[1])
        dsk2 = jnp.where(first1, dsk[2 * pp], dsk[2 * pp + 1])
        X = xs[pp] * dt2
        y = (jnp.where(first, _dotd(decay[0], X), _dotd(decay[1], X)) + jnp.exp(acs2) * _dotd(Cm, h[pp])
             + dsk2 * xs[pp])
        hnew.append(jnp.exp(tot2) * h[pp] + _dotd(Bm, X * jnp.exp(tot2 - acs2), _TN))
        g = y * (z[pp] * jax.nn.sigmoid(z[pp]))
        ssq = ssq + jnp.sum(g * g, axis=-1, keepdims=True)
        gated.append(g)
    rs = lax.rsqrt(ssq / (len(xs) * LANES) + EPS)
    return [g * rs * nw[pp] for pp, g in enumerate(gated)], hnew


def _ssd_args(xs_ref, b_ref, c_ref, z_ref, dt_ref, dtb_ref, al_ref, dsk_ref, nw_ref, h_ref):
    pairs = range(PAIRS_PER_GROUP)
    heads = range(HEADS_PER_GROUP)
    lanes = lambda ref, pp: ref[:, pp * LANES:(pp + 1) * LANES]
    return ([lanes(xs_ref, pp) for pp in pairs], b_ref[...], c_ref[...], [lanes(z_ref, pp) for pp in pairs],
            [dt_ref[r] for r in heads], [dtb_ref[r] for r in heads], [al_ref[r] for r in heads],
            [dsk_ref[r] for r in heads], [lanes(nw_ref, pp) for pp in pairs], [h_ref[pp] for pp in pairs])


def _ssd_specs(rev):
    H, N, L = HEADS_PER_GROUP, SSM_STATE, CHUNK
    gw = H * HEAD_DIM
    return dict(
        cols=lambda col0: pl.BlockSpec((L, gw), lambda g, c: (rev(c), col0 // gw + g)),
        bc=lambda first_block: pl.BlockSpec((L, N), lambda g, c: (rev(c), first_block + g)),
        dt=pl.BlockSpec((H, L, 1), lambda g, c: (g, rev(c), 0)),
        scal=pl.BlockSpec((H, 1, 1), lambda g, c: (g, 0, 0)),
        nw=pl.BlockSpec((1, gw), lambda g, c: (0, g)),
        hs=pl.BlockSpec((None, PAIRS_PER_GROUP, N, LANES), lambda g, c: (rev(c), g, 0, 0)),
        b_block=SSM_INNER // N,
    )


def _ssd_fwd(xbc, proj, dt_hm, dtb, alog, dsk, nw):
    S = xbc.shape[0]
    N, L = SSM_STATE, CHUNK
    nc = S // L
    sp = _ssd_specs(lambda c: c)

    def body(xs_ref, b_ref, c_ref, z_ref, dt_ref, dtb_ref, al_ref, dsk_ref, nw_ref, y_ref, hs_ref, h_ref):
        @pl.when(pl.program_id(1) == 0)
        def _():
            h_ref[...] = jnp.zeros_like(h_ref)

        hs_ref[...] = h_ref[...]
        out, hnew = _ssd_chunk(*_ssd_args(xs_ref, b_ref, c_ref, z_ref, dt_ref, dtb_ref, al_ref, dsk_ref, nw_ref, h_ref))
        for pp in range(PAIRS_PER_GROUP):
            y_ref[:, pp * LANES:(pp + 1) * LANES] = out[pp].astype(y_ref.dtype)
            h_ref[pp] = hnew[pp]

    return pl.pallas_call(
        body, grid=(SSM_GROUPS, nc),
        in_specs=[sp["cols"](0), sp["bc"](sp["b_block"]), sp["bc"](sp["b_block"] + SSM_GROUPS), sp["cols"](COL_Z),
                  sp["dt"], sp["scal"], sp["scal"], sp["scal"], sp["nw"]],
        out_specs=[sp["cols"](0), sp["hs"]],
        out_shape=[jax.ShapeDtypeStruct((S, MIX_WIDTH), BF16),
                   jax.ShapeDtypeStruct((nc, SSM_HEADS // 2, N, LANES), F32)],
        scratch_shapes=[pltpu.VMEM((PAIRS_PER_GROUP, N, LANES), F32)],
        compiler_params=_params(("parallel", "arbitrary")), name="ssd_fwd",
    )(xbc, xbc, xbc, proj, dt_hm, dtb, alog, dsk, nw)


def _ssd_bwd(xbc, proj, dt_hm, dtb, alog, dsk, nw, hs, dmixed, dproj):
    S = xbc.shape[0]
    N, L = SSM_STATE, CHUNK
    nc = S // L
    sp = _ssd_specs(lambda c: nc - 1 - c)

    def body(xs_ref, b_ref, c_ref, z_ref, dt_ref, dtb_ref, al_ref, dsk_ref, nw_ref, hs_ref, dy_ref, buf_ref,
             dxs_ref, dz_ref, db_ref, dc_ref, ddt_ref, ddtb_ref, dal_ref, ddsk_ref, dnw_ref, dh_ref):
        @pl.when(pl.program_id(1) == 0)
        def _():
            dh_ref[...] = jnp.zeros_like(dh_ref)
            ddtb_ref[...] = jnp.zeros_like(ddtb_ref)
            dal_ref[...] = jnp.zeros_like(dal_ref)
            ddsk_ref[...] = jnp.zeros_like(ddsk_ref)
            dnw_ref[...] = jnp.zeros_like(dnw_ref)

        pairs = range(PAIRS_PER_GROUP)
        lanes = lambda pp: slice(pp * LANES, (pp + 1) * LANES)
        _, vjp = jax.vjp(_ssd_chunk, *_ssd_args(xs_ref, b_ref, c_ref, z_ref, dt_ref, dtb_ref, al_ref, dsk_ref, nw_ref,
                                                hs_ref))
        dxs, dB, dC, dz, ddt, ddtb, dal, ddsk, dnw, dh = vjp(([dy_ref[:, lanes(pp)] for pp in pairs],
                                                              [dh_ref[pp] for pp in pairs]))
        db_ref[...] = dB
        dc_ref[...] = dC
        for pp in pairs:
            dxs_ref[:, lanes(pp)] = dxs[pp]
            dz_ref[:, lanes(pp)] = dz[pp].astype(dz_ref.dtype)
            dnw_ref[:, lanes(pp)] += dnw[pp]
            dh_ref[pp] = dh[pp]
        for r in range(HEADS_PER_GROUP):
            ddt_ref[r] = ddt[r]
            ddtb_ref[r] += ddtb[r]
            dal_ref[r] += dal[r]
            ddsk_ref[r] += ddsk[r]

    bc_out = pl.BlockSpec((L, N), lambda g, c: (nc - 1 - c, g))
    return pl.pallas_call(
        body, grid=(SSM_GROUPS, nc),
        in_specs=[sp["cols"](0), sp["bc"](sp["b_block"]), sp["bc"](sp["b_block"] + SSM_GROUPS), sp["cols"](COL_Z),
                  sp["dt"], sp["scal"], sp["scal"], sp["scal"], sp["nw"], sp["hs"], sp["cols"](0), _ANY],
        out_specs=[sp["cols"](0), sp["cols"](COL_Z), bc_out, bc_out, sp["dt"], sp["scal"], sp["scal"], sp["scal"],
                   sp["nw"]],
        input_output_aliases={11: 1},
        out_shape=[jax.ShapeDtypeStruct((S, SSM_INNER), F32), jax.ShapeDtypeStruct(dproj.shape, dproj.dtype),
                   jax.ShapeDtypeStruct((S, SSM_GROUPS * N), F32), jax.ShapeDtypeStruct((S, SSM_GROUPS * N), F32),
                   jax.ShapeDtypeStruct((SSM_HEADS, S, 1), F32),
                   jax.ShapeDtypeStruct((SSM_HEADS, 1, 1), F32), jax.ShapeDtypeStruct((SSM_HEADS, 1, 1), F32),
                   jax.ShapeDtypeStruct((SSM_HEADS, 1, 1), F32), jax.ShapeDtypeStruct((1, SSM_INNER), F32)],
        scratch_shapes=[pltpu.VMEM((PAIRS_PER_GROUP, N, LANES), F32)],
        compiler_params=_params(("parallel", "arbitrary")), name="ssd_bwd",
    )(xbc, xbc, xbc, proj, dt_hm, dtb, alog, dsk, nw, hs, dmixed, dproj)


ATTN_SCALE = HEAD_DIM ** -0.5
PREP_COLS = 512


def _first_head(rows):
    return lax.broadcasted_iota(jnp.int32, (rows, LANES), 1) < HEAD_DIM


def _pair_norm(x, g2, scale):
    first = _first_head(x.shape[0])
    sq = x * x
    ms0 = jnp.sum(jnp.where(first, sq, 0.0), axis=-1, keepdims=True) * (1.0 / HEAD_DIM)
    ms1 = jnp.sum(jnp.where(first, 0.0, sq), axis=-1, keepdims=True) * (1.0 / HEAD_DIM)
    r = jnp.where(first, lax.rsqrt(ms0 + EPS), lax.rsqrt(ms1 + EPS))
    return x * r * g2 * scale


def _qk_prep_fwd(proj, gq2, gk2):
    S = proj.shape[0]
    tq = _pick(S, (512, 256))

    def body(q_ref, k_ref, v_ref, gq_ref, gk_ref, qo_ref, ko_ref, vo_ref):
        for b in range(PREP_COLS // LANES):
            pair = slice(b * LANES, (b + 1) * LANES)
            qo_ref[:, pair] = _pair_norm(q_ref[:, pair], gq_ref[...], ATTN_SCALE).astype(BF16)
            ko_ref[:, pair] = _pair_norm(k_ref[:, pair], gk_ref[...], 1.0).astype(BF16)
        vo_ref[...] = v_ref[...].astype(BF16)

    col = lambda c0: pl.BlockSpec((tq, PREP_COLS), lambda h, i: (i, c0 // PREP_COLS + h))
    blk = pl.BlockSpec((tq, PREP_COLS), lambda h, i: (i, h))
    vec = pl.BlockSpec((1, LANES), lambda h, i: (0, 0))
    return pl.pallas_call(
        body, grid=(ATTN_WIDTH // PREP_COLS, S // tq), in_specs=[col(COL_Q), col(COL_K), col(COL_V), vec, vec],
        out_specs=[blk, blk, blk], out_shape=[jax.ShapeDtypeStruct((S, ATTN_WIDTH), BF16)] * 3,
        compiler_params=_params(("parallel", "parallel")), name="qk_prep_fwd")(proj, proj, proj, gq2, gk2)


def _pair_norm_bwd(proj, col0, g2, scale, dn, dproj, name):
    S = proj.shape[0]
    tq = _pick(S, (512, 256))

    def body(u_ref, g_ref, dn_ref, buf_ref, du_ref, dg_ref):
        @pl.when((pl.program_id(0) == 0) & (pl.program_id(1) == 0))
        def _():
            dg_ref[...] = jnp.zeros_like(dg_ref)

        for b in range(PREP_COLS // LANES):
            pair = slice(b * LANES, (b + 1) * LANES)
            _, vjp = jax.vjp(lambda u, g: _pair_norm(u, g, scale), u_ref[:, pair], g_ref[...])
            du, dg = vjp(dn_ref[:, pair])
            du_ref[:, pair] = du.astype(du_ref.dtype)
            dg_ref[...] += dg

    ublk = pl.BlockSpec((tq, PREP_COLS), lambda h, i: (i, col0 // PREP_COLS + h))
    blk = pl.BlockSpec((tq, PREP_COLS), lambda h, i: (i, h))
    vec = pl.BlockSpec((1, LANES), lambda h, i: (0, 0))
    return pl.pallas_call(
        body, grid=(ATTN_WIDTH // PREP_COLS, S // tq), in_specs=[ublk, vec, blk, _ANY], out_specs=[ublk, vec],
        out_shape=[jax.ShapeDtypeStruct(dproj.shape, dproj.dtype), jax.ShapeDtypeStruct((1, LANES), F32)],
        input_output_aliases={3: 0},
        compiler_params=_params(("arbitrary", "arbitrary")), name=name)(proj, g2, dn, dproj)


def _logf_cumsum_fwd(f_raw, f_bias):
    S, Hh = f_raw.shape
    L = CHUNK

    def body(f_ref, b_ref, o_ref, wide_ref):
        ri = lax.broadcasted_iota(jnp.int32, (L, L), 0)
        ci = lax.broadcasted_iota(jnp.int32, (L, L), 1)
        tril = (ri >= ci).astype(F32)
        carry = jnp.zeros((1, Hh), F32)
        for c in range(S // L):
            rows = slice(c * L, (c + 1) * L)
            lf = -_softplus(-(f_ref[rows, :] + b_ref[...]))
            cum = _dot32(tril, lf) + carry
            o_ref[rows, :] = cum
            for h in range(Hh):
                wide_ref[rows, h * HEAD_DIM:(h + 1) * HEAD_DIM] = jnp.broadcast_to(cum[:, h:h + 1], (L, HEAD_DIM))
            carry = cum[L - 1:L, :]

    return pl.pallas_call(
        body, out_shape=[jax.ShapeDtypeStruct((S, Hh), F32), jax.ShapeDtypeStruct((S, Hh * HEAD_DIM), F32)],
        name="logf_cumsum_fwd")(f_raw, f_bias)


def _logf_cumsum_bwd(f_raw, f_bias, dcum):
    S, Hh = f_raw.shape
    L = CHUNK

    def body(f_ref, b_ref, d_ref, df_ref, db_ref):
        ri = lax.broadcasted_iota(jnp.int32, (L, L), 0)
        ci = lax.broadcasted_iota(jnp.int32, (L, L), 1)
        triu = (ri <= ci).astype(F32)
        carry = jnp.zeros((1, Hh), F32)
        db = jnp.zeros((1, Hh), F32)
        for c in reversed(range(S // L)):
            suf = _dot32(triu, d_ref[c * L:(c + 1) * L, :]) + carry
            df = suf * jax.nn.sigmoid(-(f_ref[c * L:(c + 1) * L, :] + b_ref[...]))
            df_ref[c * L:(c + 1) * L, :] = df
            db = db + jnp.sum(df, axis=0, keepdims=True)
            carry = suf[0:1, :]
        db_ref[...] = db

    return pl.pallas_call(
        body, out_shape=[jax.ShapeDtypeStruct((S, Hh), F32), jax.ShapeDtypeStruct((1, Hh), F32)],
        name="logf_cumsum_bwd")(f_raw, f_bias, dcum)


_NT = (((1,), (1,)), ((), ()))
_TN = (((0,), (0,)), ((), ()))


def _mxu(a, b, dims=(((1,), (0,)), ((), ()))):
    return lax.dot_general(a, b, dims, preferred_element_type=F32)


def _flash_fwd(qs, kn, vb, cq, ck, mixed):
    S, W = qs.shape
    tq = tk = _pick(S, (512, 256))
    nmask = max(tq // tk, 1)

    def body(q_ref, k_ref, v_ref, cq_ref, ck_ref, buf_ref, o_ref, of_ref, lse_ref):
        i = pl.program_id(1)
        first = _first_head(tq)
        q2 = q_ref[...]
        zero = jnp.zeros_like(q2)
        qa = (jnp.where(first, q2, zero), jnp.where(first, zero, q2))
        cqa = (cq_ref[:, 0:1], cq_ref[:, HEAD_DIM:HEAD_DIM + 1])
        row0 = i * tq

        def step(j, carry, masked):
            ms, ls, acc, rem = carry
            off = pl.multiple_of(j * tk, tk)
            k = k_ref[pl.ds(off, tk), :]
            v = v_ref[pl.ds(off, tk), :]
            new_m, new_l, alphas, pvs, prs = [], [], [], [], []
            for a in range(2):
                s = _mxu(qa[a], k, _NT) + cqa[a] - ck_ref[a, :, pl.ds(off, tk)]
                if masked:
                    ri = lax.broadcasted_iota(jnp.int32, (tq, tk), 0) + row0
                    ci = lax.broadcasted_iota(jnp.int32, (tq, tk), 1) + off
                    s = jnp.where(ri >= ci, s, -1e30)
                m_new = jnp.maximum(ms[a], jnp.max(s, axis=-1, keepdims=True))
                alpha = jnp.exp(ms[a] - m_new)
                p = jnp.exp(s - m_new)
                new_l.append(alpha * ls[a] + jnp.sum(p, axis=-1, keepdims=True))
                new_m.append(m_new)
                alphas.append(alpha)
                p_hi = p.astype(BF16)
                pvs.append(_mxu(p_hi, v))
                prs.append(_mxu((p - p_hi.astype(F32)).astype(BF16), v))
            al = jnp.where(first, alphas[0], alphas[1])
            acc = al * acc + jnp.where(first, pvs[0], pvs[1])
            rem = al * rem + jnp.where(first, prs[0], prs[1])
            return tuple(new_m), tuple(new_l), acc, rem

        neg = jnp.full((tq, 1), -1e30, F32)
        z1 = jnp.zeros((tq, 1), F32)
        z2 = jnp.zeros((tq, LANES), F32)
        carry = ((neg, neg), (z1, z1), z2, z2)
        n_full = (i * tq) // tk
        carry = lax.fori_loop(0, n_full, lambda j, c: step(j, c, False), carry)
        for jj in range(nmask):
            carry = step(n_full + jj, carry, True)
        ms, ls, acc, rem = carry
        linv = jnp.where(first, 1.0 / ls[0], 1.0 / ls[1])
        o_ref[...] = (acc * linv).astype(o_ref.dtype)
        of_ref[...] = (acc + rem) * linv
        lse_ref[...] = jnp.where(first, ms[0] + jnp.log(ls[0]), ms[1] + jnp.log(ls[1]))

    qblk = pl.BlockSpec((tq, LANES), lambda h, i: (i, h))
    full = pl.BlockSpec((S, LANES), lambda h, i: (0, h))
    return pl.pallas_call(
        body, grid=(W // LANES, S // tq),
        in_specs=[qblk, full, full, qblk, pl.BlockSpec((2, 1, S), lambda h, i: (h, 0, 0)), _ANY],
        out_specs=[pl.BlockSpec((tq, LANES), lambda h, i: (i, SSM_INNER // LANES + h)), qblk, qblk],
        out_shape=[jax.ShapeDtypeStruct(mixed.shape, mixed.dtype), jax.ShapeDtypeStruct((S, W), F32),
                   jax.ShapeDtypeStruct((S, W), F32)],
        input_output_aliases={5: 0},
        compiler_params=_params(("parallel", "parallel")), name="flash_fwd")(qs, kn, vb, cq, ck, mixed)


def _flash_bwd(qs, kn, vb, cq, ck, o_fine, do, do_col0, lse):
    S, W = qs.shape
    tq = tk = _pick(S, (512, 256))
    nq = S // tq
    nmask = max(tk // tq, 1)

    def body(q_ref, k_ref, v_ref, cq_ref, ck_ref, of_ref, do_ref, lse_ref, dq_ref, dk_ref, dv_ref, dck_ref):
        j = pl.program_id(1)

        @pl.when(j == 0)
        def _():
            dq_ref[...] = jnp.zeros_like(dq_ref)

        firstk = _first_head(tk)
        firstq = _first_head(tq)
        k2 = k_ref[...]
        v2 = v_ref[...]
        zk = jnp.zeros_like(k2)
        ka = (jnp.where(firstk, k2, zk), jnp.where(firstk, zk, k2))
        va = (jnp.where(firstk, v2, zk), jnp.where(firstk, zk, v2))
        cka = (ck_ref[0], ck_ref[1])
        col0 = j * tk

        def step(i, carry, masked):
            dk, dv, dck0, dck1 = carry
            dcks = [dck0, dck1]
            off = pl.multiple_of(i * tq, tq)
            rows = pl.ds(off, tq)
            q2 = q_ref[rows, :]
            dob = do_ref[rows, :].astype(BF16)
            prod = dob.astype(F32) * of_ref[rows, :]
            dkp, dvp, dqp = [], [], []
            for a in range(2):
                lane = pl.ds(a * HEAD_DIM, 1)
                s = _mxu(q2, ka[a], _NT) + cq_ref[rows, lane] - cka[a]
                if masked:
                    ri = lax.broadcasted_iota(jnp.int32, (tq, tk), 0) + off
                    ci = lax.broadcasted_iota(jnp.int32, (tq, tk), 1) + col0
                    s = jnp.where(ri >= ci, s, -1e30)
                p = jnp.exp(s - lse_ref[rows, lane])
                dp = _mxu(dob, va[a], _NT)
                own = jnp.where(firstq, prod, 0.0) if a == 0 else jnp.where(firstq, 0.0, prod)
                ds = p * (dp - jnp.sum(own, axis=-1, keepdims=True))
                dsb = ds.astype(BF16)
                dvp.append(_mxu(p.astype(BF16), dob, _TN))
                dkp.append(_mxu(dsb, q2, _TN))
                dqp.append(_mxu(dsb, k2))
                dcks[a] = dcks[a] - jnp.sum(ds, axis=0, keepdims=True)
            dq_ref[rows, :] += jnp.where(firstq, dqp[0], dqp[1])
            dk = dk + jnp.where(firstk, dkp[0], dkp[1])
            dv = dv + jnp.where(firstk, dvp[0], dvp[1])
            return dk, dv, dcks[0], dcks[1]

        z2 = jnp.zeros((tk, LANES), F32)
        z1 = jnp.zeros((1, tk), F32)
        carry = (z2, z2, z1, z1)
        i0 = (j * tk) // tq
        for ii in range(nmask):
            carry = step(i0 + ii, carry, True)
        dk, dv, dck0, dck1 = lax.fori_loop(i0 + nmask, nq, lambda i, c: step(i, c, False), carry)
        dk_ref[...] = dk
        dv_ref[...] = dv.astype(dv_ref.dtype)
        dck_ref[0] = dck0
        dck_ref[1] = dck1

    kblk = pl.BlockSpec((tk, LANES), lambda h, j: (j, h))
    full = pl.BlockSpec((S, LANES), lambda h, j: (0, h))
    dofull = pl.BlockSpec((S, LANES), lambda h, j: (0, do_col0 // LANES + h))
    rowt = pl.BlockSpec((2, 1, tk), lambda h, j: (h, 0, j))
    dvblk = pl.BlockSpec((tk, LANES), lambda h, j: (j, COL_V // LANES + h))
    return pl.pallas_call(
        body, grid=(W // LANES, S // tk),
        in_specs=[full, kblk, kblk, full, rowt, full, dofull, full],
        out_specs=[full, kblk, dvblk, rowt],
        out_shape=[jax.ShapeDtypeStruct((S, W), F32), jax.ShapeDtypeStruct((S, W), F32),
                   jax.ShapeDtypeStruct((S, IN_COLS_PAD), BF16), jax.ShapeDtypeStruct((2 * (W // LANES), 1, S), F32)],
        compiler_params=_params(("parallel", "arbitrary")), name="flash_bwd")(qs, kn, vb, cq, ck, o_fine, do, lse)


XATTN_SCALE = XATTN_DIM ** -0.5


def _xq_norm(q, g):
    return _rms(q, g) * XATTN_SCALE


def _xattn_fwd(xq, kv, gq, gk):
    S = xq.shape[0]
    Mm = kv.shape[0]
    Dh = XATTN_DIM
    tq = _pick(S, (512, 256))

    def body(q_ref, k_ref, v_ref, gq_ref, gk_ref, o_ref):
        qn = _xq_norm(q_ref[...], gq_ref[...]).astype(BF16)
        kn = _rms(k_ref[...], gk_ref[...]).astype(BF16)
        s = _mxu(qn, kn, _NT)
        m = jnp.max(s, axis=-1, keepdims=True)
        p = jnp.exp(s - m)
        l = jnp.sum(p, axis=-1, keepdims=True)
        o_ref[...] = (_mxu(p.astype(BF16), v_ref[...].astype(BF16)) / l).astype(o_ref.dtype)

    vec = pl.BlockSpec((1, Dh), lambda h, i: (0, 0))
    return pl.pallas_call(
        body, grid=(XATTN_HEADS, S // tq),
        in_specs=[pl.BlockSpec((tq, Dh), lambda h, i: (i, h)), pl.BlockSpec((Mm, Dh), lambda h, i: (0, h)),
                  pl.BlockSpec((Mm, Dh), lambda h, i: (0, XATTN_HEADS + h)), vec, vec],
        out_specs=pl.BlockSpec((tq, Dh), lambda h, i: (i, h)),
        out_shape=jax.ShapeDtypeStruct((S, XATTN_HEADS * Dh), BF16),
        compiler_params=_params(("parallel", "parallel")), name="xattn_fwd")(xq, kv, kv, gq, gk)


def _xattn_bwd(xq, kv, gq, gk, do):
    S = xq.shape[0]
    Mm = kv.shape[0]
    Dh = XATTN_DIM
    tq = _pick(S, (512, 256))
    nq = S // tq

    def body(q_ref, k_ref, v_ref, gq_ref, gk_ref, do_ref, dq_ref, dk_ref, dv_ref, dgq_ref, dgk_ref, dkn_acc, dv_acc):
        h = pl.program_id(0)
        i = pl.program_id(1)

        @pl.when((h == 0) & (i == 0))
        def _():
            dgq_ref[...] = jnp.zeros_like(dgq_ref)
            dgk_ref[...] = jnp.zeros_like(dgk_ref)

        @pl.when(i == 0)
        def _():
            dkn_acc[...] = jnp.zeros_like(dkn_acc)
            dv_acc[...] = jnp.zeros_like(dv_acc)

        qn32, vq = jax.vjp(_xq_norm, q_ref[...], gq_ref[...])
        kn32, vk = jax.vjp(_rms, k_ref[...], gk_ref[...])
        qn = qn32.astype(BF16)
        kn = kn32.astype(BF16)
        vb = v_ref[...].astype(BF16)
        s = _mxu(qn, kn, _NT)
        m = jnp.max(s, axis=-1, keepdims=True)
        p = jnp.exp(s - m)
        p = p / jnp.sum(p, axis=-1, keepdims=True)
        dob = do_ref[...].astype(BF16)
        dp = _mxu(dob, vb, _NT)
        delta = jnp.sum(p * dp, axis=-1, keepdims=True)
        ds = (p * (dp - delta)).astype(BF16)
        dv_acc[...] += _mxu(p.astype(BF16), dob, _TN)
        dkn_acc[...] += _mxu(ds, qn, _TN)
        dq, dgq = vq(_mxu(ds, kn))
        dq_ref[...] = dq.astype(dq_ref.dtype)
        dgq_ref[...] += dgq

        @pl.when(i == nq - 1)
        def _():
            dk, dgk = vk(dkn_acc[...])
            dk_ref[...] = dk.astype(dk_ref.dtype)
            dv_ref[...] = dv_acc[...].astype(dv_ref.dtype)
            dgk_ref[...] += dgk

    vec = pl.BlockSpec((1, Dh), lambda h, i: (0, 0))
    qblk = pl.BlockSpec((tq, Dh), lambda h, i: (i, h))
    kblk = pl.BlockSpec((Mm, Dh), lambda h, i: (0, h))
    vblk = pl.BlockSpec((Mm, Dh), lambda h, i: (0, XATTN_HEADS + h))
    return pl.pallas_call(
        body, grid=(XATTN_HEADS, nq),
        in_specs=[qblk, kblk, vblk, vec, vec, qblk],
        out_specs=[qblk, kblk, kblk, vec, vec],
        out_shape=[jax.ShapeDtypeStruct((S, XATTN_HEADS * Dh), BF16),
                   jax.ShapeDtypeStruct((Mm, XATTN_HEADS * Dh), BF16),
                   jax.ShapeDtypeStruct((Mm, XATTN_HEADS * Dh), BF16),
                   jax.ShapeDtypeStruct((1, Dh), F32), jax.ShapeDtypeStruct((1, Dh), F32)],
        scratch_shapes=[pltpu.VMEM((Mm, Dh), F32), pltpu.VMEM((Mm, Dh), F32)],
        compiler_params=_params(("arbitrary", "arbitrary")), name="xattn_bwd")(xq, kv, kv, gq, gk, do)


def _loss_head(y, target):
    S, D = y.shape
    tr = _pick(S, (512, 256))

    def body(y_ref, t_ref, dy_ref, loss_ref):
        @pl.when(pl.program_id(0) == 0)
        def _():
            loss_ref[...] = jnp.zeros_like(loss_ref)

        err = y_ref[...] - t_ref[...]
        dy_ref[...] = err * (1.0 / D)
        loss_ref[...] += jnp.sum(err * err) * (0.5 / D)

    row = pl.BlockSpec((tr, D), lambda i: (i, 0))
    return pl.pallas_call(
        body, grid=(S // tr,), in_specs=[row, row],
        out_specs=[row, pl.BlockSpec((1, LANES), lambda i: (0, 0))],
        out_shape=[jax.ShapeDtypeStruct((S, D), F32), jax.ShapeDtypeStruct((1, LANES), F32)],
        compiler_params=_params(("arbitrary",)), name="loss_head")(y, target)


def _row_tile(R, C):
    for tr in (1024, 512, 256, 128, 64, 32, 16, 8):
        if R % tr == 0 and tr * C * 4 <= (1 << 20):
            return tr
    return R


def _chip_sum(own, from_chips, name):
    R, C = own.shape
    tr = _row_tile(R, C)

    def body(own_ref, a_ref, b_ref, c_ref, o_ref):
        total = ((own_ref[...].astype(F32) + a_ref[...].astype(F32)) + b_ref[...].astype(F32)) + c_ref[...].astype(F32)
        o_ref[...] = total.astype(o_ref.dtype)

    blk = pl.BlockSpec((tr, C), lambda i: (i, 0))
    slab = lambda s: pl.BlockSpec((None, tr, C), lambda i: (s, i, 0))
    return pl.pallas_call(
        body, grid=(R // tr,), in_specs=[blk, slab(0), slab(1), slab(2)], out_specs=blk,
        out_shape=jax.ShapeDtypeStruct((R, C), BF16),
        compiler_params=_params(("parallel",)), name=name)(own, from_chips, from_chips, from_chips)


def _adamw(w, g_mine, g_sibling, m, v, name):
    R, C = w.shape
    tr = _row_tile(R, C)
    c1 = 1.0 - ADAM_B1 ** ADAM_STEP
    c2 = 1.0 - ADAM_B2 ** ADAM_STEP

    def body(w_ref, ga_ref, gb_ref, m_ref, v_ref, g_ref, d_ref, mo_ref, vo_ref):
        g_t = ga_ref[...].astype(F32) + gb_ref[...].astype(F32)
        m_new = ADAM_B1 * m_ref[...] + (1.0 - ADAM_B1) * g_t
        v_new = ADAM_B2 * v_ref[...] + (1.0 - ADAM_B2) * (g_t * g_t)
        g_ref[...] = g_t
        d_ref[...] = -ADAM_LR * ((m_new / c1) / (jnp.sqrt(v_new / c2) + ADAM_EPS) + ADAM_WD * w_ref[...])
        mo_ref[...] = m_new
        vo_ref[...] = v_new

    blk = pl.BlockSpec((tr, C), lambda i: (i, 0))
    return pl.pallas_call(
        body, grid=(R // tr,), in_specs=[blk] * 5, out_specs=[blk] * 4,
        out_shape=[jax.ShapeDtypeStruct((R, C), F32)] * 4,
        compiler_params=_params(("parallel",)), name=name)(w, g_mine, g_sibling, m, v)


SSM_INNER = SSM_HEADS * HEAD_DIM
CONV_DIM = SSM_INNER + 2 * SSM_GROUPS * SSM_STATE
ATTN_WIDTH = ATTN_HEADS * HEAD_DIM
MIX_WIDTH = SSM_INNER + ATTN_WIDTH
COL_Z = 0
COL_XBC = COL_Z + SSM_INNER
COL_Q = COL_XBC + CONV_DIM
COL_K = COL_Q + ATTN_WIDTH
COL_V = COL_K + ATTN_WIDTH
COL_DT = COL_V + ATTN_WIDTH
COL_F = COL_DT + SSM_HEADS
IN_COLS = COL_F + ATTN_HEADS
IN_COLS_PAD = -(-IN_COLS // LANES) * LANES
REF_COL_DT = COL_Q
SHARD_COLS = IN_COLS // N_CHIPS
_COL_RANGES = ((0, REF_COL_DT, 0), (REF_COL_DT + SSM_HEADS, COL_F, COL_Q), (REF_COL_DT, REF_COL_DT + SSM_HEADS, COL_DT),
               (COL_F, IN_COLS, COL_F))


def _w_in_from_shards(g):
    parts = []
    for lo, hi, _ in _COL_RANGES:
        while lo < hi:
            j = lo // SHARD_COLS
            end = min(hi, (j + 1) * SHARD_COLS)
            parts.append(g[j][:, lo - j * SHARD_COLS:end - j * SHARD_COLS])
            lo = end
    parts.append(jnp.zeros((g.shape[1], IN_COLS_PAD - IN_COLS), g.dtype))
    return jnp.concatenate(parts, axis=1)


def _w_in_to_shards(w):
    shards = []
    for j in range(N_CHIPS):
        parts = []
        for lo, hi, here in sorted(_COL_RANGES):
            a, b = max(lo, j * SHARD_COLS), min(hi, (j + 1) * SHARD_COLS)
            if a < b:
                parts.append(w[:, here + a - lo:here + b - lo])
        shards.append(jnp.concatenate(parts, axis=1))
    return jnp.stack(shards)


def _add_residual(acc, res):
    return (res + acc,)


def _relu2(acc):
    r = jnp.maximum(acc, 0.0)
    return acc, r * r


def _relu2_bwd(acc, a):
    return (acc * (2.0 * jnp.maximum(a.astype(F32), 0.0)),)


def _layer_fwd_bwd(x, mem, target, w_in, p, late_weights, send_late_grads, send_w_in_grad):
    S = x.shape[0]
    hd3 = lambda a: a.reshape(SSM_HEADS, 1, 1)

    h1 = _rmsnorm_fwd(x, p["g_mix"], "norm_mix")
    proj = _mm(h1, w_in, "nn", "in_proj")
    xbc = _conv_fwd(proj, COL_XBC, CONV_DIM, p["conv_w"], p["conv_b"])
    dt_hm = proj[:, COL_DT:COL_DT + SSM_HEADS].T[:, :, None]
    ssd_par = (hd3(p["dt_bias"]), hd3(p["a_log"]), hd3(p["d_skip"]), p["ssm_norm_w"])
    mixed, hs = _ssd_fwd(xbc, proj, dt_hm, *ssd_par)
    f_raw = proj[:, COL_F:COL_F + ATTN_HEADS]
    gq2 = jnp.tile(p["g_q"], (1, 2))
    gk2 = jnp.tile(p["g_k"], (1, 2))
    qs, kn, vb = _qk_prep_fwd(proj, gq2, gk2)
    cum, cq = _logf_cumsum_fwd(f_raw, p["f_bias"])
    ck = cum.T[:, None, :]
    mixed, o_fine, lse = _flash_fwd(qs, kn, vb, cq, ck, mixed)
    W = late_weights((mixed,))
    x1 = _mm(mixed, W["w_out"], "nn", "out_proj", epilogue=_add_residual, extras=(x,))
    h2 = _rmsnorm_fwd(x1, p["g_xattn"], "norm_xattn")
    mem_n = _rmsnorm_fwd(mem, p["g_mem"], "norm_mem")
    xq = _mm(h2, W["xq_w"], "nn", "xq_proj")
    kv = _mm(mem_n, W["xkv_w"], "nn", "xkv_proj", b_chunks=N_CHIPS)
    xo = _xattn_fwd(xq, kv, p["xg_q"], p["xg_k"])
    x2 = _mm(xo, W["xo_w"], "nn", "xo_proj", epilogue=_add_residual, extras=(x1,))
    h3 = _rmsnorm_fwd(x2, p["g_mlp"], "norm_mlp")
    a, act = _mm(h3, W["w_up"], "nn", "mlp_up", out_dtypes=(BF16, BF16), epilogue=_relu2, b_chunks=N_CHIPS)
    x3 = _mm(act, W["w_down"], "nn", "mlp_down", epilogue=_add_residual, extras=(x2,))
    dy, loss_row = _loss_head(x3, target)

    gW, gp = {}, {}
    da = _mm(dy, W["w_down"], "nt", "d_act", out_dtypes=(BF16,), epilogue=_relu2_bwd, extras=(a,))
    gW["w_down"] = _mm(act, dy, "tn", "g_w_down", out_dtypes=(BF16,))
    gW["w_up"] = _mm(h3, da, "tn", "g_w_up", out_dtypes=(BF16,), out_chunks=N_CHIPS)
    dh3 = _mm(da, W["w_up"], "nt", "d_h3", b_chunks=N_CHIPS)
    dx2, gp["g_mlp"] = _rmsnorm_bwd(x2, p["g_mlp"], dh3, dy, "norm_mlp_bwd")
    dxo = _mm(dx2, W["xo_w"], "nt", "d_xo", out_dtypes=(BF16,))
    gW["xo_w"] = _mm(xo, dx2, "tn", "g_xo_w", out_dtypes=(BF16,))
    dxq, dk_x, dv_x, gp["xg_q"], gp["xg_k"] = _xattn_bwd(xq, kv, p["xg_q"], p["xg_k"], dxo)
    dkv = jnp.concatenate([dk_x, dv_x], axis=-1)
    gW["xq_w"] = _mm(h2, dxq, "tn", "g_xq_w", out_dtypes=(BF16,))
    dh2 = _mm(dxq, W["xq_w"], "nt", "d_h2")
    gW["xkv_w"] = _mm(mem_n, dkv, "tn", "g_xkv_w", out_dtypes=(BF16,), out_chunks=N_CHIPS)
    dmem_n = _mm(dkv, W["xkv_w"], "nt", "d_mem_n", b_chunks=N_CHIPS)
    _, gp["g_mem"] = _rmsnorm_bwd(mem, p["g_mem"], dmem_n, None, "norm_mem_bwd")
    dx1, gp["g_xattn"] = _rmsnorm_bwd(x1, p["g_xattn"], dh2, dx2, "norm_xattn_bwd")
    dmixed = _mm(dx1, W["w_out"], "nt", "d_mixed")
    gW["w_out"] = _mm(mixed, dx1, "tn", "g_w_out", out_dtypes=(BF16,))
    token = send_late_grads(gW)
    dqs, dkn, dproj, dck = _flash_bwd(qs, kn, vb, cq, ck + token[:1, :1], o_fine, dmixed, SSM_INNER, lse)
    dproj, dgq2 = _pair_norm_bwd(proj, COL_Q, gq2, ATTN_SCALE, dqs, dproj, "q_norm_bwd")
    dproj, dgk2 = _pair_norm_bwd(proj, COL_K, gk2, 1.0, dkn, dproj, "k_norm_bwd")
    gp["g_q"] = dgq2[:, :HEAD_DIM] + dgq2[:, HEAD_DIM:]
    gp["g_k"] = dgk2[:, :HEAD_DIM] + dgk2[:, HEAD_DIM:]
    df, gp["f_bias"] = _logf_cumsum_bwd(f_raw, p["f_bias"], dck[:, 0, :].T)
    dxs, dproj, dB, dC, ddt, ddtb, dalog, ddsk, gp["ssm_norm_w"] = _ssd_bwd(xbc, proj, dt_hm, *ssd_par, hs, dmixed, dproj)
    gp["dt_bias"] = ddtb.reshape(1, SSM_HEADS)
    gp["a_log"] = dalog.reshape(1, SSM_HEADS)
    gp["d_skip"] = ddsk.reshape(1, SSM_HEADS)
    dproj, dconv_w, gp["conv_b"] = _conv_bwd(proj, COL_XBC, CONV_DIM, p["conv_w"], p["conv_b"], (dxs, dB, dC), dproj)
    gp["conv_w"] = dconv_w[:CONV_WIDTH]
    tail = jnp.concatenate([ddt[:, :, 0].T, df, jnp.zeros((S, IN_COLS_PAD - IN_COLS), F32)], axis=-1).astype(BF16)
    dproj = lax.dynamic_update_slice(dproj, tail, (0, COL_DT))
    token = send_w_in_grad(_mm(h1, dproj, "tn", "g_w_in", out_dtypes=(BF16,)))
    dh1 = _mm(dproj, w_in, "nt", "d_h1")
    dx, gp["g_mix"] = _rmsnorm_bwd(x, p["g_mix"] + token[:1, :1], dh1, dx1, "norm_mix_bwd")
    return loss_row, dx, gp


_ANY = pl.BlockSpec(memory_space=pl.ANY)


def _place():
    x, y, c = lax.axis_index("x"), lax.axis_index("y"), lax.axis_index("c")
    chips = [(1 - x, y), (x, 1 - y), (1 - x, 1 - y)]
    return x, y, c, chips


def _chip_index(px, py):
    return 2 * px + py


def _all_gather_chips(split, whole):
    ns, nw = len(split), len(whole)
    n = ns + nw

    def body(*refs):
        ins, outs = refs[:n], refs[n:2 * n]
        send_ici, recv_ici, send_d2d, recv_d2d = refs[2 * n:]
        x, y, c, chips = _place()
        me = _chip_index(x, y)
        sib = (x, y, 1 - c)

        def ici(k, j, src, dst):
            return pltpu.make_async_remote_copy(src_ref=src, dst_ref=dst, send_sem=send_ici.at[3 * k + j],
                                                recv_sem=recv_ici.at[3 * k + j], device_id=(*chips[j], c),
                                                device_id_type=MESH)

        def d2d(k, j, piece):
            return pltpu.make_async_remote_copy(src_ref=piece, dst_ref=piece, send_sem=send_d2d.at[3 * k + j],
                                                recv_sem=recv_d2d.at[3 * k + j], device_id=sib, device_id_type=MESH)

        sends = []
        for k in range(n):
            for j in range(3):
                if k < ns:
                    sends.append(ici(k, j, ins[k].at[c], outs[k].at[me, c]))
                else:
                    sends.append(ici(k, j, ins[k], outs[k].at[me]))
                sends[-1].start()
        passed = []
        for k in range(n):
            for j in range(3):
                src_chip = _chip_index(*chips[j])
                if k < ns:
                    ici(k, j, ins[k].at[c], outs[k].at[src_chip, c]).wait_recv()
                    passed.append(d2d(k, j, outs[k].at[src_chip, c]))
                    passed[-1].start()
                else:
                    ici(k, j, ins[k], outs[k].at[src_chip]).wait_recv()
        for k in range(ns):
            for j in range(3):
                d2d(k, j, outs[k].at[_chip_index(*chips[j]), 1 - c]).wait_recv()
        for cp in sends + passed:
            cp.wait_send()

    arrs = list(split) + list(whole)
    return pl.pallas_call(
        body, in_specs=[_ANY] * n, out_specs=[_ANY] * n,
        out_shape=[jax.ShapeDtypeStruct((N_CHIPS,) + a.shape, a.dtype) for a in arrs],
        scratch_shapes=[pltpu.SemaphoreType.DMA((3 * n,)), pltpu.SemaphoreType.DMA((3 * n,)),
                        pltpu.SemaphoreType.DMA((3 * ns,)), pltpu.SemaphoreType.DMA((3 * ns,))],
        name="all_gather_chips")(*arrs)


def _sibling_swap(arrs, name):
    n = len(arrs)

    def body(*refs):
        ins, outs = refs[:n], refs[n:2 * n]
        send_sem, recv_sem = refs[2 * n:]
        x, y, c, _ = _place()
        copies = [pltpu.make_async_remote_copy(src_ref=ins[k], dst_ref=outs[k], send_sem=send_sem.at[k],
                                               recv_sem=recv_sem.at[k], device_id=(x, y, 1 - c), device_id_type=MESH)
                  for k in range(n)]
        for q in copies:
            q.start()
        for q in copies:
            q.wait()

    return pl.pallas_call(
        body, in_specs=[_ANY] * n, out_specs=[_ANY] * n,
        out_shape=[jax.ShapeDtypeStruct(a.shape, a.dtype) for a in arrs],
        scratch_shapes=[pltpu.SemaphoreType.DMA((n,)), pltpu.SemaphoreType.DMA((n,))],
        name=name)(*arrs)


_HBM = pl.BlockSpec(memory_space=pltpu.HBM)
_SEM = pl.BlockSpec(memory_space=pltpu.SEMAPHORE)
_SPLIT_EFFECT = pltpu.SideEffectType.DATAFLOW_SIDE_EFFECTING


class _Split(NamedTuple):
    send_sems: jax.Array
    recv_sems: jax.Array
    sources: tuple
    lands: tuple
    token: jax.Array


def _split_copies(kind, srcs, lands, send_sems, recv_sems):
    x, y, c, chips = _place()
    me = _chip_index(x, y)
    copies = []
    for k in range(len(srcs)):
        for j in range(3):
            if kind == "gather":
                src, dst = srcs[k], lands[k].at[me]
            else:
                src, dst = srcs[k].at[_chip_index(*chips[j])], lands[k].at[j]
            copies.append(pltpu.make_async_remote_copy(
                src_ref=src, dst_ref=dst, send_sem=send_sems.at[3 * k + j], recv_sem=recv_sems.at[3 * k + j],
                device_id=(*chips[j], c), device_id_type=MESH))
    return copies


def _split_start(name, sources, kind, after):
    n = len(sources)
    if kind == "gather":
        lands = [lax.empty((N_CHIPS,) + s.shape, s.dtype) for s in sources]
    else:
        lands = [lax.empty((3,) + s.shape[1:], s.dtype) for s in sources]
    deps = [] if after is None else [after]

    def body(*refs):
        srcs, lnds = refs[:n], refs[n:2 * n]
        send_sems, recv_sems = refs[2 * n + len(deps)], refs[2 * n + len(deps) + 1]
        for cp in _split_copies(kind, srcs, lnds, send_sems, recv_sems):
            cp.start()
        refs[-1][...] = jnp.zeros_like(refs[-1])

    hbm = lambda a: pltpu.with_memory_space_constraint(a, pltpu.HBM)
    outs = pl.pallas_call(
        body, name=name,
        in_specs=[_HBM] * (2 * n) + [_ANY] * len(deps),
        out_specs=[_SEM, _SEM] + [_HBM] * (2 * n) + [pl.BlockSpec(memory_space=pltpu.VMEM)],
        out_shape=[pltpu.SemaphoreType.DMA((3 * n,)), pltpu.SemaphoreType.DMA((3 * n,))]
        + [pltpu.HBM(a.shape, a.dtype) for a in list(sources) + lands] + [jax.ShapeDtypeStruct((8, LANES), F32)],
        input_output_aliases={k: 2 + k for k in range(2 * n)},
        compiler_params=pltpu.CompilerParams(has_side_effects=_SPLIT_EFFECT),
    )(*[hbm(s) for s in sources], *[hbm(l) for l in lands], *deps)
    return _Split(outs[0], outs[1], tuple(outs[2:2 + n]), tuple(outs[2 + n:2 + 2 * n]), outs[-1])


def _split_wait(name, h, kind, after):
    n = len(h.sources)

    def body(*refs):
        srcs, lnds = refs[:n], refs[n:2 * n]
        for cp in _split_copies(kind, srcs, lnds, refs[2 * n], refs[2 * n + 1]):
            cp.wait_send()
            cp.wait_recv()

    outs = pl.pallas_call(
        body, name=name,
        in_specs=[_HBM] * (2 * n) + [_SEM, _SEM] + [_ANY] * len(after),
        out_specs=[_HBM] * (2 * n),
        out_shape=[pltpu.HBM(a.shape, a.dtype) for a in h.sources + h.lands],
        input_output_aliases={k: k for k in range(2 * n)},
        compiler_params=pltpu.CompilerParams(has_side_effects=_SPLIT_EFFECT),
    )(*h.sources, *h.lands, h.send_sems, h.recv_sems, *after)
    return outs[:n], outs[n:]


def _all_reduce_small(vec, after):
    R, C = vec.shape

    def body(v_ref, after_ref, o_ref, buf, send_sem, recv_sem):
        x, y, c = lax.axis_index("x"), lax.axis_index("y"), lax.axis_index("c")
        me = 4 * x + 2 * y + c
        buf[me] = v_ref[...]
        copies = []
        for r in range(1, N_DEV):
            fx, fy, fc = (r >> 2) & 1, (r >> 1) & 1, r & 1
            peer = (x ^ fx, y ^ fy, c ^ fc)
            copies.append(pltpu.make_async_remote_copy(src_ref=v_ref, dst_ref=buf.at[me], send_sem=send_sem.at[r - 1],
                                                       recv_sem=recv_sem.at[r - 1], device_id=peer, device_id_type=MESH))
        for q in copies:
            q.start()
        for r in range(1, N_DEV):
            fx, fy, fc = (r >> 2) & 1, (r >> 1) & 1, r & 1
            src = 4 * (x ^ fx) + 2 * (y ^ fy) + (c ^ fc)
            pltpu.make_async_remote_copy(src_ref=v_ref, dst_ref=buf.at[src], send_sem=send_sem.at[r - 1],
                                         recv_sem=recv_sem.at[r - 1], device_id=(x, y, c), device_id_type=MESH).wait_recv()
        acc = buf[0]
        for d in range(1, N_DEV):
            acc = acc + buf[d]
        o_ref[...] = acc
        for q in copies:
            q.wait_send()

    vm = pl.BlockSpec(memory_space=pltpu.VMEM)
    return pl.pallas_call(
        body, in_specs=[vm, _ANY], out_specs=vm, out_shape=jax.ShapeDtypeStruct((R, C), F32),
        scratch_shapes=[pltpu.VMEM((N_DEV, R, C), F32), pltpu.SemaphoreType.DMA((N_DEV - 1,)),
                        pltpu.SemaphoreType.DMA((N_DEV - 1,))],
        name="all_reduce_small")(vec, after)


_INPUTS = ["x", "mem", "g_mix", "w_in", "conv_w", "conv_b", "dt_bias", "a_log", "d_skip", "ssm_norm_w", "g_q", "g_k",
           "f_bias", "w_out", "g_xattn", "g_mem", "xq_w", "xkv_w", "xg_q", "xg_k", "xo_w", "g_mlp", "w_up", "w_down"]
_WEIGHTS = _INPUTS[2:]
_BIG = ["w_in", "w_out", "xq_w", "xkv_w", "xo_w", "w_up", "w_down"]
_LATE = _BIG[1:]
_COL_SHARDED = ["w_in", "xkv_w", "w_up"]
_SMALL = [n for n in _WEIGHTS if n not in _BIG]


def _pack_rows(arrs, width):
    starts, r = [], 0
    for a in arrs:
        starts.append(r)
        r += a.shape[0]
    out = jnp.concatenate([jnp.pad(a, ((0, 0), (0, width - a.shape[1]))) for a in arrs], axis=0)
    return jnp.pad(out, ((0, -r % 8), (0, 0))), starts


def _adamw_small(summed, starts, ws, ms, vs, conv_w_index):
    n = len(ws)
    c1 = 1.0 - ADAM_B1 ** ADAM_STEP
    c2 = 1.0 - ADAM_B2 ** ADAM_STEP

    def body(s_ref, *refs):
        w_refs, m_refs, v_refs = refs[:n], refs[n:2 * n], refs[2 * n:3 * n]
        outs = refs[3 * n:]
        chip = _chip_index(lax.axis_index("x"), lax.axis_index("y"))
        for k in range(n):
            rows, cols = w_refs[k].shape
            if k == conv_w_index:
                g = s_ref[starts[k]:starts[k] + rows, pl.ds(pl.multiple_of(chip * cols, LANES), cols)]
            else:
                g = s_ref[starts[k]:starts[k] + rows, 0:cols]
            m_new = ADAM_B1 * m_refs[k][...] + (1.0 - ADAM_B1) * g
            v_new = ADAM_B2 * v_refs[k][...] + (1.0 - ADAM_B2) * (g * g)
            outs[4 * k][...] = g
            outs[4 * k + 1][...] = -ADAM_LR * ((m_new / c1) / (jnp.sqrt(v_new / c2) + ADAM_EPS) + ADAM_WD * w_refs[k][...])
            outs[4 * k + 2][...] = m_new
            outs[4 * k + 3][...] = v_new

    vm = pl.BlockSpec(memory_space=pltpu.VMEM)
    outs = pl.pallas_call(
        body, in_specs=[vm] * (1 + 3 * n), out_specs=[vm] * (4 * n),
        out_shape=[jax.ShapeDtypeStruct(a.shape, F32) for a in ws for _ in range(4)],
        name="adamw_small")(summed, *ws, *ms, *vs)
    return [outs[4 * k:4 * k + 4] for k in range(n)]


def kernel(x, mem, g_mix, w_in, conv_w, conv_b, dt_bias, a_log, d_skip, ssm_norm_w, g_q, g_k, f_bias, w_out, g_xattn, g_mem, xq_w, xkv_w, xg_q, xg_k, xo_w, g_mlp, w_up, w_down, loss_target, m_g_mix, m_w_in, m_conv_w, m_conv_b, m_dt_bias, m_a_log, m_d_skip, m_ssm_norm_w, m_g_q, m_g_k, m_f_bias, m_w_out, m_g_xattn, m_g_mem, m_xq_w, m_xkv_w, m_xg_q, m_xg_k, m_xo_w, m_g_mlp, m_w_up, m_w_down, v_g_mix, v_w_in, v_conv_w, v_conv_b, v_dt_bias, v_a_log, v_d_skip, v_ssm_norm_w, v_g_q, v_g_k, v_f_bias, v_w_out, v_g_xattn, v_g_mem, v_xq_w, v_xkv_w, v_xg_q, v_xg_k, v_xo_w, v_g_mlp, v_w_up, v_w_down):
    args = (x, mem, g_mix, w_in, conv_w, conv_b, dt_bias, a_log, d_skip, ssm_norm_w, g_q, g_k, f_bias, w_out, g_xattn,
            g_mem, xq_w, xkv_w, xg_q, xg_k, xo_w, g_mlp, w_up, w_down)
    w = dict(zip(_INPUTS, args))
    mom1 = dict(zip(_WEIGHTS, (m_g_mix, m_w_in, m_conv_w, m_conv_b, m_dt_bias, m_a_log, m_d_skip, m_ssm_norm_w, m_g_q,
                               m_g_k, m_f_bias, m_w_out, m_g_xattn, m_g_mem, m_xq_w, m_xkv_w, m_xg_q, m_xg_k, m_xo_w,
                               m_g_mlp, m_w_up, m_w_down)))
    mom2 = dict(zip(_WEIGHTS, (v_g_mix, v_w_in, v_conv_w, v_conv_b, v_dt_bias, v_a_log, v_d_skip, v_ssm_norm_w, v_g_q,
                               v_g_k, v_f_bias, v_w_out, v_g_xattn, v_g_mem, v_xq_w, v_xkv_w, v_xg_q, v_xg_k, v_xo_w,
                               v_g_mlp, v_w_up, v_w_down)))
    chip = _chip_index(lax.axis_index("x"), lax.axis_index("y"))

    shard_bf = {n: w[n][0].astype(BF16) for n in _BIG}

    def layout_for_compute(n, g):
        if n == "w_in":
            return _w_in_from_shards(g)
        return g if n in _COL_SHARDED else g.reshape(N_CHIPS * g.shape[1], g.shape[2])

    def layout_for_reduction(n, g):
        if n == "w_in":
            return _w_in_to_shards(g)
        return g if n in _COL_SHARDED else g.reshape(N_CHIPS, g.shape[0] // N_CHIPS, g.shape[1])

    halves_in = shard_bf["w_in"].reshape(2, shard_bf["w_in"].shape[0] // 2, -1)
    g_in, g_conv = _all_gather_chips([halves_in], [w["conv_w"][0]])
    g_in = lax.dynamic_update_index_in_dim(g_in, halves_in, chip, axis=0)
    g_conv = lax.dynamic_update_index_in_dim(g_conv, w["conv_w"][0], chip, axis=0)
    w_in_full = layout_for_compute("w_in", g_in.reshape(N_CHIPS, -1, g_in.shape[-1]))
    p = {n: w[n] for n in _SMALL}
    p["conv_w"] = g_conv.transpose(1, 0, 2).reshape(CONV_WIDTH, CONV_DIM)
    gather = _split_start("gather_late", [shard_bf[n] for n in _LATE], "gather", after=g_in)
    p["g_mix"] = p["g_mix"] + gather.token[:1, :1]

    def late_weights(after):
        srcs, lands = _split_wait("gather_late_wait", gather, "gather", after)
        lands = [lax.dynamic_update_index_in_dim(l, s, chip, axis=0) for l, s in zip(lands, srcs)]
        return {n: layout_for_compute(n, l) for n, l in zip(_LATE, lands)}

    scatter = {}

    def send_late_grads(grads):
        scatter["late"] = _split_start("scatter_late", [layout_for_reduction(n, grads[n]) for n in _LATE], "scatter",
                                       after=None)
        return scatter["late"].token

    def send_w_in_grad(g):
        scatter["w_in"] = _split_start("scatter_w_in", [layout_for_reduction("w_in", g)], "scatter", after=None)
        return scatter["w_in"].token

    loss_row, dx, gp = _layer_fwd_bwd(x[0], mem[0], loss_target[0], w_in_full, p, late_weights, send_late_grads,
                                      send_w_in_grad)

    grad, delta, new_m, new_v = {}, {}, {}, {}

    def finish(names, sources, from_chips, tag):
        mine = [_chip_sum(lax.dynamic_index_in_dim(s, chip, axis=0, keepdims=False), fc, "rs_chip_sum_" + n)
                for n, s, fc in zip(names, sources, from_chips)]
        for n, a, b in zip(names, mine, _sibling_swap(mine, "rs_sibling_swap_" + tag)):
            shape = w[n].shape
            res = _adamw(w[n][0], a, b, mom1[n][0], mom2[n][0], "adamw_" + n)
            grad[n], delta[n], new_m[n], new_v[n] = (r.reshape(shape) for r in res)

    finish(_LATE, *_split_wait("scatter_late_wait", scatter["late"], "scatter", (dx,)), "late")

    sources_in, from_chips_in = _split_wait("scatter_w_in_wait", scatter["w_in"], "scatter",
                                            tuple(new_v[n] for n in _LATE))

    packed, starts = _pack_rows([gp[n] for n in _SMALL] + [loss_row], CONV_DIM)
    summed = _all_reduce_small(packed, from_chips_in[0])
    loss = summed[starts[-1], 0]
    finish(["w_in"], sources_in, from_chips_in, "w_in")

    as_rows = lambda a: a.reshape(-1, a.shape[-1])
    results = _adamw_small(summed, starts, [as_rows(w[n]) for n in _SMALL], [as_rows(mom1[n]) for n in _SMALL],
                           [as_rows(mom2[n]) for n in _SMALL], _SMALL.index("conv_w"))
    for n, res in zip(_SMALL, results):
        grad[n], delta[n], new_m[n], new_v[n] = (a.reshape(w[n].shape) for a in res)

    return (loss, dx[None], *[grad[n] for n in _WEIGHTS], *[delta[n] for n in _WEIGHTS],
            *[new_m[n] for n in _WEIGHTS], *[new_v[n] for n in _WEIGHTS])
```

```python
from typing import NamedTuple

import jax
import jax.numpy as jnp
from jax import lax
from jax.experimental import pallas as pl
from jax.experimental.pallas import tpu as pltpu

F32 = jnp.float32
BF16 = jnp.bfloat16
HI = lax.Precision.HIGHEST
MESH = pl.DeviceIdType.MESH

EPS = 1e-5
CHUNK = 128
SSM_HEADS = 16
SSM_GROUPS = 2
HEADS_PER_GROUP = SSM_HEADS // SSM_GROUPS
HEAD_DIM = 64
SSM_STATE = 128
ATTN_HEADS = 16
XATTN_HEADS = 4
XATTN_DIM = 256
CONV_WIDTH = 4
CONV_COLS = 256
N_CHIPS = 4
N_DEV = 8
LANES = 128
VMEM_LIMIT = 56 * 1024 * 1024

ADAM_LR = 0.001
ADAM_B1 = 0.9
ADAM_B2 = 0.999
ADAM_EPS = 1e-08
ADAM_WD = 0.01
ADAM_STEP = 10


def _params(sem):
    return pltpu.CompilerParams(dimension_semantics=sem, vmem_limit_bytes=VMEM_LIMIT)


def _pick(n, cands):
    for c in cands:
        if n % c == 0:
            return c
    return n


def _mm(a, b, mode, name, out_dtypes=(F32,), epilogue=None, extras=(), b_chunks=1, out_chunks=1,
        tm=None, tn=None, tk=None):
    if mode == "nn":
        M, K = a.shape
        N = b.shape[-1] * b_chunks
    elif mode == "nt":
        M, K = a.shape
        N = b.shape[-2]
        assert b.shape[-1] * b_chunks == K
    else:
        K, M = a.shape
        N = b.shape[-1] * b_chunks
    tm = tm or _pick(M, (2048, 1024, 512, 256, 128))
    tn = tn or _pick(N // max(b_chunks if mode != "nt" else 1, out_chunks), (512, 640, 384, 256, 128))
    if tk is None:
        kmax = b.shape[-1] if mode == "nt" else K
        tk = kmax if kmax <= 2048 else _pick(kmax, (2048, 1920, 1152, 1024, 512))
    nk = K // tk
    assert M % tm == 0 and N % tn == 0 and K % tk == 0
    grid = (M // tm, N // tn, nk)

    if mode == "tn":
        a_spec = pl.BlockSpec((tk, tm), lambda i, j, k: (k, i))
    else:
        a_spec = pl.BlockSpec((tm, tk), lambda i, j, k: (i, k))

    def b_index(t_row, t_last, tile_last):
        if b_chunks == 1:
            return (t_row, t_last)
        q = (b.shape[-1]) // tile_last
        return (t_last // q, t_row, t_last % q)

    if mode == "nn" or mode == "tn":
        bshape = (tk, tn)
        bmap = lambda i, j, k: b_index(k, j, tn)
    else:
        bshape = (tn, tk)
        bmap = lambda i, j, k: b_index(j, k, tk)
    if b_chunks > 1:
        bshape = (None,) + bshape
    b_spec = pl.BlockSpec(bshape, bmap)

    if out_chunks == 1:
        o_spec = pl.BlockSpec((tm, tn), lambda i, j, k: (i, j))
        o_shape = (M, N)
    else:
        qo = (N // out_chunks) // tn
        o_spec = pl.BlockSpec((None, tm, tn), lambda i, j, k: (j // qo, i, j % qo))
        o_shape = (out_chunks, M, N // out_chunks)
    e_spec = pl.BlockSpec((tm, tn), lambda i, j, k: (i, j))

    dims = {"nn": (((1,), (0,)), ((), ())), "nt": (((1,), (1,)), ((), ())), "tn": (((0,), (0,)), ((), ()))}[mode]
    n_ex = len(extras)
    n_out = len(out_dtypes)

    def body(*refs):
        a_ref, b_ref = refs[0], refs[1]
        ex_refs = refs[2:2 + n_ex]
        o_refs = refs[2 + n_ex:2 + n_ex + n_out]

        def finish(acc):
            outs = epilogue(acc, *[r[...] for r in ex_refs]) if epilogue is not None else (acc,)
            for r, o in zip(o_refs, outs):
                r[...] = o.astype(r.dtype)

        part = lax.dot_general(a_ref[...].astype(BF16), b_ref[...].astype(BF16), dims,
                               preferred_element_type=F32)
        if nk == 1:
            finish(part)
        else:
            acc_ref = refs[-1]
            k = pl.program_id(2)

            @pl.when(k == 0)
            def _():
                acc_ref[...] = part

            @pl.when(k > 0)
            def _():
                acc_ref[...] += part

            @pl.when(k == nk - 1)
            def _():
                finish(acc_ref[...])

    outs = pl.pallas_call(
        body,
        grid=grid,
        in_specs=[a_spec, b_spec] + [e_spec] * n_ex,
        out_specs=[o_spec] * n_out,
        out_shape=[jax.ShapeDtypeStruct(o_shape, d) for d in out_dtypes],
        scratch_shapes=[pltpu.VMEM((tm, tn), F32)] if nk > 1 else [],
        compiler_params=_params(("parallel", "parallel", "arbitrary")),
        name=name,
    )(a, b, *extras)
    return outs[0] if n_out == 1 else outs


def _rms(x, g):
    r = lax.rsqrt(jnp.mean(x * x, axis=-1, keepdims=True) + EPS)
    return x * r * g


def _rmsnorm_fwd(x, g, name):
    R, D = x.shape
    tr = _pick(R, (512, 256))

    def body(x_ref, g_ref, o_ref):
        o_ref[...] = _rms(x_ref[...], g_ref[...]).astype(o_ref.dtype)

    return pl.pallas_call(
        body, grid=(R // tr,),
        in_specs=[pl.BlockSpec((tr, D), lambda i: (i, 0)), pl.BlockSpec((1, D), lambda i: (0, 0))],
        out_specs=pl.BlockSpec((tr, D), lambda i: (i, 0)),
        out_shape=jax.ShapeDtypeStruct((R, D), BF16),
        compiler_params=_params(("parallel",)), name=name)(x, g)


def _rmsnorm_bwd(x, g, dh, dres, name):
    R, D = x.shape
    tr = _pick(R, (256,))
    has_res = dres is not None

    def body(*refs):
        if has_res:
            x_ref, g_ref, dh_ref, dres_ref, dx_ref, dg_ref = refs
        else:
            x_ref, g_ref, dh_ref, dx_ref, dg_ref = refs
        _, vjp = jax.vjp(_rms, x_ref[...], g_ref[...])
        dx, dg = vjp(dh_ref[...])
        if has_res:
            dx = dx + dres_ref[...]
        dx_ref[...] = dx

        @pl.when(pl.program_id(0) == 0)
        def _():
            dg_ref[...] = jnp.zeros_like(dg_ref)

        dg_ref[...] += dg

    row = pl.BlockSpec((tr, D), lambda i: (i, 0))
    vec = pl.BlockSpec((1, D), lambda i: (0, 0))
    ins = [x, g, dh] + ([dres] if has_res else [])
    return pl.pallas_call(
        body, grid=(R // tr,),
        in_specs=[row, vec, row] + ([row] if has_res else []),
        out_specs=[row, vec],
        out_shape=[jax.ShapeDtypeStruct((R, D), F32), jax.ShapeDtypeStruct((1, D), F32)],
        compiler_params=_params(("arbitrary",)), name=name)(*ins)


def _shift_down(u, k):
    if k == 0:
        return u
    rows = lax.broadcasted_iota(jnp.int32, u.shape, 0)
    return jnp.where(rows >= k, pltpu.roll(u, k, axis=0), 0.0)


def _shift_up(u, k):
    if k == 0:
        return u
    n = u.shape[0]
    rows = lax.broadcasted_iota(jnp.int32, u.shape, 0)
    return jnp.where(rows < n - k, pltpu.roll(u, n - k, axis=0), 0.0)


def _conv_pre(u, w, b):
    pre = b
    for j in range(CONV_WIDTH):
        pre = pre + w[j:j + 1, :] * _shift_down(u, CONV_WIDTH - 1 - j)
    return pre


def _conv_fwd(proj, col0, ncols, conv_w, conv_b):
    S = proj.shape[0]
    cb0 = col0 // CONV_COLS

    def body(u_ref, w_ref, b_ref, o_ref):
        pre = _conv_pre(u_ref[...], w_ref[...], b_ref[...])
        o_ref[...] = pre * jax.nn.sigmoid(pre)

    return pl.pallas_call(
        body, grid=(ncols // CONV_COLS,),
        in_specs=[pl.BlockSpec((S, CONV_COLS), lambda j: (0, j + cb0)),
                  pl.BlockSpec((CONV_WIDTH, CONV_COLS), lambda j: (0, j)),
                  pl.BlockSpec((1, CONV_COLS), lambda j: (0, j))],
        out_specs=pl.BlockSpec((S, CONV_COLS), lambda j: (0, j)),
        out_shape=jax.ShapeDtypeStruct((S, ncols), F32),
        compiler_params=_params(("parallel",)), name="conv_fwd")(proj, conv_w, conv_b)


def _conv_bwd(proj, col0, ncols, conv_w, conv_b, douts, dproj):
    S = proj.shape[0]
    cb0 = col0 // CONV_COLS
    starts = [0]
    for d in douts:
        starts.append(starts[-1] + d.shape[1] // CONV_COLS)
    assert starts[-1] == ncols // CONV_COLS
    nd = len(douts)

    def body(u_ref, w_ref, b_ref, *rest):
        d_refs, (du_ref, dw_ref, db_ref) = rest[:nd], rest[nd + 1:]
        j = pl.program_id(0)
        dout = d_refs[-1][...]
        for i in range(nd - 2, -1, -1):
            dout = jnp.where(j < starts[i + 1], d_refs[i][...], dout)
        u = u_ref[...]
        w = w_ref[...]
        pre = _conv_pre(u, w, b_ref[...])
        s = jax.nn.sigmoid(pre)
        dpre = dout * (s * (1.0 + pre * (1.0 - s)))
        du = jnp.zeros_like(u)
        rows = []
        for j in range(CONV_WIDTH):
            k = CONV_WIDTH - 1 - j
            du = du + w[j:j + 1, :] * _shift_up(dpre, k)
            rows.append(jnp.sum(dpre * _shift_down(u, k), axis=0, keepdims=True))
        du_ref[...] = du.astype(du_ref.dtype)
        rows.append(jnp.zeros((8 - CONV_WIDTH, CONV_COLS), F32))
        dw_ref[...] = jnp.concatenate(rows, axis=0)
        db_ref[...] = jnp.sum(dpre, axis=0, keepdims=True)

    return pl.pallas_call(
        body, grid=(ncols // CONV_COLS,),
        in_specs=[pl.BlockSpec((S, CONV_COLS), lambda j: (0, j + cb0)),
                  pl.BlockSpec((CONV_WIDTH, CONV_COLS), lambda j: (0, j)),
                  pl.BlockSpec((1, CONV_COLS), lambda j: (0, j))]
        + [pl.BlockSpec((S, CONV_COLS), lambda j, lo=starts[i], hi=starts[i + 1]: (0, jnp.clip(j - lo, 0, hi - lo - 1)))
           for i in range(nd)] + [_ANY],
        out_specs=[pl.BlockSpec((S, CONV_COLS), lambda j: (0, j + cb0)),
                   pl.BlockSpec((8, CONV_COLS), lambda j: (0, j)),
                   pl.BlockSpec((1, CONV_COLS), lambda j: (0, j))],
        out_shape=[jax.ShapeDtypeStruct(dproj.shape, dproj.dtype),
                   jax.ShapeDtypeStruct((8, ncols), F32),
                   jax.ShapeDtypeStruct((1, ncols), F32)],
        input_output_aliases={3 + nd: 0},
        compiler_params=_params(("parallel",)), name="conv_bwd")(proj, conv_w, conv_b, *douts, dproj)


def _softplus(x):
    return jnp.maximum(x, 0.0) + jnp.log1p(jnp.exp(-jnp.abs(x)))


def _dot32(a, b, dims=(((1,), (0,)), ((), ()))):
    return lax.dot_general(a, b, dims, precision=HI, preferred_element_type=F32)


def _dotd(a, b, dims=(((1,), (0,)), ((), ()))):
    return lax.dot_general(a, b, dims, preferred_element_type=F32)


PAIRS_PER_GROUP = HEADS_PER_GROUP // 2


def _ssd_chunk(xs, Bm, Cm, z, dtr, dtb, alog, dsk, nw, h):
    L = Bm.shape[0]
    ri = lax.broadcasted_iota(jnp.int32, (L, L), 0)
    ci = lax.broadcasted_iota(jnp.int32, (L, L), 1)
    causal = ri >= ci
    tril = causal.astype(F32)
    first = _first_head(L)
    first1 = _first_head(1)
    CB = _dotd(Cm, Bm, _NT)
    gated, hnew = [], []
    ssq = jnp.zeros((L, 1), F32)
    for pp in range(len(xs)):
        dts, cums, tots, decay = [], [], [], []
        for a in range(2):
            r = 2 * pp + a
            dt = _softplus(dtr[r] + dtb[r])
            dA = dt * (-jnp.exp(alog[r]))
            acs = _dot32(tril, dA)
            cc = jnp.broadcast_to(acs, (L, L))
            decay.append(CB * jnp.exp(jnp.where(causal, cc - cc.T, -1e30)))
            dts.append(dt)
            cums.append(acs)
            tots.append(jnp.sum(dA, axis=0, keepdims=True))
        dt2 = jnp.where(first, dts[0], dts[1])
        acs2 = jnp.where(first, cums[0], cums[1])
        tot2 = jnp.where(first1, tots[0], tots[1])
        dsk2 = jnp.where(first1, dsk[2 * pp], dsk[2 * pp + 1])
        X = xs[pp] * dt2
        y = (jnp.where(first, _dotd(decay[0], X), _dotd(decay[1], X)) + jnp.exp(acs2) * _dotd(Cm, h[pp])
             + dsk2 * xs[pp])
        hnew.append(jnp.exp(tot2) * h[pp] + _dotd(Bm, X * jnp.exp(tot2 - acs2), _TN))
        g = y * (z[pp] * jax.nn.sigmoid(z[pp]))
        ssq = ssq + jnp.sum(g * g, axis=-1, keepdims=True)
        gated.append(g)
    rs = lax.rsqrt(ssq / (len(xs) * LANES) + EPS)
    return [g * rs * nw[pp] for pp, g in enumerate(gated)], hnew


def _ssd_args(xs_ref, b_ref, c_ref, z_ref, dt_ref, dtb_ref, al_ref, dsk_ref, nw_ref, h_ref):
    pairs = range(PAIRS_PER_GROUP)
    heads = range(HEADS_PER_GROUP)
    lanes = lambda ref, pp: ref[:, pp * LANES:(pp + 1) * LANES]
    return ([lanes(xs_ref, pp) for pp in pairs], b_ref[...], c_ref[...], [lanes(z_ref, pp) for pp in pairs],
            [dt_ref[r] for r in heads], [dtb_ref[r] for r in heads], [al_ref[r] for r in heads],
            [dsk_ref[r] for r in heads], [lanes(nw_ref, pp) for pp in pairs], [h_ref[pp] for pp in pairs])


def _ssd_specs(rev):
    H, N, L = HEADS_PER_GROUP, SSM_STATE, CHUNK
    gw = H * HEAD_DIM
    return dict(
        cols=lambda col0: pl.BlockSpec((L, gw), lambda g, c: (rev(c), col0 // gw + g)),
        bc=lambda first_block: pl.BlockSpec((L, N), lambda g, c: (rev(c), first_block + g)),
        dt=pl.BlockSpec((H, L, 1), lambda g, c: (g, rev(c), 0)),
        scal=pl.BlockSpec((H, 1, 1), lambda g, c: (g, 0, 0)),
        nw=pl.BlockSpec((1, gw), lambda g, c: (0, g)),
        hs=pl.BlockSpec((None, PAIRS_PER_GROUP, N, LANES), lambda g, c: (rev(c), g, 0, 0)),
        b_block=SSM_INNER // N,
    )


def _ssd_fwd(xbc, proj, dt_hm, dtb, alog, dsk, nw):
    S = xbc.shape[0]
    N, L = SSM_STATE, CHUNK
    nc = S // L
    sp = _ssd_specs(lambda c: c)

    def body(xs_ref, b_ref, c_ref, z_ref, dt_ref, dtb_ref, al_ref, dsk_ref, nw_ref, y_ref, hs_ref, h_ref):
        @pl.when(pl.program_id(1) == 0)
        def _():
            h_ref[...] = jnp.zeros_like(h_ref)

        hs_ref[...] = h_ref[...]
        out, hnew = _ssd_chunk(*_ssd_args(xs_ref, b_ref, c_ref, z_ref, dt_ref, dtb_ref, al_ref, dsk_ref, nw_ref, h_ref))
        for pp in range(PAIRS_PER_GROUP):
            y_ref[:, pp * LANES:(pp + 1) * LANES] = out[pp].astype(y_ref.dtype)
            h_ref[pp] = hnew[pp]

    return pl.pallas_call(
        body, grid=(SSM_GROUPS, nc),
        in_specs=[sp["cols"](0), sp["bc"](sp["b_block"]), sp["bc"](sp["b_block"] + SSM_GROUPS), sp["cols"](COL_Z),
                  sp["dt"], sp["scal"], sp["scal"], sp["scal"], sp["nw"]],
        out_specs=[sp["cols"](0), sp["hs"]],
        out_shape=[jax.ShapeDtypeStruct((S, MIX_WIDTH), BF16),
                   jax.ShapeDtypeStruct((nc, SSM_HEADS // 2, N, LANES), F32)],
        scratch_shapes=[pltpu.VMEM((PAIRS_PER_GROUP, N, LANES), F32)],
        compiler_params=_params(("parallel", "arbitrary")), name="ssd_fwd",
    )(xbc, xbc, xbc, proj, dt_hm, dtb, alog, dsk, nw)


def _ssd_bwd(xbc, proj, dt_hm, dtb, alog, dsk, nw, hs, dmixed, dproj):
    S = xbc.shape[0]
    N, L = SSM_STATE, CHUNK
    nc = S // L
    sp = _ssd_specs(lambda c: nc - 1 - c)

    def body(xs_ref, b_ref, c_ref, z_ref, dt_ref, dtb_ref, al_ref, dsk_ref, nw_ref, hs_ref, dy_ref, buf_ref,
             dxs_ref, dz_ref, db_ref, dc_ref, ddt_ref, ddtb_ref, dal_ref, ddsk_ref, dnw_ref, dh_ref):
        @pl.when(pl.program_id(1) == 0)
        def _():
            dh_ref[...] = jnp.zeros_like(dh_ref)
            ddtb_ref[...] = jnp.zeros_like(ddtb_ref)
            dal_ref[...] = jnp.zeros_like(dal_ref)
            ddsk_ref[...] = jnp.zeros_like(ddsk_ref)
            dnw_ref[...] = jnp.zeros_like(dnw_ref)

        pairs = range(PAIRS_PER_GROUP)
        lanes = lambda pp: slice(pp * LANES, (pp + 1) * LANES)
        _, vjp = jax.vjp(_ssd_chunk, *_ssd_args(xs_ref, b_ref, c_ref, z_ref, dt_ref, dtb_ref, al_ref, dsk_ref, nw_ref,
                                                hs_ref))
        dxs, dB, dC, dz, ddt, ddtb, dal, ddsk, dnw, dh = vjp(([dy_ref[:, lanes(pp)] for pp in pairs],
                                                              [dh_ref[pp] for pp in pairs]))
        db_ref[...] = dB
        dc_ref[...] = dC
        for pp in pairs:
            dxs_ref[:, lanes(pp)] = dxs[pp]
            dz_ref[:, lanes(pp)] = dz[pp].astype(dz_ref.dtype)
            dnw_ref[:, lanes(pp)] += dnw[pp]
            dh_ref[pp] = dh[pp]
        for r in range(HEADS_PER_GROUP):
            ddt_ref[r] = ddt[r]
            ddtb_ref[r] += ddtb[r]
            dal_ref[r] += dal[r]
            ddsk_ref[r] += ddsk[r]

    bc_out = pl.BlockSpec((L, N), lambda g, c: (nc - 1 - c, g))
    return pl.pallas_call(
        body, grid=(SSM_GROUPS, nc),
        in_specs=[sp["cols"](0), sp["bc"](sp["b_block"]), sp["bc"](sp["b_block"] + SSM_GROUPS), sp["cols"](COL_Z),
                  sp["dt"], sp["scal"], sp["scal"], sp["scal"], sp["nw"], sp["hs"], sp["cols"](0), _ANY],
        out_specs=[sp["cols"](0), sp["cols"](COL_Z), bc_out, bc_out, sp["dt"], sp["scal"], sp["scal"], sp["scal"],
                   sp["nw"]],
        input_output_aliases={11: 1},
        out_shape=[jax.ShapeDtypeStruct((S, SSM_INNER), F32), jax.ShapeDtypeStruct(dproj.shape, dproj.dtype),
                   jax.ShapeDtypeStruct((S, SSM_GROUPS * N), F32), jax.ShapeDtypeStruct((S, SSM_GROUPS * N), F32),
                   jax.ShapeDtypeStruct((SSM_HEADS, S, 1), F32),
                   jax.ShapeDtypeStruct((SSM_HEADS, 1, 1), F32), jax.ShapeDtypeStruct((SSM_HEADS, 1, 1), F32),
                   jax.ShapeDtypeStruct((SSM_HEADS, 1, 1), F32), jax.ShapeDtypeStruct((1, SSM_INNER), F32)],
        scratch_shapes=[pltpu.VMEM((PAIRS_PER_GROUP, N, LANES), F32)],
        compiler_params=_params(("parallel", "arbitrary")), name="ssd_bwd",
    )(xbc, xbc, xbc, proj, dt_hm, dtb, alog, dsk, nw, hs, dmixed, dproj)


ATTN_SCALE = HEAD_DIM ** -0.5
PREP_COLS = 512


def _first_head(rows):
    return lax.broadcasted_iota(jnp.int32, (rows, LANES), 1) < HEAD_DIM


def _pair_norm(x, g2, scale):
    first = _first_head(x.shape[0])
    sq = x * x
    ms0 = jnp.sum(jnp.where(first, sq, 0.0), axis=-1, keepdims=True) * (1.0 / HEAD_DIM)
    ms1 = jnp.sum(jnp.where(first, 0.0, sq), axis=-1, keepdims=True) * (1.0 / HEAD_DIM)
    r = jnp.where(first, lax.rsqrt(ms0 + EPS), lax.rsqrt(ms1 + EPS))
    return x * r * g2 * scale


def _qk_prep_fwd(proj, gq2, gk2):
    S = proj.shape[0]
    tq = _pick(S, (512, 256))

    def body(q_ref, k_ref, v_ref, gq_ref, gk_ref, qo_ref, ko_ref, vo_ref):
        for b in range(PREP_COLS // LANES):
            pair = slice(b * LANES, (b + 1) * LANES)
            qo_ref[:, pair] = _pair_norm(q_ref[:, pair], gq_ref[...], ATTN_SCALE).astype(BF16)
            ko_ref[:, pair] = _pair_norm(k_ref[:, pair], gk_ref[...], 1.0).astype(BF16)
        vo_ref[...] = v_ref[...].astype(BF16)

    col = lambda c0: pl.BlockSpec((tq, PREP_COLS), lambda h, i: (i, c0 // PREP_COLS + h))
    blk = pl.BlockSpec((tq, PREP_COLS), lambda h, i: (i, h))
    vec = pl.BlockSpec((1, LANES), lambda h, i: (0, 0))
    return pl.pallas_call(
        body, grid=(ATTN_WIDTH // PREP_COLS, S // tq), in_specs=[col(COL_Q), col(COL_K), col(COL_V), vec, vec],
        out_specs=[blk, blk, blk], out_shape=[jax.ShapeDtypeStruct((S, ATTN_WIDTH), BF16)] * 3,
        compiler_params=_params(("parallel", "parallel")), name="qk_prep_fwd")(proj, proj, proj, gq2, gk2)


def _pair_norm_bwd(proj, col0, g2, scale, dn, dproj, name):
    S = proj.shape[0]
    tq = _pick(S, (512, 256))

    def body(u_ref, g_ref, dn_ref, buf_ref, du_ref, dg_ref):
        @pl.when((pl.program_id(0) == 0) & (pl.program_id(1) == 0))
        def _():
            dg_ref[...] = jnp.zeros_like(dg_ref)

        for b in range(PREP_COLS // LANES):
            pair = slice(b * LANES, (b + 1) * LANES)
            _, vjp = jax.vjp(lambda u, g: _pair_norm(u, g, scale), u_ref[:, pair], g_ref[...])
            du, dg = vjp(dn_ref[:, pair])
            du_ref[:, pair] = du.astype(du_ref.dtype)
            dg_ref[...] += dg

    ublk = pl.BlockSpec((tq, PREP_COLS), lambda h, i: (i, col0 // PREP_COLS + h))
    blk = pl.BlockSpec((tq, PREP_COLS), lambda h, i: (i, h))
    vec = pl.BlockSpec((1, LANES), lambda h, i: (0, 0))
    return pl.pallas_call(
        body, grid=(ATTN_WIDTH // PREP_COLS, S // tq), in_specs=[ublk, vec, blk, _ANY], out_specs=[ublk, vec],
        out_shape=[jax.ShapeDtypeStruct(dproj.shape, dproj.dtype), jax.ShapeDtypeStruct((1, LANES), F32)],
        input_output_aliases={3: 0},
        compiler_params=_params(("arbitrary", "arbitrary")), name=name)(proj, g2, dn, dproj)


def _logf_cumsum_fwd(f_raw, f_bias):
    S, Hh = f_raw.shape
    L = CHUNK

    def body(f_ref, b_ref, o_ref, wide_ref):
        ri = lax.broadcasted_iota(jnp.int32, (L, L), 0)
        ci = lax.broadcasted_iota(jnp.int32, (L, L), 1)
        tril = (ri >= ci).astype(F32)
        carry = jnp.zeros((1, Hh), F32)
        for c in range(S // L):
            rows = slice(c * L, (c + 1) * L)
            lf = -_softplus(-(f_ref[rows, :] + b_ref[...]))
            cum = _dot32(tril, lf) + carry
            o_ref[rows, :] = cum
            for h in range(Hh):
                wide_ref[rows, h * HEAD_DIM:(h + 1) * HEAD_DIM] = jnp.broadcast_to(cum[:, h:h + 1], (L, HEAD_DIM))
            carry = cum[L - 1:L, :]

    return pl.pallas_call(
        body, out_shape=[jax.ShapeDtypeStruct((S, Hh), F32), jax.ShapeDtypeStruct((S, Hh * HEAD_DIM), F32)],
        name="logf_cumsum_fwd")(f_raw, f_bias)


def _logf_cumsum_bwd(f_raw, f_bias, dcum):
    S, Hh = f_raw.shape
    L = CHUNK

    def body(f_ref, b_ref, d_ref, df_ref, db_ref):
        ri = lax.broadcasted_iota(jnp.int32, (L, L), 0)
        ci = lax.broadcasted_iota(jnp.int32, (L, L), 1)
        triu = (ri <= ci).astype(F32)
        carry = jnp.zeros((1, Hh), F32)
        db = jnp.zeros((1, Hh), F32)
        for c in reversed(range(S // L)):
            suf = _dot32(triu, d_ref[c * L:(c + 1) * L, :]) + carry
            df = suf * jax.nn.sigmoid(-(f_ref[c * L:(c + 1) * L, :] + b_ref[...]))
            df_ref[c * L:(c + 1) * L, :] = df
            db = db + jnp.sum(df, axis=0, keepdims=True)
            carry = suf[0:1, :]
        db_ref[...] = db

    return pl.pallas_call(
        body, out_shape=[jax.ShapeDtypeStruct((S, Hh), F32), jax.ShapeDtypeStruct((1, Hh), F32)],
        name="logf_cumsum_bwd")(f_raw, f_bias, dcum)


_NT = (((1,), (1,)), ((), ()))
_TN = (((0,), (0,)), ((), ()))


def _mxu(a, b, dims=(((1,), (0,)), ((), ()))):
    return lax.dot_general(a, b, dims, preferred_element_type=F32)


def _flash_fwd(qs, kn, vb, cq, ck, mixed):
    S, W = qs.shape
    tq = tk = _pick(S, (512, 256))
    nmask = max(tq // tk, 1)

    def body(q_ref, k_ref, v_ref, cq_ref, ck_ref, buf_ref, o_ref, of_ref, lse_ref):
        i = pl.program_id(1)
        first = _first_head(tq)
        q2 = q_ref[...]
        zero = jnp.zeros_like(q2)
        qa = (jnp.where(first, q2, zero), jnp.where(first, zero, q2))
        cqa = (cq_ref[:, 0:1], cq_ref[:, HEAD_DIM:HEAD_DIM + 1])
        row0 = i * tq

        def step(j, carry, masked):
            ms, ls, acc, rem = carry
            off = pl.multiple_of(j * tk, tk)
            k = k_ref[pl.ds(off, tk), :]
            v = v_ref[pl.ds(off, tk), :]
            new_m, new_l, alphas, pvs, prs = [], [], [], [], []
            for a in range(2):
                s = _mxu(qa[a], k, _NT) + cqa[a] - ck_ref[a, :, pl.ds(off, tk)]
                if masked:
                    ri = lax.broadcasted_iota(jnp.int32, (tq, tk), 0) + row0
                    ci = lax.broadcasted_iota(jnp.int32, (tq, tk), 1) + off
                    s = jnp.where(ri >= ci, s, -1e30)
                m_new = jnp.maximum(ms[a], jnp.max(s, axis=-1, keepdims=True))
                alpha = jnp.exp(ms[a] - m_new)
                p = jnp.exp(s - m_new)
                new_l.append(alpha * ls[a] + jnp.sum(p, axis=-1, keepdims=True))
                new_m.append(m_new)
                alphas.append(alpha)
                p_hi = p.astype(BF16)
                pvs.append(_mxu(p_hi, v))
                prs.append(_mxu((p - p_hi.astype(F32)).astype(BF16), v))
            al = jnp.where(first, alphas[0], alphas[1])
            acc = al * acc + jnp.where(first, pvs[0], pvs[1])
            rem = al * rem + jnp.where(first, prs[0], prs[1])
            return tuple(new_m), tuple(new_l), acc, rem

        neg = jnp.full((tq, 1), -1e30, F32)
        z1 = jnp.zeros((tq, 1), F32)
        z2 = jnp.zeros((tq, LANES), F32)
        carry = ((neg, neg), (z1, z1), z2, z2)
        n_full = (i * tq) // tk
        carry = lax.fori_loop(0, n_full, lambda j, c: step(j, c, False), carry)
        for jj in range(nmask):
            carry = step(n_full + jj, carry, True)
        ms, ls, acc, rem = carry
        linv = jnp.where(first, 1.0 / ls[0], 1.0 / ls[1])
        o_ref[...] = (acc * linv).astype(o_ref.dtype)
        of_ref[...] = (acc + rem) * linv
        lse_ref[...] = jnp.where(first, ms[0] + jnp.log(ls[0]), ms[1] + jnp.log(ls[1]))

    qblk = pl.BlockSpec((tq, LANES), lambda h, i: (i, h))
    full = pl.BlockSpec((S, LANES), lambda h, i: (0, h))
    return pl.pallas_call(
        body, grid=(W // LANES, S // tq),
        in_specs=[qblk, full, full, qblk, pl.BlockSpec((2, 1, S), lambda h, i: (h, 0, 0)), _ANY],
        out_specs=[pl.BlockSpec((tq, LANES), lambda h, i: (i, SSM_INNER // LANES + h)), qblk, qblk],
        out_shape=[jax.ShapeDtypeStruct(mixed.shape, mixed.dtype), jax.ShapeDtypeStruct((S, W), F32),
                   jax.ShapeDtypeStruct((S, W), F32)],
        input_output_aliases={5: 0},
        compiler_params=_params(("parallel", "parallel")), name="flash_fwd")(qs, kn, vb, cq, ck, mixed)


def _flash_bwd(qs, kn, vb, cq, ck, o_fine, do, do_col0, lse):
    S, W = qs.shape
    tq = tk = _pick(S, (512, 256))
    nq = S // tq
    nmask = max(tk // tq, 1)

    def body(q_ref, k_ref, v_ref, cq_ref, ck_ref, of_ref, do_ref, lse_ref, dq_ref, dk_ref, dv_ref, dck_ref):
        j = pl.program_id(1)

        @pl.when(j == 0)
        def _():
            dq_ref[...] = jnp.zeros_like(dq_ref)

        firstk = _first_head(tk)
        firstq = _first_head(tq)
        k2 = k_ref[...]
        v2 = v_ref[...]
        zk = jnp.zeros_like(k2)
        ka = (jnp.where(firstk, k2, zk), jnp.where(firstk, zk, k2))
        va = (jnp.where(firstk, v2, zk), jnp.where(firstk, zk, v2))
        cka = (ck_ref[0], ck_ref[1])
        col0 = j * tk

        def step(i, carry, masked):
            dk, dv, dck0, dck1 = carry
            dcks = [dck0, dck1]
            off = pl.multiple_of(i * tq, tq)
            rows = pl.ds(off, tq)
            q2 = q_ref[rows, :]
            dob = do_ref[rows, :].astype(BF16)
            prod = dob.astype(F32) * of_ref[rows, :]
            dkp, dvp, dqp = [], [], []
            for a in range(2):
                lane = pl.ds(a * HEAD_DIM, 1)
                s = _mxu(q2, ka[a], _NT) + cq_ref[rows, lane] - cka[a]
                if masked:
                    ri = lax.broadcasted_iota(jnp.int32, (tq, tk), 0) + off
                    ci = lax.broadcasted_iota(jnp.int32, (tq, tk), 1) + col0
                    s = jnp.where(ri >= ci, s, -1e30)
                p = jnp.exp(s - lse_ref[rows, lane])
                dp = _mxu(dob, va[a], _NT)
                own = jnp.where(firstq, prod, 0.0) if a == 0 else jnp.where(firstq, 0.0, prod)
                ds = p * (dp - jnp.sum(own, axis=-1, keepdims=True))
                dsb = ds.astype(BF16)
                dvp.append(_mxu(p.astype(BF16), dob, _TN))
                dkp.append(_mxu(dsb, q2, _TN))
                dqp.append(_mxu(dsb, k2))
                dcks[a] = dcks[a] - jnp.sum(ds, axis=0, keepdims=True)
            dq_ref[rows, :] += jnp.where(firstq, dqp[0], dqp[1])
            dk = dk + jnp.where(firstk, dkp[0], dkp[1])
            dv = dv + jnp.where(firstk, dvp[0], dvp[1])
            return dk, dv, dcks[0], dcks[1]

        z2 = jnp.zeros((tk, LANES), F32)
        z1 = jnp.zeros((1, tk), F32)
        carry = (z2, z2, z1, z1)
        i0 = (j * tk) // tq
        for ii in range(nmask):
            carry = step(i0 + ii, carry, True)
        dk, dv, dck0, dck1 = lax.fori_loop(i0 + nmask, nq, lambda i, c: step(i, c, False), carry)
        dk_ref[...] = dk
        dv_ref[...] = dv.astype(dv_ref.dtype)
        dck_ref[0] = dck0
        dck_ref[1] = dck1

    kblk = pl.BlockSpec((tk, LANES), lambda h, j: (j, h))
    full = pl.BlockSpec((S, LANES), lambda h, j: (0, h))
    dofull = pl.BlockSpec((S, LANES), lambda h, j: (0, do_col0 // LANES + h))
    rowt = pl.BlockSpec((2, 1, tk), lambda h, j: (h, 0, j))
    dvblk = pl.BlockSpec((tk, LANES), lambda h, j: (j, COL_V // LANES + h))
    return pl.pallas_call(
        body, grid=(W // LANES, S // tk),
        in_specs=[full, kblk, kblk, full, rowt, full, dofull, full],
        out_specs=[full, kblk, dvblk, rowt],
        out_shape=[jax.ShapeDtypeStruct((S, W), F32), jax.ShapeDtypeStruct((S, W), F32),
                   jax.ShapeDtypeStruct((S, IN_COLS_PAD), BF16), jax.ShapeDtypeStruct((2 * (W // LANES), 1, S), F32)],
        compiler_params=_params(("parallel", "arbitrary")), name="flash_bwd")(qs, kn, vb, cq, ck, o_fine, do, lse)


XATTN_SCALE = XATTN_DIM ** -0.5


def _xq_norm(q, g):
    return _rms(q, g) * XATTN_SCALE


def _xattn_fwd(xq, kv, gq, gk):
    S = xq.shape[0]
    Mm = kv.shape[0]
    Dh = XATTN_DIM
    tq = _pick(S, (512, 256))

    def body(q_ref, k_ref, v_ref, gq_ref, gk_ref, o_ref):
        qn = _xq_norm(q_ref[...], gq_ref[...]).astype(BF16)
        kn = _rms(k_ref[...], gk_ref[...]).astype(BF16)
        s = _mxu(qn, kn, _NT)
        m = jnp.max(s, axis=-1, keepdims=True)
        p = jnp.exp(s - m)
        l = jnp.sum(p, axis=-1, keepdims=True)
        o_ref[...] = (_mxu(p.astype(BF16), v_ref[...].astype(BF16)) / l).astype(o_ref.dtype)

    vec = pl.BlockSpec((1, Dh), lambda h, i: (0, 0))
    return pl.pallas_call(
        body, grid=(XATTN_HEADS, S // tq),
        in_specs=[pl.BlockSpec((tq, Dh), lambda h, i: (i, h)), pl.BlockSpec((Mm, Dh), lambda h, i: (0, h)),
                  pl.BlockSpec((Mm, Dh), lambda h, i: (0, XATTN_HEADS + h)), vec, vec],
        out_specs=pl.BlockSpec((tq, Dh), lambda h, i: (i, h)),
        out_shape=jax.ShapeDtypeStruct((S, XATTN_HEADS * Dh), BF16),
        compiler_params=_params(("parallel", "parallel")), name="xattn_fwd")(xq, kv, kv, gq, gk)


def _xattn_bwd(xq, kv, gq, gk, do):
    S = xq.shape[0]
    Mm = kv.shape[0]
    Dh = XATTN_DIM
    tq = _pick(S, (512, 256))
    nq = S // tq

    def body(q_ref, k_ref, v_ref, gq_ref, gk_ref, do_ref, dq_ref, dk_ref, dv_ref, dgq_ref, dgk_ref, dkn_acc, dv_acc):
        h = pl.program_id(0)
        i = pl.program_id(1)

        @pl.when((h == 0) & (i == 0))
        def _():
            dgq_ref[...] = jnp.zeros_like(dgq_ref)
            dgk_ref[...] = jnp.zeros_like(dgk_ref)

        @pl.when(i == 0)
        def _():
            dkn_acc[...] = jnp.zeros_like(dkn_acc)
            dv_acc[...] = jnp.zeros_like(dv_acc)

        qn32, vq = jax.vjp(_xq_norm, q_ref[...], gq_ref[...])
        kn32, vk = jax.vjp(_rms, k_ref[...], gk_ref[...])
        qn = qn32.astype(BF16)
        kn = kn32.astype(BF16)
        vb = v_ref[...].astype(BF16)
        s = _mxu(qn, kn, _NT)
        m = jnp.max(s, axis=-1, keepdims=True)
        p = jnp.exp(s - m)
        p = p / jnp.sum(p, axis=-1, keepdims=True)
        dob = do_ref[...].astype(BF16)
        dp = _mxu(dob, vb, _NT)
        delta = jnp.sum(p * dp, axis=-1, keepdims=True)
        ds = (p * (dp - delta)).astype(BF16)
        dv_acc[...] += _mxu(p.astype(BF16), dob, _TN)
        dkn_acc[...] += _mxu(ds, qn, _TN)
        dq, dgq = vq(_mxu(ds, kn))
        dq_ref[...] = dq.astype(dq_ref.dtype)
        dgq_ref[...] += dgq

        @pl.when(i == nq - 1)
        def _():
            dk, dgk = vk(dkn_acc[...])
            dk_ref[...] = dk.astype(dk_ref.dtype)
            dv_ref[...] = dv_acc[...].astype(dv_ref.dtype)
            dgk_ref[...] += dgk

    vec = pl.BlockSpec((1, Dh), lambda h, i: (0, 0))
    qblk = pl.BlockSpec((tq, Dh), lambda h, i: (i, h))
    kblk = pl.BlockSpec((Mm, Dh), lambda h, i: (0, h))
    vblk = pl.BlockSpec((Mm, Dh), lambda h, i: (0, XATTN_HEADS + h))
    return pl.pallas_call(
        body, grid=(XATTN_HEADS, nq),
        in_specs=[qblk, kblk, vblk, vec, vec, qblk],
        out_specs=[qblk, kblk, kblk, vec, vec],
        out_shape=[jax.ShapeDtypeStruct((S, XATTN_HEADS * Dh), BF16),
                   jax.ShapeDtypeStruct((Mm, XATTN_HEADS * Dh), BF16),
                   jax.ShapeDtypeStruct((Mm, XATTN_HEADS * Dh), BF16),
                   jax.ShapeDtypeStruct((1, Dh), F32), jax.ShapeDtypeStruct((1, Dh), F32)],
        scratch_shapes=[pltpu.VMEM((Mm, Dh), F32), pltpu.VMEM((Mm, Dh), F32)],
        compiler_params=_params(("arbitrary", "arbitrary")), name="xattn_bwd")(xq, kv, kv, gq, gk, do)


def _loss_head(y, target):
    S, D = y.shape
    tr = _pick(S, (512, 256))

    def body(y_ref, t_ref, dy_ref, loss_ref):
        @pl.when(pl.program_id(0) == 0)
        def _():
            loss_ref[...] = jnp.zeros_like(loss_ref)

        err = y_ref[...] - t_ref[...]
        dy_ref[...] = err * (1.0 / D)
        loss_ref[...] += jnp.sum(err * err) * (0.5 / D)

    row = pl.BlockSpec((tr, D), lambda i: (i, 0))
    return pl.pallas_call(
        body, grid=(S // tr,), in_specs=[row, row],
        out_specs=[row, pl.BlockSpec((1, LANES), lambda i: (0, 0))],
        out_shape=[jax.ShapeDtypeStruct((S, D), F32), jax.ShapeDtypeStruct((1, LANES), F32)],
        compiler_params=_params(("arbitrary",)), name="loss_head")(y, target)


def _row_tile(R, C):
    for tr in (1024, 512, 256, 128, 64, 32, 16, 8):
        if R % tr == 0 and tr * C * 4 <= (1 << 20):
            return tr
    return R


def _chip_sum(own, from_chips, name):
    R, C = own.shape
    tr = _row_tile(R, C)

    def body(own_ref, a_ref, b_ref, c_ref, o_ref):
        total = ((own_ref[...].astype(F32) + a_ref[...].astype(F32)) + b_ref[...].astype(F32)) + c_ref[...].astype(F32)
        o_ref[...] = total.astype(o_ref.dtype)

    blk = pl.BlockSpec((tr, C), lambda i: (i, 0))
    slab = lambda s: pl.BlockSpec((None, tr, C), lambda i: (s, i, 0))
    return pl.pallas_call(
        body, grid=(R // tr,), in_specs=[blk, slab(0), slab(1), slab(2)], out_specs=blk,
        out_shape=jax.ShapeDtypeStruct((R, C), BF16),
        compiler_params=_params(("parallel",)), name=name)(own, from_chips, from_chips, from_chips)


def _adamw(w, g_mine, g_sibling, m, v, name):
    R, C = w.shape
    tr = _row_tile(R, C)
    c1 = 1.0 - ADAM_B1 ** ADAM_STEP
    c2 = 1.0 - ADAM_B2 ** ADAM_STEP

    def body(w_ref, ga_ref, gb_ref, m_ref, v_ref, g_ref, d_ref, mo_ref, vo_ref):
        g_t = ga_ref[...].astype(F32) + gb_ref[...].astype(F32)
        m_new = ADAM_B1 * m_ref[...] + (1.0 - ADAM_B1) * g_t
        v_new = ADAM_B2 * v_ref[...] + (1.0 - ADAM_B2) * (g_t * g_t)
        g_ref[...] = g_t
        d_ref[...] = -ADAM_LR * ((m_new / c1) / (jnp.sqrt(v_new / c2) + ADAM_EPS) + ADAM_WD * w_ref[...])
        mo_ref[...] = m_new
        vo_ref[...] = v_new

    blk = pl.BlockSpec((tr, C), lambda i: (i, 0))
    return pl.pallas_call(
        body, grid=(R // tr,), in_specs=[blk] * 5, out_specs=[blk] * 4,
        out_shape=[jax.ShapeDtypeStruct((R, C), F32)] * 4,
        compiler_params=_params(("parallel",)), name=name)(w, g_mine, g_sibling, m, v)


SSM_INNER = SSM_HEADS * HEAD_DIM
CONV_DIM = SSM_INNER + 2 * SSM_GROUPS * SSM_STATE
ATTN_WIDTH = ATTN_HEADS * HEAD_DIM
MIX_WIDTH = SSM_INNER + ATTN_WIDTH
COL_Z = 0
COL_XBC = COL_Z + SSM_INNER
COL_Q = COL_XBC + CONV_DIM
COL_K = COL_Q + ATTN_WIDTH
COL_V = COL_K + ATTN_WIDTH
COL_DT = COL_V + ATTN_WIDTH
COL_F = COL_DT + SSM_HEADS
IN_COLS = COL_F + ATTN_HEADS
IN_COLS_PAD = -(-IN_COLS // LANES) * LANES
REF_COL_DT = COL_Q
SHARD_COLS = IN_COLS // N_CHIPS
_COL_RANGES = ((0, REF_COL_DT, 0), (REF_COL_DT + SSM_HEADS, COL_F, COL_Q), (REF_COL_DT, REF_COL_DT + SSM_HEADS, COL_DT),
               (COL_F, IN_COLS, COL_F))


def _w_in_from_shards(g):
    parts = []
    for lo, hi, _ in _COL_RANGES:
        while lo < hi:
            j = lo // SHARD_COLS
            end = min(hi, (j + 1) * SHARD_COLS)
            parts.append(g[j][:, lo - j * SHARD_COLS:end - j * SHARD_COLS])
            lo = end
    parts.append(jnp.zeros((g.shape[1], IN_COLS_PAD - IN_COLS), g.dtype))
    return jnp.concatenate(parts, axis=1)


def _w_in_to_shards(w):
    shards = []
    for j in range(N_CHIPS):
        parts = []
        for lo, hi, here in sorted(_COL_RANGES):
            a, b = max(lo, j * SHARD_COLS), min(hi, (j + 1) * SHARD_COLS)
            if a < b:
                parts.append(w[:, here + a - lo:here + b - lo])
        shards.append(jnp.concatenate(parts, axis=1))
    return jnp.stack(shards)


def _add_residual(acc, res):
    return (res + acc,)


def _relu2(acc):
    r = jnp.maximum(acc, 0.0)
    return acc, r * r


def _relu2_bwd(acc, a):
    return (acc * (2.0 * jnp.maximum(a.astype(F32), 0.0)),)


def _layer_fwd_bwd(x, mem, target, w_in, p, late_weights, send_late_grads, send_w_in_grad):
    S = x.shape[0]
    hd3 = lambda a: a.reshape(SSM_HEADS, 1, 1)

    h1 = _rmsnorm_fwd(x, p["g_mix"], "norm_mix")
    proj = _mm(h1, w_in, "nn", "in_proj")
    xbc = _conv_fwd(proj, COL_XBC, CONV_DIM, p["conv_w"], p["conv_b"])
    dt_hm = proj[:, COL_DT:COL_DT + SSM_HEADS].T[:, :, None]
    ssd_par = (hd3(p["dt_bias"]), hd3(p["a_log"]), hd3(p["d_skip"]), p["ssm_norm_w"])
    mixed, hs = _ssd_fwd(xbc, proj, dt_hm, *ssd_par)
    f_raw = proj[:, COL_F:COL_F + ATTN_HEADS]
    gq2 = jnp.tile(p["g_q"], (1, 2))
    gk2 = jnp.tile(p["g_k"], (1, 2))
    qs, kn, vb = _qk_prep_fwd(proj, gq2, gk2)
    cum, cq = _logf_cumsum_fwd(f_raw, p["f_bias"])
    ck = cum.T[:, None, :]
    mixed, o_fine, lse = _flash_fwd(qs, kn, vb, cq, ck, mixed)
    W = late_weights((mixed,))
    x1 = _mm(mixed, W["w_out"], "nn", "out_proj", epilogue=_add_residual, extras=(x,))
    h2 = _rmsnorm_fwd(x1, p["g_xattn"], "norm_xattn")
    mem_n = _rmsnorm_fwd(mem, p["g_mem"], "norm_mem")
    xq = _mm(h2, W["xq_w"], "nn", "xq_proj")
    kv = _mm(mem_n, W["xkv_w"], "nn", "xkv_proj", b_chunks=N_CHIPS)
    xo = _xattn_fwd(xq, kv, p["xg_q"], p["xg_k"])
    x2 = _mm(xo, W["xo_w"], "nn", "xo_proj", epilogue=_add_residual, extras=(x1,))
    h3 = _rmsnorm_fwd(x2, p["g_mlp"], "norm_mlp")
    a, act = _mm(h3, W["w_up"], "nn", "mlp_up", out_dtypes=(BF16, BF16), epilogue=_relu2, b_chunks=N_CHIPS)
    x3 = _mm(act, W["w_down"], "nn", "mlp_down", epilogue=_add_residual, extras=(x2,))
    dy, loss_row = _loss_head(x3, target)

    gW, gp = {}, {}
    da = _mm(dy, W["w_down"], "nt", "d_act", out_dtypes=(BF16,), epilogue=_relu2_bwd, extras=(a,))
    gW["w_down"] = _mm(act, dy, "tn", "g_w_down", out_dtypes=(BF16,))
    gW["w_up"] = _mm(h3, da, "tn", "g_w_up", out_dtypes=(BF16,), out_chunks=N_CHIPS)
    dh3 = _mm(da, W["w_up"], "nt", "d_h3", b_chunks=N_CHIPS)
    dx2, gp["g_mlp"] = _rmsnorm_bwd(x2, p["g_mlp"], dh3, dy, "norm_mlp_bwd")
    dxo = _mm(dx2, W["xo_w"], "nt", "d_xo", out_dtypes=(BF16,))
    gW["xo_w"] = _mm(xo, dx2, "tn", "g_xo_w", out_dtypes=(BF16,))
    dxq, dk_x, dv_x, gp["xg_q"], gp["xg_k"] = _xattn_bwd(xq, kv, p["xg_q"], p["xg_k"], dxo)
    dkv = jnp.concatenate([dk_x, dv_x], axis=-1)
    gW["xq_w"] = _mm(h2, dxq, "tn", "g_xq_w", out_dtypes=(BF16,))
    dh2 = _mm(dxq, W["xq_w"], "nt", "d_h2")
    gW["xkv_w"] = _mm(mem_n, dkv, "tn", "g_xkv_w", out_dtypes=(BF16,), out_chunks=N_CHIPS)
    dmem_n = _mm(dkv, W["xkv_w"], "nt", "d_mem_n", b_chunks=N_CHIPS)
    _, gp["g_mem"] = _rmsnorm_bwd(mem, p["g_mem"], dmem_n, None, "norm_mem_bwd")
    dx1, gp["g_xattn"] = _rmsnorm_bwd(x1, p["g_xattn"], dh2, dx2, "norm_xattn_bwd")
    dmixed = _mm(dx1, W["w_out"], "nt", "d_mixed")
    gW["w_out"] = _mm(mixed, dx1, "tn", "g_w_out", out_dtypes=(BF16,))
    token = send_late_grads(gW)
    dqs, dkn, dproj, dck = _flash_bwd(qs, kn, vb, cq, ck + token[:1, :1], o_fine, dmixed, SSM_INNER, lse)
    dproj, dgq2 = _pair_norm_bwd(proj, COL_Q, gq2, ATTN_SCALE, dqs, dproj, "q_norm_bwd")
    dproj, dgk2 = _pair_norm_bwd(proj, COL_K, gk2, 1.0, dkn, dproj, "k_norm_bwd")
    gp["g_q"] = dgq2[:, :HEAD_DIM] + dgq2[:, HEAD_DIM:]
    gp["g_k"] = dgk2[:, :HEAD_DIM] + dgk2[:, HEAD_DIM:]
    df, gp["f_bias"] = _logf_cumsum_bwd(f_raw, p["f_bias"], dck[:, 0, :].T)
    dxs, dproj, dB, dC, ddt, ddtb, dalog, ddsk, gp["ssm_norm_w"] = _ssd_bwd(xbc, proj, dt_hm, *ssd_par, hs, dmixed, dproj)
    gp["dt_bias"] = ddtb.reshape(1, SSM_HEADS)
    gp["a_log"] = dalog.reshape(1, SSM_HEADS)
    gp["d_skip"] = ddsk.reshape(1, SSM_HEADS)
    dproj, dconv_w, gp["conv_b"] = _conv_bwd(proj, COL_XBC, CONV_DIM, p["conv_w"], p["conv_b"], (dxs, dB, dC), dproj)
    gp["conv_w"] = dconv_w[:CONV_WIDTH]
    tail = jnp.concatenate([ddt[:, :, 0].T, df, jnp.zeros((S, IN_COLS_PAD - IN_COLS), F32)], axis=-1).astype(BF16)
    dproj = lax.dynamic_update_slice(dproj, tail, (0, COL_DT))
    token = send_w_in_grad(_mm(h1, dproj, "tn", "g_w_in", out_dtypes=(BF16,)))
    dh1 = _mm(dproj, w_in, "nt", "d_h1")
    dx, gp["g_mix"] = _rmsnorm_bwd(x, p["g_mix"] + token[:1, :1], dh1, dx1, "norm_mix_bwd")
    return loss_row, dx, gp


_ANY = pl.BlockSpec(memory_space=pl.ANY)


def _place():
    x, y, c = lax.axis_index("x"), lax.axis_index("y"), lax.axis_index("c")
    chips = [(1 - x, y), (x, 1 - y), (1 - x, 1 - y)]
    return x, y, c, chips


def _chip_index(px, py):
    return 2 * px + py


def _all_gather_chips(split, whole):
    ns, nw = len(split), len(whole)
    n = ns + nw

    def body(*refs):
        ins, outs = refs[:n], refs[n:2 * n]
        send_ici, recv_ici, send_d2d, recv_d2d = refs[2 * n:]
        x, y, c, chips = _place()
        me = _chip_index(x, y)
        sib = (x, y, 1 - c)

        def ici(k, j, src, dst):
            return pltpu.make_async_remote_copy(src_ref=src, dst_ref=dst, send_sem=send_ici.at[3 * k + j],
                                                recv_sem=recv_ici.at[3 * k + j], device_id=(*chips[j], c),
                                                device_id_type=MESH)

        def d2d(k, j, piece):
            return pltpu.make_async_remote_copy(src_ref=piece, dst_ref=piece, send_sem=send_d2d.at[3 * k + j],
                                                recv_sem=recv_d2d.at[3 * k + j], device_id=sib, device_id_type=MESH)

        sends = []
        for k in range(n):
            for j in range(3):
                if k < ns:
                    sends.append(ici(k, j, ins[k].at[c], outs[k].at[me, c]))
                else:
                    sends.append(ici(k, j, ins[k], outs[k].at[me]))
                sends[-1].start()
        passed = []
        for k in range(n):
            for j in range(3):
                src_chip = _chip_index(*chips[j])
                if k < ns:
                    ici(k, j, ins[k].at[c], outs[k].at[src_chip, c]).wait_recv()
                    passed.append(d2d(k, j, outs[k].at[src_chip, c]))
                    passed[-1].start()
                else:
                    ici(k, j, ins[k], outs[k].at[src_chip]).wait_recv()
        for k in range(ns):
            for j in range(3):
                d2d(k, j, outs[k].at[_chip_index(*chips[j]), 1 - c]).wait_recv()
        for cp in sends + passed:
            cp.wait_send()

    arrs = list(split) + list(whole)
    return pl.pallas_call(
        body, in_specs=[_ANY] * n, out_specs=[_ANY] * n,
        out_shape=[jax.ShapeDtypeStruct((N_CHIPS,) + a.shape, a.dtype) for a in arrs],
        scratch_shapes=[pltpu.SemaphoreType.DMA((3 * n,)), pltpu.SemaphoreType.DMA((3 * n,)),
                        pltpu.SemaphoreType.DMA((3 * ns,)), pltpu.SemaphoreType.DMA((3 * ns,))],
        name="all_gather_chips")(*arrs)


def _sibling_swap(arrs, name):
    n = len(arrs)

    def body(*refs):
        ins, outs = refs[:n], refs[n:2 * n]
        send_sem, recv_sem = refs[2 * n:]
        x, y, c, _ = _place()
        copies = [pltpu.make_async_remote_copy(src_ref=ins[k], dst_ref=outs[k], send_sem=send_sem.at[k],
                                               recv_sem=recv_sem.at[k], device_id=(x, y, 1 - c), device_id_type=MESH)
                  for k in range(n)]
        for q in copies:
            q.start()
        for q in copies:
            q.wait()

    return pl.pallas_call(
        body, in_specs=[_ANY] * n, out_specs=[_ANY] * n,
        out_shape=[jax.ShapeDtypeStruct(a.shape, a.dtype) for a in arrs],
        scratch_shapes=[pltpu.SemaphoreType.DMA((n,)), pltpu.SemaphoreType.DMA((n,))],
        name=name)(*arrs)


_HBM = pl.BlockSpec(memory_space=pltpu.HBM)
_SEM = pl.BlockSpec(memory_space=pltpu.SEMAPHORE)
_SPLIT_EFFECT = pltpu.SideEffectType.DATAFLOW_SIDE_EFFECTING


class _Split(NamedTuple):
    send_sems: jax.Array
    recv_sems: jax.Array
    sources: tuple
    lands: tuple
    token: jax.Array


def _split_peers(kind):
    return N_DEV - 1 if kind == "everyone" else N_CHIPS - 1


def _split_copies(kind, srcs, lands, send_sems, recv_sems):
    x, y, c, chips = _place()
    me = _chip_index(x, y)
    if kind == "everyone":
        peers = [(x ^ ((r >> 2) & 1), y ^ ((r >> 1) & 1), c ^ (r & 1)) for r in range(1, N_DEV)]
    else:
        peers = [(*chip, c) for chip in chips]
    copies = []
    for k in range(len(srcs)):
        for j, peer in enumerate(peers):
            if kind == "gather":
                src, dst = srcs[k], lands[k].at[me]
            elif kind == "scatter":
                src, dst = srcs[k].at[_chip_index(*chips[j])], lands[k].at[j]
            else:
                src, dst = srcs[k], lands[k].at[2 * me + c]
            sem = len(peers) * k + j
            copies.append(pltpu.make_async_remote_copy(
                src_ref=src, dst_ref=dst, send_sem=send_sems.at[sem], recv_sem=recv_sems.at[sem],
                device_id=peer, device_id_type=MESH))
    return copies


def _split_start(name, sources, kind, after):
    n = len(sources)
    if kind == "gather":
        lands = [lax.empty((N_CHIPS,) + s.shape, s.dtype) for s in sources]
    elif kind == "scatter":
        lands = [lax.empty((N_CHIPS - 1,) + s.shape[1:], s.dtype) for s in sources]
    else:
        lands = [lax.empty((N_DEV,) + s.shape, s.dtype) for s in sources]
    n_sems = _split_peers(kind) * n
    deps = [] if after is None else [after]

    def body(*refs):
        srcs, lnds = refs[:n], refs[n:2 * n]
        send_sems, recv_sems = refs[2 * n + len(deps)], refs[2 * n + len(deps) + 1]
        for cp in _split_copies(kind, srcs, lnds, send_sems, recv_sems):
            cp.start()
        refs[-1][...] = jnp.zeros_like(refs[-1])

    hbm = lambda a: pltpu.with_memory_space_constraint(a, pltpu.HBM)
    outs = pl.pallas_call(
        body, name=name,
        in_specs=[_HBM] * (2 * n) + [_ANY] * len(deps),
        out_specs=[_SEM, _SEM] + [_HBM] * (2 * n) + [pl.BlockSpec(memory_space=pltpu.VMEM)],
        out_shape=[pltpu.SemaphoreType.DMA((n_sems,)), pltpu.SemaphoreType.DMA((n_sems,))]
        + [pltpu.HBM(a.shape, a.dtype) for a in list(sources) + lands] + [jax.ShapeDtypeStruct((8, LANES), F32)],
        input_output_aliases={k: 2 + k for k in range(2 * n)},
        compiler_params=pltpu.CompilerParams(has_side_effects=_SPLIT_EFFECT),
    )(*[hbm(s) for s in sources], *[hbm(l) for l in lands], *deps)
    return _Split(outs[0], outs[1], tuple(outs[2:2 + n]), tuple(outs[2 + n:2 + 2 * n]), outs[-1])


def _split_wait(name, h, kind, after):
    n = len(h.sources)

    def body(*refs):
        srcs, lnds = refs[:n], refs[n:2 * n]
        for cp in _split_copies(kind, srcs, lnds, refs[2 * n], refs[2 * n + 1]):
            cp.wait_send()
            cp.wait_recv()

    outs = pl.pallas_call(
        body, name=name,
        in_specs=[_HBM] * (2 * n) + [_SEM, _SEM] + [_ANY] * len(after),
        out_specs=[_HBM] * (2 * n),
        out_shape=[pltpu.HBM(a.shape, a.dtype) for a in h.sources + h.lands],
        input_output_aliases={k: k for k in range(2 * n)},
        compiler_params=pltpu.CompilerParams(has_side_effects=_SPLIT_EFFECT),
    )(*h.sources, *h.lands, h.send_sems, h.recv_sems, *after)
    return outs[:n], outs[n:]


def _sum_devices(parts):
    def body(p_ref, o_ref):
        acc = p_ref[0]
        for d in range(1, N_DEV):
            acc = acc + p_ref[d]
        o_ref[...] = acc

    vm = pl.BlockSpec(memory_space=pltpu.VMEM)
    return pl.pallas_call(body, in_specs=[vm], out_specs=vm, out_shape=jax.ShapeDtypeStruct(parts.shape[1:], F32),
                          name="sum_devices")(parts)


_INPUTS = ["x", "mem", "g_mix", "w_in", "conv_w", "conv_b", "dt_bias", "a_log", "d_skip", "ssm_norm_w", "g_q", "g_k",
           "f_bias", "w_out", "g_xattn", "g_mem", "xq_w", "xkv_w", "xg_q", "xg_k", "xo_w", "g_mlp", "w_up", "w_down"]
_WEIGHTS = _INPUTS[2:]
_BIG = ["w_in", "w_out", "xq_w", "xkv_w", "xo_w", "w_up", "w_down"]
_LATE = _BIG[1:]
_COL_SHARDED = ["w_in", "xkv_w", "w_up"]
_SMALL = [n for n in _WEIGHTS if n not in _BIG]


def _pack_rows(arrs, width):
    starts, r = [], 0
    for a in arrs:
        starts.append(r)
        r += a.shape[0]
    out = jnp.concatenate([jnp.pad(a, ((0, 0), (0, width - a.shape[1]))) for a in arrs], axis=0)
    return jnp.pad(out, ((0, -r % 8), (0, 0))), starts


def _adamw_small(summed, starts, ws, ms, vs, conv_w_index):
    n = len(ws)
    c1 = 1.0 - ADAM_B1 ** ADAM_STEP
    c2 = 1.0 - ADAM_B2 ** ADAM_STEP

    def body(s_ref, *refs):
        w_refs, m_refs, v_refs = refs[:n], refs[n:2 * n], refs[2 * n:3 * n]
        outs = refs[3 * n:]
        chip = _chip_index(lax.axis_index("x"), lax.axis_index("y"))
        for k in range(n):
            rows, cols = w_refs[k].shape
            if k == conv_w_index:
                g = s_ref[starts[k]:starts[k] + rows, pl.ds(pl.multiple_of(chip * cols, LANES), cols)]
            else:
                g = s_ref[starts[k]:starts[k] + rows, 0:cols]
            m_new = ADAM_B1 * m_refs[k][...] + (1.0 - ADAM_B1) * g
            v_new = ADAM_B2 * v_refs[k][...] + (1.0 - ADAM_B2) * (g * g)
            outs[4 * k][...] = g
            outs[4 * k + 1][...] = -ADAM_LR * ((m_new / c1) / (jnp.sqrt(v_new / c2) + ADAM_EPS) + ADAM_WD * w_refs[k][...])
            outs[4 * k + 2][...] = m_new
            outs[4 * k + 3][...] = v_new

    vm = pl.BlockSpec(memory_space=pltpu.VMEM)
    outs = pl.pallas_call(
        body, in_specs=[vm] * (1 + 3 * n), out_specs=[vm] * (4 * n),
        out_shape=[jax.ShapeDtypeStruct(a.shape, F32) for a in ws for _ in range(4)],
        name="adamw_small")(summed, *ws, *ms, *vs)
    return [outs[4 * k:4 * k + 4] for k in range(n)]


def kernel(x, mem, g_mix, w_in, conv_w, conv_b, dt_bias, a_log, d_skip, ssm_norm_w, g_q, g_k, f_bias, w_out, g_xattn, g_mem, xq_w, xkv_w, xg_q, xg_k, xo_w, g_mlp, w_up, w_down, loss_target, m_g_mix, m_w_in, m_conv_w, m_conv_b, m_dt_bias, m_a_log, m_d_skip, m_ssm_norm_w, m_g_q, m_g_k, m_f_bias, m_w_out, m_g_xattn, m_g_mem, m_xq_w, m_xkv_w, m_xg_q, m_xg_k, m_xo_w, m_g_mlp, m_w_up, m_w_down, v_g_mix, v_w_in, v_conv_w, v_conv_b, v_dt_bias, v_a_log, v_d_skip, v_ssm_norm_w, v_g_q, v_g_k, v_f_bias, v_w_out, v_g_xattn, v_g_mem, v_xq_w, v_xkv_w, v_xg_q, v_xg_k, v_xo_w, v_g_mlp, v_w_up, v_w_down):
    args = (x, mem, g_mix, w_in, conv_w, conv_b, dt_bias, a_log, d_skip, ssm_norm_w, g_q, g_k, f_bias, w_out, g_xattn,
            g_mem, xq_w, xkv_w, xg_q, xg_k, xo_w, g_mlp, w_up, w_down)
    w = dict(zip(_INPUTS, args))
    mom1 = dict(zip(_WEIGHTS, (m_g_mix, m_w_in, m_conv_w, m_conv_b, m_dt_bias, m_a_log, m_d_skip, m_ssm_norm_w, m_g_q,
                               m_g_k, m_f_bias, m_w_out, m_g_xattn, m_g_mem, m_xq_w, m_xkv_w, m_xg_q, m_xg_k, m_xo_w,
                               m_g_mlp, m_w_up, m_w_down)))
    mom2 = dict(zip(_WEIGHTS, (v_g_mix, v_w_in, v_conv_w, v_conv_b, v_dt_bias, v_a_log, v_d_skip, v_ssm_norm_w, v_g_q,
                               v_g_k, v_f_bias, v_w_out, v_g_xattn, v_g_mem, v_xq_w, v_xkv_w, v_xg_q, v_xg_k, v_xo_w,
                               v_g_mlp, v_w_up, v_w_down)))
    chip = _chip_index(lax.axis_index("x"), lax.axis_index("y"))

    shard_bf = {n: w[n][0].astype(BF16) for n in _BIG}

    def layout_for_compute(n, g):
        if n == "w_in":
            return _w_in_from_shards(g)
        return g if n in _COL_SHARDED else g.reshape(N_CHIPS * g.shape[1], g.shape[2])

    def layout_for_reduction(n, g):
        if n == "w_in":
            return _w_in_to_shards(g)
        return g if n in _COL_SHARDED else g.reshape(N_CHIPS, g.shape[0] // N_CHIPS, g.shape[1])

    halves_in = shard_bf["w_in"].reshape(2, shard_bf["w_in"].shape[0] // 2, -1)
    g_in, g_conv = _all_gather_chips([halves_in], [w["conv_w"][0]])
    g_in = lax.dynamic_update_index_in_dim(g_in, halves_in, chip, axis=0)
    g_conv = lax.dynamic_update_index_in_dim(g_conv, w["conv_w"][0], chip, axis=0)
    w_in_full = layout_for_compute("w_in", g_in.reshape(N_CHIPS, -1, g_in.shape[-1]))
    p = {n: w[n] for n in _SMALL}
    p["conv_w"] = g_conv.transpose(1, 0, 2).reshape(CONV_WIDTH, CONV_DIM)
    gather = _split_start("gather_late", [shard_bf[n] for n in _LATE], "gather", after=g_in)
    p["g_mix"] = p["g_mix"] + gather.token[:1, :1]

    def late_weights(after):
        srcs, lands = _split_wait("gather_late_wait", gather, "gather", after)
        lands = [lax.dynamic_update_index_in_dim(l, s, chip, axis=0) for l, s in zip(lands, srcs)]
        return {n: layout_for_compute(n, l) for n, l in zip(_LATE, lands)}

    scatter = {}

    def send_late_grads(grads):
        scatter["late"] = _split_start("scatter_late", [layout_for_reduction(n, grads[n]) for n in _LATE], "scatter",
                                       after=None)
        return scatter["late"].token

    def send_w_in_grad(g):
        scatter["w_in"] = _split_start("scatter_w_in", [layout_for_reduction("w_in", g)], "scatter", after=None)
        return scatter["w_in"].token

    loss_row, dx, gp = _layer_fwd_bwd(x[0], mem[0], loss_target[0], w_in_full, p, late_weights, send_late_grads,
                                      send_w_in_grad)

    grad, delta, new_m, new_v = {}, {}, {}, {}

    def finish(names, sources, from_chips, tag):
        mine = [_chip_sum(lax.dynamic_index_in_dim(s, chip, axis=0, keepdims=False), fc, "rs_chip_sum_" + n)
                for n, s, fc in zip(names, sources, from_chips)]
        for n, a, b in zip(names, mine, _sibling_swap(mine, "rs_sibling_swap_" + tag)):
            shape = w[n].shape
            res = _adamw(w[n][0], a, b, mom1[n][0], mom2[n][0], "adamw_" + n)
            grad[n], delta[n], new_m[n], new_v[n] = (r.reshape(shape) for r in res)

    finish(_LATE, *_split_wait("scatter_late_wait", scatter["late"], "scatter", (dx,)), "late")

    sources_in, from_chips_in = _split_wait("scatter_w_in_wait", scatter["w_in"], "scatter",
                                            tuple(new_v[n] for n in _LATE))

    packed, starts = _pack_rows([gp[n] for n in _SMALL] + [loss_row], CONV_DIM)
    small = _split_start("small_all_gather", [packed], "everyone", after=from_chips_in[0])
    finish(["w_in"], sources_in, from_chips_in, "w_in")
    (packed,), (from_all,) = _split_wait("small_all_gather_wait", small, "everyone", (new_v["w_in"],))
    device = 2 * chip + lax.axis_index("c")
    summed = _sum_devices(lax.dynamic_update_index_in_dim(from_all, packed, device, axis=0))
    loss = summed[starts[-1], 0]

    as_rows = lambda a: a.reshape(-1, a.shape[-1])
    results = _adamw_small(summed, starts, [as_rows(w[n]) for n in _SMALL], [as_rows(mom1[n]) for n in _SMALL],
                           [as_rows(mom2[n]) for n in _SMALL], _SMALL.index("conv_w"))
    for n, res in zip(_SMALL, results):
        grad[n], delta[n], new_m[n], new_v[n] = (a.reshape(w[n].shape) for a in res)

    return (loss, dx[None], *[grad[n] for n in _WEIGHTS], *[delta[n] for n in _WEIGHTS],
            *[new_m[n] for n in _WEIGHTS], *[new_v[n] for n in _WEIGHTS])
```

```python
from typing import NamedTuple

import jax
import jax.numpy as jnp
from jax import lax
from jax.experimental import pallas as pl
from jax.experimental.pallas import tpu as pltpu

F32 = jnp.float32
BF16 = jnp.bfloat16
HI = lax.Precision.HIGHEST
MESH = pl.DeviceIdType.MESH

EPS = 1e-5
CHUNK = 128
SSM_HEADS = 16
SSM_GROUPS = 2
HEADS_PER_GROUP = SSM_HEADS // SSM_GROUPS
HEAD_DIM = 64
SSM_STATE = 128
ATTN_HEADS = 16
XATTN_HEADS = 4
XATTN_DIM = 256
CONV_WIDTH = 4
CONV_COLS = 256
N_CHIPS = 4
N_DEV = 8
LANES = 128
VMEM_LIMIT = 56 * 1024 * 1024

ADAM_LR = 0.001
ADAM_B1 = 0.9
ADAM_B2 = 0.999
ADAM_EPS = 1e-08
ADAM_WD = 0.01
ADAM_STEP = 10


def _params(sem):
    return pltpu.CompilerParams(dimension_semantics=sem, vmem_limit_bytes=VMEM_LIMIT)


def _pick(n, cands):
    for c in cands:
        if n % c == 0:
            return c
    return n


def _mm(a, b, mode, name, out_dtypes=(F32,), epilogue=None, extras=(), b_chunks=1, out_chunks=1,
        tm=None, tn=None, tk=None):
    if mode == "nn":
        M, K = a.shape
        N = b.shape[-1] * b_chunks
    elif mode == "nt":
        M, K = a.shape
        N = b.shape[-2]
        assert b.shape[-1] * b_chunks == K
    else:
        K, M = a.shape
        N = b.shape[-1] * b_chunks
    tm = tm or _pick(M, (2048, 1024, 512, 256, 128))
    tn = tn or _pick(N // max(b_chunks if mode != "nt" else 1, out_chunks), (512, 640, 384, 256, 128))
    if tk is None:
        kmax = b.shape[-1] if mode == "nt" else K
        tk = kmax if kmax <= 2048 else _pick(kmax, (2048, 1920, 1152, 1024, 512))
    nk = K // tk
    assert M % tm == 0 and N % tn == 0 and K % tk == 0
    grid = (M // tm, N // tn, nk)

    if mode == "tn":
        a_spec = pl.BlockSpec((tk, tm), lambda i, j, k: (k, i))
    else:
        a_spec = pl.BlockSpec((tm, tk), lambda i, j, k: (i, k))

    def b_index(t_row, t_last, tile_last):
        if b_chunks == 1:
            return (t_row, t_last)
        q = (b.shape[-1]) // tile_last
        return (t_last // q, t_row, t_last % q)

    if mode == "nn" or mode == "tn":
        bshape = (tk, tn)
        bmap = lambda i, j, k: b_index(k, j, tn)
    else:
        bshape = (tn, tk)
        bmap = lambda i, j, k: b_index(j, k, tk)
    if b_chunks > 1:
        bshape = (None,) + bshape
    b_spec = pl.BlockSpec(bshape, bmap)

    if out_chunks == 1:
        o_spec = pl.BlockSpec((tm, tn), lambda i, j, k: (i, j))
        o_shape = (M, N)
    else:
        qo = (N // out_chunks) // tn
        o_spec = pl.BlockSpec((None, tm, tn), lambda i, j, k: (j // qo, i, j % qo))
        o_shape = (out_chunks, M, N // out_chunks)
    e_spec = pl.BlockSpec((tm, tn), lambda i, j, k: (i, j))

    dims = {"nn": (((1,), (0,)), ((), ())), "nt": (((1,), (1,)), ((), ())), "tn": (((0,), (0,)), ((), ()))}[mode]
    n_ex = len(extras)
    n_out = len(out_dtypes)

    def body(*refs):
        a_ref, b_ref = refs[0], refs[1]
        ex_refs = refs[2:2 + n_ex]
        o_refs = refs[2 + n_ex:2 + n_ex + n_out]

        def finish(acc):
            outs = epilogue(acc, *[r[...] for r in ex_refs]) if epilogue is not None else (acc,)
            for r, o in zip(o_refs, outs):
                r[...] = o.astype(r.dtype)

        part = lax.dot_general(a_ref[...].astype(BF16), b_ref[...].astype(BF16), dims,
                               preferred_element_type=F32)
        if nk == 1:
            finish(part)
        else:
            acc_ref = refs[-1]
            k = pl.program_id(2)

            @pl.when(k == 0)
            def _():
                acc_ref[...] = part

            @pl.when(k > 0)
            def _():
                acc_ref[...] += part

            @pl.when(k == nk - 1)
            def _():
                finish(acc_ref[...])

    outs = pl.pallas_call(
        body,
        grid=grid,
        in_specs=[a_spec, b_spec] + [e_spec] * n_ex,
        out_specs=[o_spec] * n_out,
        out_shape=[jax.ShapeDtypeStruct(o_shape, d) for d in out_dtypes],
        scratch_shapes=[pltpu.VMEM((tm, tn), F32)] if nk > 1 else [],
        compiler_params=_params(("parallel", "parallel", "arbitrary")),
        name=name,
    )(a, b, *extras)
    return outs[0] if n_out == 1 else outs


def _rms(x, g):
    r = lax.rsqrt(jnp.mean(x * x, axis=-1, keepdims=True) + EPS)
    return x * r * g


def _rmsnorm_fwd(x, g, name):
    R, D = x.shape
    tr = _pick(R, (512, 256))

    def body(x_ref, g_ref, o_ref):
        o_ref[...] = _rms(x_ref[...], g_ref[...]).astype(o_ref.dtype)

    return pl.pallas_call(
        body, grid=(R // tr,),
        in_specs=[pl.BlockSpec((tr, D), lambda i: (i, 0)), pl.BlockSpec((1, D), lambda i: (0, 0))],
        out_specs=pl.BlockSpec((tr, D), lambda i: (i, 0)),
        out_shape=jax.ShapeDtypeStruct((R, D), BF16),
        compiler_params=_params(("parallel",)), name=name)(x, g)


def _rmsnorm_bwd(x, g, dh, dres, name):
    R, D = x.shape
    tr = _pick(R, (256,))
    has_res = dres is not None

    def body(*refs):
        if has_res:
            x_ref, g_ref, dh_ref, dres_ref, dx_ref, dg_ref = refs
        else:
            x_ref, g_ref, dh_ref, dx_ref, dg_ref = refs
        _, vjp = jax.vjp(_rms, x_ref[...], g_ref[...])
        dx, dg = vjp(dh_ref[...])
        if has_res:
            dx = dx + dres_ref[...]
        dx_ref[...] = dx

        @pl.when(pl.program_id(0) == 0)
        def _():
            dg_ref[...] = jnp.zeros_like(dg_ref)

        dg_ref[...] += dg

    row = pl.BlockSpec((tr, D), lambda i: (i, 0))
    vec = pl.BlockSpec((1, D), lambda i: (0, 0))
    ins = [x, g, dh] + ([dres] if has_res else [])
    return pl.pallas_call(
        body, grid=(R // tr,),
        in_specs=[row, vec, row] + ([row] if has_res else []),
        out_specs=[row, vec],
        out_shape=[jax.ShapeDtypeStruct((R, D), F32), jax.ShapeDtypeStruct((1, D), F32)],
        compiler_params=_params(("arbitrary",)), name=name)(*ins)


def _shift_down(u, k):
    if k == 0:
        return u
    rows = lax.broadcasted_iota(jnp.int32, u.shape, 0)
    return jnp.where(rows >= k, pltpu.roll(u, k, axis=0), 0.0)


def _shift_up(u, k):
    if k == 0:
        return u
    n = u.shape[0]
    rows = lax.broadcasted_iota(jnp.int32, u.shape, 0)
    return jnp.where(rows < n - k, pltpu.roll(u, n - k, axis=0), 0.0)


def _conv_pre(u, w, b):
    pre = b
    for j in range(CONV_WIDTH):
        pre = pre + w[j:j + 1, :] * _shift_down(u, CONV_WIDTH - 1 - j)
    return pre


def _conv_fwd(proj, col0, ncols, conv_w, conv_b):
    S = proj.shape[0]
    cb0 = col0 // CONV_COLS

    def body(u_ref, w_ref, b_ref, o_ref):
        pre = _conv_pre(u_ref[...], w_ref[...], b_ref[...])
        o_ref[...] = pre * jax.nn.sigmoid(pre)

    return pl.pallas_call(
        body, grid=(ncols // CONV_COLS,),
        in_specs=[pl.BlockSpec((S, CONV_COLS), lambda j: (0, j + cb0)),
                  pl.BlockSpec((CONV_WIDTH, CONV_COLS), lambda j: (0, j)),
                  pl.BlockSpec((1, CONV_COLS), lambda j: (0, j))],
        out_specs=pl.BlockSpec((S, CONV_COLS), lambda j: (0, j)),
        out_shape=jax.ShapeDtypeStruct((S, ncols), F32),
        compiler_params=_params(("parallel",)), name="conv_fwd")(proj, conv_w, conv_b)


def _conv_bwd(proj, col0, ncols, conv_w, conv_b, douts, dproj):
    S = proj.shape[0]
    cb0 = col0 // CONV_COLS
    starts = [0]
    for d in douts:
        starts.append(starts[-1] + d.shape[1] // CONV_COLS)
    assert starts[-1] == ncols // CONV_COLS
    nd = len(douts)

    def body(u_ref, w_ref, b_ref, *rest):
        d_refs, (du_ref, dw_ref, db_ref) = rest[:nd], rest[nd + 1:]
        j = pl.program_id(0)
        dout = d_refs[-1][...]
        for i in range(nd - 2, -1, -1):
            dout = jnp.where(j < starts[i + 1], d_refs[i][...], dout)
        u = u_ref[...]
        w = w_ref[...]
        pre = _conv_pre(u, w, b_ref[...])
        s = jax.nn.sigmoid(pre)
        dpre = dout * (s * (1.0 + pre * (1.0 - s)))
        du = jnp.zeros_like(u)
        rows = []
        for j in range(CONV_WIDTH):
            k = CONV_WIDTH - 1 - j
            du = du + w[j:j + 1, :] * _shift_up(dpre, k)
            rows.append(jnp.sum(dpre * _shift_down(u, k), axis=0, keepdims=True))
        du_ref[...] = du.astype(du_ref.dtype)
        rows.append(jnp.zeros((8 - CONV_WIDTH, CONV_COLS), F32))
        dw_ref[...] = jnp.concatenate(rows, axis=0)
        db_ref[...] = jnp.sum(dpre, axis=0, keepdims=True)

    return pl.pallas_call(
        body, grid=(ncols // CONV_COLS,),
        in_specs=[pl.BlockSpec((S, CONV_COLS), lambda j: (0, j + cb0)),
                  pl.BlockSpec((CONV_WIDTH, CONV_COLS), lambda j: (0, j)),
                  pl.BlockSpec((1, CONV_COLS), lambda j: (0, j))]
        + [pl.BlockSpec((S, CONV_COLS), lambda j, lo=starts[i], hi=starts[i + 1]: (0, jnp.clip(j - lo, 0, hi - lo - 1)))
           for i in range(nd)] + [_ANY],
        out_specs=[pl.BlockSpec((S, CONV_COLS), lambda j: (0, j + cb0)),
                   pl.BlockSpec((8, CONV_COLS), lambda j: (0, j)),
                   pl.BlockSpec((1, CONV_COLS), lambda j: (0, j))],
        out_shape=[jax.ShapeDtypeStruct(dproj.shape, dproj.dtype),
                   jax.ShapeDtypeStruct((8, ncols), F32),
                   jax.ShapeDtypeStruct((1, ncols), F32)],
        input_output_aliases={3 + nd: 0},
        compiler_params=_params(("parallel",)), name="conv_bwd")(proj, conv_w, conv_b, *douts, dproj)


def _softplus(x):
    return jnp.maximum(x, 0.0) + jnp.log1p(jnp.exp(-jnp.abs(x)))


def _dot32(a, b, dims=(((1,), (0,)), ((), ()))):
    return lax.dot_general(a, b, dims, precision=HI, preferred_element_type=F32)


def _dotd(a, b, dims=(((1,), (0,)), ((), ()))):
    return lax.dot_general(a, b, dims, preferred_element_type=F32)


PAIRS_PER_GROUP = HEADS_PER_GROUP // 2


def _ssd_chunk(xs, Bm, Cm, z, dtr, dtb, alog, dsk, nw, h):
    L = Bm.shape[0]
    ri = lax.broadcasted_iota(jnp.int32, (L, L), 0)
    ci = lax.broadcasted_iota(jnp.int32, (L, L), 1)
    causal = ri >= ci
    tril = causal.astype(F32)
    first = _first_head(L)
    first1 = _first_head(1)
    CB = _dotd(Cm, Bm, _NT)
    gated, hnew = [], []
    ssq = jnp.zeros((L, 1), F32)
    for pp in range(len(xs)):
        dts, cums, tots, decay = [], [], [], []
        for a in range(2):
            r = 2 * pp + a
            dt = _softplus(dtr[r] + dtb[r])
            dA = dt * (-jnp.exp(alog[r]))
            acs = _dot32(tril, dA)
            cc = jnp.broadcast_to(acs, (L, L))
            decay.append(CB * jnp.exp(jnp.where(causal, cc - cc.T, -1e30)))
            dts.append(dt)
            cums.append(acs)
            tots.append(jnp.sum(dA, axis=0, keepdims=True))
        dt2 = jnp.where(first, dts[0], dts[1])
        acs2 = jnp.where(first, cums[0], cums[1])
        tot2 = jnp.where(first1, tots[0], tots[1])
        dsk2 = jnp.where(first1, dsk[2 * pp], dsk[2 * pp + 1])
        X = xs[pp] * dt2
        y = (jnp.where(first, _dotd(decay[0], X), _dotd(decay[1], X)) + jnp.exp(acs2) * _dotd(Cm, h[pp])
             + dsk2 * xs[pp])
        hnew.append(jnp.exp(tot2) * h[pp] + _dotd(Bm, X * jnp.exp(tot2 - acs2), _TN))
        g = y * (z[pp] * jax.nn.sigmoid(z[pp]))
        ssq = ssq + jnp.sum(g * g, axis=-1, keepdims=True)
        gated.append(g)
    rs = lax.rsqrt(ssq / (len(xs) * LANES) + EPS)
    return [g * rs * nw[pp] for pp, g in enumerate(gated)], hnew


def _ssd_args(xs_ref, b_ref, c_ref, z_ref, dt_ref, dtb_ref, al_ref, dsk_ref, nw_ref, h_ref):
    pairs = range(PAIRS_PER_GROUP)
    heads = range(HEADS_PER_GROUP)
    lanes = lambda ref, pp: ref[:, pp * LANES:(pp + 1) * LANES]
    return ([lanes(xs_ref, pp) for pp in pairs], b_ref[...], c_ref[...], [lanes(z_ref, pp) for pp in pairs],
            [dt_ref[r] for r in heads], [dtb_ref[r] for r in heads], [al_ref[r] for r in heads],
            [dsk_ref[r] for r in heads], [lanes(nw_ref, pp) for pp in pairs], [h_ref[pp] for pp in pairs])


def _ssd_specs(rev):
    H, N, L = HEADS_PER_GROUP, SSM_STATE, CHUNK
    gw = H * HEAD_DIM
    return dict(
        cols=lambda col0: pl.BlockSpec((L, gw), lambda g, c: (rev(c), col0 // gw + g)),
        bc=lambda first_block: pl.BlockSpec((L, N), lambda g, c: (rev(c), first_block + g)),
        dt=pl.BlockSpec((H, L, 1), lambda g, c: (g, rev(c), 0)),
        scal=pl.BlockSpec((H, 1, 1), lambda g, c: (g, 0, 0)),
        nw=pl.BlockSpec((1, gw), lambda g, c: (0, g)),
        hs=pl.BlockSpec((None, PAIRS_PER_GROUP, N, LANES), lambda g, c: (rev(c), g, 0, 0)),
        b_block=SSM_INNER // N,
    )


def _ssd_fwd(xbc, proj, dt_hm, dtb, alog, dsk, nw):
    S = xbc.shape[0]
    N, L = SSM_STATE, CHUNK
    nc = S // L
    sp = _ssd_specs(lambda c: c)

    def body(xs_ref, b_ref, c_ref, z_ref, dt_ref, dtb_ref, al_ref, dsk_ref, nw_ref, y_ref, hs_ref, h_ref):
        @pl.when(pl.program_id(1) == 0)
        def _():
            h_ref[...] = jnp.zeros_like(h_ref)

        hs_ref[...] = h_ref[...]
        out, hnew = _ssd_chunk(*_ssd_args(xs_ref, b_ref, c_ref, z_ref, dt_ref, dtb_ref, al_ref, dsk_ref, nw_ref, h_ref))
        for pp in range(PAIRS_PER_GROUP):
            y_ref[:, pp * LANES:(pp + 1) * LANES] = out[pp].astype(y_ref.dtype)
            h_ref[pp] = hnew[pp]

    return pl.pallas_call(
        body, grid=(SSM_GROUPS, nc),
        in_specs=[sp["cols"](0), sp["bc"](sp["b_block"]), sp["bc"](sp["b_block"] + SSM_GROUPS), sp["cols"](COL_Z),
                  sp["dt"], sp["scal"], sp["scal"], sp["scal"], sp["nw"]],
        out_specs=[sp["cols"](0), sp["hs"]],
        out_shape=[jax.ShapeDtypeStruct((S, MIX_WIDTH), BF16),
                   jax.ShapeDtypeStruct((nc, SSM_HEADS // 2, N, LANES), F32)],
        scratch_shapes=[pltpu.VMEM((PAIRS_PER_GROUP, N, LANES), F32)],
        compiler_params=_params(("parallel", "arbitrary")), name="ssd_fwd",
    )(xbc, xbc, xbc, proj, dt_hm, dtb, alog, dsk, nw)


def _ssd_bwd(xbc, proj, dt_hm, dtb, alog, dsk, nw, hs, dmixed, dproj):
    S = xbc.shape[0]
    N, L = SSM_STATE, CHUNK
    nc = S // L
    sp = _ssd_specs(lambda c: nc - 1 - c)

    def body(xs_ref, b_ref, c_ref, z_ref, dt_ref, dtb_ref, al_ref, dsk_ref, nw_ref, hs_ref, dy_ref, buf_ref,
             dxs_ref, dz_ref, db_ref, dc_ref, ddt_ref, ddtb_ref, dal_ref, ddsk_ref, dnw_ref, dh_ref):
        @pl.when(pl.program_id(1) == 0)
        def _():
            dh_ref[...] = jnp.zeros_like(dh_ref)
            ddtb_ref[...] = jnp.zeros_like(ddtb_ref)
            dal_ref[...] = jnp.zeros_like(dal_ref)
            ddsk_ref[...] = jnp.zeros_like(ddsk_ref)
            dnw_ref[...] = jnp.zeros_like(dnw_ref)

        pairs = range(PAIRS_PER_GROUP)
        lanes = lambda pp: slice(pp * LANES, (pp + 1) * LANES)
        _, vjp = jax.vjp(_ssd_chunk, *_ssd_args(xs_ref, b_ref, c_ref, z_ref, dt_ref, dtb_ref, al_ref, dsk_ref, nw_ref,
                                                hs_ref))
        dxs, dB, dC, dz, ddt, ddtb, dal, ddsk, dnw, dh = vjp(([dy_ref[:, lanes(pp)] for pp in pairs],
                                                              [dh_ref[pp] for pp in pairs]))
        db_ref[...] = dB
        dc_ref[...] = dC
        for pp in pairs:
            dxs_ref[:, lanes(pp)] = dxs[pp]
            dz_ref[:, lanes(pp)] = dz[pp].astype(dz_ref.dtype)
            dnw_ref[:, lanes(pp)] += dnw[pp]
            dh_ref[pp] = dh[pp]
        for r in range(HEADS_PER_GROUP):
            ddt_ref[r] = ddt[r]
            ddtb_ref[r] += ddtb[r]
            dal_ref[r] += dal[r]
            ddsk_ref[r] += ddsk[r]

    bc_out = pl.BlockSpec((L, N), lambda g, c: (nc - 1 - c, g))
    return pl.pallas_call(
        body, grid=(SSM_GROUPS, nc),
        in_specs=[sp["cols"](0), sp["bc"](sp["b_block"]), sp["bc"](sp["b_block"] + SSM_GROUPS), sp["cols"](COL_Z),
                  sp["dt"], sp["scal"], sp["scal"], sp["scal"], sp["nw"], sp["hs"], sp["cols"](0), _ANY],
        out_specs=[sp["cols"](0), sp["cols"](COL_Z), bc_out, bc_out, sp["dt"], sp["scal"], sp["scal"], sp["scal"],
                   sp["nw"]],
        input_output_aliases={11: 1},
        out_shape=[jax.ShapeDtypeStruct((S, SSM_INNER), F32), jax.ShapeDtypeStruct(dproj.shape, dproj.dtype),
                   jax.ShapeDtypeStruct((S, SSM_GROUPS * N), F32), jax.ShapeDtypeStruct((S, SSM_GROUPS * N), F32),
                   jax.ShapeDtypeStruct((SSM_HEADS, S, 1), F32),
                   jax.ShapeDtypeStruct((SSM_HEADS, 1, 1), F32), jax.ShapeDtypeStruct((SSM_HEADS, 1, 1), F32),
                   jax.ShapeDtypeStruct((SSM_HEADS, 1, 1), F32), jax.ShapeDtypeStruct((1, SSM_INNER), F32)],
        scratch_shapes=[pltpu.VMEM((PAIRS_PER_GROUP, N, LANES), F32)],
        compiler_params=_params(("parallel", "arbitrary")), name="ssd_bwd",
    )(xbc, xbc, xbc, proj, dt_hm, dtb, alog, dsk, nw, hs, dmixed, dproj)


ATTN_SCALE = HEAD_DIM ** -0.5
PREP_COLS = 512


def _first_head(rows):
    return lax.broadcasted_iota(jnp.int32, (rows, LANES), 1) < HEAD_DIM


def _pair_norm(x, g2, scale):
    first = _first_head(x.shape[0])
    sq = x * x
    ms0 = jnp.sum(jnp.where(first, sq, 0.0), axis=-1, keepdims=True) * (1.0 / HEAD_DIM)
    ms1 = jnp.sum(jnp.where(first, 0.0, sq), axis=-1, keepdims=True) * (1.0 / HEAD_DIM)
    r = jnp.where(first, lax.rsqrt(ms0 + EPS), lax.rsqrt(ms1 + EPS))
    return x * r * g2 * scale


def _qk_prep_fwd(proj, gq2, gk2):
    S = proj.shape[0]
    tq = _pick(S, (512, 256))

    def body(q_ref, k_ref, v_ref, gq_ref, gk_ref, qo_ref, ko_ref, vo_ref):
        for b in range(PREP_COLS // LANES):
            pair = slice(b * LANES, (b + 1) * LANES)
            qo_ref[:, pair] = _pair_norm(q_ref[:, pair], gq_ref[...], ATTN_SCALE).astype(BF16)
            ko_ref[:, pair] = _pair_norm(k_ref[:, pair], gk_ref[...], 1.0).astype(BF16)
        vo_ref[...] = v_ref[...].astype(BF16)

    col = lambda c0: pl.BlockSpec((tq, PREP_COLS), lambda h, i: (i, c0 // PREP_COLS + h))
    blk = pl.BlockSpec((tq, PREP_COLS), lambda h, i: (i, h))
    vec = pl.BlockSpec((1, LANES), lambda h, i: (0, 0))
    return pl.pallas_call(
        body, grid=(ATTN_WIDTH // PREP_COLS, S // tq), in_specs=[col(COL_Q), col(COL_K), col(COL_V), vec, vec],
        out_specs=[blk, blk, blk], out_shape=[jax.ShapeDtypeStruct((S, ATTN_WIDTH), BF16)] * 3,
        compiler_params=_params(("parallel", "parallel")), name="qk_prep_fwd")(proj, proj, proj, gq2, gk2)


def _pair_norm_bwd(proj, col0, g2, scale, dn, dproj, name):
    S = proj.shape[0]
    tq = _pick(S, (512, 256))

    def body(u_ref, g_ref, dn_ref, buf_ref, du_ref, dg_ref):
        @pl.when((pl.program_id(0) == 0) & (pl.program_id(1) == 0))
        def _():
            dg_ref[...] = jnp.zeros_like(dg_ref)

        for b in range(PREP_COLS // LANES):
            pair = slice(b * LANES, (b + 1) * LANES)
            _, vjp = jax.vjp(lambda u, g: _pair_norm(u, g, scale), u_ref[:, pair], g_ref[...])
            du, dg = vjp(dn_ref[:, pair])
            du_ref[:, pair] = du.astype(du_ref.dtype)
            dg_ref[...] += dg

    ublk = pl.BlockSpec((tq, PREP_COLS), lambda h, i: (i, col0 // PREP_COLS + h))
    blk = pl.BlockSpec((tq, PREP_COLS), lambda h, i: (i, h))
    vec = pl.BlockSpec((1, LANES), lambda h, i: (0, 0))
    return pl.pallas_call(
        body, grid=(ATTN_WIDTH // PREP_COLS, S // tq), in_specs=[ublk, vec, blk, _ANY], out_specs=[ublk, vec],
        out_shape=[jax.ShapeDtypeStruct(dproj.shape, dproj.dtype), jax.ShapeDtypeStruct((1, LANES), F32)],
        input_output_aliases={3: 0},
        compiler_params=_params(("arbitrary", "arbitrary")), name=name)(proj, g2, dn, dproj)


def _logf_cumsum_fwd(f_raw, f_bias):
    S, Hh = f_raw.shape
    L = CHUNK

    def body(f_ref, b_ref, o_ref, wide_ref):
        ri = lax.broadcasted_iota(jnp.int32, (L, L), 0)
        ci = lax.broadcasted_iota(jnp.int32, (L, L), 1)
        tril = (ri >= ci).astype(F32)
        carry = jnp.zeros((1, Hh), F32)
        for c in range(S // L):
            rows = slice(c * L, (c + 1) * L)
            lf = -_softplus(-(f_ref[rows, :] + b_ref[...]))
            cum = _dot32(tril, lf) + carry
            o_ref[rows, :] = cum
            for h in range(Hh):
                wide_ref[rows, h * HEAD_DIM:(h + 1) * HEAD_DIM] = jnp.broadcast_to(cum[:, h:h + 1], (L, HEAD_DIM))
            carry = cum[L - 1:L, :]

    return pl.pallas_call(
        body, out_shape=[jax.ShapeDtypeStruct((S, Hh), F32), jax.ShapeDtypeStruct((S, Hh * HEAD_DIM), F32)],
        name="logf_cumsum_fwd")(f_raw, f_bias)


def _logf_cumsum_bwd(f_raw, f_bias, dcum):
    S, Hh = f_raw.shape
    L = CHUNK

    def body(f_ref, b_ref, d_ref, df_ref, db_ref):
        ri = lax.broadcasted_iota(jnp.int32, (L, L), 0)
        ci = lax.broadcasted_iota(jnp.int32, (L, L), 1)
        triu = (ri <= ci).astype(F32)
        carry = jnp.zeros((1, Hh), F32)
        db = jnp.zeros((1, Hh), F32)
        for c in reversed(range(S // L)):
            suf = _dot32(triu, d_ref[c * L:(c + 1) * L, :]) + carry
            df = suf * jax.nn.sigmoid(-(f_ref[c * L:(c + 1) * L, :] + b_ref[...]))
            df_ref[c * L:(c + 1) * L, :] = df
            db = db + jnp.sum(df, axis=0, keepdims=True)
            carry = suf[0:1, :]
        db_ref[...] = db

    return pl.pallas_call(
        body, out_shape=[jax.ShapeDtypeStruct((S, Hh), F32), jax.ShapeDtypeStruct((1, Hh), F32)],
        name="logf_cumsum_bwd")(f_raw, f_bias, dcum)


_NT = (((1,), (1,)), ((), ()))
_TN = (((0,), (0,)), ((), ()))


def _mxu(a, b, dims=(((1,), (0,)), ((), ()))):
    return lax.dot_general(a, b, dims, preferred_element_type=F32)


def _flash_fwd(qs, kn, vb, cq, ck, mixed):
    S, W = qs.shape
    tq = tk = _pick(S, (512, 256))
    nmask = max(tq // tk, 1)

    def body(q_ref, k_ref, v_ref, cq_ref, ck_ref, buf_ref, o_ref, of_ref, lse_ref):
        i = pl.program_id(1)
        first = _first_head(tq)
        q2 = q_ref[...]
        zero = jnp.zeros_like(q2)
        qa = (jnp.where(first, q2, zero), jnp.where(first, zero, q2))
        cqa = (cq_ref[:, 0:1], cq_ref[:, HEAD_DIM:HEAD_DIM + 1])
        row0 = i * tq

        def step(j, carry, masked):
            ms, ls, acc, rem = carry
            off = pl.multiple_of(j * tk, tk)
            k = k_ref[pl.ds(off, tk), :]
            v = v_ref[pl.ds(off, tk), :]
            new_m, new_l, alphas, pvs, prs = [], [], [], [], []
            for a in range(2):
                s = _mxu(qa[a], k, _NT) + cqa[a] - ck_ref[a, :, pl.ds(off, tk)]
                if masked:
                    ri = lax.broadcasted_iota(jnp.int32, (tq, tk), 0) + row0
                    ci = lax.broadcasted_iota(jnp.int32, (tq, tk), 1) + off
                    s = jnp.where(ri >= ci, s, -1e30)
                m_new = jnp.maximum(ms[a], jnp.max(s, axis=-1, keepdims=True))
                alpha = jnp.exp(ms[a] - m_new)
                p = jnp.exp(s - m_new)
                new_l.append(alpha * ls[a] + jnp.sum(p, axis=-1, keepdims=True))
                new_m.append(m_new)
                alphas.append(alpha)
                p_hi = p.astype(BF16)
                pvs.append(_mxu(p_hi, v))
                prs.append(_mxu((p - p_hi.astype(F32)).astype(BF16), v))
            al = jnp.where(first, alphas[0], alphas[1])
            acc = al * acc + jnp.where(first, pvs[0], pvs[1])
            rem = al * rem + jnp.where(first, prs[0], prs[1])
            return tuple(new_m), tuple(new_l), acc, rem

        neg = jnp.full((tq, 1), -1e30, F32)
        z1 = jnp.zeros((tq, 1), F32)
        z2 = jnp.zeros((tq, LANES), F32)
        carry = ((neg, neg), (z1, z1), z2, z2)
        n_full = (i * tq) // tk
        carry = lax.fori_loop(0, n_full, lambda j, c: step(j, c, False), carry)
        for jj in range(nmask):
            carry = step(n_full + jj, carry, True)
        ms, ls, acc, rem = carry
        linv = jnp.where(first, 1.0 / ls[0], 1.0 / ls[1])
        o_ref[...] = (acc * linv).astype(o_ref.dtype)
        of_ref[...] = (acc + rem) * linv
        lse_ref[...] = jnp.where(first, ms[0] + jnp.log(ls[0]), ms[1] + jnp.log(ls[1]))

    qblk = pl.BlockSpec((tq, LANES), lambda h, i: (i, h))
    full = pl.BlockSpec((S, LANES), lambda h, i: (0, h))
    return pl.pallas_call(
        body, grid=(W // LANES, S // tq),
        in_specs=[qblk, full, full, qblk, pl.BlockSpec((2, 1, S), lambda h, i: (h, 0, 0)), _ANY],
        out_specs=[pl.BlockSpec((tq, LANES), lambda h, i: (i, SSM_INNER // LANES + h)), qblk, qblk],
        out_shape=[jax.ShapeDtypeStruct(mixed.shape, mixed.dtype), jax.ShapeDtypeStruct((S, W), F32),
                   jax.ShapeDtypeStruct((S, W), F32)],
        input_output_aliases={5: 0},
        compiler_params=_params(("parallel", "parallel")), name="flash_fwd")(qs, kn, vb, cq, ck, mixed)


def _flash_bwd(qs, kn, vb, cq, ck, o_fine, do, do_col0, lse):
    S, W = qs.shape
    tq = tk = _pick(S, (512, 256))
    nq = S // tq
    nmask = max(tk // tq, 1)

    def body(q_ref, k_ref, v_ref, cq_ref, ck_ref, of_ref, do_ref, lse_ref, dq_ref, dk_ref, dv_ref, dck_ref):
        j = pl.program_id(1)

        @pl.when(j == 0)
        def _():
            dq_ref[...] = jnp.zeros_like(dq_ref)

        firstk = _first_head(tk)
        firstq = _first_head(tq)
        k2 = k_ref[...]
        v2 = v_ref[...]
        zk = jnp.zeros_like(k2)
        ka = (jnp.where(firstk, k2, zk), jnp.where(firstk, zk, k2))
        va = (jnp.where(firstk, v2, zk), jnp.where(firstk, zk, v2))
        cka = (ck_ref[0], ck_ref[1])
        col0 = j * tk

        def step(i, carry, masked):
            dk, dv, dck0, dck1 = carry
            dcks = [dck0, dck1]
            off = pl.multiple_of(i * tq, tq)
            rows = pl.ds(off, tq)
            q2 = q_ref[rows, :]
            dob = do_ref[rows, :].astype(BF16)
            prod = dob.astype(F32) * of_ref[rows, :]
            dkp, dvp, dqp = [], [], []
            for a in range(2):
                lane = pl.ds(a * HEAD_DIM, 1)
                s = _mxu(q2, ka[a], _NT) + cq_ref[rows, lane] - cka[a]
                if masked:
                    ri = lax.broadcasted_iota(jnp.int32, (tq, tk), 0) + off
                    ci = lax.broadcasted_iota(jnp.int32, (tq, tk), 1) + col0
                    s = jnp.where(ri >= ci, s, -1e30)
                p = jnp.exp(s - lse_ref[rows, lane])
                dp = _mxu(dob, va[a], _NT)
                own = jnp.where(firstq, prod, 0.0) if a == 0 else jnp.where(firstq, 0.0, prod)
                ds = p * (dp - jnp.sum(own, axis=-1, keepdims=True))
                dsb = ds.astype(BF16)
                dvp.append(_mxu(p.astype(BF16), dob, _TN))
                dkp.append(_mxu(dsb, q2, _TN))
                dqp.append(_mxu(dsb, k2))
                dcks[a] = dcks[a] - jnp.sum(ds, axis=0, keepdims=True)
            dq_ref[rows, :] += jnp.where(firstq, dqp[0], dqp[1])
            dk = dk + jnp.where(firstk, dkp[0], dkp[1])
            dv = dv + jnp.where(firstk, dvp[0], dvp[1])
            return dk, dv, dcks[0], dcks[1]

        z2 = jnp.zeros((tk, LANES), F32)
        z1 = jnp.zeros((1, tk), F32)
        carry = (z2, z2, z1, z1)
        i0 = (j * tk) // tq
        for ii in range(nmask):
            carry = step(i0 + ii, carry, True)
        dk, dv, dck0, dck1 = lax.fori_loop(i0 + nmask, nq, lambda i, c: step(i, c, False), carry)
        dk_ref[...] = dk
        dv_ref[...] = dv.astype(dv_ref.dtype)
        dck_ref[0] = dck0
        dck_ref[1] = dck1

    kblk = pl.BlockSpec((tk, LANES), lambda h, j: (j, h))
    full = pl.BlockSpec((S, LANES), lambda h, j: (0, h))
    dofull = pl.BlockSpec((S, LANES), lambda h, j: (0, do_col0 // LANES + h))
    rowt = pl.BlockSpec((2, 1, tk), lambda h, j: (h, 0, j))
    dvblk = pl.BlockSpec((tk, LANES), lambda h, j: (j, COL_V // LANES + h))
    return pl.pallas_call(
        body, grid=(W // LANES, S // tk),
        in_specs=[full, kblk, kblk, full, rowt, full, dofull, full],
        out_specs=[full, kblk, dvblk, rowt],
        out_shape=[jax.ShapeDtypeStruct((S, W), F32), jax.ShapeDtypeStruct((S, W), F32),
                   jax.ShapeDtypeStruct((S, IN_COLS_PAD), BF16), jax.ShapeDtypeStruct((2 * (W // LANES), 1, S), F32)],
        compiler_params=_params(("parallel", "arbitrary")), name="flash_bwd")(qs, kn, vb, cq, ck, o_fine, do, lse)


XATTN_SCALE = XATTN_DIM ** -0.5


def _xq_norm(q, g):
    return _rms(q, g) * XATTN_SCALE


def _xattn_fwd(xq, kv, gq, gk):
    S = xq.shape[0]
    Mm = kv.shape[0]
    Dh = XATTN_DIM
    tq = _pick(S, (512, 256))

    def body(q_ref, k_ref, v_ref, gq_ref, gk_ref, o_ref):
        qn = _xq_norm(q_ref[...], gq_ref[...]).astype(BF16)
        kn = _rms(k_ref[...], gk_ref[...]).astype(BF16)
        s = _mxu(qn, kn, _NT)
        m = jnp.max(s, axis=-1, keepdims=True)
        p = jnp.exp(s - m)
        l = jnp.sum(p, axis=-1, keepdims=True)
        o_ref[...] = (_mxu(p.astype(BF16), v_ref[...].astype(BF16)) / l).astype(o_ref.dtype)

    vec = pl.BlockSpec((1, Dh), lambda h, i: (0, 0))
    return pl.pallas_call(
        body, grid=(XATTN_HEADS, S // tq),
        in_specs=[pl.BlockSpec((tq, Dh), lambda h, i: (i, h)), pl.BlockSpec((Mm, Dh), lambda h, i: (0, h)),
                  pl.BlockSpec((Mm, Dh), lambda h, i: (0, XATTN_HEADS + h)), vec, vec],
        out_specs=pl.BlockSpec((tq, Dh), lambda h, i: (i, h)),
        out_shape=jax.ShapeDtypeStruct((S, XATTN_HEADS * Dh), BF16),
        compiler_params=_params(("parallel", "parallel")), name="xattn_fwd")(xq, kv, kv, gq, gk)


def _xattn_bwd(xq, kv, gq, gk, do):
    S = xq.shape[0]
    Mm = kv.shape[0]
    Dh = XATTN_DIM
    tq = _pick(S, (512, 256))
    nq = S // tq

    def body(q_ref, k_ref, v_ref, gq_ref, gk_ref, do_ref, dq_ref, dk_ref, dv_ref, dgq_ref, dgk_ref, dkn_acc, dv_acc):
        h = pl.program_id(0)
        i = pl.program_id(1)

        @pl.when((h == 0) & (i == 0))
        def _():
            dgq_ref[...] = jnp.zeros_like(dgq_ref)
            dgk_ref[...] = jnp.zeros_like(dgk_ref)

        @pl.when(i == 0)
        def _():
            dkn_acc[...] = jnp.zeros_like(dkn_acc)
            dv_acc[...] = jnp.zeros_like(dv_acc)

        qn32, vq = jax.vjp(_xq_norm, q_ref[...], gq_ref[...])
        kn32, vk = jax.vjp(_rms, k_ref[...], gk_ref[...])
        qn = qn32.astype(BF16)
        kn = kn32.astype(BF16)
        vb = v_ref[...].astype(BF16)
        s = _mxu(qn, kn, _NT)
        m = jnp.max(s, axis=-1, keepdims=True)
        p = jnp.exp(s - m)
        p = p / jnp.sum(p, axis=-1, keepdims=True)
        dob = do_ref[...].astype(BF16)
        dp = _mxu(dob, vb, _NT)
        delta = jnp.sum(p * dp, axis=-1, keepdims=True)
        ds = (p * (dp - delta)).astype(BF16)
        dv_acc[...] += _mxu(p.astype(BF16), dob, _TN)
        dkn_acc[...] += _mxu(ds, qn, _TN)
        dq, dgq = vq(_mxu(ds, kn))
        dq_ref[...] = dq.astype(dq_ref.dtype)
        dgq_ref[...] += dgq

        @pl.when(i == nq - 1)
        def _():
            dk, dgk = vk(dkn_acc[...])
            dk_ref[...] = dk.astype(dk_ref.dtype)
            dv_ref[...] = dv_acc[...].astype(dv_ref.dtype)
            dgk_ref[...] += dgk

    vec = pl.BlockSpec((1, Dh), lambda h, i: (0, 0))
    qblk = pl.BlockSpec((tq, Dh), lambda h, i: (i, h))
    kblk = pl.BlockSpec((Mm, Dh), lambda h, i: (0, h))
    vblk = pl.BlockSpec((Mm, Dh), lambda h, i: (0, XATTN_HEADS + h))
    return pl.pallas_call(
        body, grid=(XATTN_HEADS, nq),
        in_specs=[qblk, kblk, vblk, vec, vec, qblk],
        out_specs=[qblk, kblk, kblk, vec, vec],
        out_shape=[jax.ShapeDtypeStruct((S, XATTN_HEADS * Dh), BF16),
                   jax.ShapeDtypeStruct((Mm, XATTN_HEADS * Dh), BF16),
                   jax.ShapeDtypeStruct((Mm, XATTN_HEADS * Dh), BF16),
                   jax.ShapeDtypeStruct((1, Dh), F32), jax.ShapeDtypeStruct((1, Dh), F32)],
        scratch_shapes=[pltpu.VMEM((Mm, Dh), F32), pltpu.VMEM((Mm, Dh), F32)],
        compiler_params=_params(("arbitrary", "arbitrary")), name="xattn_bwd")(xq, kv, kv, gq, gk, do)


def _loss_head(y, target):
    S, D = y.shape
    tr = _pick(S, (512, 256))

    def body(y_ref, t_ref, dy_ref, loss_ref):
        @pl.when(pl.program_id(0) == 0)
        def _():
            loss_ref[...] = jnp.zeros_like(loss_ref)

        err = y_ref[...] - t_ref[...]
        dy_ref[...] = err * (1.0 / D)
        loss_ref[...] += jnp.sum(err * err) * (0.5 / D)

    row = pl.BlockSpec((tr, D), lambda i: (i, 0))
    return pl.pallas_call(
        body, grid=(S // tr,), in_specs=[row, row],
        out_specs=[row, pl.BlockSpec((1, LANES), lambda i: (0, 0))],
        out_shape=[jax.ShapeDtypeStruct((S, D), F32), jax.ShapeDtypeStruct((1, LANES), F32)],
        compiler_params=_params(("arbitrary",)), name="loss_head")(y, target)


def _row_tile(R, C):
    for tr in (1024, 512, 256, 128, 64, 32, 16, 8):
        if R % tr == 0 and tr * C * 4 <= (1 << 20):
            return tr
    return R


def _chip_sum(own, from_chips, name):
    R, C = own.shape
    tr = _row_tile(R, C)

    def body(own_ref, a_ref, b_ref, c_ref, o_ref):
        total = ((own_ref[...].astype(F32) + a_ref[...].astype(F32)) + b_ref[...].astype(F32)) + c_ref[...].astype(F32)
        o_ref[...] = total.astype(o_ref.dtype)

    blk = pl.BlockSpec((tr, C), lambda i: (i, 0))
    slab = lambda s: pl.BlockSpec((None, tr, C), lambda i: (s, i, 0))
    return pl.pallas_call(
        body, grid=(R // tr,), in_specs=[blk, slab(0), slab(1), slab(2)], out_specs=blk,
        out_shape=jax.ShapeDtypeStruct((R, C), BF16),
        compiler_params=_params(("parallel",)), name=name)(own, from_chips, from_chips, from_chips)


def _adamw(w, g_mine, g_sibling, m, v, name):
    R, C = w.shape
    tr = _row_tile(R, C)
    c1 = 1.0 - ADAM_B1 ** ADAM_STEP
    c2 = 1.0 - ADAM_B2 ** ADAM_STEP

    def body(w_ref, ga_ref, gb_ref, m_ref, v_ref, g_ref, d_ref, mo_ref, vo_ref):
        g_t = ga_ref[...].astype(F32) + gb_ref[...].astype(F32)
        m_new = ADAM_B1 * m_ref[...] + (1.0 - ADAM_B1) * g_t
        v_new = ADAM_B2 * v_ref[...] + (1.0 - ADAM_B2) * (g_t * g_t)
        g_ref[...] = g_t
        d_ref[...] = -ADAM_LR * ((m_new / c1) / (jnp.sqrt(v_new / c2) + ADAM_EPS) + ADAM_WD * w_ref[...])
        mo_ref[...] = m_new
        vo_ref[...] = v_new

    blk = pl.BlockSpec((tr, C), lambda i: (i, 0))
    return pl.pallas_call(
        body, grid=(R // tr,), in_specs=[blk] * 5, out_specs=[blk] * 4,
        out_shape=[jax.ShapeDtypeStruct((R, C), F32)] * 4,
        compiler_params=_params(("parallel",)), name=name)(w, g_mine, g_sibling, m, v)


SSM_INNER = SSM_HEADS * HEAD_DIM
CONV_DIM = SSM_INNER + 2 * SSM_GROUPS * SSM_STATE
ATTN_WIDTH = ATTN_HEADS * HEAD_DIM
MIX_WIDTH = SSM_INNER + ATTN_WIDTH
COL_Z = 0
COL_XBC = COL_Z + SSM_INNER
COL_Q = COL_XBC + CONV_DIM
COL_K = COL_Q + ATTN_WIDTH
COL_V = COL_K + ATTN_WIDTH
COL_DT = COL_V + ATTN_WIDTH
COL_F = COL_DT + SSM_HEADS
IN_COLS = COL_F + ATTN_HEADS
IN_COLS_PAD = -(-IN_COLS // LANES) * LANES
REF_COL_DT = COL_Q
SHARD_COLS = IN_COLS // N_CHIPS
_COL_RANGES = ((0, REF_COL_DT, 0), (REF_COL_DT + SSM_HEADS, COL_F, COL_Q), (REF_COL_DT, REF_COL_DT + SSM_HEADS, COL_DT),
               (COL_F, IN_COLS, COL_F))


def _w_in_from_shards(g):
    parts = []
    for lo, hi, _ in _COL_RANGES:
        while lo < hi:
            j = lo // SHARD_COLS
            end = min(hi, (j + 1) * SHARD_COLS)
            parts.append(g[j][:, lo - j * SHARD_COLS:end - j * SHARD_COLS])
            lo = end
    parts.append(jnp.zeros((g.shape[1], IN_COLS_PAD - IN_COLS), g.dtype))
    return jnp.concatenate(parts, axis=1)


def _w_in_to_shards(w):
    shards = []
    for j in range(N_CHIPS):
        parts = []
        for lo, hi, here in sorted(_COL_RANGES):
            a, b = max(lo, j * SHARD_COLS), min(hi, (j + 1) * SHARD_COLS)
            if a < b:
                parts.append(w[:, here + a - lo:here + b - lo])
        shards.append(jnp.concatenate(parts, axis=1))
    return jnp.stack(shards)


def _add_residual(acc, res):
    return (res + acc,)


def _relu2(acc):
    r = jnp.maximum(acc, 0.0)
    return acc, r * r


def _relu2_bwd(acc, a):
    return (acc * (2.0 * jnp.maximum(a.astype(F32), 0.0)),)


def _layer_fwd_bwd(x, mem, target, w_in, p, late_weights, send_late_grads, send_w_in_grad):
    S = x.shape[0]
    hd3 = lambda a: a.reshape(SSM_HEADS, 1, 1)

    h1 = _rmsnorm_fwd(x, p["g_mix"], "norm_mix")
    proj = _mm(h1, w_in, "nn", "in_proj")
    xbc = _conv_fwd(proj, COL_XBC, CONV_DIM, p["conv_w"], p["conv_b"])
    dt_hm = proj[:, COL_DT:COL_DT + SSM_HEADS].T[:, :, None]
    ssd_par = (hd3(p["dt_bias"]), hd3(p["a_log"]), hd3(p["d_skip"]), p["ssm_norm_w"])
    mixed, hs = _ssd_fwd(xbc, proj, dt_hm, *ssd_par)
    f_raw = proj[:, COL_F:COL_F + ATTN_HEADS]
    gq2 = jnp.tile(p["g_q"], (1, 2))
    gk2 = jnp.tile(p["g_k"], (1, 2))
    qs, kn, vb = _qk_prep_fwd(proj, gq2, gk2)
    cum, cq = _logf_cumsum_fwd(f_raw, p["f_bias"])
    ck = cum.T[:, None, :]
    mixed, o_fine, lse = _flash_fwd(qs, kn, vb, cq, ck, mixed)
    W = late_weights((mixed,))
    x1 = _mm(mixed, W["w_out"], "nn", "out_proj", epilogue=_add_residual, extras=(x,))
    h2 = _rmsnorm_fwd(x1, p["g_xattn"], "norm_xattn")
    mem_n = _rmsnorm_fwd(mem, p["g_mem"], "norm_mem")
    xq = _mm(h2, W["xq_w"], "nn", "xq_proj")
    kv = _mm(mem_n, W["xkv_w"], "nn", "xkv_proj", b_chunks=N_CHIPS)
    xo = _xattn_fwd(xq, kv, p["xg_q"], p["xg_k"])
    x2 = _mm(xo, W["xo_w"], "nn", "xo_proj", epilogue=_add_residual, extras=(x1,))
    h3 = _rmsnorm_fwd(x2, p["g_mlp"], "norm_mlp")
    a, act = _mm(h3, W["w_up"], "nn", "mlp_up", out_dtypes=(BF16, BF16), epilogue=_relu2, b_chunks=N_CHIPS)
    x3 = _mm(act, W["w_down"], "nn", "mlp_down", epilogue=_add_residual, extras=(x2,))
    dy, loss_row = _loss_head(x3, target)

    gW, gp = {}, {}
    da = _mm(dy, W["w_down"], "nt", "d_act", out_dtypes=(BF16,), epilogue=_relu2_bwd, extras=(a,))
    gW["w_down"] = _mm(act, dy, "tn", "g_w_down", out_dtypes=(BF16,))
    gW["w_up"] = _mm(h3, da, "tn", "g_w_up", out_dtypes=(BF16,), out_chunks=N_CHIPS)
    dh3 = _mm(da, W["w_up"], "nt", "d_h3", b_chunks=N_CHIPS)
    dx2, gp["g_mlp"] = _rmsnorm_bwd(x2, p["g_mlp"], dh3, dy, "norm_mlp_bwd")
    dxo = _mm(dx2, W["xo_w"], "nt", "d_xo", out_dtypes=(BF16,))
    gW["xo_w"] = _mm(xo, dx2, "tn", "g_xo_w", out_dtypes=(BF16,))
    dxq, dk_x, dv_x, gp["xg_q"], gp["xg_k"] = _xattn_bwd(xq, kv, p["xg_q"], p["xg_k"], dxo)
    dkv = jnp.concatenate([dk_x, dv_x], axis=-1)
    gW["xq_w"] = _mm(h2, dxq, "tn", "g_xq_w", out_dtypes=(BF16,))
    dh2 = _mm(dxq, W["xq_w"], "nt", "d_h2")
    gW["xkv_w"] = _mm(mem_n, dkv, "tn", "g_xkv_w", out_dtypes=(BF16,), out_chunks=N_CHIPS)
    dmem_n = _mm(dkv, W["xkv_w"], "nt", "d_mem_n", b_chunks=N_CHIPS)
    _, gp["g_mem"] = _rmsnorm_bwd(mem, p["g_mem"], dmem_n, None, "norm_mem_bwd")
    dx1, gp["g_xattn"] = _rmsnorm_bwd(x1, p["g_xattn"], dh2, dx2, "norm_xattn_bwd")
    dmixed = _mm(dx1, W["w_out"], "nt", "d_mixed")
    gW["w_out"] = _mm(mixed, dx1, "tn", "g_w_out", out_dtypes=(BF16,))
    token = send_late_grads(gW, gp["g_mem"])
    dqs, dkn, dproj, dck = _flash_bwd(qs, kn, vb, cq, ck + token[:1, :1], o_fine, dmixed, SSM_INNER, lse)
    dproj, dgq2 = _pair_norm_bwd(proj, COL_Q, gq2, ATTN_SCALE, dqs, dproj, "q_norm_bwd")
    dproj, dgk2 = _pair_norm_bwd(proj, COL_K, gk2, 1.0, dkn, dproj, "k_norm_bwd")
    gp["g_q"] = dgq2[:, :HEAD_DIM] + dgq2[:, HEAD_DIM:]
    gp["g_k"] = dgk2[:, :HEAD_DIM] + dgk2[:, HEAD_DIM:]
    df, gp["f_bias"] = _logf_cumsum_bwd(f_raw, p["f_bias"], dck[:, 0, :].T)
    dxs, dproj, dB, dC, ddt, ddtb, dalog, ddsk, gp["ssm_norm_w"] = _ssd_bwd(xbc, proj, dt_hm, *ssd_par, hs, dmixed, dproj)
    gp["dt_bias"] = ddtb.reshape(1, SSM_HEADS)
    gp["a_log"] = dalog.reshape(1, SSM_HEADS)
    gp["d_skip"] = ddsk.reshape(1, SSM_HEADS)
    dproj, dconv_w, gp["conv_b"] = _conv_bwd(proj, COL_XBC, CONV_DIM, p["conv_w"], p["conv_b"], (dxs, dB, dC), dproj)
    gp["conv_w"] = dconv_w[:CONV_WIDTH]
    tail = jnp.concatenate([ddt[:, :, 0].T, df, jnp.zeros((S, IN_COLS_PAD - IN_COLS), F32)], axis=-1).astype(BF16)
    dproj = lax.dynamic_update_slice(dproj, tail, (0, COL_DT))
    token = send_w_in_grad(_mm(h1, dproj, "tn", "g_w_in", out_dtypes=(BF16,)))
    dh1 = _mm(dproj, w_in, "nt", "d_h1")
    dx, gp["g_mix"] = _rmsnorm_bwd(x, p["g_mix"] + token[:1, :1], dh1, dx1, "norm_mix_bwd")
    return loss_row, dx, gp


_ANY = pl.BlockSpec(memory_space=pl.ANY)


def _place():
    x, y, c = lax.axis_index("x"), lax.axis_index("y"), lax.axis_index("c")
    chips = [(1 - x, y), (x, 1 - y), (1 - x, 1 - y)]
    return x, y, c, chips


def _chip_index(px, py):
    return 2 * px + py


def _all_gather_chips(split, whole):
    ns, nw = len(split), len(whole)
    n = ns + nw

    def body(*refs):
        ins, outs = refs[:n], refs[n:2 * n]
        send_ici, recv_ici, send_d2d, recv_d2d = refs[2 * n:]
        x, y, c, chips = _place()
        me = _chip_index(x, y)
        sib = (x, y, 1 - c)

        def ici(k, j, src, dst):
            return pltpu.make_async_remote_copy(src_ref=src, dst_ref=dst, send_sem=send_ici.at[3 * k + j],
                                                recv_sem=recv_ici.at[3 * k + j], device_id=(*chips[j], c),
                                                device_id_type=MESH)

        def d2d(k, j, piece):
            return pltpu.make_async_remote_copy(src_ref=piece, dst_ref=piece, send_sem=send_d2d.at[3 * k + j],
                                                recv_sem=recv_d2d.at[3 * k + j], device_id=sib, device_id_type=MESH)

        sends = []
        for k in range(n):
            for j in range(3):
                if k < ns:
                    sends.append(ici(k, j, ins[k].at[c], outs[k].at[me, c]))
                else:
                    sends.append(ici(k, j, ins[k], outs[k].at[me]))
                sends[-1].start()
        passed = []
        for k in range(n):
            for j in range(3):
                src_chip = _chip_index(*chips[j])
                if k < ns:
                    ici(k, j, ins[k].at[c], outs[k].at[src_chip, c]).wait_recv()
                    passed.append(d2d(k, j, outs[k].at[src_chip, c]))
                    passed[-1].start()
                else:
                    ici(k, j, ins[k], outs[k].at[src_chip]).wait_recv()
        for k in range(ns):
            for j in range(3):
                d2d(k, j, outs[k].at[_chip_index(*chips[j]), 1 - c]).wait_recv()
        for cp in sends + passed:
            cp.wait_send()

    arrs = list(split) + list(whole)
    return pl.pallas_call(
        body, in_specs=[_ANY] * n, out_specs=[_ANY] * n,
        out_shape=[jax.ShapeDtypeStruct((N_CHIPS,) + a.shape, a.dtype) for a in arrs],
        scratch_shapes=[pltpu.SemaphoreType.DMA((3 * n,)), pltpu.SemaphoreType.DMA((3 * n,)),
                        pltpu.SemaphoreType.DMA((3 * ns,)), pltpu.SemaphoreType.DMA((3 * ns,))],
        name="all_gather_chips")(*arrs)


def _sibling_swap(arrs, name):
    n = len(arrs)

    def body(*refs):
        ins, outs = refs[:n], refs[n:2 * n]
        send_sem, recv_sem = refs[2 * n:]
        x, y, c, _ = _place()
        copies = [pltpu.make_async_remote_copy(src_ref=ins[k], dst_ref=outs[k], send_sem=send_sem.at[k],
                                               recv_sem=recv_sem.at[k], device_id=(x, y, 1 - c), device_id_type=MESH)
                  for k in range(n)]
        for q in copies:
            q.start()
        for q in copies:
            q.wait()

    return pl.pallas_call(
        body, in_specs=[_ANY] * n, out_specs=[_ANY] * n,
        out_shape=[jax.ShapeDtypeStruct(a.shape, a.dtype) for a in arrs],
        scratch_shapes=[pltpu.SemaphoreType.DMA((n,)), pltpu.SemaphoreType.DMA((n,))],
        name=name)(*arrs)


_HBM = pl.BlockSpec(memory_space=pltpu.HBM)
_SEM = pl.BlockSpec(memory_space=pltpu.SEMAPHORE)
_SPLIT_EFFECT = pltpu.SideEffectType.DATAFLOW_SIDE_EFFECTING


class _Split(NamedTuple):
    send_sems: jax.Array
    recv_sems: jax.Array
    sources: tuple
    lands: tuple
    token: jax.Array


def _split_peers(kind):
    return N_DEV - 1 if kind == "everyone" else N_CHIPS - 1


def _split_copies(kind, srcs, lands, send_sems, recv_sems):
    x, y, c, chips = _place()
    me = _chip_index(x, y)
    if kind == "everyone":
        peers = [(x ^ ((r >> 2) & 1), y ^ ((r >> 1) & 1), c ^ (r & 1)) for r in range(1, N_DEV)]
    else:
        peers = [(*chip, c) for chip in chips]
    copies = []
    for k in range(len(srcs)):
        for j, peer in enumerate(peers):
            if kind == "gather":
                src, dst = srcs[k], lands[k].at[me]
            elif kind == "scatter":
                src, dst = srcs[k].at[_chip_index(*chips[j])], lands[k].at[j]
            else:
                src, dst = srcs[k], lands[k].at[2 * me + c]
            sem = len(peers) * k + j
            copies.append(pltpu.make_async_remote_copy(
                src_ref=src, dst_ref=dst, send_sem=send_sems.at[sem], recv_sem=recv_sems.at[sem],
                device_id=peer, device_id_type=MESH))
    return copies


def _split_start(name, sources, kind, after):
    n = len(sources)
    if kind == "gather":
        lands = [lax.empty((N_CHIPS,) + s.shape, s.dtype) for s in sources]
    elif kind == "scatter":
        lands = [lax.empty((N_CHIPS - 1,) + s.shape[1:], s.dtype) for s in sources]
    else:
        lands = [lax.empty((N_DEV,) + s.shape, s.dtype) for s in sources]
    n_sems = _split_peers(kind) * n
    deps = [] if after is None else [after]

    def body(*refs):
        srcs, lnds = refs[:n], refs[n:2 * n]
        send_sems, recv_sems = refs[2 * n + len(deps)], refs[2 * n + len(deps) + 1]
        for cp in _split_copies(kind, srcs, lnds, send_sems, recv_sems):
            cp.start()
        refs[-1][...] = jnp.zeros_like(refs[-1])

    hbm = lambda a: pltpu.with_memory_space_constraint(a, pltpu.HBM)
    outs = pl.pallas_call(
        body, name=name,
        in_specs=[_HBM] * (2 * n) + [_ANY] * len(deps),
        out_specs=[_SEM, _SEM] + [_HBM] * (2 * n) + [pl.BlockSpec(memory_space=pltpu.VMEM)],
        out_shape=[pltpu.SemaphoreType.DMA((n_sems,)), pltpu.SemaphoreType.DMA((n_sems,))]
        + [pltpu.HBM(a.shape, a.dtype) for a in list(sources) + lands] + [jax.ShapeDtypeStruct((8, LANES), F32)],
        input_output_aliases={k: 2 + k for k in range(2 * n)},
        compiler_params=pltpu.CompilerParams(has_side_effects=_SPLIT_EFFECT),
    )(*[hbm(s) for s in sources], *[hbm(l) for l in lands], *deps)
    return _Split(outs[0], outs[1], tuple(outs[2:2 + n]), tuple(outs[2 + n:2 + 2 * n]), outs[-1])


def _split_wait(name, h, kind, after):
    n = len(h.sources)

    def body(*refs):
        srcs, lnds = refs[:n], refs[n:2 * n]
        for cp in _split_copies(kind, srcs, lnds, refs[2 * n], refs[2 * n + 1]):
            cp.wait_send()
            cp.wait_recv()

    outs = pl.pallas_call(
        body, name=name,
        in_specs=[_HBM] * (2 * n) + [_SEM, _SEM] + [_ANY] * len(after),
        out_specs=[_HBM] * (2 * n),
        out_shape=[pltpu.HBM(a.shape, a.dtype) for a in h.sources + h.lands],
        input_output_aliases={k: k for k in range(2 * n)},
        compiler_params=pltpu.CompilerParams(has_side_effects=_SPLIT_EFFECT),
    )(*h.sources, *h.lands, h.send_sems, h.recv_sems, *after)
    return outs[:n], outs[n:]


def _sum_devices(parts):
    def body(p_ref, o_ref):
        acc = p_ref[0]
        for d in range(1, N_DEV):
            acc = acc + p_ref[d]
        o_ref[...] = acc

    vm = pl.BlockSpec(memory_space=pltpu.VMEM)
    return pl.pallas_call(body, in_specs=[vm], out_specs=vm, out_shape=jax.ShapeDtypeStruct(parts.shape[1:], F32),
                          name="sum_devices")(parts)


_INPUTS = ["x", "mem", "g_mix", "w_in", "conv_w", "conv_b", "dt_bias", "a_log", "d_skip", "ssm_norm_w", "g_q", "g_k",
           "f_bias", "w_out", "g_xattn", "g_mem", "xq_w", "xkv_w", "xg_q", "xg_k", "xo_w", "g_mlp", "w_up", "w_down"]
_WEIGHTS = _INPUTS[2:]
_BIG = ["w_in", "w_out", "xq_w", "xkv_w", "xo_w", "w_up", "w_down"]
_LATE = _BIG[1:]
_COL_SHARDED = ["w_in", "xkv_w", "w_up"]
_SMALL = [n for n in _WEIGHTS if n not in _BIG]


def _pack_rows(arrs, width):
    starts, r = [], 0
    for a in arrs:
        starts.append(r)
        r += a.shape[0]
    out = jnp.concatenate([jnp.pad(a, ((0, 0), (0, width - a.shape[1]))) for a in arrs], axis=0)
    return jnp.pad(out, ((0, -r % 8), (0, 0))), starts


def _adamw_small(summed, starts, ws, ms, vs, conv_w_index):
    n = len(ws)
    c1 = 1.0 - ADAM_B1 ** ADAM_STEP
    c2 = 1.0 - ADAM_B2 ** ADAM_STEP

    def body(s_ref, *refs):
        w_refs, m_refs, v_refs = refs[:n], refs[n:2 * n], refs[2 * n:3 * n]
        outs = refs[3 * n:]
        chip = _chip_index(lax.axis_index("x"), lax.axis_index("y"))
        for k in range(n):
            rows, cols = w_refs[k].shape
            if k == conv_w_index:
                g = s_ref[starts[k]:starts[k] + rows, pl.ds(pl.multiple_of(chip * cols, LANES), cols)]
            else:
                g = s_ref[starts[k]:starts[k] + rows, 0:cols]
            m_new = ADAM_B1 * m_refs[k][...] + (1.0 - ADAM_B1) * g
            v_new = ADAM_B2 * v_refs[k][...] + (1.0 - ADAM_B2) * (g * g)
            outs[4 * k][...] = g
            outs[4 * k + 1][...] = -ADAM_LR * ((m_new / c1) / (jnp.sqrt(v_new / c2) + ADAM_EPS) + ADAM_WD * w_refs[k][...])
            outs[4 * k + 2][...] = m_new
            outs[4 * k + 3][...] = v_new

    vm = pl.BlockSpec(memory_space=pltpu.VMEM)
    outs = pl.pallas_call(
        body, in_specs=[vm] * (1 + 3 * n), out_specs=[vm] * (4 * n),
        out_shape=[jax.ShapeDtypeStruct(a.shape, F32) for a in ws for _ in range(4)],
        name="adamw_small")(summed, *ws, *ms, *vs)
    return [outs[4 * k:4 * k + 4] for k in range(n)]


def kernel(x, mem, g_mix, w_in, conv_w, conv_b, dt_bias, a_log, d_skip, ssm_norm_w, g_q, g_k, f_bias, w_out, g_xattn, g_mem, xq_w, xkv_w, xg_q, xg_k, xo_w, g_mlp, w_up, w_down, loss_target, m_g_mix, m_w_in, m_conv_w, m_conv_b, m_dt_bias, m_a_log, m_d_skip, m_ssm_norm_w, m_g_q, m_g_k, m_f_bias, m_w_out, m_g_xattn, m_g_mem, m_xq_w, m_xkv_w, m_xg_q, m_xg_k, m_xo_w, m_g_mlp, m_w_up, m_w_down, v_g_mix, v_w_in, v_conv_w, v_conv_b, v_dt_bias, v_a_log, v_d_skip, v_ssm_norm_w, v_g_q, v_g_k, v_f_bias, v_w_out, v_g_xattn, v_g_mem, v_xq_w, v_xkv_w, v_xg_q, v_xg_k, v_xo_w, v_g_mlp, v_w_up, v_w_down):
    args = (x, mem, g_mix, w_in, conv_w, conv_b, dt_bias, a_log, d_skip, ssm_norm_w, g_q, g_k, f_bias, w_out, g_xattn,
            g_mem, xq_w, xkv_w, xg_q, xg_k, xo_w, g_mlp, w_up, w_down)
    w = dict(zip(_INPUTS, args))
    mom1 = dict(zip(_WEIGHTS, (m_g_mix, m_w_in, m_conv_w, m_conv_b, m_dt_bias, m_a_log, m_d_skip, m_ssm_norm_w, m_g_q,
                               m_g_k, m_f_bias, m_w_out, m_g_xattn, m_g_mem, m_xq_w, m_xkv_w, m_xg_q, m_xg_k, m_xo_w,
                               m_g_mlp, m_w_up, m_w_down)))
    mom2 = dict(zip(_WEIGHTS, (v_g_mix, v_w_in, v_conv_w, v_conv_b, v_dt_bias, v_a_log, v_d_skip, v_ssm_norm_w, v_g_q,
                               v_g_k, v_f_bias, v_w_out, v_g_xattn, v_g_mem, v_xq_w, v_xkv_w, v_xg_q, v_xg_k, v_xo_w,
                               v_g_mlp, v_w_up, v_w_down)))
    chip = _chip_index(lax.axis_index("x"), lax.axis_index("y"))

    shard_bf = {n: w[n][0].astype(BF16) for n in _BIG}

    def layout_for_compute(n, g):
        if n == "w_in":
            return _w_in_from_shards(g)
        return g if n in _COL_SHARDED else g.reshape(N_CHIPS * g.shape[1], g.shape[2])

    def layout_for_reduction(n, g):
        if n == "w_in":
            return _w_in_to_shards(g)
        return g if n in _COL_SHARDED else g.reshape(N_CHIPS, g.shape[0] // N_CHIPS, g.shape[1])

    halves_in = shard_bf["w_in"].reshape(2, shard_bf["w_in"].shape[0] // 2, -1)
    g_in, g_conv = _all_gather_chips([halves_in], [w["conv_w"][0]])
    g_in = lax.dynamic_update_index_in_dim(g_in, halves_in, chip, axis=0)
    g_conv = lax.dynamic_update_index_in_dim(g_conv, w["conv_w"][0], chip, axis=0)
    w_in_full = layout_for_compute("w_in", g_in.reshape(N_CHIPS, -1, g_in.shape[-1]))
    p = {n: w[n] for n in _SMALL}
    p["conv_w"] = g_conv.transpose(1, 0, 2).reshape(CONV_WIDTH, CONV_DIM)
    gather = _split_start("gather_late", [shard_bf[n] for n in _LATE], "gather", after=g_in)
    p["g_mix"] = p["g_mix"] + gather.token[:1, :1]

    def late_weights(after):
        srcs, lands = _split_wait("gather_late_wait", gather, "gather", after)
        lands = [lax.dynamic_update_index_in_dim(l, s, chip, axis=0) for l, s in zip(lands, srcs)]
        return {n: layout_for_compute(n, l) for n, l in zip(_LATE, lands)}

    scatter = {}

    def send_late_grads(grads, after):
        scatter["late"] = _split_start("scatter_late", [layout_for_reduction(n, grads[n]) for n in _LATE], "scatter",
                                       after=after)
        return scatter["late"].token

    def send_w_in_grad(g):
        scatter["w_in"] = _split_start("scatter_w_in", [layout_for_reduction("w_in", g)], "scatter", after=None)
        return scatter["w_in"].token

    loss_row, dx, gp = _layer_fwd_bwd(x[0], mem[0], loss_target[0], w_in_full, p, late_weights, send_late_grads,
                                      send_w_in_grad)

    grad, delta, new_m, new_v = {}, {}, {}, {}

    def finish(names, sources, from_chips, tag, token=None):
        own = [lax.dynamic_index_in_dim(s, chip, axis=0, keepdims=False) for s in sources]
        if token is not None:
            own = [o + token[0, 0].astype(o.dtype) for o in own]
        mine = [_chip_sum(o, fc, "rs_chip_sum_" + n) for n, o, fc in zip(names, own, from_chips)]
        for n, a, b in zip(names, mine, _sibling_swap(mine, "rs_sibling_swap_" + tag)):
            shape = w[n].shape
            res = _adamw(w[n][0], a, b, mom1[n][0], mom2[n][0], "adamw_" + n)
            grad[n], delta[n], new_m[n], new_v[n] = (r.reshape(shape) for r in res)

    finish(_LATE, *_split_wait("scatter_late_wait", scatter["late"], "scatter", (dx,)), "late")

    sources_in, from_chips_in = _split_wait("scatter_w_in_wait", scatter["w_in"], "scatter",
                                            tuple(new_v[n] for n in _LATE))

    packed, starts = _pack_rows([gp[n] for n in _SMALL] + [loss_row], CONV_DIM)
    small = _split_start("small_all_gather", [packed], "everyone", after=from_chips_in[0])
    finish(["w_in"], sources_in, from_chips_in, "w_in", small.token)
    (packed,), (from_all,) = _split_wait("small_all_gather_wait", small, "everyone", (new_v["w_in"],))
    device = 2 * chip + lax.axis_index("c")
    summed = _sum_devices(lax.dynamic_update_index_in_dim(from_all, packed, device, axis=0))
    loss = summed[starts[-1], 0]

    as_rows = lambda a: a.reshape(-1, a.shape[-1])
    results = _adamw_small(summed, starts, [as_rows(w[n]) for n in _SMALL], [as_rows(mom1[n]) for n in _SMALL],
                           [as_rows(mom2[n]) for n in _SMALL], _SMALL.index("conv_w"))
    for n, res in zip(_SMALL, results):
        grad[n], delta[n], new_m[n], new_v[n] = (a.reshape(w[n].shape) for a in res)

    return (loss, dx[None], *[grad[n] for n in _WEIGHTS], *[delta[n] for n in _WEIGHTS],
            *[new_m[n] for n in _WEIGHTS], *[new_v[n] for n in _WEIGHTS])
```

```python
import functools
from typing import NamedTuple

import jax
import jax.numpy as jnp
from jax import lax
from jax.experimental import pallas as pl
from jax.experimental.pallas import tpu as pltpu

F32 = jnp.float32
BF16 = jnp.bfloat16
HI = lax.Precision.HIGHEST
MESH = pl.DeviceIdType.MESH

EPS = 1e-5
CHUNK = 128
SSM_HEADS = 16
SSM_GROUPS = 2
HEADS_PER_GROUP = SSM_HEADS // SSM_GROUPS
HEAD_DIM = 64
SSM_STATE = 128
ATTN_HEADS = 16
XATTN_HEADS = 4
XATTN_DIM = 256
CONV_WIDTH = 4
CONV_COLS = 256
N_CHIPS = 4
N_DEV = 8
LANES = 128
VMEM_LIMIT = 56 * 1024 * 1024

ADAM_LR = 0.001
ADAM_B1 = 0.9
ADAM_B2 = 0.999
ADAM_EPS = 1e-08
ADAM_WD = 0.01
ADAM_STEP = 10


def _params(sem):
    return pltpu.CompilerParams(dimension_semantics=sem, vmem_limit_bytes=VMEM_LIMIT)


def _pick(n, cands):
    for c in cands:
        if n % c == 0:
            return c
    return n


def _mm(a, b, mode, name, out_dtypes=(F32,), epilogue=None, extras=(), b_chunks=1, out_chunks=1,
        tm=None, tn=None, tk=None):
    if mode == "nn":
        M, K = a.shape
        N = b.shape[-1] * b_chunks
    elif mode == "nt":
        M, K = a.shape
        N = b.shape[-2]
        assert b.shape[-1] * b_chunks == K
    else:
        K, M = a.shape
        N = b.shape[-1] * b_chunks
    tm = tm or _pick(M, (2048, 1024, 512, 256, 128))
    tn = tn or _pick(N // max(b_chunks if mode != "nt" else 1, out_chunks), (512, 640, 384, 256, 128))
    if tk is None:
        kmax = b.shape[-1] if mode == "nt" else K
        tk = kmax if kmax <= 2048 else _pick(kmax, (2048, 1920, 1152, 1024, 512))
    nk = K // tk
    assert M % tm == 0 and N % tn == 0 and K % tk == 0
    grid = (M // tm, N // tn, nk)

    if mode == "tn":
        a_spec = pl.BlockSpec((tk, tm), lambda i, j, k: (k, i))
    else:
        a_spec = pl.BlockSpec((tm, tk), lambda i, j, k: (i, k))

    def b_index(t_row, t_last, tile_last):
        if b_chunks == 1:
            return (t_row, t_last)
        q = (b.shape[-1]) // tile_last
        return (t_last // q, t_row, t_last % q)

    if mode == "nn" or mode == "tn":
        bshape = (tk, tn)
        bmap = lambda i, j, k: b_index(k, j, tn)
    else:
        bshape = (tn, tk)
        bmap = lambda i, j, k: b_index(j, k, tk)
    if b_chunks > 1:
        bshape = (None,) + bshape
    b_spec = pl.BlockSpec(bshape, bmap)

    if out_chunks == 1:
        o_spec = pl.BlockSpec((tm, tn), lambda i, j, k: (i, j))
        o_shape = (M, N)
    else:
        qo = (N // out_chunks) // tn
        o_spec = pl.BlockSpec((None, tm, tn), lambda i, j, k: (j // qo, i, j % qo))
        o_shape = (out_chunks, M, N // out_chunks)
    e_spec = pl.BlockSpec((tm, tn), lambda i, j, k: (i, j))

    dims = {"nn": (((1,), (0,)), ((), ())), "nt": (((1,), (1,)), ((), ())), "tn": (((0,), (0,)), ((), ()))}[mode]
    n_ex = len(extras)
    n_out = len(out_dtypes)

    def body(*refs):
        a_ref, b_ref = refs[0], refs[1]
        ex_refs = refs[2:2 + n_ex]
        o_refs = refs[2 + n_ex:2 + n_ex + n_out]

        def finish(acc):
            outs = epilogue(acc, *[r[...] for r in ex_refs]) if epilogue is not None else (acc,)
            for r, o in zip(o_refs, outs):
                r[...] = o.astype(r.dtype)

        part = lax.dot_general(a_ref[...].astype(BF16), b_ref[...].astype(BF16), dims,
                               preferred_element_type=F32)
        if nk == 1:
            finish(part)
        else:
            acc_ref = refs[-1]
            k = pl.program_id(2)

            @pl.when(k == 0)
            def _():
                acc_ref[...] = part

            @pl.when(k > 0)
            def _():
                acc_ref[...] += part

            @pl.when(k == nk - 1)
            def _():
                finish(acc_ref[...])

    outs = pl.pallas_call(
        body,
        grid=grid,
        in_specs=[a_spec, b_spec] + [e_spec] * n_ex,
        out_specs=[o_spec] * n_out,
        out_shape=[jax.ShapeDtypeStruct(o_shape, d) for d in out_dtypes],
        scratch_shapes=[pltpu.VMEM((tm, tn), F32)] if nk > 1 else [],
        compiler_params=_params(("parallel", "parallel", "arbitrary")),
        name=name,
    )(a, b, *extras)
    return outs[0] if n_out == 1 else outs


def _rms(x, g):
    r = lax.rsqrt(jnp.mean(x * x, axis=-1, keepdims=True) + EPS)
    return x * r * g


def _rmsnorm_fwd(x, g, name):
    R, D = x.shape
    tr = _pick(R, (512, 256))

    def body(x_ref, g_ref, o_ref):
        o_ref[...] = _rms(x_ref[...], g_ref[...]).astype(o_ref.dtype)

    return pl.pallas_call(
        body, grid=(R // tr,),
        in_specs=[pl.BlockSpec((tr, D), lambda i: (i, 0)), pl.BlockSpec((1, D), lambda i: (0, 0))],
        out_specs=pl.BlockSpec((tr, D), lambda i: (i, 0)),
        out_shape=jax.ShapeDtypeStruct((R, D), BF16),
        compiler_params=_params(("parallel",)), name=name)(x, g)


def _rmsnorm_bwd(x, g, dh, dres, name):
    R, D = x.shape
    tr = _pick(R, (256,))
    has_res = dres is not None

    def body(*refs):
        if has_res:
            x_ref, g_ref, dh_ref, dres_ref, dx_ref, dg_ref = refs
        else:
            x_ref, g_ref, dh_ref, dx_ref, dg_ref = refs
        _, vjp = jax.vjp(_rms, x_ref[...], g_ref[...])
        dx, dg = vjp(dh_ref[...])
        if has_res:
            dx = dx + dres_ref[...]
        dx_ref[...] = dx

        @pl.when(pl.program_id(0) == 0)
        def _():
            dg_ref[...] = jnp.zeros_like(dg_ref)

        dg_ref[...] += dg

    row = pl.BlockSpec((tr, D), lambda i: (i, 0))
    vec = pl.BlockSpec((1, D), lambda i: (0, 0))
    ins = [x, g, dh] + ([dres] if has_res else [])
    return pl.pallas_call(
        body, grid=(R // tr,),
        in_specs=[row, vec, row] + ([row] if has_res else []),
        out_specs=[row, vec],
        out_shape=[jax.ShapeDtypeStruct((R, D), F32), jax.ShapeDtypeStruct((1, D), F32)],
        compiler_params=_params(("arbitrary",)), name=name)(*ins)


def _shift_down(u, k):
    if k == 0:
        return u
    rows = lax.broadcasted_iota(jnp.int32, u.shape, 0)
    return jnp.where(rows >= k, pltpu.roll(u, k, axis=0), 0.0)


def _shift_up(u, k):
    if k == 0:
        return u
    n = u.shape[0]
    rows = lax.broadcasted_iota(jnp.int32, u.shape, 0)
    return jnp.where(rows < n - k, pltpu.roll(u, n - k, axis=0), 0.0)


def _conv_pre(u, w, b):
    pre = b
    for j in range(CONV_WIDTH):
        pre = pre + w[j:j + 1, :] * _shift_down(u, CONV_WIDTH - 1 - j)
    return pre


def _conv_fwd(proj, col0, ncols, conv_w, conv_b):
    S = proj.shape[0]
    cb0 = col0 // CONV_COLS

    def body(u_ref, w_ref, b_ref, o_ref):
        pre = _conv_pre(u_ref[...], w_ref[...], b_ref[...])
        o_ref[...] = pre * jax.nn.sigmoid(pre)

    return pl.pallas_call(
        body, grid=(ncols // CONV_COLS,),
        in_specs=[pl.BlockSpec((S, CONV_COLS), lambda j: (0, j + cb0)),
                  pl.BlockSpec((CONV_WIDTH, CONV_COLS), lambda j: (0, j)),
                  pl.BlockSpec((1, CONV_COLS), lambda j: (0, j))],
        out_specs=pl.BlockSpec((S, CONV_COLS), lambda j: (0, j)),
        out_shape=jax.ShapeDtypeStruct((S, ncols), F32),
        compiler_params=_params(("parallel",)), name="conv_fwd")(proj, conv_w, conv_b)


def _conv_bwd(proj, col0, ncols, conv_w, conv_b, douts, dproj):
    S = proj.shape[0]
    cb0 = col0 // CONV_COLS
    starts = [0]
    for d in douts:
        starts.append(starts[-1] + d.shape[1] // CONV_COLS)
    assert starts[-1] == ncols // CONV_COLS
    nd = len(douts)

    def body(u_ref, w_ref, b_ref, *rest):
        d_refs, (du_ref, dw_ref, db_ref) = rest[:nd], rest[nd + 1:]
        j = pl.program_id(0)
        dout = d_refs[-1][...]
        for i in range(nd - 2, -1, -1):
            dout = jnp.where(j < starts[i + 1], d_refs[i][...], dout)
        u = u_ref[...]
        w = w_ref[...]
        pre = _conv_pre(u, w, b_ref[...])
        s = jax.nn.sigmoid(pre)
        dpre = dout * (s * (1.0 + pre * (1.0 - s)))
        du = jnp.zeros_like(u)
        rows = []
        for j in range(CONV_WIDTH):
            k = CONV_WIDTH - 1 - j
            du = du + w[j:j + 1, :] * _shift_up(dpre, k)
            rows.append(jnp.sum(dpre * _shift_down(u, k), axis=0, keepdims=True))
        du_ref[...] = du.astype(du_ref.dtype)
        rows.append(jnp.zeros((8 - CONV_WIDTH, CONV_COLS), F32))
        dw_ref[...] = jnp.concatenate(rows, axis=0)
        db_ref[...] = jnp.sum(dpre, axis=0, keepdims=True)

    return pl.pallas_call(
        body, grid=(ncols // CONV_COLS,),
        in_specs=[pl.BlockSpec((S, CONV_COLS), lambda j: (0, j + cb0)),
                  pl.BlockSpec((CONV_WIDTH, CONV_COLS), lambda j: (0, j)),
                  pl.BlockSpec((1, CONV_COLS), lambda j: (0, j))]
        + [pl.BlockSpec((S, CONV_COLS), lambda j, lo=starts[i], hi=starts[i + 1]: (0, jnp.clip(j - lo, 0, hi - lo - 1)))
           for i in range(nd)] + [_ANY],
        out_specs=[pl.BlockSpec((S, CONV_COLS), lambda j: (0, j + cb0)),
                   pl.BlockSpec((8, CONV_COLS), lambda j: (0, j)),
                   pl.BlockSpec((1, CONV_COLS), lambda j: (0, j))],
        out_shape=[jax.ShapeDtypeStruct(dproj.shape, dproj.dtype),
                   jax.ShapeDtypeStruct((8, ncols), F32),
                   jax.ShapeDtypeStruct((1, ncols), F32)],
        input_output_aliases={3 + nd: 0},
        compiler_params=_params(("parallel",)), name="conv_bwd")(proj, conv_w, conv_b, *douts, dproj)


def _softplus(x):
    return jnp.maximum(x, 0.0) + jnp.log1p(jnp.exp(-jnp.abs(x)))


def _dot32(a, b, dims=(((1,), (0,)), ((), ()))):
    return lax.dot_general(a, b, dims, precision=HI, preferred_element_type=F32)


_NN = (((1,), (0,)), ((), ()))
_NT = (((1,), (1,)), ((), ()))
_TN = (((0,), (0,)), ((), ()))


def _bf16_dot(a, b, dims):
    return lax.dot_general(a.astype(BF16), b.astype(BF16), dims, preferred_element_type=F32)


@functools.partial(jax.custom_vjp, nondiff_argnums=(2,))
def _dotd(a, b, dims=_NN):
    return _bf16_dot(a, b, dims)


def _dotd_fwd(a, b, dims):
    return _bf16_dot(a, b, dims), (a, b)


def _dotd_bwd(dims, res, ct):
    a, b = res
    if dims == _NN:
        return _bf16_dot(ct, b, _NT), _bf16_dot(a, ct, _TN)
    if dims == _NT:
        return _bf16_dot(ct, b, _NN), _bf16_dot(ct, a, _TN)
    return _bf16_dot(b, ct, _NT), _bf16_dot(a, ct, _NN)


_dotd.defvjp(_dotd_fwd, _dotd_bwd)


PAIRS_PER_GROUP = HEADS_PER_GROUP // 2


def _ssd_chunk(xs, Bm, Cm, z, dtr, dtb, alog, dsk, nw, h):
    L = Bm.shape[0]
    ri = lax.broadcasted_iota(jnp.int32, (L, L), 0)
    ci = lax.broadcasted_iota(jnp.int32, (L, L), 1)
    causal = ri >= ci
    tril = causal.astype(F32)
    first = _first_head(L)
    first1 = _first_head(1)
    CB = _dotd(Cm, Bm, _NT)
    gated, hnew = [], []
    ssq = jnp.zeros((L, 1), F32)
    for pp in range(len(xs)):
        dts, cums, tots, decay = [], [], [], []
        for a in range(2):
            r = 2 * pp + a
            dt = _softplus(dtr[r] + dtb[r])
            dA = dt * (-jnp.exp(alog[r]))
            acs = _dot32(tril, dA)
            cc = jnp.broadcast_to(acs, (L, L))
            decay.append(CB * jnp.exp(jnp.where(causal, cc - cc.T, -1e30)))
            dts.append(dt)
            cums.append(acs)
            tots.append(jnp.sum(dA, axis=0, keepdims=True))
        dt2 = jnp.where(first, dts[0], dts[1])
        acs2 = jnp.where(first, cums[0], cums[1])
        tot2 = jnp.where(first1, tots[0], tots[1])
        dsk2 = jnp.where(first1, dsk[2 * pp], dsk[2 * pp + 1])
        X = xs[pp] * dt2
        y = (jnp.where(first, _dotd(decay[0], X), _dotd(decay[1], X)) + jnp.exp(acs2) * _dotd(Cm, h[pp])
             + dsk2 * xs[pp])
        hnew.append(jnp.exp(tot2) * h[pp] + _dotd(Bm, X * jnp.exp(tot2 - acs2), _TN))
        g = y * (z[pp] * jax.nn.sigmoid(z[pp]))
        ssq = ssq + jnp.sum(g * g, axis=-1, keepdims=True)
        gated.append(g)
    rs = lax.rsqrt(ssq / (len(xs) * LANES) + EPS)
    return [g * rs * nw[pp] for pp, g in enumerate(gated)], hnew


def _ssd_args(xs_ref, b_ref, c_ref, z_ref, dt_ref, dtb_ref, al_ref, dsk_ref, nw_ref, h_ref):
    pairs = range(PAIRS_PER_GROUP)
    heads = range(HEADS_PER_GROUP)
    lanes = lambda ref, pp: ref[:, pp * LANES:(pp + 1) * LANES]
    return ([lanes(xs_ref, pp) for pp in pairs], b_ref[...], c_ref[...], [lanes(z_ref, pp) for pp in pairs],
            [dt_ref[r] for r in heads], [dtb_ref[r] for r in heads], [al_ref[r] for r in heads],
            [dsk_ref[r] for r in heads], [lanes(nw_ref, pp) for pp in pairs], [h_ref[pp] for pp in pairs])


def _ssd_specs(rev):
    H, N, L = HEADS_PER_GROUP, SSM_STATE, CHUNK
    gw = H * HEAD_DIM
    return dict(
        cols=lambda col0: pl.BlockSpec((L, gw), lambda g, c: (rev(c), col0 // gw + g)),
        bc=lambda first_block: pl.BlockSpec((L, N), lambda g, c: (rev(c), first_block + g)),
        dt=pl.BlockSpec((H, L, 1), lambda g, c: (g, rev(c), 0)),
        scal=pl.BlockSpec((H, 1, 1), lambda g, c: (g, 0, 0)),
        nw=pl.BlockSpec((1, gw), lambda g, c: (0, g)),
        hs=pl.BlockSpec((None, PAIRS_PER_GROUP, N, LANES), lambda g, c: (rev(c), g, 0, 0)),
        b_block=SSM_INNER // N,
    )


def _ssd_fwd(xbc, proj, dt_hm, dtb, alog, dsk, nw):
    S = xbc.shape[0]
    N, L = SSM_STATE, CHUNK
    nc = S // L
    sp = _ssd_specs(lambda c: c)

    def body(xs_ref, b_ref, c_ref, z_ref, dt_ref, dtb_ref, al_ref, dsk_ref, nw_ref, y_ref, hs_ref, h_ref):
        @pl.when(pl.program_id(1) == 0)
        def _():
            h_ref[...] = jnp.zeros_like(h_ref)

        hs_ref[...] = h_ref[...]
        out, hnew = _ssd_chunk(*_ssd_args(xs_ref, b_ref, c_ref, z_ref, dt_ref, dtb_ref, al_ref, dsk_ref, nw_ref, h_ref))
        for pp in range(PAIRS_PER_GROUP):
            y_ref[:, pp * LANES:(pp + 1) * LANES] = out[pp].astype(y_ref.dtype)
            h_ref[pp] = hnew[pp]

    return pl.pallas_call(
        body, grid=(SSM_GROUPS, nc),
        in_specs=[sp["cols"](0), sp["bc"](sp["b_block"]), sp["bc"](sp["b_block"] + SSM_GROUPS), sp["cols"](COL_Z),
                  sp["dt"], sp["scal"], sp["scal"], sp["scal"], sp["nw"]],
        out_specs=[sp["cols"](0), sp["hs"]],
        out_shape=[jax.ShapeDtypeStruct((S, MIX_WIDTH), BF16),
                   jax.ShapeDtypeStruct((nc, SSM_HEADS // 2, N, LANES), F32)],
        scratch_shapes=[pltpu.VMEM((PAIRS_PER_GROUP, N, LANES), F32)],
        compiler_params=_params(("parallel", "arbitrary")), name="ssd_fwd",
    )(xbc, xbc, xbc, proj, dt_hm, dtb, alog, dsk, nw)


def _ssd_bwd(xbc, proj, dt_hm, dtb, alog, dsk, nw, hs, dmixed, dproj):
    S = xbc.shape[0]
    N, L = SSM_STATE, CHUNK
    nc = S // L
    sp = _ssd_specs(lambda c: nc - 1 - c)

    def body(xs_ref, b_ref, c_ref, z_ref, dt_ref, dtb_ref, al_ref, dsk_ref, nw_ref, hs_ref, dy_ref, buf_ref,
             dxs_ref, dz_ref, db_ref, dc_ref, ddt_ref, ddtb_ref, dal_ref, ddsk_ref, dnw_ref, dh_ref):
        @pl.when(pl.program_id(1) == 0)
        def _():
            dh_ref[...] = jnp.zeros_like(dh_ref)
            ddtb_ref[...] = jnp.zeros_like(ddtb_ref)
            dal_ref[...] = jnp.zeros_like(dal_ref)
            ddsk_ref[...] = jnp.zeros_like(ddsk_ref)
            dnw_ref[...] = jnp.zeros_like(dnw_ref)

        pairs = range(PAIRS_PER_GROUP)
        lanes = lambda pp: slice(pp * LANES, (pp + 1) * LANES)
        _, vjp = jax.vjp(_ssd_chunk, *_ssd_args(xs_ref, b_ref, c_ref, z_ref, dt_ref, dtb_ref, al_ref, dsk_ref, nw_ref,
                                                hs_ref))
        dxs, dB, dC, dz, ddt, ddtb, dal, ddsk, dnw, dh = vjp(([dy_ref[:, lanes(pp)] for pp in pairs],
                                                              [dh_ref[pp] for pp in pairs]))
        db_ref[...] = dB
        dc_ref[...] = dC
        for pp in pairs:
            dxs_ref[:, lanes(pp)] = dxs[pp]
            dz_ref[:, lanes(pp)] = dz[pp].astype(dz_ref.dtype)
            dnw_ref[:, lanes(pp)] += dnw[pp]
            dh_ref[pp] = dh[pp]
        for r in range(HEADS_PER_GROUP):
            ddt_ref[r] = ddt[r]
            ddtb_ref[r] += ddtb[r]
            dal_ref[r] += dal[r]
            ddsk_ref[r] += ddsk[r]

    bc_out = pl.BlockSpec((L, N), lambda g, c: (nc - 1 - c, g))
    return pl.pallas_call(
        body, grid=(SSM_GROUPS, nc),
        in_specs=[sp["cols"](0), sp["bc"](sp["b_block"]), sp["bc"](sp["b_block"] + SSM_GROUPS), sp["cols"](COL_Z),
                  sp["dt"], sp["scal"], sp["scal"], sp["scal"], sp["nw"], sp["hs"], sp["cols"](0), _ANY],
        out_specs=[sp["cols"](0), sp["cols"](COL_Z), bc_out, bc_out, sp["dt"], sp["scal"], sp["scal"], sp["scal"],
                   sp["nw"]],
        input_output_aliases={11: 1},
        out_shape=[jax.ShapeDtypeStruct((S, SSM_INNER), F32), jax.ShapeDtypeStruct(dproj.shape, dproj.dtype),
                   jax.ShapeDtypeStruct((S, SSM_GROUPS * N), F32), jax.ShapeDtypeStruct((S, SSM_GROUPS * N), F32),
                   jax.ShapeDtypeStruct((SSM_HEADS, S, 1), F32),
                   jax.ShapeDtypeStruct((SSM_HEADS, 1, 1), F32), jax.ShapeDtypeStruct((SSM_HEADS, 1, 1), F32),
                   jax.ShapeDtypeStruct((SSM_HEADS, 1, 1), F32), jax.ShapeDtypeStruct((1, SSM_INNER), F32)],
        scratch_shapes=[pltpu.VMEM((PAIRS_PER_GROUP, N, LANES), F32)],
        compiler_params=_params(("parallel", "arbitrary")), name="ssd_bwd",
    )(xbc, xbc, xbc, proj, dt_hm, dtb, alog, dsk, nw, hs, dmixed, dproj)


ATTN_SCALE = HEAD_DIM ** -0.5
PREP_COLS = 512


def _first_head(rows):
    return lax.broadcasted_iota(jnp.int32, (rows, LANES), 1) < HEAD_DIM


def _pair_norm(x, g2, scale):
    first = _first_head(x.shape[0])
    sq = x * x
    ms0 = jnp.sum(jnp.where(first, sq, 0.0), axis=-1, keepdims=True) * (1.0 / HEAD_DIM)
    ms1 = jnp.sum(jnp.where(first, 0.0, sq), axis=-1, keepdims=True) * (1.0 / HEAD_DIM)
    r = jnp.where(first, lax.rsqrt(ms0 + EPS), lax.rsqrt(ms1 + EPS))
    return x * r * g2 * scale


def _qk_prep_fwd(proj, gq2, gk2):
    S = proj.shape[0]
    tq = _pick(S, (512, 256))

    def body(q_ref, k_ref, v_ref, gq_ref, gk_ref, qo_ref, ko_ref, vo_ref):
        for b in range(PREP_COLS // LANES):
            pair = slice(b * LANES, (b + 1) * LANES)
            qo_ref[:, pair] = _pair_norm(q_ref[:, pair], gq_ref[...], ATTN_SCALE).astype(BF16)
            ko_ref[:, pair] = _pair_norm(k_ref[:, pair], gk_ref[...], 1.0).astype(BF16)
        vo_ref[...] = v_ref[...].astype(BF16)

    col = lambda c0: pl.BlockSpec((tq, PREP_COLS), lambda h, i: (i, c0 // PREP_COLS + h))
    blk = pl.BlockSpec((tq, PREP_COLS), lambda h, i: (i, h))
    vec = pl.BlockSpec((1, LANES), lambda h, i: (0, 0))
    return pl.pallas_call(
        body, grid=(ATTN_WIDTH // PREP_COLS, S // tq), in_specs=[col(COL_Q), col(COL_K), col(COL_V), vec, vec],
        out_specs=[blk, blk, blk], out_shape=[jax.ShapeDtypeStruct((S, ATTN_WIDTH), BF16)] * 3,
        compiler_params=_params(("parallel", "parallel")), name="qk_prep_fwd")(proj, proj, proj, gq2, gk2)


def _pair_norm_bwd(proj, col0, g2, scale, dn, dproj, name):
    S = proj.shape[0]
    tq = _pick(S, (512, 256))

    def body(u_ref, g_ref, dn_ref, buf_ref, du_ref, dg_ref):
        @pl.when((pl.program_id(0) == 0) & (pl.program_id(1) == 0))
        def _():
            dg_ref[...] = jnp.zeros_like(dg_ref)

        for b in range(PREP_COLS // LANES):
            pair = slice(b * LANES, (b + 1) * LANES)
            _, vjp = jax.vjp(lambda u, g: _pair_norm(u, g, scale), u_ref[:, pair], g_ref[...])
            du, dg = vjp(dn_ref[:, pair])
            du_ref[:, pair] = du.astype(du_ref.dtype)
            dg_ref[...] += dg

    ublk = pl.BlockSpec((tq, PREP_COLS), lambda h, i: (i, col0 // PREP_COLS + h))
    blk = pl.BlockSpec((tq, PREP_COLS), lambda h, i: (i, h))
    vec = pl.BlockSpec((1, LANES), lambda h, i: (0, 0))
    return pl.pallas_call(
        body, grid=(ATTN_WIDTH // PREP_COLS, S // tq), in_specs=[ublk, vec, blk, _ANY], out_specs=[ublk, vec],
        out_shape=[jax.ShapeDtypeStruct(dproj.shape, dproj.dtype), jax.ShapeDtypeStruct((1, LANES), F32)],
        input_output_aliases={3: 0},
        compiler_params=_params(("arbitrary", "arbitrary")), name=name)(proj, g2, dn, dproj)


def _logf_cumsum_fwd(f_raw, f_bias):
    S, Hh = f_raw.shape
    L = CHUNK

    def body(f_ref, b_ref, o_ref, wide_ref):
        ri = lax.broadcasted_iota(jnp.int32, (L, L), 0)
        ci = lax.broadcasted_iota(jnp.int32, (L, L), 1)
        tril = (ri >= ci).astype(F32)
        carry = jnp.zeros((1, Hh), F32)
        for c in range(S // L):
            rows = slice(c * L, (c + 1) * L)
            lf = -_softplus(-(f_ref[rows, :] + b_ref[...]))
            cum = _dot32(tril, lf) + carry
            o_ref[rows, :] = cum
            for h in range(Hh):
                wide_ref[rows, h * HEAD_DIM:(h + 1) * HEAD_DIM] = jnp.broadcast_to(cum[:, h:h + 1], (L, HEAD_DIM))
            carry = cum[L - 1:L, :]

    return pl.pallas_call(
        body, out_shape=[jax.ShapeDtypeStruct((S, Hh), F32), jax.ShapeDtypeStruct((S, Hh * HEAD_DIM), F32)],
        name="logf_cumsum_fwd")(f_raw, f_bias)


def _logf_cumsum_bwd(f_raw, f_bias, dcum):
    S, Hh = f_raw.shape
    L = CHUNK

    def body(f_ref, b_ref, d_ref, df_ref, db_ref):
        ri = lax.broadcasted_iota(jnp.int32, (L, L), 0)
        ci = lax.broadcasted_iota(jnp.int32, (L, L), 1)
        triu = (ri <= ci).astype(F32)
        carry = jnp.zeros((1, Hh), F32)
        db = jnp.zeros((1, Hh), F32)
        for c in reversed(range(S // L)):
            suf = _dot32(triu, d_ref[c * L:(c + 1) * L, :]) + carry
            df = suf * jax.nn.sigmoid(-(f_ref[c * L:(c + 1) * L, :] + b_ref[...]))
            df_ref[c * L:(c + 1) * L, :] = df
            db = db + jnp.sum(df, axis=0, keepdims=True)
            carry = suf[0:1, :]
        db_ref[...] = db

    return pl.pallas_call(
        body, out_shape=[jax.ShapeDtypeStruct((S, Hh), F32), jax.ShapeDtypeStruct((1, Hh), F32)],
        name="logf_cumsum_bwd")(f_raw, f_bias, dcum)


def _mxu(a, b, dims=(((1,), (0,)), ((), ()))):
    return lax.dot_general(a, b, dims, preferred_element_type=F32)


def _flash_fwd(qs, kn, vb, cq, ck, mixed):
    S, W = qs.shape
    tq = tk = _pick(S, (512, 256))
    nmask = max(tq // tk, 1)

    def body(q_ref, k_ref, v_ref, cq_ref, ck_ref, buf_ref, o_ref, of_ref, lse_ref):
        i = pl.program_id(1)
        first = _first_head(tq)
        q2 = q_ref[...]
        zero = jnp.zeros_like(q2)
        qa = (jnp.where(first, q2, zero), jnp.where(first, zero, q2))
        cqa = (cq_ref[:, 0:1], cq_ref[:, HEAD_DIM:HEAD_DIM + 1])
        row0 = i * tq

        def step(j, carry, masked):
            ms, ls, acc, rem = carry
            off = pl.multiple_of(j * tk, tk)
            k = k_ref[pl.ds(off, tk), :]
            v = v_ref[pl.ds(off, tk), :]
            new_m, new_l, alphas, pvs, prs = [], [], [], [], []
            for a in range(2):
                s = _mxu(qa[a], k, _NT) + cqa[a] - ck_ref[a, :, pl.ds(off, tk)]
                if masked:
                    ri = lax.broadcasted_iota(jnp.int32, (tq, tk), 0) + row0
                    ci = lax.broadcasted_iota(jnp.int32, (tq, tk), 1) + off
                    s = jnp.where(ri >= ci, s, -1e30)
                m_new = jnp.maximum(ms[a], jnp.max(s, axis=-1, keepdims=True))
                alpha = jnp.exp(ms[a] - m_new)
                p = jnp.exp(s - m_new)
                new_l.append(alpha * ls[a] + jnp.sum(p, axis=-1, keepdims=True))
                new_m.append(m_new)
                alphas.append(alpha)
                p_hi = p.astype(BF16)
                pvs.append(_mxu(p_hi, v))
                prs.append(_mxu((p - p_hi.astype(F32)).astype(BF16), v))
            al = jnp.where(first, alphas[0], alphas[1])
            acc = al * acc + jnp.where(first, pvs[0], pvs[1])
            rem = al * rem + jnp.where(first, prs[0], prs[1])
            return tuple(new_m), tuple(new_l), acc, rem

        neg = jnp.full((tq, 1), -1e30, F32)
        z1 = jnp.zeros((tq, 1), F32)
        z2 = jnp.zeros((tq, LANES), F32)
        carry = ((neg, neg), (z1, z1), z2, z2)
        n_full = (i * tq) // tk
        carry = lax.fori_loop(0, n_full, lambda j, c: step(j, c, False), carry)
        for jj in range(nmask):
            carry = step(n_full + jj, carry, True)
        ms, ls, acc, rem = carry
        linv = jnp.where(first, 1.0 / ls[0], 1.0 / ls[1])
        o_ref[...] = (acc * linv).astype(o_ref.dtype)
        of_ref[...] = (acc + rem) * linv
        lse_ref[...] = jnp.where(first, ms[0] + jnp.log(ls[0]), ms[1] + jnp.log(ls[1]))

    qblk = pl.BlockSpec((tq, LANES), lambda h, i: (i, h))
    full = pl.BlockSpec((S, LANES), lambda h, i: (0, h))
    return pl.pallas_call(
        body, grid=(W // LANES, S // tq),
        in_specs=[qblk, full, full, qblk, pl.BlockSpec((2, 1, S), lambda h, i: (h, 0, 0)), _ANY],
        out_specs=[pl.BlockSpec((tq, LANES), lambda h, i: (i, SSM_INNER // LANES + h)), qblk, qblk],
        out_shape=[jax.ShapeDtypeStruct(mixed.shape, mixed.dtype), jax.ShapeDtypeStruct((S, W), F32),
                   jax.ShapeDtypeStruct((S, W), F32)],
        input_output_aliases={5: 0},
        compiler_params=_params(("parallel", "parallel")), name="flash_fwd")(qs, kn, vb, cq, ck, mixed)


def _flash_bwd(qs, kn, vb, cq, ck, o_fine, do, do_col0, lse):
    S, W = qs.shape
    tq = tk = _pick(S, (512, 256))
    nq = S // tq
    nmask = max(tk // tq, 1)

    def body(q_ref, k_ref, v_ref, cq_ref, ck_ref, of_ref, do_ref, lse_ref, dq_ref, dk_ref, dv_ref, dck_ref):
        j = pl.program_id(1)

        @pl.when(j == 0)
        def _():
            dq_ref[...] = jnp.zeros_like(dq_ref)

        firstk = _first_head(tk)
        firstq = _first_head(tq)
        k2 = k_ref[...]
        v2 = v_ref[...]
        zk = jnp.zeros_like(k2)
        ka = (jnp.where(firstk, k2, zk), jnp.where(firstk, zk, k2))
        va = (jnp.where(firstk, v2, zk), jnp.where(firstk, zk, v2))
        cka = (ck_ref[0], ck_ref[1])
        col0 = j * tk

        def step(i, carry, masked):
            dk, dv, dck0, dck1 = carry
            dcks = [dck0, dck1]
            off = pl.multiple_of(i * tq, tq)
            rows = pl.ds(off, tq)
            q2 = q_ref[rows, :]
            dob = do_ref[rows, :].astype(BF16)
            prod = dob.astype(F32) * of_ref[rows, :]
            dkp, dvp, dqp = [], [], []
            for a in range(2):
                lane = pl.ds(a * HEAD_DIM, 1)
                s = _mxu(q2, ka[a], _NT) + cq_ref[rows, lane] - cka[a]
                if masked:
                    ri = lax.broadcasted_iota(jnp.int32, (tq, tk), 0) + off
                    ci = lax.broadcasted_iota(jnp.int32, (tq, tk), 1) + col0
                    s = jnp.where(ri >= ci, s, -1e30)
                p = jnp.exp(s - lse_ref[rows, lane])
                dp = _mxu(dob, va[a], _NT)
                own = jnp.where(firstq, prod, 0.0) if a == 0 else jnp.where(firstq, 0.0, prod)
                ds = p * (dp - jnp.sum(own, axis=-1, keepdims=True))
                dsb = ds.astype(BF16)
                dvp.append(_mxu(p.astype(BF16), dob, _TN))
                dkp.append(_mxu(dsb, q2, _TN))
                dqp.append(_mxu(dsb, k2))
                dcks[a] = dcks[a] - jnp.sum(ds, axis=0, keepdims=True)
            dq_ref[rows, :] += jnp.where(firstq, dqp[0], dqp[1])
            dk = dk + jnp.where(firstk, dkp[0], dkp[1])
            dv = dv + jnp.where(firstk, dvp[0], dvp[1])
            return dk, dv, dcks[0], dcks[1]

        z2 = jnp.zeros((tk, LANES), F32)
        z1 = jnp.zeros((1, tk), F32)
        carry = (z2, z2, z1, z1)
        i0 = (j * tk) // tq
        for ii in range(nmask):
            carry = step(i0 + ii, carry, True)
        dk, dv, dck0, dck1 = lax.fori_loop(i0 + nmask, nq, lambda i, c: step(i, c, False), carry)
        dk_ref[...] = dk
        dv_ref[...] = dv.astype(dv_ref.dtype)
        dck_ref[0] = dck0
        dck_ref[1] = dck1

    kblk = pl.BlockSpec((tk, LANES), lambda h, j: (j, h))
    full = pl.BlockSpec((S, LANES), lambda h, j: (0, h))
    dofull = pl.BlockSpec((S, LANES), lambda h, j: (0, do_col0 // LANES + h))
    rowt = pl.BlockSpec((2, 1, tk), lambda h, j: (h, 0, j))
    dvblk = pl.BlockSpec((tk, LANES), lambda h, j: (j, COL_V // LANES + h))
    return pl.pallas_call(
        body, grid=(W // LANES, S // tk),
        in_specs=[full, kblk, kblk, full, rowt, full, dofull, full],
        out_specs=[full, kblk, dvblk, rowt],
        out_shape=[jax.ShapeDtypeStruct((S, W), F32), jax.ShapeDtypeStruct((S, W), F32),
                   jax.ShapeDtypeStruct((S, IN_COLS_PAD), BF16), jax.ShapeDtypeStruct((2 * (W // LANES), 1, S), F32)],
        compiler_params=_params(("parallel", "arbitrary")), name="flash_bwd")(qs, kn, vb, cq, ck, o_fine, do, lse)


XATTN_SCALE = XATTN_DIM ** -0.5


def _xq_norm(q, g):
    return _rms(q, g) * XATTN_SCALE


def _xattn_fwd(xq, kv, gq, gk):
    S = xq.shape[0]
    Mm = kv.shape[0]
    Dh = XATTN_DIM
    tq = _pick(S, (512, 256))

    def body(q_ref, k_ref, v_ref, gq_ref, gk_ref, o_ref):
        qn = _xq_norm(q_ref[...], gq_ref[...]).astype(BF16)
        kn = _rms(k_ref[...], gk_ref[...]).astype(BF16)
        s = _mxu(qn, kn, _NT)
        m = jnp.max(s, axis=-1, keepdims=True)
        p = jnp.exp(s - m)
        l = jnp.sum(p, axis=-1, keepdims=True)
        o_ref[...] = (_mxu(p.astype(BF16), v_ref[...].astype(BF16)) / l).astype(o_ref.dtype)

    vec = pl.BlockSpec((1, Dh), lambda h, i: (0, 0))
    return pl.pallas_call(
        body, grid=(XATTN_HEADS, S // tq),
        in_specs=[pl.BlockSpec((tq, Dh), lambda h, i: (i, h)), pl.BlockSpec((Mm, Dh), lambda h, i: (0, h)),
                  pl.BlockSpec((Mm, Dh), lambda h, i: (0, XATTN_HEADS + h)), vec, vec],
        out_specs=pl.BlockSpec((tq, Dh), lambda h, i: (i, h)),
        out_shape=jax.ShapeDtypeStruct((S, XATTN_HEADS * Dh), BF16),
        compiler_params=_params(("parallel", "parallel")), name="xattn_fwd")(xq, kv, kv, gq, gk)


def _xattn_bwd(xq, kv, gq, gk, do):
    S = xq.shape[0]
    Mm = kv.shape[0]
    Dh = XATTN_DIM
    tq = _pick(S, (512, 256))
    nq = S // tq

    def body(q_ref, k_ref, v_ref, gq_ref, gk_ref, do_ref, dq_ref, dk_ref, dv_ref, dgq_ref, dgk_ref, dkn_acc, dv_acc):
        h = pl.program_id(0)
        i = pl.program_id(1)

        @pl.when((h == 0) & (i == 0))
        def _():
            dgq_ref[...] = jnp.zeros_like(dgq_ref)
            dgk_ref[...] = jnp.zeros_like(dgk_ref)

        @pl.when(i == 0)
        def _():
            dkn_acc[...] = jnp.zeros_like(dkn_acc)
            dv_acc[...] = jnp.zeros_like(dv_acc)

        qn32, vq = jax.vjp(_xq_norm, q_ref[...], gq_ref[...])
        kn32, vk = jax.vjp(_rms, k_ref[...], gk_ref[...])
        qn = qn32.astype(BF16)
        kn = kn32.astype(BF16)
        vb = v_ref[...].astype(BF16)
        s = _mxu(qn, kn, _NT)
        m = jnp.max(s, axis=-1, keepdims=True)
        p = jnp.exp(s - m)
        p = p / jnp.sum(p, axis=-1, keepdims=True)
        dob = do_ref[...].astype(BF16)
        dp = _mxu(dob, vb, _NT)
        delta = jnp.sum(p * dp, axis=-1, keepdims=True)
        ds = (p * (dp - delta)).astype(BF16)
        dv_acc[...] += _mxu(p.astype(BF16), dob, _TN)
        dkn_acc[...] += _mxu(ds, qn, _TN)
        dq, dgq = vq(_mxu(ds, kn))
        dq_ref[...] = dq.astype(dq_ref.dtype)
        dgq_ref[...] += dgq

        @pl.when(i == nq - 1)
        def _():
            dk, dgk = vk(dkn_acc[...])
            dk_ref[...] = dk.astype(dk_ref.dtype)
            dv_ref[...] = dv_acc[...].astype(dv_ref.dtype)
            dgk_ref[...] += dgk

    vec = pl.BlockSpec((1, Dh), lambda h, i: (0, 0))
    qblk = pl.BlockSpec((tq, Dh), lambda h, i: (i, h))
    kblk = pl.BlockSpec((Mm, Dh), lambda h, i: (0, h))
    vblk = pl.BlockSpec((Mm, Dh), lambda h, i: (0, XATTN_HEADS + h))
    return pl.pallas_call(
        body, grid=(XATTN_HEADS, nq),
        in_specs=[qblk, kblk, vblk, vec, vec, qblk],
        out_specs=[qblk, kblk, kblk, vec, vec],
        out_shape=[jax.ShapeDtypeStruct((S, XATTN_HEADS * Dh), BF16),
                   jax.ShapeDtypeStruct((Mm, XATTN_HEADS * Dh), BF16),
                   jax.ShapeDtypeStruct((Mm, XATTN_HEADS * Dh), BF16),
                   jax.ShapeDtypeStruct((1, Dh), F32), jax.ShapeDtypeStruct((1, Dh), F32)],
        scratch_shapes=[pltpu.VMEM((Mm, Dh), F32), pltpu.VMEM((Mm, Dh), F32)],
        compiler_params=_params(("arbitrary", "arbitrary")), name="xattn_bwd")(xq, kv, kv, gq, gk, do)


def _loss_head(y, target):
    S, D = y.shape
    tr = _pick(S, (512, 256))

    def body(y_ref, t_ref, dy_ref, loss_ref):
        @pl.when(pl.program_id(0) == 0)
        def _():
            loss_ref[...] = jnp.zeros_like(loss_ref)

        err = y_ref[...] - t_ref[...]
        dy_ref[...] = err * (1.0 / D)
        loss_ref[...] += jnp.sum(err * err) * (0.5 / D)

    row = pl.BlockSpec((tr, D), lambda i: (i, 0))
    return pl.pallas_call(
        body, grid=(S // tr,), in_specs=[row, row],
        out_specs=[row, pl.BlockSpec((1, LANES), lambda i: (0, 0))],
        out_shape=[jax.ShapeDtypeStruct((S, D), F32), jax.ShapeDtypeStruct((1, LANES), F32)],
        compiler_params=_params(("arbitrary",)), name="loss_head")(y, target)


def _row_tile(R, C):
    for tr in (1024, 512, 256, 128, 64, 32, 16, 8):
        if R % tr == 0 and tr * C * 4 <= (1 << 20):
            return tr
    return R


def _chip_sum(own, from_chips, name):
    R, C = own.shape
    tr = _row_tile(R, C)

    def body(own_ref, a_ref, b_ref, c_ref, o_ref):
        total = ((own_ref[...].astype(F32) + a_ref[...].astype(F32)) + b_ref[...].astype(F32)) + c_ref[...].astype(F32)
        o_ref[...] = total.astype(o_ref.dtype)

    blk = pl.BlockSpec((tr, C), lambda i: (i, 0))
    slab = lambda s: pl.BlockSpec((None, tr, C), lambda i: (s, i, 0))
    return pl.pallas_call(
        body, grid=(R // tr,), in_specs=[blk, slab(0), slab(1), slab(2)], out_specs=blk,
        out_shape=jax.ShapeDtypeStruct((R, C), BF16),
        compiler_params=_params(("parallel",)), name=name)(own, from_chips, from_chips, from_chips)


def _adamw(w, g_mine, g_sibling, m, v, name):
    R, C = w.shape
    tr = _row_tile(R, C)
    c1 = 1.0 - ADAM_B1 ** ADAM_STEP
    c2 = 1.0 - ADAM_B2 ** ADAM_STEP

    def body(w_ref, ga_ref, gb_ref, m_ref, v_ref, g_ref, d_ref, mo_ref, vo_ref):
        g_t = ga_ref[...].astype(F32) + gb_ref[...].astype(F32)
        m_new = ADAM_B1 * m_ref[...] + (1.0 - ADAM_B1) * g_t
        v_new = ADAM_B2 * v_ref[...] + (1.0 - ADAM_B2) * (g_t * g_t)
        g_ref[...] = g_t
        d_ref[...] = -ADAM_LR * ((m_new / c1) / (jnp.sqrt(v_new / c2) + ADAM_EPS) + ADAM_WD * w_ref[...])
        mo_ref[...] = m_new
        vo_ref[...] = v_new

    blk = pl.BlockSpec((tr, C), lambda i: (i, 0))
    return pl.pallas_call(
        body, grid=(R // tr,), in_specs=[blk] * 5, out_specs=[blk] * 4,
        out_shape=[jax.ShapeDtypeStruct((R, C), F32)] * 4,
        compiler_params=_params(("parallel",)), name=name)(w, g_mine, g_sibling, m, v)


SSM_INNER = SSM_HEADS * HEAD_DIM
CONV_DIM = SSM_INNER + 2 * SSM_GROUPS * SSM_STATE
ATTN_WIDTH = ATTN_HEADS * HEAD_DIM
MIX_WIDTH = SSM_INNER + ATTN_WIDTH
COL_Z = 0
COL_XBC = COL_Z + SSM_INNER
COL_Q = COL_XBC + CONV_DIM
COL_K = COL_Q + ATTN_WIDTH
COL_V = COL_K + ATTN_WIDTH
COL_DT = COL_V + ATTN_WIDTH
COL_F = COL_DT + SSM_HEADS
IN_COLS = COL_F + ATTN_HEADS
IN_COLS_PAD = -(-IN_COLS // LANES) * LANES
REF_COL_DT = COL_Q
SHARD_COLS = IN_COLS // N_CHIPS
_COL_RANGES = ((0, REF_COL_DT, 0), (REF_COL_DT + SSM_HEADS, COL_F, COL_Q), (REF_COL_DT, REF_COL_DT + SSM_HEADS, COL_DT),
               (COL_F, IN_COLS, COL_F))


def _w_in_from_shards(g):
    parts = []
    for lo, hi, _ in _COL_RANGES:
        while lo < hi:
            j = lo // SHARD_COLS
            end = min(hi, (j + 1) * SHARD_COLS)
            parts.append(g[j][:, lo - j * SHARD_COLS:end - j * SHARD_COLS])
            lo = end
    parts.append(jnp.zeros((g.shape[1], IN_COLS_PAD - IN_COLS), g.dtype))
    return jnp.concatenate(parts, axis=1)


def _w_in_to_shards(w):
    shards = []
    for j in range(N_CHIPS):
        parts = []
        for lo, hi, here in sorted(_COL_RANGES):
            a, b = max(lo, j * SHARD_COLS), min(hi, (j + 1) * SHARD_COLS)
            if a < b:
                parts.append(w[:, here + a - lo:here + b - lo])
        shards.append(jnp.concatenate(parts, axis=1))
    return jnp.stack(shards)


def _add_residual(acc, res):
    return (res + acc,)


def _relu2(acc):
    r = jnp.maximum(acc, 0.0)
    return acc, r * r


def _relu2_bwd(acc, a):
    return (acc * (2.0 * jnp.maximum(a.astype(F32), 0.0)),)


def _layer_fwd_bwd(x, mem, target, w_in, p, late_weights, send_late_grads, send_w_in_grad):
    S = x.shape[0]
    hd3 = lambda a: a.reshape(SSM_HEADS, 1, 1)

    h1 = _rmsnorm_fwd(x, p["g_mix"], "norm_mix")
    proj = _mm(h1, w_in, "nn", "in_proj")
    xbc = _conv_fwd(proj, COL_XBC, CONV_DIM, p["conv_w"], p["conv_b"])
    dt_hm = proj[:, COL_DT:COL_DT + SSM_HEADS].T[:, :, None]
    ssd_par = (hd3(p["dt_bias"]), hd3(p["a_log"]), hd3(p["d_skip"]), p["ssm_norm_w"])
    mixed, hs = _ssd_fwd(xbc, proj, dt_hm, *ssd_par)
    f_raw = proj[:, COL_F:COL_F + ATTN_HEADS]
    gq2 = jnp.tile(p["g_q"], (1, 2))
    gk2 = jnp.tile(p["g_k"], (1, 2))
    qs, kn, vb = _qk_prep_fwd(proj, gq2, gk2)
    cum, cq = _logf_cumsum_fwd(f_raw, p["f_bias"])
    ck = cum.T[:, None, :]
    mixed, o_fine, lse = _flash_fwd(qs, kn, vb, cq, ck, mixed)
    W = late_weights((mixed,))
    x1 = _mm(mixed, W["w_out"], "nn", "out_proj", epilogue=_add_residual, extras=(x,))
    h2 = _rmsnorm_fwd(x1, p["g_xattn"], "norm_xattn")
    mem_n = _rmsnorm_fwd(mem, p["g_mem"], "norm_mem")
    xq = _mm(h2, W["xq_w"], "nn", "xq_proj")
    kv = _mm(mem_n, W["xkv_w"], "nn", "xkv_proj", b_chunks=N_CHIPS)
    xo = _xattn_fwd(xq, kv, p["xg_q"], p["xg_k"])
    x2 = _mm(xo, W["xo_w"], "nn", "xo_proj", epilogue=_add_residual, extras=(x1,))
    h3 = _rmsnorm_fwd(x2, p["g_mlp"], "norm_mlp")
    a, act = _mm(h3, W["w_up"], "nn", "mlp_up", out_dtypes=(BF16, BF16), epilogue=_relu2, b_chunks=N_CHIPS)
    x3 = _mm(act, W["w_down"], "nn", "mlp_down", epilogue=_add_residual, extras=(x2,))
    dy, loss_row = _loss_head(x3, target)

    gW, gp = {}, {}
    da = _mm(dy, W["w_down"], "nt", "d_act", out_dtypes=(BF16,), epilogue=_relu2_bwd, extras=(a,))
    gW["w_down"] = _mm(act, dy, "tn", "g_w_down", out_dtypes=(BF16,))
    gW["w_up"] = _mm(h3, da, "tn", "g_w_up", out_dtypes=(BF16,), out_chunks=N_CHIPS)
    dh3 = _mm(da, W["w_up"], "nt", "d_h3", b_chunks=N_CHIPS)
    dx2, gp["g_mlp"] = _rmsnorm_bwd(x2, p["g_mlp"], dh3, dy, "norm_mlp_bwd")
    dxo = _mm(dx2, W["xo_w"], "nt", "d_xo", out_dtypes=(BF16,))
    gW["xo_w"] = _mm(xo, dx2, "tn", "g_xo_w", out_dtypes=(BF16,))
    dxq, dk_x, dv_x, gp["xg_q"], gp["xg_k"] = _xattn_bwd(xq, kv, p["xg_q"], p["xg_k"], dxo)
    dkv = jnp.concatenate([dk_x, dv_x], axis=-1)
    gW["xq_w"] = _mm(h2, dxq, "tn", "g_xq_w", out_dtypes=(BF16,))
    dh2 = _mm(dxq, W["xq_w"], "nt", "d_h2")
    gW["xkv_w"] = _mm(mem_n, dkv, "tn", "g_xkv_w", out_dtypes=(BF16,), out_chunks=N_CHIPS)
    dmem_n = _mm(dkv, W["xkv_w"], "nt", "d_mem_n", b_chunks=N_CHIPS)
    _, gp["g_mem"] = _rmsnorm_bwd(mem, p["g_mem"], dmem_n, None, "norm_mem_bwd")
    dx1, gp["g_xattn"] = _rmsnorm_bwd(x1, p["g_xattn"], dh2, dx2, "norm_xattn_bwd")
    dmixed = _mm(dx1, W["w_out"], "nt", "d_mixed")
    gW["w_out"] = _mm(mixed, dx1, "tn", "g_w_out", out_dtypes=(BF16,))
    token = send_late_grads(gW)
    dqs, dkn, dproj, dck = _flash_bwd(qs, kn, vb, cq, ck + token[:1, :1], o_fine, dmixed, SSM_INNER, lse)
    dproj, dgq2 = _pair_norm_bwd(proj, COL_Q, gq2, ATTN_SCALE, dqs, dproj, "q_norm_bwd")
    dproj, dgk2 = _pair_norm_bwd(proj, COL_K, gk2, 1.0, dkn, dproj, "k_norm_bwd")
    gp["g_q"] = dgq2[:, :HEAD_DIM] + dgq2[:, HEAD_DIM:]
    gp["g_k"] = dgk2[:, :HEAD_DIM] + dgk2[:, HEAD_DIM:]
    df, gp["f_bias"] = _logf_cumsum_bwd(f_raw, p["f_bias"], dck[:, 0, :].T)
    dxs, dproj, dB, dC, ddt, ddtb, dalog, ddsk, gp["ssm_norm_w"] = _ssd_bwd(xbc, proj, dt_hm, *ssd_par, hs, dmixed, dproj)
    gp["dt_bias"] = ddtb.reshape(1, SSM_HEADS)
    gp["a_log"] = dalog.reshape(1, SSM_HEADS)
    gp["d_skip"] = ddsk.reshape(1, SSM_HEADS)
    dproj, dconv_w, gp["conv_b"] = _conv_bwd(proj, COL_XBC, CONV_DIM, p["conv_w"], p["conv_b"], (dxs, dB, dC), dproj)
    gp["conv_w"] = dconv_w[:CONV_WIDTH]
    tail = jnp.concatenate([ddt[:, :, 0].T, df, jnp.zeros((S, IN_COLS_PAD - IN_COLS), F32)], axis=-1).astype(BF16)
    dproj = lax.dynamic_update_slice(dproj, tail, (0, COL_DT))
    token = send_w_in_grad(_mm(h1, dproj, "tn", "g_w_in", out_dtypes=(BF16,)))
    dh1 = _mm(dproj, w_in, "nt", "d_h1")
    dx, gp["g_mix"] = _rmsnorm_bwd(x, p["g_mix"] + token[:1, :1], dh1, dx1, "norm_mix_bwd")
    return loss_row, dx, gp


_ANY = pl.BlockSpec(memory_space=pl.ANY)


def _place():
    x, y, c = lax.axis_index("x"), lax.axis_index("y"), lax.axis_index("c")
    chips = [(1 - x, y), (x, 1 - y), (1 - x, 1 - y)]
    return x, y, c, chips


def _chip_index(px, py):
    return 2 * px + py


def _all_gather_chips(split, whole):
    ns, nw = len(split), len(whole)
    n = ns + nw

    def body(*refs):
        ins, outs = refs[:n], refs[n:2 * n]
        send_ici, recv_ici, send_d2d, recv_d2d = refs[2 * n:]
        x, y, c, chips = _place()
        me = _chip_index(x, y)
        sib = (x, y, 1 - c)

        def ici(k, j, src, dst):
            return pltpu.make_async_remote_copy(src_ref=src, dst_ref=dst, send_sem=send_ici.at[3 * k + j],
                                                recv_sem=recv_ici.at[3 * k + j], device_id=(*chips[j], c),
                                                device_id_type=MESH)

        def d2d(k, j, piece):
            return pltpu.make_async_remote_copy(src_ref=piece, dst_ref=piece, send_sem=send_d2d.at[3 * k + j],
                                                recv_sem=recv_d2d.at[3 * k + j], device_id=sib, device_id_type=MESH)

        sends = []
        for k in range(n):
            for j in range(3):
                if k < ns:
                    sends.append(ici(k, j, ins[k].at[c], outs[k].at[me, c]))
                else:
                    sends.append(ici(k, j, ins[k], outs[k].at[me]))
                sends[-1].start()
        passed = []
        for k in range(n):
            for j in range(3):
                src_chip = _chip_index(*chips[j])
                if k < ns:
                    ici(k, j, ins[k].at[c], outs[k].at[src_chip, c]).wait_recv()
                    passed.append(d2d(k, j, outs[k].at[src_chip, c]))
                    passed[-1].start()
                else:
                    ici(k, j, ins[k], outs[k].at[src_chip]).wait_recv()
        for k in range(ns):
            for j in range(3):
                d2d(k, j, outs[k].at[_chip_index(*chips[j]), 1 - c]).wait_recv()
        for cp in sends + passed:
            cp.wait_send()

    arrs = list(split) + list(whole)
    return pl.pallas_call(
        body, in_specs=[_ANY] * n, out_specs=[_ANY] * n,
        out_shape=[jax.ShapeDtypeStruct((N_CHIPS,) + a.shape, a.dtype) for a in arrs],
        scratch_shapes=[pltpu.SemaphoreType.DMA((3 * n,)), pltpu.SemaphoreType.DMA((3 * n,)),
                        pltpu.SemaphoreType.DMA((3 * ns,)), pltpu.SemaphoreType.DMA((3 * ns,))],
        name="all_gather_chips")(*arrs)


def _sibling_swap(arrs, name):
    n = len(arrs)

    def body(*refs):
        ins, outs = refs[:n], refs[n:2 * n]
        send_sem, recv_sem = refs[2 * n:]
        x, y, c, _ = _place()
        copies = [pltpu.make_async_remote_copy(src_ref=ins[k], dst_ref=outs[k], send_sem=send_sem.at[k],
                                               recv_sem=recv_sem.at[k], device_id=(x, y, 1 - c), device_id_type=MESH)
                  for k in range(n)]
        for q in copies:
            q.start()
        for q in copies:
            q.wait()

    return pl.pallas_call(
        body, in_specs=[_ANY] * n, out_specs=[_ANY] * n,
        out_shape=[jax.ShapeDtypeStruct(a.shape, a.dtype) for a in arrs],
        scratch_shapes=[pltpu.SemaphoreType.DMA((n,)), pltpu.SemaphoreType.DMA((n,))],
        name=name)(*arrs)


_HBM = pl.BlockSpec(memory_space=pltpu.HBM)
_SEM = pl.BlockSpec(memory_space=pltpu.SEMAPHORE)
_SPLIT_EFFECT = pltpu.SideEffectType.DATAFLOW_SIDE_EFFECTING


class _Split(NamedTuple):
    send_sems: jax.Array
    recv_sems: jax.Array
    sources: tuple
    lands: tuple
    token: jax.Array


def _split_copies(kind, srcs, lands, send_sems, recv_sems):
    x, y, c, chips = _place()
    me = _chip_index(x, y)
    copies = []
    for k in range(len(srcs)):
        for j in range(3):
            if kind == "gather":
                src, dst = srcs[k], lands[k].at[me]
            else:
                src, dst = srcs[k].at[_chip_index(*chips[j])], lands[k].at[j]
            copies.append(pltpu.make_async_remote_copy(
                src_ref=src, dst_ref=dst, send_sem=send_sems.at[3 * k + j], recv_sem=recv_sems.at[3 * k + j],
                device_id=(*chips[j], c), device_id_type=MESH))
    return copies


def _split_start(name, sources, kind, after):
    n = len(sources)
    if kind == "gather":
        lands = [lax.empty((N_CHIPS,) + s.shape, s.dtype) for s in sources]
    else:
        lands = [lax.empty((3,) + s.shape[1:], s.dtype) for s in sources]
    deps = [] if after is None else [after]

    def body(*refs):
        srcs, lnds = refs[:n], refs[n:2 * n]
        send_sems, recv_sems = refs[2 * n + len(deps)], refs[2 * n + len(deps) + 1]
        for cp in _split_copies(kind, srcs, lnds, send_sems, recv_sems):
            cp.start()
        refs[-1][...] = jnp.zeros_like(refs[-1])

    hbm = lambda a: pltpu.with_memory_space_constraint(a, pltpu.HBM)
    outs = pl.pallas_call(
        body, name=name,
        in_specs=[_HBM] * (2 * n) + [_ANY] * len(deps),
        out_specs=[_SEM, _SEM] + [_HBM] * (2 * n) + [pl.BlockSpec(memory_space=pltpu.VMEM)],
        out_shape=[pltpu.SemaphoreType.DMA((3 * n,)), pltpu.SemaphoreType.DMA((3 * n,))]
        + [pltpu.HBM(a.shape, a.dtype) for a in list(sources) + lands] + [jax.ShapeDtypeStruct((8, LANES), F32)],
        input_output_aliases={k: 2 + k for k in range(2 * n)},
        compiler_params=pltpu.CompilerParams(has_side_effects=_SPLIT_EFFECT),
    )(*[hbm(s) for s in sources], *[hbm(l) for l in lands], *deps)
    return _Split(outs[0], outs[1], tuple(outs[2:2 + n]), tuple(outs[2 + n:2 + 2 * n]), outs[-1])


def _split_wait(name, h, kind, after):
    n = len(h.sources)

    def body(*refs):
        srcs, lnds = refs[:n], refs[n:2 * n]
        for cp in _split_copies(kind, srcs, lnds, refs[2 * n], refs[2 * n + 1]):
            cp.wait_send()
            cp.wait_recv()

    outs = pl.pallas_call(
        body, name=name,
        in_specs=[_HBM] * (2 * n) + [_SEM, _SEM] + [_ANY] * len(after),
        out_specs=[_HBM] * (2 * n),
        out_shape=[pltpu.HBM(a.shape, a.dtype) for a in h.sources + h.lands],
        input_output_aliases={k: k for k in range(2 * n)},
        compiler_params=pltpu.CompilerParams(has_side_effects=_SPLIT_EFFECT),
    )(*h.sources, *h.lands, h.send_sems, h.recv_sems, *after)
    return outs[:n], outs[n:]


def _all_reduce_small(vec, after):
    R, C = vec.shape

    def body(v_ref, after_ref, o_ref, buf, send_sem, recv_sem):
        x, y, c = lax.axis_index("x"), lax.axis_index("y"), lax.axis_index("c")
        me = 4 * x + 2 * y + c
        buf[me] = v_ref[...]
        copies = []
        for r in range(1, N_DEV):
            fx, fy, fc = (r >> 2) & 1, (r >> 1) & 1, r & 1
            peer = (x ^ fx, y ^ fy, c ^ fc)
            copies.append(pltpu.make_async_remote_copy(src_ref=v_ref, dst_ref=buf.at[me], send_sem=send_sem.at[r - 1],
                                                       recv_sem=recv_sem.at[r - 1], device_id=peer, device_id_type=MESH))
        for q in copies:
            q.start()
        for r in range(1, N_DEV):
            fx, fy, fc = (r >> 2) & 1, (r >> 1) & 1, r & 1
            src = 4 * (x ^ fx) + 2 * (y ^ fy) + (c ^ fc)
            pltpu.make_async_remote_copy(src_ref=v_ref, dst_ref=buf.at[src], send_sem=send_sem.at[r - 1],
                                         recv_sem=recv_sem.at[r - 1], device_id=(x, y, c), device_id_type=MESH).wait_recv()
        acc = buf[0]
        for d in range(1, N_DEV):
            acc = acc + buf[d]
        o_ref[...] = acc
        for q in copies:
            q.wait_send()

    vm = pl.BlockSpec(memory_space=pltpu.VMEM)
    return pl.pallas_call(
        body, in_specs=[vm, _ANY], out_specs=vm, out_shape=jax.ShapeDtypeStruct((R, C), F32),
        scratch_shapes=[pltpu.VMEM((N_DEV, R, C), F32), pltpu.SemaphoreType.DMA((N_DEV - 1,)),
                        pltpu.SemaphoreType.DMA((N_DEV - 1,))],
        name="all_reduce_small")(vec, after)


_INPUTS = ["x", "mem", "g_mix", "w_in", "conv_w", "conv_b", "dt_bias", "a_log", "d_skip", "ssm_norm_w", "g_q", "g_k",
           "f_bias", "w_out", "g_xattn", "g_mem", "xq_w", "xkv_w", "xg_q", "xg_k", "xo_w", "g_mlp", "w_up", "w_down"]
_WEIGHTS = _INPUTS[2:]
_BIG = ["w_in", "w_out", "xq_w", "xkv_w", "xo_w", "w_up", "w_down"]
_LATE = _BIG[1:]
_COL_SHARDED = ["w_in", "xkv_w", "w_up"]
_SMALL = [n for n in _WEIGHTS if n not in _BIG]


def _pack_rows(arrs, width):
    starts, r = [], 0
    for a in arrs:
        starts.append(r)
        r += a.shape[0]
    out = jnp.concatenate([jnp.pad(a, ((0, 0), (0, width - a.shape[1]))) for a in arrs], axis=0)
    return jnp.pad(out, ((0, -r % 8), (0, 0))), starts


def _adamw_small(summed, starts, ws, ms, vs, conv_w_index):
    n = len(ws)
    c1 = 1.0 - ADAM_B1 ** ADAM_STEP
    c2 = 1.0 - ADAM_B2 ** ADAM_STEP

    def body(s_ref, *refs):
        w_refs, m_refs, v_refs = refs[:n], refs[n:2 * n], refs[2 * n:3 * n]
        outs = refs[3 * n:]
        chip = _chip_index(lax.axis_index("x"), lax.axis_index("y"))
        for k in range(n):
            rows, cols = w_refs[k].shape
            if k == conv_w_index:
                g = s_ref[starts[k]:starts[k] + rows, pl.ds(pl.multiple_of(chip * cols, LANES), cols)]
            else:
                g = s_ref[starts[k]:starts[k] + rows, 0:cols]
            m_new = ADAM_B1 * m_refs[k][...] + (1.0 - ADAM_B1) * g
            v_new = ADAM_B2 * v_refs[k][...] + (1.0 - ADAM_B2) * (g * g)
            outs[4 * k][...] = g
            outs[4 * k + 1][...] = -ADAM_LR * ((m_new / c1) / (jnp.sqrt(v_new / c2) + ADAM_EPS) + ADAM_WD * w_refs[k][...])
            outs[4 * k + 2][...] = m_new
            outs[4 * k + 3][...] = v_new

    vm = pl.BlockSpec(memory_space=pltpu.VMEM)
    outs = pl.pallas_call(
        body, in_specs=[vm] * (1 + 3 * n), out_specs=[vm] * (4 * n),
        out_shape=[jax.ShapeDtypeStruct(a.shape, F32) for a in ws for _ in range(4)],
        name="adamw_small")(summed, *ws, *ms, *vs)
    return [outs[4 * k:4 * k + 4] for k in range(n)]


def kernel(x, mem, g_mix, w_in, conv_w, conv_b, dt_bias, a_log, d_skip, ssm_norm_w, g_q, g_k, f_bias, w_out, g_xattn, g_mem, xq_w, xkv_w, xg_q, xg_k, xo_w, g_mlp, w_up, w_down, loss_target, m_g_mix, m_w_in, m_conv_w, m_conv_b, m_dt_bias, m_a_log, m_d_skip, m_ssm_norm_w, m_g_q, m_g_k, m_f_bias, m_w_out, m_g_xattn, m_g_mem, m_xq_w, m_xkv_w, m_xg_q, m_xg_k, m_xo_w, m_g_mlp, m_w_up, m_w_down, v_g_mix, v_w_in, v_conv_w, v_conv_b, v_dt_bias, v_a_log, v_d_skip, v_ssm_norm_w, v_g_q, v_g_k, v_f_bias, v_w_out, v_g_xattn, v_g_mem, v_xq_w, v_xkv_w, v_xg_q, v_xg_k, v_xo_w, v_g_mlp, v_w_up, v_w_down):
    args = (x, mem, g_mix, w_in, conv_w, conv_b, dt_bias, a_log, d_skip, ssm_norm_w, g_q, g_k, f_bias, w_out, g_xattn,
            g_mem, xq_w, xkv_w, xg_q, xg_k, xo_w, g_mlp, w_up, w_down)
    w = dict(zip(_INPUTS, args))
    mom1 = dict(zip(_WEIGHTS, (m_g_mix, m_w_in, m_conv_w, m_conv_b, m_dt_bias, m_a_log, m_d_skip, m_ssm_norm_w, m_g_q,
                               m_g_k, m_f_bias, m_w_out, m_g_xattn, m_g_mem, m_xq_w, m_xkv_w, m_xg_q, m_xg_k, m_xo_w,
                               m_g_mlp, m_w_up, m_w_down)))
    mom2 = dict(zip(_WEIGHTS, (v_g_mix, v_w_in, v_conv_w, v_conv_b, v_dt_bias, v_a_log, v_d_skip, v_ssm_norm_w, v_g_q,
                               v_g_k, v_f_bias, v_w_out, v_g_xattn, v_g_mem, v_xq_w, v_xkv_w, v_xg_q, v_xg_k, v_xo_w,
                               v_g_mlp, v_w_up, v_w_down)))
    chip = _chip_index(lax.axis_index("x"), lax.axis_index("y"))

    shard_bf = {n: w[n][0].astype(BF16) for n in _BIG}

    def layout_for_compute(n, g):
        if n == "w_in":
            return _w_in_from_shards(g)
        return g if n in _COL_SHARDED else g.reshape(N_CHIPS * g.shape[1], g.shape[2])

    def layout_for_reduction(n, g):
        if n == "w_in":
            return _w_in_to_shards(g)
        return g if n in _COL_SHARDED else g.reshape(N_CHIPS, g.shape[0] // N_CHIPS, g.shape[1])

    halves_in = shard_bf["w_in"].reshape(2, shard_bf["w_in"].shape[0] // 2, -1)
    g_in, g_conv = _all_gather_chips([halves_in], [w["conv_w"][0]])
    g_in = lax.dynamic_update_index_in_dim(g_in, halves_in, chip, axis=0)
    g_conv = lax.dynamic_update_index_in_dim(g_conv, w["conv_w"][0], chip, axis=0)
    w_in_full = layout_for_compute("w_in", g_in.reshape(N_CHIPS, -1, g_in.shape[-1]))
    p = {n: w[n] for n in _SMALL}
    p["conv_w"] = g_conv.transpose(1, 0, 2).reshape(CONV_WIDTH, CONV_DIM)
    gather = _split_start("gather_late", [shard_bf[n] for n in _LATE], "gather", after=g_in)
    p["g_mix"] = p["g_mix"] + gather.token[:1, :1]

    def late_weights(after):
        srcs, lands = _split_wait("gather_late_wait", gather, "gather", after)
        lands = [lax.dynamic_update_index_in_dim(l, s, chip, axis=0) for l, s in zip(lands, srcs)]
        return {n: layout_for_compute(n, l) for n, l in zip(_LATE, lands)}

    scatter = {}

    def send_late_grads(grads):
        scatter["late"] = _split_start("scatter_late", [layout_for_reduction(n, grads[n]) for n in _LATE], "scatter",
                                       after=None)
        return scatter["late"].token

    def send_w_in_grad(g):
        scatter["w_in"] = _split_start("scatter_w_in", [layout_for_reduction("w_in", g)], "scatter", after=None)
        return scatter["w_in"].token

    loss_row, dx, gp = _layer_fwd_bwd(x[0], mem[0], loss_target[0], w_in_full, p, late_weights, send_late_grads,
                                      send_w_in_grad)

    grad, delta, new_m, new_v = {}, {}, {}, {}

    def finish(names, sources, from_chips, tag):
        mine = [_chip_sum(lax.dynamic_index_in_dim(s, chip, axis=0, keepdims=False), fc, "rs_chip_sum_" + n)
                for n, s, fc in zip(names, sources, from_chips)]
        for n, a, b in zip(names, mine, _sibling_swap(mine, "rs_sibling_swap_" + tag)):
            shape = w[n].shape
            res = _adamw(w[n][0], a, b, mom1[n][0], mom2[n][0], "adamw_" + n)
            grad[n], delta[n], new_m[n], new_v[n] = (r.reshape(shape) for r in res)

    finish(_LATE, *_split_wait("scatter_late_wait", scatter["late"], "scatter", (dx,)), "late")

    sources_in, from_chips_in = _split_wait("scatter_w_in_wait", scatter["w_in"], "scatter",
                                            tuple(new_v[n] for n in _LATE))

    packed, starts = _pack_rows([gp[n] for n in _SMALL] + [loss_row], CONV_DIM)
    summed = _all_reduce_small(packed, from_chips_in[0])
    loss = summed[starts[-1], 0]
    finish(["w_in"], sources_in, from_chips_in, "w_in")

    as_rows = lambda a: a.reshape(-1, a.shape[-1])
    results = _adamw_small(summed, starts, [as_rows(w[n]) for n in _SMALL], [as_rows(mom1[n]) for n in _SMALL],
                           [as_rows(mom2[n]) for n in _SMALL], _SMALL.index("conv_w"))
    for n, res in zip(_SMALL, results):
        grad[n], delta[n], new_m[n], new_v[n] = (a.reshape(w[n].shape) for a in res)

    return (loss, dx[None], *[grad[n] for n in _WEIGHTS], *[delta[n] for n in _WEIGHTS],
            *[new_m[n] for n in _WEIGHTS], *[new_v[n] for n in _WEIGHTS])
```

```python
import functools
from typing import NamedTuple

import jax
import jax.numpy as jnp
from jax import lax
from jax.experimental import pallas as pl
from jax.experimental.pallas import tpu as pltpu

F32 = jnp.float32
BF16 = jnp.bfloat16
HI = lax.Precision.HIGHEST
MESH = pl.DeviceIdType.MESH

EPS = 1e-5
CHUNK = 128
SSM_HEADS = 16
SSM_GROUPS = 2
HEADS_PER_GROUP = SSM_HEADS // SSM_GROUPS
HEAD_DIM = 64
SSM_STATE = 128
ATTN_HEADS = 16
XATTN_HEADS = 4
XATTN_DIM = 256
CONV_WIDTH = 4
CONV_COLS = 256
N_CHIPS = 4
N_DEV = 8
LANES = 128
VMEM_LIMIT = 56 * 1024 * 1024

ADAM_LR = 0.001
ADAM_B1 = 0.9
ADAM_B2 = 0.999
ADAM_EPS = 1e-08
ADAM_WD = 0.01
ADAM_STEP = 10


def _params(sem):
    return pltpu.CompilerParams(dimension_semantics=sem, vmem_limit_bytes=VMEM_LIMIT)


def _pick(n, cands):
    for c in cands:
        if n % c == 0:
            return c
    return n


def _mm(a, b, mode, name, out_dtypes=(F32,), epilogue=None, extras=(), b_chunks=1, out_chunks=1,
        tm=None, tn=None, tk=None):
    if mode == "nn":
        M, K = a.shape
        N = b.shape[-1] * b_chunks
    elif mode == "nt":
        M, K = a.shape
        N = b.shape[-2]
        assert b.shape[-1] * b_chunks == K
    else:
        K, M = a.shape
        N = b.shape[-1] * b_chunks
    tm = tm or _pick(M, (2048, 1024, 512, 256, 128))
    tn = tn or _pick(N // max(b_chunks if mode != "nt" else 1, out_chunks), (512, 640, 384, 256, 128))
    if tk is None:
        kmax = b.shape[-1] if mode == "nt" else K
        tk = kmax if kmax <= 2048 else _pick(kmax, (2048, 1920, 1152, 1024, 512))
    nk = K // tk
    assert M % tm == 0 and N % tn == 0 and K % tk == 0
    grid = (M // tm, N // tn, nk)

    if mode == "tn":
        a_spec = pl.BlockSpec((tk, tm), lambda i, j, k: (k, i))
    else:
        a_spec = pl.BlockSpec((tm, tk), lambda i, j, k: (i, k))

    def b_index(t_row, t_last, tile_last):
        if b_chunks == 1:
            return (t_row, t_last)
        q = (b.shape[-1]) // tile_last
        return (t_last // q, t_row, t_last % q)

    if mode == "nn" or mode == "tn":
        bshape = (tk, tn)
        bmap = lambda i, j, k: b_index(k, j, tn)
    else:
        bshape = (tn, tk)
        bmap = lambda i, j, k: b_index(j, k, tk)
    if b_chunks > 1:
        bshape = (None,) + bshape
    b_spec = pl.BlockSpec(bshape, bmap)

    if out_chunks == 1:
        o_spec = pl.BlockSpec((tm, tn), lambda i, j, k: (i, j))
        o_shape = (M, N)
    else:
        qo = (N // out_chunks) // tn
        o_spec = pl.BlockSpec((None, tm, tn), lambda i, j, k: (j // qo, i, j % qo))
        o_shape = (out_chunks, M, N // out_chunks)
    e_spec = pl.BlockSpec((tm, tn), lambda i, j, k: (i, j))

    dims = {"nn": (((1,), (0,)), ((), ())), "nt": (((1,), (1,)), ((), ())), "tn": (((0,), (0,)), ((), ()))}[mode]
    n_ex = len(extras)
    n_out = len(out_dtypes)

    def body(*refs):
        a_ref, b_ref = refs[0], refs[1]
        ex_refs = refs[2:2 + n_ex]
        o_refs = refs[2 + n_ex:2 + n_ex + n_out]

        def finish(acc):
            outs = epilogue(acc, *[r[...] for r in ex_refs]) if epilogue is not None else (acc,)
            for r, o in zip(o_refs, outs):
                r[...] = o.astype(r.dtype)

        part = lax.dot_general(a_ref[...].astype(BF16), b_ref[...].astype(BF16), dims,
                               preferred_element_type=F32)
        if nk == 1:
            finish(part)
        else:
            acc_ref = refs[-1]
            k = pl.program_id(2)

            @pl.when(k == 0)
            def _():
                acc_ref[...] = part

            @pl.when(k > 0)
            def _():
                acc_ref[...] += part

            @pl.when(k == nk - 1)
            def _():
                finish(acc_ref[...])

    outs = pl.pallas_call(
        body,
        grid=grid,
        in_specs=[a_spec, b_spec] + [e_spec] * n_ex,
        out_specs=[o_spec] * n_out,
        out_shape=[jax.ShapeDtypeStruct(o_shape, d) for d in out_dtypes],
        scratch_shapes=[pltpu.VMEM((tm, tn), F32)] if nk > 1 else [],
        compiler_params=_params(("parallel", "parallel", "arbitrary")),
        name=name,
    )(a, b, *extras)
    return outs[0] if n_out == 1 else outs


def _rms(x, g):
    r = lax.rsqrt(jnp.mean(x * x, axis=-1, keepdims=True) + EPS)
    return x * r * g


def _rmsnorm_fwd(x, g, name):
    R, D = x.shape
    tr = _pick(R, (512, 256))

    def body(x_ref, g_ref, o_ref):
        o_ref[...] = _rms(x_ref[...], g_ref[...]).astype(o_ref.dtype)

    return pl.pallas_call(
        body, grid=(R // tr,),
        in_specs=[pl.BlockSpec((tr, D), lambda i: (i, 0)), pl.BlockSpec((1, D), lambda i: (0, 0))],
        out_specs=pl.BlockSpec((tr, D), lambda i: (i, 0)),
        out_shape=jax.ShapeDtypeStruct((R, D), BF16),
        compiler_params=_params(("parallel",)), name=name)(x, g)


def _rmsnorm_bwd(x, g, dh, dres, name):
    R, D = x.shape
    tr = _pick(R, (256,))
    has_res = dres is not None

    def body(*refs):
        if has_res:
            x_ref, g_ref, dh_ref, dres_ref, dx_ref, dg_ref = refs
        else:
            x_ref, g_ref, dh_ref, dx_ref, dg_ref = refs
        _, vjp = jax.vjp(_rms, x_ref[...], g_ref[...])
        dx, dg = vjp(dh_ref[...])
        if has_res:
            dx = dx + dres_ref[...]
        dx_ref[...] = dx

        @pl.when(pl.program_id(0) == 0)
        def _():
            dg_ref[...] = jnp.zeros_like(dg_ref)

        dg_ref[...] += dg

    row = pl.BlockSpec((tr, D), lambda i: (i, 0))
    vec = pl.BlockSpec((1, D), lambda i: (0, 0))
    ins = [x, g, dh] + ([dres] if has_res else [])
    return pl.pallas_call(
        body, grid=(R // tr,),
        in_specs=[row, vec, row] + ([row] if has_res else []),
        out_specs=[row, vec],
        out_shape=[jax.ShapeDtypeStruct((R, D), F32), jax.ShapeDtypeStruct((1, D), F32)],
        compiler_params=_params(("arbitrary",)), name=name)(*ins)


def _shift_down(u, k):
    if k == 0:
        return u
    rows = lax.broadcasted_iota(jnp.int32, u.shape, 0)
    return jnp.where(rows >= k, pltpu.roll(u, k, axis=0), 0.0)


def _shift_up(u, k):
    if k == 0:
        return u
    n = u.shape[0]
    rows = lax.broadcasted_iota(jnp.int32, u.shape, 0)
    return jnp.where(rows < n - k, pltpu.roll(u, n - k, axis=0), 0.0)


def _conv_pre(u, w, b):
    pre = b
    for j in range(CONV_WIDTH):
        pre = pre + w[j:j + 1, :] * _shift_down(u, CONV_WIDTH - 1 - j)
    return pre


def _conv_fwd(proj, col0, ncols, conv_w, conv_b):
    S = proj.shape[0]
    cb0 = col0 // CONV_COLS

    def body(u_ref, w_ref, b_ref, o_ref):
        pre = _conv_pre(u_ref[...], w_ref[...], b_ref[...])
        o_ref[...] = pre * jax.nn.sigmoid(pre)

    return pl.pallas_call(
        body, grid=(ncols // CONV_COLS,),
        in_specs=[pl.BlockSpec((S, CONV_COLS), lambda j: (0, j + cb0)),
                  pl.BlockSpec((CONV_WIDTH, CONV_COLS), lambda j: (0, j)),
                  pl.BlockSpec((1, CONV_COLS), lambda j: (0, j))],
        out_specs=pl.BlockSpec((S, CONV_COLS), lambda j: (0, j)),
        out_shape=jax.ShapeDtypeStruct((S, ncols), F32),
        compiler_params=_params(("parallel",)), name="conv_fwd")(proj, conv_w, conv_b)


def _conv_bwd(proj, col0, ncols, conv_w, conv_b, douts, dproj):
    S = proj.shape[0]
    cb0 = col0 // CONV_COLS
    starts = [0]
    for d in douts:
        starts.append(starts[-1] + d.shape[1] // CONV_COLS)
    assert starts[-1] == ncols // CONV_COLS
    nd = len(douts)

    def body(u_ref, w_ref, b_ref, *rest):
        d_refs, (du_ref, dw_ref, db_ref) = rest[:nd], rest[nd + 1:]
        j = pl.program_id(0)
        dout = d_refs[-1][...]
        for i in range(nd - 2, -1, -1):
            dout = jnp.where(j < starts[i + 1], d_refs[i][...], dout)
        u = u_ref[...]
        w = w_ref[...]
        pre = _conv_pre(u, w, b_ref[...])
        s = jax.nn.sigmoid(pre)
        dpre = dout * (s * (1.0 + pre * (1.0 - s)))
        du = jnp.zeros_like(u)
        rows = []
        for j in range(CONV_WIDTH):
            k = CONV_WIDTH - 1 - j
            du = du + w[j:j + 1, :] * _shift_up(dpre, k)
            rows.append(jnp.sum(dpre * _shift_down(u, k), axis=0, keepdims=True))
        du_ref[...] = du.astype(du_ref.dtype)
        rows.append(jnp.zeros((8 - CONV_WIDTH, CONV_COLS), F32))
        dw_ref[...] = jnp.concatenate(rows, axis=0)
        db_ref[...] = jnp.sum(dpre, axis=0, keepdims=True)

    return pl.pallas_call(
        body, grid=(ncols // CONV_COLS,),
        in_specs=[pl.BlockSpec((S, CONV_COLS), lambda j: (0, j + cb0)),
                  pl.BlockSpec((CONV_WIDTH, CONV_COLS), lambda j: (0, j)),
                  pl.BlockSpec((1, CONV_COLS), lambda j: (0, j))]
        + [pl.BlockSpec((S, CONV_COLS), lambda j, lo=starts[i], hi=starts[i + 1]: (0, jnp.clip(j - lo, 0, hi - lo - 1)))
           for i in range(nd)] + [_ANY],
        out_specs=[pl.BlockSpec((S, CONV_COLS), lambda j: (0, j + cb0)),
                   pl.BlockSpec((8, CONV_COLS), lambda j: (0, j)),
                   pl.BlockSpec((1, CONV_COLS), lambda j: (0, j))],
        out_shape=[jax.ShapeDtypeStruct(dproj.shape, dproj.dtype),
                   jax.ShapeDtypeStruct((8, ncols), F32),
                   jax.ShapeDtypeStruct((1, ncols), F32)],
        input_output_aliases={3 + nd: 0},
        compiler_params=_params(("parallel",)), name="conv_bwd")(proj, conv_w, conv_b, *douts, dproj)


def _softplus(x):
    return jnp.maximum(x, 0.0) + jnp.log1p(jnp.exp(-jnp.abs(x)))


def _dot32(a, b, dims=(((1,), (0,)), ((), ()))):
    return lax.dot_general(a, b, dims, precision=HI, preferred_element_type=F32)


_NN = (((1,), (0,)), ((), ()))
_NT = (((1,), (1,)), ((), ()))
_TN = (((0,), (0,)), ((), ()))


def _bf16_dot(a, b, dims):
    return lax.dot_general(a.astype(BF16), b.astype(BF16), dims, preferred_element_type=F32)


@functools.partial(jax.custom_vjp, nondiff_argnums=(2,))
def _dotd(a, b, dims=_NN):
    return _bf16_dot(a, b, dims)


def _dotd_fwd(a, b, dims):
    return _bf16_dot(a, b, dims), (a, b)


def _dotd_bwd(dims, res, ct):
    a, b = res
    if dims == _NN:
        return _bf16_dot(ct, b, _NT), _bf16_dot(a, ct, _TN)
    if dims == _NT:
        return _bf16_dot(ct, b, _NN), _bf16_dot(ct, a, _TN)
    return _bf16_dot(b, ct, _NT), _bf16_dot(a, ct, _NN)


_dotd.defvjp(_dotd_fwd, _dotd_bwd)


PAIRS_PER_GROUP = HEADS_PER_GROUP // 2


def _ssd_chunk(xs, Bm, Cm, z, dtr, dtb, alog, dsk, nw, h):
    L = Bm.shape[0]
    ri = lax.broadcasted_iota(jnp.int32, (L, L), 0)
    ci = lax.broadcasted_iota(jnp.int32, (L, L), 1)
    causal = ri >= ci
    tril = causal.astype(F32)
    first = _first_head(L)
    first1 = _first_head(1)
    CB = _dotd(Cm, Bm, _NT)
    gated, hnew = [], []
    ssq = jnp.zeros((L, 1), F32)
    for pp in range(len(xs)):
        dts, cums, tots, decay = [], [], [], []
        for a in range(2):
            r = 2 * pp + a
            dt = _softplus(dtr[r] + dtb[r])
            dA = dt * (-jnp.exp(alog[r]))
            acs = _dot32(tril, dA)
            cc = jnp.broadcast_to(acs, (L, L))
            decay.append(CB * jnp.exp(jnp.where(causal, cc - cc.T, -1e30)))
            dts.append(dt)
            cums.append(acs)
            tots.append(jnp.sum(dA, axis=0, keepdims=True))
        dt2 = jnp.where(first, dts[0], dts[1])
        acs2 = jnp.where(first, cums[0], cums[1])
        tot2 = jnp.where(first1, tots[0], tots[1])
        dsk2 = jnp.where(first1, dsk[2 * pp], dsk[2 * pp + 1])
        X = xs[pp] * dt2
        y = (jnp.where(first, _dotd(decay[0], X), _dotd(decay[1], X)) + jnp.exp(acs2) * _dotd(Cm, h[pp])
             + dsk2 * xs[pp])
        hnew.append(jnp.exp(tot2) * h[pp] + _dotd(Bm, X * jnp.exp(tot2 - acs2), _TN))
        g = y * (z[pp] * jax.nn.sigmoid(z[pp]))
        ssq = ssq + jnp.sum(g * g, axis=-1, keepdims=True)
        gated.append(g)
    rs = lax.rsqrt(ssq / (len(xs) * LANES) + EPS)
    return [g * rs * nw[pp] for pp, g in enumerate(gated)], hnew


def _ssd_args(xs_ref, b_ref, c_ref, z_ref, dt_ref, dtb_ref, al_ref, dsk_ref, nw_ref, h_ref):
    pairs = range(PAIRS_PER_GROUP)
    heads = range(HEADS_PER_GROUP)
    lanes = lambda ref, pp: ref[:, pp * LANES:(pp + 1) * LANES]
    return ([lanes(xs_ref, pp) for pp in pairs], b_ref[...], c_ref[...], [lanes(z_ref, pp) for pp in pairs],
            [dt_ref[r] for r in heads], [dtb_ref[r] for r in heads], [al_ref[r] for r in heads],
            [dsk_ref[r] for r in heads], [lanes(nw_ref, pp) for pp in pairs], [h_ref[pp] for pp in pairs])


def _ssd_specs(rev):
    H, N, L = HEADS_PER_GROUP, SSM_STATE, CHUNK
    gw = H * HEAD_DIM
    return dict(
        cols=lambda col0: pl.BlockSpec((L, gw), lambda g, c: (rev(c), col0 // gw + g)),
        bc=lambda first_block: pl.BlockSpec((L, N), lambda g, c: (rev(c), first_block + g)),
        dt=pl.BlockSpec((H, L, 1), lambda g, c: (g, rev(c), 0)),
        scal=pl.BlockSpec((H, 1, 1), lambda g, c: (g, 0, 0)),
        nw=pl.BlockSpec((1, gw), lambda g, c: (0, g)),
        hs=pl.BlockSpec((None, PAIRS_PER_GROUP, N, LANES), lambda g, c: (rev(c), g, 0, 0)),
        b_block=SSM_INNER // N,
    )


def _ssd_fwd(xbc, proj, dt_hm, dtb, alog, dsk, nw):
    S = xbc.shape[0]
    N, L = SSM_STATE, CHUNK
    nc = S // L
    sp = _ssd_specs(lambda c: c)

    def body(xs_ref, b_ref, c_ref, z_ref, dt_ref, dtb_ref, al_ref, dsk_ref, nw_ref, y_ref, hs_ref, h_ref):
        @pl.when(pl.program_id(1) == 0)
        def _():
            h_ref[...] = jnp.zeros_like(h_ref)

        hs_ref[...] = h_ref[...]
        out, hnew = _ssd_chunk(*_ssd_args(xs_ref, b_ref, c_ref, z_ref, dt_ref, dtb_ref, al_ref, dsk_ref, nw_ref, h_ref))
        for pp in range(PAIRS_PER_GROUP):
            y_ref[:, pp * LANES:(pp + 1) * LANES] = out[pp].astype(y_ref.dtype)
            h_ref[pp] = hnew[pp]

    return pl.pallas_call(
        body, grid=(SSM_GROUPS, nc),
        in_specs=[sp["cols"](0), sp["bc"](sp["b_block"]), sp["bc"](sp["b_block"] + SSM_GROUPS), sp["cols"](COL_Z),
                  sp["dt"], sp["scal"], sp["scal"], sp["scal"], sp["nw"]],
        out_specs=[sp["cols"](0), sp["hs"]],
        out_shape=[jax.ShapeDtypeStruct((S, MIX_WIDTH), BF16),
                   jax.ShapeDtypeStruct((nc, SSM_HEADS // 2, N, LANES), F32)],
        scratch_shapes=[pltpu.VMEM((PAIRS_PER_GROUP, N, LANES), F32)],
        compiler_params=_params(("parallel", "arbitrary")), name="ssd_fwd",
    )(xbc, xbc, xbc, proj, dt_hm, dtb, alog, dsk, nw)


def _ssd_bwd(xbc, proj, dt_hm, dtb, alog, dsk, nw, hs, dmixed, dproj):
    S = xbc.shape[0]
    N, L = SSM_STATE, CHUNK
    nc = S // L
    sp = _ssd_specs(lambda c: nc - 1 - c)

    def body(xs_ref, b_ref, c_ref, z_ref, dt_ref, dtb_ref, al_ref, dsk_ref, nw_ref, hs_ref, dy_ref, buf_ref,
             dxs_ref, dz_ref, db_ref, dc_ref, ddt_ref, ddtb_ref, dal_ref, ddsk_ref, dnw_ref, dh_ref):
        @pl.when(pl.program_id(1) == 0)
        def _():
            dh_ref[...] = jnp.zeros_like(dh_ref)
            ddtb_ref[...] = jnp.zeros_like(ddtb_ref)
            dal_ref[...] = jnp.zeros_like(dal_ref)
            ddsk_ref[...] = jnp.zeros_like(ddsk_ref)
            dnw_ref[...] = jnp.zeros_like(dnw_ref)

        pairs = range(PAIRS_PER_GROUP)
        lanes = lambda pp: slice(pp * LANES, (pp + 1) * LANES)
        _, vjp = jax.vjp(_ssd_chunk, *_ssd_args(xs_ref, b_ref, c_ref, z_ref, dt_ref, dtb_ref, al_ref, dsk_ref, nw_ref,
                                                hs_ref))
        dxs, dB, dC, dz, ddt, ddtb, dal, ddsk, dnw, dh = vjp(([dy_ref[:, lanes(pp)] for pp in pairs],
                                                              [dh_ref[pp] for pp in pairs]))
        db_ref[...] = dB
        dc_ref[...] = dC
        for pp in pairs:
            dxs_ref[:, lanes(pp)] = dxs[pp]
            dz_ref[:, lanes(pp)] = dz[pp].astype(dz_ref.dtype)
            dnw_ref[:, lanes(pp)] += dnw[pp]
            dh_ref[pp] = dh[pp]
        for r in range(HEADS_PER_GROUP):
            ddt_ref[r] = ddt[r]
            ddtb_ref[r] += ddtb[r]
            dal_ref[r] += dal[r]
            ddsk_ref[r] += ddsk[r]

    bc_out = pl.BlockSpec((L, N), lambda g, c: (nc - 1 - c, g))
    return pl.pallas_call(
        body, grid=(SSM_GROUPS, nc),
        in_specs=[sp["cols"](0), sp["bc"](sp["b_block"]), sp["bc"](sp["b_block"] + SSM_GROUPS), sp["cols"](COL_Z),
                  sp["dt"], sp["scal"], sp["scal"], sp["scal"], sp["nw"], sp["hs"], sp["cols"](0), _ANY],
        out_specs=[sp["cols"](0), sp["cols"](COL_Z), bc_out, bc_out, sp["dt"], sp["scal"], sp["scal"], sp["scal"],
                   sp["nw"]],
        input_output_aliases={11: 1},
        out_shape=[jax.ShapeDtypeStruct((S, SSM_INNER), F32), jax.ShapeDtypeStruct(dproj.shape, dproj.dtype),
                   jax.ShapeDtypeStruct((S, SSM_GROUPS * N), F32), jax.ShapeDtypeStruct((S, SSM_GROUPS * N), F32),
                   jax.ShapeDtypeStruct((SSM_HEADS, S, 1), F32),
                   jax.ShapeDtypeStruct((SSM_HEADS, 1, 1), F32), jax.ShapeDtypeStruct((SSM_HEADS, 1, 1), F32),
                   jax.ShapeDtypeStruct((SSM_HEADS, 1, 1), F32), jax.ShapeDtypeStruct((1, SSM_INNER), F32)],
        scratch_shapes=[pltpu.VMEM((PAIRS_PER_GROUP, N, LANES), F32)],
        compiler_params=_params(("parallel", "arbitrary")), name="ssd_bwd",
    )(xbc, xbc, xbc, proj, dt_hm, dtb, alog, dsk, nw, hs, dmixed, dproj)


ATTN_SCALE = HEAD_DIM ** -0.5
PREP_COLS = 512


def _first_head(rows):
    return lax.broadcasted_iota(jnp.int32, (rows, LANES), 1) < HEAD_DIM


def _pair_norm(x, g2, scale):
    first = _first_head(x.shape[0])
    sq = x * x
    ms0 = jnp.sum(jnp.where(first, sq, 0.0), axis=-1, keepdims=True) * (1.0 / HEAD_DIM)
    ms1 = jnp.sum(jnp.where(first, 0.0, sq), axis=-1, keepdims=True) * (1.0 / HEAD_DIM)
    r = jnp.where(first, lax.rsqrt(ms0 + EPS), lax.rsqrt(ms1 + EPS))
    return x * r * g2 * scale


def _qk_prep_fwd(proj, gq2, gk2):
    S = proj.shape[0]
    tq = _pick(S, (512, 256))

    def body(q_ref, k_ref, v_ref, gq_ref, gk_ref, qo_ref, ko_ref, vo_ref):
        for b in range(PREP_COLS // LANES):
            pair = slice(b * LANES, (b + 1) * LANES)
            qo_ref[:, pair] = _pair_norm(q_ref[:, pair], gq_ref[...], ATTN_SCALE).astype(BF16)
            ko_ref[:, pair] = _pair_norm(k_ref[:, pair], gk_ref[...], 1.0).astype(BF16)
        vo_ref[...] = v_ref[...].astype(BF16)

    col = lambda c0: pl.BlockSpec((tq, PREP_COLS), lambda h, i: (i, c0 // PREP_COLS + h))
    blk = pl.BlockSpec((tq, PREP_COLS), lambda h, i: (i, h))
    vec = pl.BlockSpec((1, LANES), lambda h, i: (0, 0))
    return pl.pallas_call(
        body, grid=(ATTN_WIDTH // PREP_COLS, S // tq), in_specs=[col(COL_Q), col(COL_K), col(COL_V), vec, vec],
        out_specs=[blk, blk, blk], out_shape=[jax.ShapeDtypeStruct((S, ATTN_WIDTH), BF16)] * 3,
        compiler_params=_params(("parallel", "parallel")), name="qk_prep_fwd")(proj, proj, proj, gq2, gk2)


def _pair_norm_bwd(proj, col0, g2, scale, dn, dproj, name):
    S = proj.shape[0]
    tq = _pick(S, (512, 256))

    def body(u_ref, g_ref, dn_ref, buf_ref, du_ref, dg_ref):
        @pl.when((pl.program_id(0) == 0) & (pl.program_id(1) == 0))
        def _():
            dg_ref[...] = jnp.zeros_like(dg_ref)

        for b in range(PREP_COLS // LANES):
            pair = slice(b * LANES, (b + 1) * LANES)
            _, vjp = jax.vjp(lambda u, g: _pair_norm(u, g, scale), u_ref[:, pair], g_ref[...])
            du, dg = vjp(dn_ref[:, pair])
            du_ref[:, pair] = du.astype(du_ref.dtype)
            dg_ref[...] += dg

    ublk = pl.BlockSpec((tq, PREP_COLS), lambda h, i: (i, col0 // PREP_COLS + h))
    blk = pl.BlockSpec((tq, PREP_COLS), lambda h, i: (i, h))
    vec = pl.BlockSpec((1, LANES), lambda h, i: (0, 0))
    return pl.pallas_call(
        body, grid=(ATTN_WIDTH // PREP_COLS, S // tq), in_specs=[ublk, vec, blk, _ANY], out_specs=[ublk, vec],
        out_shape=[jax.ShapeDtypeStruct(dproj.shape, dproj.dtype), jax.ShapeDtypeStruct((1, LANES), F32)],
        input_output_aliases={3: 0},
        compiler_params=_params(("arbitrary", "arbitrary")), name=name)(proj, g2, dn, dproj)


def _logf_cumsum_fwd(f_raw, f_bias):
    S, Hh = f_raw.shape
    L = CHUNK

    def body(f_ref, b_ref, o_ref, wide_ref):
        ri = lax.broadcasted_iota(jnp.int32, (L, L), 0)
        ci = lax.broadcasted_iota(jnp.int32, (L, L), 1)
        tril = (ri >= ci).astype(F32)
        carry = jnp.zeros((1, Hh), F32)
        for c in range(S // L):
            rows = slice(c * L, (c + 1) * L)
            lf = -_softplus(-(f_ref[rows, :] + b_ref[...]))
            cum = _dot32(tril, lf) + carry
            o_ref[rows, :] = cum
            for h in range(Hh):
                wide_ref[rows, h * HEAD_DIM:(h + 1) * HEAD_DIM] = jnp.broadcast_to(cum[:, h:h + 1], (L, HEAD_DIM))
            carry = cum[L - 1:L, :]

    return pl.pallas_call(
        body, out_shape=[jax.ShapeDtypeStruct((S, Hh), F32), jax.ShapeDtypeStruct((S, Hh * HEAD_DIM), F32)],
        name="logf_cumsum_fwd")(f_raw, f_bias)


def _logf_cumsum_bwd(f_raw, f_bias, dcum):
    S, Hh = f_raw.shape
    L = CHUNK

    def body(f_ref, b_ref, d_ref, df_ref, db_ref):
        ri = lax.broadcasted_iota(jnp.int32, (L, L), 0)
        ci = lax.broadcasted_iota(jnp.int32, (L, L), 1)
        triu = (ri <= ci).astype(F32)
        carry = jnp.zeros((1, Hh), F32)
        db = jnp.zeros((1, Hh), F32)
        for c in reversed(range(S // L)):
            suf = _dot32(triu, d_ref[c * L:(c + 1) * L, :]) + carry
            df = suf * jax.nn.sigmoid(-(f_ref[c * L:(c + 1) * L, :] + b_ref[...]))
            df_ref[c * L:(c + 1) * L, :] = df
            db = db + jnp.sum(df, axis=0, keepdims=True)
            carry = suf[0:1, :]
        db_ref[...] = db

    return pl.pallas_call(
        body, out_shape=[jax.ShapeDtypeStruct((S, Hh), F32), jax.ShapeDtypeStruct((1, Hh), F32)],
        name="logf_cumsum_bwd")(f_raw, f_bias, dcum)


def _mxu(a, b, dims=(((1,), (0,)), ((), ()))):
    return lax.dot_general(a, b, dims, preferred_element_type=F32)


def _flash_fwd(qs, kn, vb, cq, ck, mixed):
    S, W = qs.shape
    tq = tk = _pick(S, (512, 256))
    nmask = max(tq // tk, 1)

    def body(q_ref, k_ref, v_ref, cq_ref, ck_ref, buf_ref, o_ref, of_ref, lse_ref):
        i = pl.program_id(1)
        first = _first_head(tq)
        q2 = q_ref[...]
        zero = jnp.zeros_like(q2)
        qa = (jnp.where(first, q2, zero), jnp.where(first, zero, q2))
        cqa = (cq_ref[:, 0:1], cq_ref[:, HEAD_DIM:HEAD_DIM + 1])
        row0 = i * tq

        def step(j, carry, masked):
            ms, ls, acc, rem = carry
            off = pl.multiple_of(j * tk, tk)
            k = k_ref[pl.ds(off, tk), :]
            v = v_ref[pl.ds(off, tk), :]
            new_m, new_l, alphas, pvs, prs = [], [], [], [], []
            for a in range(2):
                s = _mxu(qa[a], k, _NT) + cqa[a] - ck_ref[a, :, pl.ds(off, tk)]
                if masked:
                    ri = lax.broadcasted_iota(jnp.int32, (tq, tk), 0) + row0
                    ci = lax.broadcasted_iota(jnp.int32, (tq, tk), 1) + off
                    s = jnp.where(ri >= ci, s, -1e30)
                m_new = jnp.maximum(ms[a], jnp.max(s, axis=-1, keepdims=True))
                alpha = jnp.exp(ms[a] - m_new)
                p = jnp.exp(s - m_new)
                new_l.append(alpha * ls[a] + jnp.sum(p, axis=-1, keepdims=True))
                new_m.append(m_new)
                alphas.append(alpha)
                p_hi = p.astype(BF16)
                pvs.append(_mxu(p_hi, v))
                prs.append(_mxu((p - p_hi.astype(F32)).astype(BF16), v))
            al = jnp.where(first, alphas[0], alphas[1])
            acc = al * acc + jnp.where(first, pvs[0], pvs[1])
            rem = al * rem + jnp.where(first, prs[0], prs[1])
            return tuple(new_m), tuple(new_l), acc, rem

        neg = jnp.full((tq, 1), -1e30, F32)
        z1 = jnp.zeros((tq, 1), F32)
        z2 = jnp.zeros((tq, LANES), F32)
        carry = ((neg, neg), (z1, z1), z2, z2)
        n_full = (i * tq) // tk
        carry = lax.fori_loop(0, n_full, lambda j, c: step(j, c, False), carry)
        for jj in range(nmask):
            carry = step(n_full + jj, carry, True)
        ms, ls, acc, rem = carry
        linv = jnp.where(first, 1.0 / ls[0], 1.0 / ls[1])
        o_ref[...] = (acc * linv).astype(o_ref.dtype)
        of_ref[...] = (acc + rem) * linv
        lse_ref[...] = jnp.where(first, ms[0] + jnp.log(ls[0]), ms[1] + jnp.log(ls[1]))

    qblk = pl.BlockSpec((tq, LANES), lambda h, i: (i, h))
    full = pl.BlockSpec((S, LANES), lambda h, i: (0, h))
    return pl.pallas_call(
        body, grid=(W // LANES, S // tq),
        in_specs=[qblk, full, full, qblk, pl.BlockSpec((2, 1, S), lambda h, i: (h, 0, 0)), _ANY],
        out_specs=[pl.BlockSpec((tq, LANES), lambda h, i: (i, SSM_INNER // LANES + h)), qblk, qblk],
        out_shape=[jax.ShapeDtypeStruct(mixed.shape, mixed.dtype), jax.ShapeDtypeStruct((S, W), F32),
                   jax.ShapeDtypeStruct((S, W), F32)],
        input_output_aliases={5: 0},
        compiler_params=_params(("parallel", "parallel")), name="flash_fwd")(qs, kn, vb, cq, ck, mixed)


def _flash_bwd(qs, kn, vb, cq, ck, o_fine, do, do_col0, lse):
    S, W = qs.shape
    tq = tk = _pick(S, (512, 256))
    nq = S // tq
    nmask = max(tk // tq, 1)

    def body(q_ref, k_ref, v_ref, cq_ref, ck_ref, of_ref, do_ref, lse_ref, dq_ref, dk_ref, dv_ref, dck_ref):
        j = pl.program_id(1)

        @pl.when(j == 0)
        def _():
            dq_ref[...] = jnp.zeros_like(dq_ref)

        firstk = _first_head(tk)
        firstq = _first_head(tq)
        k2 = k_ref[...]
        v2 = v_ref[...]
        zk = jnp.zeros_like(k2)
        ka = (jnp.where(firstk, k2, zk), jnp.where(firstk, zk, k2))
        va = (jnp.where(firstk, v2, zk), jnp.where(firstk, zk, v2))
        cka = (ck_ref[0], ck_ref[1])
        col0 = j * tk

        def step(i, carry, masked):
            dk, dv, dck0, dck1 = carry
            dcks = [dck0, dck1]
            off = pl.multiple_of(i * tq, tq)
            rows = pl.ds(off, tq)
            q2 = q_ref[rows, :]
            dob = do_ref[rows, :].astype(BF16)
            prod = dob.astype(F32) * of_ref[rows, :]
            dkp, dvp, dqp = [], [], []
            for a in range(2):
                lane = pl.ds(a * HEAD_DIM, 1)
                s = _mxu(q2, ka[a], _NT) + cq_ref[rows, lane] - cka[a]
                if masked:
                    ri = lax.broadcasted_iota(jnp.int32, (tq, tk), 0) + off
                    ci = lax.broadcasted_iota(jnp.int32, (tq, tk), 1) + col0
                    s = jnp.where(ri >= ci, s, -1e30)
                p = jnp.exp(s - lse_ref[rows, lane])
                dp = _mxu(dob, va[a], _NT)
                own = jnp.where(firstq, prod, 0.0) if a == 0 else jnp.where(firstq, 0.0, prod)
                ds = p * (dp - jnp.sum(own, axis=-1, keepdims=True))
                dsb = ds.astype(BF16)
                dvp.append(_mxu(p.astype(BF16), dob, _TN))
                dkp.append(_mxu(dsb, q2, _TN))
                dqp.append(_mxu(dsb, k2))
                dcks[a] = dcks[a] - jnp.sum(ds, axis=0, keepdims=True)
            dq_ref[rows, :] += jnp.where(firstq, dqp[0], dqp[1])
            dk = dk + jnp.where(firstk, dkp[0], dkp[1])
            dv = dv + jnp.where(firstk, dvp[0], dvp[1])
            return dk, dv, dcks[0], dcks[1]

        z2 = jnp.zeros((tk, LANES), F32)
        z1 = jnp.zeros((1, tk), F32)
        carry = (z2, z2, z1, z1)
        i0 = (j * tk) // tq
        for ii in range(nmask):
            carry = step(i0 + ii, carry, True)
        dk, dv, dck0, dck1 = lax.fori_loop(i0 + nmask, nq, lambda i, c: step(i, c, False), carry)
        dk_ref[...] = dk
        dv_ref[...] = dv.astype(dv_ref.dtype)
        dck_ref[0] = dck0
        dck_ref[1] = dck1

    kblk = pl.BlockSpec((tk, LANES), lambda h, j: (j, h))
    full = pl.BlockSpec((S, LANES), lambda h, j: (0, h))
    dofull = pl.BlockSpec((S, LANES), lambda h, j: (0, do_col0 // LANES + h))
    rowt = pl.BlockSpec((2, 1, tk), lambda h, j: (h, 0, j))
    dvblk = pl.BlockSpec((tk, LANES), lambda h, j: (j, COL_V // LANES + h))
    return pl.pallas_call(
        body, grid=(W // LANES, S // tk),
        in_specs=[full, kblk, kblk, full, rowt, full, dofull, full],
        out_specs=[full, kblk, dvblk, rowt],
        out_shape=[jax.ShapeDtypeStruct((S, W), F32), jax.ShapeDtypeStruct((S, W), F32),
                   jax.ShapeDtypeStruct((S, IN_COLS_PAD), BF16), jax.ShapeDtypeStruct((2 * (W // LANES), 1, S), F32)],
        compiler_params=_params(("parallel", "arbitrary")), name="flash_bwd")(qs, kn, vb, cq, ck, o_fine, do, lse)


XATTN_SCALE = XATTN_DIM ** -0.5


def _xq_norm(q, g):
    return _rms(q, g) * XATTN_SCALE


def _xattn_fwd(xq, kv, gq, gk):
    S = xq.shape[0]
    Mm = kv.shape[0]
    Dh = XATTN_DIM
    tq = _pick(S, (512, 256))

    def body(q_ref, k_ref, v_ref, gq_ref, gk_ref, o_ref):
        qn = _xq_norm(q_ref[...], gq_ref[...]).astype(BF16)
        kn = _rms(k_ref[...], gk_ref[...]).astype(BF16)
        s = _mxu(qn, kn, _NT)
        m = jnp.max(s, axis=-1, keepdims=True)
        p = jnp.exp(s - m)
        l = jnp.sum(p, axis=-1, keepdims=True)
        o_ref[...] = (_mxu(p.astype(BF16), v_ref[...].astype(BF16)) / l).astype(o_ref.dtype)

    vec = pl.BlockSpec((1, Dh), lambda h, i: (0, 0))
    return pl.pallas_call(
        body, grid=(XATTN_HEADS, S // tq),
        in_specs=[pl.BlockSpec((tq, Dh), lambda h, i: (i, h)), pl.BlockSpec((Mm, Dh), lambda h, i: (0, h)),
                  pl.BlockSpec((Mm, Dh), lambda h, i: (0, XATTN_HEADS + h)), vec, vec],
        out_specs=pl.BlockSpec((tq, Dh), lambda h, i: (i, h)),
        out_shape=jax.ShapeDtypeStruct((S, XATTN_HEADS * Dh), BF16),
        compiler_params=_params(("parallel", "parallel")), name="xattn_fwd")(xq, kv, kv, gq, gk)


def _xattn_bwd(xq, kv, gq, gk, do):
    S = xq.shape[0]
    Mm = kv.shape[0]
    Dh = XATTN_DIM
    tq = _pick(S, (512, 256))
    nq = S // tq

    def body(q_ref, k_ref, v_ref, gq_ref, gk_ref, do_ref, dq_ref, dk_ref, dv_ref, dgq_ref, dgk_ref, dkn_acc, dv_acc):
        h = pl.program_id(0)
        i = pl.program_id(1)

        @pl.when((h == 0) & (i == 0))
        def _():
            dgq_ref[...] = jnp.zeros_like(dgq_ref)
            dgk_ref[...] = jnp.zeros_like(dgk_ref)

        @pl.when(i == 0)
        def _():
            dkn_acc[...] = jnp.zeros_like(dkn_acc)
            dv_acc[...] = jnp.zeros_like(dv_acc)

        qn32, vq = jax.vjp(_xq_norm, q_ref[...], gq_ref[...])
        kn32, vk = jax.vjp(_rms, k_ref[...], gk_ref[...])
        qn = qn32.astype(BF16)
        kn = kn32.astype(BF16)
        vb = v_ref[...].astype(BF16)
        s = _mxu(qn, kn, _NT)
        m = jnp.max(s, axis=-1, keepdims=True)
        p = jnp.exp(s - m)
        p = p / jnp.sum(p, axis=-1, keepdims=True)
        dob = do_ref[...].astype(BF16)
        dp = _mxu(dob, vb, _NT)
        delta = jnp.sum(p * dp, axis=-1, keepdims=True)
        ds = (p * (dp - delta)).astype(BF16)
        dv_acc[...] += _mxu(p.astype(BF16), dob, _TN)
        dkn_acc[...] += _mxu(ds, qn, _TN)
        dq, dgq = vq(_mxu(ds, kn))
        dq_ref[...] = dq.astype(dq_ref.dtype)
        dgq_ref[...] += dgq

        @pl.when(i == nq - 1)
        def _():
            dk, dgk = vk(dkn_acc[...])
            dk_ref[...] = dk.astype(dk_ref.dtype)
            dv_ref[...] = dv_acc[...].astype(dv_ref.dtype)
            dgk_ref[...] += dgk

    vec = pl.BlockSpec((1, Dh), lambda h, i: (0, 0))
    qblk = pl.BlockSpec((tq, Dh), lambda h, i: (i, h))
    kblk = pl.BlockSpec((Mm, Dh), lambda h, i: (0, h))
    vblk = pl.BlockSpec((Mm, Dh), lambda h, i: (0, XATTN_HEADS + h))
    return pl.pallas_call(
        body, grid=(XATTN_HEADS, nq),
        in_specs=[qblk, kblk, vblk, vec, vec, qblk],
        out_specs=[qblk, kblk, kblk, vec, vec],
        out_shape=[jax.ShapeDtypeStruct((S, XATTN_HEADS * Dh), BF16),
                   jax.ShapeDtypeStruct((Mm, XATTN_HEADS * Dh), BF16),
                   jax.ShapeDtypeStruct((Mm, XATTN_HEADS * Dh), BF16),
                   jax.ShapeDtypeStruct((1, Dh), F32), jax.ShapeDtypeStruct((1, Dh), F32)],
        scratch_shapes=[pltpu.VMEM((Mm, Dh), F32), pltpu.VMEM((Mm, Dh), F32)],
        compiler_params=_params(("arbitrary", "arbitrary")), name="xattn_bwd")(xq, kv, kv, gq, gk, do)


def _loss_head(y, target):
    S, D = y.shape
    tr = _pick(S, (512, 256))

    def body(y_ref, t_ref, dy_ref, loss_ref):
        @pl.when(pl.program_id(0) == 0)
        def _():
            loss_ref[...] = jnp.zeros_like(loss_ref)

        err = y_ref[...] - t_ref[...]
        dy_ref[...] = err * (1.0 / D)
        loss_ref[...] += jnp.sum(err * err) * (0.5 / D)

    row = pl.BlockSpec((tr, D), lambda i: (i, 0))
    return pl.pallas_call(
        body, grid=(S // tr,), in_specs=[row, row],
        out_specs=[row, pl.BlockSpec((1, LANES), lambda i: (0, 0))],
        out_shape=[jax.ShapeDtypeStruct((S, D), F32), jax.ShapeDtypeStruct((1, LANES), F32)],
        compiler_params=_params(("arbitrary",)), name="loss_head")(y, target)


def _row_tile(R, C):
    for tr in (1024, 512, 256, 128, 64, 32, 16, 8):
        if R % tr == 0 and tr * C * 4 <= (2 << 20):
            return tr
    return R


def _chip_sum(own, from_chips, name):
    R, C = own.shape
    tr = _row_tile(R, C)

    def body(own_ref, a_ref, b_ref, c_ref, o_ref):
        total = ((own_ref[...].astype(F32) + a_ref[...].astype(F32)) + b_ref[...].astype(F32)) + c_ref[...].astype(F32)
        o_ref[...] = total.astype(o_ref.dtype)

    blk = pl.BlockSpec((tr, C), lambda i: (i, 0))
    slab = lambda s: pl.BlockSpec((None, tr, C), lambda i: (s, i, 0))
    return pl.pallas_call(
        body, grid=(R // tr,), in_specs=[blk, slab(0), slab(1), slab(2)], out_specs=blk,
        out_shape=jax.ShapeDtypeStruct((R, C), BF16),
        compiler_params=_params(("parallel",)), name=name)(own, from_chips, from_chips, from_chips)


def _adamw(w, g_mine, g_sibling, m, v, name):
    R, C = w.shape
    tr = _row_tile(R, C)
    c1 = 1.0 - ADAM_B1 ** ADAM_STEP
    c2 = 1.0 - ADAM_B2 ** ADAM_STEP

    def body(w_ref, ga_ref, gb_ref, m_ref, v_ref, g_ref, d_ref, mo_ref, vo_ref):
        g_t = ga_ref[...].astype(F32) + gb_ref[...].astype(F32)
        m_new = ADAM_B1 * m_ref[...] + (1.0 - ADAM_B1) * g_t
        v_new = ADAM_B2 * v_ref[...] + (1.0 - ADAM_B2) * (g_t * g_t)
        g_ref[...] = g_t
        d_ref[...] = -ADAM_LR * ((m_new / c1) / (jnp.sqrt(v_new / c2) + ADAM_EPS) + ADAM_WD * w_ref[...])
        mo_ref[...] = m_new
        vo_ref[...] = v_new

    blk = pl.BlockSpec((tr, C), lambda i: (i, 0))
    return pl.pallas_call(
        body, grid=(R // tr,), in_specs=[blk] * 5, out_specs=[blk] * 4,
        out_shape=[jax.ShapeDtypeStruct((R, C), F32)] * 4,
        compiler_params=_params(("parallel",)), name=name)(w, g_mine, g_sibling, m, v)


SSM_INNER = SSM_HEADS * HEAD_DIM
CONV_DIM = SSM_INNER + 2 * SSM_GROUPS * SSM_STATE
ATTN_WIDTH = ATTN_HEADS * HEAD_DIM
MIX_WIDTH = SSM_INNER + ATTN_WIDTH
COL_Z = 0
COL_XBC = COL_Z + SSM_INNER
COL_Q = COL_XBC + CONV_DIM
COL_K = COL_Q + ATTN_WIDTH
COL_V = COL_K + ATTN_WIDTH
COL_DT = COL_V + ATTN_WIDTH
COL_F = COL_DT + SSM_HEADS
IN_COLS = COL_F + ATTN_HEADS
IN_COLS_PAD = -(-IN_COLS // LANES) * LANES
REF_COL_DT = COL_Q
SHARD_COLS = IN_COLS // N_CHIPS
_COL_RANGES = ((0, REF_COL_DT, 0), (REF_COL_DT + SSM_HEADS, COL_F, COL_Q), (REF_COL_DT, REF_COL_DT + SSM_HEADS, COL_DT),
               (COL_F, IN_COLS, COL_F))


def _w_in_from_shards(g):
    parts = []
    for lo, hi, _ in _COL_RANGES:
        while lo < hi:
            j = lo // SHARD_COLS
            end = min(hi, (j + 1) * SHARD_COLS)
            parts.append(g[j][:, lo - j * SHARD_COLS:end - j * SHARD_COLS])
            lo = end
    parts.append(jnp.zeros((g.shape[1], IN_COLS_PAD - IN_COLS), g.dtype))
    return jnp.concatenate(parts, axis=1)


def _w_in_to_shards(w):
    shards = []
    for j in range(N_CHIPS):
        parts = []
        for lo, hi, here in sorted(_COL_RANGES):
            a, b = max(lo, j * SHARD_COLS), min(hi, (j + 1) * SHARD_COLS)
            if a < b:
                parts.append(w[:, here + a - lo:here + b - lo])
        shards.append(jnp.concatenate(parts, axis=1))
    return jnp.stack(shards)


def _add_residual(acc, res):
    return (res + acc,)


def _relu2(acc):
    r = jnp.maximum(acc, 0.0)
    return acc, r * r


def _relu2_bwd(acc, a):
    return (acc * (2.0 * jnp.maximum(a.astype(F32), 0.0)),)


def _layer_fwd_bwd(x, mem, target, w_in, p, late_weights, send_late_grads, send_w_in_grad):
    S = x.shape[0]
    hd3 = lambda a: a.reshape(SSM_HEADS, 1, 1)

    h1 = _rmsnorm_fwd(x, p["g_mix"], "norm_mix")
    proj = _mm(h1, w_in, "nn", "in_proj")
    xbc = _conv_fwd(proj, COL_XBC, CONV_DIM, p["conv_w"], p["conv_b"])
    dt_hm = proj[:, COL_DT:COL_DT + SSM_HEADS].T[:, :, None]
    ssd_par = (hd3(p["dt_bias"]), hd3(p["a_log"]), hd3(p["d_skip"]), p["ssm_norm_w"])
    mixed, hs = _ssd_fwd(xbc, proj, dt_hm, *ssd_par)
    f_raw = proj[:, COL_F:COL_F + ATTN_HEADS]
    gq2 = jnp.tile(p["g_q"], (1, 2))
    gk2 = jnp.tile(p["g_k"], (1, 2))
    qs, kn, vb = _qk_prep_fwd(proj, gq2, gk2)
    cum, cq = _logf_cumsum_fwd(f_raw, p["f_bias"])
    ck = cum.T[:, None, :]
    mixed, o_fine, lse = _flash_fwd(qs, kn, vb, cq, ck, mixed)
    W = late_weights((mixed,))
    x1 = _mm(mixed, W["w_out"], "nn", "out_proj", epilogue=_add_residual, extras=(x,))
    h2 = _rmsnorm_fwd(x1, p["g_xattn"], "norm_xattn")
    mem_n = _rmsnorm_fwd(mem, p["g_mem"], "norm_mem")
    xq = _mm(h2, W["xq_w"], "nn", "xq_proj")
    kv = _mm(mem_n, W["xkv_w"], "nn", "xkv_proj", b_chunks=N_CHIPS)
    xo = _xattn_fwd(xq, kv, p["xg_q"], p["xg_k"])
    x2 = _mm(xo, W["xo_w"], "nn", "xo_proj", epilogue=_add_residual, extras=(x1,))
    h3 = _rmsnorm_fwd(x2, p["g_mlp"], "norm_mlp")
    a, act = _mm(h3, W["w_up"], "nn", "mlp_up", out_dtypes=(BF16, BF16), epilogue=_relu2, b_chunks=N_CHIPS)
    x3 = _mm(act, W["w_down"], "nn", "mlp_down", epilogue=_add_residual, extras=(x2,))
    dy, loss_row = _loss_head(x3, target)

    gW, gp = {}, {}
    da = _mm(dy, W["w_down"], "nt", "d_act", out_dtypes=(BF16,), epilogue=_relu2_bwd, extras=(a,))
    gW["w_down"] = _mm(act, dy, "tn", "g_w_down", out_dtypes=(BF16,))
    gW["w_up"] = _mm(h3, da, "tn", "g_w_up", out_dtypes=(BF16,), out_chunks=N_CHIPS)
    dh3 = _mm(da, W["w_up"], "nt", "d_h3", b_chunks=N_CHIPS)
    dx2, gp["g_mlp"] = _rmsnorm_bwd(x2, p["g_mlp"], dh3, dy, "norm_mlp_bwd")
    dxo = _mm(dx2, W["xo_w"], "nt", "d_xo", out_dtypes=(BF16,))
    gW["xo_w"] = _mm(xo, dx2, "tn", "g_xo_w", out_dtypes=(BF16,))
    dxq, dk_x, dv_x, gp["xg_q"], gp["xg_k"] = _xattn_bwd(xq, kv, p["xg_q"], p["xg_k"], dxo)
    dkv = jnp.concatenate([dk_x, dv_x], axis=-1)
    gW["xq_w"] = _mm(h2, dxq, "tn", "g_xq_w", out_dtypes=(BF16,))
    dh2 = _mm(dxq, W["xq_w"], "nt", "d_h2")
    gW["xkv_w"] = _mm(mem_n, dkv, "tn", "g_xkv_w", out_dtypes=(BF16,), out_chunks=N_CHIPS)
    dmem_n = _mm(dkv, W["xkv_w"], "nt", "d_mem_n", b_chunks=N_CHIPS)
    _, gp["g_mem"] = _rmsnorm_bwd(mem, p["g_mem"], dmem_n, None, "norm_mem_bwd")
    dx1, gp["g_xattn"] = _rmsnorm_bwd(x1, p["g_xattn"], dh2, dx2, "norm_xattn_bwd")
    dmixed = _mm(dx1, W["w_out"], "nt", "d_mixed")
    gW["w_out"] = _mm(mixed, dx1, "tn", "g_w_out", out_dtypes=(BF16,))
    token = send_late_grads(gW)
    dqs, dkn, dproj, dck = _flash_bwd(qs, kn, vb, cq, ck + token[:1, :1], o_fine, dmixed, SSM_INNER, lse)
    dproj, dgq2 = _pair_norm_bwd(proj, COL_Q, gq2, ATTN_SCALE, dqs, dproj, "q_norm_bwd")
    dproj, dgk2 = _pair_norm_bwd(proj, COL_K, gk2, 1.0, dkn, dproj, "k_norm_bwd")
    gp["g_q"] = dgq2[:, :HEAD_DIM] + dgq2[:, HEAD_DIM:]
    gp["g_k"] = dgk2[:, :HEAD_DIM] + dgk2[:, HEAD_DIM:]
    df, gp["f_bias"] = _logf_cumsum_bwd(f_raw, p["f_bias"], dck[:, 0, :].T)
    dxs, dproj, dB, dC, ddt, ddtb, dalog, ddsk, gp["ssm_norm_w"] = _ssd_bwd(xbc, proj, dt_hm, *ssd_par, hs, dmixed, dproj)
    gp["dt_bias"] = ddtb.reshape(1, SSM_HEADS)
    gp["a_log"] = dalog.reshape(1, SSM_HEADS)
    gp["d_skip"] = ddsk.reshape(1, SSM_HEADS)
    dproj, dconv_w, gp["conv_b"] = _conv_bwd(proj, COL_XBC, CONV_DIM, p["conv_w"], p["conv_b"], (dxs, dB, dC), dproj)
    gp["conv_w"] = dconv_w[:CONV_WIDTH]
    tail = jnp.concatenate([ddt[:, :, 0].T, df, jnp.zeros((S, IN_COLS_PAD - IN_COLS), F32)], axis=-1).astype(BF16)
    dproj = lax.dynamic_update_slice(dproj, tail, (0, COL_DT))
    token = send_w_in_grad(_mm(h1, dproj, "tn", "g_w_in", out_dtypes=(BF16,)))
    dh1 = _mm(dproj, w_in, "nt", "d_h1")
    dx, gp["g_mix"] = _rmsnorm_bwd(x, p["g_mix"] + token[:1, :1], dh1, dx1, "norm_mix_bwd")
    return loss_row, dx, gp


_ANY = pl.BlockSpec(memory_space=pl.ANY)


def _place():
    x, y, c = lax.axis_index("x"), lax.axis_index("y"), lax.axis_index("c")
    chips = [(1 - x, y), (x, 1 - y), (1 - x, 1 - y)]
    return x, y, c, chips


def _chip_index(px, py):
    return 2 * px + py


def _all_gather_chips(split, whole):
    ns, nw = len(split), len(whole)
    n = ns + nw

    def body(*refs):
        ins, outs = refs[:n], refs[n:2 * n]
        send_ici, recv_ici, send_d2d, recv_d2d = refs[2 * n:]
        x, y, c, chips = _place()
        me = _chip_index(x, y)
        sib = (x, y, 1 - c)

        def ici(k, j, src, dst):
            return pltpu.make_async_remote_copy(src_ref=src, dst_ref=dst, send_sem=send_ici.at[3 * k + j],
                                                recv_sem=recv_ici.at[3 * k + j], device_id=(*chips[j], c),
                                                device_id_type=MESH)

        def d2d(k, j, piece):
            return pltpu.make_async_remote_copy(src_ref=piece, dst_ref=piece, send_sem=send_d2d.at[3 * k + j],
                                                recv_sem=recv_d2d.at[3 * k + j], device_id=sib, device_id_type=MESH)

        sends = []
        for k in range(n):
            for j in range(3):
                if k < ns:
                    sends.append(ici(k, j, ins[k].at[c], outs[k].at[me, c]))
                else:
                    sends.append(ici(k, j, ins[k], outs[k].at[me]))
                sends[-1].start()
        passed = []
        for k in range(n):
            for j in range(3):
                src_chip = _chip_index(*chips[j])
                if k < ns:
                    ici(k, j, ins[k].at[c], outs[k].at[src_chip, c]).wait_recv()
                    passed.append(d2d(k, j, outs[k].at[src_chip, c]))
                    passed[-1].start()
                else:
                    ici(k, j, ins[k], outs[k].at[src_chip]).wait_recv()
        for k in range(ns):
            for j in range(3):
                d2d(k, j, outs[k].at[_chip_index(*chips[j]), 1 - c]).wait_recv()
        for cp in sends + passed:
            cp.wait_send()

    arrs = list(split) + list(whole)
    return pl.pallas_call(
        body, in_specs=[_ANY] * n, out_specs=[_ANY] * n,
        out_shape=[jax.ShapeDtypeStruct((N_CHIPS,) + a.shape, a.dtype) for a in arrs],
        scratch_shapes=[pltpu.SemaphoreType.DMA((3 * n,)), pltpu.SemaphoreType.DMA((3 * n,)),
                        pltpu.SemaphoreType.DMA((3 * ns,)), pltpu.SemaphoreType.DMA((3 * ns,))],
        name="all_gather_chips")(*arrs)


def _sibling_swap(arrs, name):
    n = len(arrs)

    def body(*refs):
        ins, outs = refs[:n], refs[n:2 * n]
        send_sem, recv_sem = refs[2 * n:]
        x, y, c, _ = _place()
        copies = [pltpu.make_async_remote_copy(src_ref=ins[k], dst_ref=outs[k], send_sem=send_sem.at[k],
                                               recv_sem=recv_sem.at[k], device_id=(x, y, 1 - c), device_id_type=MESH)
                  for k in range(n)]
        for q in copies:
            q.start()
        for q in copies:
            q.wait()

    return pl.pallas_call(
        body, in_specs=[_ANY] * n, out_specs=[_ANY] * n,
        out_shape=[jax.ShapeDtypeStruct(a.shape, a.dtype) for a in arrs],
        scratch_shapes=[pltpu.SemaphoreType.DMA((n,)), pltpu.SemaphoreType.DMA((n,))],
        name=name)(*arrs)


_HBM = pl.BlockSpec(memory_space=pltpu.HBM)
_SEM = pl.BlockSpec(memory_space=pltpu.SEMAPHORE)
_SPLIT_EFFECT = pltpu.SideEffectType.DATAFLOW_SIDE_EFFECTING


class _Split(NamedTuple):
    send_sems: jax.Array
    recv_sems: jax.Array
    sources: tuple
    lands: tuple
    token: jax.Array


def _split_copies(kind, srcs, lands, send_sems, recv_sems):
    x, y, c, chips = _place()
    me = _chip_index(x, y)
    copies = []
    for k in range(len(srcs)):
        for j in range(3):
            if kind == "gather":
                src, dst = srcs[k], lands[k].at[me]
            else:
                src, dst = srcs[k].at[_chip_index(*chips[j])], lands[k].at[j]
            copies.append(pltpu.make_async_remote_copy(
                src_ref=src, dst_ref=dst, send_sem=send_sems.at[3 * k + j], recv_sem=recv_sems.at[3 * k + j],
                device_id=(*chips[j], c), device_id_type=MESH))
    return copies


def _split_start(name, sources, kind, after):
    n = len(sources)
    if kind == "gather":
        lands = [lax.empty((N_CHIPS,) + s.shape, s.dtype) for s in sources]
    else:
        lands = [lax.empty((3,) + s.shape[1:], s.dtype) for s in sources]
    deps = [] if after is None else [after]

    def body(*refs):
        srcs, lnds = refs[:n], refs[n:2 * n]
        send_sems, recv_sems = refs[2 * n + len(deps)], refs[2 * n + len(deps) + 1]
        for cp in _split_copies(kind, srcs, lnds, send_sems, recv_sems):
            cp.start()
        refs[-1][...] = jnp.zeros_like(refs[-1])

    hbm = lambda a: pltpu.with_memory_space_constraint(a, pltpu.HBM)
    outs = pl.pallas_call(
        body, name=name,
        in_specs=[_HBM] * (2 * n) + [_ANY] * len(deps),
        out_specs=[_SEM, _SEM] + [_HBM] * (2 * n) + [pl.BlockSpec(memory_space=pltpu.VMEM)],
        out_shape=[pltpu.SemaphoreType.DMA((3 * n,)), pltpu.SemaphoreType.DMA((3 * n,))]
        + [pltpu.HBM(a.shape, a.dtype) for a in list(sources) + lands] + [jax.ShapeDtypeStruct((8, LANES), F32)],
        input_output_aliases={k: 2 + k for k in range(2 * n)},
        compiler_params=pltpu.CompilerParams(has_side_effects=_SPLIT_EFFECT),
    )(*[hbm(s) for s in sources], *[hbm(l) for l in lands], *deps)
    return _Split(outs[0], outs[1], tuple(outs[2:2 + n]), tuple(outs[2 + n:2 + 2 * n]), outs[-1])


def _split_wait(name, h, kind, after):
    n = len(h.sources)

    def body(*refs):
        srcs, lnds = refs[:n], refs[n:2 * n]
        for cp in _split_copies(kind, srcs, lnds, refs[2 * n], refs[2 * n + 1]):
            cp.wait_send()
            cp.wait_recv()

    outs = pl.pallas_call(
        body, name=name,
        in_specs=[_HBM] * (2 * n) + [_SEM, _SEM] + [_ANY] * len(after),
        out_specs=[_HBM] * (2 * n),
        out_shape=[pltpu.HBM(a.shape, a.dtype) for a in h.sources + h.lands],
        input_output_aliases={k: k for k in range(2 * n)},
        compiler_params=pltpu.CompilerParams(has_side_effects=_SPLIT_EFFECT),
    )(*h.sources, *h.lands, h.send_sems, h.recv_sems, *after)
    return outs[:n], outs[n:]


def _all_reduce_small(vec, after):
    R, C = vec.shape

    def body(v_ref, after_ref, o_ref, buf, send_sem, recv_sem):
        x, y, c = lax.axis_index("x"), lax.axis_index("y"), lax.axis_index("c")
        me = 4 * x + 2 * y + c
        buf[me] = v_ref[...]
        copies = []
        for r in range(1, N_DEV):
            fx, fy, fc = (r >> 2) & 1, (r >> 1) & 1, r & 1
            peer = (x ^ fx, y ^ fy, c ^ fc)
            copies.append(pltpu.make_async_remote_copy(src_ref=v_ref, dst_ref=buf.at[me], send_sem=send_sem.at[r - 1],
                                                       recv_sem=recv_sem.at[r - 1], device_id=peer, device_id_type=MESH))
        for q in copies:
            q.start()
        for r in range(1, N_DEV):
            fx, fy, fc = (r >> 2) & 1, (r >> 1) & 1, r & 1
            src = 4 * (x ^ fx) + 2 * (y ^ fy) + (c ^ fc)
            pltpu.make_async_remote_copy(src_ref=v_ref, dst_ref=buf.at[src], send_sem=send_sem.at[r - 1],
                                         recv_sem=recv_sem.at[r - 1], device_id=(x, y, c), device_id_type=MESH).wait_recv()
        acc = buf[0]
        for d in range(1, N_DEV):
            acc = acc + buf[d]
        o_ref[...] = acc
        for q in copies:
            q.wait_send()

    vm = pl.BlockSpec(memory_space=pltpu.VMEM)
    return pl.pallas_call(
        body, in_specs=[vm, _ANY], out_specs=vm, out_shape=jax.ShapeDtypeStruct((R, C), F32),
        scratch_shapes=[pltpu.VMEM((N_DEV, R, C), F32), pltpu.SemaphoreType.DMA((N_DEV - 1,)),
                        pltpu.SemaphoreType.DMA((N_DEV - 1,))],
        name="all_reduce_small")(vec, after)


_INPUTS = ["x", "mem", "g_mix", "w_in", "conv_w", "conv_b", "dt_bias", "a_log", "d_skip", "ssm_norm_w", "g_q", "g_k",
           "f_bias", "w_out", "g_xattn", "g_mem", "xq_w", "xkv_w", "xg_q", "xg_k", "xo_w", "g_mlp", "w_up", "w_down"]
_WEIGHTS = _INPUTS[2:]
_BIG = ["w_in", "w_out", "xq_w", "xkv_w", "xo_w", "w_up", "w_down"]
_LATE = _BIG[1:]
_COL_SHARDED = ["w_in", "xkv_w", "w_up"]
_SMALL = [n for n in _WEIGHTS if n not in _BIG]


def _pack_rows(arrs, width):
    starts, r = [], 0
    for a in arrs:
        starts.append(r)
        r += a.shape[0]
    out = jnp.concatenate([jnp.pad(a, ((0, 0), (0, width - a.shape[1]))) for a in arrs], axis=0)
    return jnp.pad(out, ((0, -r % 8), (0, 0))), starts


def _adamw_small(summed, starts, ws, ms, vs, conv_w_index):
    n = len(ws)
    c1 = 1.0 - ADAM_B1 ** ADAM_STEP
    c2 = 1.0 - ADAM_B2 ** ADAM_STEP

    def body(s_ref, *refs):
        w_refs, m_refs, v_refs = refs[:n], refs[n:2 * n], refs[2 * n:3 * n]
        outs = refs[3 * n:]
        chip = _chip_index(lax.axis_index("x"), lax.axis_index("y"))
        for k in range(n):
            rows, cols = w_refs[k].shape
            if k == conv_w_index:
                g = s_ref[starts[k]:starts[k] + rows, pl.ds(pl.multiple_of(chip * cols, LANES), cols)]
            else:
                g = s_ref[starts[k]:starts[k] + rows, 0:cols]
            m_new = ADAM_B1 * m_refs[k][...] + (1.0 - ADAM_B1) * g
            v_new = ADAM_B2 * v_refs[k][...] + (1.0 - ADAM_B2) * (g * g)
            outs[4 * k][...] = g
            outs[4 * k + 1][...] = -ADAM_LR * ((m_new / c1) / (jnp.sqrt(v_new / c2) + ADAM_EPS) + ADAM_WD * w_refs[k][...])
            outs[4 * k + 2][...] = m_new
            outs[4 * k + 3][...] = v_new

    vm = pl.BlockSpec(memory_space=pltpu.VMEM)
    outs = pl.pallas_call(
        body, in_specs=[vm] * (1 + 3 * n), out_specs=[vm] * (4 * n),
        out_shape=[jax.ShapeDtypeStruct(a.shape, F32) for a in ws for _ in range(4)],
        name="adamw_small")(summed, *ws, *ms, *vs)
    return [outs[4 * k:4 * k + 4] for k in range(n)]


def kernel(x, mem, g_mix, w_in, conv_w, conv_b, dt_bias, a_log, d_skip, ssm_norm_w, g_q, g_k, f_bias, w_out, g_xattn, g_mem, xq_w, xkv_w, xg_q, xg_k, xo_w, g_mlp, w_up, w_down, loss_target, m_g_mix, m_w_in, m_conv_w, m_conv_b, m_dt_bias, m_a_log, m_d_skip, m_ssm_norm_w, m_g_q, m_g_k, m_f_bias, m_w_out, m_g_xattn, m_g_mem, m_xq_w, m_xkv_w, m_xg_q, m_xg_k, m_xo_w, m_g_mlp, m_w_up, m_w_down, v_g_mix, v_w_in, v_conv_w, v_conv_b, v_dt_bias, v_a_log, v_d_skip, v_ssm_norm_w, v_g_q, v_g_k, v_f_bias, v_w_out, v_g_xattn, v_g_mem, v_xq_w, v_xkv_w, v_xg_q, v_xg_k, v_xo_w, v_g_mlp, v_w_up, v_w_down):
    args = (x, mem, g_mix, w_in, conv_w, conv_b, dt_bias, a_log, d_skip, ssm_norm_w, g_q, g_k, f_bias, w_out, g_xattn,
            g_mem, xq_w, xkv_w, xg_q, xg_k, xo_w, g_mlp, w_up, w_down)
    w = dict(zip(_INPUTS, args))
    mom1 = dict(zip(_WEIGHTS, (m_g_mix, m_w_in, m_conv_w, m_conv_b, m_dt_bias, m_a_log, m_d_skip, m_ssm_norm_w, m_g_q,
                               m_g_k, m_f_bias, m_w_out, m_g_xattn, m_g_mem, m_xq_w, m_xkv_w, m_xg_q, m_xg_k, m_xo_w,
                               m_g_mlp, m_w_up, m_w_down)))
    mom2 = dict(zip(_WEIGHTS, (v_g_mix, v_w_in, v_conv_w, v_conv_b, v_dt_bias, v_a_log, v_d_skip, v_ssm_norm_w, v_g_q,
                               v_g_k, v_f_bias, v_w_out, v_g_xattn, v_g_mem, v_xq_w, v_xkv_w, v_xg_q, v_xg_k, v_xo_w,
                               v_g_mlp, v_w_up, v_w_down)))
    chip = _chip_index(lax.axis_index("x"), lax.axis_index("y"))

    shard_bf = {n: w[n][0].astype(BF16) for n in _BIG}

    def layout_for_compute(n, g):
        if n == "w_in":
            return _w_in_from_shards(g)
        return g if n in _COL_SHARDED else g.reshape(N_CHIPS * g.shape[1], g.shape[2])

    def layout_for_reduction(n, g):
        if n == "w_in":
            return _w_in_to_shards(g)
        return g if n in _COL_SHARDED else g.reshape(N_CHIPS, g.shape[0] // N_CHIPS, g.shape[1])

    halves_in = shard_bf["w_in"].reshape(2, shard_bf["w_in"].shape[0] // 2, -1)
    g_in, g_conv = _all_gather_chips([halves_in], [w["conv_w"][0]])
    g_in = lax.dynamic_update_index_in_dim(g_in, halves_in, chip, axis=0)
    g_conv = lax.dynamic_update_index_in_dim(g_conv, w["conv_w"][0], chip, axis=0)
    w_in_full = layout_for_compute("w_in", g_in.reshape(N_CHIPS, -1, g_in.shape[-1]))
    p = {n: w[n] for n in _SMALL}
    p["conv_w"] = g_conv.transpose(1, 0, 2).reshape(CONV_WIDTH, CONV_DIM)
    gather = _split_start("gather_late", [shard_bf[n] for n in _LATE], "gather", after=g_in)
    p["g_mix"] = p["g_mix"] + gather.token[:1, :1]

    def late_weights(after):
        srcs, lands = _split_wait("gather_late_wait", gather, "gather", after)
        lands = [lax.dynamic_update_index_in_dim(l, s, chip, axis=0) for l, s in zip(lands, srcs)]
        return {n: layout_for_compute(n, l) for n, l in zip(_LATE, lands)}

    scatter = {}

    def send_late_grads(grads):
        scatter["late"] = _split_start("scatter_late", [layout_for_reduction(n, grads[n]) for n in _LATE], "scatter",
                                       after=None)
        return scatter["late"].token

    def send_w_in_grad(g):
        scatter["w_in"] = _split_start("scatter_w_in", [layout_for_reduction("w_in", g)], "scatter", after=None)
        return scatter["w_in"].token

    loss_row, dx, gp = _layer_fwd_bwd(x[0], mem[0], loss_target[0], w_in_full, p, late_weights, send_late_grads,
                                      send_w_in_grad)

    grad, delta, new_m, new_v = {}, {}, {}, {}

    def finish(names, sources, from_chips, tag):
        mine = [_chip_sum(lax.dynamic_index_in_dim(s, chip, axis=0, keepdims=False), fc, "rs_chip_sum_" + n)
                for n, s, fc in zip(names, sources, from_chips)]
        for n, a, b in zip(names, mine, _sibling_swap(mine, "rs_sibling_swap_" + tag)):
            shape = w[n].shape
            res = _adamw(w[n][0], a, b, mom1[n][0], mom2[n][0], "adamw_" + n)
            grad[n], delta[n], new_m[n], new_v[n] = (r.reshape(shape) for r in res)

    finish(_LATE, *_split_wait("scatter_late_wait", scatter["late"], "scatter", (dx,)), "late")

    sources_in, from_chips_in = _split_wait("scatter_w_in_wait", scatter["w_in"], "scatter",
                                            tuple(new_v[n] for n in _LATE))

    packed, starts = _pack_rows([gp[n] for n in _SMALL] + [loss_row], CONV_DIM)
    summed = _all_reduce_small(packed, from_chips_in[0])
    loss = summed[starts[-1], 0]
    finish(["w_in"], sources_in, from_chips_in, "w_in")

    as_rows = lambda a: a.reshape(-1, a.shape[-1])
    results = _adamw_small(summed, starts, [as_rows(w[n]) for n in _SMALL], [as_rows(mom1[n]) for n in _SMALL],
                           [as_rows(mom2[n]) for n in _SMALL], _SMALL.index("conv_w"))
    for n, res in zip(_SMALL, results):
        grad[n], delta[n], new_m[n], new_v[n] = (a.reshape(w[n].shape) for a in res)

    return (loss, dx[None], *[grad[n] for n in _WEIGHTS], *[delta[n] for n in _WEIGHTS],
            *[new_m[n] for n in _WEIGHTS], *[new_v[n] for n in _WEIGHTS])
```
